```python
import math
import jax, jax.numpy as jnp
from jax import lax
import numpy as np

D_MODEL = 1024
BATCH = 8
SEQ = 2048
DEPTH = 1

CHUNK = 64
A_HEADS = 8
A_HEAD_DIM = 64
A_LEFT_CHUNKS = 8
A_MAX_REL = 128
A_WIDTH = A_HEADS * A_HEAD_DIM
B_HEADS = 8
B_NOPE_DIM = 64
B_ROPE_DIM = 32
B_V_DIM = 64
B_Q_LORA = 384
B_KV_LORA = 256
B_QK_DIM = B_NOPE_DIM + B_ROPE_DIM
B_WIDTH = B_HEADS * B_V_DIM
ROPE_THETA = 10000.0
Q_BLOCK = 128
IN_SIZES = (A_WIDTH, A_WIDTH, A_WIDTH, B_Q_LORA, B_KV_LORA, B_ROPE_DIM, D_MODEL, D_MODEL)
IN_SPLITS = tuple(int(v) for v in np.cumsum(IN_SIZES)[:-1])
N_IN = int(sum(IN_SIZES))
N_EXPERTS = 32
TOP_K = 4
D_EXPERT = D_MODEL
SWIGLU_LIMIT = 7.0
SWIGLU_ALPHA = 1.702
EXPERT_BLOCK = 256
DEEPNORM_ALPHA = (2.0 * DEPTH) ** 0.25
DEEPNORM_BETA = (8.0 * DEPTH) ** -0.25
LN_EPS = 1e-5
RMS_EPS = 1e-6
NEG_INF = -1e30

kernel_name = "hybrid_chunked_mla_moe_deepnorm_adaln"


def layer_norm(x, g, b):
    xf = x.astype(jnp.float32)
    mu = jnp.mean(xf, axis=-1, keepdims=True)
    var = jnp.mean(jnp.square(xf - mu), axis=-1, keepdims=True)
    return ((xf - mu) * lax.rsqrt(var + LN_EPS) * g.astype(jnp.float32) + b.astype(jnp.float32)).astype(x.dtype)


def rms_norm(x, g):
    xf = x.astype(jnp.float32)
    ms = jnp.mean(jnp.square(xf), axis=-1, keepdims=True)
    return (xf * lax.rsqrt(ms + RMS_EPS) * g.astype(jnp.float32)).astype(x.dtype)


def rope(x, positions):
    half = x.shape[-1] // 2
    freqs = ROPE_THETA ** (-jnp.arange(half, dtype=jnp.float32) / half)
    ang = positions.astype(jnp.float32)[:, :, None, None] * freqs
    cos, sin = jnp.cos(ang), jnp.sin(ang)
    xf = x.astype(jnp.float32)
    x1, x2 = xf[..., :half], xf[..., half:]
    return jnp.concatenate([x1 * cos - x2 * sin, x2 * cos + x1 * sin], axis=-1).astype(x.dtype)


def chunked_relpos_attention(q, k, v, rel_bias):
    bsz, seq = q.shape[0], q.shape[1]
    n_chunks = seq // CHUNK
    left = A_LEFT_CHUNKS * CHUNK
    band = left + CHUNK
    qc = q.reshape(bsz, n_chunks, CHUNK, A_HEADS, A_HEAD_DIM)
    pad = ((0, 0), (left, 0), (0, 0), (0, 0))
    band_idx = (jnp.arange(n_chunks) * CHUNK)[:, None] + jnp.arange(band)[None, :]
    kb = jnp.pad(k, pad)[:, band_idx]
    vb = jnp.pad(v, pad)[:, band_idx]
    valid = band_idx >= left
    dist = jnp.arange(CHUNK)[:, None] - jnp.arange(band)[None, :] + left
    bias = rel_bias[:, jnp.clip(dist, -A_MAX_REL, A_MAX_REL) + A_MAX_REL].astype(jnp.float32)
    s = jnp.einsum('bcqhd,bckhd->bhcqk', qc, kb, preferred_element_type=jnp.float32)
    s = s * (A_HEAD_DIM ** -0.5) + bias[:, None]
    s = jnp.where(valid[None, None, :, None, :], s, NEG_INF)
    p = jax.nn.softmax(s, axis=-1).astype(v.dtype)
    o = jnp.einsum('bhcqk,bckhd->bcqhd', p, vb)
    return o.reshape(bsz, seq, A_WIDTH)


def mla_attention(c_q, c_kv, k_rope, positions, rms_q, w_uq, rms_kv, w_ukv):
    bsz, seq = c_q.shape[0], c_q.shape[1]
    q = (rms_norm(c_q, rms_q) @ w_uq).reshape(bsz, seq, B_HEADS, B_QK_DIM)
    kv = (rms_norm(c_kv, rms_kv) @ w_ukv).reshape(bsz, seq, B_HEADS, B_NOPE_DIM + B_V_DIM)
    q_nope, q_pe = q[..., :B_NOPE_DIM], q[..., B_NOPE_DIM:]
    k_nope, v = kv[..., :B_NOPE_DIM], kv[..., B_NOPE_DIM:]
    q_pe = rope(q_pe, positions)
    k_pe = rope(k_rope[:, :, None, :], positions)
    q = jnp.concatenate([q_nope, q_pe], axis=-1)
    k = jnp.concatenate([k_nope, jnp.broadcast_to(k_pe, (bsz, seq, B_HEADS, B_ROPE_DIM))], axis=-1)
    n_qb = seq // Q_BLOCK
    qb = q.reshape(bsz, n_qb, Q_BLOCK, B_HEADS, B_QK_DIM).transpose(1, 0, 2, 3, 4)
    key_chunk = jnp.arange(seq) // CHUNK
    scale = B_QK_DIM ** -0.5

    def one_block(args):
        q_blk, start = args
        q_chunk = (start + jnp.arange(Q_BLOCK)) // CHUNK
        mask = key_chunk[None, :] <= q_chunk[:, None]
        s = jnp.einsum('bqhd,bkhd->bhqk', q_blk, k, preferred_element_type=jnp.float32) * scale
        s = jnp.where(mask[None, None], s, NEG_INF)
        p = jax.nn.softmax(s, axis=-1).astype(v.dtype)
        return jnp.einsum('bhqk,bkhd->bqhd', p, v)

    o = lax.map(one_block, (qb, jnp.arange(n_qb) * Q_BLOCK))
    return o.transpose(1, 0, 2, 3, 4).reshape(bsz, seq, B_WIDTH)


def moe_ffn(u, w_router, b_router, w_gate, b_gate, w_up, b_up, w_down, b_down):
    bsz, seq, d = u.shape
    n_tok = bsz * seq
    xt = u.reshape(n_tok, d)
    logits = jnp.dot(xt, w_router, preferred_element_type=jnp.float32) + b_router.astype(jnp.float32)
    top_val, top_idx = lax.top_k(logits, TOP_K)
    top_w = jax.nn.softmax(top_val, axis=-1)
    n_slots = n_tok * TOP_K
    slot_expert = top_idx.reshape(-1).astype(jnp.int32)
    slot_token = jnp.arange(n_slots, dtype=jnp.int32) // TOP_K
    slot_w = top_w.reshape(-1)
    order = jnp.argsort(slot_expert)
    sorted_expert = slot_expert[order]
    counts = jnp.bincount(slot_expert, length=N_EXPERTS)
    padded = ((counts + EXPERT_BLOCK - 1) // EXPERT_BLOCK) * EXPERT_BLOCK
    start = jnp.cumsum(counts) - counts
    pstart = jnp.cumsum(padded) - padded
    dest = pstart[sorted_expert] + jnp.arange(n_slots, dtype=jnp.int32) - start[sorted_expert]
    n_blocks = -(-n_slots // EXPERT_BLOCK) + N_EXPERTS
    cap = n_blocks * EXPERT_BLOCK
    buf_token = jnp.zeros((cap,), jnp.int32).at[dest].set(slot_token[order])
    buf_w = jnp.zeros((cap,), jnp.float32).at[dest].set(slot_w[order])
    block_start = jnp.arange(n_blocks, dtype=jnp.int32) * EXPERT_BLOCK
    block_expert = jnp.clip(jnp.searchsorted(jnp.cumsum(padded), block_start, side='right'), 0, N_EXPERTS - 1)

    def expert_block(args):
        tok, e = args
        xb = xt[tok]
        g = xb @ w_gate[e] + b_gate[e]
        up = xb @ w_up[e] + b_up[e]
        g = jnp.minimum(g, SWIGLU_LIMIT)
        up = jnp.clip(up, -SWIGLU_LIMIT, SWIGLU_LIMIT)
        h = (up + 1.0) * (g * jax.nn.sigmoid(SWIGLU_ALPHA * g))
        return h @ w_down[e] + b_down[e]

    y = lax.map(expert_block, (buf_token.reshape(n_blocks, EXPERT_BLOCK), block_expert))
    y = y.reshape(cap, d) * buf_w[:, None].astype(y.dtype)
    out = jnp.zeros((n_tok, d), u.dtype).at[buf_token].add(y)
    return out.reshape(bsz, seq, d)


def setup_inputs(seed: int = 0) -> dict:
    key = jax.random.key(seed)
    ks = jax.random.split(key, 26)

    def nrm(k, shape, scale):
        return jax.random.normal(k, shape, jnp.float32) * scale

    L, D, E, F = DEPTH, D_MODEL, N_EXPERTS, D_EXPERT
    positions = (jax.random.randint(ks[2], (BATCH, 1), 0, 4096, dtype=jnp.int32)
                 + jnp.arange(SEQ, dtype=jnp.int32)[None, :])
    return {
        "x": nrm(ks[0], (BATCH, SEQ, D), 1.0),
        "c": nrm(ks[1], (BATCH, D), 1.0),
        "positions": positions,
        "w_ada": nrm(ks[3], (L, D, 6 * D), 0.5 * D ** -0.5),
        "b_ada": nrm(ks[4], (L, 6 * D), 0.01),
        "w_in": nrm(ks[5], (L, D, N_IN), D ** -0.5),
        "rms_q": 1.0 + nrm(ks[6], (L, B_Q_LORA), 0.01),
        "w_uq": nrm(ks[7], (L, B_Q_LORA, B_HEADS * B_QK_DIM), B_Q_LORA ** -0.5),
        "rms_kv": 1.0 + nrm(ks[8], (L, B_KV_LORA), 0.01),
        "w_ukv": nrm(ks[9], (L, B_KV_LORA, B_HEADS * (B_NOPE_DIM + B_V_DIM)), B_KV_LORA ** -0.5),
        "rel_bias": nrm(ks[10], (L, A_HEADS, 2 * A_MAX_REL + 1), 0.5),
        "w_branch_a": nrm(ks[11], (L, A_WIDTH, D), DEEPNORM_BETA * A_WIDTH ** -0.5),
        "w_branch_b": nrm(ks[12], (L, B_WIDTH, D), DEEPNORM_BETA * B_WIDTH ** -0.5),
        "w_out": nrm(ks[13], (L, D, D), DEEPNORM_BETA * D ** -0.5),
        "ln1_g": 1.0 + nrm(ks[14], (L, D), 0.01),
        "ln1_b": nrm(ks[15], (L, D), 0.01),
        "w_router": nrm(ks[16], (L, D, E), D ** -0.5),
        "b_router": nrm(ks[17], (L, E), 0.01),
        "w_gate": nrm(ks[18], (L, E, D, F), D ** -0.5),
        "b_gate": nrm(ks[19], (L, E, F), 0.01),
        "w_up": nrm(ks[20], (L, E, D, F), D ** -0.5),
        "b_up": nrm(ks[21], (L, E, F), 0.01),
        "w_down": nrm(ks[22], (L, E, F, D), DEEPNORM_BETA * F ** -0.5),
        "b_down": nrm(ks[23], (L, E, D), 0.01),
        "ln2_g": 1.0 + nrm(ks[24], (L, D), 0.01),
        "ln2_b": nrm(ks[25], (L, D), 0.01),
    }


def reference(x, c, positions, w_ada, b_ada, w_in, rms_q, w_uq, rms_kv, w_ukv, rel_bias,
              w_branch_a, w_branch_b, w_out, ln1_g, ln1_b, w_router, b_router,
              w_gate, b_gate, w_up, b_up, w_down, b_down, ln2_g, ln2_b):
    bsz, seq = x.shape[0], x.shape[1]
    for l in range(DEPTH):
        mod = jnp.einsum('bd,dm->bm', jax.nn.silu(c), w_ada[l]) + b_ada[l]
        shift1, scale1, gate1, shift2, scale2, gate2 = jnp.split(mod[:, None, :], 6, axis=-1)

        h = x * (1.0 + scale1) + shift1
        z = h @ w_in[l]
        q_a, k_a, v_a, c_q, c_kv, k_rope, g_a, g_b = jnp.split(z, IN_SPLITS, axis=-1)
        y_a = chunked_relpos_attention(
            q_a.reshape(bsz, seq, A_HEADS, A_HEAD_DIM),
            k_a.reshape(bsz, seq, A_HEADS, A_HEAD_DIM),
            v_a.reshape(bsz, seq, A_HEADS, A_HEAD_DIM),
            rel_bias[l])
        y_b = mla_attention(c_q, c_kv, k_rope, positions, rms_q[l], w_uq[l], rms_kv[l], w_ukv[l])
        merged = (jax.nn.sigmoid(g_a) * (y_a @ w_branch_a[l])
                  + jax.nn.sigmoid(g_b) * (y_b @ w_branch_b[l]))
        x = layer_norm(DEEPNORM_ALPHA * x + gate1 * (merged @ w_out[l]), ln1_g[l], ln1_b[l])

        h = x * (1.0 + scale2) + shift2
        f = moe_ffn(h, w_router[l], b_router[l], w_gate[l], b_gate[l], w_up[l], b_up[l], w_down[l], b_down[l])
        x = layer_norm(DEEPNORM_ALPHA * x + gate2 * f, ln2_g[l], ln2_b[l])
    return x
```

```python
import functools

import jax
import jax.numpy as jnp
from jax import lax
from jax.experimental import pallas as pl
from jax.experimental.pallas import tpu as pltpu

F32 = jnp.float32
BF16 = jnp.bfloat16

CHUNK = 64
A_HEADS = 8
A_HEAD_DIM = 64
A_LEFT_CHUNKS = 8
A_MAX_REL = 128
A_WIDTH = A_HEADS * A_HEAD_DIM
B_HEADS = 8
B_NOPE_DIM = 64
B_ROPE_DIM = 32
B_V_DIM = 64
B_Q_LORA = 384
B_KV_LORA = 256
B_QK_DIM = B_NOPE_DIM + B_ROPE_DIM
ROPE_THETA = 10000.0
N_EXPERTS = 32
TOP_K = 4
SWIGLU_LIMIT = 7.0
SWIGLU_ALPHA = 1.702
DEPTH = 1
DEEPNORM_ALPHA = (2.0 * DEPTH) ** 0.25
LN_EPS = 1e-5
RMS_EPS = 1e-6
NEG_INF = -1e30

LANES = 128
VMEM_BYTES = 64 << 20

TM_PROJ = 512
TQ_A = 256
TQ_B = 256
TM_MERGE = 256
TR_ROUTE = 256
TM_MOE = 256
TM_ROWS = 256
MAX_TILES_PAD = 512


def _params(sem, vmem_mb=None):
    return pltpu.CompilerParams(
        dimension_semantics=sem,
        vmem_limit_bytes=None if vmem_mb is None else vmem_mb << 20)


def _dot(a, b):
    return jnp.dot(a, b, preferred_element_type=F32)


def _dot_nt(a, b):
    return lax.dot_general(a, b, (((1,), (1,)), ((), ())), preferred_element_type=F32)


def _layer_norm(x, g, b):
    mu = jnp.mean(x, axis=-1, keepdims=True)
    xc = x - mu
    var = jnp.mean(xc * xc, axis=-1, keepdims=True)
    return xc * lax.rsqrt(var + LN_EPS) * g + b


def _ada_kernel(c_ref, w_ref, b_ref, o_ref):
    c = c_ref[...]
    sc = c * jax.nn.sigmoid(c)
    o_ref[...] = jnp.dot(sc, w_ref[...], preferred_element_type=F32,
                         precision=lax.Precision.HIGHEST) + b_ref[...]


def _ada(c, w_ada, b_ada):
    bsz, d = c.shape
    m = w_ada.shape[1]
    tn = 1024
    return pl.pallas_call(
        _ada_kernel,
        out_shape=jax.ShapeDtypeStruct((bsz, m), F32),
        grid=(m // tn,),
        in_specs=[pl.BlockSpec((bsz, d), lambda j: (0, 0)),
                  pl.BlockSpec((d, tn), lambda j: (0, j)),
                  pl.BlockSpec((1, tn), lambda j: (0, j))],
        out_specs=pl.BlockSpec((bsz, tn), lambda j: (0, j)),
        compiler_params=_params(("arbitrary",)),
    )(c, w_ada, b_ada.reshape(1, m))


def _inproj_kernel(x_ref, mod_ref, w_ref, qkv_ref, cq_ref, ckv_ref, kr_ref, ga_ref, gb_ref):
    m = mod_ref[0]
    h = (x_ref[0] * (1.0 + m[1:2]) + m[0:1]).astype(BF16)
    col = 0
    for ref in (qkv_ref, cq_ref, ckv_ref, kr_ref, ga_ref, gb_ref):
        n = ref.shape[-1]
        ref[0] = _dot(h, w_ref[:, col:col + n]).astype(ref.dtype)
        col += n


def _inproj(x, mod3, w):
    bsz, seq, d = x.shape
    tm = min(TM_PROJ, seq)
    widths = (3 * A_WIDTH, B_Q_LORA, B_KV_LORA, LANES, d, d)
    dtypes = (BF16, BF16, BF16, F32, BF16, BF16)
    assert sum(widths) == w.shape[1]
    return pl.pallas_call(
        _inproj_kernel,
        out_shape=[jax.ShapeDtypeStruct((bsz, seq, n), dt) for n, dt in zip(widths, dtypes)],
        grid=(bsz, seq // tm),
        in_specs=[pl.BlockSpec((1, tm, d), lambda b, i: (b, i, 0)),
                  pl.BlockSpec((1, 6, d), lambda b, i: (b, 0, 0)),
                  pl.BlockSpec(w.shape, lambda b, i: (0, 0))],
        out_specs=[pl.BlockSpec((1, tm, n), lambda b, i: (b, i, 0)) for n in widths],
        compiler_params=_params(("arbitrary", "arbitrary"), 48),
    )(x, mod3, w)


def _attn_a_kernel(q_ref, k_ref, v_ref, bias_ref, o_ref, *, tq, nd):
    i = pl.program_id(2)
    q = q_ref[0]
    lane = lax.broadcasted_iota(jnp.int32, (1, LANES), 1)
    outs = []
    for hh in range(2):
        in_head = (lane // A_HEAD_DIM) == hh
        scores, values = [], []
        for d in range(nd):
            start = pl.multiple_of(jnp.maximum(i - d, 0) * tq, tq)
            k = k_ref[0, pl.ds(start, tq), :]
            kh = jnp.where(in_head, k, jnp.zeros_like(k))
            s = _dot_nt(q, kh) + bias_ref[hh, d]
            if d > 0:
                s = jnp.where(i >= d, s, NEG_INF)
            scores.append(s)
            values.append(v_ref[0, pl.ds(start, tq), :])
        m = scores[0].max(axis=-1, keepdims=True)
        for s in scores[1:]:
            m = jnp.maximum(m, s.max(axis=-1, keepdims=True))
        l = jnp.zeros_like(m)
        acc = jnp.zeros((tq, LANES), F32)
        for s, v in zip(scores, values):
            p = jnp.exp(s - m)
            l = l + p.sum(axis=-1, keepdims=True)
            acc = acc + _dot(p.astype(BF16), v)
        outs.append(acc / l)
    o_ref[0] = jnp.where(lane < A_HEAD_DIM, outs[0], outs[1]).astype(o_ref.dtype)


def _attn_a(qkv, bias_tab):
    bsz, seq, _ = qkv.shape
    tq = min(TQ_A, seq)
    nd = bias_tab.shape[1]
    ncb = A_WIDTH // LANES
    return pl.pallas_call(
        functools.partial(_attn_a_kernel, tq=tq, nd=nd),
        out_shape=jax.ShapeDtypeStruct((bsz, seq, A_WIDTH), BF16),
        grid=(ncb, bsz, seq // tq),
        in_specs=[pl.BlockSpec((1, tq, LANES), lambda hp, b, i: (b, i, hp)),
                  pl.BlockSpec((1, seq, LANES), lambda hp, b, i: (b, 0, ncb + hp)),
                  pl.BlockSpec((1, seq, LANES), lambda hp, b, i: (b, 0, 2 * ncb + hp)),
                  pl.BlockSpec((2, nd, tq, tq), lambda hp, b, i: (hp, 0, 0, 0))],
        out_specs=pl.BlockSpec((1, tq, LANES), lambda hp, b, i: (b, i, hp)),
        compiler_params=_params(("arbitrary",) * 3, 32),
    )(qkv, qkv, qkv, bias_tab)


def _bias_table(rel_bias, tq):
    nd = (A_LEFT_CHUNKS * CHUNK + tq - 1) // tq + 1
    d = jnp.arange(nd)[:, None, None]
    qi = jnp.arange(tq)[None, :, None]
    kj = jnp.arange(tq)[None, None, :]
    dist = d * tq + qi - kj
    dchunk = d * (tq // CHUNK) + qi // CHUNK - kj // CHUNK
    valid = (dchunk >= 0) & (dchunk <= A_LEFT_CHUNKS)
    b = rel_bias[:, jnp.clip(dist, -A_MAX_REL, A_MAX_REL) + A_MAX_REL].astype(F32)
    return jnp.where(valid[None], b, NEG_INF)


def _mla_proj_kernel(cq_ref, ckv_ref, kr_ref, cos_ref, sa_ref, sb_ref, rq_ref, rkv_ref,
                     wq_ref, wk_ref, wv_ref, q_out, k_out, v_out):
    def rms(x, g):
        xf = x.astype(F32)
        ms = jnp.mean(xf * xf, axis=-1, keepdims=True)
        return (xf * lax.rsqrt(ms + RMS_EPS) * g).astype(BF16)

    cqn = rms(cq_ref[0], rq_ref[...])
    ckvn = rms(ckv_ref[0], rkv_ref[...])
    q = _dot(cqn, wq_ref[...])
    kn = _dot(ckvn, wk_ref[...])
    v_out[0] = _dot(ckvn, wv_ref[...]).astype(v_out.dtype)
    cos_t, sin_a, sin_b = cos_ref[0], sa_ref[0], sb_ref[0]

    def rope(x):
        return (x * cos_t + pltpu.roll(x, LANES - B_ROPE_DIM // 2, 1) * sin_a
                + pltpu.roll(x, B_ROPE_DIM // 2, 1) * sin_b)

    kpe = rope(kr_ref[0])
    scale = B_QK_DIM ** -0.5
    for h in range(B_HEADS):
        sl = slice(h * LANES, (h + 1) * LANES)
        q_out[0, :, sl] = (rope(q[:, sl]) * scale).astype(q_out.dtype)
        k_out[0, :, sl] = (kn[:, sl] + kpe).astype(k_out.dtype)


def _mla_proj(cq, ckv, kr, cos_t, sin_a, sin_b, rms_q, rms_kv, wq, wk, wv):
    bsz, seq, _ = cq.shape
    tm = min(TM_PROJ, seq)
    tok = lambda n: pl.BlockSpec((1, tm, n), lambda b, i: (b, i, 0))
    full = lambda a: pl.BlockSpec(a.shape, lambda b, i: (0,) * a.ndim)
    return pl.pallas_call(
        _mla_proj_kernel,
        out_shape=[jax.ShapeDtypeStruct((bsz, seq, B_HEADS * LANES), BF16),
                   jax.ShapeDtypeStruct((bsz, seq, B_HEADS * LANES), BF16),
                   jax.ShapeDtypeStruct((bsz, seq, B_HEADS * B_V_DIM), BF16)],
        grid=(bsz, seq // tm),
        in_specs=[tok(B_Q_LORA), tok(B_KV_LORA), tok(LANES), tok(LANES), tok(LANES), tok(LANES),
                  full(rms_q), full(rms_kv), full(wq), full(wk), full(wv)],
        out_specs=[tok(B_HEADS * LANES), tok(B_HEADS * LANES), tok(B_HEADS * B_V_DIM)],
        compiler_params=_params(("arbitrary", "arbitrary"), 32),
    )(cq, ckv, kr, cos_t, sin_a, sin_b, rms_q, rms_kv, wq, wk, wv)


def _mla_attn_kernel(q_ref, k_ref, v_ref, o_ref, *, t):
    i = pl.program_id(2)
    lane = lax.broadcasted_iota(jnp.int32, (1, LANES), 1)
    row_chunk = lax.broadcasted_iota(jnp.int32, (t, t), 0) // CHUNK
    col_chunk = lax.broadcasted_iota(jnp.int32, (t, t), 1) // CHUNK
    diag_ok = col_chunk <= row_chunk
    outs = []
    for hh in range(2):
        q = q_ref[0, :, hh * LANES:(hh + 1) * LANES]

        def step(kj, carry, masked, hh=hh, q=q):
            m, l, acc = carry
            start = pl.multiple_of(kj * t, t)
            k = k_ref[0, pl.ds(start, t), hh * LANES:(hh + 1) * LANES]
            v = v_ref[0, pl.ds(start, t), :]
            s = _dot_nt(q, k)
            if masked:
                s = jnp.where(diag_ok, s, NEG_INF)
            m_new = jnp.maximum(m, s.max(axis=-1, keepdims=True))
            alpha = jnp.exp(m - m_new)
            p = jnp.exp(s - m_new)
            l = alpha * l + p.sum(axis=-1, keepdims=True)
            acc = alpha * acc + _dot(p.astype(BF16), v)
            return m_new, l, acc

        init = (jnp.full((t, 1), NEG_INF, F32), jnp.zeros((t, 1), F32), jnp.zeros((t, LANES), F32))
        carry = lax.fori_loop(0, i, functools.partial(step, masked=False), init)
        _, l, acc = step(i, carry, True)
        outs.append(acc / l)
    o_ref[0] = jnp.where(lane < B_V_DIM, outs[0], outs[1]).astype(o_ref.dtype)


def _mla_attn(qm, km, vm):
    bsz, seq, _ = qm.shape
    t = min(TQ_B, seq)
    return pl.pallas_call(
        functools.partial(_mla_attn_kernel, t=t),
        out_shape=jax.ShapeDtypeStruct((bsz, seq, B_HEADS * B_V_DIM), BF16),
        grid=(bsz, B_HEADS // 2, seq // t),
        in_specs=[pl.BlockSpec((1, t, 2 * LANES), lambda b, hp, i: (b, i, hp)),
                  pl.BlockSpec((1, seq, 2 * LANES), lambda b, hp, i: (b, 0, hp)),
                  pl.BlockSpec((1, seq, LANES), lambda b, hp, i: (b, 0, hp))],
        out_specs=pl.BlockSpec((1, t, LANES), lambda b, hp, i: (b, i, hp)),
        compiler_params=_params(("arbitrary",) * 3, 32),
    )(qm, km, vm)


def _merge_kernel(x_ref, ya_ref, yb_ref, ga_ref, gb_ref, mod_ref, wa_ref, wb_ref, wo_ref,
                  g1_ref, b1_ref, wr_ref, br_ref, x1_ref, h2_ref, lg_ref):
    m = mod_ref[0]
    a = _dot(ya_ref[0], wa_ref[...])
    b = _dot(yb_ref[0], wb_ref[...])
    merged = (jax.nn.sigmoid(ga_ref[0].astype(F32)) * a
              + jax.nn.sigmoid(gb_ref[0].astype(F32)) * b)
    o = _dot(merged.astype(BF16), wo_ref[...])
    x1 = _layer_norm(DEEPNORM_ALPHA * x_ref[0] + m[2:3] * o, g1_ref[...], b1_ref[...])
    h2 = x1 * (1.0 + m[4:5]) + m[3:4]
    x1_ref[0] = x1
    h2_ref[0] = h2
    lg_ref[0] = jnp.dot(h2, wr_ref[...], preferred_element_type=F32,
                        precision=lax.Precision.HIGHEST) + br_ref[...]


def _merge(x, ya, yb, ga, gb, mod3, wa, wb, wo, g1, b1, wr, br):
    bsz, seq, d = x.shape
    tm = min(TM_MERGE, seq)
    tok = lambda n: pl.BlockSpec((1, tm, n), lambda b, i: (b, i, 0))
    full = lambda a: pl.BlockSpec(a.shape, lambda b, i: (0,) * a.ndim)
    return pl.pallas_call(
        _merge_kernel,
        out_shape=[jax.ShapeDtypeStruct((bsz, seq, d), F32),
                   jax.ShapeDtypeStruct((bsz, seq, d), F32),
                   jax.ShapeDtypeStruct((bsz, seq, LANES), F32)],
        grid=(bsz, seq // tm),
        in_specs=[tok(d), tok(A_WIDTH), tok(B_HEADS * B_V_DIM), tok(d), tok(d),
                  pl.BlockSpec((1, 6, d), lambda b, i: (b, 0, 0)),
                  full(wa), full(wb), full(wo), full(g1), full(b1), full(wr), full(br)],
        out_specs=[tok(d), tok(d), tok(LANES)],
        compiler_params=_params(("arbitrary", "arbitrary"), 40),
    )(x, ya, yb, ga, gb, mod3, wa, wb, wo, g1, b1, wr, br)


def _route_kernel(lg_ref, pos_ref, w_ref, tmap_ref, nv_ref, cnt_s, carry_s, pstart_s, *, tr, tm):
    phase = pl.program_id(0)
    i = pl.program_id(1)
    lane = lax.broadcasted_iota(jnp.int32, (tr, LANES), 1)
    work = lg_ref[...]
    sels, vals = [], []
    for _ in range(TOP_K):
        m = work.max(axis=-1, keepdims=True)
        idx = jnp.where(work == m, lane, LANES).min(axis=-1, keepdims=True)
        sel = lane == idx
        work = jnp.where(sel, -jnp.inf, work)
        sels.append(sel)
        vals.append(m)
    picked = sum(jnp.where(s, 1.0, 0.0) for s in sels)
    tile_count = jnp.broadcast_to(picked.sum(axis=0, keepdims=True), (8, LANES))

    @pl.when((phase == 0) & (i == 0))
    def _():
        cnt_s[...] = jnp.zeros_like(cnt_s)

    @pl.when(phase == 0)
    def _():
        cnt_s[...] += tile_count

    @pl.when((phase == 1) & (i == 0))
    def _():
        lane8 = lax.broadcasted_iota(jnp.int32, (8, LANES), 1)
        padded = jnp.floor((cnt_s[...] + (tm - 1)) * (1.0 / tm)) * tm
        cum = padded
        shift = 1
        while shift < N_EXPERTS:
            cum = cum + jnp.where(lane8 >= shift, pltpu.roll(cum, shift, 1), 0.0)
            shift *= 2
        pstart_s[...] = cum - padded
        carry_s[...] = jnp.zeros_like(carry_s)
        nt = tmap_ref.shape[0]
        tile_start = (lax.broadcasted_iota(jnp.int32, (nt, LANES), 0) * tm).astype(F32)
        lane_t = lax.broadcasted_iota(jnp.int32, (nt, LANES), 1)
        done = jnp.where((lane_t < N_EXPERTS) & (cum[0:1] <= tile_start), 1.0, 0.0)
        te = jnp.minimum(done.sum(axis=-1, keepdims=True), N_EXPERTS - 1.0)
        tmap_ref[...] = jnp.broadcast_to(te, (nt, LANES)).astype(jnp.int32)
        total = jnp.where(lane8 == N_EXPERTS - 1, cum, 0.0).sum(axis=-1, keepdims=True)
        nv_ref[...] = jnp.broadcast_to(total * (1.0 / tm), (8, LANES)).astype(jnp.int32)

    @pl.when(phase == 1)
    def _():
        r = lax.broadcasted_iota(jnp.int32, (tr, tr), 0)
        c = lax.broadcasted_iota(jnp.int32, (tr, tr), 1)
        tri = jnp.where(c < r, 1.0, 0.0).astype(BF16)
        before = _dot(tri, picked.astype(BF16)) + carry_s[0:1]
        base = before + pstart_s[0:1]
        carry_s[...] += tile_count
        den = sum(jnp.exp(v - vals[0]) for v in vals)
        pos = jnp.zeros((tr, LANES), F32)
        wts = jnp.zeros((tr, LANES), F32)
        for k in range(TOP_K):
            pk = jnp.where(sels[k], base, 0.0).sum(axis=-1, keepdims=True)
            pos = jnp.where(lane == k, pk, pos)
            wts = jnp.where(lane == k, jnp.exp(vals[k] - vals[0]) / den, wts)
        pos_ref[...] = pos.astype(jnp.int32)
        w_ref[...] = wts


def _route(logits, tm):
    n = logits.shape[0]
    tr = min(TR_ROUTE, n)
    return pl.pallas_call(
        functools.partial(_route_kernel, tr=tr, tm=tm),
        out_shape=[jax.ShapeDtypeStruct((n, LANES), jnp.int32),
                   jax.ShapeDtypeStruct((n, LANES), F32),
                   jax.ShapeDtypeStruct((MAX_TILES_PAD, LANES), jnp.int32),
                   jax.ShapeDtypeStruct((8, LANES), jnp.int32)],
        grid=(2, n // tr),
        in_specs=[pl.BlockSpec((tr, LANES), lambda p, i: (i, 0))],
        out_specs=[pl.BlockSpec((tr, LANES), lambda p, i: (i * p, 0)),
                   pl.BlockSpec((tr, LANES), lambda p, i: (i * p, 0)),
                   pl.BlockSpec((MAX_TILES_PAD, LANES), lambda p, i: (0, 0)),
                   pl.BlockSpec((8, LANES), lambda p, i: (0, 0))],
        scratch_shapes=[pltpu.VMEM((8, LANES), F32)] * 3,
        compiler_params=_params(("arbitrary", "arbitrary")),
    )(logits)


def _dispatch_kernel(pos_ref, h_ref, xs_in_ref, xs_ref, sem, *, tm):
    del xs_in_ref

    def row_copy(r, k):
        return pltpu.make_async_copy(h_ref.at[pl.ds(r, 1)],
                                     xs_ref.at[pl.ds(pos_ref[r * TOP_K + k], 1)], sem)

    def issue(r, c):
        for k in range(TOP_K):
            row_copy(r, k).start()
        return c

    lax.fori_loop(0, tm, issue, 0)
    for _ in range(TOP_K):
        pltpu.make_async_copy(h_ref, xs_ref.at[pl.ds(0, tm)], sem).wait()


def _dispatch(pos_flat, h2, cap):
    n, d = h2.shape
    tm = min(TM_ROWS, n)
    return pl.pallas_call(
        functools.partial(_dispatch_kernel, tm=tm),
        out_shape=jax.ShapeDtypeStruct((cap, d), F32),
        grid=(n // tm,),
        in_specs=[pl.BlockSpec((tm * TOP_K,), lambda i: (i,), memory_space=pltpu.SMEM),
                  pl.BlockSpec((tm, d), lambda i: (i, 0)),
                  pl.BlockSpec(memory_space=pl.ANY)],
        out_specs=pl.BlockSpec(memory_space=pl.ANY),
        scratch_shapes=[pltpu.SemaphoreType.DMA(())],
        input_output_aliases={2: 0},
        compiler_params=_params(("arbitrary",)),
    )(pos_flat, h2, jnp.zeros((cap, d), F32))


def _moe_kernel(te_ref, nv_ref, x_ref, wg_ref, bg_ref, wu_ref, bu_ref, wd_ref, bd_ref,
                y_ref, wg_s, wu_s, wd_s):
    i = pl.program_id(0)

    @pl.when(i < nv_ref[0])
    def _():
        changed = (i == 0) | (te_ref[i] != te_ref[jnp.maximum(i - 1, 0)])

        @pl.when(changed)
        def _():
            wg_s[...] = wg_ref[0].astype(BF16)
            wu_s[...] = wu_ref[0].astype(BF16)
            wd_s[...] = wd_ref[0].astype(BF16)

        xb = x_ref[...].astype(BF16)
        g = _dot(xb, wg_s[...]) + bg_ref[0]
        u = _dot(xb, wu_s[...]) + bu_ref[0]
        g = jnp.minimum(g, SWIGLU_LIMIT)
        u = jnp.clip(u, -SWIGLU_LIMIT, SWIGLU_LIMIT)
        h = (u + 1.0) * (g * jax.nn.sigmoid(SWIGLU_ALPHA * g))
        y_ref[...] = _dot(h.astype(BF16), wd_s[...]) + bd_ref[0]

    @pl.when(i >= nv_ref[0])
    def _():
        y_ref[...] = jnp.zeros_like(y_ref)


def _moe(tile_expert, n_valid, xs, w_gate, b_gate, w_up, b_up, w_down, b_down, tm):
    cap, d = xs.shape
    e, _, f = w_gate.shape
    n_tiles = cap // tm

    def tile(i, te, nv):
        return jnp.minimum(i, nv[0] - 1)

    row = lambda i, te, nv: (tile(i, te, nv), 0)
    exp3 = lambda i, te, nv: (te[tile(i, te, nv)], 0, 0)
    grid_spec = pltpu.PrefetchScalarGridSpec(
        num_scalar_prefetch=2,
        grid=(n_tiles,),
        in_specs=[pl.BlockSpec((tm, d), row),
                  pl.BlockSpec((1, d, f), exp3), pl.BlockSpec((1, 1, f), exp3),
                  pl.BlockSpec((1, d, f), exp3), pl.BlockSpec((1, 1, f), exp3),
                  pl.BlockSpec((1, f, d), exp3), pl.BlockSpec((1, 1, d), exp3)],
        out_specs=pl.BlockSpec((tm, d), lambda i, te, nv: (i, 0)),
        scratch_shapes=[pltpu.VMEM((d, f), BF16), pltpu.VMEM((d, f), BF16), pltpu.VMEM((f, d), BF16)],
    )
    return pl.pallas_call(
        _moe_kernel,
        out_shape=jax.ShapeDtypeStruct((cap, d), F32),
        grid_spec=grid_spec,
        compiler_params=_params(("arbitrary",), 56),
    )(tile_expert, n_valid, xs, w_gate, b_gate.reshape(e, 1, f), w_up, b_up.reshape(e, 1, f),
      w_down, b_down.reshape(e, 1, d))


def _combine_kernel(pos_ref, posn_ref, w_ref, x1_ref, mod_ref, g_ref, b_ref, y_ref, o_ref,
                    buf, sem, *, tm):
    i = pl.program_id(0)
    n = pl.num_programs(0)

    def issue(p_ref, slot):
        def body(r, c):
            for k in range(TOP_K):
                pltpu.make_async_copy(y_ref.at[pl.ds(p_ref[r * TOP_K + k], 1)],
                                      buf.at[slot, k, pl.ds(r, 1)], sem.at[slot]).start()
            return c
        lax.fori_loop(0, tm, body, 0)

    @pl.when(i == 0)
    def _():
        issue(pos_ref, 0)

    @pl.when(i + 1 < n)
    def _():
        issue(posn_ref, (i + 1) % 2)

    slot = i % 2
    for k in range(TOP_K):
        pltpu.make_async_copy(y_ref.at[pl.ds(0, tm)], buf.at[slot, k], sem.at[slot]).wait()
    w = w_ref[...]
    f = jnp.zeros(x1_ref.shape, F32)
    for k in range(TOP_K):
        f = f + w[:, k:k + 1] * buf[slot, k]
    m = mod_ref[0]
    o_ref[...] = _layer_norm(DEEPNORM_ALPHA * x1_ref[...] + m[5:6] * f, g_ref[...], b_ref[...])


def _combine(pos_flat, w, x1, mod3, g2, b2, y, seq):
    n, d = x1.shape
    tm = min(TM_ROWS, seq)
    nsteps = n // tm
    per_seq = seq // tm
    return pl.pallas_call(
        functools.partial(_combine_kernel, tm=tm),
        out_shape=jax.ShapeDtypeStruct((n, d), F32),
        grid=(nsteps,),
        in_specs=[pl.BlockSpec((tm * TOP_K,), lambda i: (i,), memory_space=pltpu.SMEM),
                  pl.BlockSpec((tm * TOP_K,), lambda i: (jnp.minimum(i + 1, nsteps - 1),),
                               memory_space=pltpu.SMEM),
                  pl.BlockSpec((tm, LANES), lambda i: (i, 0)),
                  pl.BlockSpec((tm, d), lambda i: (i, 0)),
                  pl.BlockSpec((1, 6, d), lambda i: (i // per_seq, 0, 0)),
                  pl.BlockSpec(g2.shape, lambda i: (0, 0)),
                  pl.BlockSpec(b2.shape, lambda i: (0, 0)),
                  pl.BlockSpec(memory_space=pl.ANY)],
        out_specs=pl.BlockSpec((tm, d), lambda i: (i, 0)),
        scratch_shapes=[pltpu.VMEM((2, TOP_K, tm, d), F32), pltpu.SemaphoreType.DMA((2,))],
        compiler_params=_params(("arbitrary",), 32),
    )(pos_flat, pos_flat, w, x1, mod3, g2, b2, y)


def _prep_w_in(w_in):
    d = w_in.shape[0]
    a3 = 3 * A_WIDTH
    lat = a3 + B_Q_LORA + B_KV_LORA
    z = lambda n: jnp.zeros((d, n), w_in.dtype)
    return jnp.concatenate([
        w_in[:, :A_WIDTH] * (A_HEAD_DIM ** -0.5),
        w_in[:, A_WIDTH:lat],
        z(B_NOPE_DIM), w_in[:, lat:lat + B_ROPE_DIM], z(LANES - B_NOPE_DIM - B_ROPE_DIM),
        w_in[:, lat + B_ROPE_DIM:]], axis=1).astype(BF16)


def _prep_w_uq(w_uq):
    r = w_uq.shape[0]
    w = w_uq.reshape(r, B_HEADS, B_QK_DIM)
    w = jnp.pad(w, ((0, 0), (0, 0), (0, LANES - B_QK_DIM)))
    return w.reshape(r, B_HEADS * LANES).astype(BF16)


def _prep_w_ukv(w_ukv):
    r = w_ukv.shape[0]
    w = w_ukv.reshape(r, B_HEADS, B_NOPE_DIM + B_V_DIM)
    wk = jnp.pad(w[:, :, :B_NOPE_DIM], ((0, 0), (0, 0), (0, LANES - B_NOPE_DIM)))
    wv = w[:, :, B_NOPE_DIM:]
    return (wk.reshape(r, B_HEADS * LANES).astype(BF16),
            wv.reshape(r, B_HEADS * B_V_DIM).astype(BF16))


def _rope_tables(positions):
    half = B_ROPE_DIM // 2
    freqs = ROPE_THETA ** (-jnp.arange(half, dtype=F32) / half)
    ang = positions.astype(F32)[:, :, None] * freqs
    cos, sin = jnp.cos(ang), jnp.sin(ang)
    z = lambda n: jnp.zeros(ang.shape[:2] + (n,), F32)
    tail = LANES - B_NOPE_DIM - B_ROPE_DIM
    cos_t = jnp.concatenate([jnp.ones(ang.shape[:2] + (B_NOPE_DIM,), F32), cos, cos, z(tail)], -1)
    sin_a = jnp.concatenate([z(B_NOPE_DIM), -sin, z(half), z(tail)], -1)
    sin_b = jnp.concatenate([z(B_NOPE_DIM), z(half), sin, z(tail)], -1)
    return cos_t, sin_a, sin_b


def kernel(x, c, positions, w_ada, b_ada, w_in, rms_q, w_uq, rms_kv, w_ukv, rel_bias,
           w_branch_a, w_branch_b, w_out, ln1_g, ln1_b, w_router, b_router,
           w_gate, b_gate, w_up, b_up, w_down, b_down, ln2_g, ln2_b):
    bsz, seq, d = x.shape
    n_tok = bsz * seq
    assert w_ada.shape[0] == DEPTH == 1
    row = lambda v: v.reshape(1, -1)

    mod3 = _ada(c, w_ada[0], b_ada[0]).reshape(bsz, 6, d)

    qkv, cq, ckv, kr, ga, gb = _inproj(x, mod3, _prep_w_in(w_in[0]))
    ya = _attn_a(qkv, _bias_table(rel_bias[0], min(TQ_A, seq)))
    wk, wv = _prep_w_ukv(w_ukv[0])
    qm, km, vm = _mla_proj(cq, ckv, kr, *_rope_tables(positions), row(rms_q[0]), row(rms_kv[0]),
                           _prep_w_uq(w_uq[0]), wk, wv)
    yb = _mla_attn(qm, km, vm)

    wr = jnp.pad(w_router[0], ((0, 0), (0, LANES - N_EXPERTS)))
    br = jnp.pad(b_router[0], (0, LANES - N_EXPERTS), constant_values=NEG_INF)
    x1, h2, logits = _merge(x, ya, yb, ga, gb, mod3,
                            w_branch_a[0].astype(BF16), w_branch_b[0].astype(BF16),
                            w_out[0].astype(BF16), row(ln1_g[0]), row(ln1_b[0]), wr, row(br))

    tm = TM_MOE
    n_tiles = -(-(n_tok * TOP_K) // tm) + N_EXPERTS
    assert n_tiles <= MAX_TILES_PAD
    pos, wts, tmap, nv = _route(logits.reshape(n_tok, LANES), tm)
    pos_flat = pos[:, :TOP_K].reshape(-1)
    xs = _dispatch(pos_flat, h2.reshape(n_tok, d), n_tiles * tm)
    y = _moe(tmap[:n_tiles, 0], nv[0, :1], xs, w_gate[0], b_gate[0], w_up[0], b_up[0],
             w_down[0], b_down[0], tm)
    out = _combine(pos_flat, wts, x1.reshape(n_tok, d), mod3, row(ln2_g[0]), row(ln2_b[0]), y, seq)
    return out.reshape(bsz, seq, d)
```

```python
import functools

import jax
import jax.numpy as jnp
from jax import lax
from jax.experimental import pallas as pl
from jax.experimental.pallas import tpu as pltpu

F32 = jnp.float32
BF16 = jnp.bfloat16

CHUNK = 64
A_HEADS = 8
A_HEAD_DIM = 64
A_LEFT_CHUNKS = 8
A_MAX_REL = 128
A_WIDTH = A_HEADS * A_HEAD_DIM
B_HEADS = 8
B_NOPE_DIM = 64
B_ROPE_DIM = 32
B_V_DIM = 64
B_Q_LORA = 384
B_KV_LORA = 256
B_QK_DIM = B_NOPE_DIM + B_ROPE_DIM
ROPE_THETA = 10000.0
N_EXPERTS = 32
TOP_K = 4
SWIGLU_LIMIT = 7.0
SWIGLU_ALPHA = 1.702
DEPTH = 1
DEEPNORM_ALPHA = (2.0 * DEPTH) ** 0.25
LN_EPS = 1e-5
RMS_EPS = 1e-6
NEG_INF = -1e30
LOG2_E = 1.4426950408889634

LANES = 128
MXU_COLS = 256
V_PAIR_LANES = MXU_COLS

TM_PROJ = 512
TQ_A = 256
TQ_B = 512
TM_MERGE = 256
TR_ROUTE = 256
TM_MOE = 256
TM_ROWS = 256
MAX_TILES_PAD = 512


def _params(sem, vmem_mb=None):
    return pltpu.CompilerParams(
        dimension_semantics=sem,
        vmem_limit_bytes=None if vmem_mb is None else vmem_mb << 20)


def _dot(a, b):
    return jnp.dot(a, b, preferred_element_type=F32)


def _dot_nt(a, b):
    return lax.dot_general(a, b, (((1,), (1,)), ((), ())), preferred_element_type=F32)


def _layer_norm(x, g, b):
    mu = jnp.mean(x, axis=-1, keepdims=True)
    xc = x - mu
    var = jnp.mean(xc * xc, axis=-1, keepdims=True)
    return xc * lax.rsqrt(var + LN_EPS) * g + b


def _ada_kernel(c_ref, w_ref, b_ref, o_ref):
    c = c_ref[...]
    sc = c * jax.nn.sigmoid(c)
    o_ref[...] = jnp.dot(sc, w_ref[...], preferred_element_type=F32,
                         precision=lax.Precision.HIGHEST) + b_ref[...]


def _ada(c, w_ada, b_ada):
    bsz, d = c.shape
    m = w_ada.shape[1]
    tn = 1024
    return pl.pallas_call(
        _ada_kernel,
        out_shape=jax.ShapeDtypeStruct((bsz, m), F32),
        grid=(m // tn,),
        in_specs=[pl.BlockSpec((bsz, d), lambda j: (0, 0)),
                  pl.BlockSpec((d, tn), lambda j: (0, j)),
                  pl.BlockSpec((1, tn), lambda j: (0, j))],
        out_specs=pl.BlockSpec((bsz, tn), lambda j: (0, j)),
        compiler_params=_params(("arbitrary",)),
    )(c, w_ada, b_ada.reshape(1, m))


def _inproj_kernel(x_ref, mod_ref, w_ref, qkv_ref, cq_ref, ckv_ref, kr_ref, ga_ref, gb_ref):
    m = mod_ref[0]
    h = (x_ref[0] * (1.0 + m[1:2]) + m[0:1]).astype(BF16)
    col = 0
    for ref in (qkv_ref, cq_ref, ckv_ref, kr_ref, ga_ref, gb_ref):
        n = ref.shape[-1]
        ref[0] = _dot(h, w_ref[:, col:col + n]).astype(ref.dtype)
        col += n


def _inproj(x, mod3, w):
    bsz, seq, d = x.shape
    tm = min(TM_PROJ, seq)
    widths = (3 * A_WIDTH, B_Q_LORA, B_KV_LORA, LANES, d, d)
    dtypes = (BF16, BF16, BF16, F32, BF16, BF16)
    assert sum(widths) == w.shape[1]
    return pl.pallas_call(
        _inproj_kernel,
        out_shape=[jax.ShapeDtypeStruct((bsz, seq, n), dt) for n, dt in zip(widths, dtypes)],
        grid=(bsz, seq // tm),
        in_specs=[pl.BlockSpec((1, tm, d), lambda b, i: (b, i, 0)),
                  pl.BlockSpec((1, 6, d), lambda b, i: (b, 0, 0)),
                  pl.BlockSpec(w.shape, lambda b, i: (0, 0))],
        out_specs=[pl.BlockSpec((1, tm, n), lambda b, i: (b, i, 0)) for n in widths],
        compiler_params=_params(("arbitrary", "arbitrary"), 48),
    )(x, mod3, w)


def _attn_a_kernel(q_ref, k_ref, v_ref, bias_ref, o_ref, *, tq, nd):
    i = pl.program_id(2)
    q = q_ref[0]
    lane = lax.broadcasted_iota(jnp.int32, (1, LANES), 1)
    outs = []
    for hh in range(2):
        in_head = (lane // A_HEAD_DIM) == hh
        scores, values = [], []
        for d in range(nd):
            start = pl.multiple_of(jnp.maximum(i - d, 0) * tq, tq)
            k = k_ref[0, pl.ds(start, tq), :]
            kh = jnp.where(in_head, k, jnp.zeros_like(k))
            s = _dot_nt(q, kh) + bias_ref[hh, :, (nd - 1 - d) * tq:(nd - d) * tq]
            if d > 0:
                s = jnp.where(i >= d, s, NEG_INF)
            scores.append(s)
            values.append(v_ref[0, pl.ds(start, tq), :])
        m = scores[0].max(axis=-1, keepdims=True)
        for s in scores[1:]:
            m = jnp.maximum(m, s.max(axis=-1, keepdims=True))
        l = jnp.zeros_like(m)
        acc = jnp.zeros((tq, LANES), F32)
        for s, v in zip(scores, values):
            p = jnp.exp(s - m)
            l = l + p.sum(axis=-1, keepdims=True)
            acc = acc + _dot(p.astype(BF16), v)
        outs.append(acc / l)
    o_ref[0] = jnp.where(lane < A_HEAD_DIM, outs[0], outs[1]).astype(o_ref.dtype)


def _attn_a(qkv, bias_tab):
    bsz, seq, _ = qkv.shape
    tq = min(TQ_A, seq)
    nd = bias_tab.shape[2] // tq
    ncb = A_WIDTH // LANES
    return pl.pallas_call(
        functools.partial(_attn_a_kernel, tq=tq, nd=nd),
        out_shape=jax.ShapeDtypeStruct((bsz, seq, A_WIDTH), BF16),
        grid=(ncb, bsz, seq // tq),
        in_specs=[pl.BlockSpec((1, tq, LANES), lambda hp, b, i: (b, i, hp)),
                  pl.BlockSpec((1, seq, LANES), lambda hp, b, i: (b, 0, ncb + hp)),
                  pl.BlockSpec((1, seq, LANES), lambda hp, b, i: (b, 0, 2 * ncb + hp)),
                  pl.BlockSpec((2, tq, nd * tq), lambda hp, b, i: (hp, 0, 0))],
        out_specs=pl.BlockSpec((1, tq, LANES), lambda hp, b, i: (b, i, hp)),
        compiler_params=_params(("arbitrary",) * 3, 32),
    )(qkv, qkv, qkv, bias_tab)


def _bias_table(rel_bias, tq):
    nd = (A_LEFT_CHUNKS * CHUNK + tq - 1) // tq + 1
    width = nd * tq
    period = width + tq
    u = jnp.arange(period)
    vec = rel_bias[:, jnp.clip(width - 1 - u, -A_MAX_REL, A_MAX_REL) + A_MAX_REL].astype(F32)
    skew = jnp.tile(vec, (1, tq + 1))[:, :tq * (period + 1)].reshape(-1, tq, period + 1)
    toep = skew[:, ::-1, :width]
    qi = jnp.arange(tq)[:, None]
    x = jnp.arange(width)[None, :]
    dchunk = qi // CHUNK - x // CHUNK + (nd - 1) * (tq // CHUNK)
    valid = (dchunk >= 0) & (dchunk <= A_LEFT_CHUNKS)
    return jnp.where(valid[None], toep, NEG_INF)


def _mla_proj_kernel(cq_ref, ckv_ref, kr_ref, cos_ref, sa_ref, sb_ref, rq_ref, rkv_ref,
                     wq_ref, wk_ref, wv_ref, q_out, k_out, v_out):
    def rms(x, g):
        xf = x.astype(F32)
        ms = jnp.mean(xf * xf, axis=-1, keepdims=True)
        return (xf * lax.rsqrt(ms + RMS_EPS) * g).astype(BF16)

    cqn = rms(cq_ref[0], rq_ref[...])
    ckvn = rms(ckv_ref[0], rkv_ref[...])
    q = _dot(cqn, wq_ref[...])
    kn = _dot(ckvn, wk_ref[...])
    vlane = lax.broadcasted_iota(jnp.int32, (1, v_out.shape[-1]), 1)
    ones_lane = jnp.where(vlane % V_PAIR_LANES == 2 * B_V_DIM, 1.0, 0.0)
    v_out[0] = (_dot(ckvn, wv_ref[...]) + ones_lane).astype(v_out.dtype)
    cos_t, sin_a, sin_b = cos_ref[0], sa_ref[0], sb_ref[0]

    def rope(x):
        return (x * cos_t + pltpu.roll(x, LANES - B_ROPE_DIM // 2, 1) * sin_a
                + pltpu.roll(x, B_ROPE_DIM // 2, 1) * sin_b)

    kpe = rope(kr_ref[0])
    scale = B_QK_DIM ** -0.5 * LOG2_E
    for h in range(B_HEADS):
        sl = slice(h * LANES, (h + 1) * LANES)
        q_out[0, :, sl] = (rope(q[:, sl]) * scale).astype(q_out.dtype)
        k_out[0, :, sl] = (kn[:, sl] + kpe).astype(k_out.dtype)


def _mla_proj(cq, ckv, kr, cos_t, sin_a, sin_b, rms_q, rms_kv, wq, wk, wv):
    bsz, seq, _ = cq.shape
    tm = min(TM_PROJ, seq)
    tok = lambda n: pl.BlockSpec((1, tm, n), lambda b, i: (b, i, 0))
    full = lambda a: pl.BlockSpec(a.shape, lambda b, i: (0,) * a.ndim)
    return pl.pallas_call(
        _mla_proj_kernel,
        out_shape=[jax.ShapeDtypeStruct((bsz, seq, B_HEADS * LANES), BF16),
                   jax.ShapeDtypeStruct((bsz, seq, B_HEADS * LANES), BF16),
                   jax.ShapeDtypeStruct((bsz, seq, wv.shape[1]), BF16)],
        grid=(bsz, seq // tm),
        in_specs=[tok(B_Q_LORA), tok(B_KV_LORA), tok(LANES), tok(LANES), tok(LANES), tok(LANES),
                  full(rms_q), full(rms_kv), full(wq), full(wk), full(wv)],
        out_specs=[tok(B_HEADS * LANES), tok(B_HEADS * LANES), tok(wv.shape[1])],
        compiler_params=_params(("arbitrary", "arbitrary"), 32),
    )(cq, ckv, kr, cos_t, sin_a, sin_b, rms_q, rms_kv, wq, wk, wv)


def _mla_attn_kernel(q_ref, k_ref, v_ref, o_ref, s_a, s_b, m_s, acc_s, *, t):
    i = pl.program_id(2)
    lane = lax.broadcasted_iota(jnp.int32, (1, LANES), 1)
    m_s[...] = jnp.full(m_s.shape, NEG_INF, F32)
    acc_s[...] = jnp.zeros(acc_s.shape, F32)

    def produce(kj, s_ref):
        start = pl.multiple_of(kj * t, t)
        for hh in range(2):
            q = q_ref[0, :, hh * LANES:(hh + 1) * LANES]
            k = k_ref[0, pl.ds(start, t), hh * LANES:(hh + 1) * LANES]
            s_ref[hh] = _dot_nt(q, k)

    def consume(kj, s_ref, masked):
        start = pl.multiple_of(kj * t, t)
        v = v_ref[0, pl.ds(start, t), :]
        for hh in range(2):
            s = s_ref[hh]
            if masked:
                row_chunk = lax.broadcasted_iota(jnp.int32, (t, t), 0) // CHUNK
                col_chunk = lax.broadcasted_iota(jnp.int32, (t, t), 1) // CHUNK
                s = jnp.where(col_chunk <= row_chunk, s, NEG_INF)
            m_old = m_s[hh]
            m_new = jnp.maximum(m_old, jnp.broadcast_to(s.max(axis=-1, keepdims=True), m_old.shape))
            p = jnp.exp2(s - jnp.tile(m_new, (1, t // LANES))).astype(BF16)
            alpha = jnp.exp2(m_old - m_new)
            acc_s[hh] = jnp.tile(alpha, (1, V_PAIR_LANES // LANES)) * acc_s[hh] + _dot(p, v)
            m_s[hh] = m_new

    def body(pair, c):
        kj = 2 * pair
        produce(kj + 1, s_b)
        consume(kj, s_a, False)
        produce(kj + 2, s_a)
        consume(kj + 1, s_b, False)
        return c

    produce(0, s_a)
    lax.fori_loop(0, i // 2, body, 0)

    @pl.when(i % 2 == 0)
    def _():
        consume(i, s_a, True)

    @pl.when(i % 2 == 1)
    def _():
        produce(i, s_b)
        consume(i - 1, s_a, False)
        consume(i, s_b, True)

    outs = []
    for hh in range(2):
        acc = acc_s[hh]
        outs.append(acc[:, :LANES] / acc[:, 2 * B_V_DIM:2 * B_V_DIM + 1])
    o_ref[0] = jnp.where(lane < B_V_DIM, outs[0], outs[1]).astype(o_ref.dtype)


def _mla_attn(qm, km, vm):
    bsz, seq, _ = qm.shape
    t = min(TQ_B, seq)
    return pl.pallas_call(
        functools.partial(_mla_attn_kernel, t=t),
        out_shape=jax.ShapeDtypeStruct((bsz, seq, B_HEADS * B_V_DIM), BF16),
        grid=(bsz, B_HEADS // 2, seq // t),
        in_specs=[pl.BlockSpec((1, t, 2 * LANES), lambda b, hp, i: (b, i, hp)),
                  pl.BlockSpec((1, seq, 2 * LANES), lambda b, hp, i: (b, 0, hp)),
                  pl.BlockSpec((1, seq, V_PAIR_LANES), lambda b, hp, i: (b, 0, hp))],
        out_specs=pl.BlockSpec((1, t, LANES), lambda b, hp, i: (b, i, hp)),
        scratch_shapes=[pltpu.VMEM((2, t, t), F32), pltpu.VMEM((2, t, t), F32),
                        pltpu.VMEM((2, t, LANES), F32),
                        pltpu.VMEM((2, t, V_PAIR_LANES), F32)],
        compiler_params=_params(("arbitrary",) * 3, 40),
    )(qm, km, vm)


def _merge_kernel(x_ref, ya_ref, yb_ref, ga_ref, gb_ref, mod_ref, wa_ref, wb_ref, wo_ref,
                  g1_ref, b1_ref, wr_ref, br_ref, x1_ref, h2_ref, lg_ref):
    m = mod_ref[0]
    a = _dot(ya_ref[0], wa_ref[...])
    b = _dot(yb_ref[0], wb_ref[...])
    merged = (jax.nn.sigmoid(ga_ref[0].astype(F32)) * a
              + jax.nn.sigmoid(gb_ref[0].astype(F32)) * b)
    o = _dot(merged.astype(BF16), wo_ref[...])
    x1 = _layer_norm(DEEPNORM_ALPHA * x_ref[0] + m[2:3] * o, g1_ref[...], b1_ref[...])
    h2 = x1 * (1.0 + m[4:5]) + m[3:4]
    x1_ref[0] = x1
    h2_ref[0] = h2
    lg_ref[0] = jnp.dot(h2, wr_ref[...], preferred_element_type=F32,
                        precision=lax.Precision.HIGHEST) + br_ref[...]


def _merge(x, ya, yb, ga, gb, mod3, wa, wb, wo, g1, b1, wr, br):
    bsz, seq, d = x.shape
    tm = min(TM_MERGE, seq)
    tok = lambda n: pl.BlockSpec((1, tm, n), lambda b, i: (b, i, 0))
    full = lambda a: pl.BlockSpec(a.shape, lambda b, i: (0,) * a.ndim)
    return pl.pallas_call(
        _merge_kernel,
        out_shape=[jax.ShapeDtypeStruct((bsz, seq, d), F32),
                   jax.ShapeDtypeStruct((bsz, seq, d), F32),
                   jax.ShapeDtypeStruct((bsz, seq, LANES), F32)],
        grid=(bsz, seq // tm),
        in_specs=[tok(d), tok(A_WIDTH), tok(B_HEADS * B_V_DIM), tok(d), tok(d),
                  pl.BlockSpec((1, 6, d), lambda b, i: (b, 0, 0)),
                  full(wa), full(wb), full(wo), full(g1), full(b1), full(wr), full(br)],
        out_specs=[tok(d), tok(d), tok(LANES)],
        compiler_params=_params(("arbitrary", "arbitrary"), 40),
    )(x, ya, yb, ga, gb, mod3, wa, wb, wo, g1, b1, wr, br)


def _route_kernel(lg_ref, pos_ref, w_ref, tmap_ref, nv_ref, cnt_s, carry_s, pstart_s, *, tr, tm):
    phase = pl.program_id(0)
    i = pl.program_id(1)
    lane = lax.broadcasted_iota(jnp.int32, (tr, LANES), 1)
    work = lg_ref[...]
    sels, vals = [], []
    for _ in range(TOP_K):
        m = work.max(axis=-1, keepdims=True)
        idx = jnp.where(work == m, lane, LANES).min(axis=-1, keepdims=True)
        sel = lane == idx
        work = jnp.where(sel, -jnp.inf, work)
        sels.append(sel)
        vals.append(m)
    picked = sum(jnp.where(s, 1.0, 0.0) for s in sels)
    tile_count = jnp.broadcast_to(picked.sum(axis=0, keepdims=True), (8, LANES))

    @pl.when((phase == 0) & (i == 0))
    def _():
        cnt_s[...] = jnp.zeros_like(cnt_s)

    @pl.when(phase == 0)
    def _():
        cnt_s[...] += tile_count

    @pl.when((phase == 1) & (i == 0))
    def _():
        lane8 = lax.broadcasted_iota(jnp.int32, (8, LANES), 1)
        padded = jnp.floor((cnt_s[...] + (tm - 1)) * (1.0 / tm)) * tm
        cum = padded
        shift = 1
        while shift < N_EXPERTS:
            cum = cum + jnp.where(lane8 >= shift, pltpu.roll(cum, shift, 1), 0.0)
            shift *= 2
        pstart_s[...] = cum - padded
        carry_s[...] = jnp.zeros_like(carry_s)
        nt = tmap_ref.shape[0]
        tile_start = (lax.broadcasted_iota(jnp.int32, (nt, LANES), 0) * tm).astype(F32)
        lane_t = lax.broadcasted_iota(jnp.int32, (nt, LANES), 1)
        done = jnp.where((lane_t < N_EXPERTS) & (cum[0:1] <= tile_start), 1.0, 0.0)
        te = jnp.minimum(done.sum(axis=-1, keepdims=True), N_EXPERTS - 1.0)
        tmap_ref[...] = jnp.broadcast_to(te, (nt, LANES)).astype(jnp.int32)
        total = jnp.where(lane8 == N_EXPERTS - 1, cum, 0.0).sum(axis=-1, keepdims=True)
        nv_ref[...] = jnp.broadcast_to(total * (1.0 / tm), (8, LANES)).astype(jnp.int32)

    @pl.when(phase == 1)
    def _():
        r = lax.broadcasted_iota(jnp.int32, (tr, tr), 0)
        c = lax.broadcasted_iota(jnp.int32, (tr, tr), 1)
        tri = jnp.where(c < r, 1.0, 0.0).astype(BF16)
        before = _dot(tri, picked.astype(BF16)) + carry_s[0:1]
        base = before + pstart_s[0:1]
        carry_s[...] += tile_count
        den = sum(jnp.exp(v - vals[0]) for v in vals)
        pos = jnp.zeros((tr, LANES), F32)
        wts = jnp.zeros((tr, LANES), F32)
        for k in range(TOP_K):
            pk = jnp.where(sels[k], base, 0.0).sum(axis=-1, keepdims=True)
            pos = jnp.where(lane == k, pk, pos)
            wts = jnp.where(lane == k, jnp.exp(vals[k] - vals[0]) / den, wts)
        pos_ref[...] = pos.astype(jnp.int32)
        w_ref[...] = wts


def _route(logits, tm):
    n = logits.shape[0]
    tr = min(TR_ROUTE, n)
    return pl.pallas_call(
        functools.partial(_route_kernel, tr=tr, tm=tm),
        out_shape=[jax.ShapeDtypeStruct((n, LANES), jnp.int32),
                   jax.ShapeDtypeStruct((n, LANES), F32),
                   jax.ShapeDtypeStruct((MAX_TILES_PAD, LANES), jnp.int32),
                   jax.ShapeDtypeStruct((8, LANES), jnp.int32)],
        grid=(2, n // tr),
        in_specs=[pl.BlockSpec((tr, LANES), lambda p, i: (i, 0))],
        out_specs=[pl.BlockSpec((tr, LANES), lambda p, i: (i * p, 0)),
                   pl.BlockSpec((tr, LANES), lambda p, i: (i * p, 0)),
                   pl.BlockSpec((MAX_TILES_PAD, LANES), lambda p, i: (0, 0)),
                   pl.BlockSpec((8, LANES), lambda p, i: (0, 0))],
        scratch_shapes=[pltpu.VMEM((8, LANES), F32)] * 3,
        compiler_params=_params(("arbitrary", "arbitrary")),
    )(logits)


def _dispatch_kernel(pos_ref, h_ref, xs_in_ref, xs_ref, sem, *, tm):
    del xs_in_ref

    def row_copy(r, k):
        return pltpu.make_async_copy(h_ref.at[pl.ds(r, 1)],
                                     xs_ref.at[pl.ds(pos_ref[r * TOP_K + k], 1)], sem)

    def issue(r, c):
        for k in range(TOP_K):
            row_copy(r, k).start()
        return c

    lax.fori_loop(0, tm, issue, 0)
    for _ in range(TOP_K):
        pltpu.make_async_copy(h_ref, xs_ref.at[pl.ds(0, tm)], sem).wait()


def _dispatch(pos_flat, h2, cap):
    n, d = h2.shape
    tm = min(TM_ROWS, n)
    return pl.pallas_call(
        functools.partial(_dispatch_kernel, tm=tm),
        out_shape=jax.ShapeDtypeStruct((cap, d), F32),
        grid=(n // tm,),
        in_specs=[pl.BlockSpec((tm * TOP_K,), lambda i: (i,), memory_space=pltpu.SMEM),
                  pl.BlockSpec((tm, d), lambda i: (i, 0)),
                  pl.BlockSpec(memory_space=pl.ANY)],
        out_specs=pl.BlockSpec(memory_space=pl.ANY),
        scratch_shapes=[pltpu.SemaphoreType.DMA(())],
        input_output_aliases={2: 0},
        compiler_params=_params(("arbitrary",)),
    )(pos_flat, h2, jnp.zeros((cap, d), F32))


def _moe_kernel(te_ref, nv_ref, x_ref, wg_ref, bg_ref, wu_ref, bu_ref, wd_ref, bd_ref,
                y_ref, wg_s, wu_s, wd_s):
    i = pl.program_id(0)

    @pl.when(i < nv_ref[0])
    def _():
        changed = (i == 0) | (te_ref[i] != te_ref[jnp.maximum(i - 1, 0)])

        @pl.when(changed)
        def _():
            wg_s[...] = wg_ref[0].astype(BF16)
            wu_s[...] = wu_ref[0].astype(BF16)
            wd_s[...] = wd_ref[0].astype(BF16)

        xb = x_ref[...].astype(BF16)
        g = _dot(xb, wg_s[...]) + bg_ref[0]
        u = _dot(xb, wu_s[...]) + bu_ref[0]
        g = jnp.minimum(g, SWIGLU_LIMIT)
        u = jnp.clip(u, -SWIGLU_LIMIT, SWIGLU_LIMIT)
        h = (u + 1.0) * (g * jax.nn.sigmoid(SWIGLU_ALPHA * g))
        y_ref[...] = _dot(h.astype(BF16), wd_s[...]) + bd_ref[0]

    @pl.when(i >= nv_ref[0])
    def _():
        y_ref[...] = jnp.zeros_like(y_ref)


def _moe(tile_expert, n_valid, xs, w_gate, b_gate, w_up, b_up, w_down, b_down, tm):
    cap, d = xs.shape
    e, _, f = w_gate.shape
    n_tiles = cap // tm

    def tile(i, te, nv):
        return jnp.minimum(i, nv[0] - 1)

    row = lambda i, te, nv: (tile(i, te, nv), 0)
    exp3 = lambda i, te, nv: (te[tile(i, te, nv)], 0, 0)
    grid_spec = pltpu.PrefetchScalarGridSpec(
        num_scalar_prefetch=2,
        grid=(n_tiles,),
        in_specs=[pl.BlockSpec((tm, d), row),
                  pl.BlockSpec((1, d, f), exp3), pl.BlockSpec((1, 1, f), exp3),
                  pl.BlockSpec((1, d, f), exp3), pl.BlockSpec((1, 1, f), exp3),
                  pl.BlockSpec((1, f, d), exp3), pl.BlockSpec((1, 1, d), exp3)],
        out_specs=pl.BlockSpec((tm, d), lambda i, te, nv: (i, 0)),
        scratch_shapes=[pltpu.VMEM((d, f), BF16), pltpu.VMEM((d, f), BF16), pltpu.VMEM((f, d), BF16)],
    )
    return pl.pallas_call(
        _moe_kernel,
        out_shape=jax.ShapeDtypeStruct((cap, d), F32),
        grid_spec=grid_spec,
        compiler_params=_params(("arbitrary",), 56),
    )(tile_expert, n_valid, xs, w_gate, b_gate.reshape(e, 1, f), w_up, b_up.reshape(e, 1, f),
      w_down, b_down.reshape(e, 1, d))


def _combine_kernel(pos_ref, posn_ref, w_ref, x1_ref, mod_ref, g_ref, b_ref, y_ref, o_ref,
                    buf, sem, *, tm):
    i = pl.program_id(0)
    n = pl.num_programs(0)

    def issue(p_ref, slot):
        def body(r, c):
            for k in range(TOP_K):
                pltpu.make_async_copy(y_ref.at[pl.ds(p_ref[r * TOP_K + k], 1)],
                                      buf.at[slot, k, pl.ds(r, 1)], sem.at[slot]).start()
            return c
        lax.fori_loop(0, tm, body, 0)

    @pl.when(i == 0)
    def _():
        issue(pos_ref, 0)

    @pl.when(i + 1 < n)
    def _():
        issue(posn_ref, (i + 1) % 2)

    slot = i % 2
    for k in range(TOP_K):
        pltpu.make_async_copy(y_ref.at[pl.ds(0, tm)], buf.at[slot, k], sem.at[slot]).wait()
    w = w_ref[...]
    f = jnp.zeros(x1_ref.shape, F32)
    for k in range(TOP_K):
        f = f + w[:, k:k + 1] * buf[slot, k]
    m = mod_ref[0]
    o_ref[...] = _layer_norm(DEEPNORM_ALPHA * x1_ref[...] + m[5:6] * f, g_ref[...], b_ref[...])


def _combine(pos_flat, w, x1, mod3, g2, b2, y, seq):
    n, d = x1.shape
    tm = min(TM_ROWS, seq)
    nsteps = n // tm
    per_seq = seq // tm
    return pl.pallas_call(
        functools.partial(_combine_kernel, tm=tm),
        out_shape=jax.ShapeDtypeStruct((n, d), F32),
        grid=(nsteps,),
        in_specs=[pl.BlockSpec((tm * TOP_K,), lambda i: (i,), memory_space=pltpu.SMEM),
                  pl.BlockSpec((tm * TOP_K,), lambda i: (jnp.minimum(i + 1, nsteps - 1),),
                               memory_space=pltpu.SMEM),
                  pl.BlockSpec((tm, LANES), lambda i: (i, 0)),
                  pl.BlockSpec((tm, d), lambda i: (i, 0)),
                  pl.BlockSpec((1, 6, d), lambda i: (i // per_seq, 0, 0)),
                  pl.BlockSpec(g2.shape, lambda i: (0, 0)),
                  pl.BlockSpec(b2.shape, lambda i: (0, 0)),
                  pl.BlockSpec(memory_space=pl.ANY)],
        out_specs=pl.BlockSpec((tm, d), lambda i: (i, 0)),
        scratch_shapes=[pltpu.VMEM((2, TOP_K, tm, d), F32), pltpu.SemaphoreType.DMA((2,))],
        compiler_params=_params(("arbitrary",), 32),
    )(pos_flat, pos_flat, w, x1, mod3, g2, b2, y)


def _prep_w_in(w_in):
    d = w_in.shape[0]
    a3 = 3 * A_WIDTH
    lat = a3 + B_Q_LORA + B_KV_LORA
    z = lambda n: jnp.zeros((d, n), w_in.dtype)
    return jnp.concatenate([
        w_in[:, :A_WIDTH] * (A_HEAD_DIM ** -0.5),
        w_in[:, A_WIDTH:lat],
        z(B_NOPE_DIM), w_in[:, lat:lat + B_ROPE_DIM], z(LANES - B_NOPE_DIM - B_ROPE_DIM),
        w_in[:, lat + B_ROPE_DIM:]], axis=1).astype(BF16)


def _prep_w_uq(w_uq):
    r = w_uq.shape[0]
    w = w_uq.reshape(r, B_HEADS, B_QK_DIM)
    w = jnp.pad(w, ((0, 0), (0, 0), (0, LANES - B_QK_DIM)))
    return w.reshape(r, B_HEADS * LANES).astype(BF16)


def _prep_w_ukv(w_ukv):
    r = w_ukv.shape[0]
    w = w_ukv.reshape(r, B_HEADS, B_NOPE_DIM + B_V_DIM)
    wk = jnp.pad(w[:, :, :B_NOPE_DIM], ((0, 0), (0, 0), (0, LANES - B_NOPE_DIM)))
    wv = w[:, :, B_NOPE_DIM:].reshape(r, B_HEADS // 2, 2 * B_V_DIM)
    wv = jnp.pad(wv, ((0, 0), (0, 0), (0, V_PAIR_LANES - 2 * B_V_DIM)))
    return (wk.reshape(r, B_HEADS * LANES).astype(BF16),
            wv.reshape(r, B_HEADS // 2 * V_PAIR_LANES).astype(BF16))


def _rope_tables(positions):
    half = B_ROPE_DIM // 2
    freqs = ROPE_THETA ** (-jnp.arange(half, dtype=F32) / half)
    ang = positions.astype(F32)[:, :, None] * freqs
    cos, sin = jnp.cos(ang), jnp.sin(ang)
    z = lambda n: jnp.zeros(ang.shape[:2] + (n,), F32)
    tail = LANES - B_NOPE_DIM - B_ROPE_DIM
    cos_t = jnp.concatenate([jnp.ones(ang.shape[:2] + (B_NOPE_DIM,), F32), cos, cos, z(tail)], -1)
    sin_a = jnp.concatenate([z(B_NOPE_DIM), -sin, z(half), z(tail)], -1)
    sin_b = jnp.concatenate([z(B_NOPE_DIM), z(half), sin, z(tail)], -1)
    return cos_t, sin_a, sin_b


def kernel(x, c, positions, w_ada, b_ada, w_in, rms_q, w_uq, rms_kv, w_ukv, rel_bias,
           w_branch_a, w_branch_b, w_out, ln1_g, ln1_b, w_router, b_router,
           w_gate, b_gate, w_up, b_up, w_down, b_down, ln2_g, ln2_b):
    bsz, seq, d = x.shape
    n_tok = bsz * seq
    assert w_ada.shape[0] == DEPTH == 1
    row = lambda v: v.reshape(1, -1)

    mod3 = _ada(c, w_ada[0], b_ada[0]).reshape(bsz, 6, d)

    qkv, cq, ckv, kr, ga, gb = _inproj(x, mod3, _prep_w_in(w_in[0]))
    ya = _attn_a(qkv, _bias_table(rel_bias[0], min(TQ_A, seq)))
    wk, wv = _prep_w_ukv(w_ukv[0])
    qm, km, vm = _mla_proj(cq, ckv, kr, *_rope_tables(positions), row(rms_q[0]), row(rms_kv[0]),
                           _prep_w_uq(w_uq[0]), wk, wv)
    yb = _mla_attn(qm, km, vm)

    wr = jnp.pad(w_router[0], ((0, 0), (0, LANES - N_EXPERTS)))
    br = jnp.pad(b_router[0], (0, LANES - N_EXPERTS), constant_values=NEG_INF)
    x1, h2, logits = _merge(x, ya, yb, ga, gb, mod3,
                            w_branch_a[0].astype(BF16), w_branch_b[0].astype(BF16),
                            w_out[0].astype(BF16), row(ln1_g[0]), row(ln1_b[0]), wr, row(br))

    tm = TM_MOE
    n_tiles = -(-(n_tok * TOP_K) // tm) + N_EXPERTS
    assert n_tiles <= MAX_TILES_PAD
    pos, wts, tmap, nv = _route(logits.reshape(n_tok, LANES), tm)
    pos_flat = pos[:, :TOP_K].reshape(-1)
    xs = _dispatch(pos_flat, h2.reshape(n_tok, d), n_tiles * tm)
    y = _moe(tmap[:n_tiles, 0], nv[0, :1], xs, w_gate[0], b_gate[0], w_up[0], b_up[0],
             w_down[0], b_down[0], tm)
    out = _combine(pos_flat, wts, x1.reshape(n_tok, d), mod3, row(ln2_g[0]), row(ln2_b[0]), y, seq)
    return out.reshape(bsz, seq, d)
```

```python
import functools

import jax
import jax.numpy as jnp
from jax import lax
from jax.experimental import pallas as pl
from jax.experimental.pallas import tpu as pltpu

F32 = jnp.float32
BF16 = jnp.bfloat16

CHUNK = 64
A_HEADS = 8
A_HEAD_DIM = 64
A_LEFT_CHUNKS = 8
A_MAX_REL = 128
A_WIDTH = A_HEADS * A_HEAD_DIM
B_HEADS = 8
B_NOPE_DIM = 64
B_ROPE_DIM = 32
B_V_DIM = 64
B_Q_LORA = 384
B_KV_LORA = 256
B_QK_DIM = B_NOPE_DIM + B_ROPE_DIM
ROPE_THETA = 10000.0
N_EXPERTS = 32
TOP_K = 4
SWIGLU_LIMIT = 7.0
SWIGLU_ALPHA = 1.702
DEPTH = 1
DEEPNORM_ALPHA = (2.0 * DEPTH) ** 0.25
LN_EPS = 1e-5
RMS_EPS = 1e-6
NEG_INF = -1e30
LOG2_E = 1.4426950408889634

LANES = 128
MXU_COLS = 256
V_PAIR_LANES = MXU_COLS

TM_PROJ = 512
TQ_A = 256
TQ_B = 512
TM_MERGE = 512
TR_ROUTE = 1024
TM_MOE = 256
TM_ROWS = 256
MAX_TILES_PAD = 512


def _params(sem, vmem_mb=None):
    return pltpu.CompilerParams(
        dimension_semantics=sem,
        vmem_limit_bytes=None if vmem_mb is None else vmem_mb << 20)


def _dot(a, b):
    return jnp.dot(a, b, preferred_element_type=F32)


def _dot_nt(a, b):
    return lax.dot_general(a, b, (((1,), (1,)), ((), ())), preferred_element_type=F32)


def _layer_norm(x, g, b):
    mu = jnp.mean(x, axis=-1, keepdims=True)
    xc = x - mu
    var = jnp.mean(xc * xc, axis=-1, keepdims=True)
    return xc * lax.rsqrt(var + LN_EPS) * g + b


def _ada_kernel(c_ref, w_ref, b_ref, o_ref):
    c = c_ref[...]
    sc = c * jax.nn.sigmoid(c)
    o_ref[...] = jnp.dot(sc, w_ref[...], preferred_element_type=F32,
                         precision=lax.Precision.HIGHEST) + b_ref[...]


def _ada(c, w_ada, b_ada):
    bsz, d = c.shape
    m = w_ada.shape[1]
    tn = 1024
    return pl.pallas_call(
        _ada_kernel,
        out_shape=jax.ShapeDtypeStruct((bsz, m), F32),
        grid=(m // tn,),
        in_specs=[pl.BlockSpec((bsz, d), lambda j: (0, 0)),
                  pl.BlockSpec((d, tn), lambda j: (0, j)),
                  pl.BlockSpec((1, tn), lambda j: (0, j))],
        out_specs=pl.BlockSpec((bsz, tn), lambda j: (0, j)),
        compiler_params=_params(("arbitrary",)),
    )(c, w_ada, b_ada.reshape(1, m))


def _inproj_kernel(x_ref, mod_ref, w_ref, qkv_ref, cq_ref, ckv_ref, kr_ref, ga_ref, gb_ref):
    m = mod_ref[0]
    h = (x_ref[0] * (1.0 + m[1:2]) + m[0:1]).astype(BF16)
    col = 0
    for ref in (qkv_ref, cq_ref, ckv_ref, kr_ref, ga_ref, gb_ref):
        n = ref.shape[-1]
        ref[0] = _dot(h, w_ref[:, col:col + n]).astype(ref.dtype)
        col += n


def _inproj(x, mod3, w):
    bsz, seq, d = x.shape
    tm = min(TM_PROJ, seq)
    widths = (3 * A_WIDTH, B_Q_LORA, B_KV_LORA, LANES, d, d)
    dtypes = (BF16, BF16, BF16, F32, BF16, BF16)
    assert sum(widths) == w.shape[1]
    return pl.pallas_call(
        _inproj_kernel,
        out_shape=[jax.ShapeDtypeStruct((bsz, seq, n), dt) for n, dt in zip(widths, dtypes)],
        grid=(bsz, seq // tm),
        in_specs=[pl.BlockSpec((1, tm, d), lambda b, i: (b, i, 0)),
                  pl.BlockSpec((1, 6, d), lambda b, i: (b, 0, 0)),
                  pl.BlockSpec(w.shape, lambda b, i: (0, 0))],
        out_specs=[pl.BlockSpec((1, tm, n), lambda b, i: (b, i, 0)) for n in widths],
        compiler_params=_params(("arbitrary", "arbitrary"), 48),
    )(x, mod3, w)


def _attn_a_kernel(q_ref, k_ref, v_ref, bias_ref, o_ref, *, tq, nd):
    i = pl.program_id(2)
    q = q_ref[0]
    lane = lax.broadcasted_iota(jnp.int32, (1, LANES), 1)
    outs = []
    for hh in range(2):
        in_head = (lane // A_HEAD_DIM) == hh
        scores, values = [], []
        for d in range(nd):
            start = pl.multiple_of(jnp.maximum(i - d, 0) * tq, tq)
            k = k_ref[0, pl.ds(start, tq), :]
            kh = jnp.where(in_head, k, jnp.zeros_like(k))
            s = _dot_nt(q, kh) + bias_ref[hh, :, (nd - 1 - d) * tq:(nd - d) * tq]
            if d > 0:
                s = jnp.where(i >= d, s, NEG_INF)
            scores.append(s)
            values.append(v_ref[0, pl.ds(start, tq), :])
        m = scores[0].max(axis=-1, keepdims=True)
        for s in scores[1:]:
            m = jnp.maximum(m, s.max(axis=-1, keepdims=True))
        l = jnp.zeros_like(m)
        acc = jnp.zeros((tq, LANES), F32)
        for s, v in zip(scores, values):
            p = jnp.exp(s - m)
            l = l + p.sum(axis=-1, keepdims=True)
            acc = acc + _dot(p.astype(BF16), v)
        outs.append(acc / l)
    o_ref[0] = jnp.where(lane < A_HEAD_DIM, outs[0], outs[1]).astype(o_ref.dtype)


def _attn_a(qkv, bias_tab):
    bsz, seq, _ = qkv.shape
    tq = min(TQ_A, seq)
    nd = bias_tab.shape[2] // tq
    ncb = A_WIDTH // LANES
    return pl.pallas_call(
        functools.partial(_attn_a_kernel, tq=tq, nd=nd),
        out_shape=jax.ShapeDtypeStruct((bsz, seq, A_WIDTH), BF16),
        grid=(ncb, bsz, seq // tq),
        in_specs=[pl.BlockSpec((1, tq, LANES), lambda hp, b, i: (b, i, hp)),
                  pl.BlockSpec((1, seq, LANES), lambda hp, b, i: (b, 0, ncb + hp)),
                  pl.BlockSpec((1, seq, LANES), lambda hp, b, i: (b, 0, 2 * ncb + hp)),
                  pl.BlockSpec((2, tq, nd * tq), lambda hp, b, i: (hp, 0, 0))],
        out_specs=pl.BlockSpec((1, tq, LANES), lambda hp, b, i: (b, i, hp)),
        compiler_params=_params(("arbitrary",) * 3, 32),
    )(qkv, qkv, qkv, bias_tab)


def _bias_table(rel_bias, tq):
    nd = (A_LEFT_CHUNKS * CHUNK + tq - 1) // tq + 1
    width = nd * tq
    period = width + tq
    u = jnp.arange(period)
    dist = jnp.where(u < width, (nd - 1) * tq - u, (nd - 1) * tq + period - u)
    vec = rel_bias[:, jnp.clip(dist, -A_MAX_REL, A_MAX_REL) + A_MAX_REL].astype(F32)
    toep = jnp.tile(vec, (1, tq))[:, :tq * (period - 1)].reshape(-1, tq, period - 1)[:, :, :width]
    qi = jnp.arange(tq)[:, None]
    x = jnp.arange(width)[None, :]
    dchunk = qi // CHUNK - x // CHUNK + (nd - 1) * (tq // CHUNK)
    valid = (dchunk >= 0) & (dchunk <= A_LEFT_CHUNKS)
    return jnp.where(valid[None], toep, NEG_INF)


def _mla_proj_kernel(cq_ref, ckv_ref, kr_ref, cos_ref, sa_ref, sb_ref, rq_ref, rkv_ref,
                     wq_ref, wk_ref, wv_ref, q_out, k_out, v_out):
    def rms(x, g):
        xf = x.astype(F32)
        ms = jnp.mean(xf * xf, axis=-1, keepdims=True)
        return (xf * lax.rsqrt(ms + RMS_EPS) * g).astype(BF16)

    cqn = rms(cq_ref[0], rq_ref[...])
    ckvn = rms(ckv_ref[0], rkv_ref[...])
    q = _dot(cqn, wq_ref[...])
    kn = _dot(ckvn, wk_ref[...])
    vlane = lax.broadcasted_iota(jnp.int32, (1, v_out.shape[-1]), 1)
    ones_lane = jnp.where(vlane % V_PAIR_LANES == 2 * B_V_DIM, 1.0, 0.0)
    v_out[0] = (_dot(ckvn, wv_ref[...]) + ones_lane).astype(v_out.dtype)
    cos_t, sin_a, sin_b = cos_ref[0], sa_ref[0], sb_ref[0]

    def rope(x):
        return (x * cos_t + pltpu.roll(x, LANES - B_ROPE_DIM // 2, 1) * sin_a
                + pltpu.roll(x, B_ROPE_DIM // 2, 1) * sin_b)

    kpe = rope(kr_ref[0])
    scale = B_QK_DIM ** -0.5 * LOG2_E
    for h in range(B_HEADS):
        sl = slice(h * LANES, (h + 1) * LANES)
        q_out[0, :, sl] = (rope(q[:, sl]) * scale).astype(q_out.dtype)
        k_out[0, :, sl] = (kn[:, sl] + kpe).astype(k_out.dtype)


def _mla_proj(cq, ckv, kr, cos_t, sin_a, sin_b, rms_q, rms_kv, wq, wk, wv):
    bsz, seq, _ = cq.shape
    tm = min(TM_PROJ, seq)
    tok = lambda n: pl.BlockSpec((1, tm, n), lambda b, i: (b, i, 0))
    full = lambda a: pl.BlockSpec(a.shape, lambda b, i: (0,) * a.ndim)
    return pl.pallas_call(
        _mla_proj_kernel,
        out_shape=[jax.ShapeDtypeStruct((bsz, seq, B_HEADS * LANES), BF16),
                   jax.ShapeDtypeStruct((bsz, seq, B_HEADS * LANES), BF16),
                   jax.ShapeDtypeStruct((bsz, seq, wv.shape[1]), BF16)],
        grid=(bsz, seq // tm),
        in_specs=[tok(B_Q_LORA), tok(B_KV_LORA), tok(LANES), tok(LANES), tok(LANES), tok(LANES),
                  full(rms_q), full(rms_kv), full(wq), full(wk), full(wv)],
        out_specs=[tok(B_HEADS * LANES), tok(B_HEADS * LANES), tok(wv.shape[1])],
        compiler_params=_params(("arbitrary", "arbitrary"), 32),
    )(cq, ckv, kr, cos_t, sin_a, sin_b, rms_q, rms_kv, wq, wk, wv)


def _mla_attn_kernel(q_ref, k_ref, v_ref, o_ref, s_a, s_b, m_s, acc_s, *, t):
    i = pl.program_id(2)
    lane = lax.broadcasted_iota(jnp.int32, (1, LANES), 1)
    m_s[...] = jnp.full(m_s.shape, NEG_INF, F32)
    acc_s[...] = jnp.zeros(acc_s.shape, F32)

    def produce(kj, s_ref):
        start = pl.multiple_of(kj * t, t)
        for hh in range(2):
            q = q_ref[0, :, hh * LANES:(hh + 1) * LANES]
            k = k_ref[0, pl.ds(start, t), hh * LANES:(hh + 1) * LANES]
            s_ref[hh] = _dot_nt(q, k)

    def consume(kj, s_ref, masked):
        start = pl.multiple_of(kj * t, t)
        v = v_ref[0, pl.ds(start, t), :]
        for hh in range(2):
            s = s_ref[hh]
            if masked:
                row_chunk = lax.broadcasted_iota(jnp.int32, (t, t), 0) // CHUNK
                col_chunk = lax.broadcasted_iota(jnp.int32, (t, t), 1) // CHUNK
                s = jnp.where(col_chunk <= row_chunk, s, NEG_INF)
            m_old = m_s[hh]
            m_new = jnp.maximum(m_old, jnp.broadcast_to(s.max(axis=-1, keepdims=True), m_old.shape))
            p = jnp.exp2(s - jnp.tile(m_new, (1, t // LANES))).astype(BF16)
            alpha = jnp.exp2(m_old - m_new)
            acc_s[hh] = jnp.tile(alpha, (1, V_PAIR_LANES // LANES)) * acc_s[hh] + _dot(p, v)
            m_s[hh] = m_new

    def body(pair, c):
        kj = 2 * pair
        produce(kj + 1, s_b)
        consume(kj, s_a, False)
        produce(kj + 2, s_a)
        consume(kj + 1, s_b, False)
        return c

    produce(0, s_a)
    lax.fori_loop(0, i // 2, body, 0)

    @pl.when(i % 2 == 0)
    def _():
        consume(i, s_a, True)

    @pl.when(i % 2 == 1)
    def _():
        produce(i, s_b)
        consume(i - 1, s_a, False)
        consume(i, s_b, True)

    outs = []
    for hh in range(2):
        acc = acc_s[hh]
        outs.append(acc[:, :LANES] / acc[:, 2 * B_V_DIM:2 * B_V_DIM + 1])
    o_ref[0] = jnp.where(lane < B_V_DIM, outs[0], outs[1]).astype(o_ref.dtype)


def _mla_attn(qm, km, vm):
    bsz, seq, _ = qm.shape
    t = min(TQ_B, seq)
    return pl.pallas_call(
        functools.partial(_mla_attn_kernel, t=t),
        out_shape=jax.ShapeDtypeStruct((bsz, seq, B_HEADS * B_V_DIM), BF16),
        grid=(bsz, B_HEADS // 2, seq // t),
        in_specs=[pl.BlockSpec((1, t, 2 * LANES), lambda b, hp, i: (b, i, hp)),
                  pl.BlockSpec((1, seq, 2 * LANES), lambda b, hp, i: (b, 0, hp)),
                  pl.BlockSpec((1, seq, V_PAIR_LANES), lambda b, hp, i: (b, 0, hp))],
        out_specs=pl.BlockSpec((1, t, LANES), lambda b, hp, i: (b, i, hp)),
        scratch_shapes=[pltpu.VMEM((2, t, t), F32), pltpu.VMEM((2, t, t), F32),
                        pltpu.VMEM((2, t, LANES), F32),
                        pltpu.VMEM((2, t, V_PAIR_LANES), F32)],
        compiler_params=_params(("arbitrary",) * 3, 40),
    )(qm, km, vm)


def _merge_kernel(x_ref, ya_ref, yb_ref, ga_ref, gb_ref, mod_ref, wa_ref, wb_ref, wo_ref,
                  g1_ref, b1_ref, wrh_ref, wrl_ref, br_ref, x1_ref, h2_ref, lg_ref):
    m = mod_ref[0]
    a = _dot(ya_ref[0], wa_ref[...])
    b = _dot(yb_ref[0], wb_ref[...])
    merged = (jax.nn.sigmoid(ga_ref[0].astype(F32)) * a
              + jax.nn.sigmoid(gb_ref[0].astype(F32)) * b)
    o = _dot(merged.astype(BF16), wo_ref[...])
    x1 = _layer_norm(DEEPNORM_ALPHA * x_ref[0] + m[2:3] * o, g1_ref[...], b1_ref[...])
    h2 = x1 * (1.0 + m[4:5]) + m[3:4]
    x1_ref[0] = x1
    h2_ref[0] = h2
    h_hi = h2.astype(BF16)
    h_lo = (h2 - h_hi.astype(F32)).astype(BF16)
    lg_ref[0] = (_dot(h_hi, wrh_ref[...]) + _dot(h_hi, wrl_ref[...]) + _dot(h_lo, wrh_ref[...])
                 + br_ref[...])


def _merge(x, ya, yb, ga, gb, mod3, wa, wb, wo, g1, b1, wr, br):
    wr_hi = wr.astype(BF16)
    wr_lo = (wr - wr_hi.astype(F32)).astype(BF16)
    bsz, seq, d = x.shape
    tm = min(TM_MERGE, seq)
    tok = lambda n: pl.BlockSpec((1, tm, n), lambda b, i: (b, i, 0))
    full = lambda a: pl.BlockSpec(a.shape, lambda b, i: (0,) * a.ndim)
    return pl.pallas_call(
        _merge_kernel,
        out_shape=[jax.ShapeDtypeStruct((bsz, seq, d), F32),
                   jax.ShapeDtypeStruct((bsz, seq, d), F32),
                   jax.ShapeDtypeStruct((bsz, seq, LANES), F32)],
        grid=(bsz, seq // tm),
        in_specs=[tok(d), tok(A_WIDTH), tok(B_HEADS * B_V_DIM), tok(d), tok(d),
                  pl.BlockSpec((1, 6, d), lambda b, i: (b, 0, 0)),
                  full(wa), full(wb), full(wo), full(g1), full(b1), full(wr_hi), full(wr_lo),
                  full(br)],
        out_specs=[tok(d), tok(d), tok(LANES)],
        compiler_params=_params(("arbitrary", "arbitrary"), 48),
    )(x, ya, yb, ga, gb, mod3, wa, wb, wo, g1, b1, wr_hi, wr_lo, br)


def _route_kernel(lg_ref, pos_ref, w_ref, tmap_ref, nv_ref, cnt_s, carry_s, pstart_s, *, tr, tm):
    phase = pl.program_id(0)
    i = pl.program_id(1)
    lane = lax.broadcasted_iota(jnp.int32, (tr, LANES), 1)
    work = lg_ref[...]
    sels, vals = [], []
    for _ in range(TOP_K):
        m = work.max(axis=-1, keepdims=True)
        idx = jnp.where(work == m, lane, LANES).min(axis=-1, keepdims=True)
        sel = lane == idx
        work = jnp.where(sel, -jnp.inf, work)
        sels.append(sel)
        vals.append(m)
    picked = sum(jnp.where(s, 1.0, 0.0) for s in sels)
    tile_count = jnp.broadcast_to(picked.sum(axis=0, keepdims=True), (8, LANES))

    @pl.when((phase == 0) & (i == 0))
    def _():
        cnt_s[...] = jnp.zeros_like(cnt_s)

    @pl.when(phase == 0)
    def _():
        cnt_s[...] += tile_count

    @pl.when((phase == 1) & (i == 0))
    def _():
        lane8 = lax.broadcasted_iota(jnp.int32, (8, LANES), 1)
        padded = jnp.floor((cnt_s[...] + (tm - 1)) * (1.0 / tm)) * tm
        cum = padded
        shift = 1
        while shift < N_EXPERTS:
            cum = cum + jnp.where(lane8 >= shift, pltpu.roll(cum, shift, 1), 0.0)
            shift *= 2
        pstart_s[...] = cum - padded
        carry_s[...] = jnp.zeros_like(carry_s)
        nt = tmap_ref.shape[0]
        tile_start = (lax.broadcasted_iota(jnp.int32, (nt, LANES), 0) * tm).astype(F32)
        lane_t = lax.broadcasted_iota(jnp.int32, (nt, LANES), 1)
        done = jnp.where((lane_t < N_EXPERTS) & (cum[0:1] <= tile_start), 1.0, 0.0)
        te = jnp.minimum(done.sum(axis=-1, keepdims=True), N_EXPERTS - 1.0)
        tmap_ref[...] = jnp.broadcast_to(te, (nt, LANES)).astype(jnp.int32)
        total = jnp.where(lane8 == N_EXPERTS - 1, cum, 0.0).sum(axis=-1, keepdims=True)
        nv_ref[...] = jnp.broadcast_to(total * (1.0 / tm), (8, LANES)).astype(jnp.int32)

    @pl.when(phase == 1)
    def _():
        r = lax.broadcasted_iota(jnp.int32, (tr, tr), 0)
        c = lax.broadcasted_iota(jnp.int32, (tr, tr), 1)
        tri = jnp.where(c < r, 1.0, 0.0).astype(BF16)
        before = _dot(tri, picked.astype(BF16)) + carry_s[0:1]
        base = before + pstart_s[0:1]
        carry_s[...] += tile_count
        den = sum(jnp.exp(v - vals[0]) for v in vals)
        pos = jnp.zeros((tr, LANES), F32)
        wts = jnp.zeros((tr, LANES), F32)
        for k in range(TOP_K):
            pk = jnp.where(sels[k], base, 0.0).sum(axis=-1, keepdims=True)
            pos = jnp.where(lane == k, pk, pos)
            wts = jnp.where(lane == k, jnp.exp(vals[k] - vals[0]) / den, wts)
        pos_ref[...] = pos.astype(jnp.int32)
        w_ref[...] = wts


def _route(logits, tm):
    n = logits.shape[0]
    tr = min(TR_ROUTE, n)
    return pl.pallas_call(
        functools.partial(_route_kernel, tr=tr, tm=tm),
        out_shape=[jax.ShapeDtypeStruct((n, LANES), jnp.int32),
                   jax.ShapeDtypeStruct((n, LANES), F32),
                   jax.ShapeDtypeStruct((MAX_TILES_PAD, LANES), jnp.int32),
                   jax.ShapeDtypeStruct((8, LANES), jnp.int32)],
        grid=(2, n // tr),
        in_specs=[pl.BlockSpec((tr, LANES), lambda p, i: (i, 0))],
        out_specs=[pl.BlockSpec((tr, LANES), lambda p, i: (i * p, 0)),
                   pl.BlockSpec((tr, LANES), lambda p, i: (i * p, 0)),
                   pl.BlockSpec((MAX_TILES_PAD, LANES), lambda p, i: (0, 0)),
                   pl.BlockSpec((8, LANES), lambda p, i: (0, 0))],
        scratch_shapes=[pltpu.VMEM((8, LANES), F32)] * 3,
        compiler_params=_params(("arbitrary", "arbitrary")),
    )(logits)


def _dispatch_kernel(pos_ref, h_ref, xs_in_ref, xs_ref, sem, *, tm):
    del xs_in_ref

    def row_copy(r, k):
        return pltpu.make_async_copy(h_ref.at[pl.ds(r, 1)],
                                     xs_ref.at[pl.ds(pos_ref[r * TOP_K + k], 1)], sem)

    def issue(r, c):
        for k in range(TOP_K):
            row_copy(r, k).start()
        return c

    lax.fori_loop(0, tm, issue, 0)
    for _ in range(TOP_K):
        pltpu.make_async_copy(h_ref, xs_ref.at[pl.ds(0, tm)], sem).wait()


def _dispatch(pos_flat, h2, cap):
    n, d = h2.shape
    tm = min(TM_ROWS, n)
    return pl.pallas_call(
        functools.partial(_dispatch_kernel, tm=tm),
        out_shape=jax.ShapeDtypeStruct((cap, d), F32),
        grid=(n // tm,),
        in_specs=[pl.BlockSpec((tm * TOP_K,), lambda i: (i,), memory_space=pltpu.SMEM),
                  pl.BlockSpec((tm, d), lambda i: (i, 0)),
                  pl.BlockSpec(memory_space=pl.ANY)],
        out_specs=pl.BlockSpec(memory_space=pl.ANY),
        scratch_shapes=[pltpu.SemaphoreType.DMA(())],
        input_output_aliases={2: 0},
        compiler_params=_params(("arbitrary",)),
    )(pos_flat, h2, jnp.zeros((cap, d), F32))


def _moe_kernel(te_ref, nv_ref, x_ref, wg_ref, bg_ref, wu_ref, bu_ref, wd_ref, bd_ref,
                y_ref, wg_s, wu_s, wd_s):
    i = pl.program_id(0)

    @pl.when(i < nv_ref[0])
    def _():
        changed = (i == 0) | (te_ref[i] != te_ref[jnp.maximum(i - 1, 0)])

        @pl.when(changed)
        def _():
            wg_s[...] = wg_ref[0].astype(BF16)
            wu_s[...] = wu_ref[0].astype(BF16)
            wd_s[...] = wd_ref[0].astype(BF16)

        xb = x_ref[...].astype(BF16)
        g = _dot(xb, wg_s[...]) + bg_ref[0]
        u = _dot(xb, wu_s[...]) + bu_ref[0]
        g = jnp.minimum(g, SWIGLU_LIMIT)
        u = jnp.clip(u, -SWIGLU_LIMIT, SWIGLU_LIMIT)
        h = (u + 1.0) * (g * jax.nn.sigmoid(SWIGLU_ALPHA * g))
        y_ref[...] = _dot(h.astype(BF16), wd_s[...]) + bd_ref[0]

    @pl.when(i >= nv_ref[0])
    def _():
        y_ref[...] = jnp.zeros_like(y_ref)


def _moe(tile_expert, n_valid, xs, w_gate, b_gate, w_up, b_up, w_down, b_down, tm):
    cap, d = xs.shape
    e, _, f = w_gate.shape
    n_tiles = cap // tm

    def tile(i, te, nv):
        return jnp.minimum(i, nv[0] - 1)

    row = lambda i, te, nv: (tile(i, te, nv), 0)
    exp3 = lambda i, te, nv: (te[tile(i, te, nv)], 0, 0)
    grid_spec = pltpu.PrefetchScalarGridSpec(
        num_scalar_prefetch=2,
        grid=(n_tiles,),
        in_specs=[pl.BlockSpec((tm, d), row),
                  pl.BlockSpec((1, d, f), exp3), pl.BlockSpec((1, 1, f), exp3),
                  pl.BlockSpec((1, d, f), exp3), pl.BlockSpec((1, 1, f), exp3),
                  pl.BlockSpec((1, f, d), exp3), pl.BlockSpec((1, 1, d), exp3)],
        out_specs=pl.BlockSpec((tm, d), lambda i, te, nv: (i, 0)),
        scratch_shapes=[pltpu.VMEM((d, f), BF16), pltpu.VMEM((d, f), BF16), pltpu.VMEM((f, d), BF16)],
    )
    return pl.pallas_call(
        _moe_kernel,
        out_shape=jax.ShapeDtypeStruct((cap, d), F32),
        grid_spec=grid_spec,
        compiler_params=_params(("arbitrary",), 56),
    )(tile_expert, n_valid, xs, w_gate, b_gate.reshape(e, 1, f), w_up, b_up.reshape(e, 1, f),
      w_down, b_down.reshape(e, 1, d))


def _combine_kernel(pos_ref, posn_ref, w_ref, x1_ref, mod_ref, g_ref, b_ref, y_ref, o_ref,
                    buf, sem, *, tm):
    i = pl.program_id(0)
    n = pl.num_programs(0)

    def issue(p_ref, slot):
        def body(r, c):
            for k in range(TOP_K):
                pltpu.make_async_copy(y_ref.at[pl.ds(p_ref[r * TOP_K + k], 1)],
                                      buf.at[slot, k, pl.ds(r, 1)], sem.at[slot]).start()
            return c
        lax.fori_loop(0, tm, body, 0)

    @pl.when(i == 0)
    def _():
        issue(pos_ref, 0)

    @pl.when(i + 1 < n)
    def _():
        issue(posn_ref, (i + 1) % 2)

    slot = i % 2
    for k in range(TOP_K):
        pltpu.make_async_copy(y_ref.at[pl.ds(0, tm)], buf.at[slot, k], sem.at[slot]).wait()
    w = w_ref[...]
    f = jnp.zeros(x1_ref.shape, F32)
    for k in range(TOP_K):
        f = f + w[:, k:k + 1] * buf[slot, k]
    m = mod_ref[0]
    o_ref[...] = _layer_norm(DEEPNORM_ALPHA * x1_ref[...] + m[5:6] * f, g_ref[...], b_ref[...])


def _combine(pos_flat, w, x1, mod3, g2, b2, y, seq):
    n, d = x1.shape
    tm = min(TM_ROWS, seq)
    nsteps = n // tm
    per_seq = seq // tm
    return pl.pallas_call(
        functools.partial(_combine_kernel, tm=tm),
        out_shape=jax.ShapeDtypeStruct((n, d), F32),
        grid=(nsteps,),
        in_specs=[pl.BlockSpec((tm * TOP_K,), lambda i: (i,), memory_space=pltpu.SMEM),
                  pl.BlockSpec((tm * TOP_K,), lambda i: (jnp.minimum(i + 1, nsteps - 1),),
                               memory_space=pltpu.SMEM),
                  pl.BlockSpec((tm, LANES), lambda i: (i, 0)),
                  pl.BlockSpec((tm, d), lambda i: (i, 0)),
                  pl.BlockSpec((1, 6, d), lambda i: (i // per_seq, 0, 0)),
                  pl.BlockSpec(g2.shape, lambda i: (0, 0)),
                  pl.BlockSpec(b2.shape, lambda i: (0, 0)),
                  pl.BlockSpec(memory_space=pl.ANY)],
        out_specs=pl.BlockSpec((tm, d), lambda i: (i, 0)),
        scratch_shapes=[pltpu.VMEM((2, TOP_K, tm, d), F32), pltpu.SemaphoreType.DMA((2,))],
        compiler_params=_params(("arbitrary",), 32),
    )(pos_flat, pos_flat, w, x1, mod3, g2, b2, y)


def _prep_w_in(w_in):
    d = w_in.shape[0]
    a3 = 3 * A_WIDTH
    lat = a3 + B_Q_LORA + B_KV_LORA
    z = lambda n: jnp.zeros((d, n), w_in.dtype)
    return jnp.concatenate([
        w_in[:, :A_WIDTH] * (A_HEAD_DIM ** -0.5),
        w_in[:, A_WIDTH:lat],
        z(B_NOPE_DIM), w_in[:, lat:lat + B_ROPE_DIM], z(LANES - B_NOPE_DIM - B_ROPE_DIM),
        w_in[:, lat + B_ROPE_DIM:]], axis=1).astype(BF16)


def _prep_w_uq(w_uq):
    r = w_uq.shape[0]
    w = w_uq.reshape(r, B_HEADS, B_QK_DIM)
    w = jnp.pad(w, ((0, 0), (0, 0), (0, LANES - B_QK_DIM)))
    return w.reshape(r, B_HEADS * LANES).astype(BF16)


def _prep_w_ukv(w_ukv):
    r = w_ukv.shape[0]
    w = w_ukv.reshape(r, B_HEADS, B_NOPE_DIM + B_V_DIM)
    wk = jnp.pad(w[:, :, :B_NOPE_DIM], ((0, 0), (0, 0), (0, LANES - B_NOPE_DIM)))
    wv = w[:, :, B_NOPE_DIM:].reshape(r, B_HEADS // 2, 2 * B_V_DIM)
    wv = jnp.pad(wv, ((0, 0), (0, 0), (0, V_PAIR_LANES - 2 * B_V_DIM)))
    return (wk.reshape(r, B_HEADS * LANES).astype(BF16),
            wv.reshape(r, B_HEADS // 2 * V_PAIR_LANES).astype(BF16))


def _rope_tables(positions):
    half = B_ROPE_DIM // 2
    freqs = ROPE_THETA ** (-jnp.arange(half, dtype=F32) / half)
    ang = positions.astype(F32)[:, :, None] * freqs
    cos, sin = jnp.cos(ang), jnp.sin(ang)
    z = lambda n: jnp.zeros(ang.shape[:2] + (n,), F32)
    tail = LANES - B_NOPE_DIM - B_ROPE_DIM
    cos_t = jnp.concatenate([jnp.ones(ang.shape[:2] + (B_NOPE_DIM,), F32), cos, cos, z(tail)], -1)
    sin_a = jnp.concatenate([z(B_NOPE_DIM), -sin, z(half), z(tail)], -1)
    sin_b = jnp.concatenate([z(B_NOPE_DIM), z(half), sin, z(tail)], -1)
    return cos_t, sin_a, sin_b


def kernel(x, c, positions, w_ada, b_ada, w_in, rms_q, w_uq, rms_kv, w_ukv, rel_bias,
           w_branch_a, w_branch_b, w_out, ln1_g, ln1_b, w_router, b_router,
           w_gate, b_gate, w_up, b_up, w_down, b_down, ln2_g, ln2_b):
    bsz, seq, d = x.shape
    n_tok = bsz * seq
    assert w_ada.shape[0] == DEPTH == 1
    row = lambda v: v.reshape(1, -1)

    mod3 = _ada(c, w_ada[0], b_ada[0]).reshape(bsz, 6, d)

    qkv, cq, ckv, kr, ga, gb = _inproj(x, mod3, _prep_w_in(w_in[0]))
    ya = _attn_a(qkv, _bias_table(rel_bias[0], min(TQ_A, seq)))
    wk, wv = _prep_w_ukv(w_ukv[0])
    qm, km, vm = _mla_proj(cq, ckv, kr, *_rope_tables(positions), row(rms_q[0]), row(rms_kv[0]),
                           _prep_w_uq(w_uq[0]), wk, wv)
    yb = _mla_attn(qm, km, vm)

    wr = jnp.pad(w_router[0], ((0, 0), (0, LANES - N_EXPERTS)))
    br = jnp.pad(b_router[0], (0, LANES - N_EXPERTS), constant_values=NEG_INF)
    x1, h2, logits = _merge(x, ya, yb, ga, gb, mod3,
                            w_branch_a[0].astype(BF16), w_branch_b[0].astype(BF16),
                            w_out[0].astype(BF16), row(ln1_g[0]), row(ln1_b[0]), wr, row(br))

    tm = TM_MOE
    n_tiles = -(-(n_tok * TOP_K) // tm) + N_EXPERTS
    assert n_tiles <= MAX_TILES_PAD
    pos, wts, tmap, nv = _route(logits.reshape(n_tok, LANES), tm)
    pos_flat = pos[:, :TOP_K].reshape(-1)
    xs = _dispatch(pos_flat, h2.reshape(n_tok, d), n_tiles * tm)
    y = _moe(tmap[:n_tiles, 0], nv[0, :1], xs, w_gate[0], b_gate[0], w_up[0], b_up[0],
             w_down[0], b_down[0], tm)
    out = _combine(pos_flat, wts, x1.reshape(n_tok, d), mod3, row(ln2_g[0]), row(ln2_b[0]), y, seq)
    return out.reshape(bsz, seq, d)
```

```python
import functools

import jax
import jax.numpy as jnp
from jax import lax
from jax.experimental import pallas as pl
from jax.experimental.pallas import tpu as pltpu

F32 = jnp.float32
BF16 = jnp.bfloat16

CHUNK = 64
A_HEADS = 8
A_HEAD_DIM = 64
A_LEFT_CHUNKS = 8
A_MAX_REL = 128
A_WIDTH = A_HEADS * A_HEAD_DIM
B_HEADS = 8
B_NOPE_DIM = 64
B_ROPE_DIM = 32
B_V_DIM = 64
B_Q_LORA = 384
B_KV_LORA = 256
B_QK_DIM = B_NOPE_DIM + B_ROPE_DIM
ROPE_THETA = 10000.0
N_EXPERTS = 32
TOP_K = 4
SWIGLU_LIMIT = 7.0
SWIGLU_ALPHA = 1.702
DEPTH = 1
DEEPNORM_ALPHA = (2.0 * DEPTH) ** 0.25
LN_EPS = 1e-5
RMS_EPS = 1e-6
NEG_INF = -1e30
LOG2_E = 1.4426950408889634

LANES = 128
MXU_COLS = 256
V_PAIR_LANES = MXU_COLS

TM_PROJ = 512
TQ_A = 256
TQ_B = 512
TM_MERGE = 512
TR_ROUTE = 1024
TM_MOE = 256
TM_ROWS = 256
MAX_TILES_PAD = 512


def _params(sem, vmem_mb=None):
    return pltpu.CompilerParams(
        dimension_semantics=sem,
        vmem_limit_bytes=None if vmem_mb is None else vmem_mb << 20)


def _dot(a, b):
    return jnp.dot(a, b, preferred_element_type=F32)


def _dot_nt(a, b):
    return lax.dot_general(a, b, (((1,), (1,)), ((), ())), preferred_element_type=F32)


def _store_token_tiles(ref, x):
    rows, d = x.shape
    nsub = d // LANES
    for s in range(nsub):
        ref[pl.ds(s, rows, stride=nsub), :] = x[:, s * LANES:(s + 1) * LANES]


def _load_token_tiles(ref, nsub):
    rows = ref.shape[0] // nsub
    return jnp.concatenate([ref[pl.ds(s, rows, stride=nsub), :] for s in range(nsub)], axis=1)


def _layer_norm(x, g, b):
    mu = jnp.mean(x, axis=-1, keepdims=True)
    xc = x - mu
    var = jnp.mean(xc * xc, axis=-1, keepdims=True)
    return xc * lax.rsqrt(var + LN_EPS) * g + b


def _ada_kernel(c_ref, w_ref, b_ref, o_ref):
    c = c_ref[...]
    sc = c * jax.nn.sigmoid(c)
    o_ref[...] = jnp.dot(sc, w_ref[...], preferred_element_type=F32,
                         precision=lax.Precision.HIGHEST) + b_ref[...]


def _ada(c, w_ada, b_ada):
    bsz, d = c.shape
    m = w_ada.shape[1]
    tn = 1024
    return pl.pallas_call(
        _ada_kernel,
        out_shape=jax.ShapeDtypeStruct((bsz, m), F32),
        grid=(m // tn,),
        in_specs=[pl.BlockSpec((bsz, d), lambda j: (0, 0)),
                  pl.BlockSpec((d, tn), lambda j: (0, j)),
                  pl.BlockSpec((1, tn), lambda j: (0, j))],
        out_specs=pl.BlockSpec((bsz, tn), lambda j: (0, j)),
        compiler_params=_params(("arbitrary",)),
    )(c, w_ada, b_ada.reshape(1, m))


def _inproj_kernel(x_ref, mod_ref, w_ref, qkv_ref, cq_ref, ckv_ref, kr_ref, ga_ref, gb_ref):
    m = mod_ref[0]
    h = (x_ref[0] * (1.0 + m[1:2]) + m[0:1]).astype(BF16)
    col = 0
    for ref in (qkv_ref, cq_ref, ckv_ref, kr_ref, ga_ref, gb_ref):
        n = ref.shape[-1]
        ref[0] = _dot(h, w_ref[:, col:col + n]).astype(ref.dtype)
        col += n


def _inproj(x, mod3, w):
    bsz, seq, d = x.shape
    tm = min(TM_PROJ, seq)
    widths = (3 * A_WIDTH, B_Q_LORA, B_KV_LORA, LANES, d, d)
    dtypes = (BF16, BF16, BF16, F32, BF16, BF16)
    assert sum(widths) == w.shape[1]
    return pl.pallas_call(
        _inproj_kernel,
        out_shape=[jax.ShapeDtypeStruct((bsz, seq, n), dt) for n, dt in zip(widths, dtypes)],
        grid=(bsz, seq // tm),
        in_specs=[pl.BlockSpec((1, tm, d), lambda b, i: (b, i, 0)),
                  pl.BlockSpec((1, 6, d), lambda b, i: (b, 0, 0)),
                  pl.BlockSpec(w.shape, lambda b, i: (0, 0))],
        out_specs=[pl.BlockSpec((1, tm, n), lambda b, i: (b, i, 0)) for n in widths],
        compiler_params=_params(("arbitrary", "arbitrary"), 48),
    )(x, mod3, w)


def _attn_a_kernel(q_ref, k_ref, v_ref, bias_ref, o_ref, *, tq, nd):
    i = pl.program_id(2)
    q = q_ref[0]
    lane = lax.broadcasted_iota(jnp.int32, (1, LANES), 1)
    outs = []
    for hh in range(2):
        in_head = (lane // A_HEAD_DIM) == hh
        scores, values = [], []
        for d in range(nd):
            start = pl.multiple_of(jnp.maximum(i - d, 0) * tq, tq)
            k = k_ref[0, pl.ds(start, tq), :]
            kh = jnp.where(in_head, k, jnp.zeros_like(k))
            s = _dot_nt(q, kh) + bias_ref[hh, :, (nd - 1 - d) * tq:(nd - d) * tq]
            if d > 0:
                s = jnp.where(i >= d, s, NEG_INF)
            scores.append(s)
            values.append(v_ref[0, pl.ds(start, tq), :])
        m = scores[0].max(axis=-1, keepdims=True)
        for s in scores[1:]:
            m = jnp.maximum(m, s.max(axis=-1, keepdims=True))
        l = jnp.zeros_like(m)
        acc = jnp.zeros((tq, LANES), F32)
        for s, v in zip(scores, values):
            p = jnp.exp(s - m)
            l = l + p.sum(axis=-1, keepdims=True)
            acc = acc + _dot(p.astype(BF16), v)
        outs.append(acc / l)
    o_ref[0] = jnp.where(lane < A_HEAD_DIM, outs[0], outs[1]).astype(o_ref.dtype)


def _attn_a(qkv, bias_tab):
    bsz, seq, _ = qkv.shape
    tq = min(TQ_A, seq)
    nd = bias_tab.shape[2] // tq
    ncb = A_WIDTH // LANES
    return pl.pallas_call(
        functools.partial(_attn_a_kernel, tq=tq, nd=nd),
        out_shape=jax.ShapeDtypeStruct((bsz, seq, A_WIDTH), BF16),
        grid=(ncb, bsz, seq // tq),
        in_specs=[pl.BlockSpec((1, tq, LANES), lambda hp, b, i: (b, i, hp)),
                  pl.BlockSpec((1, seq, LANES), lambda hp, b, i: (b, 0, ncb + hp)),
                  pl.BlockSpec((1, seq, LANES), lambda hp, b, i: (b, 0, 2 * ncb + hp)),
                  pl.BlockSpec((2, tq, nd * tq), lambda hp, b, i: (hp, 0, 0))],
        out_specs=pl.BlockSpec((1, tq, LANES), lambda hp, b, i: (b, i, hp)),
        compiler_params=_params(("arbitrary",) * 3, 32),
    )(qkv, qkv, qkv, bias_tab)


def _bias_table(rel_bias, tq):
    nd = (A_LEFT_CHUNKS * CHUNK + tq - 1) // tq + 1
    width = nd * tq
    period = width + tq
    u = jnp.arange(period)
    dist = jnp.where(u < width, (nd - 1) * tq - u, (nd - 1) * tq + period - u)
    vec = rel_bias[:, jnp.clip(dist, -A_MAX_REL, A_MAX_REL) + A_MAX_REL].astype(F32)
    toep = jnp.tile(vec, (1, tq))[:, :tq * (period - 1)].reshape(-1, tq, period - 1)[:, :, :width]
    qi = jnp.arange(tq)[:, None]
    x = jnp.arange(width)[None, :]
    dchunk = qi // CHUNK - x // CHUNK + (nd - 1) * (tq // CHUNK)
    valid = (dchunk >= 0) & (dchunk <= A_LEFT_CHUNKS)
    return jnp.where(valid[None], toep, NEG_INF)


def _mla_proj_kernel(cq_ref, ckv_ref, kr_ref, cos_ref, sa_ref, sb_ref, rq_ref, rkv_ref,
                     wq_ref, wk_ref, wv_ref, q_out, k_out, v_out):
    def rms(x, g):
        xf = x.astype(F32)
        ms = jnp.mean(xf * xf, axis=-1, keepdims=True)
        return (xf * lax.rsqrt(ms + RMS_EPS) * g).astype(BF16)

    cqn = rms(cq_ref[0], rq_ref[...])
    ckvn = rms(ckv_ref[0], rkv_ref[...])
    q = _dot(cqn, wq_ref[...])
    kn = _dot(ckvn, wk_ref[...])
    vlane = lax.broadcasted_iota(jnp.int32, (1, v_out.shape[-1]), 1)
    ones_lane = jnp.where(vlane % V_PAIR_LANES == 2 * B_V_DIM, 1.0, 0.0)
    v_out[0] = (_dot(ckvn, wv_ref[...]) + ones_lane).astype(v_out.dtype)
    cos_t, sin_a, sin_b = cos_ref[0], sa_ref[0], sb_ref[0]

    def rope(x):
        return (x * cos_t + pltpu.roll(x, LANES - B_ROPE_DIM // 2, 1) * sin_a
                + pltpu.roll(x, B_ROPE_DIM // 2, 1) * sin_b)

    kpe = rope(kr_ref[0])
    scale = B_QK_DIM ** -0.5 * LOG2_E
    for h in range(B_HEADS):
        sl = slice(h * LANES, (h + 1) * LANES)
        q_out[0, :, sl] = (rope(q[:, sl]) * scale).astype(q_out.dtype)
        k_out[0, :, sl] = (kn[:, sl] + kpe).astype(k_out.dtype)


def _mla_proj(cq, ckv, kr, cos_t, sin_a, sin_b, rms_q, rms_kv, wq, wk, wv):
    bsz, seq, _ = cq.shape
    tm = min(TM_PROJ, seq)
    tok = lambda n: pl.BlockSpec((1, tm, n), lambda b, i: (b, i, 0))
    full = lambda a: pl.BlockSpec(a.shape, lambda b, i: (0,) * a.ndim)
    return pl.pallas_call(
        _mla_proj_kernel,
        out_shape=[jax.ShapeDtypeStruct((bsz, seq, B_HEADS * LANES), BF16),
                   jax.ShapeDtypeStruct((bsz, seq, B_HEADS * LANES), BF16),
                   jax.ShapeDtypeStruct((bsz, seq, wv.shape[1]), BF16)],
        grid=(bsz, seq // tm),
        in_specs=[tok(B_Q_LORA), tok(B_KV_LORA), tok(LANES), tok(LANES), tok(LANES), tok(LANES),
                  full(rms_q), full(rms_kv), full(wq), full(wk), full(wv)],
        out_specs=[tok(B_HEADS * LANES), tok(B_HEADS * LANES), tok(wv.shape[1])],
        compiler_params=_params(("arbitrary", "arbitrary"), 32),
    )(cq, ckv, kr, cos_t, sin_a, sin_b, rms_q, rms_kv, wq, wk, wv)


def _mla_attn_kernel(q_ref, k_ref, v_ref, o_ref, s_a, s_b, m_s, acc_s, *, t):
    i = pl.program_id(2)
    lane = lax.broadcasted_iota(jnp.int32, (1, LANES), 1)
    m_s[...] = jnp.full(m_s.shape, NEG_INF, F32)
    acc_s[...] = jnp.zeros(acc_s.shape, F32)

    def produce(kj, s_ref):
        start = pl.multiple_of(kj * t, t)
        for hh in range(2):
            q = q_ref[0, :, hh * LANES:(hh + 1) * LANES]
            k = k_ref[0, pl.ds(start, t), hh * LANES:(hh + 1) * LANES]
            s_ref[hh] = _dot_nt(q, k)

    def consume(kj, s_ref, masked):
        start = pl.multiple_of(kj * t, t)
        v = v_ref[0, pl.ds(start, t), :]
        for hh in range(2):
            s = s_ref[hh]
            if masked:
                row_chunk = lax.broadcasted_iota(jnp.int32, (t, t), 0) // CHUNK
                col_chunk = lax.broadcasted_iota(jnp.int32, (t, t), 1) // CHUNK
                s = jnp.where(col_chunk <= row_chunk, s, NEG_INF)
            m_old = m_s[hh]
            m_new = jnp.maximum(m_old, jnp.broadcast_to(s.max(axis=-1, keepdims=True), m_old.shape))
            p = jnp.exp2(s - jnp.tile(m_new, (1, t // LANES))).astype(BF16)
            alpha = jnp.exp2(m_old - m_new)
            acc_s[hh] = jnp.tile(alpha, (1, V_PAIR_LANES // LANES)) * acc_s[hh] + _dot(p, v)
            m_s[hh] = m_new

    def body(pair, c):
        kj = 2 * pair
        produce(kj + 1, s_b)
        consume(kj, s_a, False)
        produce(kj + 2, s_a)
        consume(kj + 1, s_b, False)
        return c

    produce(0, s_a)
    lax.fori_loop(0, i // 2, body, 0)

    @pl.when(i % 2 == 0)
    def _():
        consume(i, s_a, True)

    @pl.when(i % 2 == 1)
    def _():
        produce(i, s_b)
        consume(i - 1, s_a, False)
        consume(i, s_b, True)

    outs = []
    for hh in range(2):
        acc = acc_s[hh]
        outs.append(acc[:, :LANES] / acc[:, 2 * B_V_DIM:2 * B_V_DIM + 1])
    o_ref[0] = jnp.where(lane < B_V_DIM, outs[0], outs[1]).astype(o_ref.dtype)


def _mla_attn(qm, km, vm):
    bsz, seq, _ = qm.shape
    t = min(TQ_B, seq)
    return pl.pallas_call(
        functools.partial(_mla_attn_kernel, t=t),
        out_shape=jax.ShapeDtypeStruct((bsz, seq, B_HEADS * B_V_DIM), BF16),
        grid=(bsz, B_HEADS // 2, seq // t),
        in_specs=[pl.BlockSpec((1, t, 2 * LANES), lambda b, hp, i: (b, i, hp)),
                  pl.BlockSpec((1, seq, 2 * LANES), lambda b, hp, i: (b, 0, hp)),
                  pl.BlockSpec((1, seq, V_PAIR_LANES), lambda b, hp, i: (b, 0, hp))],
        out_specs=pl.BlockSpec((1, t, LANES), lambda b, hp, i: (b, i, hp)),
        scratch_shapes=[pltpu.VMEM((2, t, t), F32), pltpu.VMEM((2, t, t), F32),
                        pltpu.VMEM((2, t, LANES), F32),
                        pltpu.VMEM((2, t, V_PAIR_LANES), F32)],
        compiler_params=_params(("arbitrary",) * 3, 40),
    )(qm, km, vm)


def _merge_kernel(x_ref, ya_ref, yb_ref, ga_ref, gb_ref, mod_ref, wa_ref, wb_ref, wo_ref,
                  g1_ref, b1_ref, wrh_ref, wrl_ref, br_ref, x1_ref, h2_ref, lg_ref):
    m = mod_ref[0]
    a = _dot(ya_ref[0], wa_ref[...])
    b = _dot(yb_ref[0], wb_ref[...])
    merged = (jax.nn.sigmoid(ga_ref[0].astype(F32)) * a
              + jax.nn.sigmoid(gb_ref[0].astype(F32)) * b)
    o = _dot(merged.astype(BF16), wo_ref[...])
    x1 = _layer_norm(DEEPNORM_ALPHA * x_ref[0] + m[2:3] * o, g1_ref[...], b1_ref[...])
    h2 = x1 * (1.0 + m[4:5]) + m[3:4]
    x1_ref[0] = x1
    _store_token_tiles(h2_ref.at[0], h2)
    h_hi = h2.astype(BF16)
    h_lo = (h2 - h_hi.astype(F32)).astype(BF16)
    lg_ref[0] = (_dot(h_hi, wrh_ref[...]) + _dot(h_hi, wrl_ref[...]) + _dot(h_lo, wrh_ref[...])
                 + br_ref[...])


def _merge(x, ya, yb, ga, gb, mod3, wa, wb, wo, g1, b1, wr, br):
    wr_hi = wr.astype(BF16)
    wr_lo = (wr - wr_hi.astype(F32)).astype(BF16)
    bsz, seq, d = x.shape
    tm = min(TM_MERGE, seq)
    tok = lambda n: pl.BlockSpec((1, tm, n), lambda b, i: (b, i, 0))
    full = lambda a: pl.BlockSpec(a.shape, lambda b, i: (0,) * a.ndim)
    return pl.pallas_call(
        _merge_kernel,
        out_shape=[jax.ShapeDtypeStruct((bsz, seq, d), F32),
                   jax.ShapeDtypeStruct((bsz, seq * (d // LANES), LANES), F32),
                   jax.ShapeDtypeStruct((bsz, seq, LANES), F32)],
        grid=(bsz, seq // tm),
        in_specs=[tok(d), tok(A_WIDTH), tok(B_HEADS * B_V_DIM), tok(d), tok(d),
                  pl.BlockSpec((1, 6, d), lambda b, i: (b, 0, 0)),
                  full(wa), full(wb), full(wo), full(g1), full(b1), full(wr_hi), full(wr_lo),
                  full(br)],
        out_specs=[tok(d),
                   pl.BlockSpec((1, tm * (d // LANES), LANES), lambda b, i: (b, i, 0)),
                   tok(LANES)],
        compiler_params=_params(("arbitrary", "arbitrary"), 48),
    )(x, ya, yb, ga, gb, mod3, wa, wb, wo, g1, b1, wr_hi, wr_lo, br)


def _route_kernel(lg_ref, pos_ref, w_ref, tmap_ref, nv_ref, cnt_s, carry_s, pstart_s, *, tr, tm):
    phase = pl.program_id(0)
    i = pl.program_id(1)
    lane = lax.broadcasted_iota(jnp.int32, (tr, LANES), 1)
    work = lg_ref[...]
    sels, vals = [], []
    for _ in range(TOP_K):
        m = work.max(axis=-1, keepdims=True)
        idx = jnp.where(work == m, lane, LANES).min(axis=-1, keepdims=True)
        sel = lane == idx
        work = jnp.where(sel, -jnp.inf, work)
        sels.append(sel)
        vals.append(m)
    picked = sum(jnp.where(s, 1.0, 0.0) for s in sels)
    tile_count = jnp.broadcast_to(picked.sum(axis=0, keepdims=True), (8, LANES))

    @pl.when((phase == 0) & (i == 0))
    def _():
        cnt_s[...] = jnp.zeros_like(cnt_s)

    @pl.when(phase == 0)
    def _():
        cnt_s[...] += tile_count

    @pl.when((phase == 1) & (i == 0))
    def _():
        lane8 = lax.broadcasted_iota(jnp.int32, (8, LANES), 1)
        padded = jnp.floor((cnt_s[...] + (tm - 1)) * (1.0 / tm)) * tm
        cum = padded
        shift = 1
        while shift < N_EXPERTS:
            cum = cum + jnp.where(lane8 >= shift, pltpu.roll(cum, shift, 1), 0.0)
            shift *= 2
        pstart_s[...] = cum - padded
        carry_s[...] = jnp.zeros_like(carry_s)
        nt = tmap_ref.shape[0]
        tile_start = (lax.broadcasted_iota(jnp.int32, (nt, LANES), 0) * tm).astype(F32)
        lane_t = lax.broadcasted_iota(jnp.int32, (nt, LANES), 1)
        done = jnp.where((lane_t < N_EXPERTS) & (cum[0:1] <= tile_start), 1.0, 0.0)
        te = jnp.minimum(done.sum(axis=-1, keepdims=True), N_EXPERTS - 1.0)
        tmap_ref[...] = jnp.broadcast_to(te, (nt, LANES)).astype(jnp.int32)
        total = jnp.where(lane8 == N_EXPERTS - 1, cum, 0.0).sum(axis=-1, keepdims=True)
        row8 = lax.broadcasted_iota(jnp.int32, (8, LANES), 0)
        meta = jnp.where(row8 == 0, jnp.broadcast_to(total * (1.0 / tm), (8, LANES)),
                         jnp.where(row8 == 1, cum - padded + cnt_s[...], cum))
        nv_ref[...] = meta.astype(jnp.int32)

    @pl.when(phase == 1)
    def _():
        r = lax.broadcasted_iota(jnp.int32, (tr, tr), 0)
        c = lax.broadcasted_iota(jnp.int32, (tr, tr), 1)
        tri = jnp.where(c < r, 1.0, 0.0).astype(BF16)
        before = _dot(tri, picked.astype(BF16)) + carry_s[0:1]
        base = before + pstart_s[0:1]
        carry_s[...] += tile_count
        den = sum(jnp.exp(v - vals[0]) for v in vals)
        pos = jnp.zeros((tr, LANES), F32)
        wts = jnp.zeros((tr, LANES), F32)
        for k in range(TOP_K):
            pk = jnp.where(sels[k], base, 0.0).sum(axis=-1, keepdims=True)
            pos = jnp.where(lane == k, pk, pos)
            wts = jnp.where(lane == k, jnp.exp(vals[k] - vals[0]) / den, wts)
        pos_ref[...] = pos.astype(jnp.int32)
        w_ref[...] = wts


def _route(logits, tm):
    n = logits.shape[0]
    tr = min(TR_ROUTE, n)
    return pl.pallas_call(
        functools.partial(_route_kernel, tr=tr, tm=tm),
        out_shape=[jax.ShapeDtypeStruct((n, LANES), jnp.int32),
                   jax.ShapeDtypeStruct((n, LANES), F32),
                   jax.ShapeDtypeStruct((MAX_TILES_PAD, LANES), jnp.int32),
                   jax.ShapeDtypeStruct((8, LANES), jnp.int32)],
        grid=(2, n // tr),
        in_specs=[pl.BlockSpec((tr, LANES), lambda p, i: (i, 0))],
        out_specs=[pl.BlockSpec((tr, LANES), lambda p, i: (i * p, 0)),
                   pl.BlockSpec((tr, LANES), lambda p, i: (i * p, 0)),
                   pl.BlockSpec((MAX_TILES_PAD, LANES), lambda p, i: (0, 0)),
                   pl.BlockSpec((8, LANES), lambda p, i: (0, 0))],
        scratch_shapes=[pltpu.VMEM((8, LANES), F32)] * 3,
        compiler_params=_params(("arbitrary", "arbitrary")),
    )(logits)


def _dispatch_kernel(pos_ref, fill_ref, h_ref, xs_ref, zero_s, ztile_s, sem, fill_sem, tail_sem,
                     *, tm, nsub, tm_moe):
    def tile(ref, r):
        return ref.at[pl.ds(pl.multiple_of(r * nsub, nsub), nsub)]

    @pl.when(pl.program_id(0) == 0)
    def _():
        zero_s[...] = jnp.zeros_like(zero_s)
        ztile_s[...] = jnp.zeros_like(ztile_s)

        def fill_copy(r):
            return pltpu.make_async_copy(zero_s, tile(xs_ref, r), fill_sem)

        def tail_copy(t):
            rows = tm_moe * nsub
            return pltpu.make_async_copy(
                ztile_s, xs_ref.at[pl.ds(pl.multiple_of(t * rows, rows), rows)], tail_sem)

        def per_expert(e, total):
            lo, hi = fill_ref[LANES + e], fill_ref[2 * LANES + e]

            def one(r, c):
                fill_copy(r).start()
                return c

            lax.fori_loop(lo, hi, one, 0)
            return total + (hi - lo)

        total = lax.fori_loop(0, N_EXPERTS, per_expert, 0)
        n_tiles = xs_ref.shape[0] // (tm_moe * nsub)

        def tail_start(t, c):
            tail_copy(t).start()
            return c

        lax.fori_loop(fill_ref[0], n_tiles, tail_start, 0)

        def drain(_, c):
            fill_copy(0).wait()
            return c

        lax.fori_loop(0, total, drain, 0)

        def tail_drain(_, c):
            tail_copy(0).wait()
            return c

        lax.fori_loop(fill_ref[0], n_tiles, tail_drain, 0)

    def issue(r, c):
        for k in range(TOP_K):
            pltpu.make_async_copy(tile(h_ref, r), tile(xs_ref, pos_ref[r * TOP_K + k]), sem).start()
        return c

    lax.fori_loop(0, tm, issue, 0)
    for _ in range(TOP_K):
        pltpu.make_async_copy(h_ref, xs_ref.at[pl.ds(0, tm * nsub)], sem).wait()


def _dispatch(pos_flat, fill, h2, cap, nsub, tm_moe):
    n = h2.shape[0] // nsub
    tm = min(TM_ROWS, n)
    return pl.pallas_call(
        functools.partial(_dispatch_kernel, tm=tm, nsub=nsub, tm_moe=tm_moe),
        out_shape=jax.ShapeDtypeStruct((cap * nsub, LANES), F32),
        grid=(n // tm,),
        in_specs=[pl.BlockSpec((tm * TOP_K,), lambda i: (i,), memory_space=pltpu.SMEM),
                  pl.BlockSpec(fill.shape, lambda i: (0,), memory_space=pltpu.SMEM),
                  pl.BlockSpec((tm * nsub, LANES), lambda i: (i, 0))],
        out_specs=pl.BlockSpec(memory_space=pl.ANY),
        scratch_shapes=[pltpu.VMEM((nsub, LANES), F32), pltpu.VMEM((tm_moe * nsub, LANES), F32),
                        pltpu.SemaphoreType.DMA(()), pltpu.SemaphoreType.DMA(()),
                        pltpu.SemaphoreType.DMA(())],
        compiler_params=_params(("arbitrary",)),
    )(pos_flat, fill, h2)


def _moe_kernel(te_ref, nv_ref, x_ref, bg_ref, bu_ref, bd_ref, wg_hbm, wu_hbm, wd_hbm,
                y_ref, wf_s, wg_s, wu_s, wd_s, slot_s, sem):
    i = pl.program_id(0)
    nv = nv_ref[0]
    last = te_ref.shape[0] - 1

    def weight_copies(e, slot):
        return [pltpu.make_async_copy(w.at[e], wf_s.at[slot, j], sem.at[slot])
                for j, w in enumerate((wg_hbm, wu_hbm, wd_hbm))]

    @pl.when(i < nv)
    def _():
        e = te_ref[i]

        @pl.when(i == 0)
        def _():
            slot_s[0] = 0
            for c in weight_copies(e, 0):
                c.start()

        @pl.when((i == 0) | (e != te_ref[jnp.maximum(i - 1, 0)]))
        def _():
            slot = slot_s[0]
            for c in weight_copies(e, slot):
                c.wait()
            nxt = lax.while_loop(lambda j: (j < nv) & (te_ref[jnp.minimum(j, last)] == e),
                                 lambda j: j + 1, i + 1)

            @pl.when(nxt < nv)
            def _():
                for c in weight_copies(te_ref[jnp.minimum(nxt, last)], 1 - slot):
                    c.start()

            wg_s[...] = wf_s[slot, 0].astype(BF16)
            wu_s[...] = wf_s[slot, 1].astype(BF16)
            wd_s[...] = wf_s[slot, 2].astype(BF16)
            slot_s[0] = 1 - slot

        xb = _load_token_tiles(x_ref, wg_s.shape[0] // LANES).astype(BF16)
        g = _dot(xb, wg_s[...]) + bg_ref[0]
        u = _dot(xb, wu_s[...]) + bu_ref[0]
        g = jnp.minimum(g, SWIGLU_LIMIT)
        u = jnp.clip(u, -SWIGLU_LIMIT, SWIGLU_LIMIT)
        h = (u + 1.0) * (g * jax.nn.sigmoid(SWIGLU_ALPHA * g))
        _store_token_tiles(y_ref, _dot(h.astype(BF16), wd_s[...]) + bd_ref[0])

    @pl.when(i >= nv)
    def _():
        y_ref[...] = jnp.zeros_like(y_ref)


def _moe(tile_expert, n_valid, xs, w_gate, b_gate, w_up, b_up, w_down, b_down, tm):
    e, d, f = w_gate.shape
    assert d == f
    nsub = d // LANES
    cap = xs.shape[0] // nsub
    n_tiles = cap // tm

    def tile(i, te, nv):
        return jnp.minimum(i, nv[0] - 1)

    exp3 = lambda i, te, nv: (te[tile(i, te, nv)], 0, 0)
    grid_spec = pltpu.PrefetchScalarGridSpec(
        num_scalar_prefetch=2,
        grid=(n_tiles,),
        in_specs=[pl.BlockSpec((tm * nsub, LANES), lambda i, te, nv: (tile(i, te, nv), 0)),
                  pl.BlockSpec((1, 1, f), exp3), pl.BlockSpec((1, 1, f), exp3),
                  pl.BlockSpec((1, 1, d), exp3),
                  pl.BlockSpec(memory_space=pl.ANY), pl.BlockSpec(memory_space=pl.ANY),
                  pl.BlockSpec(memory_space=pl.ANY)],
        out_specs=pl.BlockSpec((tm * nsub, LANES), lambda i, te, nv: (i, 0)),
        scratch_shapes=[pltpu.VMEM((2, 3, d, f), F32),
                        pltpu.VMEM((d, f), BF16), pltpu.VMEM((d, f), BF16), pltpu.VMEM((f, d), BF16),
                        pltpu.SMEM((1,), jnp.int32), pltpu.SemaphoreType.DMA((2,))],
    )
    return pl.pallas_call(
        _moe_kernel,
        out_shape=jax.ShapeDtypeStruct((cap * nsub, LANES), F32),
        grid_spec=grid_spec,
        compiler_params=_params(("arbitrary",), 56),
    )(tile_expert, n_valid, xs, b_gate.reshape(e, 1, f), b_up.reshape(e, 1, f),
      b_down.reshape(e, 1, d), w_gate, w_up, w_down)


def _combine_kernel(pos_ref, posn_ref, w_ref, x1_ref, mod_ref, g_ref, b_ref, y_ref, o_ref,
                    buf, sem, *, tm):
    i = pl.program_id(0)
    n = pl.num_programs(0)
    nsub = x1_ref.shape[1] // LANES

    def issue(p_ref, slot):
        def body(r, c):
            for k in range(TOP_K):
                src = pl.ds(pl.multiple_of(p_ref[r * TOP_K + k] * nsub, nsub), nsub)
                dst = pl.ds(pl.multiple_of(r * nsub, nsub), nsub)
                pltpu.make_async_copy(y_ref.at[src], buf.at[slot, k, dst], sem.at[slot]).start()
            return c
        lax.fori_loop(0, tm, body, 0)

    @pl.when(i == 0)
    def _():
        issue(pos_ref, 0)

    @pl.when(i + 1 < n)
    def _():
        issue(posn_ref, (i + 1) % 2)

    slot = i % 2
    for k in range(TOP_K):
        pltpu.make_async_copy(y_ref.at[pl.ds(0, tm * nsub)], buf.at[slot, k], sem.at[slot]).wait()
    w = w_ref[...]
    f = jnp.zeros(x1_ref.shape, F32)
    for k in range(TOP_K):
        f = f + w[:, k:k + 1] * _load_token_tiles(buf.at[slot, k], nsub)
    m = mod_ref[0]
    o_ref[...] = _layer_norm(DEEPNORM_ALPHA * x1_ref[...] + m[5:6] * f, g_ref[...], b_ref[...])


def _combine(pos_flat, w, x1, mod3, g2, b2, y, seq):
    n, d = x1.shape
    tm = min(TM_ROWS, seq)
    nsteps = n // tm
    per_seq = seq // tm
    return pl.pallas_call(
        functools.partial(_combine_kernel, tm=tm),
        out_shape=jax.ShapeDtypeStruct((n, d), F32),
        grid=(nsteps,),
        in_specs=[pl.BlockSpec((tm * TOP_K,), lambda i: (i,), memory_space=pltpu.SMEM),
                  pl.BlockSpec((tm * TOP_K,), lambda i: (jnp.minimum(i + 1, nsteps - 1),),
                               memory_space=pltpu.SMEM),
                  pl.BlockSpec((tm, LANES), lambda i: (i, 0)),
                  pl.BlockSpec((tm, d), lambda i: (i, 0)),
                  pl.BlockSpec((1, 6, d), lambda i: (i // per_seq, 0, 0)),
                  pl.BlockSpec(g2.shape, lambda i: (0, 0)),
                  pl.BlockSpec(b2.shape, lambda i: (0, 0)),
                  pl.BlockSpec(memory_space=pl.ANY)],
        out_specs=pl.BlockSpec((tm, d), lambda i: (i, 0)),
        scratch_shapes=[pltpu.VMEM((2, TOP_K, tm * (d // LANES), LANES), F32),
                        pltpu.SemaphoreType.DMA((2,))],
        compiler_params=_params(("arbitrary",), 32),
    )(pos_flat, pos_flat, w, x1, mod3, g2, b2, y)


def _prep_w_in(w_in):
    d = w_in.shape[0]
    a3 = 3 * A_WIDTH
    lat = a3 + B_Q_LORA + B_KV_LORA
    z = lambda n: jnp.zeros((d, n), w_in.dtype)
    return jnp.concatenate([
        w_in[:, :A_WIDTH] * (A_HEAD_DIM ** -0.5),
        w_in[:, A_WIDTH:lat],
        z(B_NOPE_DIM), w_in[:, lat:lat + B_ROPE_DIM], z(LANES - B_NOPE_DIM - B_ROPE_DIM),
        w_in[:, lat + B_ROPE_DIM:]], axis=1).astype(BF16)


def _prep_w_uq(w_uq):
    r = w_uq.shape[0]
    w = w_uq.reshape(r, B_HEADS, B_QK_DIM)
    w = jnp.pad(w, ((0, 0), (0, 0), (0, LANES - B_QK_DIM)))
    return w.reshape(r, B_HEADS * LANES).astype(BF16)


def _prep_w_ukv(w_ukv):
    r = w_ukv.shape[0]
    w = w_ukv.reshape(r, B_HEADS, B_NOPE_DIM + B_V_DIM)
    wk = jnp.pad(w[:, :, :B_NOPE_DIM], ((0, 0), (0, 0), (0, LANES - B_NOPE_DIM)))
    wv = w[:, :, B_NOPE_DIM:].reshape(r, B_HEADS // 2, 2 * B_V_DIM)
    wv = jnp.pad(wv, ((0, 0), (0, 0), (0, V_PAIR_LANES - 2 * B_V_DIM)))
    return (wk.reshape(r, B_HEADS * LANES).astype(BF16),
            wv.reshape(r, B_HEADS // 2 * V_PAIR_LANES).astype(BF16))


def _rope_tables(positions):
    half = B_ROPE_DIM // 2
    freqs = ROPE_THETA ** (-jnp.arange(half, dtype=F32) / half)
    ang = positions.astype(F32)[:, :, None] * freqs
    cos, sin = jnp.cos(ang), jnp.sin(ang)
    z = lambda n: jnp.zeros(ang.shape[:2] + (n,), F32)
    tail = LANES - B_NOPE_DIM - B_ROPE_DIM
    cos_t = jnp.concatenate([jnp.ones(ang.shape[:2] + (B_NOPE_DIM,), F32), cos, cos, z(tail)], -1)
    sin_a = jnp.concatenate([z(B_NOPE_DIM), -sin, z(half), z(tail)], -1)
    sin_b = jnp.concatenate([z(B_NOPE_DIM), z(half), sin, z(tail)], -1)
    return cos_t, sin_a, sin_b


def kernel(x, c, positions, w_ada, b_ada, w_in, rms_q, w_uq, rms_kv, w_ukv, rel_bias,
           w_branch_a, w_branch_b, w_out, ln1_g, ln1_b, w_router, b_router,
           w_gate, b_gate, w_up, b_up, w_down, b_down, ln2_g, ln2_b):
    bsz, seq, d = x.shape
    n_tok = bsz * seq
    assert w_ada.shape[0] == DEPTH == 1
    row = lambda v: v.reshape(1, -1)

    mod3 = _ada(c, w_ada[0], b_ada[0]).reshape(bsz, 6, d)

    qkv, cq, ckv, kr, ga, gb = _inproj(x, mod3, _prep_w_in(w_in[0]))
    ya = _attn_a(qkv, _bias_table(rel_bias[0], min(TQ_A, seq)))
    wk, wv = _prep_w_ukv(w_ukv[0])
    qm, km, vm = _mla_proj(cq, ckv, kr, *_rope_tables(positions), row(rms_q[0]), row(rms_kv[0]),
                           _prep_w_uq(w_uq[0]), wk, wv)
    yb = _mla_attn(qm, km, vm)

    wr = jnp.pad(w_router[0], ((0, 0), (0, LANES - N_EXPERTS)))
    br = jnp.pad(b_router[0], (0, LANES - N_EXPERTS), constant_values=NEG_INF)
    x1, h2, logits = _merge(x, ya, yb, ga, gb, mod3,
                            w_branch_a[0].astype(BF16), w_branch_b[0].astype(BF16),
                            w_out[0].astype(BF16), row(ln1_g[0]), row(ln1_b[0]), wr, row(br))

    tm = TM_MOE
    n_tiles = -(-(n_tok * TOP_K) // tm) + N_EXPERTS
    assert n_tiles <= MAX_TILES_PAD
    pos, wts, tmap, meta = _route(logits.reshape(n_tok, LANES), tm)
    pos_flat = pos[:, :TOP_K].reshape(-1)
    fill = meta[:3].reshape(-1)
    nsub = d // LANES
    xs = _dispatch(pos_flat, fill, h2.reshape(n_tok * nsub, LANES), n_tiles * tm, nsub, tm)
    y = _moe(tmap[:n_tiles, 0], meta[0, :1], xs, w_gate[0], b_gate[0], w_up[0], b_up[0],
             w_down[0], b_down[0], tm)
    out = _combine(pos_flat, wts, x1.reshape(n_tok, d), mod3, row(ln2_g[0]), row(ln2_b[0]), y, seq)
    return out.reshape(bsz, seq, d)
```

```python
import functools

import jax
import jax.numpy as jnp
from jax import lax
from jax.experimental import pallas as pl
from jax.experimental.pallas import tpu as pltpu

F32 = jnp.float32
BF16 = jnp.bfloat16

CHUNK = 64
A_HEADS = 8
A_HEAD_DIM = 64
A_LEFT_CHUNKS = 8
A_MAX_REL = 128
A_WIDTH = A_HEADS * A_HEAD_DIM
B_HEADS = 8
B_NOPE_DIM = 64
B_ROPE_DIM = 32
B_V_DIM = 64
B_Q_LORA = 384
B_KV_LORA = 256
B_QK_DIM = B_NOPE_DIM + B_ROPE_DIM
ROPE_THETA = 10000.0
N_EXPERTS = 32
TOP_K = 4
SWIGLU_LIMIT = 7.0
SWIGLU_ALPHA = 1.702
DEPTH = 1
DEEPNORM_ALPHA = (2.0 * DEPTH) ** 0.25
LN_EPS = 1e-5
RMS_EPS = 1e-6
NEG_INF = -1e30
LOG2_E = 1.4426950408889634

LANES = 128
MXU_COLS = 256
V_PAIR_LANES = MXU_COLS

TM_PROJ = 512
TQ_A = 256
A_QBLOCKS = 2
TQ_B = 512
TM_MERGE = 512
TR_ROUTE = 1024
TM_MOE = 256
TM_ROWS = 256
MAX_TILES_PAD = 512


def _params(sem, vmem_mb=None):
    return pltpu.CompilerParams(
        dimension_semantics=sem,
        vmem_limit_bytes=None if vmem_mb is None else vmem_mb << 20)


def _dot(a, b):
    return jnp.dot(a, b, preferred_element_type=F32)


def _dot_nt(a, b):
    return lax.dot_general(a, b, (((1,), (1,)), ((), ())), preferred_element_type=F32)


def _store_token_tiles(ref, x):
    rows, d = x.shape
    nsub = d // LANES
    for s in range(nsub):
        ref[pl.ds(s, rows, stride=nsub), :] = x[:, s * LANES:(s + 1) * LANES]


def _load_token_tiles(ref, nsub):
    rows = ref.shape[0] // nsub
    return jnp.concatenate([ref[pl.ds(s, rows, stride=nsub), :] for s in range(nsub)], axis=1)


def _layer_norm(x, g, b):
    mu = jnp.mean(x, axis=-1, keepdims=True)
    xc = x - mu
    var = jnp.mean(xc * xc, axis=-1, keepdims=True)
    return xc * lax.rsqrt(var + LN_EPS) * g + b


def _ada_kernel(c_ref, w_ref, b_ref, o_ref):
    c = c_ref[...]
    sc = c * jax.nn.sigmoid(c)
    o_ref[...] = jnp.dot(sc, w_ref[...], preferred_element_type=F32,
                         precision=lax.Precision.HIGHEST) + b_ref[...]


def _ada(c, w_ada, b_ada):
    bsz, d = c.shape
    m = w_ada.shape[1]
    tn = 1024
    return pl.pallas_call(
        _ada_kernel,
        out_shape=jax.ShapeDtypeStruct((bsz, m), F32),
        grid=(m // tn,),
        in_specs=[pl.BlockSpec((bsz, d), lambda j: (0, 0)),
                  pl.BlockSpec((d, tn), lambda j: (0, j)),
                  pl.BlockSpec((1, tn), lambda j: (0, j))],
        out_specs=pl.BlockSpec((bsz, tn), lambda j: (0, j)),
        compiler_params=_params(("arbitrary",)),
    )(c, w_ada, b_ada.reshape(1, m))


def _inproj_kernel(x_ref, mod_ref, w_ref, qk_ref, va_ref, cq_ref, ckv_ref, kr_ref, ga_ref, gb_ref):
    m = mod_ref[0]
    h = (x_ref[0] * (1.0 + m[1:2]) + m[0:1]).astype(BF16)
    col = 0
    for ref in (qk_ref, None, cq_ref, ckv_ref, kr_ref, ga_ref, gb_ref):
        if ref is None:
            v = _dot(h, w_ref[:, col:col + A_WIDTH]).astype(BF16)
            ones_blk = jnp.ones((v.shape[0], LANES), BF16)
            parts = []
            for hp in range(A_WIDTH // LANES):
                parts += [v[:, hp * LANES:(hp + 1) * LANES], ones_blk]
            va_ref[0] = jnp.concatenate(parts, axis=1)
            col += A_WIDTH
            continue
        n = ref.shape[-1]
        ref[0] = _dot(h, w_ref[:, col:col + n]).astype(ref.dtype)
        col += n


def _inproj(x, mod3, w):
    bsz, seq, d = x.shape
    tm = min(TM_PROJ, seq)
    widths = (2 * A_WIDTH, A_WIDTH // LANES * V_PAIR_LANES, B_Q_LORA, B_KV_LORA, LANES, d, d)
    dtypes = (BF16, BF16, BF16, BF16, F32, BF16, BF16)
    assert sum(widths) - widths[1] + A_WIDTH == w.shape[1]
    return pl.pallas_call(
        _inproj_kernel,
        out_shape=[jax.ShapeDtypeStruct((bsz, seq, n), dt) for n, dt in zip(widths, dtypes)],
        grid=(bsz, seq // tm),
        in_specs=[pl.BlockSpec((1, tm, d), lambda b, i: (b, i, 0)),
                  pl.BlockSpec((1, 6, d), lambda b, i: (b, 0, 0)),
                  pl.BlockSpec(w.shape, lambda b, i: (0, 0))],
        out_specs=[pl.BlockSpec((1, tm, n), lambda b, i: (b, i, 0)) for n in widths],
        compiler_params=_params(("arbitrary", "arbitrary"), 48),
    )(x, mod3, w)


def _attn_a_kernel(q_ref, k_ref, v_ref, bias_ref, o_ref, *, tq, nd):
    band = nd * tq
    lane = lax.broadcasted_iota(jnp.int32, (1, LANES), 1)
    for sub in range(A_QBLOCKS):
        i = pl.program_id(2) * A_QBLOCKS + sub
        start = pl.multiple_of(jnp.maximum(i - (nd - 1), 0) * tq, tq)
        variant = jnp.minimum(i, nd - 1)
        q = q_ref[0, sub * tq:(sub + 1) * tq, :]
        k = k_ref[0, pl.ds(start, band), :]
        v = v_ref[0, pl.ds(start, band), :]
        zero = jnp.zeros_like(k)
        kk = jnp.concatenate([jnp.where(lane < A_HEAD_DIM, k, zero),
                              jnp.where(lane >= A_HEAD_DIM, k, zero)], axis=0)
        s = _dot_nt(q, kk)
        ps = []
        for hh in range(2):
            sh = s[:, hh * band:(hh + 1) * band] + bias_ref[hh, variant]
            m = sh.max(axis=-1, keepdims=True)
            ps.append(jnp.exp2(sh - m).astype(BF16))
        acc = _dot(jnp.concatenate(ps, axis=0), v)
        out = acc[:, :LANES] / acc[:, LANES:]
        o_ref[0, sub * tq:(sub + 1) * tq, :] = jnp.where(
            lane < A_HEAD_DIM, out[:tq], out[tq:]).astype(o_ref.dtype)


def _attn_a(qk, va, bias_tab):
    bsz, seq, _ = qk.shape
    tq = min(TQ_A, seq)
    nd = bias_tab.shape[1]
    assert seq >= nd * tq
    ncb = A_WIDTH // LANES
    return pl.pallas_call(
        functools.partial(_attn_a_kernel, tq=tq, nd=nd),
        out_shape=jax.ShapeDtypeStruct((bsz, seq, A_WIDTH), BF16),
        grid=(ncb, bsz, seq // (A_QBLOCKS * tq)),
        in_specs=[pl.BlockSpec((1, A_QBLOCKS * tq, LANES), lambda hp, b, i: (b, i, hp)),
                  pl.BlockSpec((1, seq, LANES), lambda hp, b, i: (b, 0, ncb + hp)),
                  pl.BlockSpec((1, seq, V_PAIR_LANES), lambda hp, b, i: (b, 0, hp)),
                  pl.BlockSpec((2, nd, tq, nd * tq), lambda hp, b, i: (hp, 0, 0, 0))],
        out_specs=pl.BlockSpec((1, A_QBLOCKS * tq, LANES), lambda hp, b, i: (b, i, hp)),
        compiler_params=_params(("arbitrary",) * 3, 40),
    )(qk, qk, va, bias_tab)


def _bias_table(rel_bias, tq):
    nd = (A_LEFT_CHUNKS * CHUNK + tq - 1) // tq + 1
    width = nd * tq
    period = width + tq
    u = jnp.arange(period)
    dist = jnp.where(u < width, (nd - 1) * tq - u, (nd - 1) * tq + period - u)
    vec = rel_bias[:, jnp.clip(dist, -A_MAX_REL, A_MAX_REL) + A_MAX_REL].astype(F32)
    toep = jnp.tile(vec, (1, tq))[:, :tq * (period - 1)].reshape(-1, tq, period - 1)[:, :, :width]
    qi = jnp.arange(tq)[:, None]
    x = jnp.arange(width)[None, :]
    dchunk = qi // CHUNK - x // CHUNK + (nd - 1) * (tq // CHUNK)
    valid = (dchunk >= 0) & (dchunk <= A_LEFT_CHUNKS)
    table = jnp.where(valid[None], toep * LOG2_E, NEG_INF)
    variants = []
    for v in range(nd):
        cut = (nd - 1 - v) * tq
        variants.append(jnp.pad(table[:, :, cut:], ((0, 0), (0, 0), (0, cut)),
                                constant_values=NEG_INF))
    return jnp.stack(variants, axis=1)


def _mla_proj_kernel(cq_ref, ckv_ref, kr_ref, cos_ref, sa_ref, sb_ref, rq_ref, rkv_ref,
                     wq_ref, wk_ref, wv_ref, q_out, k_out, v_out):
    def rms(x, g):
        xf = x.astype(F32)
        ms = jnp.mean(xf * xf, axis=-1, keepdims=True)
        return (xf * lax.rsqrt(ms + RMS_EPS) * g).astype(BF16)

    cqn = rms(cq_ref[0], rq_ref[...])
    ckvn = rms(ckv_ref[0], rkv_ref[...])
    q = _dot(cqn, wq_ref[...])
    kn = _dot(ckvn, wk_ref[...])
    vlane = lax.broadcasted_iota(jnp.int32, (1, v_out.shape[-1]), 1)
    ones_lane = jnp.where(vlane % V_PAIR_LANES >= 2 * B_V_DIM, 1.0, 0.0)
    v_out[0] = (_dot(ckvn, wv_ref[...]) + ones_lane).astype(v_out.dtype)
    cos_t, sin_a, sin_b = cos_ref[0], sa_ref[0], sb_ref[0]

    def rope(x):
        return (x * cos_t + pltpu.roll(x, LANES - B_ROPE_DIM // 2, 1) * sin_a
                + pltpu.roll(x, B_ROPE_DIM // 2, 1) * sin_b)

    kpe = rope(kr_ref[0])
    scale = B_QK_DIM ** -0.5 * LOG2_E
    for h in range(B_HEADS):
        sl = slice(h * LANES, (h + 1) * LANES)
        q_out[0, :, sl] = (rope(q[:, sl]) * scale).astype(q_out.dtype)
        k_out[0, :, sl] = (kn[:, sl] + kpe).astype(k_out.dtype)


def _mla_proj(cq, ckv, kr, cos_t, sin_a, sin_b, rms_q, rms_kv, wq, wk, wv):
    bsz, seq, _ = cq.shape
    tm = min(TM_PROJ, seq)
    tok = lambda n: pl.BlockSpec((1, tm, n), lambda b, i: (b, i, 0))
    full = lambda a: pl.BlockSpec(a.shape, lambda b, i: (0,) * a.ndim)
    return pl.pallas_call(
        _mla_proj_kernel,
        out_shape=[jax.ShapeDtypeStruct((bsz, seq, B_HEADS * LANES), BF16),
                   jax.ShapeDtypeStruct((bsz, seq, B_HEADS * LANES), BF16),
                   jax.ShapeDtypeStruct((bsz, seq, wv.shape[1]), BF16)],
        grid=(bsz, seq // tm),
        in_specs=[tok(B_Q_LORA), tok(B_KV_LORA), tok(LANES), tok(LANES), tok(LANES), tok(LANES),
                  full(rms_q), full(rms_kv), full(wq), full(wk), full(wv)],
        out_specs=[tok(B_HEADS * LANES), tok(B_HEADS * LANES), tok(wv.shape[1])],
        compiler_params=_params(("arbitrary", "arbitrary"), 32),
    )(cq, ckv, kr, cos_t, sin_a, sin_b, rms_q, rms_kv, wq, wk, wv)


def _mla_attn_kernel(q_ref, k_ref, v_ref, o_ref, s_a, s_b, m_s, acc_s, *, t):
    i = pl.program_id(2)
    lane = lax.broadcasted_iota(jnp.int32, (1, LANES), 1)
    m_s[...] = jnp.full(m_s.shape, NEG_INF, F32)
    acc_s[...] = jnp.zeros(acc_s.shape, F32)

    def produce(kj, s_ref):
        start = pl.multiple_of(kj * t, t)
        for hh in range(2):
            q = q_ref[0, :, hh * LANES:(hh + 1) * LANES]
            k = k_ref[0, pl.ds(start, t), hh * LANES:(hh + 1) * LANES]
            s_ref[hh] = _dot_nt(q, k)

    def consume(kj, s_ref, masked):
        start = pl.multiple_of(kj * t, t)
        v = v_ref[0, pl.ds(start, t), :]
        for hh in range(2):
            s = s_ref[hh]
            if masked:
                row_chunk = lax.broadcasted_iota(jnp.int32, (t, t), 0) // CHUNK
                col_chunk = lax.broadcasted_iota(jnp.int32, (t, t), 1) // CHUNK
                s = jnp.where(col_chunk <= row_chunk, s, NEG_INF)
            m_old = m_s[hh]
            m_new = jnp.maximum(m_old, jnp.broadcast_to(s.max(axis=-1, keepdims=True), m_old.shape))
            p = jnp.exp2(s - jnp.tile(m_new, (1, t // LANES))).astype(BF16)
            alpha = jnp.exp2(m_old - m_new)
            acc_s[hh] = jnp.tile(alpha, (1, V_PAIR_LANES // LANES)) * acc_s[hh] + _dot(p, v)
            m_s[hh] = m_new

    def body(pair, c):
        kj = 2 * pair
        produce(kj + 1, s_b)
        consume(kj, s_a, False)
        produce(kj + 2, s_a)
        consume(kj + 1, s_b, False)
        return c

    produce(0, s_a)
    lax.fori_loop(0, i // 2, body, 0)

    @pl.when(i % 2 == 0)
    def _():
        consume(i, s_a, True)

    @pl.when(i % 2 == 1)
    def _():
        produce(i, s_b)
        consume(i - 1, s_a, False)
        consume(i, s_b, True)

    outs = []
    for hh in range(2):
        acc = acc_s[hh]
        outs.append(acc[:, :LANES] / acc[:, LANES:])
    o_ref[0] = jnp.where(lane < B_V_DIM, outs[0], outs[1]).astype(o_ref.dtype)


def _mla_attn(qm, km, vm):
    bsz, seq, _ = qm.shape
    t = min(TQ_B, seq)
    return pl.pallas_call(
        functools.partial(_mla_attn_kernel, t=t),
        out_shape=jax.ShapeDtypeStruct((bsz, seq, B_HEADS * B_V_DIM), BF16),
        grid=(bsz, B_HEADS // 2, seq // t),
        in_specs=[pl.BlockSpec((1, t, 2 * LANES), lambda b, hp, i: (b, i, hp)),
                  pl.BlockSpec((1, seq, 2 * LANES), lambda b, hp, i: (b, 0, hp)),
                  pl.BlockSpec((1, seq, V_PAIR_LANES), lambda b, hp, i: (b, 0, hp))],
        out_specs=pl.BlockSpec((1, t, LANES), lambda b, hp, i: (b, i, hp)),
        scratch_shapes=[pltpu.VMEM((2, t, t), F32), pltpu.VMEM((2, t, t), F32),
                        pltpu.VMEM((2, t, LANES), F32),
                        pltpu.VMEM((2, t, V_PAIR_LANES), F32)],
        compiler_params=_params(("arbitrary",) * 3, 40),
    )(qm, km, vm)


def _merge_kernel(x_ref, ya_ref, yb_ref, ga_ref, gb_ref, mod_ref, wa_ref, wb_ref, wo_ref,
                  g1_ref, b1_ref, wrh_ref, wrl_ref, br_ref, x1_ref, h2_ref, lg_ref):
    m = mod_ref[0]
    a = _dot(ya_ref[0], wa_ref[...])
    b = _dot(yb_ref[0], wb_ref[...])
    merged = (jax.nn.sigmoid(ga_ref[0].astype(F32)) * a
              + jax.nn.sigmoid(gb_ref[0].astype(F32)) * b)
    o = _dot(merged.astype(BF16), wo_ref[...])
    x1 = _layer_norm(DEEPNORM_ALPHA * x_ref[0] + m[2:3] * o, g1_ref[...], b1_ref[...])
    h2 = x1 * (1.0 + m[4:5]) + m[3:4]
    x1_ref[0] = x1
    _store_token_tiles(h2_ref.at[0], h2)
    h_hi = h2.astype(BF16)
    h_lo = (h2 - h_hi.astype(F32)).astype(BF16)
    lg_ref[0] = (_dot(h_hi, wrh_ref[...]) + _dot(h_hi, wrl_ref[...]) + _dot(h_lo, wrh_ref[...])
                 + br_ref[...])


def _merge(x, ya, yb, ga, gb, mod3, wa, wb, wo, g1, b1, wr, br):
    wr_hi = wr.astype(BF16)
    wr_lo = (wr - wr_hi.astype(F32)).astype(BF16)
    bsz, seq, d = x.shape
    tm = min(TM_MERGE, seq)
    tok = lambda n: pl.BlockSpec((1, tm, n), lambda b, i: (b, i, 0))
    full = lambda a: pl.BlockSpec(a.shape, lambda b, i: (0,) * a.ndim)
    return pl.pallas_call(
        _merge_kernel,
        out_shape=[jax.ShapeDtypeStruct((bsz, seq, d), F32),
                   jax.ShapeDtypeStruct((bsz, seq * (d // LANES), LANES), F32),
                   jax.ShapeDtypeStruct((bsz, seq, LANES), F32)],
        grid=(bsz, seq // tm),
        in_specs=[tok(d), tok(A_WIDTH), tok(B_HEADS * B_V_DIM), tok(d), tok(d),
                  pl.BlockSpec((1, 6, d), lambda b, i: (b, 0, 0)),
                  full(wa), full(wb), full(wo), full(g1), full(b1), full(wr_hi), full(wr_lo),
                  full(br)],
        out_specs=[tok(d),
                   pl.BlockSpec((1, tm * (d // LANES), LANES), lambda b, i: (b, i, 0)),
                   tok(LANES)],
        compiler_params=_params(("arbitrary", "arbitrary"), 48),
    )(x, ya, yb, ga, gb, mod3, wa, wb, wo, g1, b1, wr_hi, wr_lo, br)


def _route_kernel(lg_ref, pos_ref, w_ref, tmap_ref, nv_ref, cnt_s, carry_s, pstart_s, *, tr, tm):
    phase = pl.program_id(0)
    i = pl.program_id(1)
    lane = lax.broadcasted_iota(jnp.int32, (tr, LANES), 1)
    work = lg_ref[...]
    sels, vals = [], []
    for _ in range(TOP_K):
        m = work.max(axis=-1, keepdims=True)
        idx = jnp.where(work == m, lane, LANES).min(axis=-1, keepdims=True)
        sel = lane == idx
        work = jnp.where(sel, -jnp.inf, work)
        sels.append(sel)
        vals.append(m)
    picked = sum(jnp.where(s, 1.0, 0.0) for s in sels)
    tile_count = jnp.broadcast_to(picked.sum(axis=0, keepdims=True), (8, LANES))

    @pl.when((phase == 0) & (i == 0))
    def _():
        cnt_s[...] = jnp.zeros_like(cnt_s)

    @pl.when(phase == 0)
    def _():
        cnt_s[...] += tile_count

    @pl.when((phase == 1) & (i == 0))
    def _():
        lane8 = lax.broadcasted_iota(jnp.int32, (8, LANES), 1)
        padded = jnp.floor((cnt_s[...] + (tm - 1)) * (1.0 / tm)) * tm
        cum = padded
        shift = 1
        while shift < N_EXPERTS:
            cum = cum + jnp.where(lane8 >= shift, pltpu.roll(cum, shift, 1), 0.0)
            shift *= 2
        pstart_s[...] = cum - padded
        carry_s[...] = jnp.zeros_like(carry_s)
        nt = tmap_ref.shape[0]
        tile_start = (lax.broadcasted_iota(jnp.int32, (nt, LANES), 0) * tm).astype(F32)
        lane_t = lax.broadcasted_iota(jnp.int32, (nt, LANES), 1)
        done = jnp.where((lane_t < N_EXPERTS) & (cum[0:1] <= tile_start), 1.0, 0.0)
        te = jnp.minimum(done.sum(axis=-1, keepdims=True), N_EXPERTS - 1.0)
        tmap_ref[...] = jnp.broadcast_to(te, (nt, LANES)).astype(jnp.int32)
        total = jnp.where(lane8 == N_EXPERTS - 1, cum, 0.0).sum(axis=-1, keepdims=True)
        row8 = lax.broadcasted_iota(jnp.int32, (8, LANES), 0)
        meta = jnp.where(row8 == 0, jnp.broadcast_to(total * (1.0 / tm), (8, LANES)),
                         jnp.where(row8 == 1, cum - padded + cnt_s[...], cum))
        nv_ref[...] = meta.astype(jnp.int32)

    @pl.when(phase == 1)
    def _():
        r = lax.broadcasted_iota(jnp.int32, (tr, tr), 0)
        c = lax.broadcasted_iota(jnp.int32, (tr, tr), 1)
        tri = jnp.where(c < r, 1.0, 0.0).astype(BF16)
        before = _dot(tri, picked.astype(BF16)) + carry_s[0:1]
        base = before + pstart_s[0:1]
        carry_s[...] += tile_count
        den = sum(jnp.exp(v - vals[0]) for v in vals)
        pos = jnp.zeros((tr, LANES), F32)
        wts = jnp.zeros((tr, LANES), F32)
        for k in range(TOP_K):
            pk = jnp.where(sels[k], base, 0.0).sum(axis=-1, keepdims=True)
            pos = jnp.where(lane == k, pk, pos)
            wts = jnp.where(lane == k, jnp.exp(vals[k] - vals[0]) / den, wts)
        pos_ref[...] = pos.astype(jnp.int32)
        w_ref[...] = wts


def _route(logits, tm):
    n = logits.shape[0]
    tr = min(TR_ROUTE, n)
    return pl.pallas_call(
        functools.partial(_route_kernel, tr=tr, tm=tm),
        out_shape=[jax.ShapeDtypeStruct((n, LANES), jnp.int32),
                   jax.ShapeDtypeStruct((n, LANES), F32),
                   jax.ShapeDtypeStruct((MAX_TILES_PAD, LANES), jnp.int32),
                   jax.ShapeDtypeStruct((8, LANES), jnp.int32)],
        grid=(2, n // tr),
        in_specs=[pl.BlockSpec((tr, LANES), lambda p, i: (i, 0))],
        out_specs=[pl.BlockSpec((tr, LANES), lambda p, i: (i * p, 0)),
                   pl.BlockSpec((tr, LANES), lambda p, i: (i * p, 0)),
                   pl.BlockSpec((MAX_TILES_PAD, LANES), lambda p, i: (0, 0)),
                   pl.BlockSpec((8, LANES), lambda p, i: (0, 0))],
        scratch_shapes=[pltpu.VMEM((8, LANES), F32)] * 3,
        compiler_params=_params(("arbitrary", "arbitrary")),
    )(logits)


def _dispatch_kernel(pos_ref, fill_ref, h_ref, xs_ref, zero_s, ztile_s, sem, fill_sem, tail_sem,
                     *, tm, nsub, tm_moe):
    def tile(ref, r):
        return ref.at[pl.ds(pl.multiple_of(r * nsub, nsub), nsub)]

    @pl.when(pl.program_id(0) == 0)
    def _():
        zero_s[...] = jnp.zeros_like(zero_s)
        ztile_s[...] = jnp.zeros_like(ztile_s)

        def fill_copy(r):
            return pltpu.make_async_copy(zero_s, tile(xs_ref, r), fill_sem)

        def tail_copy(t):
            rows = tm_moe * nsub
            return pltpu.make_async_copy(
                ztile_s, xs_ref.at[pl.ds(pl.multiple_of(t * rows, rows), rows)], tail_sem)

        def per_expert(e, total):
            lo, hi = fill_ref[LANES + e], fill_ref[2 * LANES + e]

            def one(r, c):
                fill_copy(r).start()
                return c

            lax.fori_loop(lo, hi, one, 0)
            return total + (hi - lo)

        total = lax.fori_loop(0, N_EXPERTS, per_expert, 0)
        n_tiles = xs_ref.shape[0] // (tm_moe * nsub)

        def tail_start(t, c):
            tail_copy(t).start()
            return c

        lax.fori_loop(fill_ref[0], n_tiles, tail_start, 0)

        def drain(_, c):
            fill_copy(0).wait()
            return c

        lax.fori_loop(0, total, drain, 0)

        def tail_drain(_, c):
            tail_copy(0).wait()
            return c

        lax.fori_loop(fill_ref[0], n_tiles, tail_drain, 0)

    def issue(r, c):
        for k in range(TOP_K):
            pltpu.make_async_copy(tile(h_ref, r), tile(xs_ref, pos_ref[r * TOP_K + k]), sem).start()
        return c

    lax.fori_loop(0, tm, issue, 0)
    for _ in range(TOP_K):
        pltpu.make_async_copy(h_ref, xs_ref.at[pl.ds(0, tm * nsub)], sem).wait()


def _dispatch(pos_flat, fill, h2, cap, nsub, tm_moe):
    n = h2.shape[0] // nsub
    tm = min(TM_ROWS, n)
    return pl.pallas_call(
        functools.partial(_dispatch_kernel, tm=tm, nsub=nsub, tm_moe=tm_moe),
        out_shape=jax.ShapeDtypeStruct((cap * nsub, LANES), F32),
        grid=(n // tm,),
        in_specs=[pl.BlockSpec((tm * TOP_K,), lambda i: (i,), memory_space=pltpu.SMEM),
                  pl.BlockSpec(fill.shape, lambda i: (0,), memory_space=pltpu.SMEM),
                  pl.BlockSpec((tm * nsub, LANES), lambda i: (i, 0))],
        out_specs=pl.BlockSpec(memory_space=pl.ANY),
        scratch_shapes=[pltpu.VMEM((nsub, LANES), F32), pltpu.VMEM((tm_moe * nsub, LANES), F32),
                        pltpu.SemaphoreType.DMA(()), pltpu.SemaphoreType.DMA(()),
                        pltpu.SemaphoreType.DMA(())],
        compiler_params=_params(("arbitrary",)),
    )(pos_flat, fill, h2)


def _moe_kernel(te_ref, nv_ref, x_ref, bg_ref, bu_ref, bd_ref, wg_hbm, wu_hbm, wd_hbm,
                y_ref, wf_s, wg_s, wu_s, wd_s, slot_s, sem):
    i = pl.program_id(0)
    nv = nv_ref[0]
    last = te_ref.shape[0] - 1

    def weight_copies(e, slot):
        return [pltpu.make_async_copy(w.at[e], wf_s.at[slot, j], sem.at[slot])
                for j, w in enumerate((wg_hbm, wu_hbm, wd_hbm))]

    @pl.when(i < nv)
    def _():
        e = te_ref[i]

        @pl.when(i == 0)
        def _():
            slot_s[0] = 0
            for c in weight_copies(e, 0):
                c.start()

        @pl.when((i == 0) | (e != te_ref[jnp.maximum(i - 1, 0)]))
        def _():
            slot = slot_s[0]
            for c in weight_copies(e, slot):
                c.wait()
            nxt = lax.while_loop(lambda j: (j < nv) & (te_ref[jnp.minimum(j, last)] == e),
                                 lambda j: j + 1, i + 1)

            @pl.when(nxt < nv)
            def _():
                for c in weight_copies(te_ref[jnp.minimum(nxt, last)], 1 - slot):
                    c.start()

            wg_s[...] = wf_s[slot, 0].astype(BF16)
            wu_s[...] = wf_s[slot, 1].astype(BF16)
            wd_s[...] = wf_s[slot, 2].astype(BF16)
            slot_s[0] = 1 - slot

        xb = _load_token_tiles(x_ref, wg_s.shape[0] // LANES).astype(BF16)
        g = _dot(xb, wg_s[...]) + bg_ref[0]
        u = _dot(xb, wu_s[...]) + bu_ref[0]
        g = jnp.minimum(g, SWIGLU_LIMIT)
        u = jnp.clip(u, -SWIGLU_LIMIT, SWIGLU_LIMIT)
        h = (u + 1.0) * (g * jax.nn.sigmoid(SWIGLU_ALPHA * g))
        _store_token_tiles(y_ref, _dot(h.astype(BF16), wd_s[...]) + bd_ref[0])

    @pl.when(i >= nv)
    def _():
        y_ref[...] = jnp.zeros_like(y_ref)


def _moe(tile_expert, n_valid, xs, w_gate, b_gate, w_up, b_up, w_down, b_down, tm):
    e, d, f = w_gate.shape
    assert d == f
    nsub = d // LANES
    cap = xs.shape[0] // nsub
    n_tiles = cap // tm

    def tile(i, te, nv):
        return jnp.minimum(i, nv[0] - 1)

    exp3 = lambda i, te, nv: (te[tile(i, te, nv)], 0, 0)
    grid_spec = pltpu.PrefetchScalarGridSpec(
        num_scalar_prefetch=2,
        grid=(n_tiles,),
        in_specs=[pl.BlockSpec((tm * nsub, LANES), lambda i, te, nv: (tile(i, te, nv), 0)),
                  pl.BlockSpec((1, 1, f), exp3), pl.BlockSpec((1, 1, f), exp3),
                  pl.BlockSpec((1, 1, d), exp3),
                  pl.BlockSpec(memory_space=pl.ANY), pl.BlockSpec(memory_space=pl.ANY),
                  pl.BlockSpec(memory_space=pl.ANY)],
        out_specs=pl.BlockSpec((tm * nsub, LANES), lambda i, te, nv: (i, 0)),
        scratch_shapes=[pltpu.VMEM((2, 3, d, f), F32),
                        pltpu.VMEM((d, f), BF16), pltpu.VMEM((d, f), BF16), pltpu.VMEM((f, d), BF16),
                        pltpu.SMEM((1,), jnp.int32), pltpu.SemaphoreType.DMA((2,))],
    )
    return pl.pallas_call(
        _moe_kernel,
        out_shape=jax.ShapeDtypeStruct((cap * nsub, LANES), F32),
        grid_spec=grid_spec,
        compiler_params=_params(("arbitrary",), 56),
    )(tile_expert, n_valid, xs, b_gate.reshape(e, 1, f), b_up.reshape(e, 1, f),
      b_down.reshape(e, 1, d), w_gate, w_up, w_down)


def _combine_kernel(pos_ref, posn_ref, w_ref, x1_ref, mod_ref, g_ref, b_ref, y_ref, o_ref,
                    buf, sem, *, tm):
    i = pl.program_id(0)
    n = pl.num_programs(0)
    nsub = x1_ref.shape[1] // LANES

    def issue(p_ref, slot):
        def body(r, c):
            for k in range(TOP_K):
                src = pl.ds(pl.multiple_of(p_ref[r * TOP_K + k] * nsub, nsub), nsub)
                dst = pl.ds(pl.multiple_of(r * nsub, nsub), nsub)
                pltpu.make_async_copy(y_ref.at[src], buf.at[slot, k, dst], sem.at[slot]).start()
            return c
        lax.fori_loop(0, tm, body, 0)

    @pl.when(i == 0)
    def _():
        issue(pos_ref, 0)

    @pl.when(i + 1 < n)
    def _():
        issue(posn_ref, (i + 1) % 2)

    slot = i % 2
    for k in range(TOP_K):
        pltpu.make_async_copy(y_ref.at[pl.ds(0, tm * nsub)], buf.at[slot, k], sem.at[slot]).wait()
    w = w_ref[...]
    f = jnp.zeros(x1_ref.shape, F32)
    for k in range(TOP_K):
        f = f + w[:, k:k + 1] * _load_token_tiles(buf.at[slot, k], nsub)
    m = mod_ref[0]
    o_ref[...] = _layer_norm(DEEPNORM_ALPHA * x1_ref[...] + m[5:6] * f, g_ref[...], b_ref[...])


def _combine(pos_flat, w, x1, mod3, g2, b2, y, seq):
    n, d = x1.shape
    tm = min(TM_ROWS, seq)
    nsteps = n // tm
    per_seq = seq // tm
    return pl.pallas_call(
        functools.partial(_combine_kernel, tm=tm),
        out_shape=jax.ShapeDtypeStruct((n, d), F32),
        grid=(nsteps,),
        in_specs=[pl.BlockSpec((tm * TOP_K,), lambda i: (i,), memory_space=pltpu.SMEM),
                  pl.BlockSpec((tm * TOP_K,), lambda i: (jnp.minimum(i + 1, nsteps - 1),),
                               memory_space=pltpu.SMEM),
                  pl.BlockSpec((tm, LANES), lambda i: (i, 0)),
                  pl.BlockSpec((tm, d), lambda i: (i, 0)),
                  pl.BlockSpec((1, 6, d), lambda i: (i // per_seq, 0, 0)),
                  pl.BlockSpec(g2.shape, lambda i: (0, 0)),
                  pl.BlockSpec(b2.shape, lambda i: (0, 0)),
                  pl.BlockSpec(memory_space=pl.ANY)],
        out_specs=pl.BlockSpec((tm, d), lambda i: (i, 0)),
        scratch_shapes=[pltpu.VMEM((2, TOP_K, tm * (d // LANES), LANES), F32),
                        pltpu.SemaphoreType.DMA((2,))],
        compiler_params=_params(("arbitrary",), 32),
    )(pos_flat, pos_flat, w, x1, mod3, g2, b2, y)


def _prep_w_in(w_in):
    d = w_in.shape[0]
    a3 = 3 * A_WIDTH
    lat = a3 + B_Q_LORA + B_KV_LORA
    z = lambda n: jnp.zeros((d, n), w_in.dtype)
    return jnp.concatenate([
        w_in[:, :A_WIDTH] * (A_HEAD_DIM ** -0.5 * LOG2_E),
        w_in[:, A_WIDTH:lat],
        z(B_NOPE_DIM), w_in[:, lat:lat + B_ROPE_DIM], z(LANES - B_NOPE_DIM - B_ROPE_DIM),
        w_in[:, lat + B_ROPE_DIM:]], axis=1).astype(BF16)


def _prep_w_uq(w_uq):
    r = w_uq.shape[0]
    w = w_uq.reshape(r, B_HEADS, B_QK_DIM)
    w = jnp.pad(w, ((0, 0), (0, 0), (0, LANES - B_QK_DIM)))
    return w.reshape(r, B_HEADS * LANES).astype(BF16)


def _prep_w_ukv(w_ukv):
    r = w_ukv.shape[0]
    w = w_ukv.reshape(r, B_HEADS, B_NOPE_DIM + B_V_DIM)
    wk = jnp.pad(w[:, :, :B_NOPE_DIM], ((0, 0), (0, 0), (0, LANES - B_NOPE_DIM)))
    wv = w[:, :, B_NOPE_DIM:].reshape(r, B_HEADS // 2, 2 * B_V_DIM)
    wv = jnp.pad(wv, ((0, 0), (0, 0), (0, V_PAIR_LANES - 2 * B_V_DIM)))
    return (wk.reshape(r, B_HEADS * LANES).astype(BF16),
            wv.reshape(r, B_HEADS // 2 * V_PAIR_LANES).astype(BF16))


def _rope_tables(positions):
    half = B_ROPE_DIM // 2
    freqs = ROPE_THETA ** (-jnp.arange(half, dtype=F32) / half)
    ang_t = positions.astype(F32)[:, None, :] * freqs[None, :, None]
    cos, sin = jnp.swapaxes(jnp.cos(ang_t), 1, 2), jnp.swapaxes(jnp.sin(ang_t), 1, 2)
    ang = cos
    z = lambda n: jnp.zeros(ang.shape[:2] + (n,), F32)
    tail = LANES - B_NOPE_DIM - B_ROPE_DIM
    cos_t = jnp.concatenate([jnp.ones(ang.shape[:2] + (B_NOPE_DIM,), F32), cos, cos, z(tail)], -1)
    sin_a = jnp.concatenate([z(B_NOPE_DIM), -sin, z(half), z(tail)], -1)
    sin_b = jnp.concatenate([z(B_NOPE_DIM), z(half), sin, z(tail)], -1)
    return cos_t, sin_a, sin_b


def kernel(x, c, positions, w_ada, b_ada, w_in, rms_q, w_uq, rms_kv, w_ukv, rel_bias,
           w_branch_a, w_branch_b, w_out, ln1_g, ln1_b, w_router, b_router,
           w_gate, b_gate, w_up, b_up, w_down, b_down, ln2_g, ln2_b):
    bsz, seq, d = x.shape
    n_tok = bsz * seq
    assert w_ada.shape[0] == DEPTH == 1
    row = lambda v: v.reshape(1, -1)

    mod3 = _ada(c, w_ada[0], b_ada[0]).reshape(bsz, 6, d)

    qk, va, cq, ckv, kr, ga, gb = _inproj(x, mod3, _prep_w_in(w_in[0]))
    ya = _attn_a(qk, va, _bias_table(rel_bias[0], min(TQ_A, seq)))
    wk, wv = _prep_w_ukv(w_ukv[0])
    qm, km, vm = _mla_proj(cq, ckv, kr, *_rope_tables(positions), row(rms_q[0]), row(rms_kv[0]),
                           _prep_w_uq(w_uq[0]), wk, wv)
    yb = _mla_attn(qm, km, vm)

    wr = jnp.pad(w_router[0], ((0, 0), (0, LANES - N_EXPERTS)))
    br = jnp.pad(b_router[0], (0, LANES - N_EXPERTS), constant_values=NEG_INF)
    x1, h2, logits = _merge(x, ya, yb, ga, gb, mod3,
                            w_branch_a[0].astype(BF16), w_branch_b[0].astype(BF16),
                            w_out[0].astype(BF16), row(ln1_g[0]), row(ln1_b[0]), wr, row(br))

    tm = TM_MOE
    n_tiles = -(-(n_tok * TOP_K) // tm) + N_EXPERTS
    assert n_tiles <= MAX_TILES_PAD
    pos, wts, tmap, meta = _route(logits.reshape(n_tok, LANES), tm)
    pos_flat = pos[:, :TOP_K].reshape(-1)
    fill = meta[:3].reshape(-1)
    nsub = d // LANES
    xs = _dispatch(pos_flat, fill, h2.reshape(n_tok * nsub, LANES), n_tiles * tm, nsub, tm)
    y = _moe(tmap[:n_tiles, 0], meta[0, :1], xs, w_gate[0], b_gate[0], w_up[0], b_up[0],
             w_down[0], b_down[0], tm)
    out = _combine(pos_flat, wts, x1.reshape(n_tok, d), mod3, row(ln2_g[0]), row(ln2_b[0]), y, seq)
    return out.reshape(bsz, seq, d)
```

```python
import functools

import jax
import jax.numpy as jnp
from jax import lax
from jax.experimental import pallas as pl
from jax.experimental.pallas import tpu as pltpu

F32 = jnp.float32
BF16 = jnp.bfloat16

CHUNK = 64
A_HEADS = 8
A_HEAD_DIM = 64
A_LEFT_CHUNKS = 8
A_MAX_REL = 128
A_WIDTH = A_HEADS * A_HEAD_DIM
B_HEADS = 8
B_NOPE_DIM = 64
B_ROPE_DIM = 32
B_V_DIM = 64
B_Q_LORA = 384
B_KV_LORA = 256
B_QK_DIM = B_NOPE_DIM + B_ROPE_DIM
ROPE_THETA = 10000.0
N_EXPERTS = 32
TOP_K = 4
SWIGLU_LIMIT = 7.0
SWIGLU_ALPHA = 1.702
DEPTH = 1
DEEPNORM_ALPHA = (2.0 * DEPTH) ** 0.25
LN_EPS = 1e-5
RMS_EPS = 1e-6
NEG_INF = -1e30
LOG2_E = 1.4426950408889634

LANES = 128
MXU_COLS = 256
V_PAIR_LANES = MXU_COLS

TM_PROJ = 512
TQ_A = 256
A_QBLOCKS = 2
TQ_B = 512
TM_MERGE = 512
TR_ROUTE = 1024
TM_MOE = 256
TM_ROWS = 256
MAX_TILES_PAD = 512


def _params(sem, vmem_mb=None):
    return pltpu.CompilerParams(
        dimension_semantics=sem,
        vmem_limit_bytes=None if vmem_mb is None else vmem_mb << 20)


def _dot(a, b):
    return jnp.dot(a, b, preferred_element_type=F32)


def _dot_nt(a, b):
    return lax.dot_general(a, b, (((1,), (1,)), ((), ())), preferred_element_type=F32)


def _store_token_tiles(ref, x):
    rows, d = x.shape
    nsub = d // LANES
    for s in range(nsub):
        ref[pl.ds(s, rows, stride=nsub), :] = x[:, s * LANES:(s + 1) * LANES]


def _load_token_tiles(ref, nsub):
    rows = ref.shape[0] // nsub
    return jnp.concatenate([ref[pl.ds(s, rows, stride=nsub), :] for s in range(nsub)], axis=1)


def _layer_norm(x, g, b):
    mu = jnp.mean(x, axis=-1, keepdims=True)
    xc = x - mu
    var = jnp.mean(xc * xc, axis=-1, keepdims=True)
    return xc * lax.rsqrt(var + LN_EPS) * g + b


def _ada_kernel(c_ref, w_ref, b_ref, o_ref):
    c = c_ref[...]
    sc = c * jax.nn.sigmoid(c)
    o_ref[...] = jnp.dot(sc, w_ref[...], preferred_element_type=F32,
                         precision=lax.Precision.HIGHEST) + b_ref[...]


def _ada(c, w_ada, b_ada):
    bsz, d = c.shape
    m = w_ada.shape[1]
    tn = 1024
    return pl.pallas_call(
        _ada_kernel,
        out_shape=jax.ShapeDtypeStruct((bsz, m), F32),
        grid=(m // tn,),
        in_specs=[pl.BlockSpec((bsz, d), lambda j: (0, 0)),
                  pl.BlockSpec((d, tn), lambda j: (0, j)),
                  pl.BlockSpec((1, tn), lambda j: (0, j))],
        out_specs=pl.BlockSpec((bsz, tn), lambda j: (0, j)),
        compiler_params=_params(("arbitrary",)),
    )(c, w_ada, b_ada.reshape(1, m))


def _inproj_kernel(x_ref, mod_ref, w_ref, qk_ref, va_ref, cq_ref, ckv_ref, kr_ref, ga_ref, gb_ref):
    m = mod_ref[0]
    h = (x_ref[0] * (1.0 + m[1:2]) + m[0:1]).astype(BF16)
    col = 0
    for ref in (qk_ref, None, cq_ref, ckv_ref, kr_ref, ga_ref, gb_ref):
        if ref is None:
            v = _dot(h, w_ref[:, col:col + A_WIDTH]).astype(BF16)
            ones_blk = jnp.ones((v.shape[0], LANES), BF16)
            parts = []
            for hp in range(A_WIDTH // LANES):
                parts += [v[:, hp * LANES:(hp + 1) * LANES], ones_blk]
            va_ref[0] = jnp.concatenate(parts, axis=1)
            col += A_WIDTH
            continue
        n = ref.shape[-1]
        ref[0] = _dot(h, w_ref[:, col:col + n]).astype(ref.dtype)
        col += n


def _inproj(x, mod3, w):
    bsz, seq, d = x.shape
    tm = min(TM_PROJ, seq)
    widths = (2 * A_WIDTH, A_WIDTH // LANES * V_PAIR_LANES, B_Q_LORA, B_KV_LORA, LANES, d, d)
    dtypes = (BF16, BF16, BF16, BF16, F32, BF16, BF16)
    assert sum(widths) - widths[1] + A_WIDTH == w.shape[1]
    return pl.pallas_call(
        _inproj_kernel,
        out_shape=[jax.ShapeDtypeStruct((bsz, seq, n), dt) for n, dt in zip(widths, dtypes)],
        grid=(bsz, seq // tm),
        in_specs=[pl.BlockSpec((1, tm, d), lambda b, i: (b, i, 0)),
                  pl.BlockSpec((1, 6, d), lambda b, i: (b, 0, 0)),
                  pl.BlockSpec(w.shape, lambda b, i: (0, 0))],
        out_specs=[pl.BlockSpec((1, tm, n), lambda b, i: (b, i, 0)) for n in widths],
        compiler_params=_params(("arbitrary", "arbitrary"), 48),
    )(x, mod3, w)


def _attn_a_kernel(q_ref, k_ref, v_ref, bias_ref, o_ref, *, tq, nd):
    band = nd * tq
    lane = lax.broadcasted_iota(jnp.int32, (1, LANES), 1)
    for sub in range(A_QBLOCKS):
        i = pl.program_id(2) * A_QBLOCKS + sub
        start = pl.multiple_of(jnp.maximum(i - (nd - 1), 0) * tq, tq)
        variant = jnp.minimum(i, nd - 1)
        q = q_ref[0, sub * tq:(sub + 1) * tq, :]
        k = k_ref[0, pl.ds(start, band), :]
        v = v_ref[0, pl.ds(start, band), :]
        zero = jnp.zeros_like(k)
        kk = jnp.concatenate([jnp.where(lane < A_HEAD_DIM, k, zero),
                              jnp.where(lane >= A_HEAD_DIM, k, zero)], axis=0)
        s = _dot_nt(q, kk)
        ps = []
        for hh in range(2):
            sh = s[:, hh * band:(hh + 1) * band] + bias_ref[hh, variant]
            m = sh.max(axis=-1, keepdims=True)
            ps.append(jnp.exp2(sh - m).astype(BF16))
        acc = _dot(jnp.concatenate(ps, axis=0), v)
        out = acc[:, :LANES] / acc[:, LANES:]
        o_ref[0, sub * tq:(sub + 1) * tq, :] = jnp.where(
            lane < A_HEAD_DIM, out[:tq], out[tq:]).astype(o_ref.dtype)


def _attn_a(qk, va, bias_tab):
    bsz, seq, _ = qk.shape
    tq = min(TQ_A, seq)
    nd = bias_tab.shape[1]
    assert seq >= nd * tq
    ncb = A_WIDTH // LANES
    return pl.pallas_call(
        functools.partial(_attn_a_kernel, tq=tq, nd=nd),
        out_shape=jax.ShapeDtypeStruct((bsz, seq, A_WIDTH), BF16),
        grid=(ncb, bsz, seq // (A_QBLOCKS * tq)),
        in_specs=[pl.BlockSpec((1, A_QBLOCKS * tq, LANES), lambda hp, b, i: (b, i, hp)),
                  pl.BlockSpec((1, seq, LANES), lambda hp, b, i: (b, 0, ncb + hp)),
                  pl.BlockSpec((1, seq, V_PAIR_LANES), lambda hp, b, i: (b, 0, hp)),
                  pl.BlockSpec((2, nd, tq, nd * tq), lambda hp, b, i: (hp, 0, 0, 0))],
        out_specs=pl.BlockSpec((1, A_QBLOCKS * tq, LANES), lambda hp, b, i: (b, i, hp)),
        compiler_params=_params(("arbitrary",) * 3, 40),
    )(qk, qk, va, bias_tab)


def _bias_table(rel_bias, tq):
    nd = (A_LEFT_CHUNKS * CHUNK + tq - 1) // tq + 1
    width = nd * tq
    period = width + tq
    u = jnp.arange(period)
    dist = jnp.where(u < width, (nd - 1) * tq - u, (nd - 1) * tq + period - u)
    vec = rel_bias[:, jnp.clip(dist, -A_MAX_REL, A_MAX_REL) + A_MAX_REL].astype(F32)
    toep = jnp.tile(vec, (1, tq))[:, :tq * (period - 1)].reshape(-1, tq, period - 1)[:, :, :width]
    qi = jnp.arange(tq)[:, None]
    x = jnp.arange(width)[None, :]
    dchunk = qi // CHUNK - x // CHUNK + (nd - 1) * (tq // CHUNK)
    valid = (dchunk >= 0) & (dchunk <= A_LEFT_CHUNKS)
    table = jnp.where(valid[None], toep * LOG2_E, NEG_INF)
    variants = []
    for v in range(nd):
        cut = (nd - 1 - v) * tq
        variants.append(jnp.pad(table[:, :, cut:], ((0, 0), (0, 0), (0, cut)),
                                constant_values=NEG_INF))
    return jnp.stack(variants, axis=1)


def _mla_proj_kernel(cq_ref, ckv_ref, kr_ref, cos_ref, sa_ref, sb_ref, rq_ref, rkv_ref,
                     wq_ref, wk_ref, wv_ref, q_out, k_out, v_out):
    def rms(x, g):
        xf = x.astype(F32)
        ms = jnp.mean(xf * xf, axis=-1, keepdims=True)
        return (xf * lax.rsqrt(ms + RMS_EPS) * g).astype(BF16)

    cqn = rms(cq_ref[0], rq_ref[...])
    ckvn = rms(ckv_ref[0], rkv_ref[...])
    q = _dot(cqn, wq_ref[...])
    kn = _dot(ckvn, wk_ref[...])
    vlane = lax.broadcasted_iota(jnp.int32, (1, v_out.shape[-1]), 1)
    ones_lane = jnp.where(vlane % V_PAIR_LANES >= 2 * B_V_DIM, 1.0, 0.0)
    v_out[0] = (_dot(ckvn, wv_ref[...]) + ones_lane).astype(v_out.dtype)
    cos_t, sin_a, sin_b = cos_ref[0], sa_ref[0], sb_ref[0]

    def rope(x):
        return (x * cos_t + pltpu.roll(x, LANES - B_ROPE_DIM // 2, 1) * sin_a
                + pltpu.roll(x, B_ROPE_DIM // 2, 1) * sin_b)

    kpe = rope(kr_ref[0])
    scale = B_QK_DIM ** -0.5 * LOG2_E
    for h in range(B_HEADS):
        sl = slice(h * LANES, (h + 1) * LANES)
        q_out[0, :, sl] = (rope(q[:, sl]) * scale).astype(q_out.dtype)
        k_out[0, :, sl] = (kn[:, sl] + kpe).astype(k_out.dtype)


def _mla_proj(cq, ckv, kr, cos_t, sin_a, sin_b, rms_q, rms_kv, wq, wk, wv):
    bsz, seq, _ = cq.shape
    tm = min(TM_PROJ, seq)
    tok = lambda n: pl.BlockSpec((1, tm, n), lambda b, i: (b, i, 0))
    full = lambda a: pl.BlockSpec(a.shape, lambda b, i: (0,) * a.ndim)
    return pl.pallas_call(
        _mla_proj_kernel,
        out_shape=[jax.ShapeDtypeStruct((bsz, seq, B_HEADS * LANES), BF16),
                   jax.ShapeDtypeStruct((bsz, seq, B_HEADS * LANES), BF16),
                   jax.ShapeDtypeStruct((bsz, seq, wv.shape[1]), BF16)],
        grid=(bsz, seq // tm),
        in_specs=[tok(B_Q_LORA), tok(B_KV_LORA), tok(LANES), tok(LANES), tok(LANES), tok(LANES),
                  full(rms_q), full(rms_kv), full(wq), full(wk), full(wv)],
        out_specs=[tok(B_HEADS * LANES), tok(B_HEADS * LANES), tok(wv.shape[1])],
        compiler_params=_params(("arbitrary", "arbitrary"), 32),
    )(cq, ckv, kr, cos_t, sin_a, sin_b, rms_q, rms_kv, wq, wk, wv)


def _mla_attn_kernel(q_ref, k_ref, v_ref, o_ref, s_a, s_b, m_s, acc_s, *, t):
    i = pl.program_id(2)
    lane = lax.broadcasted_iota(jnp.int32, (1, LANES), 1)
    m_s[...] = jnp.full(m_s.shape, NEG_INF, F32)
    acc_s[...] = jnp.zeros(acc_s.shape, F32)

    def produce(kj, s_ref):
        start = pl.multiple_of(kj * t, t)
        for hh in range(2):
            q = q_ref[0, :, hh * LANES:(hh + 1) * LANES]
            k = k_ref[0, pl.ds(start, t), hh * LANES:(hh + 1) * LANES]
            s_ref[hh] = _dot_nt(q, k)

    def consume(kj, s_ref, masked):
        start = pl.multiple_of(kj * t, t)
        v = v_ref[0, pl.ds(start, t), :]
        for hh in range(2):
            s = s_ref[hh]
            if masked:
                row_chunk = lax.broadcasted_iota(jnp.int32, (t, t), 0) // CHUNK
                col_chunk = lax.broadcasted_iota(jnp.int32, (t, t), 1) // CHUNK
                s = jnp.where(col_chunk <= row_chunk, s, NEG_INF)
            m_old = m_s[hh]
            m_new = jnp.maximum(m_old, jnp.broadcast_to(s.max(axis=-1, keepdims=True), m_old.shape))
            p = jnp.exp2(s - jnp.tile(m_new, (1, t // LANES))).astype(BF16)
            alpha = jnp.exp2(m_old - m_new)
            acc_s[hh] = jnp.tile(alpha, (1, V_PAIR_LANES // LANES)) * acc_s[hh] + _dot(p, v)
            m_s[hh] = m_new

    def body(pair, c):
        kj = 2 * pair
        produce(kj + 1, s_b)
        consume(kj, s_a, False)
        produce(kj + 2, s_a)
        consume(kj + 1, s_b, False)
        return c

    produce(0, s_a)
    lax.fori_loop(0, i // 2, body, 0)

    @pl.when(i % 2 == 0)
    def _():
        consume(i, s_a, True)

    @pl.when(i % 2 == 1)
    def _():
        produce(i, s_b)
        consume(i - 1, s_a, False)
        consume(i, s_b, True)

    outs = []
    for hh in range(2):
        acc = acc_s[hh]
        outs.append(acc[:, :LANES] / acc[:, LANES:])
    o_ref[0] = jnp.where(lane < B_V_DIM, outs[0], outs[1]).astype(o_ref.dtype)


def _mla_attn(qm, km, vm):
    bsz, seq, _ = qm.shape
    t = min(TQ_B, seq)
    return pl.pallas_call(
        functools.partial(_mla_attn_kernel, t=t),
        out_shape=jax.ShapeDtypeStruct((bsz, seq, B_HEADS * B_V_DIM), BF16),
        grid=(bsz, B_HEADS // 2, seq // t),
        in_specs=[pl.BlockSpec((1, t, 2 * LANES), lambda b, hp, i: (b, i, hp)),
                  pl.BlockSpec((1, seq, 2 * LANES), lambda b, hp, i: (b, 0, hp)),
                  pl.BlockSpec((1, seq, V_PAIR_LANES), lambda b, hp, i: (b, 0, hp))],
        out_specs=pl.BlockSpec((1, t, LANES), lambda b, hp, i: (b, i, hp)),
        scratch_shapes=[pltpu.VMEM((2, t, t), F32), pltpu.VMEM((2, t, t), F32),
                        pltpu.VMEM((2, t, LANES), F32),
                        pltpu.VMEM((2, t, V_PAIR_LANES), F32)],
        compiler_params=_params(("arbitrary",) * 3, 40),
    )(qm, km, vm)


def _merge_kernel(x_ref, ya_ref, yb_ref, ga_ref, gb_ref, mod_ref, wa_ref, wb_ref, wo_ref,
                  g1_ref, b1_ref, wrh_ref, wrl_ref, br_ref, x1_ref, h2_ref, lg_ref):
    m = mod_ref[0]
    a = _dot(ya_ref[0], wa_ref[...])
    b = _dot(yb_ref[0], wb_ref[...])
    merged = (jax.nn.sigmoid(ga_ref[0].astype(F32)) * a
              + jax.nn.sigmoid(gb_ref[0].astype(F32)) * b)
    o = _dot(merged.astype(BF16), wo_ref[...])
    x1 = _layer_norm(DEEPNORM_ALPHA * x_ref[0] + m[2:3] * o, g1_ref[...], b1_ref[...])
    h2 = x1 * (1.0 + m[4:5]) + m[3:4]
    x1_ref[0] = x1
    _store_token_tiles(h2_ref.at[0], h2)
    h_hi = h2.astype(BF16)
    h_lo = (h2 - h_hi.astype(F32)).astype(BF16)
    lg_ref[0] = (_dot(h_hi, wrh_ref[...]) + _dot(h_hi, wrl_ref[...]) + _dot(h_lo, wrh_ref[...])
                 + br_ref[...])


def _merge(x, ya, yb, ga, gb, mod3, wa, wb, wo, g1, b1, wr, br):
    wr_hi = wr.astype(BF16)
    wr_lo = (wr - wr_hi.astype(F32)).astype(BF16)
    bsz, seq, d = x.shape
    tm = min(TM_MERGE, seq)
    tok = lambda n: pl.BlockSpec((1, tm, n), lambda b, i: (b, i, 0))
    full = lambda a: pl.BlockSpec(a.shape, lambda b, i: (0,) * a.ndim)
    return pl.pallas_call(
        _merge_kernel,
        out_shape=[jax.ShapeDtypeStruct((bsz, seq, d), F32),
                   jax.ShapeDtypeStruct((bsz, seq * (d // LANES), LANES), F32),
                   jax.ShapeDtypeStruct((bsz, seq, LANES), F32)],
        grid=(bsz, seq // tm),
        in_specs=[tok(d), tok(A_WIDTH), tok(B_HEADS * B_V_DIM), tok(d), tok(d),
                  pl.BlockSpec((1, 6, d), lambda b, i: (b, 0, 0)),
                  full(wa), full(wb), full(wo), full(g1), full(b1), full(wr_hi), full(wr_lo),
                  full(br)],
        out_specs=[tok(d),
                   pl.BlockSpec((1, tm * (d // LANES), LANES), lambda b, i: (b, i, 0)),
                   tok(LANES)],
        compiler_params=_params(("arbitrary", "arbitrary"), 48),
    )(x, ya, yb, ga, gb, mod3, wa, wb, wo, g1, b1, wr_hi, wr_lo, br)


def _route_kernel(lg_ref, pos_ref, w_ref, tmap_ref, nv_ref, cnt_s, carry_s, pstart_s, *, tr, tm):
    phase = pl.program_id(0)
    i = pl.program_id(1)
    lane = lax.broadcasted_iota(jnp.int32, (tr, LANES), 1)
    work = lg_ref[...]
    sels, vals = [], []
    for _ in range(TOP_K):
        m = work.max(axis=-1, keepdims=True)
        idx = jnp.where(work == m, lane, LANES).min(axis=-1, keepdims=True)
        sel = lane == idx
        work = jnp.where(sel, -jnp.inf, work)
        sels.append(sel)
        vals.append(m)
    picked = sum(jnp.where(s, 1.0, 0.0) for s in sels)
    tile_count = jnp.broadcast_to(picked.sum(axis=0, keepdims=True), (8, LANES))

    @pl.when((phase == 0) & (i == 0))
    def _():
        cnt_s[...] = jnp.zeros_like(cnt_s)

    @pl.when(phase == 0)
    def _():
        cnt_s[...] += tile_count

    @pl.when((phase == 1) & (i == 0))
    def _():
        lane8 = lax.broadcasted_iota(jnp.int32, (8, LANES), 1)
        padded = jnp.floor((cnt_s[...] + (tm - 1)) * (1.0 / tm)) * tm
        cum = padded
        shift = 1
        while shift < N_EXPERTS:
            cum = cum + jnp.where(lane8 >= shift, pltpu.roll(cum, shift, 1), 0.0)
            shift *= 2
        pstart_s[...] = cum - padded
        carry_s[...] = jnp.zeros_like(carry_s)
        nt = tmap_ref.shape[0]
        tile_start = (lax.broadcasted_iota(jnp.int32, (nt, LANES), 0) * tm).astype(F32)
        lane_t = lax.broadcasted_iota(jnp.int32, (nt, LANES), 1)
        done = jnp.where((lane_t < N_EXPERTS) & (cum[0:1] <= tile_start), 1.0, 0.0)
        te = jnp.minimum(done.sum(axis=-1, keepdims=True), N_EXPERTS - 1.0)
        tmap_ref[...] = jnp.broadcast_to(te, (nt, LANES)).astype(jnp.int32)
        total = jnp.where(lane8 == N_EXPERTS - 1, cum, 0.0).sum(axis=-1, keepdims=True)
        row8 = lax.broadcasted_iota(jnp.int32, (8, LANES), 0)
        meta = jnp.where(row8 == 0, jnp.broadcast_to(total * (1.0 / tm), (8, LANES)),
                         jnp.where(row8 == 1, cum - padded + cnt_s[...], cum))
        nv_ref[...] = meta.astype(jnp.int32)

    @pl.when(phase == 1)
    def _():
        r = lax.broadcasted_iota(jnp.int32, (tr, tr), 0)
        c = lax.broadcasted_iota(jnp.int32, (tr, tr), 1)
        tri = jnp.where(c < r, 1.0, 0.0).astype(BF16)
        before = _dot(tri, picked.astype(BF16)) + carry_s[0:1]
        base = before + pstart_s[0:1]
        carry_s[...] += tile_count
        den = sum(jnp.exp(v - vals[0]) for v in vals)
        pos = jnp.zeros((tr, LANES), F32)
        wts = jnp.zeros((tr, LANES), F32)
        for k in range(TOP_K):
            pk = jnp.where(sels[k], base, 0.0).sum(axis=-1, keepdims=True)
            pos = jnp.where(lane == k, pk, pos)
            wts = jnp.where(lane == k, jnp.exp(vals[k] - vals[0]) / den, wts)
        pos_ref[...] = pos.astype(jnp.int32)
        w_ref[...] = wts


def _route(logits, tm):
    n = logits.shape[0]
    tr = min(TR_ROUTE, n)
    return pl.pallas_call(
        functools.partial(_route_kernel, tr=tr, tm=tm),
        out_shape=[jax.ShapeDtypeStruct((n, LANES), jnp.int32),
                   jax.ShapeDtypeStruct((n, LANES), F32),
                   jax.ShapeDtypeStruct((MAX_TILES_PAD, LANES), jnp.int32),
                   jax.ShapeDtypeStruct((8, LANES), jnp.int32)],
        grid=(2, n // tr),
        in_specs=[pl.BlockSpec((tr, LANES), lambda p, i: (i, 0))],
        out_specs=[pl.BlockSpec((tr, LANES), lambda p, i: (i * p, 0)),
                   pl.BlockSpec((tr, LANES), lambda p, i: (i * p, 0)),
                   pl.BlockSpec((MAX_TILES_PAD, LANES), lambda p, i: (0, 0)),
                   pl.BlockSpec((8, LANES), lambda p, i: (0, 0))],
        scratch_shapes=[pltpu.VMEM((8, LANES), F32)] * 3,
        compiler_params=_params(("arbitrary", "arbitrary")),
    )(logits)


def _dispatch_kernel(pos_ref, fill_ref, h_ref, xs_ref, zero_s, ztile_s, sem, fill_sem, tail_sem,
                     *, tm, nsub, tm_moe):
    def tile(ref, r):
        return ref.at[pl.ds(pl.multiple_of(r * nsub, nsub), nsub)]

    @pl.when(pl.program_id(0) == 0)
    def _():
        zero_s[...] = jnp.zeros_like(zero_s)
        ztile_s[...] = jnp.zeros_like(ztile_s)

        def fill_copy(r):
            return pltpu.make_async_copy(zero_s, tile(xs_ref, r), fill_sem)

        def tail_copy(t):
            rows = tm_moe * nsub
            return pltpu.make_async_copy(
                ztile_s, xs_ref.at[pl.ds(pl.multiple_of(t * rows, rows), rows)], tail_sem)

        def per_expert(e, total):
            lo, hi = fill_ref[LANES + e], fill_ref[2 * LANES + e]

            def one(r, c):
                fill_copy(r).start()
                return c

            lax.fori_loop(lo, hi, one, 0)
            return total + (hi - lo)

        total = lax.fori_loop(0, N_EXPERTS, per_expert, 0)
        n_tiles = xs_ref.shape[0] // (tm_moe * nsub)

        def tail_start(t, c):
            tail_copy(t).start()
            return c

        lax.fori_loop(fill_ref[0], n_tiles, tail_start, 0)

        def drain(_, c):
            fill_copy(0).wait()
            return c

        lax.fori_loop(0, total, drain, 0)

        def tail_drain(_, c):
            tail_copy(0).wait()
            return c

        lax.fori_loop(fill_ref[0], n_tiles, tail_drain, 0)

    def issue(r, c):
        for k in range(TOP_K):
            pltpu.make_async_copy(tile(h_ref, r), tile(xs_ref, pos_ref[r * TOP_K + k]),
                                  sem).start(priority=k % 2)
        return c

    lax.fori_loop(0, tm, issue, 0)
    for _ in range(TOP_K):
        pltpu.make_async_copy(h_ref, xs_ref.at[pl.ds(0, tm * nsub)], sem).wait()


def _dispatch(pos_flat, fill, h2, cap, nsub, tm_moe):
    n = h2.shape[0] // nsub
    tm = min(TM_ROWS, n)
    return pl.pallas_call(
        functools.partial(_dispatch_kernel, tm=tm, nsub=nsub, tm_moe=tm_moe),
        out_shape=jax.ShapeDtypeStruct((cap * nsub, LANES), F32),
        grid=(n // tm,),
        in_specs=[pl.BlockSpec((tm * TOP_K,), lambda i: (i,), memory_space=pltpu.SMEM),
                  pl.BlockSpec(fill.shape, lambda i: (0,), memory_space=pltpu.SMEM),
                  pl.BlockSpec((tm * nsub, LANES), lambda i: (i, 0))],
        out_specs=pl.BlockSpec(memory_space=pl.ANY),
        scratch_shapes=[pltpu.VMEM((nsub, LANES), F32), pltpu.VMEM((tm_moe * nsub, LANES), F32),
                        pltpu.SemaphoreType.DMA(()), pltpu.SemaphoreType.DMA(()),
                        pltpu.SemaphoreType.DMA(())],
        compiler_params=_params(("arbitrary",)),
    )(pos_flat, fill, h2)


def _moe_kernel(te_ref, nv_ref, x_ref, bg_ref, bu_ref, bd_ref, wg_hbm, wu_hbm, wd_hbm,
                y_ref, wf_s, wg_s, wu_s, wd_s, slot_s, sem):
    i = pl.program_id(0)
    nv = nv_ref[0]
    last = te_ref.shape[0] - 1

    def weight_copies(e, slot):
        return [pltpu.make_async_copy(w.at[e], wf_s.at[slot, j], sem.at[slot])
                for j, w in enumerate((wg_hbm, wu_hbm, wd_hbm))]

    @pl.when(i < nv)
    def _():
        e = te_ref[i]

        @pl.when(i == 0)
        def _():
            slot_s[0] = 0
            for c in weight_copies(e, 0):
                c.start()

        @pl.when((i == 0) | (e != te_ref[jnp.maximum(i - 1, 0)]))
        def _():
            slot = slot_s[0]
            for c in weight_copies(e, slot):
                c.wait()
            nxt = lax.while_loop(lambda j: (j < nv) & (te_ref[jnp.minimum(j, last)] == e),
                                 lambda j: j + 1, i + 1)

            @pl.when(nxt < nv)
            def _():
                for c in weight_copies(te_ref[jnp.minimum(nxt, last)], 1 - slot):
                    c.start()

            wg_s[...] = wf_s[slot, 0].astype(BF16)
            wu_s[...] = wf_s[slot, 1].astype(BF16)
            wd_s[...] = wf_s[slot, 2].astype(BF16)
            slot_s[0] = 1 - slot

        xb = _load_token_tiles(x_ref, wg_s.shape[0] // LANES).astype(BF16)
        g = _dot(xb, wg_s[...]) + bg_ref[0]
        u = _dot(xb, wu_s[...]) + bu_ref[0]
        g = jnp.minimum(g, SWIGLU_LIMIT)
        u = jnp.clip(u, -SWIGLU_LIMIT, SWIGLU_LIMIT)
        h = (u + 1.0) * (g * jax.nn.sigmoid(SWIGLU_ALPHA * g))
        _store_token_tiles(y_ref, _dot(h.astype(BF16), wd_s[...]) + bd_ref[0])

    @pl.when(i >= nv)
    def _():
        y_ref[...] = jnp.zeros_like(y_ref)


def _moe(tile_expert, n_valid, xs, w_gate, b_gate, w_up, b_up, w_down, b_down, tm):
    e, d, f = w_gate.shape
    assert d == f
    nsub = d // LANES
    cap = xs.shape[0] // nsub
    n_tiles = cap // tm

    def tile(i, te, nv):
        return jnp.minimum(i, nv[0] - 1)

    exp3 = lambda i, te, nv: (te[tile(i, te, nv)], 0, 0)
    grid_spec = pltpu.PrefetchScalarGridSpec(
        num_scalar_prefetch=2,
        grid=(n_tiles,),
        in_specs=[pl.BlockSpec((tm * nsub, LANES), lambda i, te, nv: (tile(i, te, nv), 0)),
                  pl.BlockSpec((1, 1, f), exp3), pl.BlockSpec((1, 1, f), exp3),
                  pl.BlockSpec((1, 1, d), exp3),
                  pl.BlockSpec(memory_space=pl.ANY), pl.BlockSpec(memory_space=pl.ANY),
                  pl.BlockSpec(memory_space=pl.ANY)],
        out_specs=pl.BlockSpec((tm * nsub, LANES), lambda i, te, nv: (i, 0)),
        scratch_shapes=[pltpu.VMEM((2, 3, d, f), F32),
                        pltpu.VMEM((d, f), BF16), pltpu.VMEM((d, f), BF16), pltpu.VMEM((f, d), BF16),
                        pltpu.SMEM((1,), jnp.int32), pltpu.SemaphoreType.DMA((2,))],
    )
    return pl.pallas_call(
        _moe_kernel,
        out_shape=jax.ShapeDtypeStruct((cap * nsub, LANES), F32),
        grid_spec=grid_spec,
        compiler_params=_params(("arbitrary",), 56),
    )(tile_expert, n_valid, xs, b_gate.reshape(e, 1, f), b_up.reshape(e, 1, f),
      b_down.reshape(e, 1, d), w_gate, w_up, w_down)


def _combine_kernel(pos_ref, posn_ref, w_ref, x1_ref, mod_ref, g_ref, b_ref, y_ref, o_ref,
                    buf, sem, *, tm):
    i = pl.program_id(0)
    n = pl.num_programs(0)
    nsub = x1_ref.shape[1] // LANES

    def issue(p_ref, slot):
        def body(r, c):
            for k in range(TOP_K):
                src = pl.ds(pl.multiple_of(p_ref[r * TOP_K + k] * nsub, nsub), nsub)
                dst = pl.ds(pl.multiple_of(r * nsub, nsub), nsub)
                pltpu.make_async_copy(y_ref.at[src], buf.at[slot, k, dst],
                                      sem.at[slot]).start(priority=k % 2)
            return c
        lax.fori_loop(0, tm, body, 0)

    @pl.when(i == 0)
    def _():
        issue(pos_ref, 0)

    @pl.when(i + 1 < n)
    def _():
        issue(posn_ref, (i + 1) % 2)

    slot = i % 2
    for k in range(TOP_K):
        pltpu.make_async_copy(y_ref.at[pl.ds(0, tm * nsub)], buf.at[slot, k], sem.at[slot]).wait()
    w = w_ref[...]
    f = jnp.zeros(x1_ref.shape, F32)
    for k in range(TOP_K):
        f = f + w[:, k:k + 1] * _load_token_tiles(buf.at[slot, k], nsub)
    m = mod_ref[0]
    o_ref[...] = _layer_norm(DEEPNORM_ALPHA * x1_ref[...] + m[5:6] * f, g_ref[...], b_ref[...])


def _combine(pos_flat, w, x1, mod3, g2, b2, y, seq):
    n, d = x1.shape
    tm = min(TM_ROWS, seq)
    nsteps = n // tm
    per_seq = seq // tm
    return pl.pallas_call(
        functools.partial(_combine_kernel, tm=tm),
        out_shape=jax.ShapeDtypeStruct((n, d), F32),
        grid=(nsteps,),
        in_specs=[pl.BlockSpec((tm * TOP_K,), lambda i: (i,), memory_space=pltpu.SMEM),
                  pl.BlockSpec((tm * TOP_K,), lambda i: (jnp.minimum(i + 1, nsteps - 1),),
                               memory_space=pltpu.SMEM),
                  pl.BlockSpec((tm, LANES), lambda i: (i, 0)),
                  pl.BlockSpec((tm, d), lambda i: (i, 0)),
                  pl.BlockSpec((1, 6, d), lambda i: (i // per_seq, 0, 0)),
                  pl.BlockSpec(g2.shape, lambda i: (0, 0)),
                  pl.BlockSpec(b2.shape, lambda i: (0, 0)),
                  pl.BlockSpec(memory_space=pl.ANY)],
        out_specs=pl.BlockSpec((tm, d), lambda i: (i, 0)),
        scratch_shapes=[pltpu.VMEM((2, TOP_K, tm * (d // LANES), LANES), F32),
                        pltpu.SemaphoreType.DMA((2,))],
        compiler_params=_params(("arbitrary",), 32),
    )(pos_flat, pos_flat, w, x1, mod3, g2, b2, y)


def _prep_w_in(w_in):
    d = w_in.shape[0]
    a3 = 3 * A_WIDTH
    lat = a3 + B_Q_LORA + B_KV_LORA
    z = lambda n: jnp.zeros((d, n), w_in.dtype)
    return jnp.concatenate([
        w_in[:, :A_WIDTH] * (A_HEAD_DIM ** -0.5 * LOG2_E),
        w_in[:, A_WIDTH:lat],
        z(B_NOPE_DIM), w_in[:, lat:lat + B_ROPE_DIM], z(LANES - B_NOPE_DIM - B_ROPE_DIM),
        w_in[:, lat + B_ROPE_DIM:]], axis=1).astype(BF16)


def _prep_w_uq(w_uq):
    r = w_uq.shape[0]
    w = w_uq.reshape(r, B_HEADS, B_QK_DIM)
    w = jnp.pad(w, ((0, 0), (0, 0), (0, LANES - B_QK_DIM)))
    return w.reshape(r, B_HEADS * LANES).astype(BF16)


def _prep_w_ukv(w_ukv):
    r = w_ukv.shape[0]
    w = w_ukv.reshape(r, B_HEADS, B_NOPE_DIM + B_V_DIM)
    wk = jnp.pad(w[:, :, :B_NOPE_DIM], ((0, 0), (0, 0), (0, LANES - B_NOPE_DIM)))
    wv = w[:, :, B_NOPE_DIM:].reshape(r, B_HEADS // 2, 2 * B_V_DIM)
    wv = jnp.pad(wv, ((0, 0), (0, 0), (0, V_PAIR_LANES - 2 * B_V_DIM)))
    return (wk.reshape(r, B_HEADS * LANES).astype(BF16),
            wv.reshape(r, B_HEADS // 2 * V_PAIR_LANES).astype(BF16))


def _rope_tables(positions):
    half = B_ROPE_DIM // 2
    freqs = ROPE_THETA ** (-jnp.arange(half, dtype=F32) / half)
    ang_t = positions.astype(F32)[:, None, :] * freqs[None, :, None]
    cos, sin = jnp.swapaxes(jnp.cos(ang_t), 1, 2), jnp.swapaxes(jnp.sin(ang_t), 1, 2)
    ang = cos
    z = lambda n: jnp.zeros(ang.shape[:2] + (n,), F32)
    tail = LANES - B_NOPE_DIM - B_ROPE_DIM
    cos_t = jnp.concatenate([jnp.ones(ang.shape[:2] + (B_NOPE_DIM,), F32), cos, cos, z(tail)], -1)
    sin_a = jnp.concatenate([z(B_NOPE_DIM), -sin, z(half), z(tail)], -1)
    sin_b = jnp.concatenate([z(B_NOPE_DIM), z(half), sin, z(tail)], -1)
    return cos_t, sin_a, sin_b


def kernel(x, c, positions, w_ada, b_ada, w_in, rms_q, w_uq, rms_kv, w_ukv, rel_bias,
           w_branch_a, w_branch_b, w_out, ln1_g, ln1_b, w_router, b_router,
           w_gate, b_gate, w_up, b_up, w_down, b_down, ln2_g, ln2_b):
    bsz, seq, d = x.shape
    n_tok = bsz * seq
    assert w_ada.shape[0] == DEPTH == 1
    row = lambda v: v.reshape(1, -1)

    mod3 = _ada(c, w_ada[0], b_ada[0]).reshape(bsz, 6, d)

    qk, va, cq, ckv, kr, ga, gb = _inproj(x, mod3, _prep_w_in(w_in[0]))
    ya = _attn_a(qk, va, _bias_table(rel_bias[0], min(TQ_A, seq)))
    wk, wv = _prep_w_ukv(w_ukv[0])
    qm, km, vm = _mla_proj(cq, ckv, kr, *_rope_tables(positions), row(rms_q[0]), row(rms_kv[0]),
                           _prep_w_uq(w_uq[0]), wk, wv)
    yb = _mla_attn(qm, km, vm)

    wr = jnp.pad(w_router[0], ((0, 0), (0, LANES - N_EXPERTS)))
    br = jnp.pad(b_router[0], (0, LANES - N_EXPERTS), constant_values=NEG_INF)
    x1, h2, logits = _merge(x, ya, yb, ga, gb, mod3,
                            w_branch_a[0].astype(BF16), w_branch_b[0].astype(BF16),
                            w_out[0].astype(BF16), row(ln1_g[0]), row(ln1_b[0]), wr, row(br))

    tm = TM_MOE
    n_tiles = -(-(n_tok * TOP_K) // tm) + N_EXPERTS
    assert n_tiles <= MAX_TILES_PAD
    pos, wts, tmap, meta = _route(logits.reshape(n_tok, LANES), tm)
    pos_flat = pos[:, :TOP_K].reshape(-1)
    fill = meta[:3].reshape(-1)
    nsub = d // LANES
    xs = _dispatch(pos_flat, fill, h2.reshape(n_tok * nsub, LANES), n_tiles * tm, nsub, tm)
    y = _moe(tmap[:n_tiles, 0], meta[0, :1], xs, w_gate[0], b_gate[0], w_up[0], b_up[0],
             w_down[0], b_down[0], tm)
    out = _combine(pos_flat, wts, x1.reshape(n_tok, d), mod3, row(ln2_g[0]), row(ln2_b[0]), y, seq)
    return out.reshape(bsz, seq, d)
```

```python
import functools

import jax
import jax.numpy as jnp
from jax import lax
from jax.experimental import pallas as pl
from jax.experimental.pallas import tpu as pltpu

F32 = jnp.float32
BF16 = jnp.bfloat16

CHUNK = 64
A_HEADS = 8
A_HEAD_DIM = 64
A_LEFT_CHUNKS = 8
A_MAX_REL = 128
A_WIDTH = A_HEADS * A_HEAD_DIM
B_HEADS = 8
B_NOPE_DIM = 64
B_ROPE_DIM = 32
B_V_DIM = 64
B_Q_LORA = 384
B_KV_LORA = 256
B_QK_DIM = B_NOPE_DIM + B_ROPE_DIM
ROPE_THETA = 10000.0
N_EXPERTS = 32
TOP_K = 4
SWIGLU_LIMIT = 7.0
SWIGLU_ALPHA = 1.702
DEPTH = 1
DEEPNORM_ALPHA = (2.0 * DEPTH) ** 0.25
LN_EPS = 1e-5
RMS_EPS = 1e-6
NEG_INF = -1e30
LOG2_E = 1.4426950408889634

LANES = 128
MXU_COLS = 256
V_PAIR_LANES = MXU_COLS

TM_PROJ = 512
TQ_A = 256
A_QBLOCKS = 2
TQ_B = 512
TM_MERGE = 512
TR_ROUTE = 1024
TM_MOE = 256
TM_ROWS = 256
COMBINE_GROUP = 8
MAX_TILES_PAD = 512


def _params(sem, vmem_mb=None):
    return pltpu.CompilerParams(
        dimension_semantics=sem,
        vmem_limit_bytes=None if vmem_mb is None else vmem_mb << 20)


def _dot(a, b):
    return jnp.dot(a, b, preferred_element_type=F32)


def _dot_nt(a, b):
    return lax.dot_general(a, b, (((1,), (1,)), ((), ())), preferred_element_type=F32)


def _store_token_tiles(ref, x):
    rows, d = x.shape
    nsub = d // LANES
    for s in range(nsub):
        ref[pl.ds(s, rows, stride=nsub), :] = x[:, s * LANES:(s + 1) * LANES]


def _load_token_tiles(ref, nsub):
    rows = ref.shape[0] // nsub
    return jnp.concatenate([ref[pl.ds(s, rows, stride=nsub), :] for s in range(nsub)], axis=1)


def _layer_norm(x, g, b):
    mu = jnp.mean(x, axis=-1, keepdims=True)
    xc = x - mu
    var = jnp.mean(xc * xc, axis=-1, keepdims=True)
    return xc * lax.rsqrt(var + LN_EPS) * g + b


def _ada_kernel(c_ref, w_ref, b_ref, o_ref):
    c = c_ref[...]
    sc = c * jax.nn.sigmoid(c)
    o_ref[...] = jnp.dot(sc, w_ref[...], preferred_element_type=F32,
                         precision=lax.Precision.HIGHEST) + b_ref[...]


def _ada(c, w_ada, b_ada):
    bsz, d = c.shape
    m = w_ada.shape[1]
    tn = 1024
    return pl.pallas_call(
        _ada_kernel,
        out_shape=jax.ShapeDtypeStruct((bsz, m), F32),
        grid=(m // tn,),
        in_specs=[pl.BlockSpec((bsz, d), lambda j: (0, 0)),
                  pl.BlockSpec((d, tn), lambda j: (0, j)),
                  pl.BlockSpec((1, tn), lambda j: (0, j))],
        out_specs=pl.BlockSpec((bsz, tn), lambda j: (0, j)),
        compiler_params=_params(("arbitrary",)),
    )(c, w_ada, b_ada.reshape(1, m))


def _inproj_kernel(x_ref, mod_ref, w_ref, qk_ref, va_ref, cq_ref, ckv_ref, kr_ref, ga_ref, gb_ref):
    m = mod_ref[0]
    h = (x_ref[0] * (1.0 + m[1:2]) + m[0:1]).astype(BF16)
    col = 0
    for ref in (qk_ref, None, cq_ref, ckv_ref, kr_ref, ga_ref, gb_ref):
        if ref is None:
            v = _dot(h, w_ref[:, col:col + A_WIDTH]).astype(BF16)
            ones_blk = jnp.ones((v.shape[0], LANES), BF16)
            parts = []
            for hp in range(A_WIDTH // LANES):
                parts += [v[:, hp * LANES:(hp + 1) * LANES], ones_blk]
            va_ref[0] = jnp.concatenate(parts, axis=1)
            col += A_WIDTH
            continue
        n = ref.shape[-1]
        ref[0] = _dot(h, w_ref[:, col:col + n]).astype(ref.dtype)
        col += n


def _inproj(x, mod3, w):
    bsz, seq, d = x.shape
    tm = min(TM_PROJ, seq)
    widths = (2 * A_WIDTH, A_WIDTH // LANES * V_PAIR_LANES, B_Q_LORA, B_KV_LORA, LANES, d, d)
    dtypes = (BF16, BF16, BF16, BF16, F32, BF16, BF16)
    assert sum(widths) - widths[1] + A_WIDTH == w.shape[1]
    return pl.pallas_call(
        _inproj_kernel,
        out_shape=[jax.ShapeDtypeStruct((bsz, seq, n), dt) for n, dt in zip(widths, dtypes)],
        grid=(bsz, seq // tm),
        in_specs=[pl.BlockSpec((1, tm, d), lambda b, i: (b, i, 0)),
                  pl.BlockSpec((1, 6, d), lambda b, i: (b, 0, 0)),
                  pl.BlockSpec(w.shape, lambda b, i: (0, 0))],
        out_specs=[pl.BlockSpec((1, tm, n), lambda b, i: (b, i, 0)) for n in widths],
        compiler_params=_params(("arbitrary", "arbitrary"), 48),
    )(x, mod3, w)


def _attn_a_kernel(q_ref, k_ref, v_ref, bias_ref, o_ref, *, tq, nd):
    band = nd * tq
    lane = lax.broadcasted_iota(jnp.int32, (1, LANES), 1)
    for sub in range(A_QBLOCKS):
        i = pl.program_id(2) * A_QBLOCKS + sub
        start = pl.multiple_of(jnp.maximum(i - (nd - 1), 0) * tq, tq)
        variant = jnp.minimum(i, nd - 1)
        q = q_ref[0, sub * tq:(sub + 1) * tq, :]
        k = k_ref[0, pl.ds(start, band), :]
        v = v_ref[0, pl.ds(start, band), :]
        zero = jnp.zeros_like(k)
        kk = jnp.concatenate([jnp.where(lane < A_HEAD_DIM, k, zero),
                              jnp.where(lane >= A_HEAD_DIM, k, zero)], axis=0)
        s = _dot_nt(q, kk)
        ps = []
        for hh in range(2):
            sh = s[:, hh * band:(hh + 1) * band] + bias_ref[hh, variant]
            m = sh.max(axis=-1, keepdims=True)
            ps.append(jnp.exp2(sh - m).astype(BF16))
        acc = _dot(jnp.concatenate(ps, axis=0), v)
        out = acc[:, :LANES] / acc[:, LANES:]
        o_ref[0, sub * tq:(sub + 1) * tq, :] = jnp.where(
            lane < A_HEAD_DIM, out[:tq], out[tq:]).astype(o_ref.dtype)


def _attn_a(qk, va, bias_tab):
    bsz, seq, _ = qk.shape
    tq = min(TQ_A, seq)
    nd = bias_tab.shape[1]
    assert seq >= nd * tq
    ncb = A_WIDTH // LANES
    return pl.pallas_call(
        functools.partial(_attn_a_kernel, tq=tq, nd=nd),
        out_shape=jax.ShapeDtypeStruct((bsz, seq, A_WIDTH), BF16),
        grid=(ncb, bsz, seq // (A_QBLOCKS * tq)),
        in_specs=[pl.BlockSpec((1, A_QBLOCKS * tq, LANES), lambda hp, b, i: (b, i, hp)),
                  pl.BlockSpec((1, seq, LANES), lambda hp, b, i: (b, 0, ncb + hp)),
                  pl.BlockSpec((1, seq, V_PAIR_LANES), lambda hp, b, i: (b, 0, hp)),
                  pl.BlockSpec((2, nd, tq, nd * tq), lambda hp, b, i: (hp, 0, 0, 0))],
        out_specs=pl.BlockSpec((1, A_QBLOCKS * tq, LANES), lambda hp, b, i: (b, i, hp)),
        compiler_params=_params(("arbitrary",) * 3, 40),
    )(qk, qk, va, bias_tab)


def _bias_table(rel_bias, tq):
    nd = (A_LEFT_CHUNKS * CHUNK + tq - 1) // tq + 1
    width = nd * tq
    period = width + tq
    u = jnp.arange(period)
    dist = jnp.where(u < width, (nd - 1) * tq - u, (nd - 1) * tq + period - u)
    vec = rel_bias[:, jnp.clip(dist, -A_MAX_REL, A_MAX_REL) + A_MAX_REL].astype(F32)
    toep = jnp.tile(vec, (1, tq))[:, :tq * (period - 1)].reshape(-1, tq, period - 1)[:, :, :width]
    qi = jnp.arange(tq)[:, None]
    x = jnp.arange(width)[None, :]
    dchunk = qi // CHUNK - x // CHUNK + (nd - 1) * (tq // CHUNK)
    valid = (dchunk >= 0) & (dchunk <= A_LEFT_CHUNKS)
    table = jnp.where(valid[None], toep * LOG2_E, NEG_INF)
    variants = []
    for v in range(nd):
        cut = (nd - 1 - v) * tq
        variants.append(jnp.pad(table[:, :, cut:], ((0, 0), (0, 0), (0, cut)),
                                constant_values=NEG_INF))
    return jnp.stack(variants, axis=1)


def _mla_proj_kernel(cq_ref, ckv_ref, kr_ref, cos_ref, sa_ref, sb_ref, rq_ref, rkv_ref,
                     wq_ref, wk_ref, wv_ref, q_out, k_out, v_out):
    def rms(x, g):
        xf = x.astype(F32)
        ms = jnp.mean(xf * xf, axis=-1, keepdims=True)
        return (xf * lax.rsqrt(ms + RMS_EPS) * g).astype(BF16)

    cqn = rms(cq_ref[0], rq_ref[...])
    ckvn = rms(ckv_ref[0], rkv_ref[...])
    q = _dot(cqn, wq_ref[...])
    kn = _dot(ckvn, wk_ref[...])
    vlane = lax.broadcasted_iota(jnp.int32, (1, v_out.shape[-1]), 1)
    ones_lane = jnp.where(vlane % V_PAIR_LANES >= 2 * B_V_DIM, 1.0, 0.0)
    v_out[0] = (_dot(ckvn, wv_ref[...]) + ones_lane).astype(v_out.dtype)
    cos_t, sin_a, sin_b = cos_ref[0], sa_ref[0], sb_ref[0]

    def rope(x):
        return (x * cos_t + pltpu.roll(x, LANES - B_ROPE_DIM // 2, 1) * sin_a
                + pltpu.roll(x, B_ROPE_DIM // 2, 1) * sin_b)

    kpe = rope(kr_ref[0])
    scale = B_QK_DIM ** -0.5 * LOG2_E
    for h in range(B_HEADS):
        sl = slice(h * LANES, (h + 1) * LANES)
        q_out[0, :, sl] = (rope(q[:, sl]) * scale).astype(q_out.dtype)
        k_out[0, :, sl] = (kn[:, sl] + kpe).astype(k_out.dtype)


def _mla_proj(cq, ckv, kr, cos_t, sin_a, sin_b, rms_q, rms_kv, wq, wk, wv):
    bsz, seq, _ = cq.shape
    tm = min(TM_PROJ, seq)
    tok = lambda n: pl.BlockSpec((1, tm, n), lambda b, i: (b, i, 0))
    full = lambda a: pl.BlockSpec(a.shape, lambda b, i: (0,) * a.ndim)
    return pl.pallas_call(
        _mla_proj_kernel,
        out_shape=[jax.ShapeDtypeStruct((bsz, seq, B_HEADS * LANES), BF16),
                   jax.ShapeDtypeStruct((bsz, seq, B_HEADS * LANES), BF16),
                   jax.ShapeDtypeStruct((bsz, seq, wv.shape[1]), BF16)],
        grid=(bsz, seq // tm),
        in_specs=[tok(B_Q_LORA), tok(B_KV_LORA), tok(LANES), tok(LANES), tok(LANES), tok(LANES),
                  full(rms_q), full(rms_kv), full(wq), full(wk), full(wv)],
        out_specs=[tok(B_HEADS * LANES), tok(B_HEADS * LANES), tok(wv.shape[1])],
        compiler_params=_params(("arbitrary", "arbitrary"), 32),
    )(cq, ckv, kr, cos_t, sin_a, sin_b, rms_q, rms_kv, wq, wk, wv)


def _mla_attn_kernel(q_ref, k_ref, v_ref, o_ref, s_a, s_b, m_s, acc_s, *, t):
    i = pl.program_id(2)
    lane = lax.broadcasted_iota(jnp.int32, (1, LANES), 1)
    m_s[...] = jnp.full(m_s.shape, NEG_INF, F32)
    acc_s[...] = jnp.zeros(acc_s.shape, F32)

    def produce(kj, s_ref):
        start = pl.multiple_of(kj * t, t)
        for hh in range(2):
            q = q_ref[0, :, hh * LANES:(hh + 1) * LANES]
            k = k_ref[0, pl.ds(start, t), hh * LANES:(hh + 1) * LANES]
            s_ref[hh] = _dot_nt(q, k)

    def consume(kj, s_ref, masked):
        start = pl.multiple_of(kj * t, t)
        v = v_ref[0, pl.ds(start, t), :]
        for hh in range(2):
            s = s_ref[hh]
            if masked:
                row_chunk = lax.broadcasted_iota(jnp.int32, (t, t), 0) // CHUNK
                col_chunk = lax.broadcasted_iota(jnp.int32, (t, t), 1) // CHUNK
                s = jnp.where(col_chunk <= row_chunk, s, NEG_INF)
            m_old = m_s[hh]
            m_new = jnp.maximum(m_old, jnp.broadcast_to(s.max(axis=-1, keepdims=True), m_old.shape))
            p = jnp.exp2(s - jnp.tile(m_new, (1, t // LANES))).astype(BF16)
            alpha = jnp.exp2(m_old - m_new)
            acc_s[hh] = jnp.tile(alpha, (1, V_PAIR_LANES // LANES)) * acc_s[hh] + _dot(p, v)
            m_s[hh] = m_new

    def body(pair, c):
        kj = 2 * pair
        produce(kj + 1, s_b)
        consume(kj, s_a, False)
        produce(kj + 2, s_a)
        consume(kj + 1, s_b, False)
        return c

    produce(0, s_a)
    lax.fori_loop(0, i // 2, body, 0)

    @pl.when(i % 2 == 0)
    def _():
        consume(i, s_a, True)

    @pl.when(i % 2 == 1)
    def _():
        produce(i, s_b)
        consume(i - 1, s_a, False)
        consume(i, s_b, True)

    outs = []
    for hh in range(2):
        acc = acc_s[hh]
        outs.append(acc[:, :LANES] / acc[:, LANES:])
    o_ref[0] = jnp.where(lane < B_V_DIM, outs[0], outs[1]).astype(o_ref.dtype)


def _mla_attn(qm, km, vm):
    bsz, seq, _ = qm.shape
    t = min(TQ_B, seq)
    return pl.pallas_call(
        functools.partial(_mla_attn_kernel, t=t),
        out_shape=jax.ShapeDtypeStruct((bsz, seq, B_HEADS * B_V_DIM), BF16),
        grid=(bsz, B_HEADS // 2, seq // t),
        in_specs=[pl.BlockSpec((1, t, 2 * LANES), lambda b, hp, i: (b, i, hp)),
                  pl.BlockSpec((1, seq, 2 * LANES), lambda b, hp, i: (b, 0, hp)),
                  pl.BlockSpec((1, seq, V_PAIR_LANES), lambda b, hp, i: (b, 0, hp))],
        out_specs=pl.BlockSpec((1, t, LANES), lambda b, hp, i: (b, i, hp)),
        scratch_shapes=[pltpu.VMEM((2, t, t), F32), pltpu.VMEM((2, t, t), F32),
                        pltpu.VMEM((2, t, LANES), F32),
                        pltpu.VMEM((2, t, V_PAIR_LANES), F32)],
        compiler_params=_params(("arbitrary",) * 3, 40),
    )(qm, km, vm)


def _merge_kernel(x_ref, ya_ref, yb_ref, ga_ref, gb_ref, mod_ref, wa_ref, wb_ref, wo_ref,
                  g1_ref, b1_ref, wrh_ref, wrl_ref, br_ref, x1_ref, h2_ref, lg_ref):
    m = mod_ref[0]
    a = _dot(ya_ref[0], wa_ref[...])
    b = _dot(yb_ref[0], wb_ref[...])
    merged = (jax.nn.sigmoid(ga_ref[0].astype(F32)) * a
              + jax.nn.sigmoid(gb_ref[0].astype(F32)) * b)
    o = _dot(merged.astype(BF16), wo_ref[...])
    x1 = _layer_norm(DEEPNORM_ALPHA * x_ref[0] + m[2:3] * o, g1_ref[...], b1_ref[...])
    h2 = x1 * (1.0 + m[4:5]) + m[3:4]
    x1_ref[0] = x1
    _store_token_tiles(h2_ref.at[0], h2)
    h_hi = h2.astype(BF16)
    h_lo = (h2 - h_hi.astype(F32)).astype(BF16)
    lg_ref[0] = (_dot(h_hi, wrh_ref[...]) + _dot(h_hi, wrl_ref[...]) + _dot(h_lo, wrh_ref[...])
                 + br_ref[...])


def _merge(x, ya, yb, ga, gb, mod3, wa, wb, wo, g1, b1, wr, br):
    wr_hi = wr.astype(BF16)
    wr_lo = (wr - wr_hi.astype(F32)).astype(BF16)
    bsz, seq, d = x.shape
    tm = min(TM_MERGE, seq)
    tok = lambda n: pl.BlockSpec((1, tm, n), lambda b, i: (b, i, 0))
    full = lambda a: pl.BlockSpec(a.shape, lambda b, i: (0,) * a.ndim)
    return pl.pallas_call(
        _merge_kernel,
        out_shape=[jax.ShapeDtypeStruct((bsz, seq, d), F32),
                   jax.ShapeDtypeStruct((bsz, seq * (d // LANES), LANES), F32),
                   jax.ShapeDtypeStruct((bsz, seq, LANES), F32)],
        grid=(bsz, seq // tm),
        in_specs=[tok(d), tok(A_WIDTH), tok(B_HEADS * B_V_DIM), tok(d), tok(d),
                  pl.BlockSpec((1, 6, d), lambda b, i: (b, 0, 0)),
                  full(wa), full(wb), full(wo), full(g1), full(b1), full(wr_hi), full(wr_lo),
                  full(br)],
        out_specs=[tok(d),
                   pl.BlockSpec((1, tm * (d // LANES), LANES), lambda b, i: (b, i, 0)),
                   tok(LANES)],
        compiler_params=_params(("arbitrary", "arbitrary"), 48),
    )(x, ya, yb, ga, gb, mod3, wa, wb, wo, g1, b1, wr_hi, wr_lo, br)


def _route_kernel(lg_ref, pos_ref, w_ref, tmap_ref, nv_ref, cnt_s, carry_s, pstart_s, *, tr, tm):
    phase = pl.program_id(0)
    i = pl.program_id(1)
    lane = lax.broadcasted_iota(jnp.int32, (tr, LANES), 1)
    work = lg_ref[...]
    sels, vals = [], []
    for _ in range(TOP_K):
        m = work.max(axis=-1, keepdims=True)
        idx = jnp.where(work == m, lane, LANES).min(axis=-1, keepdims=True)
        sel = lane == idx
        work = jnp.where(sel, -jnp.inf, work)
        sels.append(sel)
        vals.append(m)
    picked = sum(jnp.where(s, 1.0, 0.0) for s in sels)
    tile_count = jnp.broadcast_to(picked.sum(axis=0, keepdims=True), (8, LANES))

    @pl.when((phase == 0) & (i == 0))
    def _():
        cnt_s[...] = jnp.zeros_like(cnt_s)

    @pl.when(phase == 0)
    def _():
        cnt_s[...] += tile_count

    @pl.when((phase == 1) & (i == 0))
    def _():
        lane8 = lax.broadcasted_iota(jnp.int32, (8, LANES), 1)
        padded = jnp.floor((cnt_s[...] + (tm - 1)) * (1.0 / tm)) * tm
        cum = padded
        shift = 1
        while shift < N_EXPERTS:
            cum = cum + jnp.where(lane8 >= shift, pltpu.roll(cum, shift, 1), 0.0)
            shift *= 2
        pstart_s[...] = cum - padded
        carry_s[...] = jnp.zeros_like(carry_s)
        nt = tmap_ref.shape[0]
        tile_start = (lax.broadcasted_iota(jnp.int32, (nt, LANES), 0) * tm).astype(F32)
        lane_t = lax.broadcasted_iota(jnp.int32, (nt, LANES), 1)
        done = jnp.where((lane_t < N_EXPERTS) & (cum[0:1] <= tile_start), 1.0, 0.0)
        te = jnp.minimum(done.sum(axis=-1, keepdims=True), N_EXPERTS - 1.0)
        tmap_ref[...] = jnp.broadcast_to(te, (nt, LANES)).astype(jnp.int32)
        total = jnp.where(lane8 == N_EXPERTS - 1, cum, 0.0).sum(axis=-1, keepdims=True)
        row8 = lax.broadcasted_iota(jnp.int32, (8, LANES), 0)
        meta = jnp.where(row8 == 0, jnp.broadcast_to(total * (1.0 / tm), (8, LANES)),
                         jnp.where(row8 == 1, cum - padded + cnt_s[...], cum))
        nv_ref[...] = meta.astype(jnp.int32)

    @pl.when(phase == 1)
    def _():
        r = lax.broadcasted_iota(jnp.int32, (tr, tr), 0)
        c = lax.broadcasted_iota(jnp.int32, (tr, tr), 1)
        tri = jnp.where(c < r, 1.0, 0.0).astype(BF16)
        before = _dot(tri, picked.astype(BF16)) + carry_s[0:1]
        base = before + pstart_s[0:1]
        carry_s[...] += tile_count
        den = sum(jnp.exp(v - vals[0]) for v in vals)
        pos = jnp.zeros((tr, LANES), F32)
        wts = jnp.zeros((tr, LANES), F32)
        for k in range(TOP_K):
            pk = jnp.where(sels[k], base, 0.0).sum(axis=-1, keepdims=True)
            pos = jnp.where(lane == k, pk, pos)
            wts = jnp.where(lane == k, jnp.exp(vals[k] - vals[0]) / den, wts)
        pos_ref[...] = pos.astype(jnp.int32)
        w_ref[...] = wts


def _route(logits, tm):
    n = logits.shape[0]
    tr = min(TR_ROUTE, n)
    return pl.pallas_call(
        functools.partial(_route_kernel, tr=tr, tm=tm),
        out_shape=[jax.ShapeDtypeStruct((n, LANES), jnp.int32),
                   jax.ShapeDtypeStruct((n, LANES), F32),
                   jax.ShapeDtypeStruct((MAX_TILES_PAD, LANES), jnp.int32),
                   jax.ShapeDtypeStruct((8, LANES), jnp.int32)],
        grid=(2, n // tr),
        in_specs=[pl.BlockSpec((tr, LANES), lambda p, i: (i, 0))],
        out_specs=[pl.BlockSpec((tr, LANES), lambda p, i: (i * p, 0)),
                   pl.BlockSpec((tr, LANES), lambda p, i: (i * p, 0)),
                   pl.BlockSpec((MAX_TILES_PAD, LANES), lambda p, i: (0, 0)),
                   pl.BlockSpec((8, LANES), lambda p, i: (0, 0))],
        scratch_shapes=[pltpu.VMEM((8, LANES), F32)] * 3,
        compiler_params=_params(("arbitrary", "arbitrary")),
    )(logits)


def _dispatch_kernel(pos_ref, fill_ref, h_ref, xs_ref, zero_s, ztile_s, sem, fill_sem, tail_sem,
                     *, tm, nsub, tm_moe):
    def tile(ref, r):
        return ref.at[pl.ds(pl.multiple_of(r * nsub, nsub), nsub)]

    @pl.when(pl.program_id(0) == 0)
    def _():
        zero_s[...] = jnp.zeros_like(zero_s)
        ztile_s[...] = jnp.zeros_like(ztile_s)

        def fill_copy(r):
            return pltpu.make_async_copy(zero_s, tile(xs_ref, r), fill_sem)

        def tail_copy(t):
            rows = tm_moe * nsub
            return pltpu.make_async_copy(
                ztile_s, xs_ref.at[pl.ds(pl.multiple_of(t * rows, rows), rows)], tail_sem)

        def per_expert(e, total):
            lo, hi = fill_ref[LANES + e], fill_ref[2 * LANES + e]

            def one(r, c):
                fill_copy(r).start()
                return c

            lax.fori_loop(lo, hi, one, 0)
            return total + (hi - lo)

        total = lax.fori_loop(0, N_EXPERTS, per_expert, 0)
        n_tiles = xs_ref.shape[0] // (tm_moe * nsub)

        def tail_start(t, c):
            tail_copy(t).start()
            return c

        lax.fori_loop(fill_ref[0], n_tiles, tail_start, 0)

        def drain(_, c):
            fill_copy(0).wait()
            return c

        lax.fori_loop(0, total, drain, 0)

        def tail_drain(_, c):
            tail_copy(0).wait()
            return c

        lax.fori_loop(fill_ref[0], n_tiles, tail_drain, 0)

    def issue(j, c):
        for r in range(COMBINE_GROUP):
            row = pl.multiple_of(j * COMBINE_GROUP, COMBINE_GROUP) + r
            for k in range(TOP_K):
                pltpu.make_async_copy(tile(h_ref, row), tile(xs_ref, pos_ref[row * TOP_K + k]),
                                      sem).start(priority=k % 2)
        return c

    lax.fori_loop(0, tm // COMBINE_GROUP, issue, 0)
    for _ in range(TOP_K):
        pltpu.make_async_copy(h_ref, xs_ref.at[pl.ds(0, tm * nsub)], sem).wait()


def _dispatch(pos_flat, fill, h2, cap, nsub, tm_moe):
    n = h2.shape[0] // nsub
    tm = min(TM_ROWS, n)
    return pl.pallas_call(
        functools.partial(_dispatch_kernel, tm=tm, nsub=nsub, tm_moe=tm_moe),
        out_shape=jax.ShapeDtypeStruct((cap * nsub, LANES), F32),
        grid=(n // tm,),
        in_specs=[pl.BlockSpec((tm * TOP_K,), lambda i: (i,), memory_space=pltpu.SMEM),
                  pl.BlockSpec(fill.shape, lambda i: (0,), memory_space=pltpu.SMEM),
                  pl.BlockSpec((tm * nsub, LANES), lambda i: (i, 0))],
        out_specs=pl.BlockSpec(memory_space=pl.ANY),
        scratch_shapes=[pltpu.VMEM((nsub, LANES), F32), pltpu.VMEM((tm_moe * nsub, LANES), F32),
                        pltpu.SemaphoreType.DMA(()), pltpu.SemaphoreType.DMA(()),
                        pltpu.SemaphoreType.DMA(())],
        compiler_params=_params(("arbitrary",)),
    )(pos_flat, fill, h2)


def _moe_kernel(te_ref, nv_ref, x_ref, bg_ref, bu_ref, bd_ref, wg_hbm, wu_hbm, wd_hbm,
                y_ref, wf_s, wg_s, wu_s, wd_s, slot_s, sem):
    i = pl.program_id(0)
    nv = nv_ref[0]
    last = te_ref.shape[0] - 1

    def weight_copies(e, slot):
        return [pltpu.make_async_copy(w.at[e], wf_s.at[slot, j], sem.at[slot])
                for j, w in enumerate((wg_hbm, wu_hbm, wd_hbm))]

    @pl.when(i < nv)
    def _():
        e = te_ref[i]

        @pl.when(i == 0)
        def _():
            slot_s[0] = 0
            for c in weight_copies(e, 0):
                c.start()

        @pl.when((i == 0) | (e != te_ref[jnp.maximum(i - 1, 0)]))
        def _():
            slot = slot_s[0]
            for c in weight_copies(e, slot):
                c.wait()
            nxt = lax.while_loop(lambda j: (j < nv) & (te_ref[jnp.minimum(j, last)] == e),
                                 lambda j: j + 1, i + 1)

            @pl.when(nxt < nv)
            def _():
                for c in weight_copies(te_ref[jnp.minimum(nxt, last)], 1 - slot):
                    c.start()

            wg_s[...] = wf_s[slot, 0].astype(BF16)
            wu_s[...] = wf_s[slot, 1].astype(BF16)
            wd_s[...] = wf_s[slot, 2].astype(BF16)
            slot_s[0] = 1 - slot

        xb = _load_token_tiles(x_ref, wg_s.shape[0] // LANES).astype(BF16)
        g = _dot(xb, wg_s[...]) + bg_ref[0]
        u = _dot(xb, wu_s[...]) + bu_ref[0]
        g = jnp.minimum(g, SWIGLU_LIMIT)
        u = jnp.clip(u, -SWIGLU_LIMIT, SWIGLU_LIMIT)
        h = (u + 1.0) * (g * jax.nn.sigmoid(SWIGLU_ALPHA * g))
        _store_token_tiles(y_ref, _dot(h.astype(BF16), wd_s[...]) + bd_ref[0])

    @pl.when(i >= nv)
    def _():
        y_ref[...] = jnp.zeros_like(y_ref)


def _moe(tile_expert, n_valid, xs, w_gate, b_gate, w_up, b_up, w_down, b_down, tm):
    e, d, f = w_gate.shape
    assert d == f
    nsub = d // LANES
    cap = xs.shape[0] // nsub
    n_tiles = cap // tm

    def tile(i, te, nv):
        return jnp.minimum(i, nv[0] - 1)

    exp3 = lambda i, te, nv: (te[tile(i, te, nv)], 0, 0)
    grid_spec = pltpu.PrefetchScalarGridSpec(
        num_scalar_prefetch=2,
        grid=(n_tiles,),
        in_specs=[pl.BlockSpec((tm * nsub, LANES), lambda i, te, nv: (tile(i, te, nv), 0)),
                  pl.BlockSpec((1, 1, f), exp3), pl.BlockSpec((1, 1, f), exp3),
                  pl.BlockSpec((1, 1, d), exp3),
                  pl.BlockSpec(memory_space=pl.ANY), pl.BlockSpec(memory_space=pl.ANY),
                  pl.BlockSpec(memory_space=pl.ANY)],
        out_specs=pl.BlockSpec((tm * nsub, LANES), lambda i, te, nv: (i, 0)),
        scratch_shapes=[pltpu.VMEM((2, 3, d, f), F32),
                        pltpu.VMEM((d, f), BF16), pltpu.VMEM((d, f), BF16), pltpu.VMEM((f, d), BF16),
                        pltpu.SMEM((1,), jnp.int32), pltpu.SemaphoreType.DMA((2,))],
    )
    return pl.pallas_call(
        _moe_kernel,
        out_shape=jax.ShapeDtypeStruct((cap * nsub, LANES), F32),
        grid_spec=grid_spec,
        compiler_params=_params(("arbitrary",), 56),
    )(tile_expert, n_valid, xs, b_gate.reshape(e, 1, f), b_up.reshape(e, 1, f),
      b_down.reshape(e, 1, d), w_gate, w_up, w_down)


def _combine_kernel(pos_ref, posn_ref, w_ref, x1_ref, mod_ref, g_ref, b_ref, y_ref, o_ref,
                    buf, sem, *, tm):
    i = pl.program_id(0)
    n = pl.num_programs(0)
    nsub = x1_ref.shape[1] // LANES

    group = COMBINE_GROUP

    def issue_rows(p_ref, slot, r0):
        for r in range(group):
            for k in range(TOP_K):
                row = r0 + r
                src = pl.ds(pl.multiple_of(p_ref[row * TOP_K + k] * nsub, nsub), nsub)
                dst = pl.ds(pl.multiple_of(row * nsub, nsub), nsub)
                pltpu.make_async_copy(y_ref.at[src], buf.at[slot, k, dst],
                                      sem.at[slot]).start(priority=k % 2)

    def reduce_rows(slot, r0):
        rows = pl.ds(r0, group)
        w = w_ref[rows, :]
        f = jnp.zeros((group, nsub * LANES), F32)
        for k in range(TOP_K):
            yk = jnp.concatenate(
                [buf[slot, k, pl.ds(r0 * nsub + s, group, stride=nsub), :] for s in range(nsub)],
                axis=1)
            f = f + w[:, k:k + 1] * yk
        o_ref[rows, :] = f

    @pl.when(i == 0)
    def _():
        def body(j, c):
            issue_rows(pos_ref, 0, pl.multiple_of(j * group, group))
            return c
        lax.fori_loop(0, tm // group, body, 0)

    slot = i % 2
    for k in range(TOP_K):
        pltpu.make_async_copy(y_ref.at[pl.ds(0, tm * nsub)], buf.at[slot, k], sem.at[slot]).wait()

    @pl.when(i + 1 < n)
    def _():
        def body(j, c):
            r0 = pl.multiple_of(j * group, group)
            issue_rows(posn_ref, 1 - slot, r0)
            reduce_rows(slot, r0)
            return c
        lax.fori_loop(0, tm // group, body, 0)

    @pl.when(i + 1 == n)
    def _():
        def body(j, c):
            reduce_rows(slot, pl.multiple_of(j * group, group))
            return c
        lax.fori_loop(0, tm // group, body, 0)

    o_ref[...] = _layer_norm(DEEPNORM_ALPHA * x1_ref[...] + mod_ref[0, 5:6, :] * o_ref[...],
                             g_ref[...], b_ref[...])


def _combine(pos_flat, w, x1, mod3, g2, b2, y, seq):
    n, d = x1.shape
    tm = min(TM_ROWS, seq)
    nsteps = n // tm
    per_seq = seq // tm
    return pl.pallas_call(
        functools.partial(_combine_kernel, tm=tm),
        out_shape=jax.ShapeDtypeStruct((n, d), F32),
        grid=(nsteps,),
        in_specs=[pl.BlockSpec((tm * TOP_K,), lambda i: (i,), memory_space=pltpu.SMEM),
                  pl.BlockSpec((tm * TOP_K,), lambda i: (jnp.minimum(i + 1, nsteps - 1),),
                               memory_space=pltpu.SMEM),
                  pl.BlockSpec((tm, LANES), lambda i: (i, 0)),
                  pl.BlockSpec((tm, d), lambda i: (i, 0)),
                  pl.BlockSpec((1, 6, d), lambda i: (i // per_seq, 0, 0)),
                  pl.BlockSpec(g2.shape, lambda i: (0, 0)),
                  pl.BlockSpec(b2.shape, lambda i: (0, 0)),
                  pl.BlockSpec(memory_space=pl.ANY)],
        out_specs=pl.BlockSpec((tm, d), lambda i: (i, 0)),
        scratch_shapes=[pltpu.VMEM((2, TOP_K, tm * (d // LANES), LANES), F32),
                        pltpu.SemaphoreType.DMA((2,))],
        compiler_params=_params(("arbitrary",), 32),
    )(pos_flat, pos_flat, w, x1, mod3, g2, b2, y)


def _prep_w_in(w_in):
    d = w_in.shape[0]
    a3 = 3 * A_WIDTH
    lat = a3 + B_Q_LORA + B_KV_LORA
    z = lambda n: jnp.zeros((d, n), w_in.dtype)
    return jnp.concatenate([
        w_in[:, :A_WIDTH] * (A_HEAD_DIM ** -0.5 * LOG2_E),
        w_in[:, A_WIDTH:lat],
        z(B_NOPE_DIM), w_in[:, lat:lat + B_ROPE_DIM], z(LANES - B_NOPE_DIM - B_ROPE_DIM),
        w_in[:, lat + B_ROPE_DIM:]], axis=1).astype(BF16)


def _prep_w_uq(w_uq):
    r = w_uq.shape[0]
    w = w_uq.reshape(r, B_HEADS, B_QK_DIM)
    w = jnp.pad(w, ((0, 0), (0, 0), (0, LANES - B_QK_DIM)))
    return w.reshape(r, B_HEADS * LANES).astype(BF16)


def _prep_w_ukv(w_ukv):
    r = w_ukv.shape[0]
    w = w_ukv.reshape(r, B_HEADS, B_NOPE_DIM + B_V_DIM)
    wk = jnp.pad(w[:, :, :B_NOPE_DIM], ((0, 0), (0, 0), (0, LANES - B_NOPE_DIM)))
    wv = w[:, :, B_NOPE_DIM:].reshape(r, B_HEADS // 2, 2 * B_V_DIM)
    wv = jnp.pad(wv, ((0, 0), (0, 0), (0, V_PAIR_LANES - 2 * B_V_DIM)))
    return (wk.reshape(r, B_HEADS * LANES).astype(BF16),
            wv.reshape(r, B_HEADS // 2 * V_PAIR_LANES).astype(BF16))


def _rope_tables(positions):
    half = B_ROPE_DIM // 2
    freqs = ROPE_THETA ** (-jnp.arange(half, dtype=F32) / half)
    ang_t = positions.astype(F32)[:, None, :] * freqs[None, :, None]
    cos, sin = jnp.swapaxes(jnp.cos(ang_t), 1, 2), jnp.swapaxes(jnp.sin(ang_t), 1, 2)
    ang = cos
    z = lambda n: jnp.zeros(ang.shape[:2] + (n,), F32)
    tail = LANES - B_NOPE_DIM - B_ROPE_DIM
    cos_t = jnp.concatenate([jnp.ones(ang.shape[:2] + (B_NOPE_DIM,), F32), cos, cos, z(tail)], -1)
    sin_a = jnp.concatenate([z(B_NOPE_DIM), -sin, z(half), z(tail)], -1)
    sin_b = jnp.concatenate([z(B_NOPE_DIM), z(half), sin, z(tail)], -1)
    return cos_t, sin_a, sin_b


def kernel(x, c, positions, w_ada, b_ada, w_in, rms_q, w_uq, rms_kv, w_ukv, rel_bias,
           w_branch_a, w_branch_b, w_out, ln1_g, ln1_b, w_router, b_router,
           w_gate, b_gate, w_up, b_up, w_down, b_down, ln2_g, ln2_b):
    bsz, seq, d = x.shape
    n_tok = bsz * seq
    assert w_ada.shape[0] == DEPTH == 1
    row = lambda v: v.reshape(1, -1)

    mod3 = _ada(c, w_ada[0], b_ada[0]).reshape(bsz, 6, d)

    qk, va, cq, ckv, kr, ga, gb = _inproj(x, mod3, _prep_w_in(w_in[0]))
    ya = _attn_a(qk, va, _bias_table(rel_bias[0], min(TQ_A, seq)))
    wk, wv = _prep_w_ukv(w_ukv[0])
    qm, km, vm = _mla_proj(cq, ckv, kr, *_rope_tables(positions), row(rms_q[0]), row(rms_kv[0]),
                           _prep_w_uq(w_uq[0]), wk, wv)
    yb = _mla_attn(qm, km, vm)

    wr = jnp.pad(w_router[0], ((0, 0), (0, LANES - N_EXPERTS)))
    br = jnp.pad(b_router[0], (0, LANES - N_EXPERTS), constant_values=NEG_INF)
    x1, h2, logits = _merge(x, ya, yb, ga, gb, mod3,
                            w_branch_a[0].astype(BF16), w_branch_b[0].astype(BF16),
                            w_out[0].astype(BF16), row(ln1_g[0]), row(ln1_b[0]), wr, row(br))

    tm = TM_MOE
    n_tiles = -(-(n_tok * TOP_K) // tm) + N_EXPERTS
    assert n_tiles <= MAX_TILES_PAD
    pos, wts, tmap, meta = _route(logits.reshape(n_tok, LANES), tm)
    pos_flat = pos[:, :TOP_K].reshape(-1)
    fill = meta[:3].reshape(-1)
    nsub = d // LANES
    xs = _dispatch(pos_flat, fill, h2.reshape(n_tok * nsub, LANES), n_tiles * tm, nsub, tm)
    y = _moe(tmap[:n_tiles, 0], meta[0, :1], xs, w_gate[0], b_gate[0], w_up[0], b_up[0],
             w_down[0], b_down[0], tm)
    out = _combine(pos_flat, wts, x1.reshape(n_tok, d), mod3, row(ln2_g[0]), row(ln2_b[0]), y, seq)
    return out.reshape(bsz, seq, d)
```

```python
import functools

import jax
import jax.numpy as jnp
from jax import lax
from jax.experimental import pallas as pl
from jax.experimental.pallas import tpu as pltpu

F32 = jnp.float32
BF16 = jnp.bfloat16

CHUNK = 64
A_HEADS = 8
A_HEAD_DIM = 64
A_LEFT_CHUNKS = 8
A_MAX_REL = 128
A_WIDTH = A_HEADS * A_HEAD_DIM
B_HEADS = 8
B_NOPE_DIM = 64
B_ROPE_DIM = 32
B_V_DIM = 64
B_Q_LORA = 384
B_KV_LORA = 256
B_QK_DIM = B_NOPE_DIM + B_ROPE_DIM
ROPE_THETA = 10000.0
N_EXPERTS = 32
TOP_K = 4
SWIGLU_LIMIT = 7.0
SWIGLU_ALPHA = 1.702
DEPTH = 1
DEEPNORM_ALPHA = (2.0 * DEPTH) ** 0.25
LN_EPS = 1e-5
RMS_EPS = 1e-6
NEG_INF = -1e30
LOG2_E = 1.4426950408889634

LANES = 128
MXU_COLS = 256
V_PAIR_LANES = MXU_COLS

TM_PROJ = 512
TQ_A = 256
A_QBLOCKS = 2
TQ_B = 512
TM_MERGE = 512
TR_ROUTE = 1024
TM_MOE = 256
TM_ROWS = 256
COMBINE_GROUP = 8
MAX_TILES_PAD = 512


def _params(sem, vmem_mb=None):
    return pltpu.CompilerParams(
        dimension_semantics=sem,
        vmem_limit_bytes=None if vmem_mb is None else vmem_mb << 20)


def _dot(a, b):
    return jnp.dot(a, b, preferred_element_type=F32)


def _dot_nt(a, b):
    return lax.dot_general(a, b, (((1,), (1,)), ((), ())), preferred_element_type=F32)


def _store_token_tiles(ref, x):
    rows, d = x.shape
    nsub = d // LANES
    for s in range(nsub):
        ref[pl.ds(s, rows, stride=nsub), :] = x[:, s * LANES:(s + 1) * LANES]


def _load_token_tiles(ref, nsub):
    rows = ref.shape[0] // nsub
    return jnp.concatenate([ref[pl.ds(s, rows, stride=nsub), :] for s in range(nsub)], axis=1)


def _layer_norm(x, g, b):
    mu = jnp.mean(x, axis=-1, keepdims=True)
    xc = x - mu
    var = jnp.mean(xc * xc, axis=-1, keepdims=True)
    return xc * lax.rsqrt(var + LN_EPS) * g + b


def _ada_kernel(c_ref, w_ref, b_ref, o_ref):
    c = c_ref[...]
    sc = c * jax.nn.sigmoid(c)
    o_ref[...] = jnp.dot(sc, w_ref[...], preferred_element_type=F32,
                         precision=lax.Precision.HIGHEST) + b_ref[...]


def _ada(c, w_ada, b_ada):
    bsz, d = c.shape
    m = w_ada.shape[1]
    tn = 1024
    return pl.pallas_call(
        _ada_kernel,
        out_shape=jax.ShapeDtypeStruct((bsz, m), F32),
        grid=(m // tn,),
        in_specs=[pl.BlockSpec((bsz, d), lambda j: (0, 0)),
                  pl.BlockSpec((d, tn), lambda j: (0, j)),
                  pl.BlockSpec((1, tn), lambda j: (0, j))],
        out_specs=pl.BlockSpec((bsz, tn), lambda j: (0, j)),
        compiler_params=_params(("arbitrary",)),
    )(c, w_ada, b_ada.reshape(1, m))


def _inproj_kernel(x_ref, mod_ref, w_ref, qk_ref, va_ref, cq_ref, ckv_ref, kr_ref, ga_ref, gb_ref):
    m = mod_ref[0]
    h = (x_ref[0] * (1.0 + m[1:2]) + m[0:1]).astype(BF16)
    col = 0
    for ref in (qk_ref, None, cq_ref, ckv_ref, kr_ref, ga_ref, gb_ref):
        if ref is None:
            v = _dot(h, w_ref[:, col:col + A_WIDTH]).astype(BF16)
            ones_blk = jnp.ones((v.shape[0], LANES), BF16)
            parts = []
            for hp in range(A_WIDTH // LANES):
                parts += [v[:, hp * LANES:(hp + 1) * LANES], ones_blk]
            va_ref[0] = jnp.concatenate(parts, axis=1)
            col += A_WIDTH
            continue
        n = ref.shape[-1]
        ref[0] = _dot(h, w_ref[:, col:col + n]).astype(ref.dtype)
        col += n


def _inproj(x, mod3, w):
    bsz, seq, d = x.shape
    tm = min(TM_PROJ, seq)
    widths = (2 * A_WIDTH, A_WIDTH // LANES * V_PAIR_LANES, B_Q_LORA, B_KV_LORA, LANES, d, d)
    dtypes = (BF16, BF16, BF16, BF16, F32, BF16, BF16)
    assert sum(widths) - widths[1] + A_WIDTH == w.shape[1]
    return pl.pallas_call(
        _inproj_kernel,
        out_shape=[jax.ShapeDtypeStruct((bsz, seq, n), dt) for n, dt in zip(widths, dtypes)],
        grid=(bsz, seq // tm),
        in_specs=[pl.BlockSpec((1, tm, d), lambda b, i: (b, i, 0)),
                  pl.BlockSpec((1, 6, d), lambda b, i: (b, 0, 0)),
                  pl.BlockSpec(w.shape, lambda b, i: (0, 0))],
        out_specs=[pl.BlockSpec((1, tm, n), lambda b, i: (b, i, 0)) for n in widths],
        compiler_params=_params(("arbitrary", "arbitrary"), 48),
    )(x, mod3, w)


def _attn_a_kernel(q_ref, k_ref, v_ref, bias_ref, o_ref, *, tq, nd):
    band = nd * tq
    lane = lax.broadcasted_iota(jnp.int32, (1, LANES), 1)
    for sub in range(A_QBLOCKS):
        i = pl.program_id(2) * A_QBLOCKS + sub
        start = pl.multiple_of(jnp.maximum(i - (nd - 1), 0) * tq, tq)
        variant = jnp.minimum(i, nd - 1)
        q = q_ref[0, sub * tq:(sub + 1) * tq, :]
        k = k_ref[0, pl.ds(start, band), :]
        v = v_ref[0, pl.ds(start, band), :]
        zero = jnp.zeros_like(k)
        kk = jnp.concatenate([jnp.where(lane < A_HEAD_DIM, k, zero),
                              jnp.where(lane >= A_HEAD_DIM, k, zero)], axis=0)
        s = _dot_nt(q, kk)
        ps = []
        for hh in range(2):
            sh = s[:, hh * band:(hh + 1) * band] + bias_ref[hh, variant]
            m = sh.max(axis=-1, keepdims=True)
            ps.append(jnp.exp2(sh - m).astype(BF16))
        acc = _dot(jnp.concatenate(ps, axis=0), v)
        out = acc[:, :LANES] / acc[:, LANES:]
        o_ref[0, sub * tq:(sub + 1) * tq, :] = jnp.where(
            lane < A_HEAD_DIM, out[:tq], out[tq:]).astype(o_ref.dtype)


def _attn_a(qk, va, bias_tab):
    bsz, seq, _ = qk.shape
    tq = min(TQ_A, seq)
    nd = bias_tab.shape[1]
    assert seq >= nd * tq
    ncb = A_WIDTH // LANES
    return pl.pallas_call(
        functools.partial(_attn_a_kernel, tq=tq, nd=nd),
        out_shape=jax.ShapeDtypeStruct((bsz, seq, A_WIDTH), BF16),
        grid=(ncb, bsz, seq // (A_QBLOCKS * tq)),
        in_specs=[pl.BlockSpec((1, A_QBLOCKS * tq, LANES), lambda hp, b, i: (b, i, hp)),
                  pl.BlockSpec((1, seq, LANES), lambda hp, b, i: (b, 0, ncb + hp)),
                  pl.BlockSpec((1, seq, V_PAIR_LANES), lambda hp, b, i: (b, 0, hp)),
                  pl.BlockSpec((2, nd, tq, nd * tq), lambda hp, b, i: (hp, 0, 0, 0))],
        out_specs=pl.BlockSpec((1, A_QBLOCKS * tq, LANES), lambda hp, b, i: (b, i, hp)),
        compiler_params=_params(("arbitrary",) * 3, 40),
    )(qk, qk, va, bias_tab)


def _bias_table(rel_bias, tq):
    nd = (A_LEFT_CHUNKS * CHUNK + tq - 1) // tq + 1
    width = nd * tq
    period = width + tq
    u = jnp.arange(period)
    dist = jnp.where(u < width, (nd - 1) * tq - u, (nd - 1) * tq + period - u)
    vec = rel_bias[:, jnp.clip(dist, -A_MAX_REL, A_MAX_REL) + A_MAX_REL].astype(F32)
    toep = jnp.tile(vec, (1, tq))[:, :tq * (period - 1)].reshape(-1, tq, period - 1)[:, :, :width]
    qi = jnp.arange(tq)[:, None]
    x = jnp.arange(width)[None, :]
    dchunk = qi // CHUNK - x // CHUNK + (nd - 1) * (tq // CHUNK)
    valid = (dchunk >= 0) & (dchunk <= A_LEFT_CHUNKS)
    table = jnp.where(valid[None], toep * LOG2_E, NEG_INF)
    variants = []
    for v in range(nd):
        cut = (nd - 1 - v) * tq
        variants.append(jnp.pad(table[:, :, cut:], ((0, 0), (0, 0), (0, cut)),
                                constant_values=NEG_INF))
    return jnp.stack(variants, axis=1)


def _mla_proj_kernel(cq_ref, ckv_ref, kr_ref, tab_ref, rq_ref, rkv_ref,
                     wq_ref, wk_ref, wv_ref, q_out, k_out, v_out):
    def rms(x, g):
        xf = x.astype(F32)
        ms = jnp.mean(xf * xf, axis=-1, keepdims=True)
        return (xf * lax.rsqrt(ms + RMS_EPS) * g).astype(BF16)

    cqn = rms(cq_ref[0], rq_ref[...])
    ckvn = rms(ckv_ref[0], rkv_ref[...])
    q = _dot(cqn, wq_ref[...])
    kn = _dot(ckvn, wk_ref[...])
    vlane = lax.broadcasted_iota(jnp.int32, (1, v_out.shape[-1]), 1)
    ones_lane = jnp.where(vlane % V_PAIR_LANES >= 2 * B_V_DIM, 1.0, 0.0)
    v_out[0] = (_dot(ckvn, wv_ref[...]) + ones_lane).astype(v_out.dtype)
    tab = tab_ref[0]
    tlane = lax.broadcasted_iota(jnp.int32, tab.shape, 1)
    sines = pltpu.roll(tab, B_NOPE_DIM, 1)
    rope_end = B_NOPE_DIM + B_ROPE_DIM
    cos_t = jnp.where(tlane < B_NOPE_DIM, 1.0, jnp.where(tlane < rope_end, tab, 0.0))
    sin_a = jnp.where((tlane >= B_NOPE_DIM) & (tlane < B_NOPE_DIM + B_ROPE_DIM // 2), sines, 0.0)
    sin_b = jnp.where((tlane >= B_NOPE_DIM + B_ROPE_DIM // 2) & (tlane < rope_end), sines, 0.0)

    def rope(x):
        return (x * cos_t + pltpu.roll(x, LANES - B_ROPE_DIM // 2, 1) * sin_a
                + pltpu.roll(x, B_ROPE_DIM // 2, 1) * sin_b)

    kpe = rope(kr_ref[0])
    scale = B_QK_DIM ** -0.5 * LOG2_E
    for h in range(B_HEADS):
        sl = slice(h * LANES, (h + 1) * LANES)
        q_out[0, :, sl] = (rope(q[:, sl]) * scale).astype(q_out.dtype)
        k_out[0, :, sl] = (kn[:, sl] + kpe).astype(k_out.dtype)


def _mla_proj(cq, ckv, kr, rope_tab, rms_q, rms_kv, wq, wk, wv):
    bsz, seq, _ = cq.shape
    tm = min(TM_PROJ, seq)
    tok = lambda n: pl.BlockSpec((1, tm, n), lambda b, i: (b, i, 0))
    full = lambda a: pl.BlockSpec(a.shape, lambda b, i: (0,) * a.ndim)
    return pl.pallas_call(
        _mla_proj_kernel,
        out_shape=[jax.ShapeDtypeStruct((bsz, seq, B_HEADS * LANES), BF16),
                   jax.ShapeDtypeStruct((bsz, seq, B_HEADS * LANES), BF16),
                   jax.ShapeDtypeStruct((bsz, seq, wv.shape[1]), BF16)],
        grid=(bsz, seq // tm),
        in_specs=[tok(B_Q_LORA), tok(B_KV_LORA), tok(LANES), tok(LANES),
                  full(rms_q), full(rms_kv), full(wq), full(wk), full(wv)],
        out_specs=[tok(B_HEADS * LANES), tok(B_HEADS * LANES), tok(wv.shape[1])],
        compiler_params=_params(("arbitrary", "arbitrary"), 32),
    )(cq, ckv, kr, rope_tab, rms_q, rms_kv, wq, wk, wv)


def _mla_attn_kernel(q_ref, k_ref, v_ref, o_ref, s_a, s_b, m_s, acc_s, *, t):
    i = pl.program_id(2)
    lane = lax.broadcasted_iota(jnp.int32, (1, LANES), 1)
    m_s[...] = jnp.full(m_s.shape, NEG_INF, F32)
    acc_s[...] = jnp.zeros(acc_s.shape, F32)

    def produce(kj, s_ref):
        start = pl.multiple_of(kj * t, t)
        for hh in range(2):
            q = q_ref[0, :, hh * LANES:(hh + 1) * LANES]
            k = k_ref[0, pl.ds(start, t), hh * LANES:(hh + 1) * LANES]
            s_ref[hh] = _dot_nt(q, k)

    def consume(kj, s_ref, masked):
        start = pl.multiple_of(kj * t, t)
        v = v_ref[0, pl.ds(start, t), :]
        for hh in range(2):
            s = s_ref[hh]
            if masked:
                row_chunk = lax.broadcasted_iota(jnp.int32, (t, t), 0) // CHUNK
                col_chunk = lax.broadcasted_iota(jnp.int32, (t, t), 1) // CHUNK
                s = jnp.where(col_chunk <= row_chunk, s, NEG_INF)
            m_old = m_s[hh]
            m_new = jnp.maximum(m_old, jnp.broadcast_to(s.max(axis=-1, keepdims=True), m_old.shape))
            p = jnp.exp2(s - jnp.tile(m_new, (1, t // LANES))).astype(BF16)
            alpha = jnp.exp2(m_old - m_new)
            acc_s[hh] = jnp.tile(alpha, (1, V_PAIR_LANES // LANES)) * acc_s[hh] + _dot(p, v)
            m_s[hh] = m_new

    def body(pair, c):
        kj = 2 * pair
        produce(kj + 1, s_b)
        consume(kj, s_a, False)
        produce(kj + 2, s_a)
        consume(kj + 1, s_b, False)
        return c

    produce(0, s_a)
    lax.fori_loop(0, i // 2, body, 0)

    @pl.when(i % 2 == 0)
    def _():
        consume(i, s_a, True)

    @pl.when(i % 2 == 1)
    def _():
        produce(i, s_b)
        consume(i - 1, s_a, False)
        consume(i, s_b, True)

    outs = []
    for hh in range(2):
        acc = acc_s[hh]
        outs.append(acc[:, :LANES] / acc[:, LANES:])
    o_ref[0] = jnp.where(lane < B_V_DIM, outs[0], outs[1]).astype(o_ref.dtype)


def _mla_attn(qm, km, vm):
    bsz, seq, _ = qm.shape
    t = min(TQ_B, seq)
    return pl.pallas_call(
        functools.partial(_mla_attn_kernel, t=t),
        out_shape=jax.ShapeDtypeStruct((bsz, seq, B_HEADS * B_V_DIM), BF16),
        grid=(bsz, B_HEADS // 2, seq // t),
        in_specs=[pl.BlockSpec((1, t, 2 * LANES), lambda b, hp, i: (b, i, hp)),
                  pl.BlockSpec((1, seq, 2 * LANES), lambda b, hp, i: (b, 0, hp)),
                  pl.BlockSpec((1, seq, V_PAIR_LANES), lambda b, hp, i: (b, 0, hp))],
        out_specs=pl.BlockSpec((1, t, LANES), lambda b, hp, i: (b, i, hp)),
        scratch_shapes=[pltpu.VMEM((2, t, t), F32), pltpu.VMEM((2, t, t), F32),
                        pltpu.VMEM((2, t, LANES), F32),
                        pltpu.VMEM((2, t, V_PAIR_LANES), F32)],
        compiler_params=_params(("arbitrary",) * 3, 40),
    )(qm, km, vm)


def _merge_kernel(x_ref, ya_ref, yb_ref, ga_ref, gb_ref, mod_ref, wa_ref, wb_ref, wo_ref,
                  g1_ref, b1_ref, wrh_ref, wrl_ref, br_ref, x1_ref, h2_ref, lg_ref):
    m = mod_ref[0]
    a = _dot(ya_ref[0], wa_ref[...])
    b = _dot(yb_ref[0], wb_ref[...])
    merged = (jax.nn.sigmoid(ga_ref[0].astype(F32)) * a
              + jax.nn.sigmoid(gb_ref[0].astype(F32)) * b)
    o = _dot(merged.astype(BF16), wo_ref[...])
    x1 = _layer_norm(DEEPNORM_ALPHA * x_ref[0] + m[2:3] * o, g1_ref[...], b1_ref[...])
    h2 = x1 * (1.0 + m[4:5]) + m[3:4]
    x1_ref[0] = x1
    _store_token_tiles(h2_ref.at[0], h2)
    h_hi = h2.astype(BF16)
    h_lo = (h2 - h_hi.astype(F32)).astype(BF16)
    lg_ref[0] = (_dot(h_hi, wrh_ref[...]) + _dot(h_hi, wrl_ref[...]) + _dot(h_lo, wrh_ref[...])
                 + br_ref[...])


def _merge(x, ya, yb, ga, gb, mod3, wa, wb, wo, g1, b1, wr, br):
    wr_hi = wr.astype(BF16)
    wr_lo = (wr - wr_hi.astype(F32)).astype(BF16)
    bsz, seq, d = x.shape
    tm = min(TM_MERGE, seq)
    tok = lambda n: pl.BlockSpec((1, tm, n), lambda b, i: (b, i, 0))
    full = lambda a: pl.BlockSpec(a.shape, lambda b, i: (0,) * a.ndim)
    return pl.pallas_call(
        _merge_kernel,
        out_shape=[jax.ShapeDtypeStruct((bsz, seq, d), F32),
                   jax.ShapeDtypeStruct((bsz, seq * (d // LANES), LANES), F32),
                   jax.ShapeDtypeStruct((bsz, seq, LANES), F32)],
        grid=(bsz, seq // tm),
        in_specs=[tok(d), tok(A_WIDTH), tok(B_HEADS * B_V_DIM), tok(d), tok(d),
                  pl.BlockSpec((1, 6, d), lambda b, i: (b, 0, 0)),
                  full(wa), full(wb), full(wo), full(g1), full(b1), full(wr_hi), full(wr_lo),
                  full(br)],
        out_specs=[tok(d),
                   pl.BlockSpec((1, tm * (d // LANES), LANES), lambda b, i: (b, i, 0)),
                   tok(LANES)],
        compiler_params=_params(("arbitrary", "arbitrary"), 48),
    )(x, ya, yb, ga, gb, mod3, wa, wb, wo, g1, b1, wr_hi, wr_lo, br)


def _route_kernel(lg_ref, pos_ref, w_ref, tmap_ref, nv_ref, cnt_s, carry_s, pstart_s, *, tr, tm):
    phase = pl.program_id(0)
    i = pl.program_id(1)
    lane = lax.broadcasted_iota(jnp.int32, (tr, LANES), 1)
    work = lg_ref[...]
    sels, vals = [], []
    for _ in range(TOP_K):
        m = work.max(axis=-1, keepdims=True)
        idx = jnp.where(work == m, lane, LANES).min(axis=-1, keepdims=True)
        sel = lane == idx
        work = jnp.where(sel, -jnp.inf, work)
        sels.append(sel)
        vals.append(m)
    picked = sum(jnp.where(s, 1.0, 0.0) for s in sels)
    tile_count = jnp.broadcast_to(picked.sum(axis=0, keepdims=True), (8, LANES))

    @pl.when((phase == 0) & (i == 0))
    def _():
        cnt_s[...] = jnp.zeros_like(cnt_s)

    @pl.when(phase == 0)
    def _():
        cnt_s[...] += tile_count

    @pl.when((phase == 1) & (i == 0))
    def _():
        lane8 = lax.broadcasted_iota(jnp.int32, (8, LANES), 1)
        padded = jnp.floor((cnt_s[...] + (tm - 1)) * (1.0 / tm)) * tm
        cum = padded
        shift = 1
        while shift < N_EXPERTS:
            cum = cum + jnp.where(lane8 >= shift, pltpu.roll(cum, shift, 1), 0.0)
            shift *= 2
        pstart_s[...] = cum - padded
        carry_s[...] = jnp.zeros_like(carry_s)
        nt = tmap_ref.shape[0]
        tile_start = (lax.broadcasted_iota(jnp.int32, (nt, LANES), 0) * tm).astype(F32)
        lane_t = lax.broadcasted_iota(jnp.int32, (nt, LANES), 1)
        done = jnp.where((lane_t < N_EXPERTS) & (cum[0:1] <= tile_start), 1.0, 0.0)
        te = jnp.minimum(done.sum(axis=-1, keepdims=True), N_EXPERTS - 1.0)
        tmap_ref[...] = jnp.broadcast_to(te, (nt, LANES)).astype(jnp.int32)
        total = jnp.where(lane8 == N_EXPERTS - 1, cum, 0.0).sum(axis=-1, keepdims=True)
        row8 = lax.broadcasted_iota(jnp.int32, (8, LANES), 0)
        meta = jnp.where(row8 == 0, jnp.broadcast_to(total * (1.0 / tm), (8, LANES)),
                         jnp.where(row8 == 1, cum - padded + cnt_s[...], cum))
        nv_ref[...] = meta.astype(jnp.int32)

    @pl.when(phase == 1)
    def _():
        r = lax.broadcasted_iota(jnp.int32, (tr, tr), 0)
        c = lax.broadcasted_iota(jnp.int32, (tr, tr), 1)
        tri = jnp.where(c < r, 1.0, 0.0).astype(BF16)
        before = _dot(tri, picked.astype(BF16)) + carry_s[0:1]
        base = before + pstart_s[0:1]
        carry_s[...] += tile_count
        den = sum(jnp.exp(v - vals[0]) for v in vals)
        pos = jnp.zeros((tr, LANES), F32)
        wts = jnp.zeros((tr, LANES), F32)
        for k in range(TOP_K):
            pk = jnp.where(sels[k], base, 0.0).sum(axis=-1, keepdims=True)
            pos = jnp.where(lane == k, pk, pos)
            wts = jnp.where(lane == k, jnp.exp(vals[k] - vals[0]) / den, wts)
        pos_ref[...] = pos.astype(jnp.int32)
        w_ref[...] = wts


def _route(logits, tm):
    n = logits.shape[0]
    tr = min(TR_ROUTE, n)
    return pl.pallas_call(
        functools.partial(_route_kernel, tr=tr, tm=tm),
        out_shape=[jax.ShapeDtypeStruct((n, LANES), jnp.int32),
                   jax.ShapeDtypeStruct((n, LANES), F32),
                   jax.ShapeDtypeStruct((MAX_TILES_PAD, LANES), jnp.int32),
                   jax.ShapeDtypeStruct((8, LANES), jnp.int32)],
        grid=(2, n // tr),
        in_specs=[pl.BlockSpec((tr, LANES), lambda p, i: (i, 0))],
        out_specs=[pl.BlockSpec((tr, LANES), lambda p, i: (i * p, 0)),
                   pl.BlockSpec((tr, LANES), lambda p, i: (i * p, 0)),
                   pl.BlockSpec((MAX_TILES_PAD, LANES), lambda p, i: (0, 0)),
                   pl.BlockSpec((8, LANES), lambda p, i: (0, 0))],
        scratch_shapes=[pltpu.VMEM((8, LANES), F32)] * 3,
        compiler_params=_params(("arbitrary", "arbitrary")),
    )(logits)


def _dispatch_kernel(pos_ref, fill_ref, h_ref, xs_ref, zero_s, ztile_s, sem, fill_sem, tail_sem,
                     *, tm, nsub, tm_moe):
    def tile(ref, r):
        return ref.at[pl.ds(pl.multiple_of(r * nsub, nsub), nsub)]

    @pl.when(pl.program_id(0) == 0)
    def _():
        zero_s[...] = jnp.zeros_like(zero_s)
        ztile_s[...] = jnp.zeros_like(ztile_s)

        def fill_copy(r):
            return pltpu.make_async_copy(zero_s, tile(xs_ref, r), fill_sem)

        def tail_copy(t):
            rows = tm_moe * nsub
            return pltpu.make_async_copy(
                ztile_s, xs_ref.at[pl.ds(pl.multiple_of(t * rows, rows), rows)], tail_sem)

        def per_expert(e, total):
            lo, hi = fill_ref[LANES + e], fill_ref[2 * LANES + e]

            def one(r, c):
                fill_copy(r).start()
                return c

            lax.fori_loop(lo, hi, one, 0)
            return total + (hi - lo)

        total = lax.fori_loop(0, N_EXPERTS, per_expert, 0)
        n_tiles = xs_ref.shape[0] // (tm_moe * nsub)

        def tail_start(t, c):
            tail_copy(t).start()
            return c

        lax.fori_loop(fill_ref[0], n_tiles, tail_start, 0)

        def drain(_, c):
            fill_copy(0).wait()
            return c

        lax.fori_loop(0, total, drain, 0)

        def tail_drain(_, c):
            tail_copy(0).wait()
            return c

        lax.fori_loop(fill_ref[0], n_tiles, tail_drain, 0)

    def issue(j, c):
        for r in range(COMBINE_GROUP):
            row = pl.multiple_of(j * COMBINE_GROUP, COMBINE_GROUP) + r
            for k in range(TOP_K):
                pltpu.make_async_copy(tile(h_ref, row), tile(xs_ref, pos_ref[row * TOP_K + k]),
                                      sem).start(priority=k % 2)
        return c

    lax.fori_loop(0, tm // COMBINE_GROUP, issue, 0)
    for _ in range(TOP_K):
        pltpu.make_async_copy(h_ref, xs_ref.at[pl.ds(0, tm * nsub)], sem).wait()


def _dispatch(pos_flat, fill, h2, cap, nsub, tm_moe):
    n = h2.shape[0] // nsub
    tm = min(TM_ROWS, n)
    return pl.pallas_call(
        functools.partial(_dispatch_kernel, tm=tm, nsub=nsub, tm_moe=tm_moe),
        out_shape=jax.ShapeDtypeStruct((cap * nsub, LANES), F32),
        grid=(n // tm,),
        in_specs=[pl.BlockSpec((tm * TOP_K,), lambda i: (i,), memory_space=pltpu.SMEM),
                  pl.BlockSpec(fill.shape, lambda i: (0,), memory_space=pltpu.SMEM),
                  pl.BlockSpec((tm * nsub, LANES), lambda i: (i, 0))],
        out_specs=pl.BlockSpec(memory_space=pl.ANY),
        scratch_shapes=[pltpu.VMEM((nsub, LANES), F32), pltpu.VMEM((tm_moe * nsub, LANES), F32),
                        pltpu.SemaphoreType.DMA(()), pltpu.SemaphoreType.DMA(()),
                        pltpu.SemaphoreType.DMA(())],
        compiler_params=_params(("arbitrary",)),
    )(pos_flat, fill, h2)


def _moe_kernel(te_ref, nv_ref, x_ref, bg_ref, bu_ref, bd_ref, wg_hbm, wu_hbm, wd_hbm,
                y_ref, wf_s, wg_s, wu_s, wd_s, slot_s, sem):
    i = pl.program_id(0)
    nv = nv_ref[0]
    last = te_ref.shape[0] - 1

    def weight_copies(e, slot):
        return [pltpu.make_async_copy(w.at[e], wf_s.at[slot, j], sem.at[slot])
                for j, w in enumerate((wg_hbm, wu_hbm, wd_hbm))]

    @pl.when(i < nv)
    def _():
        e = te_ref[i]

        @pl.when(i == 0)
        def _():
            slot_s[0] = 0
            for c in weight_copies(e, 0):
                c.start()

        @pl.when((i == 0) | (e != te_ref[jnp.maximum(i - 1, 0)]))
        def _():
            slot = slot_s[0]
            for c in weight_copies(e, slot):
                c.wait()
            nxt = lax.while_loop(lambda j: (j < nv) & (te_ref[jnp.minimum(j, last)] == e),
                                 lambda j: j + 1, i + 1)

            @pl.when(nxt < nv)
            def _():
                for c in weight_copies(te_ref[jnp.minimum(nxt, last)], 1 - slot):
                    c.start()

            wg_s[...] = wf_s[slot, 0].astype(BF16)
            wu_s[...] = wf_s[slot, 1].astype(BF16)
            wd_s[...] = wf_s[slot, 2].astype(BF16)
            slot_s[0] = 1 - slot

        xb = _load_token_tiles(x_ref, wg_s.shape[0] // LANES).astype(BF16)
        g = _dot(xb, wg_s[...]) + bg_ref[0]
        u = _dot(xb, wu_s[...]) + bu_ref[0]
        g = jnp.minimum(g, SWIGLU_LIMIT)
        u = jnp.clip(u, -SWIGLU_LIMIT, SWIGLU_LIMIT)
        h = (u + 1.0) * (g * jax.nn.sigmoid(SWIGLU_ALPHA * g))
        _store_token_tiles(y_ref, _dot(h.astype(BF16), wd_s[...]) + bd_ref[0])

    @pl.when(i >= nv)
    def _():
        y_ref[...] = jnp.zeros_like(y_ref)


def _moe(tile_expert, n_valid, xs, w_gate, b_gate, w_up, b_up, w_down, b_down, tm):
    e, d, f = w_gate.shape
    assert d == f
    nsub = d // LANES
    cap = xs.shape[0] // nsub
    n_tiles = cap // tm

    def tile(i, te, nv):
        return jnp.minimum(i, nv[0] - 1)

    exp3 = lambda i, te, nv: (te[tile(i, te, nv)], 0, 0)
    grid_spec = pltpu.PrefetchScalarGridSpec(
        num_scalar_prefetch=2,
        grid=(n_tiles,),
        in_specs=[pl.BlockSpec((tm * nsub, LANES), lambda i, te, nv: (tile(i, te, nv), 0)),
                  pl.BlockSpec((1, 1, f), exp3), pl.BlockSpec((1, 1, f), exp3),
                  pl.BlockSpec((1, 1, d), exp3),
                  pl.BlockSpec(memory_space=pl.ANY), pl.BlockSpec(memory_space=pl.ANY),
                  pl.BlockSpec(memory_space=pl.ANY)],
        out_specs=pl.BlockSpec((tm * nsub, LANES), lambda i, te, nv: (i, 0)),
        scratch_shapes=[pltpu.VMEM((2, 3, d, f), F32),
                        pltpu.VMEM((d, f), BF16), pltpu.VMEM((d, f), BF16), pltpu.VMEM((f, d), BF16),
                        pltpu.SMEM((1,), jnp.int32), pltpu.SemaphoreType.DMA((2,))],
    )
    return pl.pallas_call(
        _moe_kernel,
        out_shape=jax.ShapeDtypeStruct((cap * nsub, LANES), F32),
        grid_spec=grid_spec,
        compiler_params=_params(("arbitrary",), 56),
    )(tile_expert, n_valid, xs, b_gate.reshape(e, 1, f), b_up.reshape(e, 1, f),
      b_down.reshape(e, 1, d), w_gate, w_up, w_down)


def _combine_kernel(pos_ref, posn_ref, w_ref, x1_ref, mod_ref, g_ref, b_ref, y_ref, o_ref,
                    buf, sem, *, tm):
    i = pl.program_id(0)
    n = pl.num_programs(0)
    nsub = x1_ref.shape[1] // LANES

    group = COMBINE_GROUP

    def issue_rows(p_ref, slot, r0):
        for r in range(group):
            for k in range(TOP_K):
                row = r0 + r
                src = pl.ds(pl.multiple_of(p_ref[row * TOP_K + k] * nsub, nsub), nsub)
                dst = pl.ds(pl.multiple_of(row * nsub, nsub), nsub)
                pltpu.make_async_copy(y_ref.at[src], buf.at[slot, k, dst],
                                      sem.at[slot]).start(priority=k % 2)

    def reduce_rows(slot, r0):
        rows = pl.ds(r0, group)
        w = w_ref[rows, :]
        f = jnp.zeros((group, nsub * LANES), F32)
        for k in range(TOP_K):
            yk = jnp.concatenate(
                [buf[slot, k, pl.ds(r0 * nsub + s, group, stride=nsub), :] for s in range(nsub)],
                axis=1)
            f = f + w[:, k:k + 1] * yk
        o_ref[rows, :] = f

    @pl.when(i == 0)
    def _():
        def body(j, c):
            issue_rows(pos_ref, 0, pl.multiple_of(j * group, group))
            return c
        lax.fori_loop(0, tm // group, body, 0)

    slot = i % 2
    for k in range(TOP_K):
        pltpu.make_async_copy(y_ref.at[pl.ds(0, tm * nsub)], buf.at[slot, k], sem.at[slot]).wait()

    @pl.when(i + 1 < n)
    def _():
        def body(j, c):
            r0 = pl.multiple_of(j * group, group)
            issue_rows(posn_ref, 1 - slot, r0)
            reduce_rows(slot, r0)
            return c
        lax.fori_loop(0, tm // group, body, 0)

    @pl.when(i + 1 == n)
    def _():
        def body(j, c):
            reduce_rows(slot, pl.multiple_of(j * group, group))
            return c
        lax.fori_loop(0, tm // group, body, 0)

    o_ref[...] = _layer_norm(DEEPNORM_ALPHA * x1_ref[...] + mod_ref[0, 5:6, :] * o_ref[...],
                             g_ref[...], b_ref[...])


def _combine(pos_flat, w, x1, mod3, g2, b2, y, seq):
    n, d = x1.shape
    tm = min(TM_ROWS, seq)
    nsteps = n // tm
    per_seq = seq // tm
    return pl.pallas_call(
        functools.partial(_combine_kernel, tm=tm),
        out_shape=jax.ShapeDtypeStruct((n, d), F32),
        grid=(nsteps,),
        in_specs=[pl.BlockSpec((tm * TOP_K,), lambda i: (i,), memory_space=pltpu.SMEM),
                  pl.BlockSpec((tm * TOP_K,), lambda i: (jnp.minimum(i + 1, nsteps - 1),),
                               memory_space=pltpu.SMEM),
                  pl.BlockSpec((tm, LANES), lambda i: (i, 0)),
                  pl.BlockSpec((tm, d), lambda i: (i, 0)),
                  pl.BlockSpec((1, 6, d), lambda i: (i // per_seq, 0, 0)),
                  pl.BlockSpec(g2.shape, lambda i: (0, 0)),
                  pl.BlockSpec(b2.shape, lambda i: (0, 0)),
                  pl.BlockSpec(memory_space=pl.ANY)],
        out_specs=pl.BlockSpec((tm, d), lambda i: (i, 0)),
        scratch_shapes=[pltpu.VMEM((2, TOP_K, tm * (d // LANES), LANES), F32),
                        pltpu.SemaphoreType.DMA((2,))],
        compiler_params=_params(("arbitrary",), 32),
    )(pos_flat, pos_flat, w, x1, mod3, g2, b2, y)


def _prep_w_in(w_in):
    d = w_in.shape[0]
    a3 = 3 * A_WIDTH
    lat = a3 + B_Q_LORA + B_KV_LORA
    z = lambda n: jnp.zeros((d, n), w_in.dtype)
    return jnp.concatenate([
        w_in[:, :A_WIDTH] * (A_HEAD_DIM ** -0.5 * LOG2_E),
        w_in[:, A_WIDTH:lat],
        z(B_NOPE_DIM), w_in[:, lat:lat + B_ROPE_DIM], z(LANES - B_NOPE_DIM - B_ROPE_DIM),
        w_in[:, lat + B_ROPE_DIM:]], axis=1).astype(BF16)


def _prep_w_uq(w_uq):
    r = w_uq.shape[0]
    w = w_uq.reshape(r, B_HEADS, B_QK_DIM)
    w = jnp.pad(w, ((0, 0), (0, 0), (0, LANES - B_QK_DIM)))
    return w.reshape(r, B_HEADS * LANES).astype(BF16)


def _prep_w_ukv(w_ukv):
    r = w_ukv.shape[0]
    w = w_ukv.reshape(r, B_HEADS, B_NOPE_DIM + B_V_DIM)
    wk = jnp.pad(w[:, :, :B_NOPE_DIM], ((0, 0), (0, 0), (0, LANES - B_NOPE_DIM)))
    wv = w[:, :, B_NOPE_DIM:].reshape(r, B_HEADS // 2, 2 * B_V_DIM)
    wv = jnp.pad(wv, ((0, 0), (0, 0), (0, V_PAIR_LANES - 2 * B_V_DIM)))
    return (wk.reshape(r, B_HEADS * LANES).astype(BF16),
            wv.reshape(r, B_HEADS // 2 * V_PAIR_LANES).astype(BF16))


def _rope_tables(positions):
    half = B_ROPE_DIM // 2
    freqs = ROPE_THETA ** (-jnp.arange(half, dtype=F32) / half)
    bsz, seq = positions.shape
    ang_t = positions.astype(F32).reshape(1, -1, LANES) * freqs[:, None, None]
    cos_t, sin_t = lax.optimization_barrier((jnp.cos(ang_t), jnp.sin(ang_t)))
    cos = cos_t.reshape(half, bsz * seq).T.reshape(bsz, seq, half)
    sin = sin_t.reshape(half, bsz * seq).T.reshape(bsz, seq, half)
    z = jnp.zeros(cos.shape[:2] + (B_ROPE_DIM,), F32)
    return jnp.concatenate([-sin, sin, z, cos, cos, z], -1)


def kernel(x, c, positions, w_ada, b_ada, w_in, rms_q, w_uq, rms_kv, w_ukv, rel_bias,
           w_branch_a, w_branch_b, w_out, ln1_g, ln1_b, w_router, b_router,
           w_gate, b_gate, w_up, b_up, w_down, b_down, ln2_g, ln2_b):
    bsz, seq, d = x.shape
    n_tok = bsz * seq
    assert w_ada.shape[0] == DEPTH == 1
    row = lambda v: v.reshape(1, -1)

    mod3 = _ada(c, w_ada[0], b_ada[0]).reshape(bsz, 6, d)

    qk, va, cq, ckv, kr, ga, gb = _inproj(x, mod3, _prep_w_in(w_in[0]))
    ya = _attn_a(qk, va, _bias_table(rel_bias[0], min(TQ_A, seq)))
    wk, wv = _prep_w_ukv(w_ukv[0])
    qm, km, vm = _mla_proj(cq, ckv, kr, _rope_tables(positions), row(rms_q[0]), row(rms_kv[0]),
                           _prep_w_uq(w_uq[0]), wk, wv)
    yb = _mla_attn(qm, km, vm)

    wr = jnp.pad(w_router[0], ((0, 0), (0, LANES - N_EXPERTS)))
    br = jnp.pad(b_router[0], (0, LANES - N_EXPERTS), constant_values=NEG_INF)
    x1, h2, logits = _merge(x, ya, yb, ga, gb, mod3,
                            w_branch_a[0].astype(BF16), w_branch_b[0].astype(BF16),
                            w_out[0].astype(BF16), row(ln1_g[0]), row(ln1_b[0]), wr, row(br))

    tm = TM_MOE
    n_tiles = -(-(n_tok * TOP_K) // tm) + N_EXPERTS
    assert n_tiles <= MAX_TILES_PAD
    pos, wts, tmap, meta = _route(logits.reshape(n_tok, LANES), tm)
    pos_flat = pos[:, :TOP_K].reshape(-1)
    fill = meta[:3].reshape(-1)
    nsub = d // LANES
    xs = _dispatch(pos_flat, fill, h2.reshape(n_tok * nsub, LANES), n_tiles * tm, nsub, tm)
    y = _moe(tmap[:n_tiles, 0], meta[0, :1], xs, w_gate[0], b_gate[0], w_up[0], b_up[0],
             w_down[0], b_down[0], tm)
    out = _combine(pos_flat, wts, x1.reshape(n_tok, d), mod3, row(ln2_g[0]), row(ln2_b[0]), y, seq)
    return out.reshape(bsz, seq, d)
```

```python
import functools

import jax
import jax.numpy as jnp
from jax import lax
from jax.experimental import pallas as pl
from jax.experimental.pallas import tpu as pltpu

F32 = jnp.float32
BF16 = jnp.bfloat16

CHUNK = 64
A_HEADS = 8
A_HEAD_DIM = 64
A_LEFT_CHUNKS = 8
A_MAX_REL = 128
A_WIDTH = A_HEADS * A_HEAD_DIM
B_HEADS = 8
B_NOPE_DIM = 64
B_ROPE_DIM = 32
B_V_DIM = 64
B_Q_LORA = 384
B_KV_LORA = 256
B_QK_DIM = B_NOPE_DIM + B_ROPE_DIM
ROPE_THETA = 10000.0
N_EXPERTS = 32
TOP_K = 4
SWIGLU_LIMIT = 7.0
SWIGLU_ALPHA = 1.702
DEPTH = 1
DEEPNORM_ALPHA = (2.0 * DEPTH) ** 0.25
LN_EPS = 1e-5
RMS_EPS = 1e-6
NEG_INF = -1e30
LOG2_E = 1.4426950408889634

LANES = 128
MXU_COLS = 256
V_PAIR_LANES = MXU_COLS

TM_PROJ = 512
TQ_A = 256
A_QBLOCKS = 2
TQ_B = 512
TM_MERGE = 512
TR_ROUTE = 1024
TM_MOE = 512
TM_ROWS = 256
COMBINE_GROUP = 8
MAX_TILES_PAD = 512


def _params(sem, vmem_mb=None):
    return pltpu.CompilerParams(
        dimension_semantics=sem,
        vmem_limit_bytes=None if vmem_mb is None else vmem_mb << 20)


def _dot(a, b):
    return jnp.dot(a, b, preferred_element_type=F32)


def _dot_nt(a, b):
    return lax.dot_general(a, b, (((1,), (1,)), ((), ())), preferred_element_type=F32)


def _store_token_tiles(ref, x):
    rows, d = x.shape
    nsub = d // LANES
    for s in range(nsub):
        ref[pl.ds(s, rows, stride=nsub), :] = x[:, s * LANES:(s + 1) * LANES]


def _load_token_tiles(ref, nsub):
    rows = ref.shape[0] // nsub
    return jnp.concatenate([ref[pl.ds(s, rows, stride=nsub), :] for s in range(nsub)], axis=1)


def _layer_norm(x, g, b):
    mu = jnp.mean(x, axis=-1, keepdims=True)
    xc = x - mu
    var = jnp.mean(xc * xc, axis=-1, keepdims=True)
    return xc * lax.rsqrt(var + LN_EPS) * g + b


def _ada_kernel(c_ref, w_ref, b_ref, o_ref):
    c = c_ref[...]
    sc = c * jax.nn.sigmoid(c)
    o_ref[...] = jnp.dot(sc, w_ref[...], preferred_element_type=F32,
                         precision=lax.Precision.HIGHEST) + b_ref[...]


def _ada(c, w_ada, b_ada):
    bsz, d = c.shape
    m = w_ada.shape[1]
    tn = 1024
    return pl.pallas_call(
        _ada_kernel,
        out_shape=jax.ShapeDtypeStruct((bsz, m), F32),
        grid=(m // tn,),
        in_specs=[pl.BlockSpec((bsz, d), lambda j: (0, 0)),
                  pl.BlockSpec((d, tn), lambda j: (0, j)),
                  pl.BlockSpec((1, tn), lambda j: (0, j))],
        out_specs=pl.BlockSpec((bsz, tn), lambda j: (0, j)),
        compiler_params=_params(("arbitrary",)),
    )(c, w_ada, b_ada.reshape(1, m))


def _inproj_kernel(x_ref, mod_ref, w_ref, qk_ref, va_ref, cq_ref, ckv_ref, kr_ref, ga_ref, gb_ref):
    m = mod_ref[0]
    h = (x_ref[0] * (1.0 + m[1:2]) + m[0:1]).astype(BF16)
    col = 0
    for ref in (qk_ref, None, cq_ref, ckv_ref, kr_ref, ga_ref, gb_ref):
        if ref is None:
            v = _dot(h, w_ref[:, col:col + A_WIDTH]).astype(BF16)
            ones_blk = jnp.ones((v.shape[0], LANES), BF16)
            parts = []
            for hp in range(A_WIDTH // LANES):
                parts += [v[:, hp * LANES:(hp + 1) * LANES], ones_blk]
            va_ref[0] = jnp.concatenate(parts, axis=1)
            col += A_WIDTH
            continue
        n = ref.shape[-1]
        ref[0] = _dot(h, w_ref[:, col:col + n]).astype(ref.dtype)
        col += n


def _inproj(x, mod3, w):
    bsz, seq, d = x.shape
    tm = min(TM_PROJ, seq)
    widths = (2 * A_WIDTH, A_WIDTH // LANES * V_PAIR_LANES, B_Q_LORA, B_KV_LORA, LANES, d, d)
    dtypes = (BF16, BF16, BF16, BF16, F32, BF16, BF16)
    assert sum(widths) - widths[1] + A_WIDTH == w.shape[1]
    return pl.pallas_call(
        _inproj_kernel,
        out_shape=[jax.ShapeDtypeStruct((bsz, seq, n), dt) for n, dt in zip(widths, dtypes)],
        grid=(bsz, seq // tm),
        in_specs=[pl.BlockSpec((1, tm, d), lambda b, i: (b, i, 0)),
                  pl.BlockSpec((1, 6, d), lambda b, i: (b, 0, 0)),
                  pl.BlockSpec(w.shape, lambda b, i: (0, 0))],
        out_specs=[pl.BlockSpec((1, tm, n), lambda b, i: (b, i, 0)) for n in widths],
        compiler_params=_params(("arbitrary", "arbitrary"), 48),
    )(x, mod3, w)


def _attn_a_kernel(q_ref, k_ref, v_ref, bias_ref, o_ref, *, tq, nd):
    band = nd * tq
    lane = lax.broadcasted_iota(jnp.int32, (1, LANES), 1)
    for sub in range(A_QBLOCKS):
        i = pl.program_id(2) * A_QBLOCKS + sub
        start = pl.multiple_of(jnp.maximum(i - (nd - 1), 0) * tq, tq)
        variant = jnp.minimum(i, nd - 1)
        q = q_ref[0, sub * tq:(sub + 1) * tq, :]
        k = k_ref[0, pl.ds(start, band), :]
        v = v_ref[0, pl.ds(start, band), :]
        zero = jnp.zeros_like(k)
        kk = jnp.concatenate([jnp.where(lane < A_HEAD_DIM, k, zero),
                              jnp.where(lane >= A_HEAD_DIM, k, zero)], axis=0)
        s = _dot_nt(q, kk)
        ps = []
        for hh in range(2):
            sh = s[:, hh * band:(hh + 1) * band] + bias_ref[hh, variant]
            m = sh.max(axis=-1, keepdims=True)
            ps.append(jnp.exp2(sh - m).astype(BF16))
        acc = _dot(jnp.concatenate(ps, axis=0), v)
        out = acc[:, :LANES] / acc[:, LANES:]
        o_ref[0, sub * tq:(sub + 1) * tq, :] = jnp.where(
            lane < A_HEAD_DIM, out[:tq], out[tq:]).astype(o_ref.dtype)


def _attn_a(qk, va, bias_tab):
    bsz, seq, _ = qk.shape
    tq = min(TQ_A, seq)
    nd = bias_tab.shape[1]
    assert seq >= nd * tq
    ncb = A_WIDTH // LANES
    return pl.pallas_call(
        functools.partial(_attn_a_kernel, tq=tq, nd=nd),
        out_shape=jax.ShapeDtypeStruct((bsz, seq, A_WIDTH), BF16),
        grid=(ncb, bsz, seq // (A_QBLOCKS * tq)),
        in_specs=[pl.BlockSpec((1, A_QBLOCKS * tq, LANES), lambda hp, b, i: (b, i, hp)),
                  pl.BlockSpec((1, seq, LANES), lambda hp, b, i: (b, 0, ncb + hp)),
                  pl.BlockSpec((1, seq, V_PAIR_LANES), lambda hp, b, i: (b, 0, hp)),
                  pl.BlockSpec((2, nd, tq, nd * tq), lambda hp, b, i: (hp, 0, 0, 0))],
        out_specs=pl.BlockSpec((1, A_QBLOCKS * tq, LANES), lambda hp, b, i: (b, i, hp)),
        compiler_params=_params(("arbitrary",) * 3, 40),
    )(qk, qk, va, bias_tab)


def _bias_table(rel_bias, tq):
    nd = (A_LEFT_CHUNKS * CHUNK + tq - 1) // tq + 1
    width = nd * tq
    period = width + tq
    u = jnp.arange(period)
    dist = jnp.where(u < width, (nd - 1) * tq - u, (nd - 1) * tq + period - u)
    vec = rel_bias[:, jnp.clip(dist, -A_MAX_REL, A_MAX_REL) + A_MAX_REL].astype(F32)
    toep = jnp.tile(vec, (1, tq))[:, :tq * (period - 1)].reshape(-1, tq, period - 1)[:, :, :width]
    qi = jnp.arange(tq)[:, None]
    x = jnp.arange(width)[None, :]
    dchunk = qi // CHUNK - x // CHUNK + (nd - 1) * (tq // CHUNK)
    valid = (dchunk >= 0) & (dchunk <= A_LEFT_CHUNKS)
    table = jnp.where(valid[None], toep * LOG2_E, NEG_INF)
    variants = []
    for v in range(nd):
        cut = (nd - 1 - v) * tq
        variants.append(jnp.pad(table[:, :, cut:], ((0, 0), (0, 0), (0, cut)),
                                constant_values=NEG_INF))
    return jnp.stack(variants, axis=1)


def _mla_proj_kernel(cq_ref, ckv_ref, kr_ref, tab_ref, rq_ref, rkv_ref,
                     wq_ref, wk_ref, wv_ref, q_out, k_out, v_out):
    def rms(x, g):
        xf = x.astype(F32)
        ms = jnp.mean(xf * xf, axis=-1, keepdims=True)
        return (xf * lax.rsqrt(ms + RMS_EPS) * g).astype(BF16)

    cqn = rms(cq_ref[0], rq_ref[...])
    ckvn = rms(ckv_ref[0], rkv_ref[...])
    q = _dot(cqn, wq_ref[...])
    kn = _dot(ckvn, wk_ref[...])
    vlane = lax.broadcasted_iota(jnp.int32, (1, v_out.shape[-1]), 1)
    ones_lane = jnp.where(vlane % V_PAIR_LANES >= 2 * B_V_DIM, 1.0, 0.0)
    v_out[0] = (_dot(ckvn, wv_ref[...]) + ones_lane).astype(v_out.dtype)
    tab = tab_ref[...].T
    tlane = lax.broadcasted_iota(jnp.int32, tab.shape, 1)
    rope_end = B_NOPE_DIM + B_ROPE_DIM
    in_rope = (tlane >= B_NOPE_DIM) & (tlane < rope_end)
    cos_t = jnp.where(tlane < B_NOPE_DIM, 1.0, jnp.where(in_rope, tab, 0.0))
    sin_t = jnp.where(in_rope, pltpu.roll(tab, B_NOPE_DIM, 1), 0.0)
    first = tlane < B_NOPE_DIM + B_ROPE_DIM // 2
    sin_a = jnp.where(first, -sin_t, 0.0)
    sin_b = jnp.where(first, 0.0, sin_t)

    def rope(x, tables):
        c, sa, sb = tables
        return (x * c + pltpu.roll(x, LANES - B_ROPE_DIM // 2, 1) * sa
                + pltpu.roll(x, B_ROPE_DIM // 2, 1) * sb)

    kpe = rope(kr_ref[0], (cos_t, sin_a, sin_b))
    scale = B_QK_DIM ** -0.5 * LOG2_E
    q_tables = (cos_t * scale, sin_a * scale, sin_b * scale)
    for h in range(B_HEADS):
        sl = slice(h * LANES, (h + 1) * LANES)
        q_out[0, :, sl] = rope(q[:, sl], q_tables).astype(q_out.dtype)
        k_out[0, :, sl] = (kn[:, sl] + kpe).astype(k_out.dtype)


def _mla_proj(cq, ckv, kr, rope_tab, rms_q, rms_kv, wq, wk, wv):
    bsz, seq, _ = cq.shape
    tm = min(TM_PROJ, seq)
    tok = lambda n: pl.BlockSpec((1, tm, n), lambda b, i: (b, i, 0))
    full = lambda a: pl.BlockSpec(a.shape, lambda b, i: (0,) * a.ndim)
    return pl.pallas_call(
        _mla_proj_kernel,
        out_shape=[jax.ShapeDtypeStruct((bsz, seq, B_HEADS * LANES), BF16),
                   jax.ShapeDtypeStruct((bsz, seq, B_HEADS * LANES), BF16),
                   jax.ShapeDtypeStruct((bsz, seq, wv.shape[1]), BF16)],
        grid=(bsz, seq // tm),
        in_specs=[tok(B_Q_LORA), tok(B_KV_LORA), tok(LANES),
                  pl.BlockSpec((LANES, tm), lambda b, i: (0, b * (seq // tm) + i)),
                  full(rms_q), full(rms_kv), full(wq), full(wk), full(wv)],
        out_specs=[tok(B_HEADS * LANES), tok(B_HEADS * LANES), tok(wv.shape[1])],
        compiler_params=_params(("arbitrary", "arbitrary"), 32),
    )(cq, ckv, kr, rope_tab, rms_q, rms_kv, wq, wk, wv)


def _mla_attn_kernel(q_ref, k_ref, v_ref, o_ref, s_a, s_b, m_s, acc_s, *, t):
    i = pl.program_id(2)
    lane = lax.broadcasted_iota(jnp.int32, (1, LANES), 1)
    m_s[...] = jnp.full(m_s.shape, NEG_INF, F32)
    acc_s[...] = jnp.zeros(acc_s.shape, F32)

    def produce(kj, s_ref):
        start = pl.multiple_of(kj * t, t)
        for hh in range(2):
            q = q_ref[0, :, hh * LANES:(hh + 1) * LANES]
            k = k_ref[0, pl.ds(start, t), hh * LANES:(hh + 1) * LANES]
            s_ref[hh] = _dot_nt(q, k)

    def consume(kj, s_ref, masked):
        start = pl.multiple_of(kj * t, t)
        v = v_ref[0, pl.ds(start, t), :]
        for hh in range(2):
            s = s_ref[hh]
            if masked:
                row_chunk = lax.broadcasted_iota(jnp.int32, (t, t), 0) // CHUNK
                col_chunk = lax.broadcasted_iota(jnp.int32, (t, t), 1) // CHUNK
                s = jnp.where(col_chunk <= row_chunk, s, NEG_INF)
            m_old = m_s[hh]
            m_new = jnp.maximum(m_old, jnp.broadcast_to(s.max(axis=-1, keepdims=True), m_old.shape))
            p = jnp.exp2(s - jnp.tile(m_new, (1, t // LANES))).astype(BF16)
            alpha = jnp.exp2(m_old - m_new)
            acc_s[hh] = jnp.tile(alpha, (1, V_PAIR_LANES // LANES)) * acc_s[hh] + _dot(p, v)
            m_s[hh] = m_new

    def body(pair, c):
        kj = 2 * pair
        produce(kj + 1, s_b)
        consume(kj, s_a, False)
        produce(kj + 2, s_a)
        consume(kj + 1, s_b, False)
        return c

    produce(0, s_a)
    lax.fori_loop(0, i // 2, body, 0)

    @pl.when(i % 2 == 0)
    def _():
        consume(i, s_a, True)

    @pl.when(i % 2 == 1)
    def _():
        produce(i, s_b)
        consume(i - 1, s_a, False)
        consume(i, s_b, True)

    outs = []
    for hh in range(2):
        acc = acc_s[hh]
        outs.append(acc[:, :LANES] / acc[:, LANES:])
    o_ref[0] = jnp.where(lane < B_V_DIM, outs[0], outs[1]).astype(o_ref.dtype)


def _mla_attn(qm, km, vm):
    bsz, seq, _ = qm.shape
    t = min(TQ_B, seq)
    return pl.pallas_call(
        functools.partial(_mla_attn_kernel, t=t),
        out_shape=jax.ShapeDtypeStruct((bsz, seq, B_HEADS * B_V_DIM), BF16),
        grid=(bsz, B_HEADS // 2, seq // t),
        in_specs=[pl.BlockSpec((1, t, 2 * LANES), lambda b, hp, i: (b, i, hp)),
                  pl.BlockSpec((1, seq, 2 * LANES), lambda b, hp, i: (b, 0, hp)),
                  pl.BlockSpec((1, seq, V_PAIR_LANES), lambda b, hp, i: (b, 0, hp))],
        out_specs=pl.BlockSpec((1, t, LANES), lambda b, hp, i: (b, i, hp)),
        scratch_shapes=[pltpu.VMEM((2, t, t), F32), pltpu.VMEM((2, t, t), F32),
                        pltpu.VMEM((2, t, LANES), F32),
                        pltpu.VMEM((2, t, V_PAIR_LANES), F32)],
        compiler_params=_params(("arbitrary",) * 3, 40),
    )(qm, km, vm)


def _merge_kernel(x_ref, ya_ref, yb_ref, ga_ref, gb_ref, mod_ref, wa_ref, wb_ref, wo_ref,
                  g1_ref, b1_ref, wrh_ref, wrl_ref, br_ref, x1_ref, h2_ref, lg_ref):
    m = mod_ref[0]
    a = _dot(ya_ref[0], wa_ref[...])
    b = _dot(yb_ref[0], wb_ref[...])
    merged = (jax.nn.sigmoid(ga_ref[0].astype(F32)) * a
              + jax.nn.sigmoid(gb_ref[0].astype(F32)) * b)
    o = _dot(merged.astype(BF16), wo_ref[...])
    x1 = _layer_norm(DEEPNORM_ALPHA * x_ref[0] + m[2:3] * o, g1_ref[...], b1_ref[...])
    h2 = x1 * (1.0 + m[4:5]) + m[3:4]
    x1_ref[0] = x1
    _store_token_tiles(h2_ref.at[0], h2)
    h_hi = h2.astype(BF16)
    h_lo = (h2 - h_hi.astype(F32)).astype(BF16)
    lg_ref[0] = (_dot(h_hi, wrh_ref[...]) + _dot(h_hi, wrl_ref[...]) + _dot(h_lo, wrh_ref[...])
                 + br_ref[...])


def _merge(x, ya, yb, ga, gb, mod3, wa, wb, wo, g1, b1, wr, br):
    wr_hi = wr.astype(BF16)
    wr_lo = (wr - wr_hi.astype(F32)).astype(BF16)
    bsz, seq, d = x.shape
    tm = min(TM_MERGE, seq)
    tok = lambda n: pl.BlockSpec((1, tm, n), lambda b, i: (b, i, 0))
    full = lambda a: pl.BlockSpec(a.shape, lambda b, i: (0,) * a.ndim)
    return pl.pallas_call(
        _merge_kernel,
        out_shape=[jax.ShapeDtypeStruct((bsz, seq, d), F32),
                   jax.ShapeDtypeStruct((bsz, seq * (d // LANES), LANES), F32),
                   jax.ShapeDtypeStruct((bsz, seq, LANES), F32)],
        grid=(bsz, seq // tm),
        in_specs=[tok(d), tok(A_WIDTH), tok(B_HEADS * B_V_DIM), tok(d), tok(d),
                  pl.BlockSpec((1, 6, d), lambda b, i: (b, 0, 0)),
                  full(wa), full(wb), full(wo), full(g1), full(b1), full(wr_hi), full(wr_lo),
                  full(br)],
        out_specs=[tok(d),
                   pl.BlockSpec((1, tm * (d // LANES), LANES), lambda b, i: (b, i, 0)),
                   tok(LANES)],
        compiler_params=_params(("arbitrary", "arbitrary"), 48),
    )(x, ya, yb, ga, gb, mod3, wa, wb, wo, g1, b1, wr_hi, wr_lo, br)


def _route_kernel(lg_ref, pos_ref, w_ref, tmap_ref, nv_ref, cnt_s, carry_s, pstart_s, *, tr, tm):
    phase = pl.program_id(0)
    i = pl.program_id(1)
    lane = lax.broadcasted_iota(jnp.int32, (tr, LANES), 1)
    work = lg_ref[...]
    sels, vals = [], []
    for _ in range(TOP_K):
        m = work.max(axis=-1, keepdims=True)
        idx = jnp.where(work == m, lane, LANES).min(axis=-1, keepdims=True)
        sel = lane == idx
        work = jnp.where(sel, -jnp.inf, work)
        sels.append(sel)
        vals.append(m)
    picked = sum(jnp.where(s, 1.0, 0.0) for s in sels)
    tile_count = jnp.broadcast_to(picked.sum(axis=0, keepdims=True), (8, LANES))

    @pl.when((phase == 0) & (i == 0))
    def _():
        cnt_s[...] = jnp.zeros_like(cnt_s)

    @pl.when(phase == 0)
    def _():
        cnt_s[...] += tile_count

    @pl.when((phase == 1) & (i == 0))
    def _():
        lane8 = lax.broadcasted_iota(jnp.int32, (8, LANES), 1)
        padded = jnp.floor((cnt_s[...] + (tm - 1)) * (1.0 / tm)) * tm
        cum = padded
        shift = 1
        while shift < N_EXPERTS:
            cum = cum + jnp.where(lane8 >= shift, pltpu.roll(cum, shift, 1), 0.0)
            shift *= 2
        pstart_s[...] = cum - padded
        carry_s[...] = jnp.zeros_like(carry_s)
        nt = tmap_ref.shape[0]
        tile_start = (lax.broadcasted_iota(jnp.int32, (nt, LANES), 0) * tm).astype(F32)
        lane_t = lax.broadcasted_iota(jnp.int32, (nt, LANES), 1)
        done = jnp.where((lane_t < N_EXPERTS) & (cum[0:1] <= tile_start), 1.0, 0.0)
        te = jnp.minimum(done.sum(axis=-1, keepdims=True), N_EXPERTS - 1.0)
        tmap_ref[...] = jnp.broadcast_to(te, (nt, LANES)).astype(jnp.int32)
        total = jnp.where(lane8 == N_EXPERTS - 1, cum, 0.0).sum(axis=-1, keepdims=True)
        row8 = lax.broadcasted_iota(jnp.int32, (8, LANES), 0)
        meta = jnp.where(row8 == 0, jnp.broadcast_to(total * (1.0 / tm), (8, LANES)),
                         jnp.where(row8 == 1, cum - padded + cnt_s[...], cum))
        nv_ref[...] = meta.astype(jnp.int32)

    @pl.when(phase == 1)
    def _():
        r = lax.broadcasted_iota(jnp.int32, (tr, tr), 0)
        c = lax.broadcasted_iota(jnp.int32, (tr, tr), 1)
        tri = jnp.where(c < r, 1.0, 0.0).astype(BF16)
        before = _dot(tri, picked.astype(BF16)) + carry_s[0:1]
        base = before + pstart_s[0:1]
        carry_s[...] += tile_count
        den = sum(jnp.exp(v - vals[0]) for v in vals)
        pos = jnp.zeros((tr, LANES), F32)
        wts = jnp.zeros((tr, LANES), F32)
        for k in range(TOP_K):
            pk = jnp.where(sels[k], base, 0.0).sum(axis=-1, keepdims=True)
            pos = jnp.where(lane == k, pk, pos)
            wts = jnp.where(lane == k, jnp.exp(vals[k] - vals[0]) / den, wts)
        pos_ref[...] = pos.astype(jnp.int32)
        w_ref[...] = wts


def _route(logits, tm):
    n = logits.shape[0]
    tr = min(TR_ROUTE, n)
    return pl.pallas_call(
        functools.partial(_route_kernel, tr=tr, tm=tm),
        out_shape=[jax.ShapeDtypeStruct((n, LANES), jnp.int32),
                   jax.ShapeDtypeStruct((n, LANES), F32),
                   jax.ShapeDtypeStruct((MAX_TILES_PAD, LANES), jnp.int32),
                   jax.ShapeDtypeStruct((8, LANES), jnp.int32)],
        grid=(2, n // tr),
        in_specs=[pl.BlockSpec((tr, LANES), lambda p, i: (i, 0))],
        out_specs=[pl.BlockSpec((tr, LANES), lambda p, i: (i * p, 0)),
                   pl.BlockSpec((tr, LANES), lambda p, i: (i * p, 0)),
                   pl.BlockSpec((MAX_TILES_PAD, LANES), lambda p, i: (0, 0)),
                   pl.BlockSpec((8, LANES), lambda p, i: (0, 0))],
        scratch_shapes=[pltpu.VMEM((8, LANES), F32)] * 3,
        compiler_params=_params(("arbitrary", "arbitrary")),
    )(logits)


def _dispatch_kernel(pos_ref, fill_ref, h_ref, xs_ref, zero_s, ztile_s, sem, fill_sem, tail_sem,
                     *, tm, nsub, tm_moe):
    def tile(ref, r):
        return ref.at[pl.ds(pl.multiple_of(r * nsub, nsub), nsub)]

    @pl.when(pl.program_id(0) == 0)
    def _():
        zero_s[...] = jnp.zeros_like(zero_s)
        ztile_s[...] = jnp.zeros_like(ztile_s)

        def fill_copy(r):
            return pltpu.make_async_copy(zero_s, tile(xs_ref, r), fill_sem)

        def tail_copy(t):
            rows = tm_moe * nsub
            return pltpu.make_async_copy(
                ztile_s, xs_ref.at[pl.ds(pl.multiple_of(t * rows, rows), rows)], tail_sem)

        def per_expert(e, total):
            lo, hi = fill_ref[LANES + e], fill_ref[2 * LANES + e]

            def one(r, c):
                fill_copy(r).start()
                return c

            lax.fori_loop(lo, hi, one, 0)
            return total + (hi - lo)

        total = lax.fori_loop(0, N_EXPERTS, per_expert, 0)
        n_tiles = xs_ref.shape[0] // (tm_moe * nsub)

        def tail_start(t, c):
            tail_copy(t).start()
            return c

        lax.fori_loop(fill_ref[0], n_tiles, tail_start, 0)

        def drain(_, c):
            fill_copy(0).wait()
            return c

        lax.fori_loop(0, total, drain, 0)

        def tail_drain(_, c):
            tail_copy(0).wait()
            return c

        lax.fori_loop(fill_ref[0], n_tiles, tail_drain, 0)

    def issue(j, c):
        for r in range(COMBINE_GROUP):
            row = pl.multiple_of(j * COMBINE_GROUP, COMBINE_GROUP) + r
            for k in range(TOP_K):
                pltpu.make_async_copy(tile(h_ref, row), tile(xs_ref, pos_ref[row * TOP_K + k]),
                                      sem).start(priority=k % 2)
        return c

    lax.fori_loop(0, tm // COMBINE_GROUP, issue, 0)
    for _ in range(TOP_K):
        pltpu.make_async_copy(h_ref, xs_ref.at[pl.ds(0, tm * nsub)], sem).wait()


def _dispatch(pos_flat, fill, h2, cap, nsub, tm_moe):
    n = h2.shape[0] // nsub
    tm = min(TM_ROWS, n)
    return pl.pallas_call(
        functools.partial(_dispatch_kernel, tm=tm, nsub=nsub, tm_moe=tm_moe),
        out_shape=jax.ShapeDtypeStruct((cap * nsub, LANES), F32),
        grid=(n // tm,),
        in_specs=[pl.BlockSpec((tm * TOP_K,), lambda i: (i,), memory_space=pltpu.SMEM),
                  pl.BlockSpec(fill.shape, lambda i: (0,), memory_space=pltpu.SMEM),
                  pl.BlockSpec((tm * nsub, LANES), lambda i: (i, 0))],
        out_specs=pl.BlockSpec(memory_space=pl.ANY),
        scratch_shapes=[pltpu.VMEM((nsub, LANES), F32), pltpu.VMEM((tm_moe * nsub, LANES), F32),
                        pltpu.SemaphoreType.DMA(()), pltpu.SemaphoreType.DMA(()),
                        pltpu.SemaphoreType.DMA(())],
        compiler_params=_params(("arbitrary",)),
    )(pos_flat, fill, h2)


def _moe_kernel(te_ref, nv_ref, x_ref, bg_ref, bu_ref, bd_ref, wg_hbm, wu_hbm, wd_hbm,
                y_ref, wf_s, wg_s, wu_s, wd_s, slot_s, sem):
    i = pl.program_id(0)
    nv = nv_ref[0]
    last = te_ref.shape[0] - 1

    def weight_copies(e, slot):
        return [pltpu.make_async_copy(w.at[e], wf_s.at[slot, j], sem.at[slot])
                for j, w in enumerate((wg_hbm, wu_hbm, wd_hbm))]

    @pl.when(i < nv)
    def _():
        e = te_ref[i]

        @pl.when(i == 0)
        def _():
            slot_s[0] = 0
            for c in weight_copies(e, 0):
                c.start()

        @pl.when((i == 0) | (e != te_ref[jnp.maximum(i - 1, 0)]))
        def _():
            slot = slot_s[0]
            for c in weight_copies(e, slot):
                c.wait()
            nxt = lax.while_loop(lambda j: (j < nv) & (te_ref[jnp.minimum(j, last)] == e),
                                 lambda j: j + 1, i + 1)

            @pl.when(nxt < nv)
            def _():
                for c in weight_copies(te_ref[jnp.minimum(nxt, last)], 1 - slot):
                    c.start()

            wg_s[...] = wf_s[slot, 0].astype(BF16)
            wu_s[...] = wf_s[slot, 1].astype(BF16)
            wd_s[...] = wf_s[slot, 2].astype(BF16)
            slot_s[0] = 1 - slot

        xb = _load_token_tiles(x_ref, wg_s.shape[0] // LANES).astype(BF16)
        g = _dot(xb, wg_s[...]) + bg_ref[0]
        u = _dot(xb, wu_s[...]) + bu_ref[0]
        g = jnp.minimum(g, SWIGLU_LIMIT)
        u = jnp.clip(u, -SWIGLU_LIMIT, SWIGLU_LIMIT)
        h = (u + 1.0) * (g * jax.nn.sigmoid(SWIGLU_ALPHA * g))
        _store_token_tiles(y_ref, _dot(h.astype(BF16), wd_s[...]) + bd_ref[0])

    @pl.when(i >= nv)
    def _():
        y_ref[...] = jnp.zeros_like(y_ref)


def _moe(tile_expert, n_valid, xs, w_gate, b_gate, w_up, b_up, w_down, b_down, tm):
    e, d, f = w_gate.shape
    assert d == f
    nsub = d // LANES
    cap = xs.shape[0] // nsub
    n_tiles = cap // tm

    def tile(i, te, nv):
        return jnp.minimum(i, nv[0] - 1)

    exp3 = lambda i, te, nv: (te[tile(i, te, nv)], 0, 0)
    grid_spec = pltpu.PrefetchScalarGridSpec(
        num_scalar_prefetch=2,
        grid=(n_tiles,),
        in_specs=[pl.BlockSpec((tm * nsub, LANES), lambda i, te, nv: (tile(i, te, nv), 0)),
                  pl.BlockSpec((1, 1, f), exp3), pl.BlockSpec((1, 1, f), exp3),
                  pl.BlockSpec((1, 1, d), exp3),
                  pl.BlockSpec(memory_space=pl.ANY), pl.BlockSpec(memory_space=pl.ANY),
                  pl.BlockSpec(memory_space=pl.ANY)],
        out_specs=pl.BlockSpec((tm * nsub, LANES), lambda i, te, nv: (i, 0)),
        scratch_shapes=[pltpu.VMEM((2, 3, d, f), F32),
                        pltpu.VMEM((d, f), BF16), pltpu.VMEM((d, f), BF16), pltpu.VMEM((f, d), BF16),
                        pltpu.SMEM((1,), jnp.int32), pltpu.SemaphoreType.DMA((2,))],
    )
    return pl.pallas_call(
        _moe_kernel,
        out_shape=jax.ShapeDtypeStruct((cap * nsub, LANES), F32),
        grid_spec=grid_spec,
        compiler_params=_params(("arbitrary",), 56),
    )(tile_expert, n_valid, xs, b_gate.reshape(e, 1, f), b_up.reshape(e, 1, f),
      b_down.reshape(e, 1, d), w_gate, w_up, w_down)


def _combine_kernel(pos_ref, posn_ref, w_ref, x1_ref, mod_ref, g_ref, b_ref, y_ref, o_ref,
                    buf, sem, *, tm):
    i = pl.program_id(0)
    n = pl.num_programs(0)
    nsub = x1_ref.shape[1] // LANES

    group = COMBINE_GROUP

    def issue_rows(p_ref, slot, r0):
        for r in range(group):
            for k in range(TOP_K):
                row = r0 + r
                src = pl.ds(pl.multiple_of(p_ref[row * TOP_K + k] * nsub, nsub), nsub)
                dst = pl.ds(pl.multiple_of(row * nsub, nsub), nsub)
                pltpu.make_async_copy(y_ref.at[src], buf.at[slot, k, dst],
                                      sem.at[slot]).start(priority=k % 2)

    def reduce_rows(slot, r0):
        rows = pl.ds(r0, group)
        w = w_ref[rows, :]
        f = jnp.zeros((group, nsub * LANES), F32)
        for k in range(TOP_K):
            yk = jnp.concatenate(
                [buf[slot, k, pl.ds(r0 * nsub + s, group, stride=nsub), :] for s in range(nsub)],
                axis=1)
            f = f + w[:, k:k + 1] * yk
        o_ref[rows, :] = f

    @pl.when(i == 0)
    def _():
        def body(j, c):
            issue_rows(pos_ref, 0, pl.multiple_of(j * group, group))
            return c
        lax.fori_loop(0, tm // group, body, 0)

    slot = i % 2
    for k in range(TOP_K):
        pltpu.make_async_copy(y_ref.at[pl.ds(0, tm * nsub)], buf.at[slot, k], sem.at[slot]).wait()

    @pl.when(i + 1 < n)
    def _():
        def body(j, c):
            r0 = pl.multiple_of(j * group, group)
            issue_rows(posn_ref, 1 - slot, r0)
            reduce_rows(slot, r0)
            return c
        lax.fori_loop(0, tm // group, body, 0)

    @pl.when(i + 1 == n)
    def _():
        def body(j, c):
            reduce_rows(slot, pl.multiple_of(j * group, group))
            return c
        lax.fori_loop(0, tm // group, body, 0)

    o_ref[...] = _layer_norm(DEEPNORM_ALPHA * x1_ref[...] + mod_ref[0, 5:6, :] * o_ref[...],
                             g_ref[...], b_ref[...])


def _combine(pos_flat, w, x1, mod3, g2, b2, y, seq):
    n, d = x1.shape
    tm = min(TM_ROWS, seq)
    nsteps = n // tm
    per_seq = seq // tm
    return pl.pallas_call(
        functools.partial(_combine_kernel, tm=tm),
        out_shape=jax.ShapeDtypeStruct((n, d), F32),
        grid=(nsteps,),
        in_specs=[pl.BlockSpec((tm * TOP_K,), lambda i: (i,), memory_space=pltpu.SMEM),
                  pl.BlockSpec((tm * TOP_K,), lambda i: (jnp.minimum(i + 1, nsteps - 1),),
                               memory_space=pltpu.SMEM),
                  pl.BlockSpec((tm, LANES), lambda i: (i, 0)),
                  pl.BlockSpec((tm, d), lambda i: (i, 0)),
                  pl.BlockSpec((1, 6, d), lambda i: (i // per_seq, 0, 0)),
                  pl.BlockSpec(g2.shape, lambda i: (0, 0)),
                  pl.BlockSpec(b2.shape, lambda i: (0, 0)),
                  pl.BlockSpec(memory_space=pl.ANY)],
        out_specs=pl.BlockSpec((tm, d), lambda i: (i, 0)),
        scratch_shapes=[pltpu.VMEM((2, TOP_K, tm * (d // LANES), LANES), F32),
                        pltpu.SemaphoreType.DMA((2,))],
        compiler_params=_params(("arbitrary",), 32),
    )(pos_flat, pos_flat, w, x1, mod3, g2, b2, y)


def _prep_w_in(w_in):
    d = w_in.shape[0]
    a3 = 3 * A_WIDTH
    lat = a3 + B_Q_LORA + B_KV_LORA
    z = lambda n: jnp.zeros((d, n), w_in.dtype)
    return jnp.concatenate([
        w_in[:, :A_WIDTH] * (A_HEAD_DIM ** -0.5 * LOG2_E),
        w_in[:, A_WIDTH:lat],
        z(B_NOPE_DIM), w_in[:, lat:lat + B_ROPE_DIM], z(LANES - B_NOPE_DIM - B_ROPE_DIM),
        w_in[:, lat + B_ROPE_DIM:]], axis=1).astype(BF16)


def _prep_w_uq(w_uq):
    r = w_uq.shape[0]
    w = w_uq.reshape(r, B_HEADS, B_QK_DIM)
    w = jnp.pad(w, ((0, 0), (0, 0), (0, LANES - B_QK_DIM)))
    return w.reshape(r, B_HEADS * LANES).astype(BF16)


def _prep_w_ukv(w_ukv):
    r = w_ukv.shape[0]
    w = w_ukv.reshape(r, B_HEADS, B_NOPE_DIM + B_V_DIM)
    wk = jnp.pad(w[:, :, :B_NOPE_DIM], ((0, 0), (0, 0), (0, LANES - B_NOPE_DIM)))
    wv = w[:, :, B_NOPE_DIM:].reshape(r, B_HEADS // 2, 2 * B_V_DIM)
    wv = jnp.pad(wv, ((0, 0), (0, 0), (0, V_PAIR_LANES - 2 * B_V_DIM)))
    return (wk.reshape(r, B_HEADS * LANES).astype(BF16),
            wv.reshape(r, B_HEADS // 2 * V_PAIR_LANES).astype(BF16))


def _rope_tables(positions):
    half = B_ROPE_DIM // 2
    freqs = ROPE_THETA ** (-jnp.arange(half, dtype=F32) / half)
    ang = positions.astype(F32).reshape(1, -1) * freqs[:, None]
    cos, sin = jnp.cos(ang), jnp.sin(ang)
    z = jnp.zeros((B_ROPE_DIM, ang.shape[1]), F32)
    return jnp.concatenate([sin, sin, z, cos, cos, z], 0)


def kernel(x, c, positions, w_ada, b_ada, w_in, rms_q, w_uq, rms_kv, w_ukv, rel_bias,
           w_branch_a, w_branch_b, w_out, ln1_g, ln1_b, w_router, b_router,
           w_gate, b_gate, w_up, b_up, w_down, b_down, ln2_g, ln2_b):
    bsz, seq, d = x.shape
    n_tok = bsz * seq
    assert w_ada.shape[0] == DEPTH == 1
    row = lambda v: v.reshape(1, -1)

    mod3 = _ada(c, w_ada[0], b_ada[0]).reshape(bsz, 6, d)

    qk, va, cq, ckv, kr, ga, gb = _inproj(x, mod3, _prep_w_in(w_in[0]))
    ya = _attn_a(qk, va, _bias_table(rel_bias[0], min(TQ_A, seq)))
    wk, wv = _prep_w_ukv(w_ukv[0])
    qm, km, vm = _mla_proj(cq, ckv, kr, _rope_tables(positions), row(rms_q[0]), row(rms_kv[0]),
                           _prep_w_uq(w_uq[0]), wk, wv)
    yb = _mla_attn(qm, km, vm)

    wr = jnp.pad(w_router[0], ((0, 0), (0, LANES - N_EXPERTS)))
    br = jnp.pad(b_router[0], (0, LANES - N_EXPERTS), constant_values=NEG_INF)
    x1, h2, logits = _merge(x, ya, yb, ga, gb, mod3,
                            w_branch_a[0].astype(BF16), w_branch_b[0].astype(BF16),
                            w_out[0].astype(BF16), row(ln1_g[0]), row(ln1_b[0]), wr, row(br))

    tm = TM_MOE
    n_tiles = -(-(n_tok * TOP_K) // tm) + N_EXPERTS
    assert n_tiles <= MAX_TILES_PAD
    pos, wts, tmap, meta = _route(logits.reshape(n_tok, LANES), tm)
    pos_flat = pos[:, :TOP_K].reshape(-1)
    fill = meta[:3].reshape(-1)
    nsub = d // LANES
    xs = _dispatch(pos_flat, fill, h2.reshape(n_tok * nsub, LANES), n_tiles * tm, nsub, tm)
    y = _moe(tmap[:n_tiles, 0], meta[0, :1], xs, w_gate[0], b_gate[0], w_up[0], b_up[0],
             w_down[0], b_down[0], tm)
    out = _combine(pos_flat, wts, x1.reshape(n_tok, d), mod3, row(ln2_g[0]), row(ln2_b[0]), y, seq)
    return out.reshape(bsz, seq, d)
```

```python
import functools

import jax
import jax.numpy as jnp
from jax import lax
from jax.experimental import pallas as pl
from jax.experimental.pallas import tpu as pltpu

F32 = jnp.float32
BF16 = jnp.bfloat16

CHUNK = 64
A_HEADS = 8
A_HEAD_DIM = 64
A_LEFT_CHUNKS = 8
A_MAX_REL = 128
A_WIDTH = A_HEADS * A_HEAD_DIM
B_HEADS = 8
B_NOPE_DIM = 64
B_ROPE_DIM = 32
B_V_DIM = 64
B_Q_LORA = 384
B_KV_LORA = 256
B_QK_DIM = B_NOPE_DIM + B_ROPE_DIM
ROPE_THETA = 10000.0
N_EXPERTS = 32
TOP_K = 4
SWIGLU_LIMIT = 7.0
SWIGLU_ALPHA = 1.702
DEPTH = 1
DEEPNORM_ALPHA = (2.0 * DEPTH) ** 0.25
LN_EPS = 1e-5
RMS_EPS = 1e-6
NEG_INF = -1e30
LOG2_E = 1.4426950408889634

LANES = 128
MXU_COLS = 256
V_PAIR_LANES = MXU_COLS

TM_PROJ = 512
TQ_A = 256
A_QBLOCKS = 2
TQ_B = 512
TM_MERGE = 512
TR_ROUTE = 1024
TM_MOE = 512
TM_ROWS = 256
COMBINE_GROUP = 8
MAX_TILES_PAD = 512


def _params(sem, vmem_mb=None):
    return pltpu.CompilerParams(
        dimension_semantics=sem,
        vmem_limit_bytes=None if vmem_mb is None else vmem_mb << 20)


def _dot(a, b):
    return jnp.dot(a, b, preferred_element_type=F32)


def _dot_nt(a, b):
    return lax.dot_general(a, b, (((1,), (1,)), ((), ())), preferred_element_type=F32)


def _store_token_tiles(ref, x):
    rows, d = x.shape
    nsub = d // LANES
    for s in range(nsub):
        ref[pl.ds(s, rows, stride=nsub), :] = x[:, s * LANES:(s + 1) * LANES]


def _load_token_tiles(ref, nsub):
    rows = ref.shape[0] // nsub
    return jnp.concatenate([ref[pl.ds(s, rows, stride=nsub), :] for s in range(nsub)], axis=1)


def _layer_norm(x, g, b):
    mu = jnp.mean(x, axis=-1, keepdims=True)
    xc = x - mu
    var = jnp.mean(xc * xc, axis=-1, keepdims=True)
    return xc * lax.rsqrt(var + LN_EPS) * g + b


def _ada_kernel(c_ref, w_ref, b_ref, o_ref):
    c = c_ref[...]
    sc = c * jax.nn.sigmoid(c)
    o_ref[...] = jnp.dot(sc, w_ref[...], preferred_element_type=F32,
                         precision=lax.Precision.HIGHEST) + b_ref[...]


def _ada(c, w_ada, b_ada):
    bsz, d = c.shape
    m = w_ada.shape[1]
    tn = 1024
    return pl.pallas_call(
        _ada_kernel,
        out_shape=jax.ShapeDtypeStruct((bsz, m), F32),
        grid=(m // tn,),
        in_specs=[pl.BlockSpec((bsz, d), lambda j: (0, 0)),
                  pl.BlockSpec((d, tn), lambda j: (0, j)),
                  pl.BlockSpec((1, tn), lambda j: (0, j))],
        out_specs=pl.BlockSpec((bsz, tn), lambda j: (0, j)),
        compiler_params=_params(("arbitrary",)),
    )(c, w_ada, b_ada.reshape(1, m))


def _inproj_kernel(x_ref, mod_ref, wqk_ref, wv_ref, wcq_ref, wckv_ref, wkr_ref, wga_ref, wgb_ref,
                   qk_ref, va_ref, cq_ref, ckv_ref, kr_ref, ga_ref, gb_ref):
    m = mod_ref[0]
    h = (x_ref[0] * (1.0 + m[1:2]) + m[0:1]).astype(BF16)
    for w_ref, ref in ((wqk_ref, qk_ref), (wcq_ref, cq_ref), (wckv_ref, ckv_ref),
                       (wkr_ref, kr_ref), (wga_ref, ga_ref), (wgb_ref, gb_ref)):
        ref[0] = _dot(h, w_ref[...]).astype(ref.dtype)
    v = _dot(h, wv_ref[...]).astype(BF16)
    ones_blk = jnp.ones((v.shape[0], LANES), BF16)
    parts = []
    for hp in range(A_WIDTH // LANES):
        parts += [v[:, hp * LANES:(hp + 1) * LANES], ones_blk]
    va_ref[0] = jnp.concatenate(parts, axis=1)


def _inproj(x, mod3, weights):
    bsz, seq, d = x.shape
    tm = min(TM_PROJ, seq)
    widths = (2 * A_WIDTH, A_WIDTH // LANES * V_PAIR_LANES, B_Q_LORA, B_KV_LORA, LANES, d, d)
    dtypes = (BF16, BF16, BF16, BF16, F32, BF16, BF16)
    return pl.pallas_call(
        _inproj_kernel,
        out_shape=[jax.ShapeDtypeStruct((bsz, seq, n), dt) for n, dt in zip(widths, dtypes)],
        grid=(bsz, seq // tm),
        in_specs=[pl.BlockSpec((1, tm, d), lambda b, i: (b, i, 0)),
                  pl.BlockSpec((1, 6, d), lambda b, i: (b, 0, 0))]
                 + [pl.BlockSpec(w.shape, lambda b, i: (0, 0)) for w in weights],
        out_specs=[pl.BlockSpec((1, tm, n), lambda b, i: (b, i, 0)) for n in widths],
        compiler_params=_params(("arbitrary", "arbitrary"), 48),
    )(x, mod3, *weights)


def _attn_a_kernel(q_ref, k_ref, v_ref, bias_ref, o_ref, *, tq, nd):
    band = nd * tq
    lane = lax.broadcasted_iota(jnp.int32, (1, LANES), 1)
    for sub in range(A_QBLOCKS):
        i = pl.program_id(2) * A_QBLOCKS + sub
        start = pl.multiple_of(jnp.maximum(i - (nd - 1), 0) * tq, tq)
        variant = jnp.minimum(i, nd - 1)
        q = q_ref[0, sub * tq:(sub + 1) * tq, :]
        k = k_ref[0, pl.ds(start, band), :]
        v = v_ref[0, pl.ds(start, band), :]
        zero = jnp.zeros_like(k)
        kk = jnp.concatenate([jnp.where(lane < A_HEAD_DIM, k, zero),
                              jnp.where(lane >= A_HEAD_DIM, k, zero)], axis=0)
        s = _dot_nt(q, kk)
        ps = []
        for hh in range(2):
            sh = s[:, hh * band:(hh + 1) * band] + bias_ref[hh, variant]
            m = sh.max(axis=-1, keepdims=True)
            ps.append(jnp.exp2(sh - m).astype(BF16))
        acc = _dot(jnp.concatenate(ps, axis=0), v)
        out = acc[:, :LANES] / acc[:, LANES:]
        o_ref[0, sub * tq:(sub + 1) * tq, :] = jnp.where(
            lane < A_HEAD_DIM, out[:tq], out[tq:]).astype(o_ref.dtype)


def _attn_a(qk, va, bias_tab):
    bsz, seq, _ = qk.shape
    tq = min(TQ_A, seq)
    nd = bias_tab.shape[1]
    assert seq >= nd * tq
    ncb = A_WIDTH // LANES
    return pl.pallas_call(
        functools.partial(_attn_a_kernel, tq=tq, nd=nd),
        out_shape=jax.ShapeDtypeStruct((bsz, seq, A_WIDTH), BF16),
        grid=(ncb, bsz, seq // (A_QBLOCKS * tq)),
        in_specs=[pl.BlockSpec((1, A_QBLOCKS * tq, LANES), lambda hp, b, i: (b, i, hp)),
                  pl.BlockSpec((1, seq, LANES), lambda hp, b, i: (b, 0, ncb + hp)),
                  pl.BlockSpec((1, seq, V_PAIR_LANES), lambda hp, b, i: (b, 0, hp)),
                  pl.BlockSpec((2, nd, tq, nd * tq), lambda hp, b, i: (hp, 0, 0, 0))],
        out_specs=pl.BlockSpec((1, A_QBLOCKS * tq, LANES), lambda hp, b, i: (b, i, hp)),
        compiler_params=_params(("arbitrary",) * 3, 40),
    )(qk, qk, va, bias_tab)


def _bias_table(rel_bias, tq):
    nd = (A_LEFT_CHUNKS * CHUNK + tq - 1) // tq + 1
    width = nd * tq
    period = width + tq
    u = jnp.arange(period)
    dist = jnp.where(u < width, (nd - 1) * tq - u, (nd - 1) * tq + period - u)
    vec = rel_bias[:, jnp.clip(dist, -A_MAX_REL, A_MAX_REL) + A_MAX_REL].astype(F32)
    toep = jnp.tile(vec, (1, tq))[:, :tq * (period - 1)].reshape(-1, tq, period - 1)[:, :, :width]
    qi = jnp.arange(tq)[:, None]
    x = jnp.arange(width)[None, :]
    dchunk = qi // CHUNK - x // CHUNK + (nd - 1) * (tq // CHUNK)
    valid = (dchunk >= 0) & (dchunk <= A_LEFT_CHUNKS)
    table = jnp.where(valid[None], toep * LOG2_E, NEG_INF)
    variants = []
    for v in range(nd):
        cut = (nd - 1 - v) * tq
        variants.append(jnp.pad(table[:, :, cut:], ((0, 0), (0, 0), (0, cut)),
                                constant_values=NEG_INF))
    return jnp.stack(variants, axis=1)


def _mla_proj_kernel(cq_ref, ckv_ref, kr_ref, tab_ref, rq_ref, rkv_ref,
                     wq_ref, wk_ref, wv_ref, q_out, k_out, v_out):
    def rms(x, g):
        xf = x.astype(F32)
        ms = jnp.mean(xf * xf, axis=-1, keepdims=True)
        return (xf * lax.rsqrt(ms + RMS_EPS) * g).astype(BF16)

    cqn = rms(cq_ref[0], rq_ref[...])
    ckvn = rms(ckv_ref[0], rkv_ref[...])
    q = _dot(cqn, wq_ref[...])
    kn = _dot(ckvn, wk_ref[...])
    vlane = lax.broadcasted_iota(jnp.int32, (1, v_out.shape[-1]), 1)
    ones_lane = jnp.where(vlane % V_PAIR_LANES >= 2 * B_V_DIM, 1.0, 0.0)
    v_out[0] = (_dot(ckvn, wv_ref[...]) + ones_lane).astype(v_out.dtype)
    tab = tab_ref[...].T
    tlane = lax.broadcasted_iota(jnp.int32, tab.shape, 1)
    rope_end = B_NOPE_DIM + B_ROPE_DIM
    in_rope = (tlane >= B_NOPE_DIM) & (tlane < rope_end)
    cos_t = jnp.where(tlane < B_NOPE_DIM, 1.0, jnp.where(in_rope, tab, 0.0))
    sin_t = jnp.where(in_rope, pltpu.roll(tab, B_NOPE_DIM, 1), 0.0)
    first = tlane < B_NOPE_DIM + B_ROPE_DIM // 2
    sin_a = jnp.where(first, -sin_t, 0.0)
    sin_b = jnp.where(first, 0.0, sin_t)

    def rope(x, tables):
        c, sa, sb = tables
        return (x * c + pltpu.roll(x, LANES - B_ROPE_DIM // 2, 1) * sa
                + pltpu.roll(x, B_ROPE_DIM // 2, 1) * sb)

    kpe = rope(kr_ref[0], (cos_t, sin_a, sin_b))
    scale = B_QK_DIM ** -0.5 * LOG2_E
    q_tables = (cos_t * scale, sin_a * scale, sin_b * scale)
    for h in range(B_HEADS):
        sl = slice(h * LANES, (h + 1) * LANES)
        q_out[0, :, sl] = rope(q[:, sl], q_tables).astype(q_out.dtype)
        k_out[0, :, sl] = (kn[:, sl] + kpe).astype(k_out.dtype)


def _mla_proj(cq, ckv, kr, rope_tab, rms_q, rms_kv, wq, wk, wv):
    bsz, seq, _ = cq.shape
    tm = min(TM_PROJ, seq)
    tok = lambda n: pl.BlockSpec((1, tm, n), lambda b, i: (b, i, 0))
    full = lambda a: pl.BlockSpec(a.shape, lambda b, i: (0,) * a.ndim)
    return pl.pallas_call(
        _mla_proj_kernel,
        out_shape=[jax.ShapeDtypeStruct((bsz, seq, B_HEADS * LANES), BF16),
                   jax.ShapeDtypeStruct((bsz, seq, B_HEADS * LANES), BF16),
                   jax.ShapeDtypeStruct((bsz, seq, wv.shape[1]), BF16)],
        grid=(bsz, seq // tm),
        in_specs=[tok(B_Q_LORA), tok(B_KV_LORA), tok(LANES),
                  pl.BlockSpec((LANES, tm), lambda b, i: (0, b * (seq // tm) + i)),
                  full(rms_q), full(rms_kv), full(wq), full(wk), full(wv)],
        out_specs=[tok(B_HEADS * LANES), tok(B_HEADS * LANES), tok(wv.shape[1])],
        compiler_params=_params(("arbitrary", "arbitrary"), 32),
    )(cq, ckv, kr, rope_tab, rms_q, rms_kv, wq, wk, wv)


def _mla_attn_kernel(q_ref, k_ref, v_ref, o_ref, s_a, s_b, m_s, acc_s, *, t):
    i = pl.program_id(2)
    lane = lax.broadcasted_iota(jnp.int32, (1, LANES), 1)
    m_s[...] = jnp.full(m_s.shape, NEG_INF, F32)
    acc_s[...] = jnp.zeros(acc_s.shape, F32)

    def produce(kj, s_ref):
        start = pl.multiple_of(kj * t, t)
        for hh in range(2):
            q = q_ref[0, :, hh * LANES:(hh + 1) * LANES]
            k = k_ref[0, pl.ds(start, t), hh * LANES:(hh + 1) * LANES]
            s_ref[hh] = _dot_nt(q, k)

    def consume(kj, s_ref, masked):
        start = pl.multiple_of(kj * t, t)
        v = v_ref[0, pl.ds(start, t), :]
        for hh in range(2):
            s = s_ref[hh]
            if masked:
                row_chunk = lax.broadcasted_iota(jnp.int32, (t, t), 0) // CHUNK
                col_chunk = lax.broadcasted_iota(jnp.int32, (t, t), 1) // CHUNK
                s = jnp.where(col_chunk <= row_chunk, s, NEG_INF)
            m_old = m_s[hh]
            m_new = jnp.maximum(m_old, jnp.broadcast_to(s.max(axis=-1, keepdims=True), m_old.shape))
            p = jnp.exp2(s - jnp.tile(m_new, (1, t // LANES))).astype(BF16)
            alpha = jnp.exp2(m_old - m_new)
            acc_s[hh] = jnp.tile(alpha, (1, V_PAIR_LANES // LANES)) * acc_s[hh] + _dot(p, v)
            m_s[hh] = m_new

    def body(pair, c):
        kj = 2 * pair
        produce(kj + 1, s_b)
        consume(kj, s_a, False)
        produce(kj + 2, s_a)
        consume(kj + 1, s_b, False)
        return c

    produce(0, s_a)
    lax.fori_loop(0, i // 2, body, 0)

    @pl.when(i % 2 == 0)
    def _():
        consume(i, s_a, True)

    @pl.when(i % 2 == 1)
    def _():
        produce(i, s_b)
        consume(i - 1, s_a, False)
        consume(i, s_b, True)

    outs = []
    for hh in range(2):
        acc = acc_s[hh]
        outs.append(acc[:, :LANES] / acc[:, LANES:])
    o_ref[0] = jnp.where(lane < B_V_DIM, outs[0], outs[1]).astype(o_ref.dtype)


def _mla_attn(qm, km, vm):
    bsz, seq, _ = qm.shape
    t = min(TQ_B, seq)
    return pl.pallas_call(
        functools.partial(_mla_attn_kernel, t=t),
        out_shape=jax.ShapeDtypeStruct((bsz, seq, B_HEADS * B_V_DIM), BF16),
        grid=(bsz, B_HEADS // 2, seq // t),
        in_specs=[pl.BlockSpec((1, t, 2 * LANES), lambda b, hp, i: (b, i, hp)),
                  pl.BlockSpec((1, seq, 2 * LANES), lambda b, hp, i: (b, 0, hp)),
                  pl.BlockSpec((1, seq, V_PAIR_LANES), lambda b, hp, i: (b, 0, hp))],
        out_specs=pl.BlockSpec((1, t, LANES), lambda b, hp, i: (b, i, hp)),
        scratch_shapes=[pltpu.VMEM((2, t, t), F32), pltpu.VMEM((2, t, t), F32),
                        pltpu.VMEM((2, t, LANES), F32),
                        pltpu.VMEM((2, t, V_PAIR_LANES), F32)],
        compiler_params=_params(("arbitrary",) * 3, 40),
    )(qm, km, vm)


def _merge_kernel(x_ref, ya_ref, yb_ref, ga_ref, gb_ref, mod_ref, wa_ref, wb_ref, wo_ref,
                  g1_ref, b1_ref, wrh_ref, wrl_ref, br_ref, x1_ref, h2_ref, lg_ref):
    m = mod_ref[0]
    a = _dot(ya_ref[0], wa_ref[...])
    b = _dot(yb_ref[0], wb_ref[...])
    merged = (jax.nn.sigmoid(ga_ref[0].astype(F32)) * a
              + jax.nn.sigmoid(gb_ref[0].astype(F32)) * b)
    o = _dot(merged.astype(BF16), wo_ref[...])
    x1 = _layer_norm(DEEPNORM_ALPHA * x_ref[0] + m[2:3] * o, g1_ref[...], b1_ref[...])
    h2 = x1 * (1.0 + m[4:5]) + m[3:4]
    x1_ref[0] = x1
    _store_token_tiles(h2_ref.at[0], h2)
    h_hi = h2.astype(BF16)
    h_lo = (h2 - h_hi.astype(F32)).astype(BF16)
    lg_ref[0] = (_dot(h_hi, wrh_ref[...]) + _dot(h_hi, wrl_ref[...]) + _dot(h_lo, wrh_ref[...])
                 + br_ref[...])


def _merge(x, ya, yb, ga, gb, mod3, wa, wb, wo, g1, b1, wr, br):
    wr_hi = wr.astype(BF16)
    wr_lo = (wr - wr_hi.astype(F32)).astype(BF16)
    bsz, seq, d = x.shape
    tm = min(TM_MERGE, seq)
    tok = lambda n: pl.BlockSpec((1, tm, n), lambda b, i: (b, i, 0))
    full = lambda a: pl.BlockSpec(a.shape, lambda b, i: (0,) * a.ndim)
    return pl.pallas_call(
        _merge_kernel,
        out_shape=[jax.ShapeDtypeStruct((bsz, seq, d), F32),
                   jax.ShapeDtypeStruct((bsz, seq * (d // LANES), LANES), F32),
                   jax.ShapeDtypeStruct((bsz, seq, LANES), F32)],
        grid=(bsz, seq // tm),
        in_specs=[tok(d), tok(A_WIDTH), tok(B_HEADS * B_V_DIM), tok(d), tok(d),
                  pl.BlockSpec((1, 6, d), lambda b, i: (b, 0, 0)),
                  full(wa), full(wb), full(wo), full(g1), full(b1), full(wr_hi), full(wr_lo),
                  full(br)],
        out_specs=[tok(d),
                   pl.BlockSpec((1, tm * (d // LANES), LANES), lambda b, i: (b, i, 0)),
                   tok(LANES)],
        compiler_params=_params(("arbitrary", "arbitrary"), 48),
    )(x, ya, yb, ga, gb, mod3, wa, wb, wo, g1, b1, wr_hi, wr_lo, br)


def _route_kernel(lg_ref, pos_ref, w_ref, tmap_ref, nv_ref, cnt_s, carry_s, pstart_s, *, tr, tm):
    phase = pl.program_id(0)
    i = pl.program_id(1)
    lane = lax.broadcasted_iota(jnp.int32, (tr, LANES), 1)
    work = lg_ref[...]
    sels, vals = [], []
    for _ in range(TOP_K):
        m = work.max(axis=-1, keepdims=True)
        idx = jnp.where(work == m, lane, LANES).min(axis=-1, keepdims=True)
        sel = lane == idx
        work = jnp.where(sel, -jnp.inf, work)
        sels.append(sel)
        vals.append(m)
    picked = sum(jnp.where(s, 1.0, 0.0) for s in sels)
    tile_count = jnp.broadcast_to(picked.sum(axis=0, keepdims=True), (8, LANES))

    @pl.when((phase == 0) & (i == 0))
    def _():
        cnt_s[...] = jnp.zeros_like(cnt_s)

    @pl.when(phase == 0)
    def _():
        cnt_s[...] += tile_count

    @pl.when((phase == 1) & (i == 0))
    def _():
        lane8 = lax.broadcasted_iota(jnp.int32, (8, LANES), 1)
        padded = jnp.floor((cnt_s[...] + (tm - 1)) * (1.0 / tm)) * tm
        cum = padded
        shift = 1
        while shift < N_EXPERTS:
            cum = cum + jnp.where(lane8 >= shift, pltpu.roll(cum, shift, 1), 0.0)
            shift *= 2
        pstart_s[...] = cum - padded
        carry_s[...] = jnp.zeros_like(carry_s)
        nt = tmap_ref.shape[0]
        tile_start = (lax.broadcasted_iota(jnp.int32, (nt, LANES), 0) * tm).astype(F32)
        lane_t = lax.broadcasted_iota(jnp.int32, (nt, LANES), 1)
        done = jnp.where((lane_t < N_EXPERTS) & (cum[0:1] <= tile_start), 1.0, 0.0)
        te = jnp.minimum(done.sum(axis=-1, keepdims=True), N_EXPERTS - 1.0)
        tmap_ref[...] = jnp.broadcast_to(te, (nt, LANES)).astype(jnp.int32)
        total = jnp.where(lane8 == N_EXPERTS - 1, cum, 0.0).sum(axis=-1, keepdims=True)
        row8 = lax.broadcasted_iota(jnp.int32, (8, LANES), 0)
        meta = jnp.where(row8 == 0, jnp.broadcast_to(total * (1.0 / tm), (8, LANES)),
                         jnp.where(row8 == 1, cum - padded + cnt_s[...], cum))
        nv_ref[...] = meta.astype(jnp.int32)

    @pl.when(phase == 1)
    def _():
        r = lax.broadcasted_iota(jnp.int32, (tr, tr), 0)
        c = lax.broadcasted_iota(jnp.int32, (tr, tr), 1)
        tri = jnp.where(c < r, 1.0, 0.0).astype(BF16)
        before = _dot(tri, picked.astype(BF16)) + carry_s[0:1]
        base = before + pstart_s[0:1]
        carry_s[...] += tile_count
        den = sum(jnp.exp(v - vals[0]) for v in vals)
        pos = jnp.zeros((tr, LANES), F32)
        wts = jnp.zeros((tr, LANES), F32)
        for k in range(TOP_K):
            pk = jnp.where(sels[k], base, 0.0).sum(axis=-1, keepdims=True)
            pos = jnp.where(lane == k, pk, pos)
            wts = jnp.where(lane == k, jnp.exp(vals[k] - vals[0]) / den, wts)
        pos_ref[...] = pos.astype(jnp.int32)
        w_ref[...] = wts


def _route(logits, tm):
    n = logits.shape[0]
    tr = min(TR_ROUTE, n)
    return pl.pallas_call(
        functools.partial(_route_kernel, tr=tr, tm=tm),
        out_shape=[jax.ShapeDtypeStruct((n, LANES), jnp.int32),
                   jax.ShapeDtypeStruct((n, LANES), F32),
                   jax.ShapeDtypeStruct((MAX_TILES_PAD, LANES), jnp.int32),
                   jax.ShapeDtypeStruct((8, LANES), jnp.int32)],
        grid=(2, n // tr),
        in_specs=[pl.BlockSpec((tr, LANES), lambda p, i: (i, 0))],
        out_specs=[pl.BlockSpec((tr, LANES), lambda p, i: (i * p, 0)),
                   pl.BlockSpec((tr, LANES), lambda p, i: (i * p, 0)),
                   pl.BlockSpec((MAX_TILES_PAD, LANES), lambda p, i: (0, 0)),
                   pl.BlockSpec((8, LANES), lambda p, i: (0, 0))],
        scratch_shapes=[pltpu.VMEM((8, LANES), F32)] * 3,
        compiler_params=_params(("arbitrary", "arbitrary")),
    )(logits)


def _dispatch_kernel(pos_ref, fill_ref, h_ref, xs_ref, ztile_s, sem, fill_sem, tail_sem,
                     *, tm, nsub, tm_moe):
    def tile(ref, r):
        return ref.at[pl.ds(pl.multiple_of(r * nsub, nsub), nsub)]

    @pl.when(pl.program_id(0) == 0)
    def _():
        ztile_s[...] = jnp.zeros_like(ztile_s)

        def tail_copy(t):
            rows = tm_moe * nsub
            return pltpu.make_async_copy(
                ztile_s, xs_ref.at[pl.ds(pl.multiple_of(t * rows, rows), rows)], tail_sem)

        sizes = [1 << b for b in reversed(range(tm_moe.bit_length() - 1))]

        def fill_pieces(e, act):
            lo, hi = fill_ref[LANES + e], fill_ref[2 * LANES + e]
            at = lo
            for size in sizes:
                has = ((hi - lo) & size) != 0
                copy = pltpu.make_async_copy(
                    ztile_s.at[pl.ds(0, size * nsub)],
                    xs_ref.at[pl.ds(pl.multiple_of(at * nsub, nsub), size * nsub)], fill_sem)
                pl.when(has)(functools.partial(act, copy))
                at = at + jnp.where(has, size, 0)

        def fill_start(e, c):
            fill_pieces(e, lambda copy: copy.start())
            return c

        def fill_wait(e, c):
            fill_pieces(e, lambda copy: copy.wait())
            return c

        n_tiles = xs_ref.shape[0] // (tm_moe * nsub)

        def tail_start(t, c):
            tail_copy(t).start()
            return c

        def tail_drain(_, c):
            tail_copy(0).wait()
            return c

        lax.fori_loop(0, N_EXPERTS, fill_start, 0)
        lax.fori_loop(fill_ref[0], n_tiles, tail_start, 0)
        lax.fori_loop(0, N_EXPERTS, fill_wait, 0)
        lax.fori_loop(fill_ref[0], n_tiles, tail_drain, 0)

    def issue(j, c):
        for r in range(COMBINE_GROUP):
            row = pl.multiple_of(j * COMBINE_GROUP, COMBINE_GROUP) + r
            for k in range(TOP_K):
                pltpu.make_async_copy(tile(h_ref, row), tile(xs_ref, pos_ref[row * TOP_K + k]),
                                      sem).start(priority=k % 2)
        return c

    lax.fori_loop(0, tm // COMBINE_GROUP, issue, 0)
    for _ in range(TOP_K):
        pltpu.make_async_copy(h_ref, xs_ref.at[pl.ds(0, tm * nsub)], sem).wait()


def _dispatch(pos_flat, fill, h2, cap, nsub, tm_moe):
    n = h2.shape[0] // nsub
    tm = min(TM_ROWS, n)
    return pl.pallas_call(
        functools.partial(_dispatch_kernel, tm=tm, nsub=nsub, tm_moe=tm_moe),
        out_shape=jax.ShapeDtypeStruct((cap * nsub, LANES), F32),
        grid=(n // tm,),
        in_specs=[pl.BlockSpec((tm * TOP_K,), lambda i: (i,), memory_space=pltpu.SMEM),
                  pl.BlockSpec(fill.shape, lambda i: (0,), memory_space=pltpu.SMEM),
                  pl.BlockSpec((tm * nsub, LANES), lambda i: (i, 0))],
        out_specs=pl.BlockSpec(memory_space=pl.ANY),
        scratch_shapes=[pltpu.VMEM((tm_moe * nsub, LANES), F32),
                        pltpu.SemaphoreType.DMA(()), pltpu.SemaphoreType.DMA(()),
                        pltpu.SemaphoreType.DMA(())],
        compiler_params=_params(("arbitrary",)),
    )(pos_flat, fill, h2)


def _moe_kernel(te_ref, nv_ref, x_ref, bg_ref, bu_ref, bd_ref, wg_hbm, wu_hbm, wd_hbm,
                y_ref, wf_s, wg_s, wu_s, wd_s, slot_s, sem):
    i = pl.program_id(0)
    nv = nv_ref[0]
    last = te_ref.shape[0] - 1

    def weight_copies(e, slot):
        return [pltpu.make_async_copy(w.at[e], wf_s.at[slot, j], sem.at[slot])
                for j, w in enumerate((wg_hbm, wu_hbm, wd_hbm))]

    @pl.when(i < nv)
    def _():
        e = te_ref[i]

        @pl.when(i == 0)
        def _():
            slot_s[0] = 0
            for c in weight_copies(e, 0):
                c.start()

        @pl.when((i == 0) | (e != te_ref[jnp.maximum(i - 1, 0)]))
        def _():
            slot = slot_s[0]
            for c in weight_copies(e, slot):
                c.wait()
            nxt = lax.while_loop(lambda j: (j < nv) & (te_ref[jnp.minimum(j, last)] == e),
                                 lambda j: j + 1, i + 1)

            @pl.when(nxt < nv)
            def _():
                for c in weight_copies(te_ref[jnp.minimum(nxt, last)], 1 - slot):
                    c.start()

            wg_s[...] = wf_s[slot, 0].astype(BF16)
            wu_s[...] = wf_s[slot, 1].astype(BF16)
            wd_s[...] = wf_s[slot, 2].astype(BF16)
            slot_s[0] = 1 - slot

        xb = _load_token_tiles(x_ref, wg_s.shape[0] // LANES).astype(BF16)
        g = _dot(xb, wg_s[...]) + bg_ref[0]
        u = _dot(xb, wu_s[...]) + bu_ref[0]
        g = jnp.minimum(g, SWIGLU_LIMIT)
        u = jnp.clip(u, -SWIGLU_LIMIT, SWIGLU_LIMIT)
        h = (u + 1.0) * (g * jax.nn.sigmoid(SWIGLU_ALPHA * g))
        _store_token_tiles(y_ref, _dot(h.astype(BF16), wd_s[...]) + bd_ref[0])

    @pl.when(i >= nv)
    def _():
        y_ref[...] = jnp.zeros_like(y_ref)


def _moe(tile_expert, n_valid, xs, w_gate, b_gate, w_up, b_up, w_down, b_down, tm):
    e, d, f = w_gate.shape
    assert d == f
    nsub = d // LANES
    cap = xs.shape[0] // nsub
    n_tiles = cap // tm

    def tile(i, te, nv):
        return jnp.minimum(i, nv[0] - 1)

    exp3 = lambda i, te, nv: (te[tile(i, te, nv)], 0, 0)
    grid_spec = pltpu.PrefetchScalarGridSpec(
        num_scalar_prefetch=2,
        grid=(n_tiles,),
        in_specs=[pl.BlockSpec((tm * nsub, LANES), lambda i, te, nv: (tile(i, te, nv), 0)),
                  pl.BlockSpec((1, 1, f), exp3), pl.BlockSpec((1, 1, f), exp3),
                  pl.BlockSpec((1, 1, d), exp3),
                  pl.BlockSpec(memory_space=pl.ANY), pl.BlockSpec(memory_space=pl.ANY),
                  pl.BlockSpec(memory_space=pl.ANY)],
        out_specs=pl.BlockSpec((tm * nsub, LANES), lambda i, te, nv: (i, 0)),
        scratch_shapes=[pltpu.VMEM((2, 3, d, f), F32),
                        pltpu.VMEM((d, f), BF16), pltpu.VMEM((d, f), BF16), pltpu.VMEM((f, d), BF16),
                        pltpu.SMEM((1,), jnp.int32), pltpu.SemaphoreType.DMA((2,))],
    )
    return pl.pallas_call(
        _moe_kernel,
        out_shape=jax.ShapeDtypeStruct((cap * nsub, LANES), F32),
        grid_spec=grid_spec,
        compiler_params=_params(("arbitrary",), 56),
    )(tile_expert, n_valid, xs, b_gate.reshape(e, 1, f), b_up.reshape(e, 1, f),
      b_down.reshape(e, 1, d), w_gate, w_up, w_down)


def _combine_kernel(pos_ref, posn_ref, w_ref, x1_ref, mod_ref, g_ref, b_ref, y_ref, o_ref,
                    buf, sem, *, tm):
    i = pl.program_id(0)
    n = pl.num_programs(0)
    nsub = x1_ref.shape[1] // LANES

    group = COMBINE_GROUP

    def issue_rows(p_ref, slot, r0):
        for r in range(group):
            for k in range(TOP_K):
                row = r0 + r
                src = pl.ds(pl.multiple_of(p_ref[row * TOP_K + k] * nsub, nsub), nsub)
                dst = pl.ds(pl.multiple_of(row * nsub, nsub), nsub)
                pltpu.make_async_copy(y_ref.at[src], buf.at[slot, k, dst],
                                      sem.at[slot]).start(priority=k % 2)

    def reduce_rows(slot, r0):
        rows = pl.ds(r0, group)
        w = w_ref[rows, :]
        f = jnp.zeros((group, nsub * LANES), F32)
        for k in range(TOP_K):
            yk = jnp.concatenate(
                [buf[slot, k, pl.ds(r0 * nsub + s, group, stride=nsub), :] for s in range(nsub)],
                axis=1)
            f = f + w[:, k:k + 1] * yk
        o_ref[rows, :] = f

    @pl.when(i == 0)
    def _():
        def body(j, c):
            issue_rows(pos_ref, 0, pl.multiple_of(j * group, group))
            return c
        lax.fori_loop(0, tm // group, body, 0)

    slot = i % 2
    for k in range(TOP_K):
        pltpu.make_async_copy(y_ref.at[pl.ds(0, tm * nsub)], buf.at[slot, k], sem.at[slot]).wait()

    @pl.when(i + 1 < n)
    def _():
        def body(j, c):
            r0 = pl.multiple_of(j * group, group)
            issue_rows(posn_ref, 1 - slot, r0)
            reduce_rows(slot, r0)
            return c
        lax.fori_loop(0, tm // group, body, 0)

    @pl.when(i + 1 == n)
    def _():
        def body(j, c):
            reduce_rows(slot, pl.multiple_of(j * group, group))
            return c
        lax.fori_loop(0, tm // group, body, 0)

    o_ref[...] = _layer_norm(DEEPNORM_ALPHA * x1_ref[...] + mod_ref[0, 5:6, :] * o_ref[...],
                             g_ref[...], b_ref[...])


def _combine(pos_flat, w, x1, mod3, g2, b2, y, seq):
    n, d = x1.shape
    tm = min(TM_ROWS, seq)
    nsteps = n // tm
    per_seq = seq // tm
    return pl.pallas_call(
        functools.partial(_combine_kernel, tm=tm),
        out_shape=jax.ShapeDtypeStruct((n, d), F32),
        grid=(nsteps,),
        in_specs=[pl.BlockSpec((tm * TOP_K,), lambda i: (i,), memory_space=pltpu.SMEM),
                  pl.BlockSpec((tm * TOP_K,), lambda i: (jnp.minimum(i + 1, nsteps - 1),),
                               memory_space=pltpu.SMEM),
                  pl.BlockSpec((tm, LANES), lambda i: (i, 0)),
                  pl.BlockSpec((tm, d), lambda i: (i, 0)),
                  pl.BlockSpec((1, 6, d), lambda i: (i // per_seq, 0, 0)),
                  pl.BlockSpec(g2.shape, lambda i: (0, 0)),
                  pl.BlockSpec(b2.shape, lambda i: (0, 0)),
                  pl.BlockSpec(memory_space=pl.ANY)],
        out_specs=pl.BlockSpec((tm, d), lambda i: (i, 0)),
        scratch_shapes=[pltpu.VMEM((2, TOP_K, tm * (d // LANES), LANES), F32),
                        pltpu.SemaphoreType.DMA((2,))],
        compiler_params=_params(("arbitrary",), 32),
    )(pos_flat, pos_flat, w, x1, mod3, g2, b2, y)


def _prep_w_in(w_in):
    d = w_in.shape[1] - (3 * A_WIDTH + B_Q_LORA + B_KV_LORA + B_ROPE_DIM)
    d //= 2
    cuts, off = [], 0
    for n in (A_WIDTH, A_WIDTH, A_WIDTH, B_Q_LORA, B_KV_LORA, B_ROPE_DIM, d, d):
        cuts.append(w_in[:, off:off + n])
        off += n
    q, k, v, cq, ckv, kr, ga, gb = cuts
    qk = jnp.concatenate([q * (A_HEAD_DIM ** -0.5 * LOG2_E), k], axis=1)
    kr = jnp.pad(kr, ((0, 0), (B_NOPE_DIM, LANES - B_NOPE_DIM - B_ROPE_DIM)))
    return tuple(w.astype(BF16) for w in (qk, v, cq, ckv, kr, ga, gb))


def _prep_w_uq(w_uq):
    r = w_uq.shape[0]
    w = w_uq.reshape(r, B_HEADS, B_QK_DIM)
    w = jnp.pad(w, ((0, 0), (0, 0), (0, LANES - B_QK_DIM)))
    return w.reshape(r, B_HEADS * LANES).astype(BF16)


def _prep_w_ukv(w_ukv):
    r = w_ukv.shape[0]
    w = w_ukv.reshape(r, B_HEADS, B_NOPE_DIM + B_V_DIM)
    wk = jnp.pad(w[:, :, :B_NOPE_DIM], ((0, 0), (0, 0), (0, LANES - B_NOPE_DIM)))
    wv = w[:, :, B_NOPE_DIM:].reshape(r, B_HEADS // 2, 2 * B_V_DIM)
    wv = jnp.pad(wv, ((0, 0), (0, 0), (0, V_PAIR_LANES - 2 * B_V_DIM)))
    return (wk.reshape(r, B_HEADS * LANES).astype(BF16),
            wv.reshape(r, B_HEADS // 2 * V_PAIR_LANES).astype(BF16))


def _rope_tables(positions):
    half = B_ROPE_DIM // 2
    freqs = ROPE_THETA ** (-jnp.arange(half, dtype=F32) / half)
    ang = positions.astype(F32).reshape(1, -1) * freqs[:, None]
    cos, sin = jnp.cos(ang), jnp.sin(ang)
    z = jnp.zeros((B_ROPE_DIM, ang.shape[1]), F32)
    return jnp.concatenate([sin, sin, z, cos, cos, z], 0)


def kernel(x, c, positions, w_ada, b_ada, w_in, rms_q, w_uq, rms_kv, w_ukv, rel_bias,
           w_branch_a, w_branch_b, w_out, ln1_g, ln1_b, w_router, b_router,
           w_gate, b_gate, w_up, b_up, w_down, b_down, ln2_g, ln2_b):
    bsz, seq, d = x.shape
    n_tok = bsz * seq
    assert w_ada.shape[0] == DEPTH == 1
    row = lambda v: v.reshape(1, -1)

    mod3 = _ada(c, w_ada[0], b_ada[0]).reshape(bsz, 6, d)

    qk, va, cq, ckv, kr, ga, gb = _inproj(x, mod3, _prep_w_in(w_in[0]))
    ya = _attn_a(qk, va, _bias_table(rel_bias[0], min(TQ_A, seq)))
    wk, wv = _prep_w_ukv(w_ukv[0])
    qm, km, vm = _mla_proj(cq, ckv, kr, _rope_tables(positions), row(rms_q[0]), row(rms_kv[0]),
                           _prep_w_uq(w_uq[0]), wk, wv)
    yb = _mla_attn(qm, km, vm)

    wr = jnp.pad(w_router[0], ((0, 0), (0, LANES - N_EXPERTS)))
    br = jnp.pad(b_router[0], (0, LANES - N_EXPERTS), constant_values=NEG_INF)
    x1, h2, logits = _merge(x, ya, yb, ga, gb, mod3,
                            w_branch_a[0].astype(BF16), w_branch_b[0].astype(BF16),
                            w_out[0].astype(BF16), row(ln1_g[0]), row(ln1_b[0]), wr, row(br))

    tm = TM_MOE
    n_tiles = -(-(n_tok * TOP_K) // tm) + N_EXPERTS
    assert n_tiles <= MAX_TILES_PAD
    pos, wts, tmap, meta = _route(logits.reshape(n_tok, LANES), tm)
    pos_flat = pos[:, :TOP_K].reshape(-1)
    fill = meta[:3].reshape(-1)
    nsub = d // LANES
    xs = _dispatch(pos_flat, fill, h2.reshape(n_tok * nsub, LANES), n_tiles * tm, nsub, tm)
    y = _moe(tmap[:n_tiles, 0], meta[0, :1], xs, w_gate[0], b_gate[0], w_up[0], b_up[0],
             w_down[0], b_down[0], tm)
    out = _combine(pos_flat, wts, x1.reshape(n_tok, d), mod3, row(ln2_g[0]), row(ln2_b[0]), y, seq)
    return out.reshape(bsz, seq, d)
```

```python
import functools

import jax
import jax.numpy as jnp
from jax import lax
from jax.experimental import pallas as pl
from jax.experimental.pallas import tpu as pltpu

F32 = jnp.float32
BF16 = jnp.bfloat16

CHUNK = 64
A_HEADS = 8
A_HEAD_DIM = 64
A_LEFT_CHUNKS = 8
A_MAX_REL = 128
A_WIDTH = A_HEADS * A_HEAD_DIM
B_HEADS = 8
B_NOPE_DIM = 64
B_ROPE_DIM = 32
B_V_DIM = 64
B_Q_LORA = 384
B_KV_LORA = 256
B_QK_DIM = B_NOPE_DIM + B_ROPE_DIM
ROPE_THETA = 10000.0
N_EXPERTS = 32
TOP_K = 4
SWIGLU_LIMIT = 7.0
SWIGLU_ALPHA = 1.702
DEPTH = 1
DEEPNORM_ALPHA = (2.0 * DEPTH) ** 0.25
LN_EPS = 1e-5
RMS_EPS = 1e-6
NEG_INF = -1e30
LOG2_E = 1.4426950408889634

LANES = 128
MXU_COLS = 256
V_PAIR_LANES = MXU_COLS

TM_PROJ = 512
TQ_A = 256
A_QBLOCKS = 2
TQ_B = 512
TM_MERGE = 512
MERGE_SPLIT = 1
TR_ROUTE = 1024
TM_MOE = 512
TM_ROWS = 256
COMBINE_GROUP = 8
MAX_TILES_PAD = 512


def _params(sem, vmem_mb=None):
    return pltpu.CompilerParams(
        dimension_semantics=sem,
        vmem_limit_bytes=None if vmem_mb is None else vmem_mb << 20)


def _dot(a, b):
    return jnp.dot(a, b, preferred_element_type=F32)


def _dot_nt(a, b):
    return lax.dot_general(a, b, (((1,), (1,)), ((), ())), preferred_element_type=F32)


def _store_token_tiles(ref, x):
    rows, d = x.shape
    nsub = d // LANES
    for s in range(nsub):
        ref[pl.ds(s, rows, stride=nsub), :] = x[:, s * LANES:(s + 1) * LANES]


def _load_token_tiles(ref, nsub):
    rows = ref.shape[0] // nsub
    return jnp.concatenate([ref[pl.ds(s, rows, stride=nsub), :] for s in range(nsub)], axis=1)


def _layer_norm(x, g, b):
    mu = jnp.mean(x, axis=-1, keepdims=True)
    xc = x - mu
    var = jnp.mean(xc * xc, axis=-1, keepdims=True)
    return xc * lax.rsqrt(var + LN_EPS) * g + b


def _ada_kernel(c_ref, w_ref, b_ref, o_ref):
    c = c_ref[...]
    sc = c * jax.nn.sigmoid(c)
    o_ref[...] = jnp.dot(sc, w_ref[...], preferred_element_type=F32,
                         precision=lax.Precision.HIGHEST) + b_ref[...]


def _ada(c, w_ada, b_ada):
    bsz, d = c.shape
    m = w_ada.shape[1]
    tn = 1024
    return pl.pallas_call(
        _ada_kernel,
        out_shape=jax.ShapeDtypeStruct((bsz, m), F32),
        grid=(m // tn,),
        in_specs=[pl.BlockSpec((bsz, d), lambda j: (0, 0)),
                  pl.BlockSpec((d, tn), lambda j: (0, j)),
                  pl.BlockSpec((1, tn), lambda j: (0, j))],
        out_specs=pl.BlockSpec((bsz, tn), lambda j: (0, j)),
        compiler_params=_params(("arbitrary",)),
    )(c, w_ada, b_ada.reshape(1, m))


def _inproj_kernel(x_ref, mod_ref, wqk_ref, wv_ref, wcq_ref, wckv_ref, wkr_ref, wga_ref, wgb_ref,
                   qk_ref, va_ref, cq_ref, ckv_ref, kr_ref, ga_ref, gb_ref):
    m = mod_ref[0]
    h = (x_ref[0] * (1.0 + m[1:2]) + m[0:1]).astype(BF16)
    for w_ref, ref in ((wqk_ref, qk_ref), (wcq_ref, cq_ref), (wckv_ref, ckv_ref),
                       (wkr_ref, kr_ref), (wga_ref, ga_ref), (wgb_ref, gb_ref)):
        ref[0] = _dot(h, w_ref[...]).astype(ref.dtype)
    v = _dot(h, wv_ref[...]).astype(BF16)
    ones_blk = jnp.ones((v.shape[0], LANES), BF16)
    parts = []
    for hp in range(A_WIDTH // LANES):
        parts += [v[:, hp * LANES:(hp + 1) * LANES], ones_blk]
    va_ref[0] = jnp.concatenate(parts, axis=1)


def _inproj(x, mod3, weights):
    bsz, seq, d = x.shape
    tm = min(TM_PROJ, seq)
    widths = (2 * A_WIDTH, A_WIDTH // LANES * V_PAIR_LANES, B_Q_LORA, B_KV_LORA, LANES, d, d)
    dtypes = (BF16, BF16, BF16, BF16, F32, BF16, BF16)
    return pl.pallas_call(
        _inproj_kernel,
        out_shape=[jax.ShapeDtypeStruct((bsz, seq, n), dt) for n, dt in zip(widths, dtypes)],
        grid=(bsz, seq // tm),
        in_specs=[pl.BlockSpec((1, tm, d), lambda b, i: (b, i, 0)),
                  pl.BlockSpec((1, 6, d), lambda b, i: (b, 0, 0))]
                 + [pl.BlockSpec(w.shape, lambda b, i: (0, 0)) for w in weights],
        out_specs=[pl.BlockSpec((1, tm, n), lambda b, i: (b, i, 0)) for n in widths],
        compiler_params=_params(("arbitrary", "arbitrary"), 48),
    )(x, mod3, *weights)


def _attn_a_kernel(q_ref, k_ref, v_ref, bias_ref, o_ref, *, tq, nd):
    band = nd * tq
    lane = lax.broadcasted_iota(jnp.int32, (1, LANES), 1)
    for sub in range(A_QBLOCKS):
        i = pl.program_id(2) * A_QBLOCKS + sub
        start = pl.multiple_of(jnp.maximum(i - (nd - 1), 0) * tq, tq)
        variant = jnp.minimum(i, nd - 1)
        q = q_ref[0, sub * tq:(sub + 1) * tq, :]
        k = k_ref[0, pl.ds(start, band), :]
        v = v_ref[0, pl.ds(start, band), :]
        zero = jnp.zeros_like(k)
        kk = jnp.concatenate([jnp.where(lane < A_HEAD_DIM, k, zero),
                              jnp.where(lane >= A_HEAD_DIM, k, zero)], axis=0)
        s = _dot_nt(q, kk)
        ps = []
        for hh in range(2):
            sh = s[:, hh * band:(hh + 1) * band] + bias_ref[hh, variant]
            m = sh.max(axis=-1, keepdims=True)
            ps.append(jnp.exp2(sh - m).astype(BF16))
        acc = _dot(jnp.concatenate(ps, axis=0), v)
        out = acc[:, :LANES] / acc[:, LANES:]
        o_ref[0, sub * tq:(sub + 1) * tq, :] = jnp.where(
            lane < A_HEAD_DIM, out[:tq], out[tq:]).astype(o_ref.dtype)


def _attn_a(qk, va, bias_tab):
    bsz, seq, _ = qk.shape
    tq = min(TQ_A, seq)
    nd = bias_tab.shape[1]
    assert seq >= nd * tq
    ncb = A_WIDTH // LANES
    return pl.pallas_call(
        functools.partial(_attn_a_kernel, tq=tq, nd=nd),
        out_shape=jax.ShapeDtypeStruct((bsz, seq, A_WIDTH), BF16),
        grid=(ncb, bsz, seq // (A_QBLOCKS * tq)),
        in_specs=[pl.BlockSpec((1, A_QBLOCKS * tq, LANES), lambda hp, b, i: (b, i, hp)),
                  pl.BlockSpec((1, seq, LANES), lambda hp, b, i: (b, 0, ncb + hp)),
                  pl.BlockSpec((1, seq, V_PAIR_LANES), lambda hp, b, i: (b, 0, hp)),
                  pl.BlockSpec((2, nd, tq, nd * tq), lambda hp, b, i: (hp, 0, 0, 0))],
        out_specs=pl.BlockSpec((1, A_QBLOCKS * tq, LANES), lambda hp, b, i: (b, i, hp)),
        compiler_params=_params(("arbitrary",) * 3, 40),
    )(qk, qk, va, bias_tab)


def _bias_table(rel_bias, tq):
    nd = (A_LEFT_CHUNKS * CHUNK + tq - 1) // tq + 1
    width = nd * tq
    period = width + tq
    u = jnp.arange(period)
    dist = jnp.where(u < width, (nd - 1) * tq - u, (nd - 1) * tq + period - u)
    vec = rel_bias[:, jnp.clip(dist, -A_MAX_REL, A_MAX_REL) + A_MAX_REL].astype(F32)
    toep = jnp.tile(vec, (1, tq))[:, :tq * (period - 1)].reshape(-1, tq, period - 1)[:, :, :width]
    qi = jnp.arange(tq)[:, None]
    x = jnp.arange(width)[None, :]
    dchunk = qi // CHUNK - x // CHUNK + (nd - 1) * (tq // CHUNK)
    valid = (dchunk >= 0) & (dchunk <= A_LEFT_CHUNKS)
    table = jnp.where(valid[None], toep * LOG2_E, NEG_INF)
    variants = []
    for v in range(nd):
        cut = (nd - 1 - v) * tq
        variants.append(jnp.pad(table[:, :, cut:], ((0, 0), (0, 0), (0, cut)),
                                constant_values=NEG_INF))
    return jnp.stack(variants, axis=1)


def _mla_proj_kernel(cq_ref, ckv_ref, kr_ref, tab_ref, rq_ref, rkv_ref,
                     wq_ref, wk_ref, wv_ref, q_out, k_out, v_out):
    def rms(x, g):
        xf = x.astype(F32)
        ms = jnp.mean(xf * xf, axis=-1, keepdims=True)
        return (xf * lax.rsqrt(ms + RMS_EPS) * g).astype(BF16)

    cqn = rms(cq_ref[0], rq_ref[...])
    ckvn = rms(ckv_ref[0], rkv_ref[...])
    q = _dot(cqn, wq_ref[...])
    kn = _dot(ckvn, wk_ref[...])
    vlane = lax.broadcasted_iota(jnp.int32, (1, v_out.shape[-1]), 1)
    ones_lane = jnp.where(vlane % V_PAIR_LANES >= 2 * B_V_DIM, 1.0, 0.0)
    v_out[0] = (_dot(ckvn, wv_ref[...]) + ones_lane).astype(v_out.dtype)
    tab = tab_ref[...].T
    tlane = lax.broadcasted_iota(jnp.int32, tab.shape, 1)
    rope_end = B_NOPE_DIM + B_ROPE_DIM
    in_rope = (tlane >= B_NOPE_DIM) & (tlane < rope_end)
    cos_t = jnp.where(tlane < B_NOPE_DIM, 1.0, jnp.where(in_rope, tab, 0.0))
    sin_t = jnp.where(in_rope, pltpu.roll(tab, B_NOPE_DIM, 1), 0.0)
    first = tlane < B_NOPE_DIM + B_ROPE_DIM // 2
    sin_a = jnp.where(first, -sin_t, 0.0)
    sin_b = jnp.where(first, 0.0, sin_t)

    def rope(x, tables):
        c, sa, sb = tables
        return (x * c + pltpu.roll(x, LANES - B_ROPE_DIM // 2, 1) * sa
                + pltpu.roll(x, B_ROPE_DIM // 2, 1) * sb)

    kpe = rope(kr_ref[0], (cos_t, sin_a, sin_b))
    scale = B_QK_DIM ** -0.5 * LOG2_E
    q_tables = (cos_t * scale, sin_a * scale, sin_b * scale)
    for h in range(B_HEADS):
        sl = slice(h * LANES, (h + 1) * LANES)
        q_out[0, :, sl] = rope(q[:, sl], q_tables).astype(q_out.dtype)
        k_out[0, :, sl] = (kn[:, sl] + kpe).astype(k_out.dtype)


def _mla_proj(cq, ckv, kr, rope_tab, rms_q, rms_kv, wq, wk, wv):
    bsz, seq, _ = cq.shape
    tm = min(TM_PROJ, seq)
    tok = lambda n: pl.BlockSpec((1, tm, n), lambda b, i: (b, i, 0))
    full = lambda a: pl.BlockSpec(a.shape, lambda b, i: (0,) * a.ndim)
    return pl.pallas_call(
        _mla_proj_kernel,
        out_shape=[jax.ShapeDtypeStruct((bsz, seq, B_HEADS * LANES), BF16),
                   jax.ShapeDtypeStruct((bsz, seq, B_HEADS * LANES), BF16),
                   jax.ShapeDtypeStruct((bsz, seq, wv.shape[1]), BF16)],
        grid=(bsz, seq // tm),
        in_specs=[tok(B_Q_LORA), tok(B_KV_LORA), tok(LANES),
                  pl.BlockSpec((LANES, tm), lambda b, i: (0, b * (seq // tm) + i)),
                  full(rms_q), full(rms_kv), full(wq), full(wk), full(wv)],
        out_specs=[tok(B_HEADS * LANES), tok(B_HEADS * LANES), tok(wv.shape[1])],
        compiler_params=_params(("arbitrary", "arbitrary"), 32),
    )(cq, ckv, kr, rope_tab, rms_q, rms_kv, wq, wk, wv)


def _mla_attn_kernel(q_ref, k_ref, v_ref, o_ref, s_a, s_b, m_s, acc_s, *, t):
    i = pl.program_id(2)
    lane = lax.broadcasted_iota(jnp.int32, (1, LANES), 1)
    m_s[...] = jnp.full(m_s.shape, NEG_INF, F32)
    acc_s[...] = jnp.zeros(acc_s.shape, F32)

    def produce(kj, s_ref):
        start = pl.multiple_of(kj * t, t)
        for hh in range(2):
            q = q_ref[0, :, hh * LANES:(hh + 1) * LANES]
            k = k_ref[0, pl.ds(start, t), hh * LANES:(hh + 1) * LANES]
            s_ref[hh] = _dot_nt(q, k)

    def consume(kj, s_ref, masked):
        start = pl.multiple_of(kj * t, t)
        v = v_ref[0, pl.ds(start, t), :]
        for hh in range(2):
            s = s_ref[hh]
            if masked:
                row_chunk = lax.broadcasted_iota(jnp.int32, (t, t), 0) // CHUNK
                col_chunk = lax.broadcasted_iota(jnp.int32, (t, t), 1) // CHUNK
                s = jnp.where(col_chunk <= row_chunk, s, NEG_INF)
            m_old = m_s[hh]
            m_new = jnp.maximum(m_old, jnp.broadcast_to(s.max(axis=-1, keepdims=True), m_old.shape))
            p = jnp.exp2(s - jnp.tile(m_new, (1, t // LANES))).astype(BF16)
            alpha = jnp.exp2(m_old - m_new)
            acc_s[hh] = jnp.tile(alpha, (1, V_PAIR_LANES // LANES)) * acc_s[hh] + _dot(p, v)
            m_s[hh] = m_new

    def body(pair, c):
        kj = 2 * pair
        produce(kj + 1, s_b)
        consume(kj, s_a, False)
        produce(kj + 2, s_a)
        consume(kj + 1, s_b, False)
        return c

    produce(0, s_a)
    lax.fori_loop(0, i // 2, body, 0)

    @pl.when(i % 2 == 0)
    def _():
        consume(i, s_a, True)

    @pl.when(i % 2 == 1)
    def _():
        produce(i, s_b)
        consume(i - 1, s_a, False)
        consume(i, s_b, True)

    outs = []
    for hh in range(2):
        acc = acc_s[hh]
        outs.append(acc[:, :LANES] / acc[:, LANES:])
    o_ref[0] = jnp.where(lane < B_V_DIM, outs[0], outs[1]).astype(o_ref.dtype)


def _mla_attn(qm, km, vm):
    bsz, seq, _ = qm.shape
    t = min(TQ_B, seq)
    return pl.pallas_call(
        functools.partial(_mla_attn_kernel, t=t),
        out_shape=jax.ShapeDtypeStruct((bsz, seq, B_HEADS * B_V_DIM), BF16),
        grid=(bsz, B_HEADS // 2, seq // t),
        in_specs=[pl.BlockSpec((1, t, 2 * LANES), lambda b, hp, i: (b, i, hp)),
                  pl.BlockSpec((1, seq, 2 * LANES), lambda b, hp, i: (b, 0, hp)),
                  pl.BlockSpec((1, seq, V_PAIR_LANES), lambda b, hp, i: (b, 0, hp))],
        out_specs=pl.BlockSpec((1, t, LANES), lambda b, hp, i: (b, i, hp)),
        scratch_shapes=[pltpu.VMEM((2, t, t), F32), pltpu.VMEM((2, t, t), F32),
                        pltpu.VMEM((2, t, LANES), F32),
                        pltpu.VMEM((2, t, V_PAIR_LANES), F32)],
        compiler_params=_params(("arbitrary",) * 3, 40),
    )(qm, km, vm)


def _merge_kernel(x_ref, ya_ref, yb_ref, ga_ref, gb_ref, mod_ref, wa_ref, wb_ref, wo_ref,
                  g1_ref, b1_ref, wrh_ref, wrl_ref, br_ref, x1_ref, h2_ref, code_ref, wts_ref,
                  cnt_ref):
    @pl.when((pl.program_id(0) == 0) & (pl.program_id(1) == 0))
    def _():
        cnt_ref[...] = jnp.zeros_like(cnt_ref)

    tm = x_ref.shape[1]
    half = tm // MERGE_SPLIT
    for part in range(MERGE_SPLIT):
        _merge_rows(slice(part * half, (part + 1) * half), x_ref, ya_ref, yb_ref, ga_ref, gb_ref,
                    mod_ref, wa_ref, wb_ref, wo_ref, g1_ref, b1_ref, wrh_ref, wrl_ref, br_ref,
                    x1_ref, h2_ref, code_ref, wts_ref, cnt_ref)


def _merge_rows(rows, x_ref, ya_ref, yb_ref, ga_ref, gb_ref, mod_ref, wa_ref, wb_ref, wo_ref,
                g1_ref, b1_ref, wrh_ref, wrl_ref, br_ref, x1_ref, h2_ref, code_ref, wts_ref,
                cnt_ref):
    m = mod_ref[0]
    a = _dot(ya_ref[0, rows, :], wa_ref[...])
    b = _dot(yb_ref[0, rows, :], wb_ref[...])
    merged = (jax.nn.sigmoid(ga_ref[0, rows, :].astype(F32)) * a
              + jax.nn.sigmoid(gb_ref[0, rows, :].astype(F32)) * b)
    o = _dot(merged.astype(BF16), wo_ref[...])
    x1 = _layer_norm(DEEPNORM_ALPHA * x_ref[0, rows, :] + m[2:3] * o, g1_ref[...], b1_ref[...])
    h2 = x1 * (1.0 + m[4:5]) + m[3:4]
    x1_ref[0, rows, :] = x1
    nsub = h2.shape[1] // LANES
    _store_token_tiles(h2_ref.at[0, pl.ds(rows.start * nsub, (rows.stop - rows.start) * nsub)], h2)
    h_hi = h2.astype(BF16)
    h_lo = (h2 - h_hi.astype(F32)).astype(BF16)
    work = (_dot(h_hi, wrh_ref[...]) + _dot(h_hi, wrl_ref[...]) + _dot(h_lo, wrh_ref[...])
            + br_ref[...])
    lane = lax.broadcasted_iota(jnp.int32, work.shape, 1)
    code = jnp.zeros(work.shape, F32)
    vals = []
    for k in range(TOP_K):
        top = work.max(axis=-1, keepdims=True)
        idx = jnp.where(work == top, lane, LANES).min(axis=-1, keepdims=True)
        sel = lane == idx
        work = jnp.where(sel, -jnp.inf, work)
        code = jnp.where(sel, k + 1.0, code)
        vals.append(top)
    den = sum(jnp.exp(v - vals[0]) for v in vals)
    wts = jnp.zeros(work.shape, F32)
    for k in range(TOP_K):
        wts = jnp.where(lane == k, jnp.exp(vals[k] - vals[0]) / den, wts)
    code_ref[0, rows, :] = code
    wts_ref[0, rows, :] = wts
    picked = jnp.where(code > 0.0, 1.0, 0.0)
    cnt_ref[...] += jnp.broadcast_to(picked.sum(axis=0, keepdims=True), cnt_ref.shape)


def _merge(x, ya, yb, ga, gb, mod3, wa, wb, wo, g1, b1, wr, br):
    wr_hi = wr.astype(BF16)
    wr_lo = (wr - wr_hi.astype(F32)).astype(BF16)
    bsz, seq, d = x.shape
    tm = min(TM_MERGE, seq)
    tok = lambda n: pl.BlockSpec((1, tm, n), lambda b, i: (b, i, 0))
    full = lambda a: pl.BlockSpec(a.shape, lambda b, i: (0,) * a.ndim)
    return pl.pallas_call(
        _merge_kernel,
        out_shape=[jax.ShapeDtypeStruct((bsz, seq, d), F32),
                   jax.ShapeDtypeStruct((bsz, seq * (d // LANES), LANES), F32),
                   jax.ShapeDtypeStruct((bsz, seq, LANES), F32),
                   jax.ShapeDtypeStruct((bsz, seq, LANES), F32),
                   jax.ShapeDtypeStruct((8, LANES), F32)],
        grid=(bsz, seq // tm),
        in_specs=[tok(d), tok(A_WIDTH), tok(B_HEADS * B_V_DIM), tok(d), tok(d),
                  pl.BlockSpec((1, 6, d), lambda b, i: (b, 0, 0)),
                  full(wa), full(wb), full(wo), full(g1), full(b1), full(wr_hi), full(wr_lo),
                  full(br)],
        out_specs=[tok(d),
                   pl.BlockSpec((1, tm * (d // LANES), LANES), lambda b, i: (b, i, 0)),
                   tok(LANES), tok(LANES), pl.BlockSpec((8, LANES), lambda b, i: (0, 0))],
        compiler_params=_params(("arbitrary", "arbitrary"), 48),
    )(x, ya, yb, ga, gb, mod3, wa, wb, wo, g1, b1, wr_hi, wr_lo, br)


def _route_kernel(code_ref, cnt_ref, pos_ref, tmap_ref, nv_ref, carry_s, pstart_s, *, tr, tm):
    i = pl.program_id(0)
    lane = lax.broadcasted_iota(jnp.int32, (tr, LANES), 1)
    code = code_ref[...]
    picked = jnp.where(code > 0.0, 1.0, 0.0)
    tile_count = jnp.broadcast_to(picked.sum(axis=0, keepdims=True), (8, LANES))

    @pl.when(i == 0)
    def _():
        lane8 = lax.broadcasted_iota(jnp.int32, (8, LANES), 1)
        padded = jnp.floor((cnt_ref[...] + (tm - 1)) * (1.0 / tm)) * tm
        cum = padded
        shift = 1
        while shift < N_EXPERTS:
            cum = cum + jnp.where(lane8 >= shift, pltpu.roll(cum, shift, 1), 0.0)
            shift *= 2
        pstart_s[...] = cum - padded
        carry_s[...] = jnp.zeros_like(carry_s)
        nt = tmap_ref.shape[0]
        tile_start = (lax.broadcasted_iota(jnp.int32, (nt, LANES), 0) * tm).astype(F32)
        lane_t = lax.broadcasted_iota(jnp.int32, (nt, LANES), 1)
        done = jnp.where((lane_t < N_EXPERTS) & (cum[0:1] <= tile_start), 1.0, 0.0)
        te = jnp.minimum(done.sum(axis=-1, keepdims=True), N_EXPERTS - 1.0)
        tmap_ref[...] = jnp.broadcast_to(te, (nt, LANES)).astype(jnp.int32)
        total = jnp.where(lane8 == N_EXPERTS - 1, cum, 0.0).sum(axis=-1, keepdims=True)
        row8 = lax.broadcasted_iota(jnp.int32, (8, LANES), 0)
        meta = jnp.where(row8 == 0, jnp.broadcast_to(total * (1.0 / tm), (8, LANES)),
                         jnp.where(row8 == 1, cum - padded + cnt_ref[...], cum))
        nv_ref[...] = meta.astype(jnp.int32)

    r = lax.broadcasted_iota(jnp.int32, (tr, tr), 0)
    c = lax.broadcasted_iota(jnp.int32, (tr, tr), 1)
    tri = jnp.where(c < r, 1.0, 0.0).astype(BF16)
    base = _dot(tri, picked.astype(BF16)) + (carry_s[0:1] + pstart_s[0:1])
    carry_s[...] += tile_count
    pos = jnp.zeros((tr, LANES), F32)
    for k in range(TOP_K):
        pk = jnp.where(code == k + 1.0, base, 0.0).sum(axis=-1, keepdims=True)
        pos = jnp.where(lane == k, pk, pos)
    pos_ref[...] = pos.astype(jnp.int32)


def _route(code, cnt, tm):
    n = code.shape[0]
    tr = min(TR_ROUTE, n)
    return pl.pallas_call(
        functools.partial(_route_kernel, tr=tr, tm=tm),
        out_shape=[jax.ShapeDtypeStruct((n, LANES), jnp.int32),
                   jax.ShapeDtypeStruct((MAX_TILES_PAD, LANES), jnp.int32),
                   jax.ShapeDtypeStruct((8, LANES), jnp.int32)],
        grid=(n // tr,),
        in_specs=[pl.BlockSpec((tr, LANES), lambda i: (i, 0)),
                  pl.BlockSpec((8, LANES), lambda i: (0, 0))],
        out_specs=[pl.BlockSpec((tr, LANES), lambda i: (i, 0)),
                   pl.BlockSpec((MAX_TILES_PAD, LANES), lambda i: (0, 0)),
                   pl.BlockSpec((8, LANES), lambda i: (0, 0))],
        scratch_shapes=[pltpu.VMEM((8, LANES), F32)] * 2,
        compiler_params=_params(("arbitrary",)),
    )(code, cnt)


def _dispatch_kernel(pos_ref, fill_ref, h_ref, xs_ref, ztile_s, sem, fill_sem, tail_sem,
                     *, tm, nsub, tm_moe):
    def tile(ref, r):
        return ref.at[pl.ds(pl.multiple_of(r * nsub, nsub), nsub)]

    @pl.when(pl.program_id(0) == 0)
    def _():
        ztile_s[...] = jnp.zeros_like(ztile_s)

        def tail_copy(t):
            rows = tm_moe * nsub
            return pltpu.make_async_copy(
                ztile_s, xs_ref.at[pl.ds(pl.multiple_of(t * rows, rows), rows)], tail_sem)

        sizes = [1 << b for b in reversed(range(tm_moe.bit_length() - 1))]

        def fill_pieces(e, act):
            lo, hi = fill_ref[LANES + e], fill_ref[2 * LANES + e]
            at = lo
            for size in sizes:
                has = ((hi - lo) & size) != 0
                copy = pltpu.make_async_copy(
                    ztile_s.at[pl.ds(0, size * nsub)],
                    xs_ref.at[pl.ds(pl.multiple_of(at * nsub, nsub), size * nsub)], fill_sem)
                pl.when(has)(functools.partial(act, copy))
                at = at + jnp.where(has, size, 0)

        def fill_start(e, c):
            fill_pieces(e, lambda copy: copy.start())
            return c

        def fill_wait(e, c):
            fill_pieces(e, lambda copy: copy.wait())
            return c

        n_tiles = xs_ref.shape[0] // (tm_moe * nsub)

        def tail_start(t, c):
            tail_copy(t).start()
            return c

        def tail_drain(_, c):
            tail_copy(0).wait()
            return c

        lax.fori_loop(0, N_EXPERTS, fill_start, 0)
        lax.fori_loop(fill_ref[0], n_tiles, tail_start, 0)
        lax.fori_loop(0, N_EXPERTS, fill_wait, 0)
        lax.fori_loop(fill_ref[0], n_tiles, tail_drain, 0)

    def issue(j, c):
        for r in range(COMBINE_GROUP):
            row = pl.multiple_of(j * COMBINE_GROUP, COMBINE_GROUP) + r
            for k in range(TOP_K):
                pltpu.make_async_copy(tile(h_ref, row), tile(xs_ref, pos_ref[row * TOP_K + k]),
                                      sem).start(priority=k % 2)
        return c

    lax.fori_loop(0, tm // COMBINE_GROUP, issue, 0)
    for _ in range(TOP_K):
        pltpu.make_async_copy(h_ref, xs_ref.at[pl.ds(0, tm * nsub)], sem).wait()


def _dispatch(pos_flat, fill, h2, cap, nsub, tm_moe):
    n = h2.shape[0] // nsub
    tm = min(TM_ROWS, n)
    return pl.pallas_call(
        functools.partial(_dispatch_kernel, tm=tm, nsub=nsub, tm_moe=tm_moe),
        out_shape=jax.ShapeDtypeStruct((cap * nsub, LANES), F32),
        grid=(n // tm,),
        in_specs=[pl.BlockSpec((tm * TOP_K,), lambda i: (i,), memory_space=pltpu.SMEM),
                  pl.BlockSpec(fill.shape, lambda i: (0,), memory_space=pltpu.SMEM),
                  pl.BlockSpec((tm * nsub, LANES), lambda i: (i, 0))],
        out_specs=pl.BlockSpec(memory_space=pl.ANY),
        scratch_shapes=[pltpu.VMEM((tm_moe * nsub, LANES), F32),
                        pltpu.SemaphoreType.DMA(()), pltpu.SemaphoreType.DMA(()),
                        pltpu.SemaphoreType.DMA(())],
        compiler_params=_params(("arbitrary",)),
    )(pos_flat, fill, h2)


def _moe_kernel(te_ref, nv_ref, x_ref, bg_ref, bu_ref, bd_ref, wg_hbm, wu_hbm, wd_hbm,
                y_ref, wf_s, wg_s, wu_s, wd_s, slot_s, sem):
    i = pl.program_id(0)
    nv = nv_ref[0]
    last = te_ref.shape[0] - 1

    def weight_copies(e, slot):
        return [pltpu.make_async_copy(w.at[e], wf_s.at[slot, j], sem.at[slot])
                for j, w in enumerate((wg_hbm, wu_hbm, wd_hbm))]

    @pl.when(i < nv)
    def _():
        e = te_ref[i]

        @pl.when(i == 0)
        def _():
            slot_s[0] = 0
            for c in weight_copies(e, 0):
                c.start()

        @pl.when((i == 0) | (e != te_ref[jnp.maximum(i - 1, 0)]))
        def _():
            slot = slot_s[0]
            for c in weight_copies(e, slot):
                c.wait()
            nxt = lax.while_loop(lambda j: (j < nv) & (te_ref[jnp.minimum(j, last)] == e),
                                 lambda j: j + 1, i + 1)

            @pl.when(nxt < nv)
            def _():
                for c in weight_copies(te_ref[jnp.minimum(nxt, last)], 1 - slot):
                    c.start()

            wg_s[...] = wf_s[slot, 0].astype(BF16)
            wu_s[...] = wf_s[slot, 1].astype(BF16)
            wd_s[...] = wf_s[slot, 2].astype(BF16)
            slot_s[0] = 1 - slot

        xb = _load_token_tiles(x_ref, wg_s.shape[0] // LANES).astype(BF16)
        g = _dot(xb, wg_s[...]) + bg_ref[0]
        u = _dot(xb, wu_s[...]) + bu_ref[0]
        g = jnp.minimum(g, SWIGLU_LIMIT)
        u = jnp.clip(u, -SWIGLU_LIMIT, SWIGLU_LIMIT)
        h = (u + 1.0) * (g * jax.nn.sigmoid(SWIGLU_ALPHA * g))
        _store_token_tiles(y_ref, _dot(h.astype(BF16), wd_s[...]) + bd_ref[0])

    @pl.when(i >= nv)
    def _():
        y_ref[...] = jnp.zeros_like(y_ref)


def _moe(tile_expert, n_valid, xs, w_gate, b_gate, w_up, b_up, w_down, b_down, tm):
    e, d, f = w_gate.shape
    assert d == f
    nsub = d // LANES
    cap = xs.shape[0] // nsub
    n_tiles = cap // tm

    def tile(i, te, nv):
        return jnp.minimum(i, nv[0] - 1)

    exp3 = lambda i, te, nv: (te[tile(i, te, nv)], 0, 0)
    grid_spec = pltpu.PrefetchScalarGridSpec(
        num_scalar_prefetch=2,
        grid=(n_tiles,),
        in_specs=[pl.BlockSpec((tm * nsub, LANES), lambda i, te, nv: (tile(i, te, nv), 0)),
                  pl.BlockSpec((1, 1, f), exp3), pl.BlockSpec((1, 1, f), exp3),
                  pl.BlockSpec((1, 1, d), exp3),
                  pl.BlockSpec(memory_space=pl.ANY), pl.BlockSpec(memory_space=pl.ANY),
                  pl.BlockSpec(memory_space=pl.ANY)],
        out_specs=pl.BlockSpec((tm * nsub, LANES), lambda i, te, nv: (i, 0)),
        scratch_shapes=[pltpu.VMEM((2, 3, d, f), F32),
                        pltpu.VMEM((d, f), BF16), pltpu.VMEM((d, f), BF16), pltpu.VMEM((f, d), BF16),
                        pltpu.SMEM((1,), jnp.int32), pltpu.SemaphoreType.DMA((2,))],
    )
    return pl.pallas_call(
        _moe_kernel,
        out_shape=jax.ShapeDtypeStruct((cap * nsub, LANES), F32),
        grid_spec=grid_spec,
        compiler_params=_params(("arbitrary",), 56),
    )(tile_expert, n_valid, xs, b_gate.reshape(e, 1, f), b_up.reshape(e, 1, f),
      b_down.reshape(e, 1, d), w_gate, w_up, w_down)


def _combine_kernel(pos_ref, posn_ref, w_ref, x1_ref, mod_ref, g_ref, b_ref, y_ref, o_ref,
                    buf, sem, *, tm):
    i = pl.program_id(0)
    n = pl.num_programs(0)
    nsub = x1_ref.shape[1] // LANES

    group = COMBINE_GROUP

    def issue_rows(p_ref, slot, r0):
        for r in range(group):
            for k in range(TOP_K):
                row = r0 + r
                src = pl.ds(pl.multiple_of(p_ref[row * TOP_K + k] * nsub, nsub), nsub)
                dst = pl.ds(pl.multiple_of(row * nsub, nsub), nsub)
                pltpu.make_async_copy(y_ref.at[src], buf.at[slot, k, dst],
                                      sem.at[slot]).start(priority=k % 2)

    def reduce_rows(slot, r0):
        rows = pl.ds(r0, group)
        w = w_ref[rows, :]
        f = jnp.zeros((group, nsub * LANES), F32)
        for k in range(TOP_K):
            yk = jnp.concatenate(
                [buf[slot, k, pl.ds(r0 * nsub + s, group, stride=nsub), :] for s in range(nsub)],
                axis=1)
            f = f + w[:, k:k + 1] * yk
        o_ref[rows, :] = f

    @pl.when(i == 0)
    def _():
        def body(j, c):
            issue_rows(pos_ref, 0, pl.multiple_of(j * group, group))
            return c
        lax.fori_loop(0, tm // group, body, 0)

    slot = i % 2
    for k in range(TOP_K):
        pltpu.make_async_copy(y_ref.at[pl.ds(0, tm * nsub)], buf.at[slot, k], sem.at[slot]).wait()

    @pl.when(i + 1 < n)
    def _():
        def body(j, c):
            r0 = pl.multiple_of(j * group, group)
            issue_rows(posn_ref, 1 - slot, r0)
            reduce_rows(slot, r0)
            return c
        lax.fori_loop(0, tm // group, body, 0)

    @pl.when(i + 1 == n)
    def _():
        def body(j, c):
            reduce_rows(slot, pl.multiple_of(j * group, group))
            return c
        lax.fori_loop(0, tm // group, body, 0)

    o_ref[...] = _layer_norm(DEEPNORM_ALPHA * x1_ref[...] + mod_ref[0, 5:6, :] * o_ref[...],
                             g_ref[...], b_ref[...])


def _combine(pos_flat, w, x1, mod3, g2, b2, y, seq):
    n, d = x1.shape
    tm = min(TM_ROWS, seq)
    nsteps = n // tm
    per_seq = seq // tm
    return pl.pallas_call(
        functools.partial(_combine_kernel, tm=tm),
        out_shape=jax.ShapeDtypeStruct((n, d), F32),
        grid=(nsteps,),
        in_specs=[pl.BlockSpec((tm * TOP_K,), lambda i: (i,), memory_space=pltpu.SMEM),
                  pl.BlockSpec((tm * TOP_K,), lambda i: (jnp.minimum(i + 1, nsteps - 1),),
                               memory_space=pltpu.SMEM),
                  pl.BlockSpec((tm, LANES), lambda i: (i, 0)),
                  pl.BlockSpec((tm, d), lambda i: (i, 0)),
                  pl.BlockSpec((1, 6, d), lambda i: (i // per_seq, 0, 0)),
                  pl.BlockSpec(g2.shape, lambda i: (0, 0)),
                  pl.BlockSpec(b2.shape, lambda i: (0, 0)),
                  pl.BlockSpec(memory_space=pl.ANY)],
        out_specs=pl.BlockSpec((tm, d), lambda i: (i, 0)),
        scratch_shapes=[pltpu.VMEM((2, TOP_K, tm * (d // LANES), LANES), F32),
                        pltpu.SemaphoreType.DMA((2,))],
        compiler_params=_params(("arbitrary",), 32),
    )(pos_flat, pos_flat, w, x1, mod3, g2, b2, y)


def _prep_w_in(w_in):
    d = w_in.shape[1] - (3 * A_WIDTH + B_Q_LORA + B_KV_LORA + B_ROPE_DIM)
    d //= 2
    cuts, off = [], 0
    for n in (A_WIDTH, A_WIDTH, A_WIDTH, B_Q_LORA, B_KV_LORA, B_ROPE_DIM, d, d):
        cuts.append(w_in[:, off:off + n])
        off += n
    q, k, v, cq, ckv, kr, ga, gb = cuts
    qk = jnp.concatenate([q * (A_HEAD_DIM ** -0.5 * LOG2_E), k], axis=1)
    kr = jnp.pad(kr, ((0, 0), (B_NOPE_DIM, LANES - B_NOPE_DIM - B_ROPE_DIM)))
    return tuple(w.astype(BF16) for w in (qk, v, cq, ckv, kr, ga, gb))


def _prep_w_uq(w_uq):
    r = w_uq.shape[0]
    w = w_uq.reshape(r, B_HEADS, B_QK_DIM)
    w = jnp.pad(w, ((0, 0), (0, 0), (0, LANES - B_QK_DIM)))
    return w.reshape(r, B_HEADS * LANES).astype(BF16)


def _prep_w_ukv(w_ukv):
    r = w_ukv.shape[0]
    w = w_ukv.reshape(r, B_HEADS, B_NOPE_DIM + B_V_DIM)
    wk = jnp.pad(w[:, :, :B_NOPE_DIM], ((0, 0), (0, 0), (0, LANES - B_NOPE_DIM)))
    wv = w[:, :, B_NOPE_DIM:].reshape(r, B_HEADS // 2, 2 * B_V_DIM)
    wv = jnp.pad(wv, ((0, 0), (0, 0), (0, V_PAIR_LANES - 2 * B_V_DIM)))
    return (wk.reshape(r, B_HEADS * LANES).astype(BF16),
            wv.reshape(r, B_HEADS // 2 * V_PAIR_LANES).astype(BF16))


def _rope_tables(positions):
    half = B_ROPE_DIM // 2
    freqs = ROPE_THETA ** (-jnp.arange(half, dtype=F32) / half)
    ang = positions.astype(F32).reshape(1, -1) * freqs[:, None]
    cos, sin = jnp.cos(ang), jnp.sin(ang)
    z = jnp.zeros((B_ROPE_DIM, ang.shape[1]), F32)
    return jnp.concatenate([sin, sin, z, cos, cos, z], 0)


def kernel(x, c, positions, w_ada, b_ada, w_in, rms_q, w_uq, rms_kv, w_ukv, rel_bias,
           w_branch_a, w_branch_b, w_out, ln1_g, ln1_b, w_router, b_router,
           w_gate, b_gate, w_up, b_up, w_down, b_down, ln2_g, ln2_b):
    bsz, seq, d = x.shape
    n_tok = bsz * seq
    assert w_ada.shape[0] == DEPTH == 1
    row = lambda v: v.reshape(1, -1)

    mod3 = _ada(c, w_ada[0], b_ada[0]).reshape(bsz, 6, d)

    qk, va, cq, ckv, kr, ga, gb = _inproj(x, mod3, _prep_w_in(w_in[0]))
    ya = _attn_a(qk, va, _bias_table(rel_bias[0], min(TQ_A, seq)))
    wk, wv = _prep_w_ukv(w_ukv[0])
    qm, km, vm = _mla_proj(cq, ckv, kr, _rope_tables(positions), row(rms_q[0]), row(rms_kv[0]),
                           _prep_w_uq(w_uq[0]), wk, wv)
    yb = _mla_attn(qm, km, vm)

    wr = jnp.pad(w_router[0], ((0, 0), (0, LANES - N_EXPERTS)))
    br = jnp.pad(b_router[0], (0, LANES - N_EXPERTS), constant_values=NEG_INF)
    x1, h2, code, wts, cnt = _merge(x, ya, yb, ga, gb, mod3,
                                    w_branch_a[0].astype(BF16), w_branch_b[0].astype(BF16),
                                    w_out[0].astype(BF16), row(ln1_g[0]), row(ln1_b[0]), wr, row(br))
    wts = wts.reshape(n_tok, LANES)

    tm = TM_MOE
    n_tiles = -(-(n_tok * TOP_K) // tm) + N_EXPERTS
    assert n_tiles <= MAX_TILES_PAD
    pos, tmap, meta = _route(code.reshape(n_tok, LANES), cnt, tm)
    pos_flat = pos[:, :TOP_K].reshape(-1)
    fill = meta[:3].reshape(-1)
    nsub = d // LANES
    xs = _dispatch(pos_flat, fill, h2.reshape(n_tok * nsub, LANES), n_tiles * tm, nsub, tm)
    y = _moe(tmap[:n_tiles, 0], meta[0, :1], xs, w_gate[0], b_gate[0], w_up[0], b_up[0],
             w_down[0], b_down[0], tm)
    out = _combine(pos_flat, wts, x1.reshape(n_tok, d), mod3, row(ln2_g[0]), row(ln2_b[0]), y, seq)
    return out.reshape(bsz, seq, d)
```

```python
import functools

import jax
import jax.numpy as jnp
from jax import lax
from jax.experimental import pallas as pl
from jax.experimental.pallas import tpu as pltpu

F32 = jnp.float32
BF16 = jnp.bfloat16

CHUNK = 64
A_HEADS = 8
A_HEAD_DIM = 64
A_LEFT_CHUNKS = 8
A_MAX_REL = 128
A_WIDTH = A_HEADS * A_HEAD_DIM
B_HEADS = 8
B_NOPE_DIM = 64
B_ROPE_DIM = 32
B_V_DIM = 64
B_Q_LORA = 384
B_KV_LORA = 256
B_QK_DIM = B_NOPE_DIM + B_ROPE_DIM
ROPE_THETA = 10000.0
N_EXPERTS = 32
TOP_K = 4
SWIGLU_LIMIT = 7.0
SWIGLU_ALPHA = 1.702
DEPTH = 1
DEEPNORM_ALPHA = (2.0 * DEPTH) ** 0.25
LN_EPS = 1e-5
RMS_EPS = 1e-6
NEG_INF = -1e30
LOG2_E = 1.4426950408889634

LANES = 128
MXU_COLS = 256
V_PAIR_LANES = MXU_COLS

TM_PROJ = 512
TQ_A = 256
A_QBLOCKS = 2
TQ_B = 512
TM_MERGE = 512
MERGE_SPLIT = 1
TR_ROUTE = 1024
TM_MOE = 512
TM_ROWS = 256
COMBINE_GROUP = 8
MAX_TILES_PAD = 512


def _params(sem, vmem_mb=None):
    return pltpu.CompilerParams(
        dimension_semantics=sem,
        vmem_limit_bytes=None if vmem_mb is None else vmem_mb << 20)


def _dot(a, b):
    return jnp.dot(a, b, preferred_element_type=F32)


def _dot_nt(a, b):
    return lax.dot_general(a, b, (((1,), (1,)), ((), ())), preferred_element_type=F32)


def _store_token_tiles(ref, x):
    rows, d = x.shape
    nsub = d // LANES
    for s in range(nsub):
        ref[pl.ds(s, rows, stride=nsub), :] = x[:, s * LANES:(s + 1) * LANES]


def _load_token_tiles(ref, nsub):
    rows = ref.shape[0] // nsub
    return jnp.concatenate([ref[pl.ds(s, rows, stride=nsub), :] for s in range(nsub)], axis=1)


def _layer_norm(x, g, b):
    mu = jnp.mean(x, axis=-1, keepdims=True)
    xc = x - mu
    var = jnp.mean(xc * xc, axis=-1, keepdims=True)
    return xc * lax.rsqrt(var + LN_EPS) * g + b


def _ada_kernel(c_ref, w_ref, b_ref, o_ref):
    c = c_ref[...]
    sc = c * jax.nn.sigmoid(c)
    o_ref[...] = jnp.dot(sc, w_ref[...], preferred_element_type=F32,
                         precision=lax.Precision.HIGHEST) + b_ref[...]


def _ada(c, w_ada, b_ada):
    bsz, d = c.shape
    m = w_ada.shape[1]
    tn = 1024
    return pl.pallas_call(
        _ada_kernel,
        out_shape=jax.ShapeDtypeStruct((bsz, m), F32),
        grid=(m // tn,),
        in_specs=[pl.BlockSpec((bsz, d), lambda j: (0, 0)),
                  pl.BlockSpec((d, tn), lambda j: (0, j)),
                  pl.BlockSpec((1, tn), lambda j: (0, j))],
        out_specs=pl.BlockSpec((bsz, tn), lambda j: (0, j)),
        compiler_params=_params(("arbitrary",)),
    )(c, w_ada, b_ada.reshape(1, m))


def _inproj_kernel(x_ref, mod_ref, win_hbm, qk_ref, va_ref, cq_ref, ckv_ref, kr_ref, ga_ref, gb_ref,
                   wf_s, wqk_ref, wv_ref, wcq_ref, wckv_ref, wkr_ref, wga_ref, wgb_ref, sem):
    @pl.when((pl.program_id(0) == 0) & (pl.program_id(1) == 0))
    def _():
        copy = pltpu.make_async_copy(win_hbm, wf_s, sem)
        copy.start()
        copy.wait()
        d = wf_s.shape[0]
        off = 0
        for dst, n in ((wqk_ref, 2 * A_WIDTH), (wv_ref, A_WIDTH), (wcq_ref, B_Q_LORA),
                       (wckv_ref, B_KV_LORA), (None, B_ROPE_DIM), (wga_ref, d), (wgb_ref, d)):
            if dst is not None:
                dst[...] = wf_s[:, off:off + n].astype(BF16)
            else:
                kr = wf_s[:, off:off + n]
                wkr_ref[...] = jnp.concatenate(
                    [jnp.zeros((d, B_NOPE_DIM), F32), kr,
                     jnp.zeros((d, LANES - B_NOPE_DIM - n), F32)], axis=1).astype(BF16)
            off += n
        wqk_ref[:, :A_WIDTH] = (wf_s[:, :A_WIDTH] * (A_HEAD_DIM ** -0.5 * LOG2_E)).astype(BF16)

    m = mod_ref[0]
    h = (x_ref[0] * (1.0 + m[1:2]) + m[0:1]).astype(BF16)
    for w_ref, ref in ((wqk_ref, qk_ref), (wcq_ref, cq_ref), (wckv_ref, ckv_ref),
                       (wkr_ref, kr_ref), (wga_ref, ga_ref), (wgb_ref, gb_ref)):
        ref[0] = _dot(h, w_ref[...]).astype(ref.dtype)
    v = _dot(h, wv_ref[...]).astype(BF16)
    ones_blk = jnp.ones((v.shape[0], LANES), BF16)
    parts = []
    for hp in range(A_WIDTH // LANES):
        parts += [v[:, hp * LANES:(hp + 1) * LANES], ones_blk]
    va_ref[0] = jnp.concatenate(parts, axis=1)


def _inproj(x, mod3, w_in):
    bsz, seq, d = x.shape
    tm = min(TM_PROJ, seq)
    widths = (2 * A_WIDTH, A_WIDTH // LANES * V_PAIR_LANES, B_Q_LORA, B_KV_LORA, LANES, d, d)
    dtypes = (BF16, BF16, BF16, BF16, F32, BF16, BF16)
    w_widths = (2 * A_WIDTH, A_WIDTH, B_Q_LORA, B_KV_LORA, LANES, d, d)
    assert w_in.shape[1] == sum(w_widths) - LANES + B_ROPE_DIM
    return pl.pallas_call(
        _inproj_kernel,
        out_shape=[jax.ShapeDtypeStruct((bsz, seq, n), dt) for n, dt in zip(widths, dtypes)],
        grid=(bsz, seq // tm),
        in_specs=[pl.BlockSpec((1, tm, d), lambda b, i: (b, i, 0)),
                  pl.BlockSpec((1, 6, d), lambda b, i: (b, 0, 0)),
                  pl.BlockSpec(memory_space=pl.ANY)],
        out_specs=[pl.BlockSpec((1, tm, n), lambda b, i: (b, i, 0)) for n in widths],
        scratch_shapes=[pltpu.VMEM(w_in.shape, F32)]
                       + [pltpu.VMEM((d, n), BF16) for n in w_widths]
                       + [pltpu.SemaphoreType.DMA(())],
        compiler_params=_params(("arbitrary", "arbitrary"), 56),
    )(x, mod3, w_in)


def _attn_a_kernel(q_ref, k_ref, v_ref, bias_ref, o_ref, *, tq, nd):
    band = nd * tq
    lane = lax.broadcasted_iota(jnp.int32, (1, LANES), 1)
    for sub in range(A_QBLOCKS):
        i = pl.program_id(2) * A_QBLOCKS + sub
        start = pl.multiple_of(jnp.maximum(i - (nd - 1), 0) * tq, tq)
        variant = jnp.minimum(i, nd - 1)
        q = q_ref[0, sub * tq:(sub + 1) * tq, :]
        k = k_ref[0, pl.ds(start, band), :]
        v = v_ref[0, pl.ds(start, band), :]
        zero = jnp.zeros_like(k)
        kk = jnp.concatenate([jnp.where(lane < A_HEAD_DIM, k, zero),
                              jnp.where(lane >= A_HEAD_DIM, k, zero)], axis=0)
        s = _dot_nt(q, kk)
        ps = []
        for hh in range(2):
            sh = s[:, hh * band:(hh + 1) * band] + bias_ref[hh, variant]
            m = sh.max(axis=-1, keepdims=True)
            ps.append(jnp.exp2(sh - m).astype(BF16))
        acc = _dot(jnp.concatenate(ps, axis=0), v)
        out = acc[:, :LANES] / acc[:, LANES:]
        o_ref[0, sub * tq:(sub + 1) * tq, :] = jnp.where(
            lane < A_HEAD_DIM, out[:tq], out[tq:]).astype(o_ref.dtype)


def _attn_a(qk, va, bias_tab):
    bsz, seq, _ = qk.shape
    tq = min(TQ_A, seq)
    nd = bias_tab.shape[1]
    assert seq >= nd * tq
    ncb = A_WIDTH // LANES
    return pl.pallas_call(
        functools.partial(_attn_a_kernel, tq=tq, nd=nd),
        out_shape=jax.ShapeDtypeStruct((bsz, seq, A_WIDTH), BF16),
        grid=(ncb, bsz, seq // (A_QBLOCKS * tq)),
        in_specs=[pl.BlockSpec((1, A_QBLOCKS * tq, LANES), lambda hp, b, i: (b, i, hp)),
                  pl.BlockSpec((1, seq, LANES), lambda hp, b, i: (b, 0, ncb + hp)),
                  pl.BlockSpec((1, seq, V_PAIR_LANES), lambda hp, b, i: (b, 0, hp)),
                  pl.BlockSpec((2, nd, tq, nd * tq), lambda hp, b, i: (hp, 0, 0, 0))],
        out_specs=pl.BlockSpec((1, A_QBLOCKS * tq, LANES), lambda hp, b, i: (b, i, hp)),
        compiler_params=_params(("arbitrary",) * 3, 40),
    )(qk, qk, va, bias_tab)


def _bias_table(rel_bias, tq):
    nd = (A_LEFT_CHUNKS * CHUNK + tq - 1) // tq + 1
    width = nd * tq
    period = width + tq
    u = jnp.arange(period)
    dist = jnp.where(u < width, (nd - 1) * tq - u, (nd - 1) * tq + period - u)
    vec = rel_bias[:, jnp.clip(dist, -A_MAX_REL, A_MAX_REL) + A_MAX_REL].astype(F32)
    toep = jnp.tile(vec, (1, tq))[:, :tq * (period - 1)].reshape(-1, tq, period - 1)[:, :, :width]
    qi = jnp.arange(tq)[:, None]
    x = jnp.arange(width)[None, :]
    dchunk = qi // CHUNK - x // CHUNK + (nd - 1) * (tq // CHUNK)
    valid = (dchunk >= 0) & (dchunk <= A_LEFT_CHUNKS)
    table = jnp.where(valid[None], toep * LOG2_E, NEG_INF)
    variants = []
    for v in range(nd):
        cut = (nd - 1 - v) * tq
        variants.append(jnp.pad(table[:, :, cut:], ((0, 0), (0, 0), (0, cut)),
                                constant_values=NEG_INF))
    return jnp.stack(variants, axis=1)


def _mla_proj_kernel(cq_ref, ckv_ref, kr_ref, tab_ref, rq_ref, rkv_ref,
                     wq_ref, wk_ref, wv_ref, q_out, k_out, v_out):
    def rms(x, g):
        xf = x.astype(F32)
        ms = jnp.mean(xf * xf, axis=-1, keepdims=True)
        return (xf * lax.rsqrt(ms + RMS_EPS) * g).astype(BF16)

    cqn = rms(cq_ref[0], rq_ref[...])
    ckvn = rms(ckv_ref[0], rkv_ref[...])
    q = _dot(cqn, wq_ref[...])
    kn = _dot(ckvn, wk_ref[...])
    vlane = lax.broadcasted_iota(jnp.int32, (1, v_out.shape[-1]), 1)
    ones_lane = jnp.where(vlane % V_PAIR_LANES >= 2 * B_V_DIM, 1.0, 0.0)
    v_out[0] = (_dot(ckvn, wv_ref[...]) + ones_lane).astype(v_out.dtype)
    tab = tab_ref[...].T
    tlane = lax.broadcasted_iota(jnp.int32, tab.shape, 1)
    rope_end = B_NOPE_DIM + B_ROPE_DIM
    in_rope = (tlane >= B_NOPE_DIM) & (tlane < rope_end)
    cos_t = jnp.where(tlane < B_NOPE_DIM, 1.0, jnp.where(in_rope, tab, 0.0))
    sin_t = jnp.where(in_rope, pltpu.roll(tab, B_NOPE_DIM, 1), 0.0)
    first = tlane < B_NOPE_DIM + B_ROPE_DIM // 2
    sin_a = jnp.where(first, -sin_t, 0.0)
    sin_b = jnp.where(first, 0.0, sin_t)

    def rope(x, tables):
        c, sa, sb = tables
        return (x * c + pltpu.roll(x, LANES - B_ROPE_DIM // 2, 1) * sa
                + pltpu.roll(x, B_ROPE_DIM // 2, 1) * sb)

    kpe = rope(kr_ref[0], (cos_t, sin_a, sin_b))
    scale = B_QK_DIM ** -0.5 * LOG2_E
    cos_q, sin_q = cos_t * scale, sin_t * scale
    nq = B_HEADS * LANES
    per_block = LANES // B_ROPE_DIM
    for h in range(B_HEADS):
        sl = slice(h * LANES, (h + 1) * LANES)
        blk = nq + (h // per_block) * LANES
        partner = q[:, blk:blk + LANES]
        shift = (B_NOPE_DIM - B_ROPE_DIM * (h % per_block)) % LANES
        if shift:
            partner = pltpu.roll(partner, shift, 1)
        q_out[0, :, sl] = (q[:, sl] * cos_q + partner * sin_q).astype(q_out.dtype)
        k_out[0, :, sl] = (kn[:, sl] + kpe).astype(k_out.dtype)


def _mla_proj(cq, ckv, kr, rope_tab, rms_q, rms_kv, wq, wk, wv):
    bsz, seq, _ = cq.shape
    tm = min(TM_PROJ, seq)
    tok = lambda n: pl.BlockSpec((1, tm, n), lambda b, i: (b, i, 0))
    full = lambda a: pl.BlockSpec(a.shape, lambda b, i: (0,) * a.ndim)
    return pl.pallas_call(
        _mla_proj_kernel,
        out_shape=[jax.ShapeDtypeStruct((bsz, seq, B_HEADS * LANES), BF16),
                   jax.ShapeDtypeStruct((bsz, seq, B_HEADS * LANES), BF16),
                   jax.ShapeDtypeStruct((bsz, seq, wv.shape[1]), BF16)],
        grid=(bsz, seq // tm),
        in_specs=[tok(B_Q_LORA), tok(B_KV_LORA), tok(LANES),
                  pl.BlockSpec((LANES, tm), lambda b, i: (0, b * (seq // tm) + i)),
                  full(rms_q), full(rms_kv), full(wq), full(wk), full(wv)],
        out_specs=[tok(B_HEADS * LANES), tok(B_HEADS * LANES), tok(wv.shape[1])],
        compiler_params=_params(("arbitrary", "arbitrary"), 32),
    )(cq, ckv, kr, rope_tab, rms_q, rms_kv, wq, wk, wv)


def _mla_attn_kernel(q_ref, k_ref, v_ref, o_ref, s_a, s_b, m_s, acc_s, *, t):
    i = pl.program_id(2)
    lane = lax.broadcasted_iota(jnp.int32, (1, LANES), 1)
    m_s[...] = jnp.full(m_s.shape, NEG_INF, F32)
    acc_s[...] = jnp.zeros(acc_s.shape, F32)

    def produce(kj, s_ref):
        start = pl.multiple_of(kj * t, t)
        for hh in range(2):
            q = q_ref[0, :, hh * LANES:(hh + 1) * LANES]
            k = k_ref[0, pl.ds(start, t), hh * LANES:(hh + 1) * LANES]
            s_ref[hh] = _dot_nt(q, k)

    def consume(kj, s_ref, masked):
        start = pl.multiple_of(kj * t, t)
        v = v_ref[0, pl.ds(start, t), :]
        for hh in range(2):
            s = s_ref[hh]
            if masked:
                row_chunk = lax.broadcasted_iota(jnp.int32, (t, t), 0) // CHUNK
                col_chunk = lax.broadcasted_iota(jnp.int32, (t, t), 1) // CHUNK
                s = jnp.where(col_chunk <= row_chunk, s, NEG_INF)
            m_old = m_s[hh]
            m_new = jnp.maximum(m_old, jnp.broadcast_to(s.max(axis=-1, keepdims=True), m_old.shape))
            p = jnp.exp2(s - jnp.tile(m_new, (1, t // LANES))).astype(BF16)
            alpha = jnp.exp2(m_old - m_new)
            acc_s[hh] = jnp.tile(alpha, (1, V_PAIR_LANES // LANES)) * acc_s[hh] + _dot(p, v)
            m_s[hh] = m_new

    def body(pair, c):
        kj = 2 * pair
        produce(kj + 1, s_b)
        consume(kj, s_a, False)
        produce(kj + 2, s_a)
        consume(kj + 1, s_b, False)
        return c

    produce(0, s_a)
    lax.fori_loop(0, i // 2, body, 0)

    @pl.when(i % 2 == 0)
    def _():
        consume(i, s_a, True)

    @pl.when(i % 2 == 1)
    def _():
        produce(i, s_b)
        consume(i - 1, s_a, False)
        consume(i, s_b, True)

    outs = []
    for hh in range(2):
        acc = acc_s[hh]
        outs.append(acc[:, :LANES] / acc[:, LANES:])
    o_ref[0] = jnp.where(lane < B_V_DIM, outs[0], outs[1]).astype(o_ref.dtype)


def _mla_attn(qm, km, vm):
    bsz, seq, _ = qm.shape
    t = min(TQ_B, seq)
    return pl.pallas_call(
        functools.partial(_mla_attn_kernel, t=t),
        out_shape=jax.ShapeDtypeStruct((bsz, seq, B_HEADS * B_V_DIM), BF16),
        grid=(bsz, B_HEADS // 2, seq // t),
        in_specs=[pl.BlockSpec((1, t, 2 * LANES), lambda b, hp, i: (b, i, hp)),
                  pl.BlockSpec((1, seq, 2 * LANES), lambda b, hp, i: (b, 0, hp)),
                  pl.BlockSpec((1, seq, V_PAIR_LANES), lambda b, hp, i: (b, 0, hp))],
        out_specs=pl.BlockSpec((1, t, LANES), lambda b, hp, i: (b, i, hp)),
        scratch_shapes=[pltpu.VMEM((2, t, t), F32), pltpu.VMEM((2, t, t), F32),
                        pltpu.VMEM((2, t, LANES), F32),
                        pltpu.VMEM((2, t, V_PAIR_LANES), F32)],
        compiler_params=_params(("arbitrary",) * 3, 40),
    )(qm, km, vm)


def _merge_kernel(x_ref, ya_ref, yb_ref, ga_ref, gb_ref, mod_ref, wa_ref, wb_ref, wo_ref,
                  g1_ref, b1_ref, wrh_ref, wrl_ref, br_ref, x1_ref, h2_ref, code_ref, wts_ref,
                  cnt_ref):
    @pl.when((pl.program_id(0) == 0) & (pl.program_id(1) == 0))
    def _():
        cnt_ref[...] = jnp.zeros_like(cnt_ref)

    tm = x_ref.shape[1]
    half = tm // MERGE_SPLIT
    for part in range(MERGE_SPLIT):
        _merge_rows(slice(part * half, (part + 1) * half), x_ref, ya_ref, yb_ref, ga_ref, gb_ref,
                    mod_ref, wa_ref, wb_ref, wo_ref, g1_ref, b1_ref, wrh_ref, wrl_ref, br_ref,
                    x1_ref, h2_ref, code_ref, wts_ref, cnt_ref)


def _merge_rows(rows, x_ref, ya_ref, yb_ref, ga_ref, gb_ref, mod_ref, wa_ref, wb_ref, wo_ref,
                g1_ref, b1_ref, wrh_ref, wrl_ref, br_ref, x1_ref, h2_ref, code_ref, wts_ref,
                cnt_ref):
    m = mod_ref[0]
    a = _dot(ya_ref[0, rows, :], wa_ref[...])
    b = _dot(yb_ref[0, rows, :], wb_ref[...])
    merged = (jax.nn.sigmoid(ga_ref[0, rows, :].astype(F32)) * a
              + jax.nn.sigmoid(gb_ref[0, rows, :].astype(F32)) * b)
    o = _dot(merged.astype(BF16), wo_ref[...])
    x1 = _layer_norm(DEEPNORM_ALPHA * x_ref[0, rows, :] + m[2:3] * o, g1_ref[...], b1_ref[...])
    h2 = x1 * (1.0 + m[4:5]) + m[3:4]
    x1_ref[0, rows, :] = x1
    nsub = h2.shape[1] // LANES
    _store_token_tiles(h2_ref.at[0, pl.ds(rows.start * nsub, (rows.stop - rows.start) * nsub)], h2)
    h_hi = h2.astype(BF16)
    h_lo = (h2 - h_hi.astype(F32)).astype(BF16)
    work = (_dot(h_hi, wrh_ref[...]) + _dot(h_hi, wrl_ref[...]) + _dot(h_lo, wrh_ref[...])
            + br_ref[...])
    lane = lax.broadcasted_iota(jnp.int32, work.shape, 1)
    code = jnp.zeros(work.shape, F32)
    vals = []
    for k in range(TOP_K):
        top = work.max(axis=-1, keepdims=True)
        idx = jnp.where(work == top, lane, LANES).min(axis=-1, keepdims=True)
        sel = lane == idx
        work = jnp.where(sel, -jnp.inf, work)
        code = jnp.where(sel, k + 1.0, code)
        vals.append(top)
    den = sum(jnp.exp(v - vals[0]) for v in vals)
    wts = jnp.zeros(work.shape, F32)
    for k in range(TOP_K):
        wts = jnp.where(lane == k, jnp.exp(vals[k] - vals[0]) / den, wts)
    code_ref[0, rows, :] = code
    wts_ref[0, rows, :] = wts
    picked = jnp.where(code > 0.0, 1.0, 0.0)
    cnt_ref[...] += jnp.broadcast_to(picked.sum(axis=0, keepdims=True), cnt_ref.shape)


def _merge(x, ya, yb, ga, gb, mod3, wa, wb, wo, g1, b1, wr, br):
    wr_hi = wr.astype(BF16)
    wr_lo = (wr - wr_hi.astype(F32)).astype(BF16)
    bsz, seq, d = x.shape
    tm = min(TM_MERGE, seq)
    tok = lambda n: pl.BlockSpec((1, tm, n), lambda b, i: (b, i, 0))
    full = lambda a: pl.BlockSpec(a.shape, lambda b, i: (0,) * a.ndim)
    return pl.pallas_call(
        _merge_kernel,
        out_shape=[jax.ShapeDtypeStruct((bsz, seq, d), F32),
                   jax.ShapeDtypeStruct((bsz, seq * (d // LANES), LANES), F32),
                   jax.ShapeDtypeStruct((bsz, seq, LANES), F32),
                   jax.ShapeDtypeStruct((bsz, seq, LANES), F32),
                   jax.ShapeDtypeStruct((8, LANES), F32)],
        grid=(bsz, seq // tm),
        in_specs=[tok(d), tok(A_WIDTH), tok(B_HEADS * B_V_DIM), tok(d), tok(d),
                  pl.BlockSpec((1, 6, d), lambda b, i: (b, 0, 0)),
                  full(wa), full(wb), full(wo), full(g1), full(b1), full(wr_hi), full(wr_lo),
                  full(br)],
        out_specs=[tok(d),
                   pl.BlockSpec((1, tm * (d // LANES), LANES), lambda b, i: (b, i, 0)),
                   tok(LANES), tok(LANES), pl.BlockSpec((8, LANES), lambda b, i: (0, 0))],
        compiler_params=_params(("arbitrary", "arbitrary"), 48),
    )(x, ya, yb, ga, gb, mod3, wa, wb, wo, g1, b1, wr_hi, wr_lo, br)


def _route_kernel(code_ref, cnt_ref, pos_ref, tmap_ref, nv_ref, carry_s, pstart_s, *, tr, tm):
    i = pl.program_id(0)
    lane = lax.broadcasted_iota(jnp.int32, (tr, LANES), 1)
    code = code_ref[...]
    picked = jnp.where(code > 0.0, 1.0, 0.0)
    tile_count = jnp.broadcast_to(picked.sum(axis=0, keepdims=True), (8, LANES))

    @pl.when(i == 0)
    def _():
        lane8 = lax.broadcasted_iota(jnp.int32, (8, LANES), 1)
        padded = jnp.floor((cnt_ref[...] + (tm - 1)) * (1.0 / tm)) * tm
        cum = padded
        shift = 1
        while shift < N_EXPERTS:
            cum = cum + jnp.where(lane8 >= shift, pltpu.roll(cum, shift, 1), 0.0)
            shift *= 2
        pstart_s[...] = cum - padded
        carry_s[...] = jnp.zeros_like(carry_s)
        nt = tmap_ref.shape[0]
        tile_start = (lax.broadcasted_iota(jnp.int32, (nt, LANES), 0) * tm).astype(F32)
        lane_t = lax.broadcasted_iota(jnp.int32, (nt, LANES), 1)
        done = jnp.where((lane_t < N_EXPERTS) & (cum[0:1] <= tile_start), 1.0, 0.0)
        te = jnp.minimum(done.sum(axis=-1, keepdims=True), N_EXPERTS - 1.0)
        tmap_ref[...] = jnp.broadcast_to(te, (nt, LANES)).astype(jnp.int32)
        total = jnp.where(lane8 == N_EXPERTS - 1, cum, 0.0).sum(axis=-1, keepdims=True)
        row8 = lax.broadcasted_iota(jnp.int32, (8, LANES), 0)
        meta = jnp.where(row8 == 0, jnp.broadcast_to(total * (1.0 / tm), (8, LANES)),
                         jnp.where(row8 == 1, cum - padded + cnt_ref[...], cum))
        nv_ref[...] = meta.astype(jnp.int32)

    r = lax.broadcasted_iota(jnp.int32, (tr, tr), 0)
    c = lax.broadcasted_iota(jnp.int32, (tr, tr), 1)
    tri = jnp.where(c < r, 1.0, 0.0).astype(BF16)
    base = _dot(tri, picked.astype(BF16)) + (carry_s[0:1] + pstart_s[0:1])
    carry_s[...] += tile_count
    pos = jnp.zeros((tr, LANES), F32)
    for k in range(TOP_K):
        pk = jnp.where(code == k + 1.0, base, 0.0).sum(axis=-1, keepdims=True)
        pos = jnp.where(lane == k, pk, pos)
    pos_ref[...] = pos.astype(jnp.int32)


def _route(code, cnt, tm):
    n = code.shape[0]
    tr = min(TR_ROUTE, n)
    return pl.pallas_call(
        functools.partial(_route_kernel, tr=tr, tm=tm),
        out_shape=[jax.ShapeDtypeStruct((n, LANES), jnp.int32),
                   jax.ShapeDtypeStruct((MAX_TILES_PAD, LANES), jnp.int32),
                   jax.ShapeDtypeStruct((8, LANES), jnp.int32)],
        grid=(n // tr,),
        in_specs=[pl.BlockSpec((tr, LANES), lambda i: (i, 0)),
                  pl.BlockSpec((8, LANES), lambda i: (0, 0))],
        out_specs=[pl.BlockSpec((tr, LANES), lambda i: (i, 0)),
                   pl.BlockSpec((MAX_TILES_PAD, LANES), lambda i: (0, 0)),
                   pl.BlockSpec((8, LANES), lambda i: (0, 0))],
        scratch_shapes=[pltpu.VMEM((8, LANES), F32)] * 2,
        compiler_params=_params(("arbitrary",)),
    )(code, cnt)


def _dispatch_kernel(pos_ref, fill_ref, h_ref, xs_ref, ztile_s, sem, fill_sem, tail_sem,
                     *, tm, nsub, tm_moe):
    def tile(ref, r):
        return ref.at[pl.ds(pl.multiple_of(r * nsub, nsub), nsub)]

    @pl.when(pl.program_id(0) == 0)
    def _():
        ztile_s[...] = jnp.zeros_like(ztile_s)

        def tail_copy(t):
            rows = tm_moe * nsub
            return pltpu.make_async_copy(
                ztile_s, xs_ref.at[pl.ds(pl.multiple_of(t * rows, rows), rows)], tail_sem)

        sizes = [1 << b for b in reversed(range(tm_moe.bit_length() - 1))]

        def fill_pieces(e, act):
            lo, hi = fill_ref[LANES + e], fill_ref[2 * LANES + e]
            at = lo
            for size in sizes:
                has = ((hi - lo) & size) != 0
                copy = pltpu.make_async_copy(
                    ztile_s.at[pl.ds(0, size * nsub)],
                    xs_ref.at[pl.ds(pl.multiple_of(at * nsub, nsub), size * nsub)], fill_sem)
                pl.when(has)(functools.partial(act, copy))
                at = at + jnp.where(has, size, 0)

        def fill_start(e, c):
            fill_pieces(e, lambda copy: copy.start())
            return c

        def fill_wait(e, c):
            fill_pieces(e, lambda copy: copy.wait())
            return c

        n_tiles = xs_ref.shape[0] // (tm_moe * nsub)

        def tail_start(t, c):
            tail_copy(t).start()
            return c

        def tail_drain(_, c):
            tail_copy(0).wait()
            return c

        lax.fori_loop(0, N_EXPERTS, fill_start, 0)
        lax.fori_loop(fill_ref[0], n_tiles, tail_start, 0)
        lax.fori_loop(0, N_EXPERTS, fill_wait, 0)
        lax.fori_loop(fill_ref[0], n_tiles, tail_drain, 0)

    def issue(j, c):
        for r in range(COMBINE_GROUP):
            row = pl.multiple_of(j * COMBINE_GROUP, COMBINE_GROUP) + r
            for k in range(TOP_K):
                pltpu.make_async_copy(tile(h_ref, row), tile(xs_ref, pos_ref[row * TOP_K + k]),
                                      sem).start(priority=k % 2)
        return c

    lax.fori_loop(0, tm // COMBINE_GROUP, issue, 0)
    for _ in range(TOP_K):
        pltpu.make_async_copy(h_ref, xs_ref.at[pl.ds(0, tm * nsub)], sem).wait()


def _dispatch(pos_flat, fill, h2, cap, nsub, tm_moe):
    n = h2.shape[0] // nsub
    tm = min(TM_ROWS, n)
    return pl.pallas_call(
        functools.partial(_dispatch_kernel, tm=tm, nsub=nsub, tm_moe=tm_moe),
        out_shape=jax.ShapeDtypeStruct((cap * nsub, LANES), F32),
        grid=(n // tm,),
        in_specs=[pl.BlockSpec((tm * TOP_K,), lambda i: (i,), memory_space=pltpu.SMEM),
                  pl.BlockSpec(fill.shape, lambda i: (0,), memory_space=pltpu.SMEM),
                  pl.BlockSpec((tm * nsub, LANES), lambda i: (i, 0))],
        out_specs=pl.BlockSpec(memory_space=pl.ANY),
        scratch_shapes=[pltpu.VMEM((tm_moe * nsub, LANES), F32),
                        pltpu.SemaphoreType.DMA(()), pltpu.SemaphoreType.DMA(()),
                        pltpu.SemaphoreType.DMA(())],
        compiler_params=_params(("arbitrary",)),
    )(pos_flat, fill, h2)


def _moe_kernel(te_ref, nv_ref, x_ref, bg_ref, bu_ref, bd_ref, wg_hbm, wu_hbm, wd_hbm,
                y_ref, wf_s, wg_s, wu_s, wd_s, slot_s, sem):
    i = pl.program_id(0)
    nv = nv_ref[0]
    last = te_ref.shape[0] - 1

    def weight_copies(e, slot):
        return [pltpu.make_async_copy(w.at[e], wf_s.at[slot, j], sem.at[slot])
                for j, w in enumerate((wg_hbm, wu_hbm, wd_hbm))]

    @pl.when(i < nv)
    def _():
        e = te_ref[i]

        @pl.when(i == 0)
        def _():
            slot_s[0] = 0
            for c in weight_copies(e, 0):
                c.start()

        @pl.when((i == 0) | (e != te_ref[jnp.maximum(i - 1, 0)]))
        def _():
            slot = slot_s[0]
            for c in weight_copies(e, slot):
                c.wait()
            nxt = lax.while_loop(lambda j: (j < nv) & (te_ref[jnp.minimum(j, last)] == e),
                                 lambda j: j + 1, i + 1)

            @pl.when(nxt < nv)
            def _():
                for c in weight_copies(te_ref[jnp.minimum(nxt, last)], 1 - slot):
                    c.start()

            wg_s[...] = wf_s[slot, 0].astype(BF16)
            wu_s[...] = wf_s[slot, 1].astype(BF16)
            wd_s[...] = wf_s[slot, 2].astype(BF16)
            slot_s[0] = 1 - slot

        xb = _load_token_tiles(x_ref, wg_s.shape[0] // LANES).astype(BF16)
        g = _dot(xb, wg_s[...]) + bg_ref[0]
        u = _dot(xb, wu_s[...]) + bu_ref[0]
        g = jnp.minimum(g, SWIGLU_LIMIT)
        u = jnp.clip(u, -SWIGLU_LIMIT, SWIGLU_LIMIT)
        h = (u + 1.0) * (g * jax.nn.sigmoid(SWIGLU_ALPHA * g))
        _store_token_tiles(y_ref, _dot(h.astype(BF16), wd_s[...]) + bd_ref[0])

    @pl.when(i >= nv)
    def _():
        y_ref[...] = jnp.zeros_like(y_ref)


def _moe(tile_expert, n_valid, xs, w_gate, b_gate, w_up, b_up, w_down, b_down, tm):
    e, d, f = w_gate.shape
    assert d == f
    nsub = d // LANES
    cap = xs.shape[0] // nsub
    n_tiles = cap // tm

    def tile(i, te, nv):
        return jnp.minimum(i, nv[0] - 1)

    exp3 = lambda i, te, nv: (te[tile(i, te, nv)], 0, 0)
    grid_spec = pltpu.PrefetchScalarGridSpec(
        num_scalar_prefetch=2,
        grid=(n_tiles,),
        in_specs=[pl.BlockSpec((tm * nsub, LANES), lambda i, te, nv: (tile(i, te, nv), 0)),
                  pl.BlockSpec((1, 1, f), exp3), pl.BlockSpec((1, 1, f), exp3),
                  pl.BlockSpec((1, 1, d), exp3),
                  pl.BlockSpec(memory_space=pl.ANY), pl.BlockSpec(memory_space=pl.ANY),
                  pl.BlockSpec(memory_space=pl.ANY)],
        out_specs=pl.BlockSpec((tm * nsub, LANES), lambda i, te, nv: (i, 0)),
        scratch_shapes=[pltpu.VMEM((2, 3, d, f), F32),
                        pltpu.VMEM((d, f), BF16), pltpu.VMEM((d, f), BF16), pltpu.VMEM((f, d), BF16),
                        pltpu.SMEM((1,), jnp.int32), pltpu.SemaphoreType.DMA((2,))],
    )
    return pl.pallas_call(
        _moe_kernel,
        out_shape=jax.ShapeDtypeStruct((cap * nsub, LANES), F32),
        grid_spec=grid_spec,
        compiler_params=_params(("arbitrary",), 56),
    )(tile_expert, n_valid, xs, b_gate.reshape(e, 1, f), b_up.reshape(e, 1, f),
      b_down.reshape(e, 1, d), w_gate, w_up, w_down)


def _combine_kernel(pos_ref, posn_ref, w_ref, x1_ref, mod_ref, g_ref, b_ref, y_ref, o_ref,
                    buf, sem, *, tm):
    i = pl.program_id(0)
    n = pl.num_programs(0)
    nsub = x1_ref.shape[1] // LANES

    group = COMBINE_GROUP

    def issue_rows(p_ref, slot, r0):
        for r in range(group):
            for k in range(TOP_K):
                row = r0 + r
                src = pl.ds(pl.multiple_of(p_ref[row * TOP_K + k] * nsub, nsub), nsub)
                dst = pl.ds(pl.multiple_of(row * nsub, nsub), nsub)
                pltpu.make_async_copy(y_ref.at[src], buf.at[slot, k, dst],
                                      sem.at[slot]).start(priority=k % 2)

    def reduce_rows(slot, r0):
        rows = pl.ds(r0, group)
        w = w_ref[rows, :]
        f = jnp.zeros((group, nsub * LANES), F32)
        for k in range(TOP_K):
            yk = jnp.concatenate(
                [buf[slot, k, pl.ds(r0 * nsub + s, group, stride=nsub), :] for s in range(nsub)],
                axis=1)
            f = f + w[:, k:k + 1] * yk
        o_ref[rows, :] = f

    @pl.when(i == 0)
    def _():
        def body(j, c):
            issue_rows(pos_ref, 0, pl.multiple_of(j * group, group))
            return c
        lax.fori_loop(0, tm // group, body, 0)

    slot = i % 2
    for k in range(TOP_K):
        pltpu.make_async_copy(y_ref.at[pl.ds(0, tm * nsub)], buf.at[slot, k], sem.at[slot]).wait()

    @pl.when(i + 1 < n)
    def _():
        def body(j, c):
            r0 = pl.multiple_of(j * group, group)
            issue_rows(posn_ref, 1 - slot, r0)
            reduce_rows(slot, r0)
            return c
        lax.fori_loop(0, tm // group, body, 0)

    @pl.when(i + 1 == n)
    def _():
        def body(j, c):
            reduce_rows(slot, pl.multiple_of(j * group, group))
            return c
        lax.fori_loop(0, tm // group, body, 0)

    o_ref[...] = _layer_norm(DEEPNORM_ALPHA * x1_ref[...] + mod_ref[0, 5:6, :] * o_ref[...],
                             g_ref[...], b_ref[...])


def _combine(pos_flat, w, x1, mod3, g2, b2, y, seq):
    n, d = x1.shape
    tm = min(TM_ROWS, seq)
    nsteps = n // tm
    per_seq = seq // tm
    return pl.pallas_call(
        functools.partial(_combine_kernel, tm=tm),
        out_shape=jax.ShapeDtypeStruct((n, d), F32),
        grid=(nsteps,),
        in_specs=[pl.BlockSpec((tm * TOP_K,), lambda i: (i,), memory_space=pltpu.SMEM),
                  pl.BlockSpec((tm * TOP_K,), lambda i: (jnp.minimum(i + 1, nsteps - 1),),
                               memory_space=pltpu.SMEM),
                  pl.BlockSpec((tm, LANES), lambda i: (i, 0)),
                  pl.BlockSpec((tm, d), lambda i: (i, 0)),
                  pl.BlockSpec((1, 6, d), lambda i: (i // per_seq, 0, 0)),
                  pl.BlockSpec(g2.shape, lambda i: (0, 0)),
                  pl.BlockSpec(b2.shape, lambda i: (0, 0)),
                  pl.BlockSpec(memory_space=pl.ANY)],
        out_specs=pl.BlockSpec((tm, d), lambda i: (i, 0)),
        scratch_shapes=[pltpu.VMEM((2, TOP_K, tm * (d // LANES), LANES), F32),
                        pltpu.SemaphoreType.DMA((2,))],
        compiler_params=_params(("arbitrary",), 32),
    )(pos_flat, pos_flat, w, x1, mod3, g2, b2, y)


def _prep_w_uq(w_uq):
    r = w_uq.shape[0]
    w = w_uq.reshape(r, B_HEADS, B_QK_DIM)
    half = B_ROPE_DIM // 2
    rot = jnp.concatenate([-w[:, :, B_NOPE_DIM + half:], w[:, :, B_NOPE_DIM:B_NOPE_DIM + half]],
                          axis=-1).reshape(r, B_HEADS * B_ROPE_DIM)
    w = jnp.pad(w, ((0, 0), (0, 0), (0, LANES - B_QK_DIM))).reshape(r, B_HEADS * LANES)
    return jnp.concatenate([w, rot], axis=1).astype(BF16)


def _prep_w_ukv(w_ukv):
    r = w_ukv.shape[0]
    w = w_ukv.reshape(r, B_HEADS, B_NOPE_DIM + B_V_DIM)
    wk = jnp.pad(w[:, :, :B_NOPE_DIM], ((0, 0), (0, 0), (0, LANES - B_NOPE_DIM)))
    wv = w[:, :, B_NOPE_DIM:].reshape(r, B_HEADS // 2, 2 * B_V_DIM)
    wv = jnp.pad(wv, ((0, 0), (0, 0), (0, V_PAIR_LANES - 2 * B_V_DIM)))
    return (wk.reshape(r, B_HEADS * LANES).astype(BF16),
            wv.reshape(r, B_HEADS // 2 * V_PAIR_LANES).astype(BF16))


def _rope_tables(positions):
    half = B_ROPE_DIM // 2
    freqs = ROPE_THETA ** (-jnp.arange(half, dtype=F32) / half)
    ang = positions.astype(F32).reshape(1, -1) * freqs[:, None]
    cos, sin = jnp.cos(ang), jnp.sin(ang)
    z = jnp.zeros((B_ROPE_DIM, ang.shape[1]), F32)
    return jnp.concatenate([sin, sin, z, cos, cos, z], 0)


def kernel(x, c, positions, w_ada, b_ada, w_in, rms_q, w_uq, rms_kv, w_ukv, rel_bias,
           w_branch_a, w_branch_b, w_out, ln1_g, ln1_b, w_router, b_router,
           w_gate, b_gate, w_up, b_up, w_down, b_down, ln2_g, ln2_b):
    bsz, seq, d = x.shape
    n_tok = bsz * seq
    assert w_ada.shape[0] == DEPTH == 1
    row = lambda v: v.reshape(1, -1)

    mod3 = _ada(c, w_ada[0], b_ada[0]).reshape(bsz, 6, d)

    qk, va, cq, ckv, kr, ga, gb = _inproj(x, mod3, w_in[0])
    ya = _attn_a(qk, va, _bias_table(rel_bias[0], min(TQ_A, seq)))
    wk, wv = _prep_w_ukv(w_ukv[0])
    qm, km, vm = _mla_proj(cq, ckv, kr, _rope_tables(positions), row(rms_q[0]), row(rms_kv[0]),
                           _prep_w_uq(w_uq[0]), wk, wv)
    yb = _mla_attn(qm, km, vm)

    wr = jnp.pad(w_router[0], ((0, 0), (0, LANES - N_EXPERTS)))
    br = jnp.pad(b_router[0], (0, LANES - N_EXPERTS), constant_values=NEG_INF)
    x1, h2, code, wts, cnt = _merge(x, ya, yb, ga, gb, mod3,
                                    w_branch_a[0].astype(BF16), w_branch_b[0].astype(BF16),
                                    w_out[0].astype(BF16), row(ln1_g[0]), row(ln1_b[0]), wr, row(br))
    wts = wts.reshape(n_tok, LANES)

    tm = TM_MOE
    n_tiles = -(-(n_tok * TOP_K) // tm) + N_EXPERTS
    assert n_tiles <= MAX_TILES_PAD
    pos, tmap, meta = _route(code.reshape(n_tok, LANES), cnt, tm)
    pos_flat = pos[:, :TOP_K].reshape(-1)
    fill = meta[:3].reshape(-1)
    nsub = d // LANES
    xs = _dispatch(pos_flat, fill, h2.reshape(n_tok * nsub, LANES), n_tiles * tm, nsub, tm)
    y = _moe(tmap[:n_tiles, 0], meta[0, :1], xs, w_gate[0], b_gate[0], w_up[0], b_up[0],
             w_down[0], b_down[0], tm)
    out = _combine(pos_flat, wts, x1.reshape(n_tok, d), mod3, row(ln2_g[0]), row(ln2_b[0]), y, seq)
    return out.reshape(bsz, seq, d)
```

```python
import functools

import jax
import jax.numpy as jnp
from jax import lax
from jax.experimental import pallas as pl
from jax.experimental.pallas import tpu as pltpu

F32 = jnp.float32
BF16 = jnp.bfloat16

CHUNK = 64
A_HEADS = 8
A_HEAD_DIM = 64
A_LEFT_CHUNKS = 8
A_MAX_REL = 128
A_WIDTH = A_HEADS * A_HEAD_DIM
B_HEADS = 8
B_NOPE_DIM = 64
B_ROPE_DIM = 32
B_V_DIM = 64
B_Q_LORA = 384
B_KV_LORA = 256
B_QK_DIM = B_NOPE_DIM + B_ROPE_DIM
ROPE_THETA = 10000.0
N_EXPERTS = 32
TOP_K = 4
SWIGLU_LIMIT = 7.0
SWIGLU_ALPHA = 1.702
DEPTH = 1
DEEPNORM_ALPHA = (2.0 * DEPTH) ** 0.25
LN_EPS = 1e-5
RMS_EPS = 1e-6
NEG_INF = -1e30
LOG2_E = 1.4426950408889634

LANES = 128
MXU_COLS = 256
V_PAIR_LANES = MXU_COLS

TM_PROJ = 512
TQ_A = 256
A_QBLOCKS = 8
TQ_B = 512
TM_MERGE = 512
MERGE_SPLIT = 1
TR_ROUTE = 1024
TM_MOE = 512
TM_ROWS = 256
COMBINE_GROUP = 8
MAX_TILES_PAD = 512


def _params(sem, vmem_mb=None):
    return pltpu.CompilerParams(
        dimension_semantics=sem,
        vmem_limit_bytes=None if vmem_mb is None else vmem_mb << 20)


def _dot(a, b):
    return jnp.dot(a, b, preferred_element_type=F32)


def _dot_nt(a, b):
    return lax.dot_general(a, b, (((1,), (1,)), ((), ())), preferred_element_type=F32)


def _store_token_tiles(ref, x):
    rows, d = x.shape
    nsub = d // LANES
    for s in range(nsub):
        ref[pl.ds(s, rows, stride=nsub), :] = x[:, s * LANES:(s + 1) * LANES]


def _load_token_tiles(ref, nsub):
    rows = ref.shape[0] // nsub
    return jnp.concatenate([ref[pl.ds(s, rows, stride=nsub), :] for s in range(nsub)], axis=1)


def _layer_norm(x, g, b):
    mu = jnp.mean(x, axis=-1, keepdims=True)
    xc = x - mu
    var = jnp.mean(xc * xc, axis=-1, keepdims=True)
    return xc * lax.rsqrt(var + LN_EPS) * g + b


def _ada_kernel(c_ref, w_ref, b_ref, o_ref):
    c = c_ref[...]
    sc = c * jax.nn.sigmoid(c)
    o_ref[...] = jnp.dot(sc, w_ref[...], preferred_element_type=F32,
                         precision=lax.Precision.HIGHEST) + b_ref[...]


def _ada(c, w_ada, b_ada):
    bsz, d = c.shape
    m = w_ada.shape[1]
    tn = 1024
    return pl.pallas_call(
        _ada_kernel,
        out_shape=jax.ShapeDtypeStruct((bsz, m), F32),
        grid=(m // tn,),
        in_specs=[pl.BlockSpec((bsz, d), lambda j: (0, 0)),
                  pl.BlockSpec((d, tn), lambda j: (0, j)),
                  pl.BlockSpec((1, tn), lambda j: (0, j))],
        out_specs=pl.BlockSpec((bsz, tn), lambda j: (0, j)),
        compiler_params=_params(("arbitrary",)),
    )(c, w_ada, b_ada.reshape(1, m))


def _inproj_kernel(x_ref, mod_ref, win_hbm, qk_ref, va_ref, cq_ref, ckv_ref, kr_ref, ga_ref, gb_ref,
                   wf_s, wqk_ref, wv_ref, wcq_ref, wckv_ref, wkr_ref, wga_ref, wgb_ref, sem):
    @pl.when((pl.program_id(0) == 0) & (pl.program_id(1) == 0))
    def _():
        copy = pltpu.make_async_copy(win_hbm.at[0], wf_s, sem)
        copy.start()
        copy.wait()
        d = wf_s.shape[0]
        off = 0
        for dst, n in ((wqk_ref, 2 * A_WIDTH), (wv_ref, A_WIDTH), (wcq_ref, B_Q_LORA),
                       (wckv_ref, B_KV_LORA), (None, B_ROPE_DIM), (wga_ref, d), (wgb_ref, d)):
            if dst is not None:
                dst[...] = wf_s[:, off:off + n].astype(BF16)
            else:
                kr = wf_s[:, off:off + n]
                wkr_ref[...] = jnp.concatenate(
                    [jnp.zeros((d, B_NOPE_DIM), F32), kr,
                     jnp.zeros((d, LANES - B_NOPE_DIM - n), F32)], axis=1).astype(BF16)
            off += n
        wqk_ref[:, :A_WIDTH] = (wf_s[:, :A_WIDTH] * (A_HEAD_DIM ** -0.5 * LOG2_E)).astype(BF16)

    m = mod_ref[0]
    h = (x_ref[0] * (1.0 + m[1:2]) + m[0:1]).astype(BF16)
    for w_ref, ref in ((wqk_ref, qk_ref), (wcq_ref, cq_ref), (wckv_ref, ckv_ref),
                       (wkr_ref, kr_ref), (wga_ref, ga_ref), (wgb_ref, gb_ref)):
        ref[0] = _dot(h, w_ref[...]).astype(ref.dtype)
    v = _dot(h, wv_ref[...]).astype(BF16)
    ones_blk = jnp.ones((v.shape[0], LANES), BF16)
    parts = []
    for hp in range(A_WIDTH // LANES):
        parts += [v[:, hp * LANES:(hp + 1) * LANES], ones_blk]
    va_ref[0] = jnp.concatenate(parts, axis=1)


def _inproj(x, mod3, w_in):
    bsz, seq, d = x.shape
    tm = min(TM_PROJ, seq)
    widths = (2 * A_WIDTH, A_WIDTH // LANES * V_PAIR_LANES, B_Q_LORA, B_KV_LORA, LANES, d, d)
    dtypes = (BF16, BF16, BF16, BF16, F32, BF16, BF16)
    w_widths = (2 * A_WIDTH, A_WIDTH, B_Q_LORA, B_KV_LORA, LANES, d, d)
    assert w_in.shape[0] == 1 and w_in.shape[2] == sum(w_widths) - LANES + B_ROPE_DIM
    return pl.pallas_call(
        _inproj_kernel,
        out_shape=[jax.ShapeDtypeStruct((bsz, seq, n), dt) for n, dt in zip(widths, dtypes)],
        grid=(bsz, seq // tm),
        in_specs=[pl.BlockSpec((1, tm, d), lambda b, i: (b, i, 0)),
                  pl.BlockSpec((1, 6, d), lambda b, i: (b, 0, 0)),
                  pl.BlockSpec(memory_space=pl.ANY)],
        out_specs=[pl.BlockSpec((1, tm, n), lambda b, i: (b, i, 0)) for n in widths],
        scratch_shapes=[pltpu.VMEM(w_in.shape[1:], F32)]
                       + [pltpu.VMEM((d, n), BF16) for n in w_widths]
                       + [pltpu.SemaphoreType.DMA(())],
        compiler_params=_params(("arbitrary", "arbitrary"), 56),
    )(x, mod3, w_in)


def _attn_a_kernel(q_ref, k_ref, v_ref, bias_ref, o_ref, *, tq, nd):
    band = nd * tq
    lane = lax.broadcasted_iota(jnp.int32, (1, LANES), 1)
    for sub in range(A_QBLOCKS):
        i = pl.program_id(2) * A_QBLOCKS + sub
        start = pl.multiple_of(jnp.maximum(i - (nd - 1), 0) * tq, tq)
        variant = jnp.minimum(i, nd - 1)
        q = q_ref[0, sub * tq:(sub + 1) * tq, :]
        k = k_ref[0, pl.ds(start, band), :]
        v = v_ref[0, pl.ds(start, band), :]
        zero = jnp.zeros_like(k)
        kk = jnp.concatenate([jnp.where(lane < A_HEAD_DIM, k, zero),
                              jnp.where(lane >= A_HEAD_DIM, k, zero)], axis=0)
        s = _dot_nt(q, kk)
        ps = []
        for hh in range(2):
            sh = s[:, hh * band:(hh + 1) * band] + bias_ref[hh, variant]
            m = sh.max(axis=-1, keepdims=True)
            ps.append(jnp.exp2(sh - m).astype(BF16))
        acc = _dot(jnp.concatenate(ps, axis=0), v)
        out = acc[:, :LANES] / acc[:, LANES:]
        o_ref[0, sub * tq:(sub + 1) * tq, :] = jnp.where(
            lane < A_HEAD_DIM, out[:tq], out[tq:]).astype(o_ref.dtype)


def _attn_a(qk, va, bias_tab):
    bsz, seq, _ = qk.shape
    tq = min(TQ_A, seq)
    nd = bias_tab.shape[1]
    assert seq >= nd * tq
    ncb = A_WIDTH // LANES
    return pl.pallas_call(
        functools.partial(_attn_a_kernel, tq=tq, nd=nd),
        out_shape=jax.ShapeDtypeStruct((bsz, seq, A_WIDTH), BF16),
        grid=(ncb, bsz, seq // (A_QBLOCKS * tq)),
        in_specs=[pl.BlockSpec((1, A_QBLOCKS * tq, LANES), lambda hp, b, i: (b, i, hp)),
                  pl.BlockSpec((1, seq, LANES), lambda hp, b, i: (b, 0, ncb + hp)),
                  pl.BlockSpec((1, seq, V_PAIR_LANES), lambda hp, b, i: (b, 0, hp)),
                  pl.BlockSpec((2, nd, tq, nd * tq), lambda hp, b, i: (hp, 0, 0, 0))],
        out_specs=pl.BlockSpec((1, A_QBLOCKS * tq, LANES), lambda hp, b, i: (b, i, hp)),
        compiler_params=_params(("arbitrary",) * 3, 40),
    )(qk, qk, va, bias_tab)


def _bias_table(rel_bias, tq):
    nd = (A_LEFT_CHUNKS * CHUNK + tq - 1) // tq + 1
    width = nd * tq
    period = width + tq
    u = jnp.arange(period)
    dist = jnp.where(u < width, (nd - 1) * tq - u, (nd - 1) * tq + period - u)
    vec = rel_bias[:, jnp.clip(dist, -A_MAX_REL, A_MAX_REL) + A_MAX_REL].astype(F32)
    toep = jnp.tile(vec, (1, tq))[:, :tq * (period - 1)].reshape(-1, tq, period - 1)[:, :, :width]
    qi = jnp.arange(tq)[:, None]
    x = jnp.arange(width)[None, :]
    dchunk = qi // CHUNK - x // CHUNK + (nd - 1) * (tq // CHUNK)
    valid = (dchunk >= 0) & (dchunk <= A_LEFT_CHUNKS)
    table = jnp.where(valid[None], toep * LOG2_E, NEG_INF)
    variants = []
    for v in range(nd):
        cut = (nd - 1 - v) * tq
        variants.append(jnp.pad(table[:, :, cut:], ((0, 0), (0, 0), (0, cut)),
                                constant_values=NEG_INF))
    return jnp.stack(variants, axis=1)


def _mla_proj_kernel(cq_ref, ckv_ref, kr_ref, tab_ref, rq_ref, rkv_ref,
                     wq_ref, wk_ref, wv_ref, q_out, k_out, v_out):
    def rms(x, g):
        xf = x.astype(F32)
        ms = jnp.mean(xf * xf, axis=-1, keepdims=True)
        return (xf * lax.rsqrt(ms + RMS_EPS) * g).astype(BF16)

    cqn = rms(cq_ref[0], rq_ref[...])
    ckvn = rms(ckv_ref[0], rkv_ref[...])
    q = _dot(cqn, wq_ref[...])
    kn = _dot(ckvn, wk_ref[...])
    vlane = lax.broadcasted_iota(jnp.int32, (1, v_out.shape[-1]), 1)
    ones_lane = jnp.where(vlane % V_PAIR_LANES >= 2 * B_V_DIM, 1.0, 0.0)
    v_out[0] = (_dot(ckvn, wv_ref[...]) + ones_lane).astype(v_out.dtype)
    tab = tab_ref[...].T
    tlane = lax.broadcasted_iota(jnp.int32, tab.shape, 1)
    rope_end = B_NOPE_DIM + B_ROPE_DIM
    in_rope = (tlane >= B_NOPE_DIM) & (tlane < rope_end)
    cos_t = jnp.where(tlane < B_NOPE_DIM, 1.0, jnp.where(in_rope, tab, 0.0))
    sin_t = jnp.where(in_rope, pltpu.roll(tab, B_NOPE_DIM, 1), 0.0)
    first = tlane < B_NOPE_DIM + B_ROPE_DIM // 2
    sin_a = jnp.where(first, -sin_t, 0.0)
    sin_b = jnp.where(first, 0.0, sin_t)

    def rope(x, tables):
        c, sa, sb = tables
        return (x * c + pltpu.roll(x, LANES - B_ROPE_DIM // 2, 1) * sa
                + pltpu.roll(x, B_ROPE_DIM // 2, 1) * sb)

    kpe = rope(kr_ref[0], (cos_t, sin_a, sin_b))
    scale = B_QK_DIM ** -0.5 * LOG2_E
    cos_q, sin_q = cos_t * scale, sin_t * scale
    nq = B_HEADS * LANES
    per_block = LANES // B_ROPE_DIM
    for h in range(B_HEADS):
        sl = slice(h * LANES, (h + 1) * LANES)
        blk = nq + (h // per_block) * LANES
        partner = q[:, blk:blk + LANES]
        shift = (B_NOPE_DIM - B_ROPE_DIM * (h % per_block)) % LANES
        if shift:
            partner = pltpu.roll(partner, shift, 1)
        q_out[0, :, sl] = (q[:, sl] * cos_q + partner * sin_q).astype(q_out.dtype)
        k_out[0, :, sl] = (kn[:, sl] + kpe).astype(k_out.dtype)


def _mla_proj(cq, ckv, kr, rope_tab, rms_q, rms_kv, wq, wk, wv):
    bsz, seq, _ = cq.shape
    tm = min(TM_PROJ, seq)
    tok = lambda n: pl.BlockSpec((1, tm, n), lambda b, i: (b, i, 0))
    full = lambda a: pl.BlockSpec(a.shape, lambda b, i: (0,) * a.ndim)
    return pl.pallas_call(
        _mla_proj_kernel,
        out_shape=[jax.ShapeDtypeStruct((bsz, seq, B_HEADS * LANES), BF16),
                   jax.ShapeDtypeStruct((bsz, seq, B_HEADS * LANES), BF16),
                   jax.ShapeDtypeStruct((bsz, seq, wv.shape[1]), BF16)],
        grid=(bsz, seq // tm),
        in_specs=[tok(B_Q_LORA), tok(B_KV_LORA), tok(LANES),
                  pl.BlockSpec((LANES, tm), lambda b, i: (0, b * (seq // tm) + i)),
                  full(rms_q), full(rms_kv), full(wq), full(wk), full(wv)],
        out_specs=[tok(B_HEADS * LANES), tok(B_HEADS * LANES), tok(wv.shape[1])],
        compiler_params=_params(("arbitrary", "arbitrary"), 32),
    )(cq, ckv, kr, rope_tab, rms_q, rms_kv, wq, wk, wv)


def _mla_attn_kernel(q_ref, k_ref, v_ref, o_ref, s_a, s_b, m_s, acc_s, *, t):
    i = pl.program_id(2)
    lane = lax.broadcasted_iota(jnp.int32, (1, LANES), 1)
    m_s[...] = jnp.full(m_s.shape, NEG_INF, F32)
    acc_s[...] = jnp.zeros(acc_s.shape, F32)

    def produce(kj, s_ref):
        start = pl.multiple_of(kj * t, t)
        for hh in range(2):
            q = q_ref[0, :, hh * LANES:(hh + 1) * LANES]
            k = k_ref[0, pl.ds(start, t), hh * LANES:(hh + 1) * LANES]
            s_ref[hh] = _dot_nt(q, k)

    def consume(kj, s_ref, masked):
        start = pl.multiple_of(kj * t, t)
        v = v_ref[0, pl.ds(start, t), :]
        for hh in range(2):
            s = s_ref[hh]
            if masked:
                row_chunk = lax.broadcasted_iota(jnp.int32, (t, t), 0) // CHUNK
                col_chunk = lax.broadcasted_iota(jnp.int32, (t, t), 1) // CHUNK
                s = jnp.where(col_chunk <= row_chunk, s, NEG_INF)
            m_old = m_s[hh]
            m_new = jnp.maximum(m_old, jnp.broadcast_to(s.max(axis=-1, keepdims=True), m_old.shape))
            p = jnp.exp2(s - jnp.tile(m_new, (1, t // LANES))).astype(BF16)
            alpha = jnp.exp2(m_old - m_new)
            acc_s[hh] = jnp.tile(alpha, (1, V_PAIR_LANES // LANES)) * acc_s[hh] + _dot(p, v)
            m_s[hh] = m_new

    def body(pair, c):
        kj = 2 * pair
        produce(kj + 1, s_b)
        consume(kj, s_a, False)
        produce(kj + 2, s_a)
        consume(kj + 1, s_b, False)
        return c

    produce(0, s_a)
    lax.fori_loop(0, i // 2, body, 0)

    @pl.when(i % 2 == 0)
    def _():
        consume(i, s_a, True)

    @pl.when(i % 2 == 1)
    def _():
        produce(i, s_b)
        consume(i - 1, s_a, False)
        consume(i, s_b, True)

    outs = []
    for hh in range(2):
        acc = acc_s[hh]
        outs.append(acc[:, :LANES] / acc[:, LANES:])
    o_ref[0] = jnp.where(lane < B_V_DIM, outs[0], outs[1]).astype(o_ref.dtype)


def _mla_attn(qm, km, vm):
    bsz, seq, _ = qm.shape
    t = min(TQ_B, seq)
    return pl.pallas_call(
        functools.partial(_mla_attn_kernel, t=t),
        out_shape=jax.ShapeDtypeStruct((bsz, seq, B_HEADS * B_V_DIM), BF16),
        grid=(bsz, B_HEADS // 2, seq // t),
        in_specs=[pl.BlockSpec((1, t, 2 * LANES), lambda b, hp, i: (b, i, hp)),
                  pl.BlockSpec((1, seq, 2 * LANES), lambda b, hp, i: (b, 0, hp)),
                  pl.BlockSpec((1, seq, V_PAIR_LANES), lambda b, hp, i: (b, 0, hp))],
        out_specs=pl.BlockSpec((1, t, LANES), lambda b, hp, i: (b, i, hp)),
        scratch_shapes=[pltpu.VMEM((2, t, t), F32), pltpu.VMEM((2, t, t), F32),
                        pltpu.VMEM((2, t, LANES), F32),
                        pltpu.VMEM((2, t, V_PAIR_LANES), F32)],
        compiler_params=_params(("arbitrary",) * 3, 40),
    )(qm, km, vm)


def _merge_kernel(x_ref, ya_ref, yb_ref, ga_ref, gb_ref, mod_ref, wa_ref, wb_ref, wo_ref,
                  g1_ref, b1_ref, wrh_ref, wrl_ref, br_ref, x1_ref, h2_ref, code_ref, wts_ref,
                  cnt_ref):
    @pl.when((pl.program_id(0) == 0) & (pl.program_id(1) == 0))
    def _():
        cnt_ref[...] = jnp.zeros_like(cnt_ref)

    tm = x_ref.shape[1]
    half = tm // MERGE_SPLIT
    for part in range(MERGE_SPLIT):
        _merge_rows(slice(part * half, (part + 1) * half), x_ref, ya_ref, yb_ref, ga_ref, gb_ref,
                    mod_ref, wa_ref, wb_ref, wo_ref, g1_ref, b1_ref, wrh_ref, wrl_ref, br_ref,
                    x1_ref, h2_ref, code_ref, wts_ref, cnt_ref)


def _merge_rows(rows, x_ref, ya_ref, yb_ref, ga_ref, gb_ref, mod_ref, wa_ref, wb_ref, wo_ref,
                g1_ref, b1_ref, wrh_ref, wrl_ref, br_ref, x1_ref, h2_ref, code_ref, wts_ref,
                cnt_ref):
    m = mod_ref[0]
    a = _dot(ya_ref[0, rows, :], wa_ref[...])
    b = _dot(yb_ref[0, rows, :], wb_ref[...])
    merged = (jax.nn.sigmoid(ga_ref[0, rows, :].astype(F32)) * a
              + jax.nn.sigmoid(gb_ref[0, rows, :].astype(F32)) * b)
    o = _dot(merged.astype(BF16), wo_ref[...])
    x1 = _layer_norm(DEEPNORM_ALPHA * x_ref[0, rows, :] + m[2:3] * o, g1_ref[...], b1_ref[...])
    h2 = x1 * (1.0 + m[4:5]) + m[3:4]
    x1_ref[0, rows, :] = x1
    nsub = h2.shape[1] // LANES
    _store_token_tiles(h2_ref.at[0, pl.ds(rows.start * nsub, (rows.stop - rows.start) * nsub)], h2)
    h_hi = h2.astype(BF16)
    h_lo = (h2 - h_hi.astype(F32)).astype(BF16)
    work = (_dot(h_hi, wrh_ref[...]) + _dot(h_hi, wrl_ref[...]) + _dot(h_lo, wrh_ref[...])
            + br_ref[...])
    lane = lax.broadcasted_iota(jnp.int32, work.shape, 1)
    code = jnp.zeros(work.shape, F32)
    vals = []
    for k in range(TOP_K):
        top = work.max(axis=-1, keepdims=True)
        idx = jnp.where(work == top, lane, LANES).min(axis=-1, keepdims=True)
        sel = lane == idx
        work = jnp.where(sel, -jnp.inf, work)
        code = jnp.where(sel, k + 1.0, code)
        vals.append(top)
    den = sum(jnp.exp(v - vals[0]) for v in vals)
    wts = jnp.zeros(work.shape, F32)
    for k in range(TOP_K):
        wts = jnp.where(lane == k, jnp.exp(vals[k] - vals[0]) / den, wts)
    code_ref[0, rows, :] = code
    wts_ref[0, rows, :] = wts
    picked = jnp.where(code > 0.0, 1.0, 0.0)
    cnt_ref[...] += jnp.broadcast_to(picked.sum(axis=0, keepdims=True), cnt_ref.shape)


def _merge(x, ya, yb, ga, gb, mod3, wa, wb, wo, g1, b1, wr, br):
    wr_hi = wr.astype(BF16)
    wr_lo = (wr - wr_hi.astype(F32)).astype(BF16)
    bsz, seq, d = x.shape
    tm = min(TM_MERGE, seq)
    tok = lambda n: pl.BlockSpec((1, tm, n), lambda b, i: (b, i, 0))
    full = lambda a: pl.BlockSpec(a.shape, lambda b, i: (0,) * a.ndim)
    return pl.pallas_call(
        _merge_kernel,
        out_shape=[jax.ShapeDtypeStruct((bsz, seq, d), F32),
                   jax.ShapeDtypeStruct((bsz, seq * (d // LANES), LANES), F32),
                   jax.ShapeDtypeStruct((bsz, seq, LANES), F32),
                   jax.ShapeDtypeStruct((bsz, seq, LANES), F32),
                   jax.ShapeDtypeStruct((8, LANES), F32)],
        grid=(bsz, seq // tm),
        in_specs=[tok(d), tok(A_WIDTH), tok(B_HEADS * B_V_DIM), tok(d), tok(d),
                  pl.BlockSpec((1, 6, d), lambda b, i: (b, 0, 0)),
                  full(wa), full(wb), full(wo), full(g1), full(b1), full(wr_hi), full(wr_lo),
                  full(br)],
        out_specs=[tok(d),
                   pl.BlockSpec((1, tm * (d // LANES), LANES), lambda b, i: (b, i, 0)),
                   tok(LANES), tok(LANES), pl.BlockSpec((8, LANES), lambda b, i: (0, 0))],
        compiler_params=_params(("arbitrary", "arbitrary"), 48),
    )(x, ya, yb, ga, gb, mod3, wa, wb, wo, g1, b1, wr_hi, wr_lo, br)


def _route_kernel(code_ref, cnt_ref, pos_ref, tmap_ref, nv_ref, carry_s, pstart_s, *, tr, tm):
    i = pl.program_id(0)
    lane = lax.broadcasted_iota(jnp.int32, (tr, LANES), 1)
    code = code_ref[...]
    picked = jnp.where(code > 0.0, 1.0, 0.0)
    tile_count = jnp.broadcast_to(picked.sum(axis=0, keepdims=True), (8, LANES))

    @pl.when(i == 0)
    def _():
        lane8 = lax.broadcasted_iota(jnp.int32, (8, LANES), 1)
        padded = jnp.floor((cnt_ref[...] + (tm - 1)) * (1.0 / tm)) * tm
        cum = padded
        shift = 1
        while shift < N_EXPERTS:
            cum = cum + jnp.where(lane8 >= shift, pltpu.roll(cum, shift, 1), 0.0)
            shift *= 2
        pstart_s[...] = cum - padded
        carry_s[...] = jnp.zeros_like(carry_s)
        nt = tmap_ref.shape[0]
        tile_start = (lax.broadcasted_iota(jnp.int32, (nt, LANES), 0) * tm).astype(F32)
        lane_t = lax.broadcasted_iota(jnp.int32, (nt, LANES), 1)
        done = jnp.where((lane_t < N_EXPERTS) & (cum[0:1] <= tile_start), 1.0, 0.0)
        te = jnp.minimum(done.sum(axis=-1, keepdims=True), N_EXPERTS - 1.0)
        tmap_ref[...] = jnp.broadcast_to(te, (nt, LANES)).astype(jnp.int32)
        total = jnp.where(lane8 == N_EXPERTS - 1, cum, 0.0).sum(axis=-1, keepdims=True)
        row8 = lax.broadcasted_iota(jnp.int32, (8, LANES), 0)
        meta = jnp.where(row8 == 0, jnp.broadcast_to(total * (1.0 / tm), (8, LANES)),
                         jnp.where(row8 == 1, cum - padded + cnt_ref[...], cum))
        nv_ref[...] = meta.astype(jnp.int32)

    r = lax.broadcasted_iota(jnp.int32, (tr, tr), 0)
    c = lax.broadcasted_iota(jnp.int32, (tr, tr), 1)
    tri = jnp.where(c < r, 1.0, 0.0).astype(BF16)
    base = _dot(tri, picked.astype(BF16)) + (carry_s[0:1] + pstart_s[0:1])
    carry_s[...] += tile_count
    pos = jnp.zeros((tr, LANES), F32)
    for k in range(TOP_K):
        pk = jnp.where(code == k + 1.0, base, 0.0).sum(axis=-1, keepdims=True)
        pos = jnp.where(lane == k, pk, pos)
    pos_ref[...] = pos.astype(jnp.int32)


def _route(code, cnt, tm):
    n = code.shape[0]
    tr = min(TR_ROUTE, n)
    return pl.pallas_call(
        functools.partial(_route_kernel, tr=tr, tm=tm),
        out_shape=[jax.ShapeDtypeStruct((n, LANES), jnp.int32),
                   jax.ShapeDtypeStruct((MAX_TILES_PAD, LANES), jnp.int32),
                   jax.ShapeDtypeStruct((8, LANES), jnp.int32)],
        grid=(n // tr,),
        in_specs=[pl.BlockSpec((tr, LANES), lambda i: (i, 0)),
                  pl.BlockSpec((8, LANES), lambda i: (0, 0))],
        out_specs=[pl.BlockSpec((tr, LANES), lambda i: (i, 0)),
                   pl.BlockSpec((MAX_TILES_PAD, LANES), lambda i: (0, 0)),
                   pl.BlockSpec((8, LANES), lambda i: (0, 0))],
        scratch_shapes=[pltpu.VMEM((8, LANES), F32)] * 2,
        compiler_params=_params(("arbitrary",)),
    )(code, cnt)


def _dispatch_kernel(pos_ref, fill_ref, h_ref, xs_ref, ztile_s, sem, fill_sem, tail_sem,
                     *, tm, nsub, tm_moe):
    def tile(ref, r):
        return ref.at[pl.ds(pl.multiple_of(r * nsub, nsub), nsub)]

    @pl.when(pl.program_id(0) == 0)
    def _():
        ztile_s[...] = jnp.zeros_like(ztile_s)

        def tail_copy(t):
            rows = tm_moe * nsub
            return pltpu.make_async_copy(
                ztile_s, xs_ref.at[pl.ds(pl.multiple_of(t * rows, rows), rows)], tail_sem)

        sizes = [1 << b for b in reversed(range(tm_moe.bit_length() - 1))]

        def fill_pieces(e, act):
            lo, hi = fill_ref[LANES + e], fill_ref[2 * LANES + e]
            at = lo
            for size in sizes:
                has = ((hi - lo) & size) != 0
                copy = pltpu.make_async_copy(
                    ztile_s.at[pl.ds(0, size * nsub)],
                    xs_ref.at[pl.ds(pl.multiple_of(at * nsub, nsub), size * nsub)], fill_sem)
                pl.when(has)(functools.partial(act, copy))
                at = at + jnp.where(has, size, 0)

        def fill_start(e, c):
            fill_pieces(e, lambda copy: copy.start())
            return c

        def fill_wait(e, c):
            fill_pieces(e, lambda copy: copy.wait())
            return c

        n_tiles = xs_ref.shape[0] // (tm_moe * nsub)

        def tail_start(t, c):
            tail_copy(t).start()
            return c

        def tail_drain(_, c):
            tail_copy(0).wait()
            return c

        lax.fori_loop(0, N_EXPERTS, fill_start, 0)
        lax.fori_loop(fill_ref[0], n_tiles, tail_start, 0)
        lax.fori_loop(0, N_EXPERTS, fill_wait, 0)
        lax.fori_loop(fill_ref[0], n_tiles, tail_drain, 0)

    def issue(j, c):
        for r in range(COMBINE_GROUP):
            row = pl.multiple_of(j * COMBINE_GROUP, COMBINE_GROUP) + r
            for k in range(TOP_K):
                pltpu.make_async_copy(tile(h_ref, row), tile(xs_ref, pos_ref[row * TOP_K + k]),
                                      sem).start(priority=k % 2)
        return c

    lax.fori_loop(0, tm // COMBINE_GROUP, issue, 0)
    for _ in range(TOP_K):
        pltpu.make_async_copy(h_ref, xs_ref.at[pl.ds(0, tm * nsub)], sem).wait()


def _dispatch(pos_flat, fill, h2, cap, nsub, tm_moe):
    n = h2.shape[0] // nsub
    tm = min(TM_ROWS, n)
    return pl.pallas_call(
        functools.partial(_dispatch_kernel, tm=tm, nsub=nsub, tm_moe=tm_moe),
        out_shape=jax.ShapeDtypeStruct((cap * nsub, LANES), F32),
        grid=(n // tm,),
        in_specs=[pl.BlockSpec((tm * TOP_K,), lambda i: (i,), memory_space=pltpu.SMEM),
                  pl.BlockSpec(fill.shape, lambda i: (0,), memory_space=pltpu.SMEM),
                  pl.BlockSpec((tm * nsub, LANES), lambda i: (i, 0))],
        out_specs=pl.BlockSpec(memory_space=pl.ANY),
        scratch_shapes=[pltpu.VMEM((tm_moe * nsub, LANES), F32),
                        pltpu.SemaphoreType.DMA(()), pltpu.SemaphoreType.DMA(()),
                        pltpu.SemaphoreType.DMA(())],
        compiler_params=_params(("arbitrary",)),
    )(pos_flat, fill, h2)


def _moe_kernel(te_ref, nv_ref, x_ref, bg_ref, bu_ref, bd_ref, wg_hbm, wu_hbm, wd_hbm,
                y_ref, wf_s, wg_s, wu_s, wd_s, slot_s, sem):
    i = pl.program_id(0)
    nv = nv_ref[0]
    last = te_ref.shape[0] - 1

    def weight_copies(e, slot):
        return [pltpu.make_async_copy(w.at[e], wf_s.at[slot, j], sem.at[slot])
                for j, w in enumerate((wg_hbm, wu_hbm, wd_hbm))]

    @pl.when(i < nv)
    def _():
        e = te_ref[i]

        @pl.when(i == 0)
        def _():
            slot_s[0] = 0
            for c in weight_copies(e, 0):
                c.start()

        @pl.when((i == 0) | (e != te_ref[jnp.maximum(i - 1, 0)]))
        def _():
            slot = slot_s[0]
            for c in weight_copies(e, slot):
                c.wait()
            nxt = lax.while_loop(lambda j: (j < nv) & (te_ref[jnp.minimum(j, last)] == e),
                                 lambda j: j + 1, i + 1)

            @pl.when(nxt < nv)
            def _():
                for c in weight_copies(te_ref[jnp.minimum(nxt, last)], 1 - slot):
                    c.start()

            wg_s[...] = wf_s[slot, 0].astype(BF16)
            wu_s[...] = wf_s[slot, 1].astype(BF16)
            wd_s[...] = wf_s[slot, 2].astype(BF16)
            slot_s[0] = 1 - slot

        xb = _load_token_tiles(x_ref, wg_s.shape[0] // LANES).astype(BF16)
        g = _dot(xb, wg_s[...]) + bg_ref[0]
        u = _dot(xb, wu_s[...]) + bu_ref[0]
        g = jnp.minimum(g, SWIGLU_LIMIT)
        u = jnp.clip(u, -SWIGLU_LIMIT, SWIGLU_LIMIT)
        h = (u + 1.0) * (g * jax.nn.sigmoid(SWIGLU_ALPHA * g))
        _store_token_tiles(y_ref, _dot(h.astype(BF16), wd_s[...]) + bd_ref[0])

    @pl.when(i >= nv)
    def _():
        y_ref[...] = jnp.zeros_like(y_ref)


def _moe(tile_expert, n_valid, xs, w_gate, b_gate, w_up, b_up, w_down, b_down, tm):
    e, d, f = w_gate.shape
    assert d == f
    nsub = d // LANES
    cap = xs.shape[0] // nsub
    n_tiles = cap // tm

    def tile(i, te, nv):
        return jnp.minimum(i, nv[0] - 1)

    exp3 = lambda i, te, nv: (te[tile(i, te, nv)], 0, 0)
    grid_spec = pltpu.PrefetchScalarGridSpec(
        num_scalar_prefetch=2,
        grid=(n_tiles,),
        in_specs=[pl.BlockSpec((tm * nsub, LANES), lambda i, te, nv: (tile(i, te, nv), 0)),
                  pl.BlockSpec((1, 1, f), exp3), pl.BlockSpec((1, 1, f), exp3),
                  pl.BlockSpec((1, 1, d), exp3),
                  pl.BlockSpec(memory_space=pl.ANY), pl.BlockSpec(memory_space=pl.ANY),
                  pl.BlockSpec(memory_space=pl.ANY)],
        out_specs=pl.BlockSpec((tm * nsub, LANES), lambda i, te, nv: (i, 0)),
        scratch_shapes=[pltpu.VMEM((2, 3, d, f), F32),
                        pltpu.VMEM((d, f), BF16), pltpu.VMEM((d, f), BF16), pltpu.VMEM((f, d), BF16),
                        pltpu.SMEM((1,), jnp.int32), pltpu.SemaphoreType.DMA((2,))],
    )
    return pl.pallas_call(
        _moe_kernel,
        out_shape=jax.ShapeDtypeStruct((cap * nsub, LANES), F32),
        grid_spec=grid_spec,
        compiler_params=_params(("arbitrary",), 56),
    )(tile_expert, n_valid, xs, b_gate.reshape(e, 1, f), b_up.reshape(e, 1, f),
      b_down.reshape(e, 1, d), w_gate, w_up, w_down)


def _combine_kernel(pos_ref, posn_ref, w_ref, x1_ref, mod_ref, g_ref, b_ref, y_ref, o_ref,
                    buf, sem, *, tm):
    i = pl.program_id(0)
    n = pl.num_programs(0)
    nsub = x1_ref.shape[1] // LANES

    group = COMBINE_GROUP

    def issue_rows(p_ref, slot, r0):
        for r in range(group):
            for k in range(TOP_K):
                row = r0 + r
                src = pl.ds(pl.multiple_of(p_ref[row * TOP_K + k] * nsub, nsub), nsub)
                dst = pl.ds(pl.multiple_of(row * nsub, nsub), nsub)
                pltpu.make_async_copy(y_ref.at[src], buf.at[slot, k, dst],
                                      sem.at[slot]).start(priority=k % 2)

    def reduce_rows(slot, r0):
        rows = pl.ds(r0, group)
        w = w_ref[rows, :]
        f = jnp.zeros((group, nsub * LANES), F32)
        for k in range(TOP_K):
            yk = jnp.concatenate(
                [buf[slot, k, pl.ds(r0 * nsub + s, group, stride=nsub), :] for s in range(nsub)],
                axis=1)
            f = f + w[:, k:k + 1] * yk
        o_ref[rows, :] = f

    @pl.when(i == 0)
    def _():
        def body(j, c):
            issue_rows(pos_ref, 0, pl.multiple_of(j * group, group))
            return c
        lax.fori_loop(0, tm // group, body, 0)

    slot = i % 2
    for k in range(TOP_K):
        pltpu.make_async_copy(y_ref.at[pl.ds(0, tm * nsub)], buf.at[slot, k], sem.at[slot]).wait()

    @pl.when(i + 1 < n)
    def _():
        def body(j, c):
            r0 = pl.multiple_of(j * group, group)
            issue_rows(posn_ref, 1 - slot, r0)
            reduce_rows(slot, r0)
            return c
        lax.fori_loop(0, tm // group, body, 0)

    @pl.when(i + 1 == n)
    def _():
        def body(j, c):
            reduce_rows(slot, pl.multiple_of(j * group, group))
            return c
        lax.fori_loop(0, tm // group, body, 0)

    o_ref[...] = _layer_norm(DEEPNORM_ALPHA * x1_ref[...] + mod_ref[0, 5:6, :] * o_ref[...],
                             g_ref[...], b_ref[...])


def _combine(pos_flat, w, x1, mod3, g2, b2, y, seq):
    n, d = x1.shape
    tm = min(TM_ROWS, seq)
    nsteps = n // tm
    per_seq = seq // tm
    return pl.pallas_call(
        functools.partial(_combine_kernel, tm=tm),
        out_shape=jax.ShapeDtypeStruct((n, d), F32),
        grid=(nsteps,),
        in_specs=[pl.BlockSpec((tm * TOP_K,), lambda i: (i,), memory_space=pltpu.SMEM),
                  pl.BlockSpec((tm * TOP_K,), lambda i: (jnp.minimum(i + 1, nsteps - 1),),
                               memory_space=pltpu.SMEM),
                  pl.BlockSpec((tm, LANES), lambda i: (i, 0)),
                  pl.BlockSpec((tm, d), lambda i: (i, 0)),
                  pl.BlockSpec((1, 6, d), lambda i: (i // per_seq, 0, 0)),
                  pl.BlockSpec(g2.shape, lambda i: (0, 0)),
                  pl.BlockSpec(b2.shape, lambda i: (0, 0)),
                  pl.BlockSpec(memory_space=pl.ANY)],
        out_specs=pl.BlockSpec((tm, d), lambda i: (i, 0)),
        scratch_shapes=[pltpu.VMEM((2, TOP_K, tm * (d // LANES), LANES), F32),
                        pltpu.SemaphoreType.DMA((2,))],
        compiler_params=_params(("arbitrary",), 32),
    )(pos_flat, pos_flat, w, x1, mod3, g2, b2, y)


def _prep_w_uq(w_uq):
    r = w_uq.shape[0]
    w = w_uq.reshape(r, B_HEADS, B_QK_DIM)
    half = B_ROPE_DIM // 2
    rot = jnp.concatenate([-w[:, :, B_NOPE_DIM + half:], w[:, :, B_NOPE_DIM:B_NOPE_DIM + half]],
                          axis=-1).reshape(r, B_HEADS * B_ROPE_DIM)
    w = jnp.pad(w, ((0, 0), (0, 0), (0, LANES - B_QK_DIM))).reshape(r, B_HEADS * LANES)
    return jnp.concatenate([w, rot], axis=1).astype(BF16)


def _prep_w_ukv(w_ukv):
    r = w_ukv.shape[0]
    w = w_ukv.reshape(r, B_HEADS, B_NOPE_DIM + B_V_DIM)
    wk = jnp.pad(w[:, :, :B_NOPE_DIM], ((0, 0), (0, 0), (0, LANES - B_NOPE_DIM)))
    wv = w[:, :, B_NOPE_DIM:].reshape(r, B_HEADS // 2, 2 * B_V_DIM)
    wv = jnp.pad(wv, ((0, 0), (0, 0), (0, V_PAIR_LANES - 2 * B_V_DIM)))
    return (wk.reshape(r, B_HEADS * LANES).astype(BF16),
            wv.reshape(r, B_HEADS // 2 * V_PAIR_LANES).astype(BF16))


def _rope_tables(positions):
    half = B_ROPE_DIM // 2
    freqs = ROPE_THETA ** (-jnp.arange(half, dtype=F32) / half)
    ang = positions.astype(F32).reshape(1, -1) * freqs[:, None]
    cos, sin = jnp.cos(ang), jnp.sin(ang)
    z = jnp.zeros((B_ROPE_DIM, ang.shape[1]), F32)
    return jnp.concatenate([sin, sin, z, cos, cos, z], 0)


def kernel(x, c, positions, w_ada, b_ada, w_in, rms_q, w_uq, rms_kv, w_ukv, rel_bias,
           w_branch_a, w_branch_b, w_out, ln1_g, ln1_b, w_router, b_router,
           w_gate, b_gate, w_up, b_up, w_down, b_down, ln2_g, ln2_b):
    bsz, seq, d = x.shape
    n_tok = bsz * seq
    assert w_ada.shape[0] == DEPTH == 1
    row = lambda v: v.reshape(1, -1)

    mod3 = _ada(c, w_ada[0], b_ada[0]).reshape(bsz, 6, d)

    qk, va, cq, ckv, kr, ga, gb = _inproj(x, mod3, w_in)
    ya = _attn_a(qk, va, _bias_table(rel_bias[0], min(TQ_A, seq)))
    wk, wv = _prep_w_ukv(w_ukv[0])
    qm, km, vm = _mla_proj(cq, ckv, kr, _rope_tables(positions), row(rms_q[0]), row(rms_kv[0]),
                           _prep_w_uq(w_uq[0]), wk, wv)
    yb = _mla_attn(qm, km, vm)

    wr = jnp.pad(w_router[0], ((0, 0), (0, LANES - N_EXPERTS)))
    br = jnp.pad(b_router[0], (0, LANES - N_EXPERTS), constant_values=NEG_INF)
    x1, h2, code, wts, cnt = _merge(x, ya, yb, ga, gb, mod3,
                                    w_branch_a[0].astype(BF16), w_branch_b[0].astype(BF16),
                                    w_out[0].astype(BF16), row(ln1_g[0]), row(ln1_b[0]), wr, row(br))
    wts = wts.reshape(n_tok, LANES)

    tm = TM_MOE
    n_tiles = -(-(n_tok * TOP_K) // tm) + N_EXPERTS
    assert n_tiles <= MAX_TILES_PAD
    pos, tmap, meta = _route(code.reshape(n_tok, LANES), cnt, tm)
    pos_flat = pos[:, :TOP_K].reshape(-1)
    fill = meta[:3].reshape(-1)
    nsub = d // LANES
    xs = _dispatch(pos_flat, fill, h2.reshape(n_tok * nsub, LANES), n_tiles * tm, nsub, tm)
    y = _moe(tmap[:n_tiles, 0], meta[0, :1], xs, w_gate[0], b_gate[0], w_up[0], b_up[0],
             w_down[0], b_down[0], tm)
    out = _combine(pos_flat, wts, x1.reshape(n_tok, d), mod3, row(ln2_g[0]), row(ln2_b[0]), y, seq)
    return out.reshape(bsz, seq, d)
```

```python
import functools

import jax
import jax.numpy as jnp
from jax import lax
from jax.experimental import pallas as pl
from jax.experimental.pallas import tpu as pltpu

F32 = jnp.float32
BF16 = jnp.bfloat16

CHUNK = 64
A_HEADS = 8
A_HEAD_DIM = 64
A_LEFT_CHUNKS = 8
A_MAX_REL = 128
A_WIDTH = A_HEADS * A_HEAD_DIM
B_HEADS = 8
B_NOPE_DIM = 64
B_ROPE_DIM = 32
B_V_DIM = 64
B_Q_LORA = 384
B_KV_LORA = 256
B_QK_DIM = B_NOPE_DIM + B_ROPE_DIM
ROPE_THETA = 10000.0
N_EXPERTS = 32
TOP_K = 4
SWIGLU_LIMIT = 7.0
SWIGLU_ALPHA = 1.702
DEPTH = 1
DEEPNORM_ALPHA = (2.0 * DEPTH) ** 0.25
LN_EPS = 1e-5
RMS_EPS = 1e-6
NEG_INF = -1e30
LOG2_E = 1.4426950408889634

LANES = 128
MXU_COLS = 256
V_PAIR_LANES = MXU_COLS

TM_PROJ = 512
TQ_A = 256
A_QBLOCKS = 8
TQ_B = 512
TM_MERGE = 512
MERGE_SPLIT = 1
TR_ROUTE = 1024
TM_MOE = 512
TM_ROWS = 256
COMBINE_GROUP = 8
MAX_TILES_PAD = 512


def _params(sem, vmem_mb=None):
    return pltpu.CompilerParams(
        dimension_semantics=sem,
        vmem_limit_bytes=None if vmem_mb is None else vmem_mb << 20)


def _dot(a, b):
    return jnp.dot(a, b, preferred_element_type=F32)


def _dot_nt(a, b):
    return lax.dot_general(a, b, (((1,), (1,)), ((), ())), preferred_element_type=F32)


def _store_token_tiles(ref, x):
    rows, d = x.shape
    nsub = d // LANES
    for s in range(nsub):
        ref[pl.ds(s, rows, stride=nsub), :] = x[:, s * LANES:(s + 1) * LANES]


def _load_token_tiles(ref, nsub):
    rows = ref.shape[0] // nsub
    return jnp.concatenate([ref[pl.ds(s, rows, stride=nsub), :] for s in range(nsub)], axis=1)


def _layer_norm(x, g, b):
    mu = jnp.mean(x, axis=-1, keepdims=True)
    xc = x - mu
    var = jnp.mean(xc * xc, axis=-1, keepdims=True)
    return xc * lax.rsqrt(var + LN_EPS) * g + b


def _ada_kernel(c_ref, w_ref, b_ref, o_ref):
    c = c_ref[...]
    sc = c * jax.nn.sigmoid(c)
    o_ref[...] = jnp.dot(sc, w_ref[...], preferred_element_type=F32,
                         precision=lax.Precision.HIGHEST) + b_ref[...]


def _ada(c, w_ada, b_ada):
    bsz, d = c.shape
    m = w_ada.shape[1]
    tn = 1024
    return pl.pallas_call(
        _ada_kernel,
        out_shape=jax.ShapeDtypeStruct((bsz, m), F32),
        grid=(m // tn,),
        in_specs=[pl.BlockSpec((bsz, d), lambda j: (0, 0)),
                  pl.BlockSpec((d, tn), lambda j: (0, j)),
                  pl.BlockSpec((1, tn), lambda j: (0, j))],
        out_specs=pl.BlockSpec((bsz, tn), lambda j: (0, j)),
        compiler_params=_params(("arbitrary",)),
    )(c, w_ada, b_ada.reshape(1, m))


def _inproj_kernel(x_ref, mod_ref, win_hbm, qk_ref, va_ref, cq_ref, ckv_ref, kr_ref, ga_ref, gb_ref,
                   wf_s, wqk_ref, wv_ref, wcq_ref, wckv_ref, wkr_ref, wga_ref, wgb_ref, sem):
    @pl.when((pl.program_id(0) == 0) & (pl.program_id(1) == 0))
    def _():
        copy = pltpu.make_async_copy(win_hbm.at[0], wf_s, sem)
        copy.start()
        copy.wait()
        d = wf_s.shape[0]
        off = 0
        for dst, n in ((wqk_ref, 2 * A_WIDTH), (wv_ref, A_WIDTH), (wcq_ref, B_Q_LORA),
                       (wckv_ref, B_KV_LORA), (None, B_ROPE_DIM), (wga_ref, d), (wgb_ref, d)):
            if dst is not None:
                dst[...] = wf_s[:, off:off + n].astype(BF16)
            else:
                kr = wf_s[:, off:off + n]
                wkr_ref[...] = jnp.concatenate(
                    [jnp.zeros((d, B_NOPE_DIM), F32), kr,
                     jnp.zeros((d, LANES - B_NOPE_DIM - n), F32)], axis=1).astype(BF16)
            off += n
        wqk_ref[:, :A_WIDTH] = (wf_s[:, :A_WIDTH] * (A_HEAD_DIM ** -0.5 * LOG2_E)).astype(BF16)

    m = mod_ref[0]
    h = (x_ref[0] * (1.0 + m[1:2]) + m[0:1]).astype(BF16)
    for w_ref, ref in ((wqk_ref, qk_ref), (wcq_ref, cq_ref), (wckv_ref, ckv_ref),
                       (wkr_ref, kr_ref), (wga_ref, ga_ref), (wgb_ref, gb_ref)):
        ref[0] = _dot(h, w_ref[...]).astype(ref.dtype)
    v = _dot(h, wv_ref[...]).astype(BF16)
    ones_blk = jnp.ones((v.shape[0], LANES), BF16)
    parts = []
    for hp in range(A_WIDTH // LANES):
        parts += [v[:, hp * LANES:(hp + 1) * LANES], ones_blk]
    va_ref[0] = jnp.concatenate(parts, axis=1)


def _inproj(x, mod3, w_in):
    bsz, seq, d = x.shape
    tm = min(TM_PROJ, seq)
    widths = (2 * A_WIDTH, A_WIDTH // LANES * V_PAIR_LANES, B_Q_LORA, B_KV_LORA, LANES, d, d)
    dtypes = (BF16, BF16, BF16, BF16, F32, BF16, BF16)
    w_widths = (2 * A_WIDTH, A_WIDTH, B_Q_LORA, B_KV_LORA, LANES, d, d)
    assert w_in.shape[0] == 1 and w_in.shape[2] == sum(w_widths) - LANES + B_ROPE_DIM
    return pl.pallas_call(
        _inproj_kernel,
        out_shape=[jax.ShapeDtypeStruct((bsz, seq, n), dt) for n, dt in zip(widths, dtypes)],
        grid=(bsz, seq // tm),
        in_specs=[pl.BlockSpec((1, tm, d), lambda b, i: (b, i, 0)),
                  pl.BlockSpec((1, 6, d), lambda b, i: (b, 0, 0)),
                  pl.BlockSpec(memory_space=pl.ANY)],
        out_specs=[pl.BlockSpec((1, tm, n), lambda b, i: (b, i, 0)) for n in widths],
        scratch_shapes=[pltpu.VMEM(w_in.shape[1:], F32)]
                       + [pltpu.VMEM((d, n), BF16) for n in w_widths]
                       + [pltpu.SemaphoreType.DMA(())],
        compiler_params=_params(("arbitrary", "arbitrary"), 56),
    )(x, mod3, w_in)


def _attn_a_kernel(q_ref, k_ref, v_ref, bias_ref, o_ref, *, tq, nd):
    band = nd * tq
    lane = lax.broadcasted_iota(jnp.int32, (1, LANES), 1)
    for sub in range(A_QBLOCKS):
        i = pl.program_id(2) * A_QBLOCKS + sub
        start = pl.multiple_of(jnp.maximum(i - (nd - 1), 0) * tq, tq)
        variant = jnp.minimum(i, nd - 1)
        q = q_ref[0, sub * tq:(sub + 1) * tq, :]
        k = k_ref[0, pl.ds(start, band), :]
        v = v_ref[0, pl.ds(start, band), :]
        zero = jnp.zeros_like(k)
        kk = jnp.concatenate([jnp.where(lane < A_HEAD_DIM, k, zero),
                              jnp.where(lane >= A_HEAD_DIM, k, zero)], axis=0)
        s = _dot_nt(q, kk)
        ps = []
        for hh in range(2):
            sh = s[:, hh * band:(hh + 1) * band] + bias_ref[hh, variant]
            m = sh.max(axis=-1, keepdims=True)
            ps.append(jnp.exp2(sh - m).astype(BF16))
        acc = _dot(jnp.concatenate(ps, axis=0), v)
        out = acc[:, :LANES] / acc[:, LANES:]
        o_ref[0, sub * tq:(sub + 1) * tq, :] = jnp.where(
            lane < A_HEAD_DIM, out[:tq], out[tq:]).astype(o_ref.dtype)


def _attn_a(qk, va, bias_tab):
    bsz, seq, _ = qk.shape
    tq = min(TQ_A, seq)
    nd = bias_tab.shape[1]
    assert seq >= nd * tq
    ncb = A_WIDTH // LANES
    return pl.pallas_call(
        functools.partial(_attn_a_kernel, tq=tq, nd=nd),
        out_shape=jax.ShapeDtypeStruct((bsz, seq, A_WIDTH), BF16),
        grid=(ncb, bsz, seq // (A_QBLOCKS * tq)),
        in_specs=[pl.BlockSpec((1, A_QBLOCKS * tq, LANES), lambda hp, b, i: (b, i, hp)),
                  pl.BlockSpec((1, seq, LANES), lambda hp, b, i: (b, 0, ncb + hp)),
                  pl.BlockSpec((1, seq, V_PAIR_LANES), lambda hp, b, i: (b, 0, hp)),
                  pl.BlockSpec((2, nd, tq, nd * tq), lambda hp, b, i: (hp, 0, 0, 0))],
        out_specs=pl.BlockSpec((1, A_QBLOCKS * tq, LANES), lambda hp, b, i: (b, i, hp)),
        compiler_params=_params(("arbitrary",) * 3, 40),
    )(qk, qk, va, bias_tab)


def _bias_table(rel_bias, tq):
    nd = (A_LEFT_CHUNKS * CHUNK + tq - 1) // tq + 1
    width = nd * tq
    period = width + tq
    u = jnp.arange(period)
    dist = jnp.where(u < width, (nd - 1) * tq - u, (nd - 1) * tq + period - u)
    vec = rel_bias[:, jnp.clip(dist, -A_MAX_REL, A_MAX_REL) + A_MAX_REL].astype(F32)
    toep = jnp.tile(vec, (1, tq))[:, :tq * (period - 1)].reshape(-1, tq, period - 1)[:, :, :width]
    qi = jnp.arange(tq)[:, None]
    x = jnp.arange(width)[None, :]
    dchunk = qi // CHUNK - x // CHUNK + (nd - 1) * (tq // CHUNK)
    valid = (dchunk >= 0) & (dchunk <= A_LEFT_CHUNKS)
    table = jnp.where(valid[None], toep * LOG2_E, NEG_INF)
    variants = []
    for v in range(nd):
        cut = (nd - 1 - v) * tq
        variants.append(jnp.pad(table[:, :, cut:], ((0, 0), (0, 0), (0, cut)),
                                constant_values=NEG_INF))
    return jnp.stack(variants, axis=1)


def _mla_proj_kernel(cq_ref, ckv_ref, kr_ref, tab_ref, rq_ref, rkv_ref,
                     wq_ref, wk_ref, wv_ref, q_out, k_out, v_out):
    def rms(x, g):
        xf = x.astype(F32)
        ms = jnp.mean(xf * xf, axis=-1, keepdims=True)
        return (xf * lax.rsqrt(ms + RMS_EPS) * g).astype(BF16)

    cqn = rms(cq_ref[0], rq_ref[...])
    ckvn = rms(ckv_ref[0], rkv_ref[...])
    q = _dot(cqn, wq_ref[...])
    kn = _dot(ckvn, wk_ref[...])
    vlane = lax.broadcasted_iota(jnp.int32, (1, v_out.shape[-1]), 1)
    ones_lane = jnp.where(vlane % V_PAIR_LANES >= 2 * B_V_DIM, 1.0, 0.0)
    v_out[0] = (_dot(ckvn, wv_ref[...]) + ones_lane).astype(v_out.dtype)
    tab = tab_ref[...].T
    tlane = lax.broadcasted_iota(jnp.int32, tab.shape, 1)
    rope_end = B_NOPE_DIM + B_ROPE_DIM
    in_rope = (tlane >= B_NOPE_DIM) & (tlane < rope_end)
    cos_t = jnp.where(tlane < B_NOPE_DIM, 1.0, jnp.where(in_rope, tab, 0.0))
    sin_t = jnp.where(in_rope, pltpu.roll(tab, B_NOPE_DIM, 1), 0.0)
    first = tlane < B_NOPE_DIM + B_ROPE_DIM // 2
    sin_a = jnp.where(first, -sin_t, 0.0)
    sin_b = jnp.where(first, 0.0, sin_t)

    def rope(x, tables):
        c, sa, sb = tables
        return (x * c + pltpu.roll(x, LANES - B_ROPE_DIM // 2, 1) * sa
                + pltpu.roll(x, B_ROPE_DIM // 2, 1) * sb)

    kpe = rope(kr_ref[0], (cos_t, sin_a, sin_b))
    scale = B_QK_DIM ** -0.5 * LOG2_E
    cos_q, sin_q = cos_t * scale, sin_t * scale
    nq = B_HEADS * LANES
    per_block = LANES // B_ROPE_DIM
    for h in range(B_HEADS):
        sl = slice(h * LANES, (h + 1) * LANES)
        blk = nq + (h // per_block) * LANES
        partner = q[:, blk:blk + LANES]
        shift = (B_NOPE_DIM - B_ROPE_DIM * (h % per_block)) % LANES
        if shift:
            partner = pltpu.roll(partner, shift, 1)
        q_out[0, :, sl] = (q[:, sl] * cos_q + partner * sin_q).astype(q_out.dtype)
        k_out[0, :, sl] = (kn[:, sl] + kpe).astype(k_out.dtype)


def _mla_proj(cq, ckv, kr, rope_tab, rms_q, rms_kv, wq, wk, wv):
    bsz, seq, _ = cq.shape
    tm = min(TM_PROJ, seq)
    tok = lambda n: pl.BlockSpec((1, tm, n), lambda b, i: (b, i, 0))
    full = lambda a: pl.BlockSpec(a.shape, lambda b, i: (0,) * a.ndim)
    return pl.pallas_call(
        _mla_proj_kernel,
        out_shape=[jax.ShapeDtypeStruct((bsz, seq, B_HEADS * LANES), BF16),
                   jax.ShapeDtypeStruct((bsz, seq, B_HEADS * LANES), BF16),
                   jax.ShapeDtypeStruct((bsz, seq, wv.shape[1]), BF16)],
        grid=(bsz, seq // tm),
        in_specs=[tok(B_Q_LORA), tok(B_KV_LORA), tok(LANES),
                  pl.BlockSpec((LANES, tm), lambda b, i: (0, b * (seq // tm) + i)),
                  full(rms_q), full(rms_kv), full(wq), full(wk), full(wv)],
        out_specs=[tok(B_HEADS * LANES), tok(B_HEADS * LANES), tok(wv.shape[1])],
        compiler_params=_params(("arbitrary", "arbitrary"), 32),
    )(cq, ckv, kr, rope_tab, rms_q, rms_kv, wq, wk, wv)


def _mla_attn_kernel(q_ref, k_ref, v_ref, o_ref, s_a, s_b, m_s, acc_s, *, t):
    lane = lax.broadcasted_iota(jnp.int32, (1, LANES), 1)
    nq = q_ref.shape[1] // t
    pairs = [(qi, kj) for qi in range(nq) for kj in range(qi + 1)]
    bufs = (s_a, s_b)

    def produce(qi, kj, s_ref):
        for hh in range(2):
            q = q_ref[0, qi * t:(qi + 1) * t, hh * LANES:(hh + 1) * LANES]
            k = k_ref[0, kj * t:(kj + 1) * t, hh * LANES:(hh + 1) * LANES]
            s_ref[hh] = _dot_nt(q, k)

    def consume(qi, kj, s_ref):
        v = v_ref[0, kj * t:(kj + 1) * t, :]
        outs = []
        for hh in range(2):
            s = s_ref[hh]
            if kj == qi:
                row_chunk = lax.broadcasted_iota(jnp.int32, (t, t), 0) // CHUNK
                col_chunk = lax.broadcasted_iota(jnp.int32, (t, t), 1) // CHUNK
                s = jnp.where(col_chunk <= row_chunk, s, NEG_INF)
            row_max = jnp.broadcast_to(s.max(axis=-1, keepdims=True), (t, LANES))
            if kj == 0:
                m_new = row_max
                acc = _dot(jnp.exp2(s - jnp.tile(m_new, (1, t // LANES))).astype(BF16), v)
            else:
                m_old = m_s[hh]
                m_new = jnp.maximum(m_old, row_max)
                p = jnp.exp2(s - jnp.tile(m_new, (1, t // LANES))).astype(BF16)
                alpha = jnp.exp2(m_old - m_new)
                acc = jnp.tile(alpha, (1, V_PAIR_LANES // LANES)) * acc_s[hh] + _dot(p, v)
            if kj == qi:
                outs.append(acc[:, :LANES] / acc[:, LANES:])
            else:
                acc_s[hh] = acc
                m_s[hh] = m_new
        if kj == qi:
            o_ref[0, qi * t:(qi + 1) * t, :] = jnp.where(
                lane < B_V_DIM, outs[0], outs[1]).astype(o_ref.dtype)

    produce(*pairs[0], bufs[0])
    for n, pair in enumerate(pairs):
        if n + 1 < len(pairs):
            produce(*pairs[n + 1], bufs[(n + 1) % 2])
        consume(*pair, bufs[n % 2])


def _mla_attn(qm, km, vm):
    bsz, seq, _ = qm.shape
    t = min(TQ_B, seq)
    return pl.pallas_call(
        functools.partial(_mla_attn_kernel, t=t),
        out_shape=jax.ShapeDtypeStruct((bsz, seq, B_HEADS * B_V_DIM), BF16),
        grid=(bsz, B_HEADS // 2),
        in_specs=[pl.BlockSpec((1, seq, 2 * LANES), lambda b, hp: (b, 0, hp)),
                  pl.BlockSpec((1, seq, 2 * LANES), lambda b, hp: (b, 0, hp)),
                  pl.BlockSpec((1, seq, V_PAIR_LANES), lambda b, hp: (b, 0, hp))],
        out_specs=pl.BlockSpec((1, seq, LANES), lambda b, hp: (b, 0, hp)),
        scratch_shapes=[pltpu.VMEM((2, t, t), F32), pltpu.VMEM((2, t, t), F32),
                        pltpu.VMEM((2, t, LANES), F32),
                        pltpu.VMEM((2, t, V_PAIR_LANES), F32)],
        compiler_params=_params(("arbitrary",) * 2, 40),
    )(qm, km, vm)


def _merge_kernel(x_ref, ya_ref, yb_ref, ga_ref, gb_ref, mod_ref, wa_ref, wb_ref, wo_ref,
                  g1_ref, b1_ref, wrh_ref, wrl_ref, br_ref, x1_ref, h2_ref, code_ref, wts_ref,
                  cnt_ref):
    @pl.when((pl.program_id(0) == 0) & (pl.program_id(1) == 0))
    def _():
        cnt_ref[...] = jnp.zeros_like(cnt_ref)

    tm = x_ref.shape[1]
    half = tm // MERGE_SPLIT
    for part in range(MERGE_SPLIT):
        _merge_rows(slice(part * half, (part + 1) * half), x_ref, ya_ref, yb_ref, ga_ref, gb_ref,
                    mod_ref, wa_ref, wb_ref, wo_ref, g1_ref, b1_ref, wrh_ref, wrl_ref, br_ref,
                    x1_ref, h2_ref, code_ref, wts_ref, cnt_ref)


def _merge_rows(rows, x_ref, ya_ref, yb_ref, ga_ref, gb_ref, mod_ref, wa_ref, wb_ref, wo_ref,
                g1_ref, b1_ref, wrh_ref, wrl_ref, br_ref, x1_ref, h2_ref, code_ref, wts_ref,
                cnt_ref):
    m = mod_ref[0]
    a = _dot(ya_ref[0, rows, :], wa_ref[...])
    b = _dot(yb_ref[0, rows, :], wb_ref[...])
    merged = (jax.nn.sigmoid(ga_ref[0, rows, :].astype(F32)) * a
              + jax.nn.sigmoid(gb_ref[0, rows, :].astype(F32)) * b)
    o = _dot(merged.astype(BF16), wo_ref[...])
    x1 = _layer_norm(DEEPNORM_ALPHA * x_ref[0, rows, :] + m[2:3] * o, g1_ref[...], b1_ref[...])
    h2 = x1 * (1.0 + m[4:5]) + m[3:4]
    x1_ref[0, rows, :] = x1
    nsub = h2.shape[1] // LANES
    _store_token_tiles(h2_ref.at[0, pl.ds(rows.start * nsub, (rows.stop - rows.start) * nsub)], h2)
    h_hi = h2.astype(BF16)
    h_lo = (h2 - h_hi.astype(F32)).astype(BF16)
    work = (_dot(h_hi, wrh_ref[...]) + _dot(h_hi, wrl_ref[...]) + _dot(h_lo, wrh_ref[...])
            + br_ref[...])
    lane = lax.broadcasted_iota(jnp.int32, work.shape, 1)
    code = jnp.zeros(work.shape, F32)
    vals = []
    for k in range(TOP_K):
        top = work.max(axis=-1, keepdims=True)
        idx = jnp.where(work == top, lane, LANES).min(axis=-1, keepdims=True)
        sel = lane == idx
        work = jnp.where(sel, -jnp.inf, work)
        code = jnp.where(sel, k + 1.0, code)
        vals.append(top)
    den = sum(jnp.exp(v - vals[0]) for v in vals)
    wts = jnp.zeros(work.shape, F32)
    for k in range(TOP_K):
        wts = jnp.where(lane == k, jnp.exp(vals[k] - vals[0]) / den, wts)
    code_ref[0, rows, :] = code
    wts_ref[0, rows, :] = wts
    picked = jnp.where(code > 0.0, 1.0, 0.0)
    cnt_ref[...] += jnp.broadcast_to(picked.sum(axis=0, keepdims=True), cnt_ref.shape)


def _merge(x, ya, yb, ga, gb, mod3, wa, wb, wo, g1, b1, wr, br):
    wr_hi = wr.astype(BF16)
    wr_lo = (wr - wr_hi.astype(F32)).astype(BF16)
    bsz, seq, d = x.shape
    tm = min(TM_MERGE, seq)
    tok = lambda n: pl.BlockSpec((1, tm, n), lambda b, i: (b, i, 0))
    full = lambda a: pl.BlockSpec(a.shape, lambda b, i: (0,) * a.ndim)
    return pl.pallas_call(
        _merge_kernel,
        out_shape=[jax.ShapeDtypeStruct((bsz, seq, d), F32),
                   jax.ShapeDtypeStruct((bsz, seq * (d // LANES), LANES), F32),
                   jax.ShapeDtypeStruct((bsz, seq, LANES), F32),
                   jax.ShapeDtypeStruct((bsz, seq, LANES), F32),
                   jax.ShapeDtypeStruct((8, LANES), F32)],
        grid=(bsz, seq // tm),
        in_specs=[tok(d), tok(A_WIDTH), tok(B_HEADS * B_V_DIM), tok(d), tok(d),
                  pl.BlockSpec((1, 6, d), lambda b, i: (b, 0, 0)),
                  full(wa), full(wb), full(wo), full(g1), full(b1), full(wr_hi), full(wr_lo),
                  full(br)],
        out_specs=[tok(d),
                   pl.BlockSpec((1, tm * (d // LANES), LANES), lambda b, i: (b, i, 0)),
                   tok(LANES), tok(LANES), pl.BlockSpec((8, LANES), lambda b, i: (0, 0))],
        compiler_params=_params(("arbitrary", "arbitrary"), 48),
    )(x, ya, yb, ga, gb, mod3, wa, wb, wo, g1, b1, wr_hi, wr_lo, br)


def _route_kernel(code_ref, cnt_ref, pos_ref, tmap_ref, nv_ref, carry_s, pstart_s, *, tr, tm):
    i = pl.program_id(0)
    lane = lax.broadcasted_iota(jnp.int32, (tr, LANES), 1)
    code = code_ref[...]
    picked = jnp.where(code > 0.0, 1.0, 0.0)
    tile_count = jnp.broadcast_to(picked.sum(axis=0, keepdims=True), (8, LANES))

    @pl.when(i == 0)
    def _():
        lane8 = lax.broadcasted_iota(jnp.int32, (8, LANES), 1)
        padded = jnp.floor((cnt_ref[...] + (tm - 1)) * (1.0 / tm)) * tm
        cum = padded
        shift = 1
        while shift < N_EXPERTS:
            cum = cum + jnp.where(lane8 >= shift, pltpu.roll(cum, shift, 1), 0.0)
            shift *= 2
        pstart_s[...] = cum - padded
        carry_s[...] = jnp.zeros_like(carry_s)
        nt = tmap_ref.shape[0]
        tile_start = (lax.broadcasted_iota(jnp.int32, (nt, LANES), 0) * tm).astype(F32)
        lane_t = lax.broadcasted_iota(jnp.int32, (nt, LANES), 1)
        done = jnp.where((lane_t < N_EXPERTS) & (cum[0:1] <= tile_start), 1.0, 0.0)
        te = jnp.minimum(done.sum(axis=-1, keepdims=True), N_EXPERTS - 1.0)
        tmap_ref[...] = jnp.broadcast_to(te, (nt, LANES)).astype(jnp.int32)
        total = jnp.where(lane8 == N_EXPERTS - 1, cum, 0.0).sum(axis=-1, keepdims=True)
        row8 = lax.broadcasted_iota(jnp.int32, (8, LANES), 0)
        meta = jnp.where(row8 == 0, jnp.broadcast_to(total * (1.0 / tm), (8, LANES)),
                         jnp.where(row8 == 1, cum - padded + cnt_ref[...], cum))
        nv_ref[...] = meta.astype(jnp.int32)

    r = lax.broadcasted_iota(jnp.int32, (tr, tr), 0)
    c = lax.broadcasted_iota(jnp.int32, (tr, tr), 1)
    tri = jnp.where(c < r, 1.0, 0.0).astype(BF16)
    base = _dot(tri, picked.astype(BF16)) + (carry_s[0:1] + pstart_s[0:1])
    carry_s[...] += tile_count
    pos = jnp.zeros((tr, LANES), F32)
    for k in range(TOP_K):
        pk = jnp.where(code == k + 1.0, base, 0.0).sum(axis=-1, keepdims=True)
        pos = jnp.where(lane == k, pk, pos)
    pos_ref[...] = pos.astype(jnp.int32)


def _route(code, cnt, tm):
    n = code.shape[0]
    tr = min(TR_ROUTE, n)
    return pl.pallas_call(
        functools.partial(_route_kernel, tr=tr, tm=tm),
        out_shape=[jax.ShapeDtypeStruct((n, LANES), jnp.int32),
                   jax.ShapeDtypeStruct((MAX_TILES_PAD, LANES), jnp.int32),
                   jax.ShapeDtypeStruct((8, LANES), jnp.int32)],
        grid=(n // tr,),
        in_specs=[pl.BlockSpec((tr, LANES), lambda i: (i, 0)),
                  pl.BlockSpec((8, LANES), lambda i: (0, 0))],
        out_specs=[pl.BlockSpec((tr, LANES), lambda i: (i, 0)),
                   pl.BlockSpec((MAX_TILES_PAD, LANES), lambda i: (0, 0)),
                   pl.BlockSpec((8, LANES), lambda i: (0, 0))],
        scratch_shapes=[pltpu.VMEM((8, LANES), F32)] * 2,
        compiler_params=_params(("arbitrary",)),
    )(code, cnt)


def _dispatch_kernel(pos_ref, fill_ref, h_ref, xs_ref, ztile_s, sem, fill_sem, tail_sem,
                     *, tm, nsub, tm_moe):
    def tile(ref, r):
        return ref.at[pl.ds(pl.multiple_of(r * nsub, nsub), nsub)]

    @pl.when(pl.program_id(0) == 0)
    def _():
        ztile_s[...] = jnp.zeros_like(ztile_s)

        def tail_copy(t):
            rows = tm_moe * nsub
            return pltpu.make_async_copy(
                ztile_s, xs_ref.at[pl.ds(pl.multiple_of(t * rows, rows), rows)], tail_sem)

        sizes = [1 << b for b in reversed(range(tm_moe.bit_length() - 1))]

        def fill_pieces(e, act):
            lo, hi = fill_ref[LANES + e], fill_ref[2 * LANES + e]
            at = lo
            for size in sizes:
                has = ((hi - lo) & size) != 0
                copy = pltpu.make_async_copy(
                    ztile_s.at[pl.ds(0, size * nsub)],
                    xs_ref.at[pl.ds(pl.multiple_of(at * nsub, nsub), size * nsub)], fill_sem)
                pl.when(has)(functools.partial(act, copy))
                at = at + jnp.where(has, size, 0)

        def fill_start(e, c):
            fill_pieces(e, lambda copy: copy.start())
            return c

        def fill_wait(e, c):
            fill_pieces(e, lambda copy: copy.wait())
            return c

        n_tiles = xs_ref.shape[0] // (tm_moe * nsub)

        def tail_start(t, c):
            tail_copy(t).start()
            return c

        def tail_drain(_, c):
            tail_copy(0).wait()
            return c

        lax.fori_loop(0, N_EXPERTS, fill_start, 0)
        lax.fori_loop(fill_ref[0], n_tiles, tail_start, 0)
        lax.fori_loop(0, N_EXPERTS, fill_wait, 0)
        lax.fori_loop(fill_ref[0], n_tiles, tail_drain, 0)

    def issue(j, c):
        for r in range(COMBINE_GROUP):
            row = pl.multiple_of(j * COMBINE_GROUP, COMBINE_GROUP) + r
            for k in range(TOP_K):
                pltpu.make_async_copy(tile(h_ref, row), tile(xs_ref, pos_ref[row * TOP_K + k]),
                                      sem).start(priority=k % 2)
        return c

    lax.fori_loop(0, tm // COMBINE_GROUP, issue, 0)
    for _ in range(TOP_K):
        pltpu.make_async_copy(h_ref, xs_ref.at[pl.ds(0, tm * nsub)], sem).wait()


def _dispatch(pos_flat, fill, h2, cap, nsub, tm_moe):
    n = h2.shape[0] // nsub
    tm = min(TM_ROWS, n)
    return pl.pallas_call(
        functools.partial(_dispatch_kernel, tm=tm, nsub=nsub, tm_moe=tm_moe),
        out_shape=jax.ShapeDtypeStruct((cap * nsub, LANES), F32),
        grid=(n // tm,),
        in_specs=[pl.BlockSpec((tm * TOP_K,), lambda i: (i,), memory_space=pltpu.SMEM),
                  pl.BlockSpec(fill.shape, lambda i: (0,), memory_space=pltpu.SMEM),
                  pl.BlockSpec((tm * nsub, LANES), lambda i: (i, 0))],
        out_specs=pl.BlockSpec(memory_space=pl.ANY),
        scratch_shapes=[pltpu.VMEM((tm_moe * nsub, LANES), F32),
                        pltpu.SemaphoreType.DMA(()), pltpu.SemaphoreType.DMA(()),
                        pltpu.SemaphoreType.DMA(())],
        compiler_params=_params(("arbitrary",)),
    )(pos_flat, fill, h2)


def _moe_kernel(te_ref, nv_ref, x_ref, bg_ref, bu_ref, bd_ref, wg_hbm, wu_hbm, wd_hbm,
                y_ref, wf_s, wg_s, wu_s, wd_s, slot_s, sem):
    i = pl.program_id(0)
    nv = nv_ref[0]
    last = te_ref.shape[0] - 1

    def weight_copies(e, slot):
        return [pltpu.make_async_copy(w.at[e], wf_s.at[slot, j], sem.at[slot])
                for j, w in enumerate((wg_hbm, wu_hbm, wd_hbm))]

    @pl.when(i < nv)
    def _():
        e = te_ref[i]

        @pl.when(i == 0)
        def _():
            slot_s[0] = 0
            for c in weight_copies(e, 0):
                c.start()

        @pl.when((i == 0) | (e != te_ref[jnp.maximum(i - 1, 0)]))
        def _():
            slot = slot_s[0]
            for c in weight_copies(e, slot):
                c.wait()
            nxt = lax.while_loop(lambda j: (j < nv) & (te_ref[jnp.minimum(j, last)] == e),
                                 lambda j: j + 1, i + 1)

            @pl.when(nxt < nv)
            def _():
                for c in weight_copies(te_ref[jnp.minimum(nxt, last)], 1 - slot):
                    c.start()

            wg_s[...] = wf_s[slot, 0].astype(BF16)
            wu_s[...] = wf_s[slot, 1].astype(BF16)
            wd_s[...] = wf_s[slot, 2].astype(BF16)
            slot_s[0] = 1 - slot

        xb = _load_token_tiles(x_ref, wg_s.shape[0] // LANES).astype(BF16)
        g = _dot(xb, wg_s[...]) + bg_ref[0]
        u = _dot(xb, wu_s[...]) + bu_ref[0]
        g = jnp.minimum(g, SWIGLU_LIMIT)
        u = jnp.clip(u, -SWIGLU_LIMIT, SWIGLU_LIMIT)
        h = (u + 1.0) * (g * jax.nn.sigmoid(SWIGLU_ALPHA * g))
        _store_token_tiles(y_ref, _dot(h.astype(BF16), wd_s[...]) + bd_ref[0])

    @pl.when(i >= nv)
    def _():
        y_ref[...] = jnp.zeros_like(y_ref)


def _moe(tile_expert, n_valid, xs, w_gate, b_gate, w_up, b_up, w_down, b_down, tm):
    e, d, f = w_gate.shape
    assert d == f
    nsub = d // LANES
    cap = xs.shape[0] // nsub
    n_tiles = cap // tm

    def tile(i, te, nv):
        return jnp.minimum(i, nv[0] - 1)

    exp3 = lambda i, te, nv: (te[tile(i, te, nv)], 0, 0)
    grid_spec = pltpu.PrefetchScalarGridSpec(
        num_scalar_prefetch=2,
        grid=(n_tiles,),
        in_specs=[pl.BlockSpec((tm * nsub, LANES), lambda i, te, nv: (tile(i, te, nv), 0)),
                  pl.BlockSpec((1, 1, f), exp3), pl.BlockSpec((1, 1, f), exp3),
                  pl.BlockSpec((1, 1, d), exp3),
                  pl.BlockSpec(memory_space=pl.ANY), pl.BlockSpec(memory_space=pl.ANY),
                  pl.BlockSpec(memory_space=pl.ANY)],
        out_specs=pl.BlockSpec((tm * nsub, LANES), lambda i, te, nv: (i, 0)),
        scratch_shapes=[pltpu.VMEM((2, 3, d, f), F32),
                        pltpu.VMEM((d, f), BF16), pltpu.VMEM((d, f), BF16), pltpu.VMEM((f, d), BF16),
                        pltpu.SMEM((1,), jnp.int32), pltpu.SemaphoreType.DMA((2,))],
    )
    return pl.pallas_call(
        _moe_kernel,
        out_shape=jax.ShapeDtypeStruct((cap * nsub, LANES), F32),
        grid_spec=grid_spec,
        compiler_params=_params(("arbitrary",), 56),
    )(tile_expert, n_valid, xs, b_gate.reshape(e, 1, f), b_up.reshape(e, 1, f),
      b_down.reshape(e, 1, d), w_gate, w_up, w_down)


def _combine_kernel(pos_ref, posn_ref, w_ref, x1_ref, mod_ref, g_ref, b_ref, y_ref, o_ref,
                    buf, sem, *, tm):
    i = pl.program_id(0)
    n = pl.num_programs(0)
    nsub = x1_ref.shape[1] // LANES

    group = COMBINE_GROUP

    def issue_rows(p_ref, slot, r0):
        for r in range(group):
            for k in range(TOP_K):
                row = r0 + r
                src = pl.ds(pl.multiple_of(p_ref[row * TOP_K + k] * nsub, nsub), nsub)
                dst = pl.ds(pl.multiple_of(row * nsub, nsub), nsub)
                pltpu.make_async_copy(y_ref.at[src], buf.at[slot, k, dst],
                                      sem.at[slot]).start(priority=k % 2)

    def reduce_rows(slot, r0):
        rows = pl.ds(r0, group)
        w = w_ref[rows, :]
        f = jnp.zeros((group, nsub * LANES), F32)
        for k in range(TOP_K):
            yk = jnp.concatenate(
                [buf[slot, k, pl.ds(r0 * nsub + s, group, stride=nsub), :] for s in range(nsub)],
                axis=1)
            f = f + w[:, k:k + 1] * yk
        o_ref[rows, :] = f

    @pl.when(i == 0)
    def _():
        def body(j, c):
            issue_rows(pos_ref, 0, pl.multiple_of(j * group, group))
            return c
        lax.fori_loop(0, tm // group, body, 0)

    slot = i % 2
    for k in range(TOP_K):
        pltpu.make_async_copy(y_ref.at[pl.ds(0, tm * nsub)], buf.at[slot, k], sem.at[slot]).wait()

    @pl.when(i + 1 < n)
    def _():
        def body(j, c):
            r0 = pl.multiple_of(j * group, group)
            issue_rows(posn_ref, 1 - slot, r0)
            reduce_rows(slot, r0)
            return c
        lax.fori_loop(0, tm // group, body, 0)

    @pl.when(i + 1 == n)
    def _():
        def body(j, c):
            reduce_rows(slot, pl.multiple_of(j * group, group))
            return c
        lax.fori_loop(0, tm // group, body, 0)

    o_ref[...] = _layer_norm(DEEPNORM_ALPHA * x1_ref[...] + mod_ref[0, 5:6, :] * o_ref[...],
                             g_ref[...], b_ref[...])


def _combine(pos_flat, w, x1, mod3, g2, b2, y, seq):
    n, d = x1.shape
    tm = min(TM_ROWS, seq)
    nsteps = n // tm
    per_seq = seq // tm
    return pl.pallas_call(
        functools.partial(_combine_kernel, tm=tm),
        out_shape=jax.ShapeDtypeStruct((n, d), F32),
        grid=(nsteps,),
        in_specs=[pl.BlockSpec((tm * TOP_K,), lambda i: (i,), memory_space=pltpu.SMEM),
                  pl.BlockSpec((tm * TOP_K,), lambda i: (jnp.minimum(i + 1, nsteps - 1),),
                               memory_space=pltpu.SMEM),
                  pl.BlockSpec((tm, LANES), lambda i: (i, 0)),
                  pl.BlockSpec((tm, d), lambda i: (i, 0)),
                  pl.BlockSpec((1, 6, d), lambda i: (i // per_seq, 0, 0)),
                  pl.BlockSpec(g2.shape, lambda i: (0, 0)),
                  pl.BlockSpec(b2.shape, lambda i: (0, 0)),
                  pl.BlockSpec(memory_space=pl.ANY)],
        out_specs=pl.BlockSpec((tm, d), lambda i: (i, 0)),
        scratch_shapes=[pltpu.VMEM((2, TOP_K, tm * (d // LANES), LANES), F32),
                        pltpu.SemaphoreType.DMA((2,))],
        compiler_params=_params(("arbitrary",), 32),
    )(pos_flat, pos_flat, w, x1, mod3, g2, b2, y)


def _prep_w_uq(w_uq):
    r = w_uq.shape[0]
    w = w_uq.reshape(r, B_HEADS, B_QK_DIM)
    half = B_ROPE_DIM // 2
    rot = jnp.concatenate([-w[:, :, B_NOPE_DIM + half:], w[:, :, B_NOPE_DIM:B_NOPE_DIM + half]],
                          axis=-1).reshape(r, B_HEADS * B_ROPE_DIM)
    w = jnp.pad(w, ((0, 0), (0, 0), (0, LANES - B_QK_DIM))).reshape(r, B_HEADS * LANES)
    return jnp.concatenate([w, rot], axis=1).astype(BF16)


def _prep_w_ukv(w_ukv):
    r = w_ukv.shape[0]
    w = w_ukv.reshape(r, B_HEADS, B_NOPE_DIM + B_V_DIM)
    wk = jnp.pad(w[:, :, :B_NOPE_DIM], ((0, 0), (0, 0), (0, LANES - B_NOPE_DIM)))
    wv = w[:, :, B_NOPE_DIM:].reshape(r, B_HEADS // 2, 2 * B_V_DIM)
    wv = jnp.pad(wv, ((0, 0), (0, 0), (0, V_PAIR_LANES - 2 * B_V_DIM)))
    return (wk.reshape(r, B_HEADS * LANES).astype(BF16),
            wv.reshape(r, B_HEADS // 2 * V_PAIR_LANES).astype(BF16))


def _rope_tables(positions):
    half = B_ROPE_DIM // 2
    freqs = ROPE_THETA ** (-jnp.arange(half, dtype=F32) / half)
    ang = positions.astype(F32).reshape(1, -1) * freqs[:, None]
    cos, sin = jnp.cos(ang), jnp.sin(ang)
    z = jnp.zeros((B_ROPE_DIM, ang.shape[1]), F32)
    return jnp.concatenate([sin, sin, z, cos, cos, z], 0)


def kernel(x, c, positions, w_ada, b_ada, w_in, rms_q, w_uq, rms_kv, w_ukv, rel_bias,
           w_branch_a, w_branch_b, w_out, ln1_g, ln1_b, w_router, b_router,
           w_gate, b_gate, w_up, b_up, w_down, b_down, ln2_g, ln2_b):
    bsz, seq, d = x.shape
    n_tok = bsz * seq
    assert w_ada.shape[0] == DEPTH == 1
    row = lambda v: v.reshape(1, -1)

    mod3 = _ada(c, w_ada[0], b_ada[0]).reshape(bsz, 6, d)

    qk, va, cq, ckv, kr, ga, gb = _inproj(x, mod3, w_in)
    ya = _attn_a(qk, va, _bias_table(rel_bias[0], min(TQ_A, seq)))
    wk, wv = _prep_w_ukv(w_ukv[0])
    qm, km, vm = _mla_proj(cq, ckv, kr, _rope_tables(positions), row(rms_q[0]), row(rms_kv[0]),
                           _prep_w_uq(w_uq[0]), wk, wv)
    yb = _mla_attn(qm, km, vm)

    wr = jnp.pad(w_router[0], ((0, 0), (0, LANES - N_EXPERTS)))
    br = jnp.pad(b_router[0], (0, LANES - N_EXPERTS), constant_values=NEG_INF)
    x1, h2, code, wts, cnt = _merge(x, ya, yb, ga, gb, mod3,
                                    w_branch_a[0].astype(BF16), w_branch_b[0].astype(BF16),
                                    w_out[0].astype(BF16), row(ln1_g[0]), row(ln1_b[0]), wr, row(br))
    wts = wts.reshape(n_tok, LANES)

    tm = TM_MOE
    n_tiles = -(-(n_tok * TOP_K) // tm) + N_EXPERTS
    assert n_tiles <= MAX_TILES_PAD
    pos, tmap, meta = _route(code.reshape(n_tok, LANES), cnt, tm)
    pos_flat = pos[:, :TOP_K].reshape(-1)
    fill = meta[:3].reshape(-1)
    nsub = d // LANES
    xs = _dispatch(pos_flat, fill, h2.reshape(n_tok * nsub, LANES), n_tiles * tm, nsub, tm)
    y = _moe(tmap[:n_tiles, 0], meta[0, :1], xs, w_gate[0], b_gate[0], w_up[0], b_up[0],
             w_down[0], b_down[0], tm)
    out = _combine(pos_flat, wts, x1.reshape(n_tok, d), mod3, row(ln2_g[0]), row(ln2_b[0]), y, seq)
    return out.reshape(bsz, seq, d)
```

```python
import functools

import jax
import jax.numpy as jnp
from jax import lax
from jax.experimental import pallas as pl
from jax.experimental.pallas import tpu as pltpu

F32 = jnp.float32
BF16 = jnp.bfloat16

CHUNK = 64
A_HEADS = 8
A_HEAD_DIM = 64
A_LEFT_CHUNKS = 8
A_MAX_REL = 128
A_WIDTH = A_HEADS * A_HEAD_DIM
B_HEADS = 8
B_NOPE_DIM = 64
B_ROPE_DIM = 32
B_V_DIM = 64
B_Q_LORA = 384
B_KV_LORA = 256
B_QK_DIM = B_NOPE_DIM + B_ROPE_DIM
ROPE_THETA = 10000.0
N_EXPERTS = 32
TOP_K = 4
SWIGLU_LIMIT = 7.0
SWIGLU_ALPHA = 1.702
DEPTH = 1
DEEPNORM_ALPHA = (2.0 * DEPTH) ** 0.25
LN_EPS = 1e-5
RMS_EPS = 1e-6
NEG_INF = -1e30
LOG2_E = 1.4426950408889634

LANES = 128
MXU_COLS = 256
V_PAIR_LANES = MXU_COLS

TM_PROJ = 512
TQ_A = 256
A_QBLOCKS = 8
TQ_B = 512
TM_MERGE = 1024
MERGE_SPLIT = 2
TR_ROUTE = 1024
TM_MOE = 512
TM_ROWS = 256
COMBINE_GROUP = 8
MAX_TILES_PAD = 512


def _params(sem, vmem_mb=None):
    return pltpu.CompilerParams(
        dimension_semantics=sem,
        vmem_limit_bytes=None if vmem_mb is None else vmem_mb << 20)


def _dot(a, b):
    return jnp.dot(a, b, preferred_element_type=F32)


def _dot_nt(a, b):
    return lax.dot_general(a, b, (((1,), (1,)), ((), ())), preferred_element_type=F32)


def _store_token_tiles(ref, x):
    rows, d = x.shape
    nsub = d // LANES
    for s in range(nsub):
        ref[pl.ds(s, rows, stride=nsub), :] = x[:, s * LANES:(s + 1) * LANES]


def _load_token_tiles(ref, nsub):
    rows = ref.shape[0] // nsub
    return jnp.concatenate([ref[pl.ds(s, rows, stride=nsub), :] for s in range(nsub)], axis=1)


def _layer_norm(x, g, b):
    mu = jnp.mean(x, axis=-1, keepdims=True)
    xc = x - mu
    var = jnp.mean(xc * xc, axis=-1, keepdims=True)
    return xc * lax.rsqrt(var + LN_EPS) * g + b


def _ada_kernel(c_ref, w_ref, b_ref, o_ref):
    c = c_ref[...]
    sc = c * jax.nn.sigmoid(c)
    o_ref[...] = jnp.dot(sc, w_ref[...], preferred_element_type=F32,
                         precision=lax.Precision.HIGHEST) + b_ref[...]


def _ada(c, w_ada, b_ada):
    bsz, d = c.shape
    m = w_ada.shape[1]
    tn = 1024
    return pl.pallas_call(
        _ada_kernel,
        out_shape=jax.ShapeDtypeStruct((bsz, m), F32),
        grid=(m // tn,),
        in_specs=[pl.BlockSpec((bsz, d), lambda j: (0, 0)),
                  pl.BlockSpec((d, tn), lambda j: (0, j)),
                  pl.BlockSpec((1, tn), lambda j: (0, j))],
        out_specs=pl.BlockSpec((bsz, tn), lambda j: (0, j)),
        compiler_params=_params(("arbitrary",)),
    )(c, w_ada, b_ada.reshape(1, m))


def _inproj_kernel(x_ref, mod_ref, win_hbm, qk_ref, va_ref, cq_ref, ckv_ref, kr_ref, ga_ref, gb_ref,
                   wf_s, wqk_ref, wv_ref, wcq_ref, wckv_ref, wkr_ref, wga_ref, wgb_ref, sem):
    @pl.when((pl.program_id(0) == 0) & (pl.program_id(1) == 0))
    def _():
        copy = pltpu.make_async_copy(win_hbm.at[0], wf_s, sem)
        copy.start()
        copy.wait()
        d = wf_s.shape[1]
        chunk = 2 * LANES
        off = 0
        for dst, n in ((wqk_ref, 2 * A_WIDTH), (wv_ref, A_WIDTH), (wcq_ref, B_Q_LORA),
                       (wckv_ref, B_KV_LORA), (None, B_ROPE_DIM), (wga_ref, d), (wgb_ref, d)):
            if dst is None:
                blk = wf_s[off:off + LANES, :].T
                lane = lax.broadcasted_iota(jnp.int32, blk.shape, 1)
                blk = pltpu.roll(jnp.where(lane < n, blk, 0.0), B_NOPE_DIM, 1)
                wkr_ref[...] = blk.astype(BF16)
            else:
                for c0 in range(0, n, chunk):
                    cn = min(chunk, n - c0)
                    blk = wf_s[off + c0:off + c0 + cn, :].T
                    if dst is wqk_ref and c0 < A_WIDTH:
                        blk = blk * (A_HEAD_DIM ** -0.5 * LOG2_E)
                    dst[:, c0:c0 + cn] = blk.astype(BF16)
            off += n

    m = mod_ref[0]
    h = (x_ref[0] * (1.0 + m[1:2]) + m[0:1]).astype(BF16)
    for w_ref, ref in ((wqk_ref, qk_ref), (wcq_ref, cq_ref), (wckv_ref, ckv_ref),
                       (wkr_ref, kr_ref), (wga_ref, ga_ref), (wgb_ref, gb_ref)):
        ref[0] = _dot(h, w_ref[...]).astype(ref.dtype)
    v = _dot(h, wv_ref[...]).astype(BF16)
    ones_blk = jnp.ones((v.shape[0], LANES), BF16)
    parts = []
    for hp in range(A_WIDTH // LANES):
        parts += [v[:, hp * LANES:(hp + 1) * LANES], ones_blk]
    va_ref[0] = jnp.concatenate(parts, axis=1)


def _inproj(x, mod3, w_in):
    bsz, seq, d = x.shape
    tm = min(TM_PROJ, seq)
    widths = (2 * A_WIDTH, A_WIDTH // LANES * V_PAIR_LANES, B_Q_LORA, B_KV_LORA, LANES, d, d)
    dtypes = (BF16, BF16, BF16, BF16, F32, BF16, BF16)
    w_widths = (2 * A_WIDTH, A_WIDTH, B_Q_LORA, B_KV_LORA, LANES, d, d)
    assert w_in.shape[0] == 1 and w_in.shape[1] == sum(w_widths) - LANES + B_ROPE_DIM
    return pl.pallas_call(
        _inproj_kernel,
        out_shape=[jax.ShapeDtypeStruct((bsz, seq, n), dt) for n, dt in zip(widths, dtypes)],
        grid=(bsz, seq // tm),
        in_specs=[pl.BlockSpec((1, tm, d), lambda b, i: (b, i, 0)),
                  pl.BlockSpec((1, 6, d), lambda b, i: (b, 0, 0)),
                  pl.BlockSpec(memory_space=pl.ANY)],
        out_specs=[pl.BlockSpec((1, tm, n), lambda b, i: (b, i, 0)) for n in widths],
        scratch_shapes=[pltpu.VMEM(w_in.shape[1:], F32)]
                       + [pltpu.VMEM((d, n), BF16) for n in w_widths]
                       + [pltpu.SemaphoreType.DMA(())],
        compiler_params=_params(("arbitrary", "arbitrary"), 56),
    )(x, mod3, w_in)


def _attn_a_kernel(q_ref, k_ref, v_ref, bias_ref, o_ref, *, tq, nd, nblk):
    band = nd * tq
    lane = lax.broadcasted_iota(jnp.int32, (1, LANES), 1)
    for sub in range(nblk):
        i = pl.program_id(2) * nblk + sub
        start = pl.multiple_of(jnp.maximum(i - (nd - 1), 0) * tq, tq)
        variant = jnp.minimum(i, nd - 1)
        q = q_ref[0, sub * tq:(sub + 1) * tq, :]
        k = k_ref[0, pl.ds(start, band), :]
        v = v_ref[0, pl.ds(start, band), :]
        zero = jnp.zeros_like(k)
        kk = jnp.concatenate([jnp.where(lane < A_HEAD_DIM, k, zero),
                              jnp.where(lane >= A_HEAD_DIM, k, zero)], axis=0)
        s = _dot_nt(q, kk)
        ps = []
        for hh in range(2):
            sh = s[:, hh * band:(hh + 1) * band] + bias_ref[hh, variant]
            m = sh.max(axis=-1, keepdims=True)
            ps.append(jnp.exp2(sh - m).astype(BF16))
        acc = _dot(jnp.concatenate(ps, axis=0), v)
        out = acc[:, :LANES] / acc[:, LANES:]
        o_ref[0, sub * tq:(sub + 1) * tq, :] = jnp.where(
            lane < A_HEAD_DIM, out[:tq], out[tq:]).astype(o_ref.dtype)


def _attn_a(qk, va, bias_tab):
    bsz, seq, _ = qk.shape
    tq = min(TQ_A, seq)
    nd = bias_tab.shape[1]
    assert seq >= nd * tq
    ncb = A_WIDTH // LANES
    nblk = min(A_QBLOCKS, seq // tq)
    return pl.pallas_call(
        functools.partial(_attn_a_kernel, tq=tq, nd=nd, nblk=nblk),
        out_shape=jax.ShapeDtypeStruct((bsz, seq, A_WIDTH), BF16),
        grid=(ncb, bsz, seq // (nblk * tq)),
        in_specs=[pl.BlockSpec((1, nblk * tq, LANES), lambda hp, b, i: (b, i, hp)),
                  pl.BlockSpec((1, seq, LANES), lambda hp, b, i: (b, 0, ncb + hp)),
                  pl.BlockSpec((1, seq, V_PAIR_LANES), lambda hp, b, i: (b, 0, hp)),
                  pl.BlockSpec((2, nd, tq, nd * tq), lambda hp, b, i: (hp, 0, 0, 0))],
        out_specs=pl.BlockSpec((1, nblk * tq, LANES), lambda hp, b, i: (b, i, hp)),
        compiler_params=_params(("arbitrary",) * 3, 40),
    )(qk, qk, va, bias_tab)


def _bias_table(rel_bias, tq):
    nd = (A_LEFT_CHUNKS * CHUNK + tq - 1) // tq + 1
    width = nd * tq
    period = width + tq
    u = jnp.arange(period)
    dist = jnp.where(u < width, (nd - 1) * tq - u, (nd - 1) * tq + period - u)
    vec = rel_bias[:, jnp.clip(dist, -A_MAX_REL, A_MAX_REL) + A_MAX_REL].astype(F32)
    toep = jnp.tile(vec, (1, tq))[:, :tq * (period - 1)].reshape(-1, tq, period - 1)[:, :, :width]
    qi = jnp.arange(tq)[:, None]
    x = jnp.arange(width)[None, :]
    dchunk = qi // CHUNK - x // CHUNK + (nd - 1) * (tq // CHUNK)
    valid = (dchunk >= 0) & (dchunk <= A_LEFT_CHUNKS)
    table = jnp.where(valid[None], toep * LOG2_E, NEG_INF)
    variants = []
    for v in range(nd):
        cut = (nd - 1 - v) * tq
        variants.append(jnp.pad(table[:, :, cut:], ((0, 0), (0, 0), (0, cut)),
                                constant_values=NEG_INF))
    return jnp.stack(variants, axis=1)


def _mla_proj_kernel(cq_ref, ckv_ref, kr_ref, tab_ref, rq_ref, rkv_ref,
                     wq_ref, wk_ref, wv_ref, q_out, k_out, v_out):
    def rms(x, g):
        xf = x.astype(F32)
        ms = jnp.mean(xf * xf, axis=-1, keepdims=True)
        return (xf * lax.rsqrt(ms + RMS_EPS) * g).astype(BF16)

    cqn = rms(cq_ref[0], rq_ref[...])
    ckvn = rms(ckv_ref[0], rkv_ref[...])
    q = _dot(cqn, wq_ref[...])
    kn = _dot(ckvn, wk_ref[...])
    vlane = lax.broadcasted_iota(jnp.int32, (1, v_out.shape[-1]), 1)
    ones_lane = jnp.where(vlane % V_PAIR_LANES >= 2 * B_V_DIM, 1.0, 0.0)
    v_out[0] = (_dot(ckvn, wv_ref[...]) + ones_lane).astype(v_out.dtype)
    tab = tab_ref[...].T
    tlane = lax.broadcasted_iota(jnp.int32, tab.shape, 1)
    rope_end = B_NOPE_DIM + B_ROPE_DIM
    in_rope = (tlane >= B_NOPE_DIM) & (tlane < rope_end)
    cos_t = jnp.where(tlane < B_NOPE_DIM, 1.0, jnp.where(in_rope, tab, 0.0))
    sin_t = jnp.where(in_rope, pltpu.roll(tab, B_NOPE_DIM, 1), 0.0)
    first = tlane < B_NOPE_DIM + B_ROPE_DIM // 2
    sin_a = jnp.where(first, -sin_t, 0.0)
    sin_b = jnp.where(first, 0.0, sin_t)

    def rope(x, tables):
        c, sa, sb = tables
        return (x * c + pltpu.roll(x, LANES - B_ROPE_DIM // 2, 1) * sa
                + pltpu.roll(x, B_ROPE_DIM // 2, 1) * sb)

    kpe = rope(kr_ref[0], (cos_t, sin_a, sin_b))
    scale = B_QK_DIM ** -0.5 * LOG2_E
    cos_q, sin_q = cos_t * scale, sin_t * scale
    nq = B_HEADS * LANES
    per_block = LANES // B_ROPE_DIM
    for h in range(B_HEADS):
        sl = slice(h * LANES, (h + 1) * LANES)
        blk = nq + (h // per_block) * LANES
        partner = q[:, blk:blk + LANES]
        shift = (B_NOPE_DIM - B_ROPE_DIM * (h % per_block)) % LANES
        if shift:
            partner = pltpu.roll(partner, shift, 1)
        q_out[0, :, sl] = (q[:, sl] * cos_q + partner * sin_q).astype(q_out.dtype)
        k_out[0, :, sl] = (kn[:, sl] + kpe).astype(k_out.dtype)


def _mla_proj(cq, ckv, kr, rope_tab, rms_q, rms_kv, wq, wk, wv):
    bsz, seq, _ = cq.shape
    tm = min(TM_PROJ, seq)
    tok = lambda n: pl.BlockSpec((1, tm, n), lambda b, i: (b, i, 0))
    full = lambda a: pl.BlockSpec(a.shape, lambda b, i: (0,) * a.ndim)
    return pl.pallas_call(
        _mla_proj_kernel,
        out_shape=[jax.ShapeDtypeStruct((bsz, seq, B_HEADS * LANES), BF16),
                   jax.ShapeDtypeStruct((bsz, seq, B_HEADS * LANES), BF16),
                   jax.ShapeDtypeStruct((bsz, seq, wv.shape[1]), BF16)],
        grid=(bsz, seq // tm),
        in_specs=[tok(B_Q_LORA), tok(B_KV_LORA), tok(LANES),
                  pl.BlockSpec((LANES, tm), lambda b, i: (0, b * (seq // tm) + i)),
                  full(rms_q), full(rms_kv), full(wq), full(wk), full(wv)],
        out_specs=[tok(B_HEADS * LANES), tok(B_HEADS * LANES), tok(wv.shape[1])],
        compiler_params=_params(("arbitrary", "arbitrary"), 32),
    )(cq, ckv, kr, rope_tab, rms_q, rms_kv, wq, wk, wv)


def _mla_attn_kernel(q_ref, k_ref, v_ref, o_ref, s_a, s_b, m_s, acc_s, *, t):
    lane = lax.broadcasted_iota(jnp.int32, (1, LANES), 1)
    nq = q_ref.shape[1] // t
    pairs = [(qi, kj) for qi in range(nq) for kj in range(qi + 1)]
    bufs = (s_a, s_b)

    def produce(qi, kj, s_ref):
        for hh in range(2):
            q = q_ref[0, qi * t:(qi + 1) * t, hh * LANES:(hh + 1) * LANES]
            k = k_ref[0, kj * t:(kj + 1) * t, hh * LANES:(hh + 1) * LANES]
            s_ref[hh] = _dot_nt(q, k)

    def consume(qi, kj, s_ref):
        v = v_ref[0, kj * t:(kj + 1) * t, :]
        outs = []
        for hh in range(2):
            s = s_ref[hh]
            if kj == qi:
                row_chunk = lax.broadcasted_iota(jnp.int32, (t, t), 0) // CHUNK
                col_chunk = lax.broadcasted_iota(jnp.int32, (t, t), 1) // CHUNK
                s = jnp.where(col_chunk <= row_chunk, s, NEG_INF)
            row_max = jnp.broadcast_to(s.max(axis=-1, keepdims=True), (t, LANES))
            if kj == 0:
                m_new = row_max
                acc = _dot(jnp.exp2(s - jnp.tile(m_new, (1, t // LANES))).astype(BF16), v)
            else:
                m_old = m_s[hh]
                m_new = jnp.maximum(m_old, row_max)
                p = jnp.exp2(s - jnp.tile(m_new, (1, t // LANES))).astype(BF16)
                alpha = jnp.exp2(m_old - m_new)
                acc = jnp.tile(alpha, (1, V_PAIR_LANES // LANES)) * acc_s[hh] + _dot(p, v)
            if kj == qi:
                outs.append(acc[:, :LANES] / acc[:, LANES:])
            else:
                acc_s[hh] = acc
                m_s[hh] = m_new
        if kj == qi:
            o_ref[0, qi * t:(qi + 1) * t, :] = jnp.where(
                lane < B_V_DIM, outs[0], outs[1]).astype(o_ref.dtype)

    produce(*pairs[0], bufs[0])
    for n, pair in enumerate(pairs):
        if n + 1 < len(pairs):
            produce(*pairs[n + 1], bufs[(n + 1) % 2])
        consume(*pair, bufs[n % 2])


def _mla_attn(qm, km, vm):
    bsz, seq, _ = qm.shape
    t = min(TQ_B, seq)
    return pl.pallas_call(
        functools.partial(_mla_attn_kernel, t=t),
        out_shape=jax.ShapeDtypeStruct((bsz, seq, B_HEADS * B_V_DIM), BF16),
        grid=(bsz, B_HEADS // 2),
        in_specs=[pl.BlockSpec((1, seq, 2 * LANES), lambda b, hp: (b, 0, hp)),
                  pl.BlockSpec((1, seq, 2 * LANES), lambda b, hp: (b, 0, hp)),
                  pl.BlockSpec((1, seq, V_PAIR_LANES), lambda b, hp: (b, 0, hp))],
        out_specs=pl.BlockSpec((1, seq, LANES), lambda b, hp: (b, 0, hp)),
        scratch_shapes=[pltpu.VMEM((2, t, t), F32), pltpu.VMEM((2, t, t), F32),
                        pltpu.VMEM((2, t, LANES), F32),
                        pltpu.VMEM((2, t, V_PAIR_LANES), F32)],
        compiler_params=_params(("arbitrary",) * 2, 40),
    )(qm, km, vm)


def _merge_kernel(x_ref, ya_ref, yb_ref, ga_ref, gb_ref, mod_ref, wa_ref, wb_ref, wo_ref,
                  g1_ref, b1_ref, wrh_ref, wrl_ref, br_ref, x1_ref, h2_ref, code_ref, wts_ref,
                  cnt_ref):
    @pl.when((pl.program_id(0) == 0) & (pl.program_id(1) == 0))
    def _():
        cnt_ref[...] = jnp.zeros_like(cnt_ref)

    tm = x_ref.shape[1]
    half = tm // MERGE_SPLIT
    for part in range(MERGE_SPLIT):
        _merge_rows(slice(part * half, (part + 1) * half), x_ref, ya_ref, yb_ref, ga_ref, gb_ref,
                    mod_ref, wa_ref, wb_ref, wo_ref, g1_ref, b1_ref, wrh_ref, wrl_ref, br_ref,
                    x1_ref, h2_ref, code_ref, wts_ref, cnt_ref)


def _merge_rows(rows, x_ref, ya_ref, yb_ref, ga_ref, gb_ref, mod_ref, wa_ref, wb_ref, wo_ref,
                g1_ref, b1_ref, wrh_ref, wrl_ref, br_ref, x1_ref, h2_ref, code_ref, wts_ref,
                cnt_ref):
    m = mod_ref[0]
    a = _dot(ya_ref[0, rows, :], wa_ref[...])
    b = _dot(yb_ref[0, rows, :], wb_ref[...])
    merged = (jax.nn.sigmoid(ga_ref[0, rows, :].astype(F32)) * a
              + jax.nn.sigmoid(gb_ref[0, rows, :].astype(F32)) * b)
    o = _dot(merged.astype(BF16), wo_ref[...])
    x1 = _layer_norm(DEEPNORM_ALPHA * x_ref[0, rows, :] + m[2:3] * o, g1_ref[...], b1_ref[...])
    h2 = x1 * (1.0 + m[4:5]) + m[3:4]
    x1_ref[0, rows, :] = x1
    nsub = h2.shape[1] // LANES
    _store_token_tiles(h2_ref.at[0, pl.ds(rows.start * nsub, (rows.stop - rows.start) * nsub)], h2)
    h_hi = h2.astype(BF16)
    h_lo = (h2 - h_hi.astype(F32)).astype(BF16)
    work = (_dot(h_hi, wrh_ref[...]) + _dot(h_hi, wrl_ref[...]) + _dot(h_lo, wrh_ref[...])
            + br_ref[...])
    lane = lax.broadcasted_iota(jnp.int32, work.shape, 1)
    code = jnp.zeros(work.shape, F32)
    vals = []
    for k in range(TOP_K):
        top = work.max(axis=-1, keepdims=True)
        idx = jnp.where(work == top, lane, LANES).min(axis=-1, keepdims=True)
        sel = lane == idx
        work = jnp.where(sel, -jnp.inf, work)
        code = jnp.where(sel, k + 1.0, code)
        vals.append(top)
    den = sum(jnp.exp(v - vals[0]) for v in vals)
    wts = jnp.zeros(work.shape, F32)
    for k in range(TOP_K):
        wts = jnp.where(lane == k, jnp.exp(vals[k] - vals[0]) / den, wts)
    code_ref[0, rows, :] = code
    wts_ref[0, rows, :] = wts
    picked = jnp.where(code > 0.0, 1.0, 0.0)
    cnt_ref[...] += jnp.broadcast_to(picked.sum(axis=0, keepdims=True), cnt_ref.shape)


def _merge(x, ya, yb, ga, gb, mod3, wa, wb, wo, g1, b1, wr, br):
    wr_hi = wr.astype(BF16)
    wr_lo = (wr - wr_hi.astype(F32)).astype(BF16)
    bsz, seq, d = x.shape
    tm = min(TM_MERGE, seq)
    tok = lambda n: pl.BlockSpec((1, tm, n), lambda b, i: (b, i, 0))
    full = lambda a: pl.BlockSpec(a.shape, lambda b, i: (0,) * a.ndim)
    return pl.pallas_call(
        _merge_kernel,
        out_shape=[jax.ShapeDtypeStruct((bsz, seq, d), F32),
                   jax.ShapeDtypeStruct((bsz, seq * (d // LANES), LANES), F32),
                   jax.ShapeDtypeStruct((bsz, seq, LANES), F32),
                   jax.ShapeDtypeStruct((bsz, seq, LANES), F32),
                   jax.ShapeDtypeStruct((8, LANES), F32)],
        grid=(bsz, seq // tm),
        in_specs=[tok(d), tok(A_WIDTH), tok(B_HEADS * B_V_DIM), tok(d), tok(d),
                  pl.BlockSpec((1, 6, d), lambda b, i: (b, 0, 0)),
                  full(wa), full(wb), full(wo), full(g1), full(b1), full(wr_hi), full(wr_lo),
                  full(br)],
        out_specs=[tok(d),
                   pl.BlockSpec((1, tm * (d // LANES), LANES), lambda b, i: (b, i, 0)),
                   tok(LANES), tok(LANES), pl.BlockSpec((8, LANES), lambda b, i: (0, 0))],
        compiler_params=_params(("arbitrary", "arbitrary"), 48),
    )(x, ya, yb, ga, gb, mod3, wa, wb, wo, g1, b1, wr_hi, wr_lo, br)


def _route_kernel(code_ref, cnt_ref, pos_ref, tmap_ref, nv_ref, carry_s, pstart_s, *, tr, tm):
    i = pl.program_id(0)
    lane = lax.broadcasted_iota(jnp.int32, (tr, LANES), 1)
    code = code_ref[...]
    picked = jnp.where(code > 0.0, 1.0, 0.0)
    tile_count = jnp.broadcast_to(picked.sum(axis=0, keepdims=True), (8, LANES))

    @pl.when(i == 0)
    def _():
        lane8 = lax.broadcasted_iota(jnp.int32, (8, LANES), 1)
        padded = jnp.floor((cnt_ref[...] + (tm - 1)) * (1.0 / tm)) * tm
        cum = padded
        shift = 1
        while shift < N_EXPERTS:
            cum = cum + jnp.where(lane8 >= shift, pltpu.roll(cum, shift, 1), 0.0)
            shift *= 2
        pstart_s[...] = cum - padded
        carry_s[...] = jnp.zeros_like(carry_s)
        nt = tmap_ref.shape[0]
        tile_start = (lax.broadcasted_iota(jnp.int32, (nt, LANES), 0) * tm).astype(F32)
        lane_t = lax.broadcasted_iota(jnp.int32, (nt, LANES), 1)
        done = jnp.where((lane_t < N_EXPERTS) & (cum[0:1] <= tile_start), 1.0, 0.0)
        te = jnp.minimum(done.sum(axis=-1, keepdims=True), N_EXPERTS - 1.0)
        tmap_ref[...] = jnp.broadcast_to(te, (nt, LANES)).astype(jnp.int32)
        total = jnp.where(lane8 == N_EXPERTS - 1, cum, 0.0).sum(axis=-1, keepdims=True)
        row8 = lax.broadcasted_iota(jnp.int32, (8, LANES), 0)
        meta = jnp.where(row8 == 0, jnp.broadcast_to(total * (1.0 / tm), (8, LANES)),
                         jnp.where(row8 == 1, cum - padded + cnt_ref[...], cum))
        nv_ref[...] = meta.astype(jnp.int32)

    r = lax.broadcasted_iota(jnp.int32, (tr, tr), 0)
    c = lax.broadcasted_iota(jnp.int32, (tr, tr), 1)
    tri = jnp.where(c < r, 1.0, 0.0).astype(BF16)
    base = _dot(tri, picked.astype(BF16)) + (carry_s[0:1] + pstart_s[0:1])
    carry_s[...] += tile_count
    pos = jnp.zeros((tr, LANES), F32)
    for k in range(TOP_K):
        pk = jnp.where(code == k + 1.0, base, 0.0).sum(axis=-1, keepdims=True)
        pos = jnp.where(lane == k, pk, pos)
    pos_ref[...] = pos.astype(jnp.int32)


def _route(code, cnt, tm):
    n = code.shape[0]
    tr = min(TR_ROUTE, n)
    return pl.pallas_call(
        functools.partial(_route_kernel, tr=tr, tm=tm),
        out_shape=[jax.ShapeDtypeStruct((n, LANES), jnp.int32),
                   jax.ShapeDtypeStruct((MAX_TILES_PAD, LANES), jnp.int32),
                   jax.ShapeDtypeStruct((8, LANES), jnp.int32)],
        grid=(n // tr,),
        in_specs=[pl.BlockSpec((tr, LANES), lambda i: (i, 0)),
                  pl.BlockSpec((8, LANES), lambda i: (0, 0))],
        out_specs=[pl.BlockSpec((tr, LANES), lambda i: (i, 0)),
                   pl.BlockSpec((MAX_TILES_PAD, LANES), lambda i: (0, 0)),
                   pl.BlockSpec((8, LANES), lambda i: (0, 0))],
        scratch_shapes=[pltpu.VMEM((8, LANES), F32)] * 2,
        compiler_params=_params(("arbitrary",)),
    )(code, cnt)


def _dispatch_kernel(pos_ref, fill_ref, h_ref, xs_ref, ztile_s, sem, fill_sem, tail_sem,
                     *, tm, nsub, tm_moe):
    def tile(ref, r):
        return ref.at[pl.ds(pl.multiple_of(r * nsub, nsub), nsub)]

    @pl.when(pl.program_id(0) == 0)
    def _():
        ztile_s[...] = jnp.zeros_like(ztile_s)

        def tail_copy(t):
            rows = tm_moe * nsub
            return pltpu.make_async_copy(
                ztile_s, xs_ref.at[pl.ds(pl.multiple_of(t * rows, rows), rows)], tail_sem)

        sizes = [1 << b for b in reversed(range(tm_moe.bit_length() - 1))]

        def fill_pieces(e, act):
            lo, hi = fill_ref[LANES + e], fill_ref[2 * LANES + e]
            at = lo
            for size in sizes:
                has = ((hi - lo) & size) != 0
                copy = pltpu.make_async_copy(
                    ztile_s.at[pl.ds(0, size * nsub)],
                    xs_ref.at[pl.ds(pl.multiple_of(at * nsub, nsub), size * nsub)], fill_sem)
                pl.when(has)(functools.partial(act, copy))
                at = at + jnp.where(has, size, 0)

        def fill_start(e, c):
            fill_pieces(e, lambda copy: copy.start())
            return c

        def fill_wait(e, c):
            fill_pieces(e, lambda copy: copy.wait())
            return c

        n_tiles = xs_ref.shape[0] // (tm_moe * nsub)

        def tail_start(t, c):
            tail_copy(t).start()
            return c

        def tail_drain(_, c):
            tail_copy(0).wait()
            return c

        lax.fori_loop(0, N_EXPERTS, fill_start, 0)
        lax.fori_loop(fill_ref[0], n_tiles, tail_start, 0)
        lax.fori_loop(0, N_EXPERTS, fill_wait, 0)
        lax.fori_loop(fill_ref[0], n_tiles, tail_drain, 0)

    def issue(j, c):
        for r in range(COMBINE_GROUP):
            row = pl.multiple_of(j * COMBINE_GROUP, COMBINE_GROUP) + r
            for k in range(TOP_K):
                pltpu.make_async_copy(tile(h_ref, row), tile(xs_ref, pos_ref[row * TOP_K + k]),
                                      sem).start(priority=k % 2)
        return c

    lax.fori_loop(0, tm // COMBINE_GROUP, issue, 0)
    for _ in range(TOP_K):
        pltpu.make_async_copy(h_ref, xs_ref.at[pl.ds(0, tm * nsub)], sem).wait()


def _dispatch(pos_flat, fill, h2, cap, nsub, tm_moe):
    n = h2.shape[0] // nsub
    tm = min(TM_ROWS, n)
    return pl.pallas_call(
        functools.partial(_dispatch_kernel, tm=tm, nsub=nsub, tm_moe=tm_moe),
        out_shape=jax.ShapeDtypeStruct((cap * nsub, LANES), F32),
        grid=(n // tm,),
        in_specs=[pl.BlockSpec((tm * TOP_K,), lambda i: (i,), memory_space=pltpu.SMEM),
                  pl.BlockSpec(fill.shape, lambda i: (0,), memory_space=pltpu.SMEM),
                  pl.BlockSpec((tm * nsub, LANES), lambda i: (i, 0))],
        out_specs=pl.BlockSpec(memory_space=pl.ANY),
        scratch_shapes=[pltpu.VMEM((tm_moe * nsub, LANES), F32),
                        pltpu.SemaphoreType.DMA(()), pltpu.SemaphoreType.DMA(()),
                        pltpu.SemaphoreType.DMA(())],
        compiler_params=_params(("arbitrary",)),
    )(pos_flat, fill, h2)


def _moe_kernel(te_ref, nv_ref, x_ref, bg_ref, bu_ref, bd_ref, wg_hbm, wu_hbm, wd_hbm,
                y_ref, wf_s, wg_s, wu_s, wd_s, slot_s, sem):
    i = pl.program_id(0)
    nv = nv_ref[0]
    last = te_ref.shape[0] - 1

    def weight_copies(e, slot):
        return [pltpu.make_async_copy(w.at[e], wf_s.at[slot, j], sem.at[slot])
                for j, w in enumerate((wg_hbm, wu_hbm, wd_hbm))]

    @pl.when(i < nv)
    def _():
        e = te_ref[i]

        @pl.when(i == 0)
        def _():
            slot_s[0] = 0
            for c in weight_copies(e, 0):
                c.start()

        @pl.when((i == 0) | (e != te_ref[jnp.maximum(i - 1, 0)]))
        def _():
            slot = slot_s[0]
            for c in weight_copies(e, slot):
                c.wait()
            nxt = lax.while_loop(lambda j: (j < nv) & (te_ref[jnp.minimum(j, last)] == e),
                                 lambda j: j + 1, i + 1)

            @pl.when(nxt < nv)
            def _():
                for c in weight_copies(te_ref[jnp.minimum(nxt, last)], 1 - slot):
                    c.start()

            wg_s[...] = wf_s[slot, 0].astype(BF16)
            wu_s[...] = wf_s[slot, 1].astype(BF16)
            wd_s[...] = wf_s[slot, 2].astype(BF16)
            slot_s[0] = 1 - slot

        xb = _load_token_tiles(x_ref, wg_s.shape[0] // LANES).astype(BF16)
        g = _dot(xb, wg_s[...]) + bg_ref[0]
        u = _dot(xb, wu_s[...]) + bu_ref[0]
        g = jnp.minimum(g, SWIGLU_LIMIT)
        u = jnp.clip(u, -SWIGLU_LIMIT, SWIGLU_LIMIT)
        h = (u + 1.0) * (g * jax.nn.sigmoid(SWIGLU_ALPHA * g))
        _store_token_tiles(y_ref, _dot(h.astype(BF16), wd_s[...]) + bd_ref[0])

    @pl.when(i >= nv)
    def _():
        y_ref[...] = jnp.zeros_like(y_ref)


def _moe(tile_expert, n_valid, xs, w_gate, b_gate, w_up, b_up, w_down, b_down, tm):
    e, d, f = w_gate.shape
    assert d == f
    nsub = d // LANES
    cap = xs.shape[0] // nsub
    n_tiles = cap // tm

    def tile(i, te, nv):
        return jnp.minimum(i, nv[0] - 1)

    exp3 = lambda i, te, nv: (te[tile(i, te, nv)], 0, 0)
    grid_spec = pltpu.PrefetchScalarGridSpec(
        num_scalar_prefetch=2,
        grid=(n_tiles,),
        in_specs=[pl.BlockSpec((tm * nsub, LANES), lambda i, te, nv: (tile(i, te, nv), 0)),
                  pl.BlockSpec((1, 1, f), exp3), pl.BlockSpec((1, 1, f), exp3),
                  pl.BlockSpec((1, 1, d), exp3),
                  pl.BlockSpec(memory_space=pl.ANY), pl.BlockSpec(memory_space=pl.ANY),
                  pl.BlockSpec(memory_space=pl.ANY)],
        out_specs=pl.BlockSpec((tm * nsub, LANES), lambda i, te, nv: (i, 0)),
        scratch_shapes=[pltpu.VMEM((2, 3, d, f), F32),
                        pltpu.VMEM((d, f), BF16), pltpu.VMEM((d, f), BF16), pltpu.VMEM((f, d), BF16),
                        pltpu.SMEM((1,), jnp.int32), pltpu.SemaphoreType.DMA((2,))],
    )
    return pl.pallas_call(
        _moe_kernel,
        out_shape=jax.ShapeDtypeStruct((cap * nsub, LANES), F32),
        grid_spec=grid_spec,
        compiler_params=_params(("arbitrary",), 56),
    )(tile_expert, n_valid, xs, b_gate.reshape(e, 1, f), b_up.reshape(e, 1, f),
      b_down.reshape(e, 1, d), w_gate, w_up, w_down)


def _combine_kernel(pos_ref, posn_ref, w_ref, x1_ref, mod_ref, g_ref, b_ref, y_ref, o_ref,
                    buf, sem, *, tm):
    i = pl.program_id(0)
    n = pl.num_programs(0)
    nsub = x1_ref.shape[1] // LANES

    group = COMBINE_GROUP

    def issue_rows(p_ref, slot, r0):
        for r in range(group):
            for k in range(TOP_K):
                row = r0 + r
                src = pl.ds(pl.multiple_of(p_ref[row * TOP_K + k] * nsub, nsub), nsub)
                dst = pl.ds(pl.multiple_of(row * nsub, nsub), nsub)
                pltpu.make_async_copy(y_ref.at[src], buf.at[slot, k, dst],
                                      sem.at[slot]).start(priority=k % 2)

    def reduce_rows(slot, r0):
        rows = pl.ds(r0, group)
        w = w_ref[rows, :]
        f = jnp.zeros((group, nsub * LANES), F32)
        for k in range(TOP_K):
            yk = jnp.concatenate(
                [buf[slot, k, pl.ds(r0 * nsub + s, group, stride=nsub), :] for s in range(nsub)],
                axis=1)
            f = f + w[:, k:k + 1] * yk
        o_ref[rows, :] = f

    @pl.when(i == 0)
    def _():
        def body(j, c):
            issue_rows(pos_ref, 0, pl.multiple_of(j * group, group))
            return c
        lax.fori_loop(0, tm // group, body, 0)

    slot = i % 2
    for k in range(TOP_K):
        pltpu.make_async_copy(y_ref.at[pl.ds(0, tm * nsub)], buf.at[slot, k], sem.at[slot]).wait()

    @pl.when(i + 1 < n)
    def _():
        def body(j, c):
            r0 = pl.multiple_of(j * group, group)
            issue_rows(posn_ref, 1 - slot, r0)
            reduce_rows(slot, r0)
            return c
        lax.fori_loop(0, tm // group, body, 0)

    @pl.when(i + 1 == n)
    def _():
        def body(j, c):
            reduce_rows(slot, pl.multiple_of(j * group, group))
            return c
        lax.fori_loop(0, tm // group, body, 0)

    o_ref[...] = _layer_norm(DEEPNORM_ALPHA * x1_ref[...] + mod_ref[0, 5:6, :] * o_ref[...],
                             g_ref[...], b_ref[...])


def _combine(pos_flat, w, x1, mod3, g2, b2, y, seq):
    n, d = x1.shape
    tm = min(TM_ROWS, seq)
    nsteps = n // tm
    per_seq = seq // tm
    return pl.pallas_call(
        functools.partial(_combine_kernel, tm=tm),
        out_shape=jax.ShapeDtypeStruct((n, d), F32),
        grid=(nsteps,),
        in_specs=[pl.BlockSpec((tm * TOP_K,), lambda i: (i,), memory_space=pltpu.SMEM),
                  pl.BlockSpec((tm * TOP_K,), lambda i: (jnp.minimum(i + 1, nsteps - 1),),
                               memory_space=pltpu.SMEM),
                  pl.BlockSpec((tm, LANES), lambda i: (i, 0)),
                  pl.BlockSpec((tm, d), lambda i: (i, 0)),
                  pl.BlockSpec((1, 6, d), lambda i: (i // per_seq, 0, 0)),
                  pl.BlockSpec(g2.shape, lambda i: (0, 0)),
                  pl.BlockSpec(b2.shape, lambda i: (0, 0)),
                  pl.BlockSpec(memory_space=pl.ANY)],
        out_specs=pl.BlockSpec((tm, d), lambda i: (i, 0)),
        scratch_shapes=[pltpu.VMEM((2, TOP_K, tm * (d // LANES), LANES), F32),
                        pltpu.SemaphoreType.DMA((2,))],
        compiler_params=_params(("arbitrary",), 32),
    )(pos_flat, pos_flat, w, x1, mod3, g2, b2, y)


def _prep_w_uq(w_uq):
    r = w_uq.shape[0]
    w = w_uq.reshape(r, B_HEADS, B_QK_DIM)
    half = B_ROPE_DIM // 2
    rot = jnp.concatenate([-w[:, :, B_NOPE_DIM + half:], w[:, :, B_NOPE_DIM:B_NOPE_DIM + half]],
                          axis=-1).reshape(r, B_HEADS * B_ROPE_DIM)
    w = jnp.pad(w, ((0, 0), (0, 0), (0, LANES - B_QK_DIM))).reshape(r, B_HEADS * LANES)
    return jnp.concatenate([w, rot], axis=1).astype(BF16)


def _prep_w_ukv(w_ukv):
    r = w_ukv.shape[0]
    w = w_ukv.reshape(r, B_HEADS, B_NOPE_DIM + B_V_DIM)
    wk = jnp.pad(w[:, :, :B_NOPE_DIM], ((0, 0), (0, 0), (0, LANES - B_NOPE_DIM)))
    wv = w[:, :, B_NOPE_DIM:].reshape(r, B_HEADS // 2, 2 * B_V_DIM)
    wv = jnp.pad(wv, ((0, 0), (0, 0), (0, V_PAIR_LANES - 2 * B_V_DIM)))
    return (wk.reshape(r, B_HEADS * LANES).astype(BF16),
            wv.reshape(r, B_HEADS // 2 * V_PAIR_LANES).astype(BF16))


def _rope_tables(positions):
    half = B_ROPE_DIM // 2
    freqs = ROPE_THETA ** (-jnp.arange(half, dtype=F32) / half)
    ang = positions.astype(F32).reshape(1, -1) * freqs[:, None]
    cos, sin = jnp.cos(ang), jnp.sin(ang)
    z = jnp.zeros((B_ROPE_DIM, ang.shape[1]), F32)
    return jnp.concatenate([sin, sin, z, cos, cos, z], 0)


def kernel(x, c, positions, w_ada, b_ada, w_in, rms_q, w_uq, rms_kv, w_ukv, rel_bias,
           w_branch_a, w_branch_b, w_out, ln1_g, ln1_b, w_router, b_router,
           w_gate, b_gate, w_up, b_up, w_down, b_down, ln2_g, ln2_b):
    bsz, seq, d = x.shape
    n_tok = bsz * seq
    assert w_ada.shape[0] == DEPTH == 1
    row = lambda v: v.reshape(1, -1)

    mod3 = _ada(c, w_ada[0], b_ada[0]).reshape(bsz, 6, d)

    qk, va, cq, ckv, kr, ga, gb = _inproj(x, mod3, jnp.swapaxes(w_in, 1, 2))
    ya = _attn_a(qk, va, _bias_table(rel_bias[0], min(TQ_A, seq)))
    wk, wv = _prep_w_ukv(w_ukv[0])
    qm, km, vm = _mla_proj(cq, ckv, kr, _rope_tables(positions), row(rms_q[0]), row(rms_kv[0]),
                           _prep_w_uq(w_uq[0]), wk, wv)
    yb = _mla_attn(qm, km, vm)

    wr = jnp.pad(w_router[0], ((0, 0), (0, LANES - N_EXPERTS)))
    br = jnp.pad(b_router[0], (0, LANES - N_EXPERTS), constant_values=NEG_INF)
    x1, h2, code, wts, cnt = _merge(x, ya, yb, ga, gb, mod3,
                                    w_branch_a[0].astype(BF16), w_branch_b[0].astype(BF16),
                                    w_out[0].astype(BF16), row(ln1_g[0]), row(ln1_b[0]), wr, row(br))
    wts = wts.reshape(n_tok, LANES)

    tm = TM_MOE
    n_tiles = -(-(n_tok * TOP_K) // tm) + N_EXPERTS
    assert n_tiles <= MAX_TILES_PAD
    pos, tmap, meta = _route(code.reshape(n_tok, LANES), cnt, tm)
    pos_flat = pos[:, :TOP_K].reshape(-1)
    fill = meta[:3].reshape(-1)
    nsub = d // LANES
    xs = _dispatch(pos_flat, fill, h2.reshape(n_tok * nsub, LANES), n_tiles * tm, nsub, tm)
    y = _moe(tmap[:n_tiles, 0], meta[0, :1], xs, w_gate[0], b_gate[0], w_up[0], b_up[0],
             w_down[0], b_down[0], tm)
    out = _combine(pos_flat, wts, x1.reshape(n_tok, d), mod3, row(ln2_g[0]), row(ln2_b[0]), y, seq)
    return out.reshape(bsz, seq, d)
```

```python
import functools

import jax
import jax.numpy as jnp
from jax import lax
from jax.experimental import pallas as pl
from jax.experimental.pallas import tpu as pltpu

F32 = jnp.float32
BF16 = jnp.bfloat16

CHUNK = 64
A_HEADS = 8
A_HEAD_DIM = 64
A_LEFT_CHUNKS = 8
A_MAX_REL = 128
A_WIDTH = A_HEADS * A_HEAD_DIM
B_HEADS = 8
B_NOPE_DIM = 64
B_ROPE_DIM = 32
B_V_DIM = 64
B_Q_LORA = 384
B_KV_LORA = 256
B_QK_DIM = B_NOPE_DIM + B_ROPE_DIM
ROPE_THETA = 10000.0
N_EXPERTS = 32
TOP_K = 4
SWIGLU_LIMIT = 7.0
SWIGLU_ALPHA = 1.702
DEPTH = 1
DEEPNORM_ALPHA = (2.0 * DEPTH) ** 0.25
LN_EPS = 1e-5
RMS_EPS = 1e-6
NEG_INF = -1e30
LOG2_E = 1.4426950408889634

LANES = 128
MXU_COLS = 256
V_PAIR_LANES = MXU_COLS

TM_PROJ = 512
TQ_A = 256
A_QBLOCKS = 8
TQ_B = 512
TM_MERGE = 1024
MERGE_SPLIT = 2
TR_ROUTE = 1024
TM_MOE = 512
TM_ROWS = 256
COMBINE_GROUP = 8
MAX_TILES_PAD = 512


def _params(sem, vmem_mb=None):
    return pltpu.CompilerParams(
        dimension_semantics=sem,
        vmem_limit_bytes=None if vmem_mb is None else vmem_mb << 20)


def _dot(a, b):
    return jnp.dot(a, b, preferred_element_type=F32)


def _dot_nt(a, b):
    return lax.dot_general(a, b, (((1,), (1,)), ((), ())), preferred_element_type=F32)


def _store_token_tiles(ref, x):
    rows, d = x.shape
    nsub = d // LANES
    for s in range(nsub):
        ref[pl.ds(s, rows, stride=nsub), :] = x[:, s * LANES:(s + 1) * LANES]


def _load_token_tiles(ref, nsub):
    rows = ref.shape[0] // nsub
    return jnp.concatenate([ref[pl.ds(s, rows, stride=nsub), :] for s in range(nsub)], axis=1)


def _layer_norm(x, g, b):
    mu = jnp.mean(x, axis=-1, keepdims=True)
    xc = x - mu
    var = jnp.mean(xc * xc, axis=-1, keepdims=True)
    return xc * lax.rsqrt(var + LN_EPS) * g + b


def _ada_kernel(c_ref, w_ref, b_ref, o_ref):
    c = c_ref[...]
    sc = c * jax.nn.sigmoid(c)
    w = w_ref[...]
    s_hi, w_hi = sc.astype(BF16), w.astype(BF16)
    s_lo = (sc - s_hi.astype(F32)).astype(BF16)
    w_lo = (w - w_hi.astype(F32)).astype(BF16)
    o_ref[...] = _dot(s_hi, w_hi) + _dot(s_hi, w_lo) + _dot(s_lo, w_hi) + b_ref[...]


def _ada(c, w_ada, b_ada):
    bsz, d = c.shape
    m = w_ada.shape[1]
    tn = 1024
    return pl.pallas_call(
        _ada_kernel,
        out_shape=jax.ShapeDtypeStruct((bsz, m), F32),
        grid=(m // tn,),
        in_specs=[pl.BlockSpec((bsz, d), lambda j: (0, 0)),
                  pl.BlockSpec((d, tn), lambda j: (0, j)),
                  pl.BlockSpec((1, tn), lambda j: (0, j))],
        out_specs=pl.BlockSpec((bsz, tn), lambda j: (0, j)),
        compiler_params=_params(("arbitrary",)),
    )(c, w_ada, b_ada.reshape(1, m))


def _inproj_kernel(x_ref, mod_ref, win_hbm, qk_ref, va_ref, cq_ref, ckv_ref, kr_ref, ga_ref, gb_ref,
                   wf_s, wqk_ref, wv_ref, wcq_ref, wckv_ref, wkr_ref, wga_ref, wgb_ref, sem):
    @pl.when((pl.program_id(0) == 0) & (pl.program_id(1) == 0))
    def _():
        copy = pltpu.make_async_copy(win_hbm.at[0], wf_s, sem)
        copy.start()
        copy.wait()
        d = wf_s.shape[1]
        chunk = 2 * LANES
        off = 0
        for dst, n in ((wqk_ref, 2 * A_WIDTH), (wv_ref, A_WIDTH), (wcq_ref, B_Q_LORA),
                       (wckv_ref, B_KV_LORA), (None, B_ROPE_DIM), (wga_ref, d), (wgb_ref, d)):
            if dst is None:
                blk = wf_s[off:off + LANES, :].T
                lane = lax.broadcasted_iota(jnp.int32, blk.shape, 1)
                blk = pltpu.roll(jnp.where(lane < n, blk, 0.0), B_NOPE_DIM, 1)
                wkr_ref[...] = blk.astype(BF16)
            else:
                for c0 in range(0, n, chunk):
                    cn = min(chunk, n - c0)
                    blk = wf_s[off + c0:off + c0 + cn, :].T
                    if dst is wqk_ref and c0 < A_WIDTH:
                        blk = blk * (A_HEAD_DIM ** -0.5 * LOG2_E)
                    dst[:, c0:c0 + cn] = blk.astype(BF16)
            off += n

    m = mod_ref[0]
    h = (x_ref[0] * (1.0 + m[1:2]) + m[0:1]).astype(BF16)
    for w_ref, ref in ((wqk_ref, qk_ref), (wcq_ref, cq_ref), (wckv_ref, ckv_ref),
                       (wkr_ref, kr_ref), (wga_ref, ga_ref), (wgb_ref, gb_ref)):
        ref[0] = _dot(h, w_ref[...]).astype(ref.dtype)
    v = _dot(h, wv_ref[...]).astype(BF16)
    ones_blk = jnp.ones((v.shape[0], LANES), BF16)
    parts = []
    for hp in range(A_WIDTH // LANES):
        parts += [v[:, hp * LANES:(hp + 1) * LANES], ones_blk]
    va_ref[0] = jnp.concatenate(parts, axis=1)


def _inproj(x, mod3, w_in):
    bsz, seq, d = x.shape
    tm = min(TM_PROJ, seq)
    widths = (2 * A_WIDTH, A_WIDTH // LANES * V_PAIR_LANES, B_Q_LORA, B_KV_LORA, LANES, d, d)
    dtypes = (BF16, BF16, BF16, BF16, F32, BF16, BF16)
    w_widths = (2 * A_WIDTH, A_WIDTH, B_Q_LORA, B_KV_LORA, LANES, d, d)
    assert w_in.shape[0] == 1 and w_in.shape[1] == sum(w_widths) - LANES + B_ROPE_DIM
    return pl.pallas_call(
        _inproj_kernel,
        out_shape=[jax.ShapeDtypeStruct((bsz, seq, n), dt) for n, dt in zip(widths, dtypes)],
        grid=(bsz, seq // tm),
        in_specs=[pl.BlockSpec((1, tm, d), lambda b, i: (b, i, 0)),
                  pl.BlockSpec((1, 6, d), lambda b, i: (b, 0, 0)),
                  pl.BlockSpec(memory_space=pl.ANY)],
        out_specs=[pl.BlockSpec((1, tm, n), lambda b, i: (b, i, 0)) for n in widths],
        scratch_shapes=[pltpu.VMEM(w_in.shape[1:], F32)]
                       + [pltpu.VMEM((d, n), BF16) for n in w_widths]
                       + [pltpu.SemaphoreType.DMA(())],
        compiler_params=_params(("arbitrary", "arbitrary"), 56),
    )(x, mod3, w_in)


def _attn_a_kernel(q_ref, k_ref, v_ref, bias_ref, o_ref, *, tq, nd, nblk, steps):
    band = nd * tq
    lane = lax.broadcasted_iota(jnp.int32, (1, LANES), 1)

    def shifted(base, missing):
        cut = missing * tq
        if cut == 0:
            return base
        return jnp.concatenate([base[:, cut:], jnp.full((tq, cut), NEG_INF, F32)], axis=1)

    def bias_for(hh, i):
        base = bias_ref[hh]
        if isinstance(i, int):
            return shifted(base, max(nd - 1 - i, 0))
        out = base
        for first in range(nd - 1):
            out = jnp.where(i == first, shifted(base, nd - 1 - first), out)
        return out

    for sub in range(nblk):
        if steps == 1:
            i = sub
            rows = slice(max(i - (nd - 1), 0) * tq, max(i - (nd - 1), 0) * tq + band)
        else:
            i = pl.program_id(2) * nblk + sub
            rows = pl.ds(pl.multiple_of(jnp.maximum(i - (nd - 1), 0) * tq, tq), band)
        q = q_ref[0, sub * tq:(sub + 1) * tq, :]
        k = k_ref[0, rows, :]
        v = v_ref[0, rows, :]
        zero = jnp.zeros_like(k)
        kk = jnp.concatenate([jnp.where(lane < A_HEAD_DIM, k, zero),
                              jnp.where(lane >= A_HEAD_DIM, k, zero)], axis=0)
        s = _dot_nt(q, kk)
        ps = []
        for hh in range(2):
            sh = s[:, hh * band:(hh + 1) * band] + bias_for(hh, i)
            m = sh.max(axis=-1, keepdims=True)
            ps.append(jnp.exp2(sh - m).astype(BF16))
        acc = _dot(jnp.concatenate(ps, axis=0), v)
        out = acc[:, :LANES] / acc[:, LANES:]
        o_ref[0, sub * tq:(sub + 1) * tq, :] = jnp.where(
            lane < A_HEAD_DIM, out[:tq], out[tq:]).astype(o_ref.dtype)


def _attn_a(qk, va, bias_tab):
    bsz, seq, _ = qk.shape
    tq = min(TQ_A, seq)
    nd = bias_tab.shape[2] // tq
    assert seq >= nd * tq
    ncb = A_WIDTH // LANES
    nblk = min(A_QBLOCKS, seq // tq)
    steps = seq // (nblk * tq)
    return pl.pallas_call(
        functools.partial(_attn_a_kernel, tq=tq, nd=nd, nblk=nblk, steps=steps),
        out_shape=jax.ShapeDtypeStruct((bsz, seq, A_WIDTH), BF16),
        grid=(ncb, bsz, steps),
        in_specs=[pl.BlockSpec((1, nblk * tq, LANES), lambda hp, b, i: (b, i, hp)),
                  pl.BlockSpec((1, seq, LANES), lambda hp, b, i: (b, 0, ncb + hp)),
                  pl.BlockSpec((1, seq, V_PAIR_LANES), lambda hp, b, i: (b, 0, hp)),
                  pl.BlockSpec((2, tq, nd * tq), lambda hp, b, i: (hp, 0, 0))],
        out_specs=pl.BlockSpec((1, nblk * tq, LANES), lambda hp, b, i: (b, i, hp)),
        compiler_params=_params(("arbitrary",) * 3, 40),
    )(qk, qk, va, bias_tab)


def _bias_table(rel_bias, tq):
    nd = (A_LEFT_CHUNKS * CHUNK + tq - 1) // tq + 1
    width = nd * tq
    period = width + tq
    u = jnp.arange(period)
    dist = jnp.where(u < width, (nd - 1) * tq - u, (nd - 1) * tq + period - u)
    vec = rel_bias[:, jnp.clip(dist, -A_MAX_REL, A_MAX_REL) + A_MAX_REL].astype(F32)
    toep = jnp.tile(vec, (1, tq))[:, :tq * (period - 1)].reshape(-1, tq, period - 1)[:, :, :width]
    qi = jnp.arange(tq)[:, None]
    x = jnp.arange(width)[None, :]
    dchunk = qi // CHUNK - x // CHUNK + (nd - 1) * (tq // CHUNK)
    valid = (dchunk >= 0) & (dchunk <= A_LEFT_CHUNKS)
    return jnp.where(valid[None], toep * LOG2_E, NEG_INF)


def _mla_proj_kernel(cq_ref, ckv_ref, kr_ref, tab_ref, rq_ref, rkv_ref,
                     wq_ref, wk_ref, wv_ref, q_out, k_out, v_out):
    def rms(x, g):
        xf = x.astype(F32)
        ms = jnp.mean(xf * xf, axis=-1, keepdims=True)
        return (xf * lax.rsqrt(ms + RMS_EPS) * g).astype(BF16)

    cqn = rms(cq_ref[0], rq_ref[...])
    ckvn = rms(ckv_ref[0], rkv_ref[...])
    q = _dot(cqn, wq_ref[...])
    kn = _dot(ckvn, wk_ref[...])
    vlane = lax.broadcasted_iota(jnp.int32, (1, v_out.shape[-1]), 1)
    ones_lane = jnp.where(vlane % V_PAIR_LANES >= 2 * B_V_DIM, 1.0, 0.0)
    v_out[0] = (_dot(ckvn, wv_ref[...]) + ones_lane).astype(v_out.dtype)
    tab = tab_ref[...].T
    tlane = lax.broadcasted_iota(jnp.int32, tab.shape, 1)
    rope_end = B_NOPE_DIM + B_ROPE_DIM
    in_rope = (tlane >= B_NOPE_DIM) & (tlane < rope_end)
    cos_t = jnp.where(tlane < B_NOPE_DIM, 1.0, jnp.where(in_rope, tab, 0.0))
    sin_t = jnp.where(in_rope, pltpu.roll(tab, B_NOPE_DIM, 1), 0.0)
    first = tlane < B_NOPE_DIM + B_ROPE_DIM // 2
    sin_a = jnp.where(first, -sin_t, 0.0)
    sin_b = jnp.where(first, 0.0, sin_t)

    def rope(x, tables):
        c, sa, sb = tables
        return (x * c + pltpu.roll(x, LANES - B_ROPE_DIM // 2, 1) * sa
                + pltpu.roll(x, B_ROPE_DIM // 2, 1) * sb)

    kpe = rope(kr_ref[0], (cos_t, sin_a, sin_b))
    scale = B_QK_DIM ** -0.5 * LOG2_E
    cos_q, sin_q = cos_t * scale, sin_t * scale
    nq = B_HEADS * LANES
    per_block = LANES // B_ROPE_DIM
    for h in range(B_HEADS):
        sl = slice(h * LANES, (h + 1) * LANES)
        blk = nq + (h // per_block) * LANES
        partner = q[:, blk:blk + LANES]
        shift = (B_NOPE_DIM - B_ROPE_DIM * (h % per_block)) % LANES
        if shift:
            partner = pltpu.roll(partner, shift, 1)
        q_out[0, :, sl] = (q[:, sl] * cos_q + partner * sin_q).astype(q_out.dtype)
        k_out[0, :, sl] = (kn[:, sl] + kpe).astype(k_out.dtype)


def _mla_proj(cq, ckv, kr, rope_tab, rms_q, rms_kv, wq, wk, wv):
    bsz, seq, _ = cq.shape
    tm = min(TM_PROJ, seq)
    tok = lambda n: pl.BlockSpec((1, tm, n), lambda b, i: (b, i, 0))
    full = lambda a: pl.BlockSpec(a.shape, lambda b, i: (0,) * a.ndim)
    return pl.pallas_call(
        _mla_proj_kernel,
        out_shape=[jax.ShapeDtypeStruct((bsz, seq, B_HEADS * LANES), BF16),
                   jax.ShapeDtypeStruct((bsz, seq, B_HEADS * LANES), BF16),
                   jax.ShapeDtypeStruct((bsz, seq, wv.shape[1]), BF16)],
        grid=(bsz, seq // tm),
        in_specs=[tok(B_Q_LORA), tok(B_KV_LORA), tok(LANES),
                  pl.BlockSpec((LANES, tm), lambda b, i: (0, b * (seq // tm) + i)),
                  full(rms_q), full(rms_kv), full(wq), full(wk), full(wv)],
        out_specs=[tok(B_HEADS * LANES), tok(B_HEADS * LANES), tok(wv.shape[1])],
        compiler_params=_params(("arbitrary", "arbitrary"), 32),
    )(cq, ckv, kr, rope_tab, rms_q, rms_kv, wq, wk, wv)


def _mla_attn_kernel(q_ref, k_ref, v_ref, o_ref, s_a, s_b, m_s, acc_s, *, t):
    lane = lax.broadcasted_iota(jnp.int32, (1, LANES), 1)
    nq = q_ref.shape[1] // t
    pairs = [(qi, kj) for qi in range(nq) for kj in range(qi + 1)]
    bufs = (s_a, s_b)

    def produce(qi, kj, s_ref):
        for hh in range(2):
            q = q_ref[0, qi * t:(qi + 1) * t, hh * LANES:(hh + 1) * LANES]
            k = k_ref[0, kj * t:(kj + 1) * t, hh * LANES:(hh + 1) * LANES]
            s_ref[hh] = _dot_nt(q, k)

    def consume(qi, kj, s_ref):
        v = v_ref[0, kj * t:(kj + 1) * t, :]
        outs = []
        for hh in range(2):
            s = s_ref[hh]
            if kj == qi:
                row_chunk = lax.broadcasted_iota(jnp.int32, (t, t), 0) // CHUNK
                col_chunk = lax.broadcasted_iota(jnp.int32, (t, t), 1) // CHUNK
                s = jnp.where(col_chunk <= row_chunk, s, NEG_INF)
            row_max = jnp.broadcast_to(s.max(axis=-1, keepdims=True), (t, LANES))
            if kj == 0:
                m_new = row_max
                acc = _dot(jnp.exp2(s - jnp.tile(m_new, (1, t // LANES))).astype(BF16), v)
            else:
                m_old = m_s[hh]
                m_new = jnp.maximum(m_old, row_max)
                p = jnp.exp2(s - jnp.tile(m_new, (1, t // LANES))).astype(BF16)
                alpha = jnp.exp2(m_old - m_new)
                acc = jnp.tile(alpha, (1, V_PAIR_LANES // LANES)) * acc_s[hh] + _dot(p, v)
            if kj == qi:
                outs.append(acc[:, :LANES] / acc[:, LANES:])
            else:
                acc_s[hh] = acc
                m_s[hh] = m_new
        if kj == qi:
            o_ref[0, qi * t:(qi + 1) * t, :] = jnp.where(
                lane < B_V_DIM, outs[0], outs[1]).astype(o_ref.dtype)

    produce(*pairs[0], bufs[0])
    for n, pair in enumerate(pairs):
        if n + 1 < len(pairs):
            produce(*pairs[n + 1], bufs[(n + 1) % 2])
        consume(*pair, bufs[n % 2])


def _mla_attn(qm, km, vm):
    bsz, seq, _ = qm.shape
    t = min(TQ_B, seq)
    return pl.pallas_call(
        functools.partial(_mla_attn_kernel, t=t),
        out_shape=jax.ShapeDtypeStruct((bsz, seq, B_HEADS * B_V_DIM), BF16),
        grid=(bsz, B_HEADS // 2),
        in_specs=[pl.BlockSpec((1, seq, 2 * LANES), lambda b, hp: (b, 0, hp)),
                  pl.BlockSpec((1, seq, 2 * LANES), lambda b, hp: (b, 0, hp)),
                  pl.BlockSpec((1, seq, V_PAIR_LANES), lambda b, hp: (b, 0, hp))],
        out_specs=pl.BlockSpec((1, seq, LANES), lambda b, hp: (b, 0, hp)),
        scratch_shapes=[pltpu.VMEM((2, t, t), F32), pltpu.VMEM((2, t, t), F32),
                        pltpu.VMEM((2, t, LANES), F32),
                        pltpu.VMEM((2, t, V_PAIR_LANES), F32)],
        compiler_params=_params(("arbitrary",) * 2, 40),
    )(qm, km, vm)


def _merge_kernel(x_ref, ya_ref, yb_ref, ga_ref, gb_ref, mod_ref, wa_ref, wb_ref, wo_ref,
                  g1_ref, b1_ref, wrh_ref, wrl_ref, br_ref, x1_ref, h2_ref, code_ref, wts_ref,
                  cnt_ref):
    @pl.when((pl.program_id(0) == 0) & (pl.program_id(1) == 0))
    def _():
        cnt_ref[...] = jnp.zeros_like(cnt_ref)

    tm = x_ref.shape[1]
    half = tm // MERGE_SPLIT
    for part in range(MERGE_SPLIT):
        _merge_rows(slice(part * half, (part + 1) * half), x_ref, ya_ref, yb_ref, ga_ref, gb_ref,
                    mod_ref, wa_ref, wb_ref, wo_ref, g1_ref, b1_ref, wrh_ref, wrl_ref, br_ref,
                    x1_ref, h2_ref, code_ref, wts_ref, cnt_ref)


def _merge_rows(rows, x_ref, ya_ref, yb_ref, ga_ref, gb_ref, mod_ref, wa_ref, wb_ref, wo_ref,
                g1_ref, b1_ref, wrh_ref, wrl_ref, br_ref, x1_ref, h2_ref, code_ref, wts_ref,
                cnt_ref):
    m = mod_ref[0]
    a = _dot(ya_ref[0, rows, :], wa_ref[...])
    b = _dot(yb_ref[0, rows, :], wb_ref[...])
    merged = (jax.nn.sigmoid(ga_ref[0, rows, :].astype(F32)) * a
              + jax.nn.sigmoid(gb_ref[0, rows, :].astype(F32)) * b)
    o = _dot(merged.astype(BF16), wo_ref[...])
    x1 = _layer_norm(DEEPNORM_ALPHA * x_ref[0, rows, :] + m[2:3] * o, g1_ref[...], b1_ref[...])
    h2 = x1 * (1.0 + m[4:5]) + m[3:4]
    x1_ref[0, rows, :] = x1
    nsub = h2.shape[1] // LANES
    _store_token_tiles(h2_ref.at[0, pl.ds(rows.start * nsub, (rows.stop - rows.start) * nsub)], h2)
    h_hi = h2.astype(BF16)
    h_lo = (h2 - h_hi.astype(F32)).astype(BF16)
    work = (_dot(h_hi, wrh_ref[...]) + _dot(h_hi, wrl_ref[...]) + _dot(h_lo, wrh_ref[...])
            + br_ref[...])
    lane = lax.broadcasted_iota(jnp.int32, work.shape, 1)
    code = jnp.zeros(work.shape, F32)
    vals = []
    for k in range(TOP_K):
        top = work.max(axis=-1, keepdims=True)
        idx = jnp.where(work == top, lane, LANES).min(axis=-1, keepdims=True)
        sel = lane == idx
        work = jnp.where(sel, -jnp.inf, work)
        code = jnp.where(sel, k + 1.0, code)
        vals.append(top)
    den = sum(jnp.exp(v - vals[0]) for v in vals)
    wts = jnp.zeros(work.shape, F32)
    for k in range(TOP_K):
        wts = jnp.where(lane == k, jnp.exp(vals[k] - vals[0]) / den, wts)
    code_ref[0, rows, :] = code
    wts_ref[0, rows, :] = wts
    picked = jnp.where(code > 0.0, 1.0, 0.0)
    cnt_ref[...] += jnp.broadcast_to(picked.sum(axis=0, keepdims=True), cnt_ref.shape)


def _merge(x, ya, yb, ga, gb, mod3, wa, wb, wo, g1, b1, wr, br):
    wr_hi = wr.astype(BF16)
    wr_lo = (wr - wr_hi.astype(F32)).astype(BF16)
    bsz, seq, d = x.shape
    tm = min(TM_MERGE, seq)
    tok = lambda n: pl.BlockSpec((1, tm, n), lambda b, i: (b, i, 0))
    full = lambda a: pl.BlockSpec(a.shape, lambda b, i: (0,) * a.ndim)
    return pl.pallas_call(
        _merge_kernel,
        out_shape=[jax.ShapeDtypeStruct((bsz, seq, d), F32),
                   jax.ShapeDtypeStruct((bsz, seq * (d // LANES), LANES), F32),
                   jax.ShapeDtypeStruct((bsz, seq, LANES), F32),
                   jax.ShapeDtypeStruct((bsz, seq, LANES), F32),
                   jax.ShapeDtypeStruct((8, LANES), F32)],
        grid=(bsz, seq // tm),
        in_specs=[tok(d), tok(A_WIDTH), tok(B_HEADS * B_V_DIM), tok(d), tok(d),
                  pl.BlockSpec((1, 6, d), lambda b, i: (b, 0, 0)),
                  full(wa), full(wb), full(wo), full(g1), full(b1), full(wr_hi), full(wr_lo),
                  full(br)],
        out_specs=[tok(d),
                   pl.BlockSpec((1, tm * (d // LANES), LANES), lambda b, i: (b, i, 0)),
                   tok(LANES), tok(LANES), pl.BlockSpec((8, LANES), lambda b, i: (0, 0))],
        compiler_params=_params(("arbitrary", "arbitrary"), 48),
    )(x, ya, yb, ga, gb, mod3, wa, wb, wo, g1, b1, wr_hi, wr_lo, br)


def _route_kernel(code_ref, cnt_ref, pos_ref, tmap_ref, nv_ref, carry_s, pstart_s, *, tr, tm):
    i = pl.program_id(0)
    lane = lax.broadcasted_iota(jnp.int32, (tr, LANES), 1)
    code = code_ref[...]
    picked = jnp.where(code > 0.0, 1.0, 0.0)
    tile_count = jnp.broadcast_to(picked.sum(axis=0, keepdims=True), (8, LANES))

    @pl.when(i == 0)
    def _():
        lane8 = lax.broadcasted_iota(jnp.int32, (8, LANES), 1)
        padded = jnp.floor((cnt_ref[...] + (tm - 1)) * (1.0 / tm)) * tm
        cum = padded
        shift = 1
        while shift < N_EXPERTS:
            cum = cum + jnp.where(lane8 >= shift, pltpu.roll(cum, shift, 1), 0.0)
            shift *= 2
        pstart_s[...] = cum - padded
        carry_s[...] = jnp.zeros_like(carry_s)
        nt = tmap_ref.shape[0]
        tile_start = (lax.broadcasted_iota(jnp.int32, (nt, LANES), 0) * tm).astype(F32)
        lane_t = lax.broadcasted_iota(jnp.int32, (nt, LANES), 1)
        done = jnp.where((lane_t < N_EXPERTS) & (cum[0:1] <= tile_start), 1.0, 0.0)
        te = jnp.minimum(done.sum(axis=-1, keepdims=True), N_EXPERTS - 1.0)
        tmap_ref[...] = jnp.broadcast_to(te, (nt, LANES)).astype(jnp.int32)
        total = jnp.where(lane8 == N_EXPERTS - 1, cum, 0.0).sum(axis=-1, keepdims=True)
        row8 = lax.broadcasted_iota(jnp.int32, (8, LANES), 0)
        meta = jnp.where(row8 == 0, jnp.broadcast_to(total * (1.0 / tm), (8, LANES)),
                         jnp.where(row8 == 1, cum - padded + cnt_ref[...], cum))
        nv_ref[...] = meta.astype(jnp.int32)

    r = lax.broadcasted_iota(jnp.int32, (tr, tr), 0)
    c = lax.broadcasted_iota(jnp.int32, (tr, tr), 1)
    tri = jnp.where(c < r, 1.0, 0.0).astype(BF16)
    base = _dot(tri, picked.astype(BF16)) + (carry_s[0:1] + pstart_s[0:1])
    carry_s[...] += tile_count
    pos = jnp.zeros((tr, LANES), F32)
    for k in range(TOP_K):
        pk = jnp.where(code == k + 1.0, base, 0.0).sum(axis=-1, keepdims=True)
        pos = jnp.where(lane == k, pk, pos)
    pos_ref[...] = pos.astype(jnp.int32)


def _route(code, cnt, tm):
    n = code.shape[0]
    tr = min(TR_ROUTE, n)
    return pl.pallas_call(
        functools.partial(_route_kernel, tr=tr, tm=tm),
        out_shape=[jax.ShapeDtypeStruct((n, LANES), jnp.int32),
                   jax.ShapeDtypeStruct((MAX_TILES_PAD, LANES), jnp.int32),
                   jax.ShapeDtypeStruct((8, LANES), jnp.int32)],
        grid=(n // tr,),
        in_specs=[pl.BlockSpec((tr, LANES), lambda i: (i, 0)),
                  pl.BlockSpec((8, LANES), lambda i: (0, 0))],
        out_specs=[pl.BlockSpec((tr, LANES), lambda i: (i, 0)),
                   pl.BlockSpec((MAX_TILES_PAD, LANES), lambda i: (0, 0)),
                   pl.BlockSpec((8, LANES), lambda i: (0, 0))],
        scratch_shapes=[pltpu.VMEM((8, LANES), F32)] * 2,
        compiler_params=_params(("arbitrary",)),
    )(code, cnt)


def _dispatch_kernel(pos_ref, fill_ref, h_ref, xs_ref, ztile_s, sem, fill_sem, tail_sem,
                     *, tm, nsub, tm_moe):
    def tile(ref, r):
        return ref.at[pl.ds(pl.multiple_of(r * nsub, nsub), nsub)]

    @pl.when(pl.program_id(0) == 0)
    def _():
        ztile_s[...] = jnp.zeros_like(ztile_s)

        def tail_copy(t):
            rows = tm_moe * nsub
            return pltpu.make_async_copy(
                ztile_s, xs_ref.at[pl.ds(pl.multiple_of(t * rows, rows), rows)], tail_sem)

        sizes = [1 << b for b in reversed(range(tm_moe.bit_length() - 1))]

        def fill_pieces(e, act):
            lo, hi = fill_ref[LANES + e], fill_ref[2 * LANES + e]
            at = lo
            for size in sizes:
                has = ((hi - lo) & size) != 0
                copy = pltpu.make_async_copy(
                    ztile_s.at[pl.ds(0, size * nsub)],
                    xs_ref.at[pl.ds(pl.multiple_of(at * nsub, nsub), size * nsub)], fill_sem)
                pl.when(has)(functools.partial(act, copy))
                at = at + jnp.where(has, size, 0)

        def fill_start(e, c):
            fill_pieces(e, lambda copy: copy.start())
            return c

        def fill_wait(e, c):
            fill_pieces(e, lambda copy: copy.wait())
            return c

        n_tiles = xs_ref.shape[0] // (tm_moe * nsub)

        def tail_start(t, c):
            tail_copy(t).start()
            return c

        def tail_drain(_, c):
            tail_copy(0).wait()
            return c

        lax.fori_loop(0, N_EXPERTS, fill_start, 0)
        lax.fori_loop(fill_ref[0], n_tiles, tail_start, 0)
        lax.fori_loop(0, N_EXPERTS, fill_wait, 0)
        lax.fori_loop(fill_ref[0], n_tiles, tail_drain, 0)

    def issue(j, c):
        for r in range(COMBINE_GROUP):
            row = pl.multiple_of(j * COMBINE_GROUP, COMBINE_GROUP) + r
            for k in range(TOP_K):
                pltpu.make_async_copy(tile(h_ref, row), tile(xs_ref, pos_ref[row * TOP_K + k]),
                                      sem).start(priority=k % 2)
        return c

    lax.fori_loop(0, tm // COMBINE_GROUP, issue, 0)
    for _ in range(TOP_K):
        pltpu.make_async_copy(h_ref, xs_ref.at[pl.ds(0, tm * nsub)], sem).wait()


def _dispatch(pos_flat, fill, h2, cap, nsub, tm_moe):
    n = h2.shape[0] // nsub
    tm = min(TM_ROWS, n)
    return pl.pallas_call(
        functools.partial(_dispatch_kernel, tm=tm, nsub=nsub, tm_moe=tm_moe),
        out_shape=jax.ShapeDtypeStruct((cap * nsub, LANES), F32),
        grid=(n // tm,),
        in_specs=[pl.BlockSpec((tm * TOP_K,), lambda i: (i,), memory_space=pltpu.SMEM),
                  pl.BlockSpec(fill.shape, lambda i: (0,), memory_space=pltpu.SMEM),
                  pl.BlockSpec((tm * nsub, LANES), lambda i: (i, 0))],
        out_specs=pl.BlockSpec(memory_space=pl.ANY),
        scratch_shapes=[pltpu.VMEM((tm_moe * nsub, LANES), F32),
                        pltpu.SemaphoreType.DMA(()), pltpu.SemaphoreType.DMA(()),
                        pltpu.SemaphoreType.DMA(())],
        compiler_params=_params(("arbitrary",)),
    )(pos_flat, fill, h2)


def _moe_kernel(te_ref, nv_ref, x_ref, bg_ref, bu_ref, bd_ref, wg_hbm, wu_hbm, wd_hbm,
                y_ref, wf_s, wg_s, wu_s, wd_s, slot_s, sem):
    i = pl.program_id(0)
    nv = nv_ref[0]
    last = te_ref.shape[0] - 1

    def weight_copies(e, slot):
        return [pltpu.make_async_copy(w.at[e], wf_s.at[slot, j], sem.at[slot])
                for j, w in enumerate((wg_hbm, wu_hbm, wd_hbm))]

    @pl.when(i < nv)
    def _():
        e = te_ref[i]

        @pl.when(i == 0)
        def _():
            slot_s[0] = 0
            for c in weight_copies(e, 0):
                c.start()

        @pl.when((i == 0) | (e != te_ref[jnp.maximum(i - 1, 0)]))
        def _():
            slot = slot_s[0]
            for c in weight_copies(e, slot):
                c.wait()
            nxt = lax.while_loop(lambda j: (j < nv) & (te_ref[jnp.minimum(j, last)] == e),
                                 lambda j: j + 1, i + 1)

            @pl.when(nxt < nv)
            def _():
                for c in weight_copies(te_ref[jnp.minimum(nxt, last)], 1 - slot):
                    c.start()

            wg_s[...] = wf_s[slot, 0].astype(BF16)
            wu_s[...] = wf_s[slot, 1].astype(BF16)
            wd_s[...] = wf_s[slot, 2].astype(BF16)
            slot_s[0] = 1 - slot

        xb = _load_token_tiles(x_ref, wg_s.shape[0] // LANES).astype(BF16)
        g = _dot(xb, wg_s[...]) + bg_ref[0]
        u = _dot(xb, wu_s[...]) + bu_ref[0]
        g = jnp.minimum(g, SWIGLU_LIMIT)
        u = jnp.clip(u, -SWIGLU_LIMIT, SWIGLU_LIMIT)
        h = (u + 1.0) * (g * jax.nn.sigmoid(SWIGLU_ALPHA * g))
        _store_token_tiles(y_ref, _dot(h.astype(BF16), wd_s[...]) + bd_ref[0])

    @pl.when(i >= nv)
    def _():
        y_ref[...] = jnp.zeros_like(y_ref)


def _moe(tile_expert, n_valid, xs, w_gate, b_gate, w_up, b_up, w_down, b_down, tm):
    e, d, f = w_gate.shape
    assert d == f
    nsub = d // LANES
    cap = xs.shape[0] // nsub
    n_tiles = cap // tm

    def tile(i, te, nv):
        return jnp.minimum(i, nv[0] - 1)

    exp3 = lambda i, te, nv: (te[tile(i, te, nv)], 0, 0)
    grid_spec = pltpu.PrefetchScalarGridSpec(
        num_scalar_prefetch=2,
        grid=(n_tiles,),
        in_specs=[pl.BlockSpec((tm * nsub, LANES), lambda i, te, nv: (tile(i, te, nv), 0)),
                  pl.BlockSpec((1, 1, f), exp3), pl.BlockSpec((1, 1, f), exp3),
                  pl.BlockSpec((1, 1, d), exp3),
                  pl.BlockSpec(memory_space=pl.ANY), pl.BlockSpec(memory_space=pl.ANY),
                  pl.BlockSpec(memory_space=pl.ANY)],
        out_specs=pl.BlockSpec((tm * nsub, LANES), lambda i, te, nv: (i, 0)),
        scratch_shapes=[pltpu.VMEM((2, 3, d, f), F32),
                        pltpu.VMEM((d, f), BF16), pltpu.VMEM((d, f), BF16), pltpu.VMEM((f, d), BF16),
                        pltpu.SMEM((1,), jnp.int32), pltpu.SemaphoreType.DMA((2,))],
    )
    return pl.pallas_call(
        _moe_kernel,
        out_shape=jax.ShapeDtypeStruct((cap * nsub, LANES), F32),
        grid_spec=grid_spec,
        compiler_params=_params(("arbitrary",), 56),
    )(tile_expert, n_valid, xs, b_gate.reshape(e, 1, f), b_up.reshape(e, 1, f),
      b_down.reshape(e, 1, d), w_gate, w_up, w_down)


def _combine_kernel(pos_ref, posn_ref, w_ref, x1_ref, mod_ref, g_ref, b_ref, y_ref, o_ref,
                    buf, sem, *, tm):
    i = pl.program_id(0)
    n = pl.num_programs(0)
    nsub = x1_ref.shape[1] // LANES

    group = COMBINE_GROUP

    def issue_rows(p_ref, slot, r0):
        for r in range(group):
            for k in range(TOP_K):
                row = r0 + r
                src = pl.ds(pl.multiple_of(p_ref[row * TOP_K + k] * nsub, nsub), nsub)
                dst = pl.ds(pl.multiple_of(row * nsub, nsub), nsub)
                pltpu.make_async_copy(y_ref.at[src], buf.at[slot, k, dst],
                                      sem.at[slot]).start(priority=k % 2)

    def reduce_rows(slot, r0):
        rows = pl.ds(r0, group)
        w = w_ref[rows, :]
        f = jnp.zeros((group, nsub * LANES), F32)
        for k in range(TOP_K):
            yk = jnp.concatenate(
                [buf[slot, k, pl.ds(r0 * nsub + s, group, stride=nsub), :] for s in range(nsub)],
                axis=1)
            f = f + w[:, k:k + 1] * yk
        o_ref[rows, :] = f

    @pl.when(i == 0)
    def _():
        def body(j, c):
            issue_rows(pos_ref, 0, pl.multiple_of(j * group, group))
            return c
        lax.fori_loop(0, tm // group, body, 0)

    slot = i % 2
    for k in range(TOP_K):
        pltpu.make_async_copy(y_ref.at[pl.ds(0, tm * nsub)], buf.at[slot, k], sem.at[slot]).wait()

    @pl.when(i + 1 < n)
    def _():
        def body(j, c):
            r0 = pl.multiple_of(j * group, group)
            issue_rows(posn_ref, 1 - slot, r0)
            reduce_rows(slot, r0)
            return c
        lax.fori_loop(0, tm // group, body, 0)

    @pl.when(i + 1 == n)
    def _():
        def body(j, c):
            reduce_rows(slot, pl.multiple_of(j * group, group))
            return c
        lax.fori_loop(0, tm // group, body, 0)

    o_ref[...] = _layer_norm(DEEPNORM_ALPHA * x1_ref[...] + mod_ref[0, 5:6, :] * o_ref[...],
                             g_ref[...], b_ref[...])


def _combine(pos_flat, w, x1, mod3, g2, b2, y, seq):
    n, d = x1.shape
    tm = min(TM_ROWS, seq)
    nsteps = n // tm
    per_seq = seq // tm
    return pl.pallas_call(
        functools.partial(_combine_kernel, tm=tm),
        out_shape=jax.ShapeDtypeStruct((n, d), F32),
        grid=(nsteps,),
        in_specs=[pl.BlockSpec((tm * TOP_K,), lambda i: (i,), memory_space=pltpu.SMEM),
                  pl.BlockSpec((tm * TOP_K,), lambda i: (jnp.minimum(i + 1, nsteps - 1),),
                               memory_space=pltpu.SMEM),
                  pl.BlockSpec((tm, LANES), lambda i: (i, 0)),
                  pl.BlockSpec((tm, d), lambda i: (i, 0)),
                  pl.BlockSpec((1, 6, d), lambda i: (i // per_seq, 0, 0)),
                  pl.BlockSpec(g2.shape, lambda i: (0, 0)),
                  pl.BlockSpec(b2.shape, lambda i: (0, 0)),
                  pl.BlockSpec(memory_space=pl.ANY)],
        out_specs=pl.BlockSpec((tm, d), lambda i: (i, 0)),
        scratch_shapes=[pltpu.VMEM((2, TOP_K, tm * (d // LANES), LANES), F32),
                        pltpu.SemaphoreType.DMA((2,))],
        compiler_params=_params(("arbitrary",), 32),
    )(pos_flat, pos_flat, w, x1, mod3, g2, b2, y)


def _prep_w_uq(w_uq):
    r = w_uq.shape[0]
    w = w_uq.reshape(r, B_HEADS, B_QK_DIM)
    half = B_ROPE_DIM // 2
    rot = jnp.concatenate([-w[:, :, B_NOPE_DIM + half:], w[:, :, B_NOPE_DIM:B_NOPE_DIM + half]],
                          axis=-1).reshape(r, B_HEADS * B_ROPE_DIM)
    w = jnp.pad(w, ((0, 0), (0, 0), (0, LANES - B_QK_DIM))).reshape(r, B_HEADS * LANES)
    return jnp.concatenate([w, rot], axis=1).astype(BF16)


def _prep_w_ukv(w_ukv):
    r = w_ukv.shape[0]
    w = w_ukv.reshape(r, B_HEADS, B_NOPE_DIM + B_V_DIM)
    wk = jnp.pad(w[:, :, :B_NOPE_DIM], ((0, 0), (0, 0), (0, LANES - B_NOPE_DIM)))
    wv = w[:, :, B_NOPE_DIM:].reshape(r, B_HEADS // 2, 2 * B_V_DIM)
    wv = jnp.pad(wv, ((0, 0), (0, 0), (0, V_PAIR_LANES - 2 * B_V_DIM)))
    return (wk.reshape(r, B_HEADS * LANES).astype(BF16),
            wv.reshape(r, B_HEADS // 2 * V_PAIR_LANES).astype(BF16))


def _rope_tables(positions):
    half = B_ROPE_DIM // 2
    freqs = ROPE_THETA ** (-jnp.arange(half, dtype=F32) / half)
    ang = positions.astype(F32).reshape(1, -1) * freqs[:, None]
    cos, sin = jnp.cos(ang), jnp.sin(ang)
    z = jnp.zeros((B_ROPE_DIM, ang.shape[1]), F32)
    return jnp.concatenate([sin, sin, z, cos, cos, z], 0)


def kernel(x, c, positions, w_ada, b_ada, w_in, rms_q, w_uq, rms_kv, w_ukv, rel_bias,
           w_branch_a, w_branch_b, w_out, ln1_g, ln1_b, w_router, b_router,
           w_gate, b_gate, w_up, b_up, w_down, b_down, ln2_g, ln2_b):
    bsz, seq, d = x.shape
    n_tok = bsz * seq
    assert w_ada.shape[0] == DEPTH == 1
    row = lambda v: v.reshape(1, -1)

    mod3 = _ada(c, w_ada[0], b_ada[0]).reshape(bsz, 6, d)

    qk, va, cq, ckv, kr, ga, gb = _inproj(x, mod3, jnp.swapaxes(w_in, 1, 2))
    ya = _attn_a(qk, va, _bias_table(rel_bias[0], min(TQ_A, seq)))
    wk, wv = _prep_w_ukv(w_ukv[0])
    qm, km, vm = _mla_proj(cq, ckv, kr, _rope_tables(positions), row(rms_q[0]), row(rms_kv[0]),
                           _prep_w_uq(w_uq[0]), wk, wv)
    yb = _mla_attn(qm, km, vm)

    wr = jnp.pad(w_router[0], ((0, 0), (0, LANES - N_EXPERTS)))
    br = jnp.pad(b_router[0], (0, LANES - N_EXPERTS), constant_values=NEG_INF)
    x1, h2, code, wts, cnt = _merge(x, ya, yb, ga, gb, mod3,
                                    w_branch_a[0].astype(BF16), w_branch_b[0].astype(BF16),
                                    w_out[0].astype(BF16), row(ln1_g[0]), row(ln1_b[0]), wr, row(br))
    wts = wts.reshape(n_tok, LANES)

    tm = TM_MOE
    n_tiles = -(-(n_tok * TOP_K) // tm) + N_EXPERTS
    assert n_tiles <= MAX_TILES_PAD
    pos, tmap, meta = _route(code.reshape(n_tok, LANES), cnt, tm)
    pos_flat = pos[:, :TOP_K].reshape(-1)
    fill = meta[:3].reshape(-1)
    nsub = d // LANES
    xs = _dispatch(pos_flat, fill, h2.reshape(n_tok * nsub, LANES), n_tiles * tm, nsub, tm)
    y = _moe(tmap[:n_tiles, 0], meta[0, :1], xs, w_gate[0], b_gate[0], w_up[0], b_up[0],
             w_down[0], b_down[0], tm)
    out = _combine(pos_flat, wts, x1.reshape(n_tok, d), mod3, row(ln2_g[0]), row(ln2_b[0]), y, seq)
    return out.reshape(bsz, seq, d)
```

```python
import functools

import jax
import jax.numpy as jnp
from jax import lax
from jax.experimental import pallas as pl
from jax.experimental.pallas import tpu as pltpu

F32 = jnp.float32
BF16 = jnp.bfloat16

CHUNK = 64
A_HEADS = 8
A_HEAD_DIM = 64
A_LEFT_CHUNKS = 8
A_MAX_REL = 128
A_WIDTH = A_HEADS * A_HEAD_DIM
B_HEADS = 8
B_NOPE_DIM = 64
B_ROPE_DIM = 32
B_V_DIM = 64
B_Q_LORA = 384
B_KV_LORA = 256
B_QK_DIM = B_NOPE_DIM + B_ROPE_DIM
ROPE_THETA = 10000.0
N_EXPERTS = 32
TOP_K = 4
SWIGLU_LIMIT = 7.0
SWIGLU_ALPHA = 1.702
DEPTH = 1
DEEPNORM_ALPHA = (2.0 * DEPTH) ** 0.25
LN_EPS = 1e-5
RMS_EPS = 1e-6
NEG_INF = -1e30
LOG2_E = 1.4426950408889634

LANES = 128
MXU_COLS = 256
V_PAIR_LANES = MXU_COLS

TM_PROJ = 512
TQ_A = 256
A_QBLOCKS = 8
TQ_B = 512
TM_MERGE = 1024
MERGE_SPLIT = 2
TR_ROUTE = 1024
TM_MOE = 512
TM_ROWS = 256
COMBINE_GROUP = 8
MAX_TILES_PAD = 512


def _params(sem, vmem_mb=None):
    return pltpu.CompilerParams(
        dimension_semantics=sem,
        vmem_limit_bytes=None if vmem_mb is None else vmem_mb << 20)


def _dot(a, b):
    return jnp.dot(a, b, preferred_element_type=F32)


def _dot_nt(a, b):
    return lax.dot_general(a, b, (((1,), (1,)), ((), ())), preferred_element_type=F32)


def _store_token_tiles(ref, x):
    rows, d = x.shape
    nsub = d // LANES
    for s in range(nsub):
        ref[pl.ds(s, rows, stride=nsub), :] = x[:, s * LANES:(s + 1) * LANES]


def _load_token_tiles(ref, nsub):
    rows = ref.shape[0] // nsub
    return jnp.concatenate([ref[pl.ds(s, rows, stride=nsub), :] for s in range(nsub)], axis=1)


def _layer_norm(x, g, b):
    mu = jnp.mean(x, axis=-1, keepdims=True)
    xc = x - mu
    var = jnp.mean(xc * xc, axis=-1, keepdims=True)
    return xc * lax.rsqrt(var + LN_EPS) * g + b


def _ada_kernel(c_ref, w_ref, b_ref, o_ref):
    c = c_ref[...]
    sc = c * jax.nn.sigmoid(c)
    w = w_ref[...]
    s_hi, w_hi = sc.astype(BF16), w.astype(BF16)
    s_lo = (sc - s_hi.astype(F32)).astype(BF16)
    w_lo = (w - w_hi.astype(F32)).astype(BF16)
    o_ref[...] = _dot(s_hi, w_hi) + _dot(s_hi, w_lo) + _dot(s_lo, w_hi) + b_ref[...]


def _ada(c, w_ada, b_ada):
    bsz, d = c.shape
    m = w_ada.shape[1]
    tn = 1024
    return pl.pallas_call(
        _ada_kernel,
        out_shape=jax.ShapeDtypeStruct((bsz, m), F32),
        grid=(m // tn,),
        in_specs=[pl.BlockSpec((bsz, d), lambda j: (0, 0)),
                  pl.BlockSpec((d, tn), lambda j: (0, j)),
                  pl.BlockSpec((1, tn), lambda j: (0, j))],
        out_specs=pl.BlockSpec((bsz, tn), lambda j: (0, j)),
        compiler_params=_params(("arbitrary",)),
    )(c, w_ada, b_ada.reshape(1, m))


def _inproj_kernel(x_ref, mod_ref, win_hbm, qk_ref, va_ref, cq_ref, ckv_ref, kr_ref, ga_ref, gb_ref,
                   wf_s, wqk_ref, wv_ref, wcq_ref, wckv_ref, wkr_ref, wga_ref, wgb_ref, sem):
    @pl.when((pl.program_id(0) == 0) & (pl.program_id(1) == 0))
    def _():
        copy = pltpu.make_async_copy(win_hbm.at[0], wf_s, sem)
        copy.start()
        copy.wait()
        d = wf_s.shape[1]
        chunk = 2 * LANES
        off = 0
        for dst, n in ((wqk_ref, 2 * A_WIDTH), (wv_ref, A_WIDTH), (wcq_ref, B_Q_LORA),
                       (wckv_ref, B_KV_LORA), (None, B_ROPE_DIM), (wga_ref, d), (wgb_ref, d)):
            if dst is None:
                blk = wf_s[off:off + LANES, :].T
                lane = lax.broadcasted_iota(jnp.int32, blk.shape, 1)
                blk = pltpu.roll(jnp.where(lane < n, blk, 0.0), B_NOPE_DIM, 1)
                wkr_ref[...] = blk.astype(BF16)
            else:
                for c0 in range(0, n, chunk):
                    cn = min(chunk, n - c0)
                    blk = wf_s[off + c0:off + c0 + cn, :].T
                    if dst is wqk_ref and c0 < A_WIDTH:
                        blk = blk * (A_HEAD_DIM ** -0.5 * LOG2_E)
                    dst[:, c0:c0 + cn] = blk.astype(BF16)
            off += n

    m = mod_ref[0]
    h = (x_ref[0] * (1.0 + m[1:2]) + m[0:1]).astype(BF16)
    for w_ref, ref in ((wqk_ref, qk_ref), (wcq_ref, cq_ref), (wckv_ref, ckv_ref),
                       (wkr_ref, kr_ref), (wga_ref, ga_ref), (wgb_ref, gb_ref)):
        ref[0] = _dot(h, w_ref[...]).astype(ref.dtype)
    v = _dot(h, wv_ref[...]).astype(BF16)
    ones_blk = jnp.ones((v.shape[0], LANES), BF16)
    parts = []
    for hp in range(A_WIDTH // LANES):
        parts += [v[:, hp * LANES:(hp + 1) * LANES], ones_blk]
    va_ref[0] = jnp.concatenate(parts, axis=1)


def _inproj(x, mod3, w_in):
    bsz, seq, d = x.shape
    tm = min(TM_PROJ, seq)
    widths = (2 * A_WIDTH, A_WIDTH // LANES * V_PAIR_LANES, B_Q_LORA, B_KV_LORA, LANES, d, d)
    dtypes = (BF16, BF16, BF16, BF16, F32, BF16, BF16)
    w_widths = (2 * A_WIDTH, A_WIDTH, B_Q_LORA, B_KV_LORA, LANES, d, d)
    assert w_in.shape[0] == 1 and w_in.shape[1] == sum(w_widths) - LANES + B_ROPE_DIM
    return pl.pallas_call(
        _inproj_kernel,
        out_shape=[jax.ShapeDtypeStruct((bsz, seq, n), dt) for n, dt in zip(widths, dtypes)],
        grid=(bsz, seq // tm),
        in_specs=[pl.BlockSpec((1, tm, d), lambda b, i: (b, i, 0)),
                  pl.BlockSpec((1, 6, d), lambda b, i: (b, 0, 0)),
                  pl.BlockSpec(memory_space=pl.ANY)],
        out_specs=[pl.BlockSpec((1, tm, n), lambda b, i: (b, i, 0)) for n in widths],
        scratch_shapes=[pltpu.VMEM(w_in.shape[1:], F32)]
                       + [pltpu.VMEM((d, n), BF16) for n in w_widths]
                       + [pltpu.SemaphoreType.DMA(())],
        compiler_params=_params(("arbitrary", "arbitrary"), 56),
    )(x, mod3, w_in)


def _attn_a_kernel(q_ref, k_ref, v_ref, bias_ref, o_ref, *, tq, nd, nblk, steps):
    band = nd * tq
    lane = lax.broadcasted_iota(jnp.int32, (1, LANES), 1)

    def shifted(base, missing):
        cut = missing * tq
        if cut == 0:
            return base
        return jnp.concatenate([base[:, cut:], jnp.full((tq, cut), NEG_INF, F32)], axis=1)

    def bias_for(hh, i):
        base = bias_ref[hh]
        if isinstance(i, int):
            return shifted(base, max(nd - 1 - i, 0))
        out = base
        for first in range(nd - 1):
            out = jnp.where(i == first, shifted(base, nd - 1 - first), out)
        return out

    for sub in range(nblk):
        if steps == 1:
            i = sub
            rows = slice(max(i - (nd - 1), 0) * tq, max(i - (nd - 1), 0) * tq + band)
        else:
            i = pl.program_id(2) * nblk + sub
            rows = pl.ds(pl.multiple_of(jnp.maximum(i - (nd - 1), 0) * tq, tq), band)
        q = q_ref[0, sub * tq:(sub + 1) * tq, :]
        k = k_ref[0, rows, :]
        v = v_ref[0, rows, :]
        zero = jnp.zeros_like(k)
        kk = jnp.concatenate([jnp.where(lane < A_HEAD_DIM, k, zero),
                              jnp.where(lane >= A_HEAD_DIM, k, zero)], axis=0)
        s = _dot_nt(q, kk)
        ps = []
        for hh in range(2):
            sh = s[:, hh * band:(hh + 1) * band] + bias_for(hh, i)
            m = sh.max(axis=-1, keepdims=True)
            ps.append(jnp.exp2(sh - m).astype(BF16))
        acc = _dot(jnp.concatenate(ps, axis=0), v)
        out = acc[:, :LANES] / acc[:, LANES:]
        o_ref[0, sub * tq:(sub + 1) * tq, :] = jnp.where(
            lane < A_HEAD_DIM, out[:tq], out[tq:]).astype(o_ref.dtype)


def _attn_a(qk, va, bias_tab):
    bsz, seq, _ = qk.shape
    tq = min(TQ_A, seq)
    nd = bias_tab.shape[2] // tq
    assert seq >= nd * tq
    ncb = A_WIDTH // LANES
    nblk = min(A_QBLOCKS, seq // tq)
    steps = seq // (nblk * tq)
    return pl.pallas_call(
        functools.partial(_attn_a_kernel, tq=tq, nd=nd, nblk=nblk, steps=steps),
        out_shape=jax.ShapeDtypeStruct((bsz, seq, A_WIDTH), BF16),
        grid=(ncb, bsz, steps),
        in_specs=[pl.BlockSpec((1, nblk * tq, LANES), lambda hp, b, i: (b, i, hp)),
                  pl.BlockSpec((1, seq, LANES), lambda hp, b, i: (b, 0, ncb + hp)),
                  pl.BlockSpec((1, seq, V_PAIR_LANES), lambda hp, b, i: (b, 0, hp)),
                  pl.BlockSpec((2, tq, nd * tq), lambda hp, b, i: (hp, 0, 0))],
        out_specs=pl.BlockSpec((1, nblk * tq, LANES), lambda hp, b, i: (b, i, hp)),
        compiler_params=_params(("arbitrary",) * 3, 40),
    )(qk, qk, va, bias_tab)


def _bias_table(rel_bias, tq):
    nd = (A_LEFT_CHUNKS * CHUNK + tq - 1) // tq + 1
    width = nd * tq
    period = width + tq + 1
    u = jnp.arange(period)
    dist = jnp.where(u < width, (nd - 1) * tq - u, (nd - 1) * tq + period - u)
    vec = rel_bias[:, jnp.clip(dist, -A_MAX_REL, A_MAX_REL) + A_MAX_REL].astype(F32)
    toep = jnp.tile(vec, (1, tq))[:, :tq * (period - 1)].reshape(-1, tq, period - 1)[:, :, :width]
    qi = jnp.arange(tq)[:, None]
    x = jnp.arange(width)[None, :]
    dchunk = qi // CHUNK - x // CHUNK + (nd - 1) * (tq // CHUNK)
    valid = (dchunk >= 0) & (dchunk <= A_LEFT_CHUNKS)
    return jnp.where(valid[None], toep * LOG2_E, NEG_INF)


def _mla_proj_kernel(cq_ref, ckv_ref, kr_ref, tab_ref, rq_ref, rkv_ref,
                     wq_ref, wk_ref, wv_ref, q_out, k_out, v_out):
    def rms(x, g):
        xf = x.astype(F32)
        ms = jnp.mean(xf * xf, axis=-1, keepdims=True)
        return (xf * lax.rsqrt(ms + RMS_EPS) * g).astype(BF16)

    cqn = rms(cq_ref[0], rq_ref[...])
    ckvn = rms(ckv_ref[0], rkv_ref[...])
    q = _dot(cqn, wq_ref[...])
    kn = _dot(ckvn, wk_ref[...])
    vlane = lax.broadcasted_iota(jnp.int32, (1, v_out.shape[-1]), 1)
    ones_lane = jnp.where(vlane % V_PAIR_LANES >= 2 * B_V_DIM, 1.0, 0.0)
    v_out[0] = (_dot(ckvn, wv_ref[...]) + ones_lane).astype(v_out.dtype)
    tab = tab_ref[...].T
    tlane = lax.broadcasted_iota(jnp.int32, tab.shape, 1)
    rope_end = B_NOPE_DIM + B_ROPE_DIM
    in_rope = (tlane >= B_NOPE_DIM) & (tlane < rope_end)
    cos_t = jnp.where(tlane < B_NOPE_DIM, 1.0, jnp.where(in_rope, tab, 0.0))
    sin_t = jnp.where(in_rope, pltpu.roll(tab, B_NOPE_DIM, 1), 0.0)
    first = tlane < B_NOPE_DIM + B_ROPE_DIM // 2
    sin_a = jnp.where(first, -sin_t, 0.0)
    sin_b = jnp.where(first, 0.0, sin_t)

    def rope(x, tables):
        c, sa, sb = tables
        return (x * c + pltpu.roll(x, LANES - B_ROPE_DIM // 2, 1) * sa
                + pltpu.roll(x, B_ROPE_DIM // 2, 1) * sb)

    kpe = rope(kr_ref[0], (cos_t, sin_a, sin_b))
    scale = B_QK_DIM ** -0.5 * LOG2_E
    cos_q, sin_q = cos_t * scale, sin_t * scale
    nq = B_HEADS * LANES
    per_block = LANES // B_ROPE_DIM
    for h in range(B_HEADS):
        sl = slice(h * LANES, (h + 1) * LANES)
        blk = nq + (h // per_block) * LANES
        partner = q[:, blk:blk + LANES]
        shift = (B_NOPE_DIM - B_ROPE_DIM * (h % per_block)) % LANES
        if shift:
            partner = pltpu.roll(partner, shift, 1)
        q_out[0, :, sl] = (q[:, sl] * cos_q + partner * sin_q).astype(q_out.dtype)
        k_out[0, :, sl] = (kn[:, sl] + kpe).astype(k_out.dtype)


def _mla_proj(cq, ckv, kr, rope_tab, rms_q, rms_kv, wq, wk, wv):
    bsz, seq, _ = cq.shape
    tm = min(TM_PROJ, seq)
    tok = lambda n: pl.BlockSpec((1, tm, n), lambda b, i: (b, i, 0))
    full = lambda a: pl.BlockSpec(a.shape, lambda b, i: (0,) * a.ndim)
    return pl.pallas_call(
        _mla_proj_kernel,
        out_shape=[jax.ShapeDtypeStruct((bsz, seq, B_HEADS * LANES), BF16),
                   jax.ShapeDtypeStruct((bsz, seq, B_HEADS * LANES), BF16),
                   jax.ShapeDtypeStruct((bsz, seq, wv.shape[1]), BF16)],
        grid=(bsz, seq // tm),
        in_specs=[tok(B_Q_LORA), tok(B_KV_LORA), tok(LANES),
                  pl.BlockSpec((LANES, tm), lambda b, i: (0, b * (seq // tm) + i)),
                  full(rms_q), full(rms_kv), full(wq), full(wk), full(wv)],
        out_specs=[tok(B_HEADS * LANES), tok(B_HEADS * LANES), tok(wv.shape[1])],
        compiler_params=_params(("arbitrary", "arbitrary"), 32),
    )(cq, ckv, kr, rope_tab, rms_q, rms_kv, wq, wk, wv)


def _mla_attn_kernel(q_ref, k_ref, v_ref, o_ref, s_a, s_b, m_s, acc_s, *, t):
    lane = lax.broadcasted_iota(jnp.int32, (1, LANES), 1)
    nq = q_ref.shape[1] // t
    pairs = [(qi, kj) for qi in range(nq) for kj in range(qi + 1)]
    bufs = (s_a, s_b)

    def produce(qi, kj, s_ref):
        for hh in range(2):
            q = q_ref[0, qi * t:(qi + 1) * t, hh * LANES:(hh + 1) * LANES]
            k = k_ref[0, kj * t:(kj + 1) * t, hh * LANES:(hh + 1) * LANES]
            s_ref[hh] = _dot_nt(q, k)

    def consume(qi, kj, s_ref):
        v = v_ref[0, kj * t:(kj + 1) * t, :]
        outs = []
        for hh in range(2):
            s = s_ref[hh]
            if kj == qi:
                row_chunk = lax.broadcasted_iota(jnp.int32, (t, t), 0) // CHUNK
                col_chunk = lax.broadcasted_iota(jnp.int32, (t, t), 1) // CHUNK
                s = jnp.where(col_chunk <= row_chunk, s, NEG_INF)
            row_max = jnp.broadcast_to(s.max(axis=-1, keepdims=True), (t, LANES))
            if kj == 0:
                m_new = row_max
                acc = _dot(jnp.exp2(s - jnp.tile(m_new, (1, t // LANES))).astype(BF16), v)
            else:
                m_old = m_s[hh]
                m_new = jnp.maximum(m_old, row_max)
                p = jnp.exp2(s - jnp.tile(m_new, (1, t // LANES))).astype(BF16)
                alpha = jnp.exp2(m_old - m_new)
                acc = jnp.tile(alpha, (1, V_PAIR_LANES // LANES)) * acc_s[hh] + _dot(p, v)
            if kj == qi:
                outs.append(acc[:, :LANES] / acc[:, LANES:])
            else:
                acc_s[hh] = acc
                m_s[hh] = m_new
        if kj == qi:
            o_ref[0, qi * t:(qi + 1) * t, :] = jnp.where(
                lane < B_V_DIM, outs[0], outs[1]).astype(o_ref.dtype)

    produce(*pairs[0], bufs[0])
    for n, pair in enumerate(pairs):
        if n + 1 < len(pairs):
            produce(*pairs[n + 1], bufs[(n + 1) % 2])
        consume(*pair, bufs[n % 2])


def _mla_attn(qm, km, vm):
    bsz, seq, _ = qm.shape
    t = min(TQ_B, seq)
    return pl.pallas_call(
        functools.partial(_mla_attn_kernel, t=t),
        out_shape=jax.ShapeDtypeStruct((bsz, seq, B_HEADS * B_V_DIM), BF16),
        grid=(bsz, B_HEADS // 2),
        in_specs=[pl.BlockSpec((1, seq, 2 * LANES), lambda b, hp: (b, 0, hp)),
                  pl.BlockSpec((1, seq, 2 * LANES), lambda b, hp: (b, 0, hp)),
                  pl.BlockSpec((1, seq, V_PAIR_LANES), lambda b, hp: (b, 0, hp))],
        out_specs=pl.BlockSpec((1, seq, LANES), lambda b, hp: (b, 0, hp)),
        scratch_shapes=[pltpu.VMEM((2, t, t), F32), pltpu.VMEM((2, t, t), F32),
                        pltpu.VMEM((2, t, LANES), F32),
                        pltpu.VMEM((2, t, V_PAIR_LANES), F32)],
        compiler_params=_params(("arbitrary",) * 2, 40),
    )(qm, km, vm)


def _merge_kernel(x_ref, ya_ref, yb_ref, ga_ref, gb_ref, mod_ref, wa_ref, wb_ref, wo_ref,
                  g1_ref, b1_ref, wrh_ref, wrl_ref, br_ref, x1_ref, h2_ref, code_ref, wts_ref,
                  cnt_ref):
    @pl.when((pl.program_id(0) == 0) & (pl.program_id(1) == 0))
    def _():
        cnt_ref[...] = jnp.zeros_like(cnt_ref)

    tm = x_ref.shape[1]
    half = tm // MERGE_SPLIT
    for part in range(MERGE_SPLIT):
        _merge_rows(slice(part * half, (part + 1) * half), x_ref, ya_ref, yb_ref, ga_ref, gb_ref,
                    mod_ref, wa_ref, wb_ref, wo_ref, g1_ref, b1_ref, wrh_ref, wrl_ref, br_ref,
                    x1_ref, h2_ref, code_ref, wts_ref, cnt_ref)


def _merge_rows(rows, x_ref, ya_ref, yb_ref, ga_ref, gb_ref, mod_ref, wa_ref, wb_ref, wo_ref,
                g1_ref, b1_ref, wrh_ref, wrl_ref, br_ref, x1_ref, h2_ref, code_ref, wts_ref,
                cnt_ref):
    m = mod_ref[0]
    a = _dot(ya_ref[0, rows, :], wa_ref[...])
    b = _dot(yb_ref[0, rows, :], wb_ref[...])
    merged = (jax.nn.sigmoid(ga_ref[0, rows, :].astype(F32)) * a
              + jax.nn.sigmoid(gb_ref[0, rows, :].astype(F32)) * b)
    o = _dot(merged.astype(BF16), wo_ref[...])
    x1 = _layer_norm(DEEPNORM_ALPHA * x_ref[0, rows, :] + m[2:3] * o, g1_ref[...], b1_ref[...])
    h2 = x1 * (1.0 + m[4:5]) + m[3:4]
    x1_ref[0, rows, :] = x1
    nsub = h2.shape[1] // LANES
    _store_token_tiles(h2_ref.at[0, pl.ds(rows.start * nsub, (rows.stop - rows.start) * nsub)], h2)
    h_hi = h2.astype(BF16)
    h_lo = (h2 - h_hi.astype(F32)).astype(BF16)
    work = (_dot(h_hi, wrh_ref[...]) + _dot(h_hi, wrl_ref[...]) + _dot(h_lo, wrh_ref[...])
            + br_ref[...])
    lane = lax.broadcasted_iota(jnp.int32, work.shape, 1)
    code = jnp.zeros(work.shape, F32)
    vals = []
    for k in range(TOP_K):
        top = work.max(axis=-1, keepdims=True)
        idx = jnp.where(work == top, lane, LANES).min(axis=-1, keepdims=True)
        sel = lane == idx
        work = jnp.where(sel, -jnp.inf, work)
        code = jnp.where(sel, k + 1.0, code)
        vals.append(top)
    den = sum(jnp.exp(v - vals[0]) for v in vals)
    wts = jnp.zeros(work.shape, F32)
    for k in range(TOP_K):
        wts = jnp.where(lane == k, jnp.exp(vals[k] - vals[0]) / den, wts)
    code_ref[0, rows, :] = code
    wts_ref[0, rows, :] = wts
    picked = jnp.where(code > 0.0, 1.0, 0.0)
    cnt_ref[...] += jnp.broadcast_to(picked.sum(axis=0, keepdims=True), cnt_ref.shape)


def _merge(x, ya, yb, ga, gb, mod3, wa, wb, wo, g1, b1, wr, br):
    wr_hi = wr.astype(BF16)
    wr_lo = (wr - wr_hi.astype(F32)).astype(BF16)
    bsz, seq, d = x.shape
    tm = min(TM_MERGE, seq)
    tok = lambda n: pl.BlockSpec((1, tm, n), lambda b, i: (b, i, 0))
    full = lambda a: pl.BlockSpec(a.shape, lambda b, i: (0,) * a.ndim)
    return pl.pallas_call(
        _merge_kernel,
        out_shape=[jax.ShapeDtypeStruct((bsz, seq, d), F32),
                   jax.ShapeDtypeStruct((bsz, seq * (d // LANES), LANES), F32),
                   jax.ShapeDtypeStruct((bsz, seq, LANES), F32),
                   jax.ShapeDtypeStruct((bsz, seq, LANES), F32),
                   jax.ShapeDtypeStruct((8, LANES), F32)],
        grid=(bsz, seq // tm),
        in_specs=[tok(d), tok(A_WIDTH), tok(B_HEADS * B_V_DIM), tok(d), tok(d),
                  pl.BlockSpec((1, 6, d), lambda b, i: (b, 0, 0)),
                  full(wa), full(wb), full(wo), full(g1), full(b1), full(wr_hi), full(wr_lo),
                  full(br)],
        out_specs=[tok(d),
                   pl.BlockSpec((1, tm * (d // LANES), LANES), lambda b, i: (b, i, 0)),
                   tok(LANES), tok(LANES), pl.BlockSpec((8, LANES), lambda b, i: (0, 0))],
        compiler_params=_params(("arbitrary", "arbitrary"), 48),
    )(x, ya, yb, ga, gb, mod3, wa, wb, wo, g1, b1, wr_hi, wr_lo, br)


def _route_kernel(code_ref, cnt_ref, pos_ref, tmap_ref, nv_ref, carry_s, pstart_s, *, tr, tm):
    i = pl.program_id(0)
    lane = lax.broadcasted_iota(jnp.int32, (tr, LANES), 1)
    code = code_ref[...]
    picked = jnp.where(code > 0.0, 1.0, 0.0)
    tile_count = jnp.broadcast_to(picked.sum(axis=0, keepdims=True), (8, LANES))

    @pl.when(i == 0)
    def _():
        lane8 = lax.broadcasted_iota(jnp.int32, (8, LANES), 1)
        padded = jnp.floor((cnt_ref[...] + (tm - 1)) * (1.0 / tm)) * tm
        cum = padded
        shift = 1
        while shift < N_EXPERTS:
            cum = cum + jnp.where(lane8 >= shift, pltpu.roll(cum, shift, 1), 0.0)
            shift *= 2
        pstart_s[...] = cum - padded
        carry_s[...] = jnp.zeros_like(carry_s)
        nt = tmap_ref.shape[0]
        tile_start = (lax.broadcasted_iota(jnp.int32, (nt, LANES), 0) * tm).astype(F32)
        lane_t = lax.broadcasted_iota(jnp.int32, (nt, LANES), 1)
        done = jnp.where((lane_t < N_EXPERTS) & (cum[0:1] <= tile_start), 1.0, 0.0)
        te = jnp.minimum(done.sum(axis=-1, keepdims=True), N_EXPERTS - 1.0)
        tmap_ref[...] = jnp.broadcast_to(te, (nt, LANES)).astype(jnp.int32)
        total = jnp.where(lane8 == N_EXPERTS - 1, cum, 0.0).sum(axis=-1, keepdims=True)
        row8 = lax.broadcasted_iota(jnp.int32, (8, LANES), 0)
        meta = jnp.where(row8 == 0, jnp.broadcast_to(total * (1.0 / tm), (8, LANES)),
                         jnp.where(row8 == 1, cum - padded + cnt_ref[...], cum))
        nv_ref[...] = meta.astype(jnp.int32)

    r = lax.broadcasted_iota(jnp.int32, (tr, tr), 0)
    c = lax.broadcasted_iota(jnp.int32, (tr, tr), 1)
    tri = jnp.where(c < r, 1.0, 0.0).astype(BF16)
    base = _dot(tri, picked.astype(BF16)) + (carry_s[0:1] + pstart_s[0:1])
    carry_s[...] += tile_count
    pos = jnp.zeros((tr, LANES), F32)
    for k in range(TOP_K):
        pk = jnp.where(code == k + 1.0, base, 0.0).sum(axis=-1, keepdims=True)
        pos = jnp.where(lane == k, pk, pos)
    pos_ref[...] = pos.astype(jnp.int32)


def _route(code, cnt, tm):
    n = code.shape[0]
    tr = min(TR_ROUTE, n)
    return pl.pallas_call(
        functools.partial(_route_kernel, tr=tr, tm=tm),
        out_shape=[jax.ShapeDtypeStruct((n, LANES), jnp.int32),
                   jax.ShapeDtypeStruct((MAX_TILES_PAD, LANES), jnp.int32),
                   jax.ShapeDtypeStruct((8, LANES), jnp.int32)],
        grid=(n // tr,),
        in_specs=[pl.BlockSpec((tr, LANES), lambda i: (i, 0)),
                  pl.BlockSpec((8, LANES), lambda i: (0, 0))],
        out_specs=[pl.BlockSpec((tr, LANES), lambda i: (i, 0)),
                   pl.BlockSpec((MAX_TILES_PAD, LANES), lambda i: (0, 0)),
                   pl.BlockSpec((8, LANES), lambda i: (0, 0))],
        scratch_shapes=[pltpu.VMEM((8, LANES), F32)] * 2,
        compiler_params=_params(("arbitrary",)),
    )(code, cnt)


def _dispatch_kernel(pos_ref, fill_ref, h_ref, xs_ref, ztile_s, sem, fill_sem, tail_sem,
                     *, tm, nsub, tm_moe):
    def tile(ref, r):
        return ref.at[pl.ds(pl.multiple_of(r * nsub, nsub), nsub)]

    @pl.when(pl.program_id(0) == 0)
    def _():
        ztile_s[...] = jnp.zeros_like(ztile_s)

        def tail_copy(t):
            rows = tm_moe * nsub
            return pltpu.make_async_copy(
                ztile_s, xs_ref.at[pl.ds(pl.multiple_of(t * rows, rows), rows)], tail_sem)

        sizes = [1 << b for b in reversed(range(tm_moe.bit_length() - 1))]

        def fill_pieces(e, act):
            lo, hi = fill_ref[LANES + e], fill_ref[2 * LANES + e]
            at = lo
            for size in sizes:
                has = ((hi - lo) & size) != 0
                copy = pltpu.make_async_copy(
                    ztile_s.at[pl.ds(0, size * nsub)],
                    xs_ref.at[pl.ds(pl.multiple_of(at * nsub, nsub), size * nsub)], fill_sem)
                pl.when(has)(functools.partial(act, copy))
                at = at + jnp.where(has, size, 0)

        def fill_start(e, c):
            fill_pieces(e, lambda copy: copy.start())
            return c

        def fill_wait(e, c):
            fill_pieces(e, lambda copy: copy.wait())
            return c

        n_tiles = xs_ref.shape[0] // (tm_moe * nsub)

        def tail_start(t, c):
            tail_copy(t).start()
            return c

        def tail_drain(_, c):
            tail_copy(0).wait()
            return c

        lax.fori_loop(0, N_EXPERTS, fill_start, 0)
        lax.fori_loop(fill_ref[0], n_tiles, tail_start, 0)
        lax.fori_loop(0, N_EXPERTS, fill_wait, 0)
        lax.fori_loop(fill_ref[0], n_tiles, tail_drain, 0)

    def issue(j, c):
        for r in range(COMBINE_GROUP):
            row = pl.multiple_of(j * COMBINE_GROUP, COMBINE_GROUP) + r
            for k in range(TOP_K):
                pltpu.make_async_copy(tile(h_ref, row), tile(xs_ref, pos_ref[row * TOP_K + k]),
                                      sem).start(priority=k % 2)
        return c

    lax.fori_loop(0, tm // COMBINE_GROUP, issue, 0)
    for _ in range(TOP_K):
        pltpu.make_async_copy(h_ref, xs_ref.at[pl.ds(0, tm * nsub)], sem).wait()


def _dispatch(pos_flat, fill, h2, cap, nsub, tm_moe):
    n = h2.shape[0] // nsub
    tm = min(TM_ROWS, n)
    return pl.pallas_call(
        functools.partial(_dispatch_kernel, tm=tm, nsub=nsub, tm_moe=tm_moe),
        out_shape=jax.ShapeDtypeStruct((cap * nsub, LANES), F32),
        grid=(n // tm,),
        in_specs=[pl.BlockSpec((tm * TOP_K,), lambda i: (i,), memory_space=pltpu.SMEM),
                  pl.BlockSpec(fill.shape, lambda i: (0,), memory_space=pltpu.SMEM),
                  pl.BlockSpec((tm * nsub, LANES), lambda i: (i, 0))],
        out_specs=pl.BlockSpec(memory_space=pl.ANY),
        scratch_shapes=[pltpu.VMEM((tm_moe * nsub, LANES), F32),
                        pltpu.SemaphoreType.DMA(()), pltpu.SemaphoreType.DMA(()),
                        pltpu.SemaphoreType.DMA(())],
        compiler_params=_params(("arbitrary",)),
    )(pos_flat, fill, h2)


def _moe_kernel(te_ref, nv_ref, x_ref, bg_ref, bu_ref, bd_ref, wg_hbm, wu_hbm, wd_hbm,
                y_ref, wf_s, wg_s, wu_s, wd_s, slot_s, sem):
    i = pl.program_id(0)
    nv = nv_ref[0]
    last = te_ref.shape[0] - 1

    def weight_copies(e, slot):
        return [pltpu.make_async_copy(w.at[e], wf_s.at[slot, j], sem.at[slot])
                for j, w in enumerate((wg_hbm, wu_hbm, wd_hbm))]

    @pl.when(i < nv)
    def _():
        e = te_ref[i]

        @pl.when(i == 0)
        def _():
            slot_s[0] = 0
            for c in weight_copies(e, 0):
                c.start()

        @pl.when((i == 0) | (e != te_ref[jnp.maximum(i - 1, 0)]))
        def _():
            slot = slot_s[0]
            for c in weight_copies(e, slot):
                c.wait()
            nxt = lax.while_loop(lambda j: (j < nv) & (te_ref[jnp.minimum(j, last)] == e),
                                 lambda j: j + 1, i + 1)

            @pl.when(nxt < nv)
            def _():
                for c in weight_copies(te_ref[jnp.minimum(nxt, last)], 1 - slot):
                    c.start()

            wg_s[...] = wf_s[slot, 0].astype(BF16)
            wu_s[...] = wf_s[slot, 1].astype(BF16)
            wd_s[...] = wf_s[slot, 2].astype(BF16)
            slot_s[0] = 1 - slot

        xb = _load_token_tiles(x_ref, wg_s.shape[0] // LANES).astype(BF16)
        g = _dot(xb, wg_s[...]) + bg_ref[0]
        u = _dot(xb, wu_s[...]) + bu_ref[0]
        g = jnp.minimum(g, SWIGLU_LIMIT)
        u = jnp.clip(u, -SWIGLU_LIMIT, SWIGLU_LIMIT)
        h = (u + 1.0) * (g * jax.nn.sigmoid(SWIGLU_ALPHA * g))
        _store_token_tiles(y_ref, _dot(h.astype(BF16), wd_s[...]) + bd_ref[0])

    @pl.when(i >= nv)
    def _():
        y_ref[...] = jnp.zeros_like(y_ref)


def _moe(tile_expert, n_valid, xs, w_gate, b_gate, w_up, b_up, w_down, b_down, tm):
    e, d, f = w_gate.shape
    assert d == f
    nsub = d // LANES
    cap = xs.shape[0] // nsub
    n_tiles = cap // tm

    def tile(i, te, nv):
        return jnp.minimum(i, nv[0] - 1)

    exp3 = lambda i, te, nv: (te[tile(i, te, nv)], 0, 0)
    grid_spec = pltpu.PrefetchScalarGridSpec(
        num_scalar_prefetch=2,
        grid=(n_tiles,),
        in_specs=[pl.BlockSpec((tm * nsub, LANES), lambda i, te, nv: (tile(i, te, nv), 0)),
                  pl.BlockSpec((1, 1, f), exp3), pl.BlockSpec((1, 1, f), exp3),
                  pl.BlockSpec((1, 1, d), exp3),
                  pl.BlockSpec(memory_space=pl.ANY), pl.BlockSpec(memory_space=pl.ANY),
                  pl.BlockSpec(memory_space=pl.ANY)],
        out_specs=pl.BlockSpec((tm * nsub, LANES), lambda i, te, nv: (i, 0)),
        scratch_shapes=[pltpu.VMEM((2, 3, d, f), F32),
                        pltpu.VMEM((d, f), BF16), pltpu.VMEM((d, f), BF16), pltpu.VMEM((f, d), BF16),
                        pltpu.SMEM((1,), jnp.int32), pltpu.SemaphoreType.DMA((2,))],
    )
    return pl.pallas_call(
        _moe_kernel,
        out_shape=jax.ShapeDtypeStruct((cap * nsub, LANES), F32),
        grid_spec=grid_spec,
        compiler_params=_params(("arbitrary",), 56),
    )(tile_expert, n_valid, xs, b_gate.reshape(e, 1, f), b_up.reshape(e, 1, f),
      b_down.reshape(e, 1, d), w_gate, w_up, w_down)


def _combine_kernel(pos_ref, posn_ref, w_ref, x1_ref, mod_ref, g_ref, b_ref, y_ref, o_ref,
                    buf, sem, *, tm):
    i = pl.program_id(0)
    n = pl.num_programs(0)
    nsub = x1_ref.shape[1] // LANES

    group = COMBINE_GROUP

    def issue_rows(p_ref, slot, r0):
        for r in range(group):
            for k in range(TOP_K):
                row = r0 + r
                src = pl.ds(pl.multiple_of(p_ref[row * TOP_K + k] * nsub, nsub), nsub)
                dst = pl.ds(pl.multiple_of(row * nsub, nsub), nsub)
                pltpu.make_async_copy(y_ref.at[src], buf.at[slot, k, dst],
                                      sem.at[slot]).start(priority=k % 2)

    def reduce_rows(slot, r0):
        rows = pl.ds(r0, group)
        w = w_ref[rows, :]
        f = jnp.zeros((group, nsub * LANES), F32)
        for k in range(TOP_K):
            yk = jnp.concatenate(
                [buf[slot, k, pl.ds(r0 * nsub + s, group, stride=nsub), :] for s in range(nsub)],
                axis=1)
            f = f + w[:, k:k + 1] * yk
        o_ref[rows, :] = f

    @pl.when(i == 0)
    def _():
        def body(j, c):
            issue_rows(pos_ref, 0, pl.multiple_of(j * group, group))
            return c
        lax.fori_loop(0, tm // group, body, 0)

    slot = i % 2
    for k in range(TOP_K):
        pltpu.make_async_copy(y_ref.at[pl.ds(0, tm * nsub)], buf.at[slot, k], sem.at[slot]).wait()

    @pl.when(i + 1 < n)
    def _():
        def body(j, c):
            r0 = pl.multiple_of(j * group, group)
            issue_rows(posn_ref, 1 - slot, r0)
            reduce_rows(slot, r0)
            return c
        lax.fori_loop(0, tm // group, body, 0)

    @pl.when(i + 1 == n)
    def _():
        def body(j, c):
            reduce_rows(slot, pl.multiple_of(j * group, group))
            return c
        lax.fori_loop(0, tm // group, body, 0)

    o_ref[...] = _layer_norm(DEEPNORM_ALPHA * x1_ref[...] + mod_ref[0, 5:6, :] * o_ref[...],
                             g_ref[...], b_ref[...])


def _combine(pos_flat, w, x1, mod3, g2, b2, y, seq):
    n, d = x1.shape
    tm = min(TM_ROWS, seq)
    nsteps = n // tm
    per_seq = seq // tm
    return pl.pallas_call(
        functools.partial(_combine_kernel, tm=tm),
        out_shape=jax.ShapeDtypeStruct((n, d), F32),
        grid=(nsteps,),
        in_specs=[pl.BlockSpec((tm * TOP_K,), lambda i: (i,), memory_space=pltpu.SMEM),
                  pl.BlockSpec((tm * TOP_K,), lambda i: (jnp.minimum(i + 1, nsteps - 1),),
                               memory_space=pltpu.SMEM),
                  pl.BlockSpec((tm, LANES), lambda i: (i, 0)),
                  pl.BlockSpec((tm, d), lambda i: (i, 0)),
                  pl.BlockSpec((1, 6, d), lambda i: (i // per_seq, 0, 0)),
                  pl.BlockSpec(g2.shape, lambda i: (0, 0)),
                  pl.BlockSpec(b2.shape, lambda i: (0, 0)),
                  pl.BlockSpec(memory_space=pl.ANY)],
        out_specs=pl.BlockSpec((tm, d), lambda i: (i, 0)),
        scratch_shapes=[pltpu.VMEM((2, TOP_K, tm * (d // LANES), LANES), F32),
                        pltpu.SemaphoreType.DMA((2,))],
        compiler_params=_params(("arbitrary",), 32),
    )(pos_flat, pos_flat, w, x1, mod3, g2, b2, y)


def _prep_w_uq(w_uq):
    r = w_uq.shape[0]
    w = w_uq.reshape(r, B_HEADS, B_QK_DIM)
    half = B_ROPE_DIM // 2
    rot = jnp.concatenate([-w[:, :, B_NOPE_DIM + half:], w[:, :, B_NOPE_DIM:B_NOPE_DIM + half]],
                          axis=-1).reshape(r, B_HEADS * B_ROPE_DIM)
    w = jnp.pad(w, ((0, 0), (0, 0), (0, LANES - B_QK_DIM))).reshape(r, B_HEADS * LANES)
    return jnp.concatenate([w, rot], axis=1).astype(BF16)


def _prep_w_ukv(w_ukv):
    r = w_ukv.shape[0]
    w = w_ukv.reshape(r, B_HEADS, B_NOPE_DIM + B_V_DIM)
    wk = jnp.pad(w[:, :, :B_NOPE_DIM], ((0, 0), (0, 0), (0, LANES - B_NOPE_DIM)))
    wv = w[:, :, B_NOPE_DIM:].reshape(r, B_HEADS // 2, 2 * B_V_DIM)
    wv = jnp.pad(wv, ((0, 0), (0, 0), (0, V_PAIR_LANES - 2 * B_V_DIM)))
    return (wk.reshape(r, B_HEADS * LANES).astype(BF16),
            wv.reshape(r, B_HEADS // 2 * V_PAIR_LANES).astype(BF16))


def _rope_tables(positions):
    half = B_ROPE_DIM // 2
    freqs = ROPE_THETA ** (-jnp.arange(half, dtype=F32) / half)
    ang = positions.astype(F32).reshape(1, -1) * freqs[:, None]
    cos, sin = jnp.cos(ang), jnp.sin(ang)
    z = jnp.zeros((B_ROPE_DIM, ang.shape[1]), F32)
    return jnp.concatenate([sin, sin, z, cos, cos, z], 0)


def kernel(x, c, positions, w_ada, b_ada, w_in, rms_q, w_uq, rms_kv, w_ukv, rel_bias,
           w_branch_a, w_branch_b, w_out, ln1_g, ln1_b, w_router, b_router,
           w_gate, b_gate, w_up, b_up, w_down, b_down, ln2_g, ln2_b):
    bsz, seq, d = x.shape
    n_tok = bsz * seq
    assert w_ada.shape[0] == DEPTH == 1
    row = lambda v: v.reshape(1, -1)

    mod3 = _ada(c, w_ada[0], b_ada[0]).reshape(bsz, 6, d)

    qk, va, cq, ckv, kr, ga, gb = _inproj(x, mod3, jnp.swapaxes(w_in, 1, 2))
    ya = _attn_a(qk, va, _bias_table(rel_bias[0], min(TQ_A, seq)))
    wk, wv = _prep_w_ukv(w_ukv[0])
    qm, km, vm = _mla_proj(cq, ckv, kr, _rope_tables(positions), row(rms_q[0]), row(rms_kv[0]),
                           _prep_w_uq(w_uq[0]), wk, wv)
    yb = _mla_attn(qm, km, vm)

    wr = jnp.pad(w_router[0], ((0, 0), (0, LANES - N_EXPERTS)))
    br = jnp.pad(b_router[0], (0, LANES - N_EXPERTS), constant_values=NEG_INF)
    x1, h2, code, wts, cnt = _merge(x, ya, yb, ga, gb, mod3,
                                    w_branch_a[0].astype(BF16), w_branch_b[0].astype(BF16),
                                    w_out[0].astype(BF16), row(ln1_g[0]), row(ln1_b[0]), wr, row(br))
    wts = wts.reshape(n_tok, LANES)

    tm = TM_MOE
    n_tiles = -(-(n_tok * TOP_K) // tm) + N_EXPERTS
    assert n_tiles <= MAX_TILES_PAD
    pos, tmap, meta = _route(code.reshape(n_tok, LANES), cnt, tm)
    pos_flat = pos[:, :TOP_K].reshape(-1)
    fill = meta[:3].reshape(-1)
    nsub = d // LANES
    xs = _dispatch(pos_flat, fill, h2.reshape(n_tok * nsub, LANES), n_tiles * tm, nsub, tm)
    y = _moe(tmap[:n_tiles, 0], meta[0, :1], xs, w_gate[0], b_gate[0], w_up[0], b_up[0],
             w_down[0], b_down[0], tm)
    out = _combine(pos_flat, wts, x1.reshape(n_tok, d), mod3, row(ln2_g[0]), row(ln2_b[0]), y, seq)
    return out.reshape(bsz, seq, d)
```

```python
import functools

import jax
import jax.numpy as jnp
from jax import lax
from jax.experimental import pallas as pl
from jax.experimental.pallas import tpu as pltpu

F32 = jnp.float32
BF16 = jnp.bfloat16

CHUNK = 64
A_HEADS = 8
A_HEAD_DIM = 64
A_LEFT_CHUNKS = 8
A_MAX_REL = 128
A_WIDTH = A_HEADS * A_HEAD_DIM
B_HEADS = 8
B_NOPE_DIM = 64
B_ROPE_DIM = 32
B_V_DIM = 64
B_Q_LORA = 384
B_KV_LORA = 256
B_QK_DIM = B_NOPE_DIM + B_ROPE_DIM
ROPE_THETA = 10000.0
N_EXPERTS = 32
TOP_K = 4
SWIGLU_LIMIT = 7.0
SWIGLU_ALPHA = 1.702
DEPTH = 1
DEEPNORM_ALPHA = (2.0 * DEPTH) ** 0.25
LN_EPS = 1e-5
RMS_EPS = 1e-6
NEG_INF = -1e30
LOG2_E = 1.4426950408889634

LANES = 128
MXU_COLS = 256
V_PAIR_LANES = MXU_COLS

TM_PROJ = 512
TQ_A = 256
A_QBLOCKS = 8
TQ_B = 512
TM_MERGE = 1024
MERGE_SPLIT = 2
TR_ROUTE = 1024
TM_MOE = 512
TM_ROWS = 256
COMBINE_GROUP = 8
MAX_TILES_PAD = 512


def _params(sem, vmem_mb=None):
    return pltpu.CompilerParams(
        dimension_semantics=sem,
        vmem_limit_bytes=None if vmem_mb is None else vmem_mb << 20)


def _dot(a, b):
    return jnp.dot(a, b, preferred_element_type=F32)


def _dot_nt(a, b):
    return lax.dot_general(a, b, (((1,), (1,)), ((), ())), preferred_element_type=F32)


def _store_token_tiles(ref, x):
    rows, d = x.shape
    nsub = d // LANES
    for s in range(nsub):
        ref[pl.ds(s, rows, stride=nsub), :] = x[:, s * LANES:(s + 1) * LANES]


def _load_token_tiles(ref, nsub):
    rows = ref.shape[0] // nsub
    return jnp.concatenate([ref[pl.ds(s, rows, stride=nsub), :] for s in range(nsub)], axis=1)


def _layer_norm(x, g, b):
    mu = jnp.mean(x, axis=-1, keepdims=True)
    xc = x - mu
    var = jnp.mean(xc * xc, axis=-1, keepdims=True)
    return xc * lax.rsqrt(var + LN_EPS) * g + b


def _ada_kernel(c_ref, w_ref, b_ref, o_ref):
    c = c_ref[...]
    sc = c * jax.nn.sigmoid(c)
    w = w_ref[...]
    s_hi, w_hi = sc.astype(BF16), w.astype(BF16)
    s_lo = (sc - s_hi.astype(F32)).astype(BF16)
    w_lo = (w - w_hi.astype(F32)).astype(BF16)
    o_ref[...] = _dot(s_hi, w_hi) + _dot(s_hi, w_lo) + _dot(s_lo, w_hi) + b_ref[...]


def _ada(c, w_ada, b_ada):
    bsz, d = c.shape
    m = w_ada.shape[1]
    tn = 1024
    return pl.pallas_call(
        _ada_kernel,
        out_shape=jax.ShapeDtypeStruct((bsz, m), F32),
        grid=(m // tn,),
        in_specs=[pl.BlockSpec((bsz, d), lambda j: (0, 0)),
                  pl.BlockSpec((d, tn), lambda j: (0, j)),
                  pl.BlockSpec((1, tn), lambda j: (0, j))],
        out_specs=pl.BlockSpec((bsz, tn), lambda j: (0, j)),
        compiler_params=_params(("arbitrary",)),
    )(c, w_ada, b_ada.reshape(1, m))


def _inproj_kernel(x_ref, mod_ref, win_hbm, qk_ref, va_ref, cq_ref, ckv_ref, kr_ref, ga_ref, gb_ref,
                   wf_s, wqk_ref, wv_ref, wcq_ref, wckv_ref, wkr_ref, wga_ref, wgb_ref, sem):
    @pl.when((pl.program_id(0) == 0) & (pl.program_id(1) == 0))
    def _():
        copy = pltpu.make_async_copy(win_hbm.at[0], wf_s, sem)
        copy.start()
        copy.wait()
        d = wf_s.shape[1]
        chunk = 2 * LANES
        off = 0
        for dst, n in ((wqk_ref, 2 * A_WIDTH), (wv_ref, A_WIDTH), (wcq_ref, B_Q_LORA),
                       (wckv_ref, B_KV_LORA), (None, B_ROPE_DIM), (wga_ref, d), (wgb_ref, d)):
            if dst is None:
                blk = wf_s[off:off + LANES, :].T
                lane = lax.broadcasted_iota(jnp.int32, blk.shape, 1)
                blk = pltpu.roll(jnp.where(lane < n, blk, 0.0), B_NOPE_DIM, 1)
                wkr_ref[...] = blk.astype(BF16)
            else:
                for c0 in range(0, n, chunk):
                    cn = min(chunk, n - c0)
                    blk = wf_s[off + c0:off + c0 + cn, :].T
                    if dst is wqk_ref and c0 < A_WIDTH:
                        blk = blk * (A_HEAD_DIM ** -0.5 * LOG2_E)
                    dst[:, c0:c0 + cn] = blk.astype(BF16)
            off += n

    m = mod_ref[0]
    h = (x_ref[0] * (1.0 + m[1:2]) + m[0:1]).astype(BF16)
    for w_ref, ref in ((wqk_ref, qk_ref), (wcq_ref, cq_ref), (wckv_ref, ckv_ref),
                       (wkr_ref, kr_ref), (wga_ref, ga_ref), (wgb_ref, gb_ref)):
        ref[0] = _dot(h, w_ref[...]).astype(ref.dtype)
    v = _dot(h, wv_ref[...]).astype(BF16)
    ones_blk = jnp.ones((v.shape[0], LANES), BF16)
    parts = []
    for hp in range(A_WIDTH // LANES):
        parts += [v[:, hp * LANES:(hp + 1) * LANES], ones_blk]
    va_ref[0] = jnp.concatenate(parts, axis=1)


def _inproj(x, mod3, w_in):
    bsz, seq, d = x.shape
    tm = min(TM_PROJ, seq)
    widths = (2 * A_WIDTH, A_WIDTH // LANES * V_PAIR_LANES, B_Q_LORA, B_KV_LORA, LANES, d, d)
    dtypes = (BF16, BF16, BF16, BF16, F32, BF16, BF16)
    w_widths = (2 * A_WIDTH, A_WIDTH, B_Q_LORA, B_KV_LORA, LANES, d, d)
    assert w_in.shape[0] == 1 and w_in.shape[1] == sum(w_widths) - LANES + B_ROPE_DIM
    return pl.pallas_call(
        _inproj_kernel,
        out_shape=[jax.ShapeDtypeStruct((bsz, seq, n), dt) for n, dt in zip(widths, dtypes)],
        grid=(bsz, seq // tm),
        in_specs=[pl.BlockSpec((1, tm, d), lambda b, i: (b, i, 0)),
                  pl.BlockSpec((1, 6, d), lambda b, i: (b, 0, 0)),
                  pl.BlockSpec(memory_space=pl.ANY)],
        out_specs=[pl.BlockSpec((1, tm, n), lambda b, i: (b, i, 0)) for n in widths],
        scratch_shapes=[pltpu.VMEM(w_in.shape[1:], F32)]
                       + [pltpu.VMEM((d, n), BF16) for n in w_widths]
                       + [pltpu.SemaphoreType.DMA(())],
        compiler_params=_params(("arbitrary", "arbitrary"), 56),
    )(x, mod3, w_in)


def _attn_a_kernel(q_ref, k_ref, v_ref, bias_ref, o_ref, *, tq, nd, nblk, steps):
    band = nd * tq
    lane = lax.broadcasted_iota(jnp.int32, (1, LANES), 1)

    def shifted(base, missing):
        cut = missing * tq
        if cut == 0:
            return base
        return jnp.concatenate([base[:, cut:], jnp.full((tq, cut), NEG_INF, F32)], axis=1)

    def bias_for(hh, i):
        base = bias_ref[hh]
        if isinstance(i, int):
            return shifted(base, max(nd - 1 - i, 0))
        out = base
        for first in range(nd - 1):
            out = jnp.where(i == first, shifted(base, nd - 1 - first), out)
        return out

    for sub in range(nblk):
        if steps == 1:
            i = sub
            rows = slice(max(i - (nd - 1), 0) * tq, max(i - (nd - 1), 0) * tq + band)
        else:
            i = pl.program_id(2) * nblk + sub
            rows = pl.ds(pl.multiple_of(jnp.maximum(i - (nd - 1), 0) * tq, tq), band)
        q = q_ref[0, sub * tq:(sub + 1) * tq, :]
        k = k_ref[0, rows, :]
        v = v_ref[0, rows, :]
        zero = jnp.zeros_like(k)
        kk = jnp.concatenate([jnp.where(lane < A_HEAD_DIM, k, zero),
                              jnp.where(lane >= A_HEAD_DIM, k, zero)], axis=0)
        s = _dot_nt(q, kk)
        ps = []
        for hh in range(2):
            sh = s[:, hh * band:(hh + 1) * band] + bias_for(hh, i)
            m = sh.max(axis=-1, keepdims=True)
            ps.append(jnp.exp2(sh - m).astype(BF16))
        acc = _dot(jnp.concatenate(ps, axis=0), v)
        out = acc[:, :LANES] / acc[:, LANES:]
        o_ref[0, sub * tq:(sub + 1) * tq, :] = jnp.where(
            lane < A_HEAD_DIM, out[:tq], out[tq:]).astype(o_ref.dtype)


def _attn_a(qk, va, bias_tab):
    bsz, seq, _ = qk.shape
    tq = min(TQ_A, seq)
    nd = bias_tab.shape[2] // tq
    assert seq >= nd * tq
    ncb = A_WIDTH // LANES
    nblk = min(A_QBLOCKS, seq // tq)
    steps = seq // (nblk * tq)
    return pl.pallas_call(
        functools.partial(_attn_a_kernel, tq=tq, nd=nd, nblk=nblk, steps=steps),
        out_shape=jax.ShapeDtypeStruct((bsz, seq, A_WIDTH), BF16),
        grid=(ncb, bsz, steps),
        in_specs=[pl.BlockSpec((1, nblk * tq, LANES), lambda hp, b, i: (b, i, hp)),
                  pl.BlockSpec((1, seq, LANES), lambda hp, b, i: (b, 0, ncb + hp)),
                  pl.BlockSpec((1, seq, V_PAIR_LANES), lambda hp, b, i: (b, 0, hp)),
                  pl.BlockSpec((2, tq, nd * tq), lambda hp, b, i: (hp, 0, 0))],
        out_specs=pl.BlockSpec((1, nblk * tq, LANES), lambda hp, b, i: (b, i, hp)),
        compiler_params=_params(("arbitrary",) * 3, 40),
    )(qk, qk, va, bias_tab)


def _bias_table(rel_bias, tq):
    nd = (A_LEFT_CHUNKS * CHUNK + tq - 1) // tq + 1
    width = nd * tq
    period = width + tq
    u = jnp.arange(period)
    dist = jnp.where(u < width, (nd - 1) * tq - u, (nd - 1) * tq + period - u)
    vec = rel_bias[:, jnp.clip(dist, -A_MAX_REL, A_MAX_REL) + A_MAX_REL].astype(F32)
    toep = jnp.tile(vec, (1, tq))[:, :tq * (period - 1)].reshape(-1, tq, period - 1)[:, :, :width]
    qi = jnp.arange(tq)[:, None]
    x = jnp.arange(width)[None, :]
    dchunk = qi // CHUNK - x // CHUNK + (nd - 1) * (tq // CHUNK)
    valid = (dchunk >= 0) & (dchunk <= A_LEFT_CHUNKS)
    return jnp.where(valid[None], toep * LOG2_E, NEG_INF)


def _mla_proj_kernel(cq_ref, ckv_ref, kr_ref, tab_ref, rq_ref, rkv_ref,
                     wq_ref, wk_ref, wv_ref, q_out, k_out, v_out):
    def rms(x, g):
        xf = x.astype(F32)
        ms = jnp.mean(xf * xf, axis=-1, keepdims=True)
        return (xf * lax.rsqrt(ms + RMS_EPS) * g).astype(BF16)

    cqn = rms(cq_ref[0], rq_ref[...])
    ckvn = rms(ckv_ref[0], rkv_ref[...])
    q = _dot(cqn, wq_ref[...])
    kn = _dot(ckvn, wk_ref[...])
    vlane = lax.broadcasted_iota(jnp.int32, (1, v_out.shape[-1]), 1)
    ones_lane = jnp.where(vlane % V_PAIR_LANES >= 2 * B_V_DIM, 1.0, 0.0)
    v_out[0] = (_dot(ckvn, wv_ref[...]) + ones_lane).astype(v_out.dtype)
    tab = tab_ref[...].T
    tlane = lax.broadcasted_iota(jnp.int32, tab.shape, 1)
    rope_end = B_NOPE_DIM + B_ROPE_DIM
    in_rope = (tlane >= B_NOPE_DIM) & (tlane < rope_end)
    cos_t = jnp.where(tlane < B_NOPE_DIM, 1.0, jnp.where(in_rope, tab, 0.0))
    sin_t = jnp.where(in_rope, pltpu.roll(tab, B_NOPE_DIM, 1), 0.0)
    first = tlane < B_NOPE_DIM + B_ROPE_DIM // 2
    sin_a = jnp.where(first, -sin_t, 0.0)
    sin_b = jnp.where(first, 0.0, sin_t)

    def rope(x, tables):
        c, sa, sb = tables
        return (x * c + pltpu.roll(x, LANES - B_ROPE_DIM // 2, 1) * sa
                + pltpu.roll(x, B_ROPE_DIM // 2, 1) * sb)

    kpe = rope(kr_ref[0], (cos_t, sin_a, sin_b))
    scale = B_QK_DIM ** -0.5 * LOG2_E
    cos_q, sin_q = cos_t * scale, sin_t * scale
    nq = B_HEADS * LANES
    per_block = LANES // B_ROPE_DIM
    for h in range(B_HEADS):
        sl = slice(h * LANES, (h + 1) * LANES)
        blk = nq + (h // per_block) * LANES
        partner = q[:, blk:blk + LANES]
        shift = (B_NOPE_DIM - B_ROPE_DIM * (h % per_block)) % LANES
        if shift:
            partner = pltpu.roll(partner, shift, 1)
        q_out[0, :, sl] = (q[:, sl] * cos_q + partner * sin_q).astype(q_out.dtype)
        k_out[0, :, sl] = (kn[:, sl] + kpe).astype(k_out.dtype)


def _mla_proj(cq, ckv, kr, rope_tab, rms_q, rms_kv, wq, wk, wv):
    bsz, seq, _ = cq.shape
    tm = min(TM_PROJ, seq)
    tok = lambda n: pl.BlockSpec((1, tm, n), lambda b, i: (b, i, 0))
    full = lambda a: pl.BlockSpec(a.shape, lambda b, i: (0,) * a.ndim)
    return pl.pallas_call(
        _mla_proj_kernel,
        out_shape=[jax.ShapeDtypeStruct((bsz, seq, B_HEADS * LANES), BF16),
                   jax.ShapeDtypeStruct((bsz, seq, B_HEADS * LANES), BF16),
                   jax.ShapeDtypeStruct((bsz, seq, wv.shape[1]), BF16)],
        grid=(bsz, seq // tm),
        in_specs=[tok(B_Q_LORA), tok(B_KV_LORA), tok(LANES),
                  pl.BlockSpec((LANES, tm), lambda b, i: (0, b * (seq // tm) + i)),
                  full(rms_q), full(rms_kv), full(wq), full(wk), full(wv)],
        out_specs=[tok(B_HEADS * LANES), tok(B_HEADS * LANES), tok(wv.shape[1])],
        compiler_params=_params(("arbitrary", "arbitrary"), 32),
    )(cq, ckv, kr, rope_tab, rms_q, rms_kv, wq, wk, wv)


def _mla_attn_kernel(q_ref, k_ref, v_ref, o_ref, s_a, s_b, m_s, acc_s, *, t):
    lane = lax.broadcasted_iota(jnp.int32, (1, LANES), 1)
    nq = q_ref.shape[1] // t
    pairs = [(qi, kj) for qi in range(nq) for kj in range(qi + 1)]
    bufs = (s_a, s_b)

    def produce(qi, kj, s_ref):
        for hh in range(2):
            q = q_ref[0, qi * t:(qi + 1) * t, hh * LANES:(hh + 1) * LANES]
            k = k_ref[0, kj * t:(kj + 1) * t, hh * LANES:(hh + 1) * LANES]
            s_ref[hh] = _dot_nt(q, k)

    def consume(qi, kj, s_ref):
        v = v_ref[0, kj * t:(kj + 1) * t, :]
        outs = []
        for hh in range(2):
            s = s_ref[hh]
            if kj == qi:
                row_chunk = lax.broadcasted_iota(jnp.int32, (t, t), 0) // CHUNK
                col_chunk = lax.broadcasted_iota(jnp.int32, (t, t), 1) // CHUNK
                s = jnp.where(col_chunk <= row_chunk, s, NEG_INF)
            row_max = jnp.broadcast_to(s.max(axis=-1, keepdims=True), (t, LANES))
            if kj == 0:
                m_new = row_max
                acc = _dot(jnp.exp2(s - jnp.tile(m_new, (1, t // LANES))).astype(BF16), v)
            else:
                m_old = m_s[hh]
                m_new = jnp.maximum(m_old, row_max)
                p = jnp.exp2(s - jnp.tile(m_new, (1, t // LANES))).astype(BF16)
                alpha = jnp.exp2(m_old - m_new)
                acc = jnp.tile(alpha, (1, V_PAIR_LANES // LANES)) * acc_s[hh] + _dot(p, v)
            if kj == qi:
                outs.append(acc[:, :LANES] / acc[:, LANES:])
            else:
                acc_s[hh] = acc
                m_s[hh] = m_new
        if kj == qi:
            o_ref[0, qi * t:(qi + 1) * t, :] = jnp.where(
                lane < B_V_DIM, outs[0], outs[1]).astype(o_ref.dtype)

    produce(*pairs[0], bufs[0])
    for n, pair in enumerate(pairs):
        if n + 1 < len(pairs):
            produce(*pairs[n + 1], bufs[(n + 1) % 2])
        consume(*pair, bufs[n % 2])


def _mla_attn(qm, km, vm):
    bsz, seq, _ = qm.shape
    t = min(TQ_B, seq)
    return pl.pallas_call(
        functools.partial(_mla_attn_kernel, t=t),
        out_shape=jax.ShapeDtypeStruct((bsz, seq, B_HEADS * B_V_DIM), BF16),
        grid=(bsz, B_HEADS // 2),
        in_specs=[pl.BlockSpec((1, seq, 2 * LANES), lambda b, hp: (b, 0, hp)),
                  pl.BlockSpec((1, seq, 2 * LANES), lambda b, hp: (b, 0, hp)),
                  pl.BlockSpec((1, seq, V_PAIR_LANES), lambda b, hp: (b, 0, hp))],
        out_specs=pl.BlockSpec((1, seq, LANES), lambda b, hp: (b, 0, hp)),
        scratch_shapes=[pltpu.VMEM((2, t, t), F32), pltpu.VMEM((2, t, t), F32),
                        pltpu.VMEM((2, t, LANES), F32),
                        pltpu.VMEM((2, t, V_PAIR_LANES), F32)],
        compiler_params=_params(("arbitrary",) * 2, 40),
    )(qm, km, vm)


def _merge_kernel(x_ref, ya_ref, yb_ref, ga_ref, gb_ref, mod_ref, wa_ref, wb_ref, wo_ref,
                  g1_ref, b1_ref, wrh_ref, wrl_ref, br_ref, x1_ref, h2_ref, code_ref, wts_ref,
                  cnt_ref):
    @pl.when((pl.program_id(0) == 0) & (pl.program_id(1) == 0))
    def _():
        cnt_ref[...] = jnp.zeros_like(cnt_ref)

    tm = x_ref.shape[1]
    half = tm // MERGE_SPLIT
    for part in range(MERGE_SPLIT):
        _merge_rows(slice(part * half, (part + 1) * half), x_ref, ya_ref, yb_ref, ga_ref, gb_ref,
                    mod_ref, wa_ref, wb_ref, wo_ref, g1_ref, b1_ref, wrh_ref, wrl_ref, br_ref,
                    x1_ref, h2_ref, code_ref, wts_ref, cnt_ref)


def _merge_rows(rows, x_ref, ya_ref, yb_ref, ga_ref, gb_ref, mod_ref, wa_ref, wb_ref, wo_ref,
                g1_ref, b1_ref, wrh_ref, wrl_ref, br_ref, x1_ref, h2_ref, code_ref, wts_ref,
                cnt_ref):
    m = mod_ref[0]
    a = _dot(ya_ref[0, rows, :], wa_ref[...])
    b = _dot(yb_ref[0, rows, :], wb_ref[...])
    merged = (jax.nn.sigmoid(ga_ref[0, rows, :].astype(F32)) * a
              + jax.nn.sigmoid(gb_ref[0, rows, :].astype(F32)) * b)
    o = _dot(merged.astype(BF16), wo_ref[...])
    x1 = _layer_norm(DEEPNORM_ALPHA * x_ref[0, rows, :] + m[2:3] * o, g1_ref[...], b1_ref[...])
    h2 = x1 * (1.0 + m[4:5]) + m[3:4]
    x1_ref[0, rows, :] = x1
    nsub = h2.shape[1] // LANES
    _store_token_tiles(h2_ref.at[0, pl.ds(rows.start * nsub, (rows.stop - rows.start) * nsub)], h2)
    h_hi = h2.astype(BF16)
    h_lo = (h2 - h_hi.astype(F32)).astype(BF16)
    work = (_dot(h_hi, wrh_ref[...]) + _dot(h_hi, wrl_ref[...]) + _dot(h_lo, wrh_ref[...])
            + br_ref[...])
    lane = lax.broadcasted_iota(jnp.int32, work.shape, 1)
    code = jnp.zeros(work.shape, F32)
    vals = []
    for k in range(TOP_K):
        top = work.max(axis=-1, keepdims=True)
        idx = jnp.where(work == top, lane, LANES).min(axis=-1, keepdims=True)
        sel = lane == idx
        work = jnp.where(sel, -jnp.inf, work)
        code = jnp.where(sel, k + 1.0, code)
        vals.append(top)
    den = sum(jnp.exp(v - vals[0]) for v in vals)
    wts = jnp.zeros(work.shape, F32)
    for k in range(TOP_K):
        wts = jnp.where(lane == k, jnp.exp(vals[k] - vals[0]) / den, wts)
    code_ref[0, rows, :] = code
    wts_ref[0, rows, :] = wts
    picked = jnp.where(code > 0.0, 1.0, 0.0)
    cnt_ref[...] += jnp.broadcast_to(picked.sum(axis=0, keepdims=True), cnt_ref.shape)


def _merge(x, ya, yb, ga, gb, mod3, wa, wb, wo, g1, b1, wr, br):
    wr_hi = wr.astype(BF16)
    wr_lo = (wr - wr_hi.astype(F32)).astype(BF16)
    bsz, seq, d = x.shape
    tm = min(TM_MERGE, seq)
    tok = lambda n: pl.BlockSpec((1, tm, n), lambda b, i: (b, i, 0))
    full = lambda a: pl.BlockSpec(a.shape, lambda b, i: (0,) * a.ndim)
    return pl.pallas_call(
        _merge_kernel,
        out_shape=[jax.ShapeDtypeStruct((bsz, seq, d), F32),
                   jax.ShapeDtypeStruct((bsz, seq * (d // LANES), LANES), F32),
                   jax.ShapeDtypeStruct((bsz, seq, LANES), F32),
                   jax.ShapeDtypeStruct((bsz, seq, LANES), F32),
                   jax.ShapeDtypeStruct((8, LANES), F32)],
        grid=(bsz, seq // tm),
        in_specs=[tok(d), tok(A_WIDTH), tok(B_HEADS * B_V_DIM), tok(d), tok(d),
                  pl.BlockSpec((1, 6, d), lambda b, i: (b, 0, 0)),
                  full(wa), full(wb), full(wo), full(g1), full(b1), full(wr_hi), full(wr_lo),
                  full(br)],
        out_specs=[tok(d),
                   pl.BlockSpec((1, tm * (d // LANES), LANES), lambda b, i: (b, i, 0)),
                   tok(LANES), tok(LANES), pl.BlockSpec((8, LANES), lambda b, i: (0, 0))],
        compiler_params=_params(("arbitrary", "arbitrary"), 48),
    )(x, ya, yb, ga, gb, mod3, wa, wb, wo, g1, b1, wr_hi, wr_lo, br)


def _route_kernel(code_ref, cnt_ref, pos_ref, tmap_ref, nv_ref, carry_s, pstart_s, *, tr, tm):
    i = pl.program_id(0)
    lane = lax.broadcasted_iota(jnp.int32, (tr, LANES), 1)
    code = code_ref[...]
    picked = jnp.where(code > 0.0, 1.0, 0.0)
    tile_count = jnp.broadcast_to(picked.sum(axis=0, keepdims=True), (8, LANES))

    @pl.when(i == 0)
    def _():
        lane8 = lax.broadcasted_iota(jnp.int32, (8, LANES), 1)
        padded = jnp.floor((cnt_ref[...] + (tm - 1)) * (1.0 / tm)) * tm
        cum = padded
        shift = 1
        while shift < N_EXPERTS:
            cum = cum + jnp.where(lane8 >= shift, pltpu.roll(cum, shift, 1), 0.0)
            shift *= 2
        pstart_s[...] = cum - padded
        carry_s[...] = jnp.zeros_like(carry_s)
        nt = tmap_ref.shape[0]
        tile_start = (lax.broadcasted_iota(jnp.int32, (nt, LANES), 0) * tm).astype(F32)
        lane_t = lax.broadcasted_iota(jnp.int32, (nt, LANES), 1)
        done = jnp.where((lane_t < N_EXPERTS) & (cum[0:1] <= tile_start), 1.0, 0.0)
        te = jnp.minimum(done.sum(axis=-1, keepdims=True), N_EXPERTS - 1.0)
        tmap_ref[...] = jnp.broadcast_to(te, (nt, LANES)).astype(jnp.int32)
        total = jnp.where(lane8 == N_EXPERTS - 1, cum, 0.0).sum(axis=-1, keepdims=True)
        row8 = lax.broadcasted_iota(jnp.int32, (8, LANES), 0)
        meta = jnp.where(row8 == 0, jnp.broadcast_to(total * (1.0 / tm), (8, LANES)),
                         jnp.where(row8 == 1, cum - padded + cnt_ref[...], cum))
        nv_ref[...] = meta.astype(jnp.int32)

    r = lax.broadcasted_iota(jnp.int32, (tr, tr), 0)
    c = lax.broadcasted_iota(jnp.int32, (tr, tr), 1)
    tri = jnp.where(c < r, 1.0, 0.0).astype(BF16)
    base = _dot(tri, picked.astype(BF16)) + (carry_s[0:1] + pstart_s[0:1])
    carry_s[...] += tile_count
    pos = jnp.zeros((tr, LANES), F32)
    for k in range(TOP_K):
        pk = jnp.where(code == k + 1.0, base, 0.0).sum(axis=-1, keepdims=True)
        pos = jnp.where(lane == k, pk, pos)
    pos_ref[...] = pos.astype(jnp.int32)


def _route(code, cnt, tm):
    n = code.shape[0]
    tr = min(TR_ROUTE, n)
    return pl.pallas_call(
        functools.partial(_route_kernel, tr=tr, tm=tm),
        out_shape=[jax.ShapeDtypeStruct((n, LANES), jnp.int32),
                   jax.ShapeDtypeStruct((MAX_TILES_PAD, LANES), jnp.int32),
                   jax.ShapeDtypeStruct((8, LANES), jnp.int32)],
        grid=(n // tr,),
        in_specs=[pl.BlockSpec((tr, LANES), lambda i: (i, 0)),
                  pl.BlockSpec((8, LANES), lambda i: (0, 0))],
        out_specs=[pl.BlockSpec((tr, LANES), lambda i: (i, 0)),
                   pl.BlockSpec((MAX_TILES_PAD, LANES), lambda i: (0, 0)),
                   pl.BlockSpec((8, LANES), lambda i: (0, 0))],
        scratch_shapes=[pltpu.VMEM((8, LANES), F32)] * 2,
        compiler_params=_params(("arbitrary",)),
    )(code, cnt)


def _dispatch_kernel(pos_ref, fill_ref, h_ref, xs_ref, ztile_s, sem, fill_sem, tail_sem,
                     *, tm, nsub, tm_moe):
    def tile(ref, r):
        return ref.at[pl.ds(pl.multiple_of(r * nsub, nsub), nsub)]

    @pl.when(pl.program_id(0) == 0)
    def _():
        ztile_s[...] = jnp.zeros_like(ztile_s)

        def tail_copy(t):
            rows = tm_moe * nsub
            return pltpu.make_async_copy(
                ztile_s, xs_ref.at[pl.ds(pl.multiple_of(t * rows, rows), rows)], tail_sem)

        sizes = [1 << b for b in reversed(range(tm_moe.bit_length() - 1))]

        def fill_pieces(e, act):
            lo, hi = fill_ref[LANES + e], fill_ref[2 * LANES + e]
            at = lo
            for size in sizes:
                has = ((hi - lo) & size) != 0
                copy = pltpu.make_async_copy(
                    ztile_s.at[pl.ds(0, size * nsub)],
                    xs_ref.at[pl.ds(pl.multiple_of(at * nsub, nsub), size * nsub)], fill_sem)
                pl.when(has)(functools.partial(act, copy))
                at = at + jnp.where(has, size, 0)

        def fill_start(e, c):
            fill_pieces(e, lambda copy: copy.start())
            return c

        def fill_wait(e, c):
            fill_pieces(e, lambda copy: copy.wait())
            return c

        n_tiles = xs_ref.shape[0] // (tm_moe * nsub)

        def tail_start(t, c):
            tail_copy(t).start()
            return c

        def tail_drain(_, c):
            tail_copy(0).wait()
            return c

        lax.fori_loop(0, N_EXPERTS, fill_start, 0)
        lax.fori_loop(fill_ref[0], n_tiles, tail_start, 0)
        lax.fori_loop(0, N_EXPERTS, fill_wait, 0)
        lax.fori_loop(fill_ref[0], n_tiles, tail_drain, 0)

    def issue(j, c):
        for r in range(COMBINE_GROUP):
            row = pl.multiple_of(j * COMBINE_GROUP, COMBINE_GROUP) + r
            for k in range(TOP_K):
                pltpu.make_async_copy(tile(h_ref, row), tile(xs_ref, pos_ref[row * TOP_K + k]),
                                      sem).start(priority=k % 2)
        return c

    lax.fori_loop(0, tm // COMBINE_GROUP, issue, 0)
    for _ in range(TOP_K):
        pltpu.make_async_copy(h_ref, xs_ref.at[pl.ds(0, tm * nsub)], sem).wait()


def _dispatch(pos_flat, fill, h2, cap, nsub, tm_moe):
    n = h2.shape[0] // nsub
    tm = min(TM_ROWS, n)
    return pl.pallas_call(
        functools.partial(_dispatch_kernel, tm=tm, nsub=nsub, tm_moe=tm_moe),
        out_shape=jax.ShapeDtypeStruct((cap * nsub, LANES), F32),
        grid=(n // tm,),
        in_specs=[pl.BlockSpec((tm * TOP_K,), lambda i: (i,), memory_space=pltpu.SMEM),
                  pl.BlockSpec(fill.shape, lambda i: (0,), memory_space=pltpu.SMEM),
                  pl.BlockSpec((tm * nsub, LANES), lambda i: (i, 0))],
        out_specs=pl.BlockSpec(memory_space=pl.ANY),
        scratch_shapes=[pltpu.VMEM((tm_moe * nsub, LANES), F32),
                        pltpu.SemaphoreType.DMA(()), pltpu.SemaphoreType.DMA(()),
                        pltpu.SemaphoreType.DMA(())],
        compiler_params=_params(("arbitrary",)),
    )(pos_flat, fill, h2)


def _moe_kernel(te_ref, nv_ref, x_ref, bg_ref, bu_ref, bd_ref, wg_hbm, wu_hbm, wd_hbm,
                y_ref, wf_s, wg_s, wu_s, wd_s, slot_s, sem):
    i = pl.program_id(0)
    nv = nv_ref[0]
    last = te_ref.shape[0] - 1

    def weight_copies(e, slot):
        return [pltpu.make_async_copy(w.at[e], wf_s.at[slot, j], sem.at[slot])
                for j, w in enumerate((wg_hbm, wu_hbm, wd_hbm))]

    @pl.when(i < nv)
    def _():
        e = te_ref[i]

        @pl.when(i == 0)
        def _():
            slot_s[0] = 0
            for c in weight_copies(e, 0):
                c.start()

        @pl.when((i == 0) | (e != te_ref[jnp.maximum(i - 1, 0)]))
        def _():
            slot = slot_s[0]
            for c in weight_copies(e, slot):
                c.wait()
            nxt = lax.while_loop(lambda j: (j < nv) & (te_ref[jnp.minimum(j, last)] == e),
                                 lambda j: j + 1, i + 1)

            @pl.when(nxt < nv)
            def _():
                for c in weight_copies(te_ref[jnp.minimum(nxt, last)], 1 - slot):
                    c.start()

            wg_s[...] = wf_s[slot, 0].astype(BF16)
            wu_s[...] = wf_s[slot, 1].astype(BF16)
            wd_s[...] = wf_s[slot, 2].astype(BF16)
            slot_s[0] = 1 - slot

        xb = _load_token_tiles(x_ref, wg_s.shape[0] // LANES).astype(BF16)
        g = _dot(xb, wg_s[...]) + bg_ref[0]
        u = _dot(xb, wu_s[...]) + bu_ref[0]
        g = jnp.minimum(g, SWIGLU_LIMIT)
        u = jnp.clip(u, -SWIGLU_LIMIT, SWIGLU_LIMIT)
        h = (u + 1.0) * (g * jax.nn.sigmoid(SWIGLU_ALPHA * g))
        _store_token_tiles(y_ref, _dot(h.astype(BF16), wd_s[...]) + bd_ref[0])

    @pl.when(i >= nv)
    def _():
        y_ref[...] = jnp.zeros_like(y_ref)


def _moe(tile_expert, n_valid, xs, w_gate, b_gate, w_up, b_up, w_down, b_down, tm):
    e, d, f = w_gate.shape
    assert d == f
    nsub = d // LANES
    cap = xs.shape[0] // nsub
    n_tiles = cap // tm

    def tile(i, te, nv):
        return jnp.minimum(i, nv[0] - 1)

    exp3 = lambda i, te, nv: (te[tile(i, te, nv)], 0, 0)
    grid_spec = pltpu.PrefetchScalarGridSpec(
        num_scalar_prefetch=2,
        grid=(n_tiles,),
        in_specs=[pl.BlockSpec((tm * nsub, LANES), lambda i, te, nv: (tile(i, te, nv), 0)),
                  pl.BlockSpec((1, 1, f), exp3), pl.BlockSpec((1, 1, f), exp3),
                  pl.BlockSpec((1, 1, d), exp3),
                  pl.BlockSpec(memory_space=pl.ANY), pl.BlockSpec(memory_space=pl.ANY),
                  pl.BlockSpec(memory_space=pl.ANY)],
        out_specs=pl.BlockSpec((tm * nsub, LANES), lambda i, te, nv: (i, 0)),
        scratch_shapes=[pltpu.VMEM((2, 3, d, f), F32),
                        pltpu.VMEM((d, f), BF16), pltpu.VMEM((d, f), BF16), pltpu.VMEM((f, d), BF16),
                        pltpu.SMEM((1,), jnp.int32), pltpu.SemaphoreType.DMA((2,))],
    )
    return pl.pallas_call(
        _moe_kernel,
        out_shape=jax.ShapeDtypeStruct((cap * nsub, LANES), F32),
        grid_spec=grid_spec,
        compiler_params=_params(("arbitrary",), 56),
    )(tile_expert, n_valid, xs, b_gate.reshape(e, 1, f), b_up.reshape(e, 1, f),
      b_down.reshape(e, 1, d), w_gate, w_up, w_down)


def _combine_kernel(pos_ref, posn_ref, w_ref, x1_ref, mod_ref, g_ref, b_ref, y_ref, o_ref,
                    buf, sem, *, tm):
    i = pl.program_id(0)
    n = pl.num_programs(0)
    nsub = x1_ref.shape[1] // LANES

    group = COMBINE_GROUP

    def issue_rows(p_ref, slot, r0):
        for r in range(group):
            for k in range(TOP_K):
                row = r0 + r
                src = pl.ds(pl.multiple_of(p_ref[row * TOP_K + k] * nsub, nsub), nsub)
                dst = pl.ds(pl.multiple_of(row * nsub, nsub), nsub)
                pltpu.make_async_copy(y_ref.at[src], buf.at[slot, k, dst],
                                      sem.at[slot]).start(priority=k % 2)

    def reduce_rows(slot, r0):
        rows = pl.ds(r0, group)
        w = w_ref[rows, :]
        f = jnp.zeros((group, nsub * LANES), F32)
        for k in range(TOP_K):
            yk = jnp.concatenate(
                [buf[slot, k, pl.ds(r0 * nsub + s, group, stride=nsub), :] for s in range(nsub)],
                axis=1)
            f = f + w[:, k:k + 1] * yk
        o_ref[rows, :] = f

    @pl.when(i == 0)
    def _():
        def body(j, c):
            issue_rows(pos_ref, 0, pl.multiple_of(j * group, group))
            return c
        lax.fori_loop(0, tm // group, body, 0)

    slot = i % 2
    for k in range(TOP_K):
        pltpu.make_async_copy(y_ref.at[pl.ds(0, tm * nsub)], buf.at[slot, k], sem.at[slot]).wait()

    @pl.when(i + 1 < n)
    def _():
        def body(j, c):
            r0 = pl.multiple_of(j * group, group)
            issue_rows(posn_ref, 1 - slot, r0)
            reduce_rows(slot, r0)
            return c
        lax.fori_loop(0, tm // group, body, 0)

    @pl.when(i + 1 == n)
    def _():
        def body(j, c):
            reduce_rows(slot, pl.multiple_of(j * group, group))
            return c
        lax.fori_loop(0, tm // group, body, 0)

    o_ref[...] = _layer_norm(DEEPNORM_ALPHA * x1_ref[...] + mod_ref[0, 5:6, :] * o_ref[...],
                             g_ref[...], b_ref[...])


def _combine(pos_flat, w, x1, mod3, g2, b2, y, seq):
    n, d = x1.shape
    tm = min(TM_ROWS, seq)
    nsteps = n // tm
    per_seq = seq // tm
    return pl.pallas_call(
        functools.partial(_combine_kernel, tm=tm),
        out_shape=jax.ShapeDtypeStruct((n, d), F32),
        grid=(nsteps,),
        in_specs=[pl.BlockSpec((tm * TOP_K,), lambda i: (i,), memory_space=pltpu.SMEM),
                  pl.BlockSpec((tm * TOP_K,), lambda i: (jnp.minimum(i + 1, nsteps - 1),),
                               memory_space=pltpu.SMEM),
                  pl.BlockSpec((tm, LANES), lambda i: (i, 0)),
                  pl.BlockSpec((tm, d), lambda i: (i, 0)),
                  pl.BlockSpec((1, 6, d), lambda i: (i // per_seq, 0, 0)),
                  pl.BlockSpec(g2.shape, lambda i: (0, 0)),
                  pl.BlockSpec(b2.shape, lambda i: (0, 0)),
                  pl.BlockSpec(memory_space=pl.ANY)],
        out_specs=pl.BlockSpec((tm, d), lambda i: (i, 0)),
        scratch_shapes=[pltpu.VMEM((2, TOP_K, tm * (d // LANES), LANES), F32),
                        pltpu.SemaphoreType.DMA((2,))],
        compiler_params=_params(("arbitrary",), 32),
    )(pos_flat, pos_flat, w, x1, mod3, g2, b2, y)


def _prep_w_uq(w_uq):
    r = w_uq.shape[0]
    w = w_uq.reshape(r, B_HEADS, B_QK_DIM)
    half = B_ROPE_DIM // 2
    rot = jnp.concatenate([-w[:, :, B_NOPE_DIM + half:], w[:, :, B_NOPE_DIM:B_NOPE_DIM + half]],
                          axis=-1).reshape(r, B_HEADS * B_ROPE_DIM)
    w = jnp.pad(w, ((0, 0), (0, 0), (0, LANES - B_QK_DIM))).reshape(r, B_HEADS * LANES)
    return jnp.concatenate([w, rot], axis=1).astype(BF16)


def _prep_w_ukv(w_ukv):
    r = w_ukv.shape[0]
    w = w_ukv.reshape(r, B_HEADS, B_NOPE_DIM + B_V_DIM)
    wk = jnp.pad(w[:, :, :B_NOPE_DIM], ((0, 0), (0, 0), (0, LANES - B_NOPE_DIM)))
    wv = w[:, :, B_NOPE_DIM:].reshape(r, B_HEADS // 2, 2 * B_V_DIM)
    wv = jnp.pad(wv, ((0, 0), (0, 0), (0, V_PAIR_LANES - 2 * B_V_DIM)))
    return (wk.reshape(r, B_HEADS * LANES).astype(BF16),
            wv.reshape(r, B_HEADS // 2 * V_PAIR_LANES).astype(BF16))


def _rope_tables(positions):
    half = B_ROPE_DIM // 2
    freqs = ROPE_THETA ** (-jnp.arange(half, dtype=F32) / half)
    ang = positions.astype(F32).reshape(1, -1) * freqs[:, None]
    cos, sin = jnp.cos(ang), jnp.sin(ang)
    z = jnp.zeros((B_ROPE_DIM, ang.shape[1]), F32)
    return jnp.concatenate([sin, sin, z, cos, cos, z], 0)


def kernel(x, c, positions, w_ada, b_ada, w_in, rms_q, w_uq, rms_kv, w_ukv, rel_bias,
           w_branch_a, w_branch_b, w_out, ln1_g, ln1_b, w_router, b_router,
           w_gate, b_gate, w_up, b_up, w_down, b_down, ln2_g, ln2_b):
    bsz, seq, d = x.shape
    n_tok = bsz * seq
    assert w_ada.shape[0] == DEPTH == 1
    row = lambda v: v.reshape(1, -1)

    mod3 = _ada(c, w_ada[0], b_ada[0]).reshape(bsz, 6, d)

    qk, va, cq, ckv, kr, ga, gb = _inproj(x, mod3, jnp.swapaxes(w_in, 1, 2))
    ya = _attn_a(qk, va, _bias_table(rel_bias[0], min(TQ_A, seq)))
    wk, wv = _prep_w_ukv(w_ukv[0])
    qm, km, vm = _mla_proj(cq, ckv, kr, _rope_tables(positions), row(rms_q[0]), row(rms_kv[0]),
                           _prep_w_uq(w_uq[0]), wk, wv)
    yb = _mla_attn(qm, km, vm)

    wr = jnp.pad(w_router[0], ((0, 0), (0, LANES - N_EXPERTS)))
    br = jnp.pad(b_router[0], (0, LANES - N_EXPERTS), constant_values=NEG_INF)
    x1, h2, code, wts, cnt = _merge(x, ya, yb, ga, gb, mod3,
                                    w_branch_a[0].astype(BF16), w_branch_b[0].astype(BF16),
                                    w_out[0].astype(BF16), row(ln1_g[0]), row(ln1_b[0]), wr, row(br))
    wts = wts.reshape(n_tok, LANES)

    tm = TM_MOE
    n_tiles = -(-(n_tok * TOP_K) // tm) + N_EXPERTS
    assert n_tiles <= MAX_TILES_PAD
    pos, tmap, meta = _route(code.reshape(n_tok, LANES), cnt, tm)
    pos_flat = pos[:, :TOP_K].reshape(-1)
    fill = meta[:3].reshape(-1)
    nsub = d // LANES
    xs = _dispatch(pos_flat, fill, h2.reshape(n_tok * nsub, LANES), n_tiles * tm, nsub, tm)
    y = _moe(tmap[:n_tiles, 0], meta[0, :1], xs, w_gate[0], b_gate[0], w_up[0], b_up[0],
             w_down[0], b_down[0], tm)
    out = _combine(pos_flat, wts, x1.reshape(n_tok, d), mod3, row(ln2_g[0]), row(ln2_b[0]), y, seq)
    return out.reshape(bsz, seq, d)
```

```python
import functools

import jax
import jax.numpy as jnp
from jax import lax
from jax.experimental import pallas as pl
from jax.experimental.pallas import tpu as pltpu

F32 = jnp.float32
BF16 = jnp.bfloat16

CHUNK = 64
A_HEADS = 8
A_HEAD_DIM = 64
A_LEFT_CHUNKS = 8
A_MAX_REL = 128
A_WIDTH = A_HEADS * A_HEAD_DIM
B_HEADS = 8
B_NOPE_DIM = 64
B_ROPE_DIM = 32
B_V_DIM = 64
B_Q_LORA = 384
B_KV_LORA = 256
B_QK_DIM = B_NOPE_DIM + B_ROPE_DIM
ROPE_THETA = 10000.0
N_EXPERTS = 32
TOP_K = 4
SWIGLU_LIMIT = 7.0
SWIGLU_ALPHA = 1.702
DEPTH = 1
DEEPNORM_ALPHA = (2.0 * DEPTH) ** 0.25
LN_EPS = 1e-5
RMS_EPS = 1e-6
NEG_INF = -1e30
LOG2_E = 1.4426950408889634

LANES = 128
MXU_COLS = 256
V_PAIR_LANES = MXU_COLS

TM_PROJ = 512
TM_MLA_PROJ = 1024
TQ_A = 256
A_QBLOCKS = 8
TQ_B = 512
TM_MERGE = 1024
MERGE_SPLIT = 2
TR_ROUTE = 1024
TM_MOE = 384
TM_ROWS = 256
COMBINE_GROUP = 8
MAX_TILES_PAD = 512


def _params(sem, vmem_mb=None):
    return pltpu.CompilerParams(
        dimension_semantics=sem,
        vmem_limit_bytes=None if vmem_mb is None else vmem_mb << 20)


def _dot(a, b):
    return jnp.dot(a, b, preferred_element_type=F32)


def _dot_nt(a, b):
    return lax.dot_general(a, b, (((1,), (1,)), ((), ())), preferred_element_type=F32)


def _store_token_tiles(ref, x):
    rows, d = x.shape
    nsub = d // LANES
    for s in range(nsub):
        ref[pl.ds(s, rows, stride=nsub), :] = x[:, s * LANES:(s + 1) * LANES]


def _load_token_tiles(ref, nsub):
    rows = ref.shape[0] // nsub
    return jnp.concatenate([ref[pl.ds(s, rows, stride=nsub), :] for s in range(nsub)], axis=1)


def _layer_norm(x, g, b):
    mu = jnp.mean(x, axis=-1, keepdims=True)
    xc = x - mu
    var = jnp.mean(xc * xc, axis=-1, keepdims=True)
    return xc * lax.rsqrt(var + LN_EPS) * g + b


def _ada_kernel(c_ref, w_ref, b_ref, o_ref):
    c = c_ref[...]
    sc = c * jax.nn.sigmoid(c)
    w = w_ref[...]
    s_hi, w_hi = sc.astype(BF16), w.astype(BF16)
    s_lo = (sc - s_hi.astype(F32)).astype(BF16)
    w_lo = (w - w_hi.astype(F32)).astype(BF16)
    o_ref[...] = _dot(s_hi, w_hi) + _dot(s_hi, w_lo) + _dot(s_lo, w_hi) + b_ref[...]


def _ada(c, w_ada, b_ada):
    bsz, d = c.shape
    m = w_ada.shape[1]
    tn = 1024
    return pl.pallas_call(
        _ada_kernel,
        out_shape=jax.ShapeDtypeStruct((bsz, m), F32),
        grid=(m // tn,),
        in_specs=[pl.BlockSpec((bsz, d), lambda j: (0, 0)),
                  pl.BlockSpec((d, tn), lambda j: (0, j)),
                  pl.BlockSpec((1, tn), lambda j: (0, j))],
        out_specs=pl.BlockSpec((bsz, tn), lambda j: (0, j)),
        compiler_params=_params(("arbitrary",)),
    )(c, w_ada, b_ada.reshape(1, m))


def _inproj_kernel(x_ref, mod_ref, win_hbm, qk_ref, va_ref, cq_ref, ckv_ref, kr_ref, ga_ref, gb_ref,
                   wf_s, wqk_ref, wv_ref, wcq_ref, wckv_ref, wkr_ref, wga_ref, wgb_ref, sem):
    @pl.when((pl.program_id(0) == 0) & (pl.program_id(1) == 0))
    def _():
        copy = pltpu.make_async_copy(win_hbm.at[0], wf_s, sem)
        copy.start()
        copy.wait()
        d = wf_s.shape[1]
        chunk = 2 * LANES
        off = 0
        for dst, n in ((wqk_ref, 2 * A_WIDTH), (wv_ref, A_WIDTH), (wcq_ref, B_Q_LORA),
                       (wckv_ref, B_KV_LORA), (None, B_ROPE_DIM), (wga_ref, d), (wgb_ref, d)):
            if dst is None:
                blk = wf_s[off:off + LANES, :].T
                lane = lax.broadcasted_iota(jnp.int32, blk.shape, 1)
                blk = pltpu.roll(jnp.where(lane < n, blk, 0.0), B_NOPE_DIM, 1)
                wkr_ref[...] = blk.astype(BF16)
            else:
                for c0 in range(0, n, chunk):
                    cn = min(chunk, n - c0)
                    blk = wf_s[off + c0:off + c0 + cn, :].T
                    if dst is wqk_ref and c0 < A_WIDTH:
                        blk = blk * (A_HEAD_DIM ** -0.5 * LOG2_E)
                    dst[:, c0:c0 + cn] = blk.astype(BF16)
            off += n

    m = mod_ref[0]
    h = (x_ref[0] * (1.0 + m[1:2]) + m[0:1]).astype(BF16)
    for w_ref, ref in ((wqk_ref, qk_ref), (wcq_ref, cq_ref), (wckv_ref, ckv_ref),
                       (wkr_ref, kr_ref), (wga_ref, ga_ref), (wgb_ref, gb_ref)):
        ref[0] = _dot(h, w_ref[...]).astype(ref.dtype)
    v = _dot(h, wv_ref[...]).astype(BF16)
    ones_blk = jnp.ones((v.shape[0], LANES), BF16)
    parts = []
    for hp in range(A_WIDTH // LANES):
        parts += [v[:, hp * LANES:(hp + 1) * LANES], ones_blk]
    va_ref[0] = jnp.concatenate(parts, axis=1)


def _inproj(x, mod3, w_in):
    bsz, seq, d = x.shape
    tm = min(TM_PROJ, seq)
    widths = (2 * A_WIDTH, A_WIDTH // LANES * V_PAIR_LANES, B_Q_LORA, B_KV_LORA, LANES, d, d)
    dtypes = (BF16, BF16, BF16, BF16, F32, BF16, BF16)
    w_widths = (2 * A_WIDTH, A_WIDTH, B_Q_LORA, B_KV_LORA, LANES, d, d)
    assert w_in.shape[0] == 1 and w_in.shape[1] == sum(w_widths) - LANES + B_ROPE_DIM
    return pl.pallas_call(
        _inproj_kernel,
        out_shape=[jax.ShapeDtypeStruct((bsz, seq, n), dt) for n, dt in zip(widths, dtypes)],
        grid=(bsz, seq // tm),
        in_specs=[pl.BlockSpec((1, tm, d), lambda b, i: (b, i, 0)),
                  pl.BlockSpec((1, 6, d), lambda b, i: (b, 0, 0)),
                  pl.BlockSpec(memory_space=pl.ANY)],
        out_specs=[pl.BlockSpec((1, tm, n), lambda b, i: (b, i, 0)) for n in widths],
        scratch_shapes=[pltpu.VMEM(w_in.shape[1:], F32)]
                       + [pltpu.VMEM((d, n), BF16) for n in w_widths]
                       + [pltpu.SemaphoreType.DMA(())],
        compiler_params=_params(("arbitrary", "arbitrary"), 56),
    )(x, mod3, w_in)


def _attn_a_kernel(q_ref, k_ref, v_ref, bias_ref, o_ref, *, tq, nd, nblk, steps):
    band = nd * tq
    lane = lax.broadcasted_iota(jnp.int32, (1, LANES), 1)

    def shifted(base, missing):
        cut = missing * tq
        if cut == 0:
            return base
        return jnp.concatenate([base[:, cut:], jnp.full((tq, cut), NEG_INF, F32)], axis=1)

    def bias_for(hh, i):
        base = bias_ref[hh]
        if isinstance(i, int):
            return shifted(base, max(nd - 1 - i, 0))
        out = base
        for first in range(nd - 1):
            out = jnp.where(i == first, shifted(base, nd - 1 - first), out)
        return out

    for sub in range(nblk):
        if steps == 1:
            i = sub
            rows = slice(max(i - (nd - 1), 0) * tq, max(i - (nd - 1), 0) * tq + band)
        else:
            i = pl.program_id(2) * nblk + sub
            rows = pl.ds(pl.multiple_of(jnp.maximum(i - (nd - 1), 0) * tq, tq), band)
        q = q_ref[0, sub * tq:(sub + 1) * tq, :]
        k = k_ref[0, rows, :]
        v = v_ref[0, rows, :]
        zero = jnp.zeros_like(k)
        kk = jnp.concatenate([jnp.where(lane < A_HEAD_DIM, k, zero),
                              jnp.where(lane >= A_HEAD_DIM, k, zero)], axis=0)
        s = _dot_nt(q, kk)
        ps = []
        for hh in range(2):
            sh = s[:, hh * band:(hh + 1) * band] + bias_for(hh, i)
            m = sh.max(axis=-1, keepdims=True)
            ps.append(jnp.exp2(sh - m).astype(BF16))
        acc = _dot(jnp.concatenate(ps, axis=0), v)
        out = acc[:, :LANES] / acc[:, LANES:]
        o_ref[0, sub * tq:(sub + 1) * tq, :] = jnp.where(
            lane < A_HEAD_DIM, out[:tq], out[tq:]).astype(o_ref.dtype)


def _attn_a(qk, va, bias_tab):
    bsz, seq, _ = qk.shape
    tq = min(TQ_A, seq)
    nd = bias_tab.shape[2] // tq
    assert seq >= nd * tq
    ncb = A_WIDTH // LANES
    nblk = min(A_QBLOCKS, seq // tq)
    steps = seq // (nblk * tq)
    return pl.pallas_call(
        functools.partial(_attn_a_kernel, tq=tq, nd=nd, nblk=nblk, steps=steps),
        out_shape=jax.ShapeDtypeStruct((bsz, seq, A_WIDTH), BF16),
        grid=(ncb, bsz, steps),
        in_specs=[pl.BlockSpec((1, nblk * tq, LANES), lambda hp, b, i: (b, i, hp)),
                  pl.BlockSpec((1, seq, LANES), lambda hp, b, i: (b, 0, ncb + hp)),
                  pl.BlockSpec((1, seq, V_PAIR_LANES), lambda hp, b, i: (b, 0, hp)),
                  pl.BlockSpec((2, tq, nd * tq), lambda hp, b, i: (hp, 0, 0))],
        out_specs=pl.BlockSpec((1, nblk * tq, LANES), lambda hp, b, i: (b, i, hp)),
        compiler_params=_params(("arbitrary",) * 3, 40),
    )(qk, qk, va, bias_tab)


def _bias_table(rel_bias, tq):
    nd = (A_LEFT_CHUNKS * CHUNK + tq - 1) // tq + 1
    width = nd * tq
    period = width + tq
    u = jnp.arange(period)
    dist = jnp.where(u < width, (nd - 1) * tq - u, (nd - 1) * tq + period - u)
    vec = rel_bias[:, jnp.clip(dist, -A_MAX_REL, A_MAX_REL) + A_MAX_REL].astype(F32)
    toep = jnp.tile(vec, (1, tq))[:, :tq * (period - 1)].reshape(-1, tq, period - 1)[:, :, :width]
    qi = jnp.arange(tq)[:, None]
    x = jnp.arange(width)[None, :]
    dchunk = qi // CHUNK - x // CHUNK + (nd - 1) * (tq // CHUNK)
    valid = (dchunk >= 0) & (dchunk <= A_LEFT_CHUNKS)
    return jnp.where(valid[None], toep * LOG2_E, NEG_INF)


def _mla_proj_kernel(cq_ref, ckv_ref, kr_ref, tab_ref, rq_ref, rkv_ref,
                     wq_ref, wk_ref, wv_ref, q_out, k_out, v_out):
    def rms(x, g):
        xf = x.astype(F32)
        ms = jnp.mean(xf * xf, axis=-1, keepdims=True)
        return (xf * lax.rsqrt(ms + RMS_EPS) * g).astype(BF16)

    cqn = rms(cq_ref[0], rq_ref[...])
    ckvn = rms(ckv_ref[0], rkv_ref[...])
    q = _dot(cqn, wq_ref[...])
    kn = _dot(ckvn, wk_ref[...])
    vlane = lax.broadcasted_iota(jnp.int32, (1, v_out.shape[-1]), 1)
    ones_lane = jnp.where(vlane % V_PAIR_LANES >= 2 * B_V_DIM, 1.0, 0.0)
    v_out[0] = (_dot(ckvn, wv_ref[...]) + ones_lane).astype(v_out.dtype)
    tab = tab_ref[...].T
    tlane = lax.broadcasted_iota(jnp.int32, tab.shape, 1)
    rope_end = B_NOPE_DIM + B_ROPE_DIM
    in_rope = (tlane >= B_NOPE_DIM) & (tlane < rope_end)
    cos_t = jnp.where(tlane < B_NOPE_DIM, 1.0, jnp.where(in_rope, tab, 0.0))
    sin_t = jnp.where(in_rope, pltpu.roll(tab, B_NOPE_DIM, 1), 0.0)
    first = tlane < B_NOPE_DIM + B_ROPE_DIM // 2
    sin_a = jnp.where(first, -sin_t, 0.0)
    sin_b = jnp.where(first, 0.0, sin_t)

    def rope(x, tables):
        c, sa, sb = tables
        return (x * c + pltpu.roll(x, LANES - B_ROPE_DIM // 2, 1) * sa
                + pltpu.roll(x, B_ROPE_DIM // 2, 1) * sb)

    kpe = rope(kr_ref[0], (cos_t, sin_a, sin_b))
    scale = B_QK_DIM ** -0.5 * LOG2_E
    cos_q, sin_q = cos_t * scale, sin_t * scale
    nq = B_HEADS * LANES
    per_block = LANES // B_ROPE_DIM
    for h in range(B_HEADS):
        sl = slice(h * LANES, (h + 1) * LANES)
        blk = nq + (h // per_block) * LANES
        partner = q[:, blk:blk + LANES]
        shift = (B_NOPE_DIM - B_ROPE_DIM * (h % per_block)) % LANES
        if shift:
            partner = pltpu.roll(partner, shift, 1)
        q_out[0, :, sl] = (q[:, sl] * cos_q + partner * sin_q).astype(q_out.dtype)
        k_out[0, :, sl] = (kn[:, sl] + kpe).astype(k_out.dtype)


def _mla_proj(cq, ckv, kr, rope_tab, rms_q, rms_kv, wq, wk, wv):
    bsz, seq, _ = cq.shape
    tm = min(TM_MLA_PROJ, seq)
    tok = lambda n: pl.BlockSpec((1, tm, n), lambda b, i: (b, i, 0))
    full = lambda a: pl.BlockSpec(a.shape, lambda b, i: (0,) * a.ndim)
    return pl.pallas_call(
        _mla_proj_kernel,
        out_shape=[jax.ShapeDtypeStruct((bsz, seq, B_HEADS * LANES), BF16),
                   jax.ShapeDtypeStruct((bsz, seq, B_HEADS * LANES), BF16),
                   jax.ShapeDtypeStruct((bsz, seq, wv.shape[1]), BF16)],
        grid=(bsz, seq // tm),
        in_specs=[tok(B_Q_LORA), tok(B_KV_LORA), tok(LANES),
                  pl.BlockSpec((LANES, tm), lambda b, i: (0, b * (seq // tm) + i)),
                  full(rms_q), full(rms_kv), full(wq), full(wk), full(wv)],
        out_specs=[tok(B_HEADS * LANES), tok(B_HEADS * LANES), tok(wv.shape[1])],
        compiler_params=_params(("arbitrary", "arbitrary"), 32),
    )(cq, ckv, kr, rope_tab, rms_q, rms_kv, wq, wk, wv)


def _mla_attn_kernel(q_ref, k_ref, v_ref, o_ref, s_a, s_b, m_s, acc_s, *, t):
    lane = lax.broadcasted_iota(jnp.int32, (1, LANES), 1)
    nq = q_ref.shape[1] // t
    pairs = [(qi, kj) for qi in range(nq) for kj in range(qi + 1)]
    bufs = (s_a, s_b)

    def produce(qi, kj, s_ref):
        for hh in range(2):
            q = q_ref[0, qi * t:(qi + 1) * t, hh * LANES:(hh + 1) * LANES]
            k = k_ref[0, kj * t:(kj + 1) * t, hh * LANES:(hh + 1) * LANES]
            s_ref[hh] = _dot_nt(q, k)

    def consume(qi, kj, s_ref):
        v = v_ref[0, kj * t:(kj + 1) * t, :]
        outs = []
        for hh in range(2):
            s = s_ref[hh]
            if kj == qi:
                row_chunk = lax.broadcasted_iota(jnp.int32, (t, t), 0) // CHUNK
                col_chunk = lax.broadcasted_iota(jnp.int32, (t, t), 1) // CHUNK
                s = jnp.where(col_chunk <= row_chunk, s, NEG_INF)
            row_max = jnp.broadcast_to(s.max(axis=-1, keepdims=True), (t, LANES))
            if kj == 0:
                m_new = row_max
                acc = _dot(jnp.exp2(s - jnp.tile(m_new, (1, t // LANES))).astype(BF16), v)
            else:
                m_old = m_s[hh]
                m_new = jnp.maximum(m_old, row_max)
                p = jnp.exp2(s - jnp.tile(m_new, (1, t // LANES))).astype(BF16)
                alpha = jnp.exp2(m_old - m_new)
                acc = jnp.tile(alpha, (1, V_PAIR_LANES // LANES)) * acc_s[hh] + _dot(p, v)
            if kj == qi:
                outs.append(acc[:, :LANES] / acc[:, LANES:])
            else:
                acc_s[hh] = acc
                m_s[hh] = m_new
        if kj == qi:
            o_ref[0, qi * t:(qi + 1) * t, :] = jnp.where(
                lane < B_V_DIM, outs[0], outs[1]).astype(o_ref.dtype)

    produce(*pairs[0], bufs[0])
    for n, pair in enumerate(pairs):
        if n + 1 < len(pairs):
            produce(*pairs[n + 1], bufs[(n + 1) % 2])
        consume(*pair, bufs[n % 2])


def _mla_attn(qm, km, vm):
    bsz, seq, _ = qm.shape
    t = min(TQ_B, seq)
    return pl.pallas_call(
        functools.partial(_mla_attn_kernel, t=t),
        out_shape=jax.ShapeDtypeStruct((bsz, seq, B_HEADS * B_V_DIM), BF16),
        grid=(bsz, B_HEADS // 2),
        in_specs=[pl.BlockSpec((1, seq, 2 * LANES), lambda b, hp: (b, 0, hp)),
                  pl.BlockSpec((1, seq, 2 * LANES), lambda b, hp: (b, 0, hp)),
                  pl.BlockSpec((1, seq, V_PAIR_LANES), lambda b, hp: (b, 0, hp))],
        out_specs=pl.BlockSpec((1, seq, LANES), lambda b, hp: (b, 0, hp)),
        scratch_shapes=[pltpu.VMEM((2, t, t), F32), pltpu.VMEM((2, t, t), F32),
                        pltpu.VMEM((2, t, LANES), F32),
                        pltpu.VMEM((2, t, V_PAIR_LANES), F32)],
        compiler_params=_params(("arbitrary",) * 2, 40),
    )(qm, km, vm)


def _merge_kernel(x_ref, ya_ref, yb_ref, ga_ref, gb_ref, mod_ref, wa_ref, wb_ref, wo_ref,
                  g1_ref, b1_ref, wrh_ref, wrl_ref, br_ref, x1_ref, h2_ref, code_ref, wts_ref,
                  cnt_ref):
    @pl.when((pl.program_id(0) == 0) & (pl.program_id(1) == 0))
    def _():
        cnt_ref[...] = jnp.zeros_like(cnt_ref)

    tm = x_ref.shape[1]
    half = tm // MERGE_SPLIT
    for part in range(MERGE_SPLIT):
        _merge_rows(slice(part * half, (part + 1) * half), x_ref, ya_ref, yb_ref, ga_ref, gb_ref,
                    mod_ref, wa_ref, wb_ref, wo_ref, g1_ref, b1_ref, wrh_ref, wrl_ref, br_ref,
                    x1_ref, h2_ref, code_ref, wts_ref, cnt_ref)


def _merge_rows(rows, x_ref, ya_ref, yb_ref, ga_ref, gb_ref, mod_ref, wa_ref, wb_ref, wo_ref,
                g1_ref, b1_ref, wrh_ref, wrl_ref, br_ref, x1_ref, h2_ref, code_ref, wts_ref,
                cnt_ref):
    m = mod_ref[0]
    a = _dot(ya_ref[0, rows, :], wa_ref[...])
    b = _dot(yb_ref[0, rows, :], wb_ref[...])
    merged = (jax.nn.sigmoid(ga_ref[0, rows, :].astype(F32)) * a
              + jax.nn.sigmoid(gb_ref[0, rows, :].astype(F32)) * b)
    o = _dot(merged.astype(BF16), wo_ref[...])
    x1 = _layer_norm(DEEPNORM_ALPHA * x_ref[0, rows, :] + m[2:3] * o, g1_ref[...], b1_ref[...])
    h2 = x1 * (1.0 + m[4:5]) + m[3:4]
    x1_ref[0, rows, :] = x1
    nsub = h2.shape[1] // LANES
    _store_token_tiles(h2_ref.at[0, pl.ds(rows.start * nsub, (rows.stop - rows.start) * nsub)], h2)
    h_hi = h2.astype(BF16)
    h_lo = (h2 - h_hi.astype(F32)).astype(BF16)
    work = (_dot(h_hi, wrh_ref[...]) + _dot(h_hi, wrl_ref[...]) + _dot(h_lo, wrh_ref[...])
            + br_ref[...])
    lane = lax.broadcasted_iota(jnp.int32, work.shape, 1)
    code = jnp.zeros(work.shape, F32)
    vals = []
    for k in range(TOP_K):
        top = work.max(axis=-1, keepdims=True)
        idx = jnp.where(work == top, lane, LANES).min(axis=-1, keepdims=True)
        sel = lane == idx
        work = jnp.where(sel, -jnp.inf, work)
        code = jnp.where(sel, k + 1.0, code)
        vals.append(top)
    den = sum(jnp.exp(v - vals[0]) for v in vals)
    wts = jnp.zeros(work.shape, F32)
    for k in range(TOP_K):
        wts = jnp.where(lane == k, jnp.exp(vals[k] - vals[0]) / den, wts)
    code_ref[0, rows, :] = code
    wts_ref[0, rows, :] = wts
    picked = jnp.where(code > 0.0, 1.0, 0.0)
    cnt_ref[...] += jnp.broadcast_to(picked.sum(axis=0, keepdims=True), cnt_ref.shape)


def _merge(x, ya, yb, ga, gb, mod3, wa, wb, wo, g1, b1, wr, br):
    wr_hi = wr.astype(BF16)
    wr_lo = (wr - wr_hi.astype(F32)).astype(BF16)
    bsz, seq, d = x.shape
    tm = min(TM_MERGE, seq)
    tok = lambda n: pl.BlockSpec((1, tm, n), lambda b, i: (b, i, 0))
    full = lambda a: pl.BlockSpec(a.shape, lambda b, i: (0,) * a.ndim)
    return pl.pallas_call(
        _merge_kernel,
        out_shape=[jax.ShapeDtypeStruct((bsz, seq, d), F32),
                   jax.ShapeDtypeStruct((bsz, seq * (d // LANES), LANES), F32),
                   jax.ShapeDtypeStruct((bsz, seq, LANES), F32),
                   jax.ShapeDtypeStruct((bsz, seq, LANES), F32),
                   jax.ShapeDtypeStruct((8, LANES), F32)],
        grid=(bsz, seq // tm),
        in_specs=[tok(d), tok(A_WIDTH), tok(B_HEADS * B_V_DIM), tok(d), tok(d),
                  pl.BlockSpec((1, 6, d), lambda b, i: (b, 0, 0)),
                  full(wa), full(wb), full(wo), full(g1), full(b1), full(wr_hi), full(wr_lo),
                  full(br)],
        out_specs=[tok(d),
                   pl.BlockSpec((1, tm * (d // LANES), LANES), lambda b, i: (b, i, 0)),
                   tok(LANES), tok(LANES), pl.BlockSpec((8, LANES), lambda b, i: (0, 0))],
        compiler_params=_params(("arbitrary", "arbitrary"), 48),
    )(x, ya, yb, ga, gb, mod3, wa, wb, wo, g1, b1, wr_hi, wr_lo, br)


def _route_kernel(code_ref, cnt_ref, pos_ref, tmap_ref, nv_ref, carry_s, pstart_s, *, tr, tm):
    i = pl.program_id(0)
    lane = lax.broadcasted_iota(jnp.int32, (tr, LANES), 1)
    code = code_ref[...]
    picked = jnp.where(code > 0.0, 1.0, 0.0)
    tile_count = jnp.broadcast_to(picked.sum(axis=0, keepdims=True), (8, LANES))

    @pl.when(i == 0)
    def _():
        lane8 = lax.broadcasted_iota(jnp.int32, (8, LANES), 1)
        padded = jnp.floor((cnt_ref[...] + (tm - 1)) * (1.0 / tm)) * tm
        cum = padded
        shift = 1
        while shift < N_EXPERTS:
            cum = cum + jnp.where(lane8 >= shift, pltpu.roll(cum, shift, 1), 0.0)
            shift *= 2
        pstart_s[...] = cum - padded
        carry_s[...] = jnp.zeros_like(carry_s)
        nt = tmap_ref.shape[0]
        tile_start = (lax.broadcasted_iota(jnp.int32, (nt, LANES), 0) * tm).astype(F32)
        lane_t = lax.broadcasted_iota(jnp.int32, (nt, LANES), 1)
        done = jnp.where((lane_t < N_EXPERTS) & (cum[0:1] <= tile_start), 1.0, 0.0)
        te = jnp.minimum(done.sum(axis=-1, keepdims=True), N_EXPERTS - 1.0)
        tmap_ref[...] = jnp.broadcast_to(te, (nt, LANES)).astype(jnp.int32)
        total = jnp.where(lane8 == N_EXPERTS - 1, cum, 0.0).sum(axis=-1, keepdims=True)
        row8 = lax.broadcasted_iota(jnp.int32, (8, LANES), 0)
        meta = jnp.where(row8 == 0, jnp.broadcast_to(total * (1.0 / tm), (8, LANES)),
                         jnp.where(row8 == 1, cum - padded + cnt_ref[...], cum))
        nv_ref[...] = meta.astype(jnp.int32)

    r = lax.broadcasted_iota(jnp.int32, (tr, tr), 0)
    c = lax.broadcasted_iota(jnp.int32, (tr, tr), 1)
    tri = jnp.where(c < r, 1.0, 0.0).astype(BF16)
    base = _dot(tri, picked.astype(BF16)) + (carry_s[0:1] + pstart_s[0:1])
    carry_s[...] += tile_count
    pos = jnp.zeros((tr, LANES), F32)
    for k in range(TOP_K):
        pk = jnp.where(code == k + 1.0, base, 0.0).sum(axis=-1, keepdims=True)
        pos = jnp.where(lane == k, pk, pos)
    pos_ref[...] = pos.astype(jnp.int32)


def _route(code, cnt, tm):
    n = code.shape[0]
    tr = min(TR_ROUTE, n)
    return pl.pallas_call(
        functools.partial(_route_kernel, tr=tr, tm=tm),
        out_shape=[jax.ShapeDtypeStruct((n, LANES), jnp.int32),
                   jax.ShapeDtypeStruct((MAX_TILES_PAD, LANES), jnp.int32),
                   jax.ShapeDtypeStruct((8, LANES), jnp.int32)],
        grid=(n // tr,),
        in_specs=[pl.BlockSpec((tr, LANES), lambda i: (i, 0)),
                  pl.BlockSpec((8, LANES), lambda i: (0, 0))],
        out_specs=[pl.BlockSpec((tr, LANES), lambda i: (i, 0)),
                   pl.BlockSpec((MAX_TILES_PAD, LANES), lambda i: (0, 0)),
                   pl.BlockSpec((8, LANES), lambda i: (0, 0))],
        scratch_shapes=[pltpu.VMEM((8, LANES), F32)] * 2,
        compiler_params=_params(("arbitrary",)),
    )(code, cnt)


def _dispatch_kernel(pos_ref, fill_ref, h_ref, xs_ref, ztile_s, sem, fill_sem, tail_sem,
                     *, tm, nsub, tm_moe):
    def tile(ref, r):
        return ref.at[pl.ds(pl.multiple_of(r * nsub, nsub), nsub)]

    @pl.when(pl.program_id(0) == 0)
    def _():
        ztile_s[...] = jnp.zeros_like(ztile_s)

        def tail_copy(t):
            rows = tm_moe * nsub
            return pltpu.make_async_copy(
                ztile_s, xs_ref.at[pl.ds(pl.multiple_of(t * rows, rows), rows)], tail_sem)

        sizes = [1 << b for b in reversed(range((tm_moe - 1).bit_length()))]

        def fill_pieces(e, act):
            lo, hi = fill_ref[LANES + e], fill_ref[2 * LANES + e]
            at = lo
            for size in sizes:
                has = ((hi - lo) & size) != 0
                copy = pltpu.make_async_copy(
                    ztile_s.at[pl.ds(0, size * nsub)],
                    xs_ref.at[pl.ds(pl.multiple_of(at * nsub, nsub), size * nsub)], fill_sem)
                pl.when(has)(functools.partial(act, copy))
                at = at + jnp.where(has, size, 0)

        def fill_start(e, c):
            fill_pieces(e, lambda copy: copy.start())
            return c

        def fill_wait(e, c):
            fill_pieces(e, lambda copy: copy.wait())
            return c

        n_tiles = xs_ref.shape[0] // (tm_moe * nsub)

        def tail_start(t, c):
            tail_copy(t).start()
            return c

        def tail_drain(_, c):
            tail_copy(0).wait()
            return c

        lax.fori_loop(0, N_EXPERTS, fill_start, 0)
        lax.fori_loop(fill_ref[0], n_tiles, tail_start, 0)
        lax.fori_loop(0, N_EXPERTS, fill_wait, 0)
        lax.fori_loop(fill_ref[0], n_tiles, tail_drain, 0)

    def issue(j, c):
        for r in range(COMBINE_GROUP):
            row = pl.multiple_of(j * COMBINE_GROUP, COMBINE_GROUP) + r
            for k in range(TOP_K):
                pltpu.make_async_copy(tile(h_ref, row), tile(xs_ref, pos_ref[row * TOP_K + k]),
                                      sem).start(priority=k % 2)
        return c

    lax.fori_loop(0, tm // COMBINE_GROUP, issue, 0)
    for _ in range(TOP_K):
        pltpu.make_async_copy(h_ref, xs_ref.at[pl.ds(0, tm * nsub)], sem).wait()


def _dispatch(pos_flat, fill, h2, cap, nsub, tm_moe):
    n = h2.shape[0] // nsub
    tm = min(TM_ROWS, n)
    return pl.pallas_call(
        functools.partial(_dispatch_kernel, tm=tm, nsub=nsub, tm_moe=tm_moe),
        out_shape=jax.ShapeDtypeStruct((cap * nsub, LANES), F32),
        grid=(n // tm,),
        in_specs=[pl.BlockSpec((tm * TOP_K,), lambda i: (i,), memory_space=pltpu.SMEM),
                  pl.BlockSpec(fill.shape, lambda i: (0,), memory_space=pltpu.SMEM),
                  pl.BlockSpec((tm * nsub, LANES), lambda i: (i, 0))],
        out_specs=pl.BlockSpec(memory_space=pl.ANY),
        scratch_shapes=[pltpu.VMEM((tm_moe * nsub, LANES), F32),
                        pltpu.SemaphoreType.DMA(()), pltpu.SemaphoreType.DMA(()),
                        pltpu.SemaphoreType.DMA(())],
        compiler_params=_params(("arbitrary",)),
    )(pos_flat, fill, h2)


def _moe_kernel(te_ref, nv_ref, x_ref, bg_ref, bu_ref, bd_ref, wg_hbm, wu_hbm, wd_hbm,
                y_ref, wf_s, wg_s, wu_s, wd_s, slot_s, sem):
    i = pl.program_id(0)
    nv = nv_ref[0]
    last = te_ref.shape[0] - 1

    def weight_copies(e, slot):
        return [pltpu.make_async_copy(w.at[e], wf_s.at[slot, j], sem.at[slot])
                for j, w in enumerate((wg_hbm, wu_hbm, wd_hbm))]

    @pl.when(i < nv)
    def _():
        e = te_ref[i]

        @pl.when(i == 0)
        def _():
            slot_s[0] = 0
            for c in weight_copies(e, 0):
                c.start()

        @pl.when((i == 0) | (e != te_ref[jnp.maximum(i - 1, 0)]))
        def _():
            slot = slot_s[0]
            for c in weight_copies(e, slot):
                c.wait()
            nxt = lax.while_loop(lambda j: (j < nv) & (te_ref[jnp.minimum(j, last)] == e),
                                 lambda j: j + 1, i + 1)

            @pl.when(nxt < nv)
            def _():
                for c in weight_copies(te_ref[jnp.minimum(nxt, last)], 1 - slot):
                    c.start()

            wg_s[...] = wf_s[slot, 0].astype(BF16)
            wu_s[...] = wf_s[slot, 1].astype(BF16)
            wd_s[...] = wf_s[slot, 2].astype(BF16)
            slot_s[0] = 1 - slot

        xb = _load_token_tiles(x_ref, wg_s.shape[0] // LANES).astype(BF16)
        g = _dot(xb, wg_s[...]) + bg_ref[0]
        u = _dot(xb, wu_s[...]) + bu_ref[0]
        g = jnp.minimum(g, SWIGLU_LIMIT)
        u = jnp.clip(u, -SWIGLU_LIMIT, SWIGLU_LIMIT)
        h = (u + 1.0) * (g * jax.nn.sigmoid(SWIGLU_ALPHA * g))
        _store_token_tiles(y_ref, _dot(h.astype(BF16), wd_s[...]) + bd_ref[0])

    @pl.when(i >= nv)
    def _():
        y_ref[...] = jnp.zeros_like(y_ref)


def _moe(tile_expert, n_valid, xs, w_gate, b_gate, w_up, b_up, w_down, b_down, tm):
    e, d, f = w_gate.shape
    assert d == f
    nsub = d // LANES
    cap = xs.shape[0] // nsub
    n_tiles = cap // tm

    def tile(i, te, nv):
        return jnp.minimum(i, nv[0] - 1)

    exp3 = lambda i, te, nv: (te[tile(i, te, nv)], 0, 0)
    grid_spec = pltpu.PrefetchScalarGridSpec(
        num_scalar_prefetch=2,
        grid=(n_tiles,),
        in_specs=[pl.BlockSpec((tm * nsub, LANES), lambda i, te, nv: (tile(i, te, nv), 0)),
                  pl.BlockSpec((1, 1, f), exp3), pl.BlockSpec((1, 1, f), exp3),
                  pl.BlockSpec((1, 1, d), exp3),
                  pl.BlockSpec(memory_space=pl.ANY), pl.BlockSpec(memory_space=pl.ANY),
                  pl.BlockSpec(memory_space=pl.ANY)],
        out_specs=pl.BlockSpec((tm * nsub, LANES), lambda i, te, nv: (i, 0)),
        scratch_shapes=[pltpu.VMEM((2, 3, d, f), F32),
                        pltpu.VMEM((d, f), BF16), pltpu.VMEM((d, f), BF16), pltpu.VMEM((f, d), BF16),
                        pltpu.SMEM((1,), jnp.int32), pltpu.SemaphoreType.DMA((2,))],
    )
    return pl.pallas_call(
        _moe_kernel,
        out_shape=jax.ShapeDtypeStruct((cap * nsub, LANES), F32),
        grid_spec=grid_spec,
        compiler_params=_params(("arbitrary",), 56),
    )(tile_expert, n_valid, xs, b_gate.reshape(e, 1, f), b_up.reshape(e, 1, f),
      b_down.reshape(e, 1, d), w_gate, w_up, w_down)


def _combine_kernel(pos_ref, posn_ref, w_ref, x1_ref, mod_ref, g_ref, b_ref, y_ref, o_ref,
                    buf, sem, *, tm):
    i = pl.program_id(0)
    n = pl.num_programs(0)
    nsub = x1_ref.shape[1] // LANES

    group = COMBINE_GROUP

    def issue_rows(p_ref, slot, r0):
        for r in range(group):
            for k in range(TOP_K):
                row = r0 + r
                src = pl.ds(pl.multiple_of(p_ref[row * TOP_K + k] * nsub, nsub), nsub)
                dst = pl.ds(pl.multiple_of(row * nsub, nsub), nsub)
                pltpu.make_async_copy(y_ref.at[src], buf.at[slot, k, dst],
                                      sem.at[slot]).start(priority=k % 2)

    def reduce_rows(slot, r0):
        rows = pl.ds(r0, group)
        w = w_ref[rows, :]
        f = jnp.zeros((group, nsub * LANES), F32)
        for k in range(TOP_K):
            yk = jnp.concatenate(
                [buf[slot, k, pl.ds(r0 * nsub + s, group, stride=nsub), :] for s in range(nsub)],
                axis=1)
            f = f + w[:, k:k + 1] * yk
        o_ref[rows, :] = f

    @pl.when(i == 0)
    def _():
        def body(j, c):
            issue_rows(pos_ref, 0, pl.multiple_of(j * group, group))
            return c
        lax.fori_loop(0, tm // group, body, 0)

    slot = i % 2
    for k in range(TOP_K):
        pltpu.make_async_copy(y_ref.at[pl.ds(0, tm * nsub)], buf.at[slot, k], sem.at[slot]).wait()

    @pl.when(i + 1 < n)
    def _():
        def body(j, c):
            r0 = pl.multiple_of(j * group, group)
            issue_rows(posn_ref, 1 - slot, r0)
            reduce_rows(slot, r0)
            return c
        lax.fori_loop(0, tm // group, body, 0)

    @pl.when(i + 1 == n)
    def _():
        def body(j, c):
            reduce_rows(slot, pl.multiple_of(j * group, group))
            return c
        lax.fori_loop(0, tm // group, body, 0)

    o_ref[...] = _layer_norm(DEEPNORM_ALPHA * x1_ref[...] + mod_ref[0, 5:6, :] * o_ref[...],
                             g_ref[...], b_ref[...])


def _combine(pos_flat, w, x1, mod3, g2, b2, y, seq):
    n, d = x1.shape
    tm = min(TM_ROWS, seq)
    nsteps = n // tm
    per_seq = seq // tm
    return pl.pallas_call(
        functools.partial(_combine_kernel, tm=tm),
        out_shape=jax.ShapeDtypeStruct((n, d), F32),
        grid=(nsteps,),
        in_specs=[pl.BlockSpec((tm * TOP_K,), lambda i: (i,), memory_space=pltpu.SMEM),
                  pl.BlockSpec((tm * TOP_K,), lambda i: (jnp.minimum(i + 1, nsteps - 1),),
                               memory_space=pltpu.SMEM),
                  pl.BlockSpec((tm, LANES), lambda i: (i, 0)),
                  pl.BlockSpec((tm, d), lambda i: (i, 0)),
                  pl.BlockSpec((1, 6, d), lambda i: (i // per_seq, 0, 0)),
                  pl.BlockSpec(g2.shape, lambda i: (0, 0)),
                  pl.BlockSpec(b2.shape, lambda i: (0, 0)),
                  pl.BlockSpec(memory_space=pl.ANY)],
        out_specs=pl.BlockSpec((tm, d), lambda i: (i, 0)),
        scratch_shapes=[pltpu.VMEM((2, TOP_K, tm * (d // LANES), LANES), F32),
                        pltpu.SemaphoreType.DMA((2,))],
        compiler_params=_params(("arbitrary",), 32),
    )(pos_flat, pos_flat, w, x1, mod3, g2, b2, y)


def _prep_w_uq(w_uq):
    r = w_uq.shape[0]
    w = w_uq.reshape(r, B_HEADS, B_QK_DIM)
    half = B_ROPE_DIM // 2
    rot = jnp.concatenate([-w[:, :, B_NOPE_DIM + half:], w[:, :, B_NOPE_DIM:B_NOPE_DIM + half]],
                          axis=-1).reshape(r, B_HEADS * B_ROPE_DIM)
    w = jnp.pad(w, ((0, 0), (0, 0), (0, LANES - B_QK_DIM))).reshape(r, B_HEADS * LANES)
    return jnp.concatenate([w, rot], axis=1).astype(BF16)


def _prep_w_ukv(w_ukv):
    r = w_ukv.shape[0]
    w = w_ukv.reshape(r, B_HEADS, B_NOPE_DIM + B_V_DIM)
    wk = jnp.pad(w[:, :, :B_NOPE_DIM], ((0, 0), (0, 0), (0, LANES - B_NOPE_DIM)))
    wv = w[:, :, B_NOPE_DIM:].reshape(r, B_HEADS // 2, 2 * B_V_DIM)
    wv = jnp.pad(wv, ((0, 0), (0, 0), (0, V_PAIR_LANES - 2 * B_V_DIM)))
    return (wk.reshape(r, B_HEADS * LANES).astype(BF16),
            wv.reshape(r, B_HEADS // 2 * V_PAIR_LANES).astype(BF16))


def _rope_tables(positions):
    half = B_ROPE_DIM // 2
    freqs = ROPE_THETA ** (-jnp.arange(half, dtype=F32) / half)
    ang = positions.astype(F32).reshape(1, -1) * freqs[:, None]
    cos, sin = jnp.cos(ang), jnp.sin(ang)
    z = jnp.zeros((B_ROPE_DIM, ang.shape[1]), F32)
    return jnp.concatenate([sin, sin, z, cos, cos, z], 0)


def kernel(x, c, positions, w_ada, b_ada, w_in, rms_q, w_uq, rms_kv, w_ukv, rel_bias,
           w_branch_a, w_branch_b, w_out, ln1_g, ln1_b, w_router, b_router,
           w_gate, b_gate, w_up, b_up, w_down, b_down, ln2_g, ln2_b):
    bsz, seq, d = x.shape
    n_tok = bsz * seq
    assert w_ada.shape[0] == DEPTH == 1
    row = lambda v: v.reshape(1, -1)

    mod3 = _ada(c, w_ada[0], b_ada[0]).reshape(bsz, 6, d)

    qk, va, cq, ckv, kr, ga, gb = _inproj(x, mod3, jnp.swapaxes(w_in, 1, 2))
    ya = _attn_a(qk, va, _bias_table(rel_bias[0], min(TQ_A, seq)))
    wk, wv = _prep_w_ukv(w_ukv[0])
    qm, km, vm = _mla_proj(cq, ckv, kr, _rope_tables(positions), row(rms_q[0]), row(rms_kv[0]),
                           _prep_w_uq(w_uq[0]), wk, wv)
    yb = _mla_attn(qm, km, vm)

    wr = jnp.pad(w_router[0], ((0, 0), (0, LANES - N_EXPERTS)))
    br = jnp.pad(b_router[0], (0, LANES - N_EXPERTS), constant_values=NEG_INF)
    x1, h2, code, wts, cnt = _merge(x, ya, yb, ga, gb, mod3,
                                    w_branch_a[0].astype(BF16), w_branch_b[0].astype(BF16),
                                    w_out[0].astype(BF16), row(ln1_g[0]), row(ln1_b[0]), wr, row(br))
    wts = wts.reshape(n_tok, LANES)

    tm = TM_MOE
    n_tiles = -(-(n_tok * TOP_K) // tm) + N_EXPERTS
    assert n_tiles <= MAX_TILES_PAD
    pos, tmap, meta = _route(code.reshape(n_tok, LANES), cnt, tm)
    pos_flat = pos[:, :TOP_K].reshape(-1)
    fill = meta[:3].reshape(-1)
    nsub = d // LANES
    xs = _dispatch(pos_flat, fill, h2.reshape(n_tok * nsub, LANES), n_tiles * tm, nsub, tm)
    y = _moe(tmap[:n_tiles, 0], meta[0, :1], xs, w_gate[0], b_gate[0], w_up[0], b_up[0],
             w_down[0], b_down[0], tm)
    out = _combine(pos_flat, wts, x1.reshape(n_tok, d), mod3, row(ln2_g[0]), row(ln2_b[0]), y, seq)
    return out.reshape(bsz, seq, d)
```

```python
import functools

import jax
import jax.numpy as jnp
from jax import lax
from jax.experimental import pallas as pl
from jax.experimental.pallas import tpu as pltpu

F32 = jnp.float32
BF16 = jnp.bfloat16

CHUNK = 64
A_HEADS = 8
A_HEAD_DIM = 64
A_LEFT_CHUNKS = 8
A_MAX_REL = 128
A_WIDTH = A_HEADS * A_HEAD_DIM
B_HEADS = 8
B_NOPE_DIM = 64
B_ROPE_DIM = 32
B_V_DIM = 64
B_Q_LORA = 384
B_KV_LORA = 256
B_QK_DIM = B_NOPE_DIM + B_ROPE_DIM
ROPE_THETA = 10000.0
N_EXPERTS = 32
TOP_K = 4
SWIGLU_LIMIT = 7.0
SWIGLU_ALPHA = 1.702
DEPTH = 1
DEEPNORM_ALPHA = (2.0 * DEPTH) ** 0.25
LN_EPS = 1e-5
RMS_EPS = 1e-6
NEG_INF = -1e30
LOG2_E = 1.4426950408889634

LANES = 128
MXU_COLS = 256
V_PAIR_LANES = MXU_COLS

TM_PROJ = 512
TM_MLA_PROJ = 1024
TQ_A = 256
A_QBLOCKS = 8
TQ_B = 512
TM_MERGE = 1024
MERGE_SPLIT = 2
TR_ROUTE = 1024
TM_MOE = 512
TM_ROWS = 256
COMBINE_GROUP = 8
MAX_TILES_PAD = 512


def _params(sem, vmem_mb=None):
    return pltpu.CompilerParams(
        dimension_semantics=sem,
        vmem_limit_bytes=None if vmem_mb is None else vmem_mb << 20)


def _dot(a, b):
    return jnp.dot(a, b, preferred_element_type=F32)


def _dot_nt(a, b):
    return lax.dot_general(a, b, (((1,), (1,)), ((), ())), preferred_element_type=F32)


def _store_token_tiles(ref, x):
    rows, d = x.shape
    nsub = d // LANES
    for s in range(nsub):
        ref[pl.ds(s, rows, stride=nsub), :] = x[:, s * LANES:(s + 1) * LANES]


def _load_token_tiles(ref, nsub):
    rows = ref.shape[0] // nsub
    return jnp.concatenate([ref[pl.ds(s, rows, stride=nsub), :] for s in range(nsub)], axis=1)


def _layer_norm(x, g, b):
    mu = jnp.mean(x, axis=-1, keepdims=True)
    xc = x - mu
    var = jnp.mean(xc * xc, axis=-1, keepdims=True)
    return xc * lax.rsqrt(var + LN_EPS) * g + b


def _ada_kernel(c_ref, w_ref, b_ref, o_ref):
    c = c_ref[...]
    sc = c * jax.nn.sigmoid(c)
    w = w_ref[...]
    s_hi, w_hi = sc.astype(BF16), w.astype(BF16)
    s_lo = (sc - s_hi.astype(F32)).astype(BF16)
    w_lo = (w - w_hi.astype(F32)).astype(BF16)
    o_ref[...] = _dot(s_hi, w_hi) + _dot(s_hi, w_lo) + _dot(s_lo, w_hi) + b_ref[...]


def _ada(c, w_ada, b_ada):
    bsz, d = c.shape
    m = w_ada.shape[1]
    tn = 1024
    return pl.pallas_call(
        _ada_kernel,
        out_shape=jax.ShapeDtypeStruct((bsz, m), F32),
        grid=(m // tn,),
        in_specs=[pl.BlockSpec((bsz, d), lambda j: (0, 0)),
                  pl.BlockSpec((d, tn), lambda j: (0, j)),
                  pl.BlockSpec((1, tn), lambda j: (0, j))],
        out_specs=pl.BlockSpec((bsz, tn), lambda j: (0, j)),
        compiler_params=_params(("arbitrary",)),
    )(c, w_ada, b_ada.reshape(1, m))


def _inproj_kernel(x_ref, mod_ref, win_hbm, qk_ref, va_ref, cq_ref, ckv_ref, kr_ref, ga_ref, gb_ref,
                   wf_s, wqk_ref, wv_ref, wcq_ref, wckv_ref, wkr_ref, wga_ref, wgb_ref, sem):
    @pl.when((pl.program_id(0) == 0) & (pl.program_id(1) == 0))
    def _():
        copy = pltpu.make_async_copy(win_hbm.at[0], wf_s, sem)
        copy.start()
        copy.wait()
        d = wf_s.shape[1]
        chunk = 2 * LANES
        off = 0
        for dst, n in ((wqk_ref, 2 * A_WIDTH), (wv_ref, A_WIDTH), (wcq_ref, B_Q_LORA),
                       (wckv_ref, B_KV_LORA), (None, B_ROPE_DIM), (wga_ref, d), (wgb_ref, d)):
            if dst is None:
                blk = wf_s[off:off + LANES, :].T
                lane = lax.broadcasted_iota(jnp.int32, blk.shape, 1)
                blk = pltpu.roll(jnp.where(lane < n, blk, 0.0), B_NOPE_DIM, 1)
                wkr_ref[...] = blk.astype(BF16)
            else:
                for c0 in range(0, n, chunk):
                    cn = min(chunk, n - c0)
                    blk = wf_s[off + c0:off + c0 + cn, :].T
                    if dst is wqk_ref and c0 < A_WIDTH:
                        blk = blk * (A_HEAD_DIM ** -0.5 * LOG2_E)
                    dst[:, c0:c0 + cn] = blk.astype(BF16)
            off += n

    m = mod_ref[0]
    h = (x_ref[0] * (1.0 + m[1:2]) + m[0:1]).astype(BF16)
    for w_ref, ref in ((wqk_ref, qk_ref), (wcq_ref, cq_ref), (wckv_ref, ckv_ref),
                       (wkr_ref, kr_ref), (wga_ref, ga_ref), (wgb_ref, gb_ref)):
        ref[0] = _dot(h, w_ref[...]).astype(ref.dtype)
    v = _dot(h, wv_ref[...]).astype(BF16)
    ones_blk = jnp.ones((v.shape[0], LANES), BF16)
    parts = []
    for hp in range(A_WIDTH // LANES):
        parts += [v[:, hp * LANES:(hp + 1) * LANES], ones_blk]
    va_ref[0] = jnp.concatenate(parts, axis=1)


def _inproj(x, mod3, w_in):
    bsz, seq, d = x.shape
    tm = min(TM_PROJ, seq)
    widths = (2 * A_WIDTH, A_WIDTH // LANES * V_PAIR_LANES, B_Q_LORA, B_KV_LORA, LANES, d, d)
    dtypes = (BF16, BF16, BF16, BF16, F32, BF16, BF16)
    w_widths = (2 * A_WIDTH, A_WIDTH, B_Q_LORA, B_KV_LORA, LANES, d, d)
    assert w_in.shape[0] == 1 and w_in.shape[1] == sum(w_widths) - LANES + B_ROPE_DIM
    return pl.pallas_call(
        _inproj_kernel,
        out_shape=[jax.ShapeDtypeStruct((bsz, seq, n), dt) for n, dt in zip(widths, dtypes)],
        grid=(bsz, seq // tm),
        in_specs=[pl.BlockSpec((1, tm, d), lambda b, i: (b, i, 0)),
                  pl.BlockSpec((1, 6, d), lambda b, i: (b, 0, 0)),
                  pl.BlockSpec(memory_space=pl.ANY)],
        out_specs=[pl.BlockSpec((1, tm, n), lambda b, i: (b, i, 0)) for n in widths],
        scratch_shapes=[pltpu.VMEM(w_in.shape[1:], F32)]
                       + [pltpu.VMEM((d, n), BF16) for n in w_widths]
                       + [pltpu.SemaphoreType.DMA(())],
        compiler_params=_params(("arbitrary", "arbitrary"), 56),
    )(x, mod3, w_in)


def _attn_a_kernel(q_ref, k_ref, v_ref, bias_ref, o_ref, *, tq, nd, nblk, steps):
    band = nd * tq
    lane = lax.broadcasted_iota(jnp.int32, (1, LANES), 1)

    def shifted(base, missing):
        cut = missing * tq
        if cut == 0:
            return base
        return jnp.concatenate([base[:, cut:], jnp.full((tq, cut), NEG_INF, F32)], axis=1)

    def bias_for(hh, i):
        base = bias_ref[hh]
        if isinstance(i, int):
            return shifted(base, max(nd - 1 - i, 0))
        out = base
        for first in range(nd - 1):
            out = jnp.where(i == first, shifted(base, nd - 1 - first), out)
        return out

    for sub in range(nblk):
        if steps == 1:
            i = sub
            rows = slice(max(i - (nd - 1), 0) * tq, max(i - (nd - 1), 0) * tq + band)
        else:
            i = pl.program_id(2) * nblk + sub
            rows = pl.ds(pl.multiple_of(jnp.maximum(i - (nd - 1), 0) * tq, tq), band)
        q = q_ref[0, sub * tq:(sub + 1) * tq, :]
        k = k_ref[0, rows, :]
        v = v_ref[0, rows, :]
        zero = jnp.zeros_like(k)
        kk = jnp.concatenate([jnp.where(lane < A_HEAD_DIM, k, zero),
                              jnp.where(lane >= A_HEAD_DIM, k, zero)], axis=0)
        s = _dot_nt(q, kk)
        ps = []
        for hh in range(2):
            sh = s[:, hh * band:(hh + 1) * band] + bias_for(hh, i)
            m = sh.max(axis=-1, keepdims=True)
            ps.append(jnp.exp2(sh - m).astype(BF16))
        acc = _dot(jnp.concatenate(ps, axis=0), v)
        out = acc[:, :LANES] / acc[:, LANES:]
        o_ref[0, sub * tq:(sub + 1) * tq, :] = jnp.where(
            lane < A_HEAD_DIM, out[:tq], out[tq:]).astype(o_ref.dtype)


def _attn_a(qk, va, bias_tab):
    bsz, seq, _ = qk.shape
    tq = min(TQ_A, seq)
    nd = bias_tab.shape[2] // tq
    assert seq >= nd * tq
    ncb = A_WIDTH // LANES
    nblk = min(A_QBLOCKS, seq // tq)
    steps = seq // (nblk * tq)
    return pl.pallas_call(
        functools.partial(_attn_a_kernel, tq=tq, nd=nd, nblk=nblk, steps=steps),
        out_shape=jax.ShapeDtypeStruct((bsz, seq, A_WIDTH), BF16),
        grid=(ncb, bsz, steps),
        in_specs=[pl.BlockSpec((1, nblk * tq, LANES), lambda hp, b, i: (b, i, hp)),
                  pl.BlockSpec((1, seq, LANES), lambda hp, b, i: (b, 0, ncb + hp)),
                  pl.BlockSpec((1, seq, V_PAIR_LANES), lambda hp, b, i: (b, 0, hp)),
                  pl.BlockSpec((2, tq, nd * tq), lambda hp, b, i: (hp, 0, 0))],
        out_specs=pl.BlockSpec((1, nblk * tq, LANES), lambda hp, b, i: (b, i, hp)),
        compiler_params=_params(("arbitrary",) * 3, 40),
    )(qk, qk, va, bias_tab)


def _bias_table(rel_bias, tq):
    nd = (A_LEFT_CHUNKS * CHUNK + tq - 1) // tq + 1
    width = nd * tq
    period = width + tq
    u = jnp.arange(period)
    dist = jnp.where(u < width, (nd - 1) * tq - u, (nd - 1) * tq + period - u)
    vec = rel_bias[:, jnp.clip(dist, -A_MAX_REL, A_MAX_REL) + A_MAX_REL].astype(F32)
    toep = jnp.tile(vec, (1, tq))[:, :tq * (period - 1)].reshape(-1, tq, period - 1)[:, :, :width]
    qi = jnp.arange(tq)[:, None]
    x = jnp.arange(width)[None, :]
    dchunk = qi // CHUNK - x // CHUNK + (nd - 1) * (tq // CHUNK)
    valid = (dchunk >= 0) & (dchunk <= A_LEFT_CHUNKS)
    return jnp.where(valid[None], toep * LOG2_E, NEG_INF)


def _mla_proj_kernel(cq_ref, ckv_ref, kr_ref, tab_ref, rq_ref, rkv_ref,
                     wq_ref, wk_ref, wv_ref, q_out, k_out, v_out):
    def rms(x, g):
        xf = x.astype(F32)
        ms = jnp.mean(xf * xf, axis=-1, keepdims=True)
        return (xf * lax.rsqrt(ms + RMS_EPS) * g).astype(BF16)

    cqn = rms(cq_ref[0], rq_ref[...])
    ckvn = rms(ckv_ref[0], rkv_ref[...])
    q = _dot(cqn, wq_ref[...])
    kn = _dot(ckvn, wk_ref[...])
    vlane = lax.broadcasted_iota(jnp.int32, (1, v_out.shape[-1]), 1)
    ones_lane = jnp.where(vlane % V_PAIR_LANES >= 2 * B_V_DIM, 1.0, 0.0)
    v_out[0] = (_dot(ckvn, wv_ref[...]) + ones_lane).astype(v_out.dtype)
    tab = tab_ref[...].T
    tlane = lax.broadcasted_iota(jnp.int32, tab.shape, 1)
    rope_end = B_NOPE_DIM + B_ROPE_DIM
    in_rope = (tlane >= B_NOPE_DIM) & (tlane < rope_end)
    cos_t = jnp.where(tlane < B_NOPE_DIM, 1.0, jnp.where(in_rope, tab, 0.0))
    sin_t = jnp.where(in_rope, pltpu.roll(tab, B_NOPE_DIM, 1), 0.0)
    first = tlane < B_NOPE_DIM + B_ROPE_DIM // 2
    sin_a = jnp.where(first, -sin_t, 0.0)
    sin_b = jnp.where(first, 0.0, sin_t)

    def rope(x, tables):
        c, sa, sb = tables
        return (x * c + pltpu.roll(x, LANES - B_ROPE_DIM // 2, 1) * sa
                + pltpu.roll(x, B_ROPE_DIM // 2, 1) * sb)

    kpe = rope(kr_ref[0], (cos_t, sin_a, sin_b))
    scale = B_QK_DIM ** -0.5 * LOG2_E
    cos_q, sin_q = cos_t * scale, sin_t * scale
    nq = B_HEADS * LANES
    per_block = LANES // B_ROPE_DIM
    for h in range(B_HEADS):
        sl = slice(h * LANES, (h + 1) * LANES)
        blk = nq + (h // per_block) * LANES
        partner = q[:, blk:blk + LANES]
        shift = (B_NOPE_DIM - B_ROPE_DIM * (h % per_block)) % LANES
        if shift:
            partner = pltpu.roll(partner, shift, 1)
        q_out[0, :, sl] = (q[:, sl] * cos_q + partner * sin_q).astype(q_out.dtype)
        k_out[0, :, sl] = (kn[:, sl] + kpe).astype(k_out.dtype)


def _mla_proj(cq, ckv, kr, rope_tab, rms_q, rms_kv, wq, wk, wv):
    bsz, seq, _ = cq.shape
    tm = min(TM_MLA_PROJ, seq)
    tok = lambda n: pl.BlockSpec((1, tm, n), lambda b, i: (b, i, 0))
    full = lambda a: pl.BlockSpec(a.shape, lambda b, i: (0,) * a.ndim)
    return pl.pallas_call(
        _mla_proj_kernel,
        out_shape=[jax.ShapeDtypeStruct((bsz, seq, B_HEADS * LANES), BF16),
                   jax.ShapeDtypeStruct((bsz, seq, B_HEADS * LANES), BF16),
                   jax.ShapeDtypeStruct((bsz, seq, wv.shape[1]), BF16)],
        grid=(bsz, seq // tm),
        in_specs=[tok(B_Q_LORA), tok(B_KV_LORA), tok(LANES),
                  pl.BlockSpec((LANES, tm), lambda b, i: (0, b * (seq // tm) + i)),
                  full(rms_q), full(rms_kv), full(wq), full(wk), full(wv)],
        out_specs=[tok(B_HEADS * LANES), tok(B_HEADS * LANES), tok(wv.shape[1])],
        compiler_params=_params(("arbitrary", "arbitrary"), 32),
    )(cq, ckv, kr, rope_tab, rms_q, rms_kv, wq, wk, wv)


def _mla_attn_kernel(q_ref, k_ref, v_ref, o_ref, s_a, s_b, m_s, acc_s, *, t):
    lane = lax.broadcasted_iota(jnp.int32, (1, LANES), 1)
    nq = q_ref.shape[1] // t
    pairs = [(qi, kj) for qi in range(nq) for kj in range(qi + 1)]
    bufs = (s_a, s_b)

    def produce(qi, kj, s_ref):
        for hh in range(2):
            q = q_ref[0, qi * t:(qi + 1) * t, hh * LANES:(hh + 1) * LANES]
            k = k_ref[0, kj * t:(kj + 1) * t, hh * LANES:(hh + 1) * LANES]
            s_ref[hh] = _dot_nt(q, k)

    def consume(qi, kj, s_ref):
        v = v_ref[0, kj * t:(kj + 1) * t, :]
        outs = []
        for hh in range(2):
            s = s_ref[hh]
            if kj == qi:
                row_chunk = lax.broadcasted_iota(jnp.int32, (t, t), 0) // CHUNK
                col_chunk = lax.broadcasted_iota(jnp.int32, (t, t), 1) // CHUNK
                s = jnp.where(col_chunk <= row_chunk, s, NEG_INF)
            row_max = jnp.broadcast_to(s.max(axis=-1, keepdims=True), (t, LANES))
            if kj == 0:
                m_new = row_max
                acc = _dot(jnp.exp2(s - jnp.tile(m_new, (1, t // LANES))).astype(BF16), v)
            else:
                m_old = m_s[hh]
                m_new = jnp.maximum(m_old, row_max)
                p = jnp.exp2(s - jnp.tile(m_new, (1, t // LANES))).astype(BF16)
                alpha = jnp.exp2(m_old - m_new)
                acc = jnp.tile(alpha, (1, V_PAIR_LANES // LANES)) * acc_s[hh] + _dot(p, v)
            if kj == qi:
                outs.append(acc[:, :LANES] / acc[:, LANES:])
            else:
                acc_s[hh] = acc
                m_s[hh] = m_new
        if kj == qi:
            o_ref[0, qi * t:(qi + 1) * t, :] = jnp.where(
                lane < B_V_DIM, outs[0], outs[1]).astype(o_ref.dtype)

    produce(*pairs[0], bufs[0])
    for n, pair in enumerate(pairs):
        if n + 1 < len(pairs):
            produce(*pairs[n + 1], bufs[(n + 1) % 2])
        consume(*pair, bufs[n % 2])


def _mla_attn(qm, km, vm):
    bsz, seq, _ = qm.shape
    t = min(TQ_B, seq)
    return pl.pallas_call(
        functools.partial(_mla_attn_kernel, t=t),
        out_shape=jax.ShapeDtypeStruct((bsz, seq, B_HEADS * B_V_DIM), BF16),
        grid=(bsz, B_HEADS // 2),
        in_specs=[pl.BlockSpec((1, seq, 2 * LANES), lambda b, hp: (b, 0, hp)),
                  pl.BlockSpec((1, seq, 2 * LANES), lambda b, hp: (b, 0, hp)),
                  pl.BlockSpec((1, seq, V_PAIR_LANES), lambda b, hp: (b, 0, hp))],
        out_specs=pl.BlockSpec((1, seq, LANES), lambda b, hp: (b, 0, hp)),
        scratch_shapes=[pltpu.VMEM((2, t, t), F32), pltpu.VMEM((2, t, t), F32),
                        pltpu.VMEM((2, t, LANES), F32),
                        pltpu.VMEM((2, t, V_PAIR_LANES), F32)],
        compiler_params=_params(("arbitrary",) * 2, 40),
    )(qm, km, vm)


def _merge_kernel(x_ref, ya_ref, yb_ref, ga_ref, gb_ref, mod_ref, wa_ref, wb_ref, wo_ref,
                  g1_ref, b1_ref, wrh_ref, wrl_ref, br_ref, x1_ref, h2_ref, code_ref, wts_ref,
                  cnt_ref):
    @pl.when((pl.program_id(0) == 0) & (pl.program_id(1) == 0))
    def _():
        cnt_ref[...] = jnp.zeros_like(cnt_ref)

    tm = x_ref.shape[1]
    half = tm // MERGE_SPLIT
    for part in range(MERGE_SPLIT):
        _merge_rows(slice(part * half, (part + 1) * half), x_ref, ya_ref, yb_ref, ga_ref, gb_ref,
                    mod_ref, wa_ref, wb_ref, wo_ref, g1_ref, b1_ref, wrh_ref, wrl_ref, br_ref,
                    x1_ref, h2_ref, code_ref, wts_ref, cnt_ref)


def _merge_rows(rows, x_ref, ya_ref, yb_ref, ga_ref, gb_ref, mod_ref, wa_ref, wb_ref, wo_ref,
                g1_ref, b1_ref, wrh_ref, wrl_ref, br_ref, x1_ref, h2_ref, code_ref, wts_ref,
                cnt_ref):
    m = mod_ref[0]
    a = _dot(ya_ref[0, rows, :], wa_ref[...])
    b = _dot(yb_ref[0, rows, :], wb_ref[...])
    merged = (jax.nn.sigmoid(ga_ref[0, rows, :].astype(F32)) * a
              + jax.nn.sigmoid(gb_ref[0, rows, :].astype(F32)) * b)
    o = _dot(merged.astype(BF16), wo_ref[...])
    x1 = _layer_norm(DEEPNORM_ALPHA * x_ref[0, rows, :] + m[2:3] * o, g1_ref[...], b1_ref[...])
    h2 = x1 * (1.0 + m[4:5]) + m[3:4]
    x1_ref[0, rows, :] = x1
    nsub = h2.shape[1] // LANES
    _store_token_tiles(h2_ref.at[0, pl.ds(rows.start * nsub, (rows.stop - rows.start) * nsub)], h2)
    h_hi = h2.astype(BF16)
    h_lo = (h2 - h_hi.astype(F32)).astype(BF16)
    work = (_dot(h_hi, wrh_ref[...]) + _dot(h_hi, wrl_ref[...]) + _dot(h_lo, wrh_ref[...])
            + br_ref[...])
    lane = lax.broadcasted_iota(jnp.int32, work.shape, 1)
    code = jnp.zeros(work.shape, F32)
    vals = []
    for k in range(TOP_K):
        top = work.max(axis=-1, keepdims=True)
        idx = jnp.where(work == top, lane, LANES).min(axis=-1, keepdims=True)
        sel = lane == idx
        work = jnp.where(sel, -jnp.inf, work)
        code = jnp.where(sel, k + 1.0, code)
        vals.append(top)
    den = sum(jnp.exp(v - vals[0]) for v in vals)
    wts = jnp.zeros(work.shape, F32)
    for k in range(TOP_K):
        wts = jnp.where(lane == k, jnp.exp(vals[k] - vals[0]) / den, wts)
    code_ref[0, rows, :] = code
    wts_ref[0, rows, :] = wts
    picked = jnp.where(code > 0.0, 1.0, 0.0)
    cnt_ref[...] += jnp.broadcast_to(picked.sum(axis=0, keepdims=True), cnt_ref.shape)


def _merge(x, ya, yb, ga, gb, mod3, wa, wb, wo, g1, b1, wr, br):
    wr_hi = wr.astype(BF16)
    wr_lo = (wr - wr_hi.astype(F32)).astype(BF16)
    bsz, seq, d = x.shape
    tm = min(TM_MERGE, seq)
    tok = lambda n: pl.BlockSpec((1, tm, n), lambda b, i: (b, i, 0))
    full = lambda a: pl.BlockSpec(a.shape, lambda b, i: (0,) * a.ndim)
    return pl.pallas_call(
        _merge_kernel,
        out_shape=[jax.ShapeDtypeStruct((bsz, seq, d), F32),
                   jax.ShapeDtypeStruct((bsz, seq * (d // LANES), LANES), F32),
                   jax.ShapeDtypeStruct((bsz, seq, LANES), F32),
                   jax.ShapeDtypeStruct((bsz, seq, LANES), F32),
                   jax.ShapeDtypeStruct((8, LANES), F32)],
        grid=(bsz, seq // tm),
        in_specs=[tok(d), tok(A_WIDTH), tok(B_HEADS * B_V_DIM), tok(d), tok(d),
                  pl.BlockSpec((1, 6, d), lambda b, i: (b, 0, 0)),
                  full(wa), full(wb), full(wo), full(g1), full(b1), full(wr_hi), full(wr_lo),
                  full(br)],
        out_specs=[tok(d),
                   pl.BlockSpec((1, tm * (d // LANES), LANES), lambda b, i: (b, i, 0)),
                   tok(LANES), tok(LANES), pl.BlockSpec((8, LANES), lambda b, i: (0, 0))],
        compiler_params=_params(("arbitrary", "arbitrary"), 48),
    )(x, ya, yb, ga, gb, mod3, wa, wb, wo, g1, b1, wr_hi, wr_lo, br)


def _route_kernel(code_ref, cnt_ref, pos_ref, tmap_ref, nv_ref, carry_s, pstart_s, *, tr, tm):
    i = pl.program_id(0)
    lane = lax.broadcasted_iota(jnp.int32, (tr, LANES), 1)
    code = code_ref[...]
    picked = jnp.where(code > 0.0, 1.0, 0.0)
    tile_count = jnp.broadcast_to(picked.sum(axis=0, keepdims=True), (8, LANES))

    @pl.when(i == 0)
    def _():
        lane8 = lax.broadcasted_iota(jnp.int32, (8, LANES), 1)
        padded = jnp.floor((cnt_ref[...] + (tm - 1)) * (1.0 / tm)) * tm
        cum = padded
        shift = 1
        while shift < N_EXPERTS:
            cum = cum + jnp.where(lane8 >= shift, pltpu.roll(cum, shift, 1), 0.0)
            shift *= 2
        pstart_s[...] = cum - padded
        carry_s[...] = jnp.zeros_like(carry_s)
        nt = tmap_ref.shape[0]
        tile_start = (lax.broadcasted_iota(jnp.int32, (nt, LANES), 0) * tm).astype(F32)
        lane_t = lax.broadcasted_iota(jnp.int32, (nt, LANES), 1)
        done = jnp.where((lane_t < N_EXPERTS) & (cum[0:1] <= tile_start), 1.0, 0.0)
        te = jnp.minimum(done.sum(axis=-1, keepdims=True), N_EXPERTS - 1.0)
        tmap_ref[...] = jnp.broadcast_to(te, (nt, LANES)).astype(jnp.int32)
        total = jnp.where(lane8 == N_EXPERTS - 1, cum, 0.0).sum(axis=-1, keepdims=True)
        row8 = lax.broadcasted_iota(jnp.int32, (8, LANES), 0)
        meta = jnp.where(row8 == 0, jnp.broadcast_to(total * (1.0 / tm), (8, LANES)),
                         jnp.where(row8 == 1, cum - padded + cnt_ref[...], cum))
        nv_ref[...] = meta.astype(jnp.int32)

    r = lax.broadcasted_iota(jnp.int32, (tr, tr), 0)
    c = lax.broadcasted_iota(jnp.int32, (tr, tr), 1)
    tri = jnp.where(c < r, 1.0, 0.0).astype(BF16)
    base = _dot(tri, picked.astype(BF16)) + (carry_s[0:1] + pstart_s[0:1])
    carry_s[...] += tile_count
    pos = jnp.zeros((tr, LANES), F32)
    for k in range(TOP_K):
        pk = jnp.where(code == k + 1.0, base, 0.0).sum(axis=-1, keepdims=True)
        pos = jnp.where(lane == k, pk, pos)
    pos_ref[...] = pos.astype(jnp.int32)


def _route(code, cnt, tm):
    n = code.shape[0]
    tr = min(TR_ROUTE, n)
    return pl.pallas_call(
        functools.partial(_route_kernel, tr=tr, tm=tm),
        out_shape=[jax.ShapeDtypeStruct((n, LANES), jnp.int32),
                   jax.ShapeDtypeStruct((MAX_TILES_PAD, LANES), jnp.int32),
                   jax.ShapeDtypeStruct((8, LANES), jnp.int32)],
        grid=(n // tr,),
        in_specs=[pl.BlockSpec((tr, LANES), lambda i: (i, 0)),
                  pl.BlockSpec((8, LANES), lambda i: (0, 0))],
        out_specs=[pl.BlockSpec((tr, LANES), lambda i: (i, 0)),
                   pl.BlockSpec((MAX_TILES_PAD, LANES), lambda i: (0, 0)),
                   pl.BlockSpec((8, LANES), lambda i: (0, 0))],
        scratch_shapes=[pltpu.VMEM((8, LANES), F32)] * 2,
        compiler_params=_params(("arbitrary",)),
    )(code, cnt)


def _dispatch_kernel(pos_ref, fill_ref, h_ref, xs_ref, ztile_s, sem, fill_sem, tail_sem,
                     *, tm, nsub, tm_moe):
    def tile(ref, r):
        return ref.at[pl.ds(pl.multiple_of(r * nsub, nsub), nsub)]

    @pl.when(pl.program_id(0) == 0)
    def _():
        ztile_s[...] = jnp.zeros_like(ztile_s)

        def tail_copy(t):
            rows = tm_moe * nsub
            return pltpu.make_async_copy(
                ztile_s, xs_ref.at[pl.ds(pl.multiple_of(t * rows, rows), rows)], tail_sem)

        sizes = [1 << b for b in reversed(range((tm_moe - 1).bit_length()))]

        def fill_pieces(e, act):
            lo, hi = fill_ref[LANES + e], fill_ref[2 * LANES + e]
            at = lo
            for size in sizes:
                has = ((hi - lo) & size) != 0
                copy = pltpu.make_async_copy(
                    ztile_s.at[pl.ds(0, size * nsub)],
                    xs_ref.at[pl.ds(pl.multiple_of(at * nsub, nsub), size * nsub)], fill_sem)
                pl.when(has)(functools.partial(act, copy))
                at = at + jnp.where(has, size, 0)

        def fill_start(e, c):
            fill_pieces(e, lambda copy: copy.start())
            return c

        def fill_wait(e, c):
            fill_pieces(e, lambda copy: copy.wait())
            return c

        n_tiles = xs_ref.shape[0] // (tm_moe * nsub)

        def tail_start(t, c):
            tail_copy(t).start()
            return c

        def tail_drain(_, c):
            tail_copy(0).wait()
            return c

        lax.fori_loop(0, N_EXPERTS, fill_start, 0)
        lax.fori_loop(fill_ref[0], n_tiles, tail_start, 0)
        lax.fori_loop(0, N_EXPERTS, fill_wait, 0)
        lax.fori_loop(fill_ref[0], n_tiles, tail_drain, 0)

    def issue(j, c):
        for r in range(COMBINE_GROUP):
            row = pl.multiple_of(j * COMBINE_GROUP, COMBINE_GROUP) + r
            for k in range(TOP_K):
                pltpu.make_async_copy(tile(h_ref, row), tile(xs_ref, pos_ref[row * TOP_K + k]),
                                      sem).start(priority=k % 2)
        return c

    lax.fori_loop(0, tm // COMBINE_GROUP, issue, 0)
    for _ in range(TOP_K):
        pltpu.make_async_copy(h_ref, xs_ref.at[pl.ds(0, tm * nsub)], sem).wait()


def _dispatch(pos_flat, fill, h2, cap, nsub, tm_moe):
    n = h2.shape[0] // nsub
    tm = min(TM_ROWS, n)
    return pl.pallas_call(
        functools.partial(_dispatch_kernel, tm=tm, nsub=nsub, tm_moe=tm_moe),
        out_shape=jax.ShapeDtypeStruct((cap * nsub, LANES), F32),
        grid=(n // tm,),
        in_specs=[pl.BlockSpec((tm * TOP_K,), lambda i: (i,), memory_space=pltpu.SMEM),
                  pl.BlockSpec(fill.shape, lambda i: (0,), memory_space=pltpu.SMEM),
                  pl.BlockSpec((tm * nsub, LANES), lambda i: (i, 0))],
        out_specs=pl.BlockSpec(memory_space=pl.ANY),
        scratch_shapes=[pltpu.VMEM((tm_moe * nsub, LANES), F32),
                        pltpu.SemaphoreType.DMA(()), pltpu.SemaphoreType.DMA(()),
                        pltpu.SemaphoreType.DMA(())],
        compiler_params=_params(("arbitrary",)),
    )(pos_flat, fill, h2)


def _moe_kernel(te_ref, nv_ref, x_ref, bg_ref, bu_ref, bd_ref, wg_hbm, wu_hbm, wd_hbm,
                y_ref, wf_s, wg_s, wu_s, wd_s, slot_s, sem):
    i = pl.program_id(0)
    nv = nv_ref[0]
    last = te_ref.shape[0] - 1

    def weight_copies(e, slot):
        return [pltpu.make_async_copy(w.at[e], wf_s.at[slot, j], sem.at[slot])
                for j, w in enumerate((wg_hbm, wu_hbm, wd_hbm))]

    @pl.when(i < nv)
    def _():
        e = te_ref[i]

        @pl.when(i == 0)
        def _():
            slot_s[0] = 0
            for c in weight_copies(e, 0):
                c.start()

        @pl.when((i == 0) | (e != te_ref[jnp.maximum(i - 1, 0)]))
        def _():
            slot = slot_s[0]
            for c in weight_copies(e, slot):
                c.wait()
            nxt = lax.while_loop(lambda j: (j < nv) & (te_ref[jnp.minimum(j, last)] == e),
                                 lambda j: j + 1, i + 1)

            @pl.when(nxt < nv)
            def _():
                for c in weight_copies(te_ref[jnp.minimum(nxt, last)], 1 - slot):
                    c.start()

            wg_s[...] = wf_s[slot, 0].astype(BF16)
            wu_s[...] = wf_s[slot, 1].astype(BF16)
            wd_s[...] = wf_s[slot, 2].astype(BF16)
            slot_s[0] = 1 - slot

        xb = _load_token_tiles(x_ref, wg_s.shape[0] // LANES).astype(BF16)
        g = _dot(xb, wg_s[...]) + bg_ref[0]
        u = _dot(xb, wu_s[...]) + bu_ref[0]
        g = jnp.minimum(g, SWIGLU_LIMIT)
        u = jnp.clip(u, -SWIGLU_LIMIT, SWIGLU_LIMIT)
        h = (u + 1.0) * (g * jax.nn.sigmoid(SWIGLU_ALPHA * g))
        _store_token_tiles(y_ref, _dot(h.astype(BF16), wd_s[...]) + bd_ref[0])

    @pl.when(i >= nv)
    def _():
        y_ref[...] = jnp.zeros_like(y_ref)


def _moe(tile_expert, n_valid, xs, w_gate, b_gate, w_up, b_up, w_down, b_down, tm):
    e, d, f = w_gate.shape
    assert d == f
    nsub = d // LANES
    cap = xs.shape[0] // nsub
    n_tiles = cap // tm

    def tile(i, te, nv):
        return jnp.minimum(i, nv[0] - 1)

    exp3 = lambda i, te, nv: (te[tile(i, te, nv)], 0, 0)
    grid_spec = pltpu.PrefetchScalarGridSpec(
        num_scalar_prefetch=2,
        grid=(n_tiles,),
        in_specs=[pl.BlockSpec((tm * nsub, LANES), lambda i, te, nv: (tile(i, te, nv), 0)),
                  pl.BlockSpec((1, 1, f), exp3), pl.BlockSpec((1, 1, f), exp3),
                  pl.BlockSpec((1, 1, d), exp3),
                  pl.BlockSpec(memory_space=pl.ANY), pl.BlockSpec(memory_space=pl.ANY),
                  pl.BlockSpec(memory_space=pl.ANY)],
        out_specs=pl.BlockSpec((tm * nsub, LANES), lambda i, te, nv: (i, 0)),
        scratch_shapes=[pltpu.VMEM((2, 3, d, f), F32),
                        pltpu.VMEM((d, f), BF16), pltpu.VMEM((d, f), BF16), pltpu.VMEM((f, d), BF16),
                        pltpu.SMEM((1,), jnp.int32), pltpu.SemaphoreType.DMA((2,))],
    )
    return pl.pallas_call(
        _moe_kernel,
        out_shape=jax.ShapeDtypeStruct((cap * nsub, LANES), F32),
        grid_spec=grid_spec,
        compiler_params=_params(("arbitrary",), 56),
    )(tile_expert, n_valid, xs, b_gate.reshape(e, 1, f), b_up.reshape(e, 1, f),
      b_down.reshape(e, 1, d), w_gate, w_up, w_down)


def _combine_kernel(pos_ref, posn_ref, w_ref, x1_ref, mod_ref, g_ref, b_ref, y_ref, o_ref,
                    buf, sem, *, tm):
    i = pl.program_id(0)
    n = pl.num_programs(0)
    nsub = x1_ref.shape[1] // LANES

    group = COMBINE_GROUP

    def issue_rows(p_ref, slot, r0):
        for r in range(group):
            for k in range(TOP_K):
                row = r0 + r
                src = pl.ds(pl.multiple_of(p_ref[row * TOP_K + k] * nsub, nsub), nsub)
                dst = pl.ds(pl.multiple_of(row * nsub, nsub), nsub)
                pltpu.make_async_copy(y_ref.at[src], buf.at[slot, k, dst],
                                      sem.at[slot]).start(priority=k % 2)

    def reduce_rows(slot, r0):
        rows = pl.ds(r0, group)
        w = w_ref[rows, :]
        f = jnp.zeros((group, nsub * LANES), F32)
        for k in range(TOP_K):
            yk = jnp.concatenate(
                [buf[slot, k, pl.ds(r0 * nsub + s, group, stride=nsub), :] for s in range(nsub)],
                axis=1)
            f = f + w[:, k:k + 1] * yk
        o_ref[rows, :] = f

    @pl.when(i == 0)
    def _():
        def body(j, c):
            issue_rows(pos_ref, 0, pl.multiple_of(j * group, group))
            return c
        lax.fori_loop(0, tm // group, body, 0)

    slot = i % 2
    for k in range(TOP_K):
        pltpu.make_async_copy(y_ref.at[pl.ds(0, tm * nsub)], buf.at[slot, k], sem.at[slot]).wait()

    @pl.when(i + 1 < n)
    def _():
        def body(j, c):
            r0 = pl.multiple_of(j * group, group)
            issue_rows(posn_ref, 1 - slot, r0)
            reduce_rows(slot, r0)
            return c
        lax.fori_loop(0, tm // group, body, 0)

    @pl.when(i + 1 == n)
    def _():
        def body(j, c):
            reduce_rows(slot, pl.multiple_of(j * group, group))
            return c
        lax.fori_loop(0, tm // group, body, 0)

    o_ref[...] = _layer_norm(DEEPNORM_ALPHA * x1_ref[...] + mod_ref[0, 5:6, :] * o_ref[...],
                             g_ref[...], b_ref[...])


def _combine(pos_flat, w, x1, mod3, g2, b2, y, seq):
    n, d = x1.shape
    tm = min(TM_ROWS, seq)
    nsteps = n // tm
    per_seq = seq // tm
    return pl.pallas_call(
        functools.partial(_combine_kernel, tm=tm),
        out_shape=jax.ShapeDtypeStruct((n, d), F32),
        grid=(nsteps,),
        in_specs=[pl.BlockSpec((tm * TOP_K,), lambda i: (i,), memory_space=pltpu.SMEM),
                  pl.BlockSpec((tm * TOP_K,), lambda i: (jnp.minimum(i + 1, nsteps - 1),),
                               memory_space=pltpu.SMEM),
                  pl.BlockSpec((tm, LANES), lambda i: (i, 0)),
                  pl.BlockSpec((tm, d), lambda i: (i, 0)),
                  pl.BlockSpec((1, 6, d), lambda i: (i // per_seq, 0, 0)),
                  pl.BlockSpec(g2.shape, lambda i: (0, 0)),
                  pl.BlockSpec(b2.shape, lambda i: (0, 0)),
                  pl.BlockSpec(memory_space=pl.ANY)],
        out_specs=pl.BlockSpec((tm, d), lambda i: (i, 0)),
        scratch_shapes=[pltpu.VMEM((2, TOP_K, tm * (d // LANES), LANES), F32),
                        pltpu.SemaphoreType.DMA((2,))],
        compiler_params=_params(("arbitrary",), 32),
    )(pos_flat, pos_flat, w, x1, mod3, g2, b2, y)


def _prep_w_uq(w_uq):
    r = w_uq.shape[0]
    w = w_uq.reshape(r, B_HEADS, B_QK_DIM)
    half = B_ROPE_DIM // 2
    rot = jnp.concatenate([-w[:, :, B_NOPE_DIM + half:], w[:, :, B_NOPE_DIM:B_NOPE_DIM + half]],
                          axis=-1).reshape(r, B_HEADS * B_ROPE_DIM)
    w = jnp.pad(w, ((0, 0), (0, 0), (0, LANES - B_QK_DIM))).reshape(r, B_HEADS * LANES)
    return jnp.concatenate([w, rot], axis=1).astype(BF16)


def _prep_w_ukv(w_ukv):
    r = w_ukv.shape[0]
    w = w_ukv.reshape(r, B_HEADS, B_NOPE_DIM + B_V_DIM)
    wk = jnp.pad(w[:, :, :B_NOPE_DIM], ((0, 0), (0, 0), (0, LANES - B_NOPE_DIM)))
    wv = w[:, :, B_NOPE_DIM:].reshape(r, B_HEADS // 2, 2 * B_V_DIM)
    wv = jnp.pad(wv, ((0, 0), (0, 0), (0, V_PAIR_LANES - 2 * B_V_DIM)))
    return (wk.reshape(r, B_HEADS * LANES).astype(BF16),
            wv.reshape(r, B_HEADS // 2 * V_PAIR_LANES).astype(BF16))


def _rope_tables(positions):
    half = B_ROPE_DIM // 2
    freqs = ROPE_THETA ** (-jnp.arange(half, dtype=F32) / half)
    ang = positions.astype(F32).reshape(1, -1) * freqs[:, None]
    cos, sin = jnp.cos(ang), jnp.sin(ang)
    z = jnp.zeros((B_ROPE_DIM, ang.shape[1]), F32)
    return jnp.concatenate([sin, sin, z, cos, cos, z], 0)


def kernel(x, c, positions, w_ada, b_ada, w_in, rms_q, w_uq, rms_kv, w_ukv, rel_bias,
           w_branch_a, w_branch_b, w_out, ln1_g, ln1_b, w_router, b_router,
           w_gate, b_gate, w_up, b_up, w_down, b_down, ln2_g, ln2_b):
    bsz, seq, d = x.shape
    n_tok = bsz * seq
    assert w_ada.shape[0] == DEPTH == 1
    row = lambda v: v.reshape(1, -1)

    mod3 = _ada(c, w_ada[0], b_ada[0]).reshape(bsz, 6, d)

    qk, va, cq, ckv, kr, ga, gb = _inproj(x, mod3, jnp.swapaxes(w_in, 1, 2))
    ya = _attn_a(qk, va, _bias_table(rel_bias[0], min(TQ_A, seq)))
    wk, wv = _prep_w_ukv(w_ukv[0])
    qm, km, vm = _mla_proj(cq, ckv, kr, _rope_tables(positions), row(rms_q[0]), row(rms_kv[0]),
                           _prep_w_uq(w_uq[0]), wk, wv)
    yb = _mla_attn(qm, km, vm)

    wr = jnp.pad(w_router[0], ((0, 0), (0, LANES - N_EXPERTS)))
    br = jnp.pad(b_router[0], (0, LANES - N_EXPERTS), constant_values=NEG_INF)
    x1, h2, code, wts, cnt = _merge(x, ya, yb, ga, gb, mod3,
                                    w_branch_a[0].astype(BF16), w_branch_b[0].astype(BF16),
                                    w_out[0].astype(BF16), row(ln1_g[0]), row(ln1_b[0]), wr, row(br))
    wts = wts.reshape(n_tok, LANES)

    tm = TM_MOE
    n_tiles = -(-(n_tok * TOP_K) // tm) + N_EXPERTS
    assert n_tiles <= MAX_TILES_PAD
    pos, tmap, meta = _route(code.reshape(n_tok, LANES), cnt, tm)
    pos_flat = pos[:, :TOP_K].reshape(-1)
    fill = meta[:3].reshape(-1)
    nsub = d // LANES
    xs = _dispatch(pos_flat, fill, h2.reshape(n_tok * nsub, LANES), n_tiles * tm, nsub, tm)
    y = _moe(tmap[:n_tiles, 0], meta[0, :1], xs, w_gate[0], b_gate[0], w_up[0], b_up[0],
             w_down[0], b_down[0], tm)
    out = _combine(pos_flat, wts, x1.reshape(n_tok, d), mod3, row(ln2_g[0]), row(ln2_b[0]), y, seq)
    return out.reshape(bsz, seq, d)
```

```python
import functools

import jax
import jax.numpy as jnp
from jax import lax
from jax.experimental import pallas as pl
from jax.experimental.pallas import tpu as pltpu

F32 = jnp.float32
BF16 = jnp.bfloat16

CHUNK = 64
A_HEADS = 8
A_HEAD_DIM = 64
A_LEFT_CHUNKS = 8
A_MAX_REL = 128
A_WIDTH = A_HEADS * A_HEAD_DIM
B_HEADS = 8
B_NOPE_DIM = 64
B_ROPE_DIM = 32
B_V_DIM = 64
B_Q_LORA = 384
B_KV_LORA = 256
B_QK_DIM = B_NOPE_DIM + B_ROPE_DIM
ROPE_THETA = 10000.0
N_EXPERTS = 32
TOP_K = 4
SWIGLU_LIMIT = 7.0
SWIGLU_ALPHA = 1.702
DEPTH = 1
DEEPNORM_ALPHA = (2.0 * DEPTH) ** 0.25
LN_EPS = 1e-5
RMS_EPS = 1e-6
NEG_INF = -1e30
LOG2_E = 1.4426950408889634

LANES = 128
MXU_COLS = 256
V_PAIR_LANES = MXU_COLS

TM_PROJ = 512
TM_MLA_PROJ = 1024
TQ_A = 256
A_QBLOCKS = 8
TQ_B = 512
TM_MERGE = 1024
MERGE_SPLIT = 2
TR_ROUTE = 1024
TM_MOE = 512
TM_ROWS = 512
COMBINE_GROUP = 8
MAX_TILES_PAD = 512


def _params(sem, vmem_mb=None):
    return pltpu.CompilerParams(
        dimension_semantics=sem,
        vmem_limit_bytes=None if vmem_mb is None else vmem_mb << 20)


def _dot(a, b):
    return jnp.dot(a, b, preferred_element_type=F32)


def _dot_nt(a, b):
    return lax.dot_general(a, b, (((1,), (1,)), ((), ())), preferred_element_type=F32)


def _store_token_tiles(ref, x):
    rows, d = x.shape
    nsub = d // LANES
    for s in range(nsub):
        ref[pl.ds(s, rows, stride=nsub), :] = x[:, s * LANES:(s + 1) * LANES]


def _load_token_tiles(ref, nsub):
    rows = ref.shape[0] // nsub
    return jnp.concatenate([ref[pl.ds(s, rows, stride=nsub), :] for s in range(nsub)], axis=1)


def _layer_norm(x, g, b):
    mu = jnp.mean(x, axis=-1, keepdims=True)
    xc = x - mu
    var = jnp.mean(xc * xc, axis=-1, keepdims=True)
    return xc * lax.rsqrt(var + LN_EPS) * g + b


def _ada_kernel(c_ref, w_ref, b_ref, o_ref):
    c = c_ref[...]
    sc = c * jax.nn.sigmoid(c)
    w = w_ref[...]
    s_hi, w_hi = sc.astype(BF16), w.astype(BF16)
    s_lo = (sc - s_hi.astype(F32)).astype(BF16)
    w_lo = (w - w_hi.astype(F32)).astype(BF16)
    o_ref[...] = _dot(s_hi, w_hi) + _dot(s_hi, w_lo) + _dot(s_lo, w_hi) + b_ref[...]


def _ada(c, w_ada, b_ada):
    bsz, d = c.shape
    m = w_ada.shape[1]
    tn = 1024
    return pl.pallas_call(
        _ada_kernel,
        out_shape=jax.ShapeDtypeStruct((bsz, m), F32),
        grid=(m // tn,),
        in_specs=[pl.BlockSpec((bsz, d), lambda j: (0, 0)),
                  pl.BlockSpec((d, tn), lambda j: (0, j)),
                  pl.BlockSpec((1, tn), lambda j: (0, j))],
        out_specs=pl.BlockSpec((bsz, tn), lambda j: (0, j)),
        compiler_params=_params(("arbitrary",)),
    )(c, w_ada, b_ada.reshape(1, m))


def _inproj_kernel(x_ref, mod_ref, win_hbm, qk_ref, va_ref, cq_ref, ckv_ref, kr_ref, ga_ref, gb_ref,
                   wf_s, wqk_ref, wv_ref, wcq_ref, wckv_ref, wkr_ref, wga_ref, wgb_ref, sem):
    @pl.when((pl.program_id(0) == 0) & (pl.program_id(1) == 0))
    def _():
        copy = pltpu.make_async_copy(win_hbm.at[0], wf_s, sem)
        copy.start()
        copy.wait()
        d = wf_s.shape[1]
        chunk = 2 * LANES
        off = 0
        for dst, n in ((wqk_ref, 2 * A_WIDTH), (wv_ref, A_WIDTH), (wcq_ref, B_Q_LORA),
                       (wckv_ref, B_KV_LORA), (None, B_ROPE_DIM), (wga_ref, d), (wgb_ref, d)):
            if dst is None:
                blk = wf_s[off:off + LANES, :].T
                lane = lax.broadcasted_iota(jnp.int32, blk.shape, 1)
                blk = pltpu.roll(jnp.where(lane < n, blk, 0.0), B_NOPE_DIM, 1)
                wkr_ref[...] = blk.astype(BF16)
            else:
                for c0 in range(0, n, chunk):
                    cn = min(chunk, n - c0)
                    blk = wf_s[off + c0:off + c0 + cn, :].T
                    if dst is wqk_ref and c0 < A_WIDTH:
                        blk = blk * (A_HEAD_DIM ** -0.5 * LOG2_E)
                    dst[:, c0:c0 + cn] = blk.astype(BF16)
            off += n

    m = mod_ref[0]
    h = (x_ref[0] * (1.0 + m[1:2]) + m[0:1]).astype(BF16)
    for w_ref, ref in ((wqk_ref, qk_ref), (wcq_ref, cq_ref), (wckv_ref, ckv_ref),
                       (wkr_ref, kr_ref), (wga_ref, ga_ref), (wgb_ref, gb_ref)):
        ref[0] = _dot(h, w_ref[...]).astype(ref.dtype)
    v = _dot(h, wv_ref[...]).astype(BF16)
    ones_blk = jnp.ones((v.shape[0], LANES), BF16)
    parts = []
    for hp in range(A_WIDTH // LANES):
        parts += [v[:, hp * LANES:(hp + 1) * LANES], ones_blk]
    va_ref[0] = jnp.concatenate(parts, axis=1)


def _inproj(x, mod3, w_in):
    bsz, seq, d = x.shape
    tm = min(TM_PROJ, seq)
    widths = (2 * A_WIDTH, A_WIDTH // LANES * V_PAIR_LANES, B_Q_LORA, B_KV_LORA, LANES, d, d)
    dtypes = (BF16, BF16, BF16, BF16, F32, BF16, BF16)
    w_widths = (2 * A_WIDTH, A_WIDTH, B_Q_LORA, B_KV_LORA, LANES, d, d)
    assert w_in.shape[0] == 1 and w_in.shape[1] == sum(w_widths) - LANES + B_ROPE_DIM
    return pl.pallas_call(
        _inproj_kernel,
        out_shape=[jax.ShapeDtypeStruct((bsz, seq, n), dt) for n, dt in zip(widths, dtypes)],
        grid=(bsz, seq // tm),
        in_specs=[pl.BlockSpec((1, tm, d), lambda b, i: (b, i, 0)),
                  pl.BlockSpec((1, 6, d), lambda b, i: (b, 0, 0)),
                  pl.BlockSpec(memory_space=pl.ANY)],
        out_specs=[pl.BlockSpec((1, tm, n), lambda b, i: (b, i, 0)) for n in widths],
        scratch_shapes=[pltpu.VMEM(w_in.shape[1:], F32)]
                       + [pltpu.VMEM((d, n), BF16) for n in w_widths]
                       + [pltpu.SemaphoreType.DMA(())],
        compiler_params=_params(("arbitrary", "arbitrary"), 56),
    )(x, mod3, w_in)


def _attn_a_kernel(q_ref, k_ref, v_ref, bias_ref, o_ref, *, tq, nd, nblk, steps):
    band = nd * tq
    lane = lax.broadcasted_iota(jnp.int32, (1, LANES), 1)

    def shifted(base, missing):
        cut = missing * tq
        if cut == 0:
            return base
        return jnp.concatenate([base[:, cut:], jnp.full((tq, cut), NEG_INF, F32)], axis=1)

    def bias_for(hh, i):
        base = bias_ref[hh]
        if isinstance(i, int):
            return shifted(base, max(nd - 1 - i, 0))
        out = base
        for first in range(nd - 1):
            out = jnp.where(i == first, shifted(base, nd - 1 - first), out)
        return out

    for sub in range(nblk):
        if steps == 1:
            i = sub
            rows = slice(max(i - (nd - 1), 0) * tq, max(i - (nd - 1), 0) * tq + band)
        else:
            i = pl.program_id(2) * nblk + sub
            rows = pl.ds(pl.multiple_of(jnp.maximum(i - (nd - 1), 0) * tq, tq), band)
        q = q_ref[0, sub * tq:(sub + 1) * tq, :]
        k = k_ref[0, rows, :]
        v = v_ref[0, rows, :]
        zero = jnp.zeros_like(k)
        kk = jnp.concatenate([jnp.where(lane < A_HEAD_DIM, k, zero),
                              jnp.where(lane >= A_HEAD_DIM, k, zero)], axis=0)
        s = _dot_nt(q, kk)
        ps = []
        for hh in range(2):
            sh = s[:, hh * band:(hh + 1) * band] + bias_for(hh, i)
            m = sh.max(axis=-1, keepdims=True)
            ps.append(jnp.exp2(sh - m).astype(BF16))
        acc = _dot(jnp.concatenate(ps, axis=0), v)
        out = acc[:, :LANES] / acc[:, LANES:]
        o_ref[0, sub * tq:(sub + 1) * tq, :] = jnp.where(
            lane < A_HEAD_DIM, out[:tq], out[tq:]).astype(o_ref.dtype)


def _attn_a(qk, va, bias_tab):
    bsz, seq, _ = qk.shape
    tq = min(TQ_A, seq)
    nd = bias_tab.shape[2] // tq
    assert seq >= nd * tq
    ncb = A_WIDTH // LANES
    nblk = min(A_QBLOCKS, seq // tq)
    steps = seq // (nblk * tq)
    return pl.pallas_call(
        functools.partial(_attn_a_kernel, tq=tq, nd=nd, nblk=nblk, steps=steps),
        out_shape=jax.ShapeDtypeStruct((bsz, seq, A_WIDTH), BF16),
        grid=(ncb, bsz, steps),
        in_specs=[pl.BlockSpec((1, nblk * tq, LANES), lambda hp, b, i: (b, i, hp)),
                  pl.BlockSpec((1, seq, LANES), lambda hp, b, i: (b, 0, ncb + hp)),
                  pl.BlockSpec((1, seq, V_PAIR_LANES), lambda hp, b, i: (b, 0, hp)),
                  pl.BlockSpec((2, tq, nd * tq), lambda hp, b, i: (hp, 0, 0))],
        out_specs=pl.BlockSpec((1, nblk * tq, LANES), lambda hp, b, i: (b, i, hp)),
        compiler_params=_params(("arbitrary",) * 3, 40),
    )(qk, qk, va, bias_tab)


def _bias_table(rel_bias, tq):
    nd = (A_LEFT_CHUNKS * CHUNK + tq - 1) // tq + 1
    width = nd * tq
    period = width + tq
    u = jnp.arange(period)
    dist = jnp.where(u < width, (nd - 1) * tq - u, (nd - 1) * tq + period - u)
    vec = rel_bias[:, jnp.clip(dist, -A_MAX_REL, A_MAX_REL) + A_MAX_REL].astype(F32)
    toep = jnp.tile(vec, (1, tq))[:, :tq * (period - 1)].reshape(-1, tq, period - 1)[:, :, :width]
    qi = jnp.arange(tq)[:, None]
    x = jnp.arange(width)[None, :]
    dchunk = qi // CHUNK - x // CHUNK + (nd - 1) * (tq // CHUNK)
    valid = (dchunk >= 0) & (dchunk <= A_LEFT_CHUNKS)
    return jnp.where(valid[None], toep * LOG2_E, NEG_INF)


def _mla_proj_kernel(cq_ref, ckv_ref, kr_ref, tab_ref, rq_ref, rkv_ref,
                     wq_ref, wk_ref, wv_ref, q_out, k_out, v_out):
    def rms(x, g):
        xf = x.astype(F32)
        ms = jnp.mean(xf * xf, axis=-1, keepdims=True)
        return (xf * lax.rsqrt(ms + RMS_EPS) * g).astype(BF16)

    cqn = rms(cq_ref[0], rq_ref[...])
    ckvn = rms(ckv_ref[0], rkv_ref[...])
    q = _dot(cqn, wq_ref[...])
    kn = _dot(ckvn, wk_ref[...])
    vlane = lax.broadcasted_iota(jnp.int32, (1, v_out.shape[-1]), 1)
    ones_lane = jnp.where(vlane % V_PAIR_LANES >= 2 * B_V_DIM, 1.0, 0.0)
    v_out[0] = (_dot(ckvn, wv_ref[...]) + ones_lane).astype(v_out.dtype)
    tab = tab_ref[...].T
    tlane = lax.broadcasted_iota(jnp.int32, tab.shape, 1)
    rope_end = B_NOPE_DIM + B_ROPE_DIM
    in_rope = (tlane >= B_NOPE_DIM) & (tlane < rope_end)
    cos_t = jnp.where(tlane < B_NOPE_DIM, 1.0, jnp.where(in_rope, tab, 0.0))
    sin_t = jnp.where(in_rope, pltpu.roll(tab, B_NOPE_DIM, 1), 0.0)
    first = tlane < B_NOPE_DIM + B_ROPE_DIM // 2
    sin_a = jnp.where(first, -sin_t, 0.0)
    sin_b = jnp.where(first, 0.0, sin_t)

    def rope(x, tables):
        c, sa, sb = tables
        return (x * c + pltpu.roll(x, LANES - B_ROPE_DIM // 2, 1) * sa
                + pltpu.roll(x, B_ROPE_DIM // 2, 1) * sb)

    kpe = rope(kr_ref[0], (cos_t, sin_a, sin_b))
    scale = B_QK_DIM ** -0.5 * LOG2_E
    cos_q, sin_q = cos_t * scale, sin_t * scale
    nq = B_HEADS * LANES
    per_block = LANES // B_ROPE_DIM
    for h in range(B_HEADS):
        sl = slice(h * LANES, (h + 1) * LANES)
        blk = nq + (h // per_block) * LANES
        partner = q[:, blk:blk + LANES]
        shift = (B_NOPE_DIM - B_ROPE_DIM * (h % per_block)) % LANES
        if shift:
            partner = pltpu.roll(partner, shift, 1)
        q_out[0, :, sl] = (q[:, sl] * cos_q + partner * sin_q).astype(q_out.dtype)
        k_out[0, :, sl] = (kn[:, sl] + kpe).astype(k_out.dtype)


def _mla_proj(cq, ckv, kr, rope_tab, rms_q, rms_kv, wq, wk, wv):
    bsz, seq, _ = cq.shape
    tm = min(TM_MLA_PROJ, seq)
    tok = lambda n: pl.BlockSpec((1, tm, n), lambda b, i: (b, i, 0))
    full = lambda a: pl.BlockSpec(a.shape, lambda b, i: (0,) * a.ndim)
    return pl.pallas_call(
        _mla_proj_kernel,
        out_shape=[jax.ShapeDtypeStruct((bsz, seq, B_HEADS * LANES), BF16),
                   jax.ShapeDtypeStruct((bsz, seq, B_HEADS * LANES), BF16),
                   jax.ShapeDtypeStruct((bsz, seq, wv.shape[1]), BF16)],
        grid=(bsz, seq // tm),
        in_specs=[tok(B_Q_LORA), tok(B_KV_LORA), tok(LANES),
                  pl.BlockSpec((LANES, tm), lambda b, i: (0, b * (seq // tm) + i)),
                  full(rms_q), full(rms_kv), full(wq), full(wk), full(wv)],
        out_specs=[tok(B_HEADS * LANES), tok(B_HEADS * LANES), tok(wv.shape[1])],
        compiler_params=_params(("arbitrary", "arbitrary"), 32),
    )(cq, ckv, kr, rope_tab, rms_q, rms_kv, wq, wk, wv)


def _mla_attn_kernel(q_ref, k_ref, v_ref, o_ref, s_a, s_b, m_s, acc_s, *, t):
    lane = lax.broadcasted_iota(jnp.int32, (1, LANES), 1)
    nq = q_ref.shape[1] // t
    pairs = [(qi, kj) for qi in range(nq) for kj in range(qi + 1)]
    bufs = (s_a, s_b)

    def produce(qi, kj, s_ref):
        for hh in range(2):
            q = q_ref[0, qi * t:(qi + 1) * t, hh * LANES:(hh + 1) * LANES]
            k = k_ref[0, kj * t:(kj + 1) * t, hh * LANES:(hh + 1) * LANES]
            s_ref[hh] = _dot_nt(q, k)

    def consume(qi, kj, s_ref):
        v = v_ref[0, kj * t:(kj + 1) * t, :]
        outs = []
        for hh in range(2):
            s = s_ref[hh]
            if kj == qi:
                row_chunk = lax.broadcasted_iota(jnp.int32, (t, t), 0) // CHUNK
                col_chunk = lax.broadcasted_iota(jnp.int32, (t, t), 1) // CHUNK
                s = jnp.where(col_chunk <= row_chunk, s, NEG_INF)
            row_max = jnp.broadcast_to(s.max(axis=-1, keepdims=True), (t, LANES))
            if kj == 0:
                m_new = row_max
                acc = _dot(jnp.exp2(s - jnp.tile(m_new, (1, t // LANES))).astype(BF16), v)
            else:
                m_old = m_s[hh]
                m_new = jnp.maximum(m_old, row_max)
                p = jnp.exp2(s - jnp.tile(m_new, (1, t // LANES))).astype(BF16)
                alpha = jnp.exp2(m_old - m_new)
                acc = jnp.tile(alpha, (1, V_PAIR_LANES // LANES)) * acc_s[hh] + _dot(p, v)
            if kj == qi:
                outs.append(acc[:, :LANES] / acc[:, LANES:])
            else:
                acc_s[hh] = acc
                m_s[hh] = m_new
        if kj == qi:
            o_ref[0, qi * t:(qi + 1) * t, :] = jnp.where(
                lane < B_V_DIM, outs[0], outs[1]).astype(o_ref.dtype)

    produce(*pairs[0], bufs[0])
    for n, pair in enumerate(pairs):
        if n + 1 < len(pairs):
            produce(*pairs[n + 1], bufs[(n + 1) % 2])
        consume(*pair, bufs[n % 2])


def _mla_attn(qm, km, vm):
    bsz, seq, _ = qm.shape
    t = min(TQ_B, seq)
    return pl.pallas_call(
        functools.partial(_mla_attn_kernel, t=t),
        out_shape=jax.ShapeDtypeStruct((bsz, seq, B_HEADS * B_V_DIM), BF16),
        grid=(bsz, B_HEADS // 2),
        in_specs=[pl.BlockSpec((1, seq, 2 * LANES), lambda b, hp: (b, 0, hp)),
                  pl.BlockSpec((1, seq, 2 * LANES), lambda b, hp: (b, 0, hp)),
                  pl.BlockSpec((1, seq, V_PAIR_LANES), lambda b, hp: (b, 0, hp))],
        out_specs=pl.BlockSpec((1, seq, LANES), lambda b, hp: (b, 0, hp)),
        scratch_shapes=[pltpu.VMEM((2, t, t), F32), pltpu.VMEM((2, t, t), F32),
                        pltpu.VMEM((2, t, LANES), F32),
                        pltpu.VMEM((2, t, V_PAIR_LANES), F32)],
        compiler_params=_params(("arbitrary",) * 2, 40),
    )(qm, km, vm)


def _merge_kernel(x_ref, ya_ref, yb_ref, ga_ref, gb_ref, mod_ref, wa_ref, wb_ref, wo_ref,
                  g1_ref, b1_ref, wrh_ref, wrl_ref, br_ref, x1_ref, h2_ref, code_ref, wts_ref,
                  cnt_ref):
    @pl.when((pl.program_id(0) == 0) & (pl.program_id(1) == 0))
    def _():
        cnt_ref[...] = jnp.zeros_like(cnt_ref)

    tm = x_ref.shape[1]
    half = tm // MERGE_SPLIT
    for part in range(MERGE_SPLIT):
        _merge_rows(slice(part * half, (part + 1) * half), x_ref, ya_ref, yb_ref, ga_ref, gb_ref,
                    mod_ref, wa_ref, wb_ref, wo_ref, g1_ref, b1_ref, wrh_ref, wrl_ref, br_ref,
                    x1_ref, h2_ref, code_ref, wts_ref, cnt_ref)


def _merge_rows(rows, x_ref, ya_ref, yb_ref, ga_ref, gb_ref, mod_ref, wa_ref, wb_ref, wo_ref,
                g1_ref, b1_ref, wrh_ref, wrl_ref, br_ref, x1_ref, h2_ref, code_ref, wts_ref,
                cnt_ref):
    m = mod_ref[0]
    a = _dot(ya_ref[0, rows, :], wa_ref[...])
    b = _dot(yb_ref[0, rows, :], wb_ref[...])
    merged = (jax.nn.sigmoid(ga_ref[0, rows, :].astype(F32)) * a
              + jax.nn.sigmoid(gb_ref[0, rows, :].astype(F32)) * b)
    o = _dot(merged.astype(BF16), wo_ref[...])
    x1 = _layer_norm(DEEPNORM_ALPHA * x_ref[0, rows, :] + m[2:3] * o, g1_ref[...], b1_ref[...])
    h2 = x1 * (1.0 + m[4:5]) + m[3:4]
    x1_ref[0, rows, :] = x1
    nsub = h2.shape[1] // LANES
    _store_token_tiles(h2_ref.at[0, pl.ds(rows.start * nsub, (rows.stop - rows.start) * nsub)], h2)
    h_hi = h2.astype(BF16)
    h_lo = (h2 - h_hi.astype(F32)).astype(BF16)
    work = (_dot(h_hi, wrh_ref[...]) + _dot(h_hi, wrl_ref[...]) + _dot(h_lo, wrh_ref[...])
            + br_ref[...])
    lane = lax.broadcasted_iota(jnp.int32, work.shape, 1)
    code = jnp.zeros(work.shape, F32)
    vals = []
    for k in range(TOP_K):
        top = work.max(axis=-1, keepdims=True)
        idx = jnp.where(work == top, lane, LANES).min(axis=-1, keepdims=True)
        sel = lane == idx
        work = jnp.where(sel, -jnp.inf, work)
        code = jnp.where(sel, k + 1.0, code)
        vals.append(top)
    den = sum(jnp.exp(v - vals[0]) for v in vals)
    wts = jnp.zeros(work.shape, F32)
    for k in range(TOP_K):
        wts = jnp.where(lane == k, jnp.exp(vals[k] - vals[0]) / den, wts)
    code_ref[0, rows, :] = code
    wts_ref[0, rows, :] = wts
    picked = jnp.where(code > 0.0, 1.0, 0.0)
    cnt_ref[...] += jnp.broadcast_to(picked.sum(axis=0, keepdims=True), cnt_ref.shape)


def _merge(x, ya, yb, ga, gb, mod3, wa, wb, wo, g1, b1, wr, br):
    wr_hi = wr.astype(BF16)
    wr_lo = (wr - wr_hi.astype(F32)).astype(BF16)
    bsz, seq, d = x.shape
    tm = min(TM_MERGE, seq)
    tok = lambda n: pl.BlockSpec((1, tm, n), lambda b, i: (b, i, 0))
    full = lambda a: pl.BlockSpec(a.shape, lambda b, i: (0,) * a.ndim)
    return pl.pallas_call(
        _merge_kernel,
        out_shape=[jax.ShapeDtypeStruct((bsz, seq, d), F32),
                   jax.ShapeDtypeStruct((bsz, seq * (d // LANES), LANES), F32),
                   jax.ShapeDtypeStruct((bsz, seq, LANES), F32),
                   jax.ShapeDtypeStruct((bsz, seq, LANES), F32),
                   jax.ShapeDtypeStruct((8, LANES), F32)],
        grid=(bsz, seq // tm),
        in_specs=[tok(d), tok(A_WIDTH), tok(B_HEADS * B_V_DIM), tok(d), tok(d),
                  pl.BlockSpec((1, 6, d), lambda b, i: (b, 0, 0)),
                  full(wa), full(wb), full(wo), full(g1), full(b1), full(wr_hi), full(wr_lo),
                  full(br)],
        out_specs=[tok(d),
                   pl.BlockSpec((1, tm * (d // LANES), LANES), lambda b, i: (b, i, 0)),
                   tok(LANES), tok(LANES), pl.BlockSpec((8, LANES), lambda b, i: (0, 0))],
        compiler_params=_params(("arbitrary", "arbitrary"), 48),
    )(x, ya, yb, ga, gb, mod3, wa, wb, wo, g1, b1, wr_hi, wr_lo, br)


def _route_kernel(code_ref, cnt_ref, pos_ref, tmap_ref, nv_ref, carry_s, pstart_s, *, tr, tm):
    i = pl.program_id(0)
    lane = lax.broadcasted_iota(jnp.int32, (tr, LANES), 1)
    code = code_ref[...]
    picked = jnp.where(code > 0.0, 1.0, 0.0)
    tile_count = jnp.broadcast_to(picked.sum(axis=0, keepdims=True), (8, LANES))

    @pl.when(i == 0)
    def _():
        lane8 = lax.broadcasted_iota(jnp.int32, (8, LANES), 1)
        padded = jnp.floor((cnt_ref[...] + (tm - 1)) * (1.0 / tm)) * tm
        cum = padded
        shift = 1
        while shift < N_EXPERTS:
            cum = cum + jnp.where(lane8 >= shift, pltpu.roll(cum, shift, 1), 0.0)
            shift *= 2
        pstart_s[...] = cum - padded
        carry_s[...] = jnp.zeros_like(carry_s)
        nt = tmap_ref.shape[0]
        tile_start = (lax.broadcasted_iota(jnp.int32, (nt, LANES), 0) * tm).astype(F32)
        lane_t = lax.broadcasted_iota(jnp.int32, (nt, LANES), 1)
        done = jnp.where((lane_t < N_EXPERTS) & (cum[0:1] <= tile_start), 1.0, 0.0)
        te = jnp.minimum(done.sum(axis=-1, keepdims=True), N_EXPERTS - 1.0)
        tmap_ref[...] = jnp.broadcast_to(te, (nt, LANES)).astype(jnp.int32)
        total = jnp.where(lane8 == N_EXPERTS - 1, cum, 0.0).sum(axis=-1, keepdims=True)
        row8 = lax.broadcasted_iota(jnp.int32, (8, LANES), 0)
        meta = jnp.where(row8 == 0, jnp.broadcast_to(total * (1.0 / tm), (8, LANES)),
                         jnp.where(row8 == 1, cum - padded + cnt_ref[...], cum))
        nv_ref[...] = meta.astype(jnp.int32)

    r = lax.broadcasted_iota(jnp.int32, (tr, tr), 0)
    c = lax.broadcasted_iota(jnp.int32, (tr, tr), 1)
    tri = jnp.where(c < r, 1.0, 0.0).astype(BF16)
    base = _dot(tri, picked.astype(BF16)) + (carry_s[0:1] + pstart_s[0:1])
    carry_s[...] += tile_count
    pos = jnp.zeros((tr, LANES), F32)
    for k in range(TOP_K):
        pk = jnp.where(code == k + 1.0, base, 0.0).sum(axis=-1, keepdims=True)
        pos = jnp.where(lane == k, pk, pos)
    pos_ref[...] = pos.astype(jnp.int32)


def _route(code, cnt, tm):
    n = code.shape[0]
    tr = min(TR_ROUTE, n)
    return pl.pallas_call(
        functools.partial(_route_kernel, tr=tr, tm=tm),
        out_shape=[jax.ShapeDtypeStruct((n, LANES), jnp.int32),
                   jax.ShapeDtypeStruct((MAX_TILES_PAD, LANES), jnp.int32),
                   jax.ShapeDtypeStruct((8, LANES), jnp.int32)],
        grid=(n // tr,),
        in_specs=[pl.BlockSpec((tr, LANES), lambda i: (i, 0)),
                  pl.BlockSpec((8, LANES), lambda i: (0, 0))],
        out_specs=[pl.BlockSpec((tr, LANES), lambda i: (i, 0)),
                   pl.BlockSpec((MAX_TILES_PAD, LANES), lambda i: (0, 0)),
                   pl.BlockSpec((8, LANES), lambda i: (0, 0))],
        scratch_shapes=[pltpu.VMEM((8, LANES), F32)] * 2,
        compiler_params=_params(("arbitrary",)),
    )(code, cnt)


def _dispatch_kernel(pos_ref, fill_ref, h_ref, xs_ref, ztile_s, sem, fill_sem, tail_sem,
                     *, tm, nsub, tm_moe):
    def tile(ref, r):
        return ref.at[pl.ds(pl.multiple_of(r * nsub, nsub), nsub)]

    @pl.when(pl.program_id(0) == 0)
    def _():
        ztile_s[...] = jnp.zeros_like(ztile_s)

        def tail_copy(t):
            rows = tm_moe * nsub
            return pltpu.make_async_copy(
                ztile_s, xs_ref.at[pl.ds(pl.multiple_of(t * rows, rows), rows)], tail_sem)

        sizes = [1 << b for b in reversed(range((tm_moe - 1).bit_length()))]

        def fill_pieces(e, act):
            lo, hi = fill_ref[LANES + e], fill_ref[2 * LANES + e]
            at = lo
            for size in sizes:
                has = ((hi - lo) & size) != 0
                copy = pltpu.make_async_copy(
                    ztile_s.at[pl.ds(0, size * nsub)],
                    xs_ref.at[pl.ds(pl.multiple_of(at * nsub, nsub), size * nsub)], fill_sem)
                pl.when(has)(functools.partial(act, copy))
                at = at + jnp.where(has, size, 0)

        def fill_start(e, c):
            fill_pieces(e, lambda copy: copy.start())
            return c

        def fill_wait(e, c):
            fill_pieces(e, lambda copy: copy.wait())
            return c

        n_tiles = xs_ref.shape[0] // (tm_moe * nsub)

        def tail_start(t, c):
            tail_copy(t).start()
            return c

        def tail_drain(_, c):
            tail_copy(0).wait()
            return c

        lax.fori_loop(0, N_EXPERTS, fill_start, 0)
        lax.fori_loop(fill_ref[0], n_tiles, tail_start, 0)
        lax.fori_loop(0, N_EXPERTS, fill_wait, 0)
        lax.fori_loop(fill_ref[0], n_tiles, tail_drain, 0)

    def issue(j, c):
        for r in range(COMBINE_GROUP):
            row = pl.multiple_of(j * COMBINE_GROUP, COMBINE_GROUP) + r
            for k in range(TOP_K):
                pltpu.make_async_copy(tile(h_ref, row), tile(xs_ref, pos_ref[row * TOP_K + k]),
                                      sem).start(priority=k % 2)
        return c

    lax.fori_loop(0, tm // COMBINE_GROUP, issue, 0)
    for _ in range(TOP_K):
        pltpu.make_async_copy(h_ref, xs_ref.at[pl.ds(0, tm * nsub)], sem).wait()


def _dispatch(pos_flat, fill, h2, cap, nsub, tm_moe):
    n = h2.shape[0] // nsub
    tm = min(TM_ROWS, n)
    return pl.pallas_call(
        functools.partial(_dispatch_kernel, tm=tm, nsub=nsub, tm_moe=tm_moe),
        out_shape=jax.ShapeDtypeStruct((cap * nsub, LANES), F32),
        grid=(n // tm,),
        in_specs=[pl.BlockSpec((tm * TOP_K,), lambda i: (i,), memory_space=pltpu.SMEM),
                  pl.BlockSpec(fill.shape, lambda i: (0,), memory_space=pltpu.SMEM),
                  pl.BlockSpec((tm * nsub, LANES), lambda i: (i, 0))],
        out_specs=pl.BlockSpec(memory_space=pl.ANY),
        scratch_shapes=[pltpu.VMEM((tm_moe * nsub, LANES), F32),
                        pltpu.SemaphoreType.DMA(()), pltpu.SemaphoreType.DMA(()),
                        pltpu.SemaphoreType.DMA(())],
        compiler_params=_params(("arbitrary",)),
    )(pos_flat, fill, h2)


def _moe_kernel(te_ref, nv_ref, x_ref, bg_ref, bu_ref, bd_ref, wg_hbm, wu_hbm, wd_hbm,
                y_ref, wf_s, wg_s, wu_s, wd_s, slot_s, sem):
    i = pl.program_id(0)
    nv = nv_ref[0]
    last = te_ref.shape[0] - 1

    def weight_copies(e, slot):
        return [pltpu.make_async_copy(w.at[e], wf_s.at[slot, j], sem.at[slot])
                for j, w in enumerate((wg_hbm, wu_hbm, wd_hbm))]

    @pl.when(i < nv)
    def _():
        e = te_ref[i]

        @pl.when(i == 0)
        def _():
            slot_s[0] = 0
            for c in weight_copies(e, 0):
                c.start()

        @pl.when((i == 0) | (e != te_ref[jnp.maximum(i - 1, 0)]))
        def _():
            slot = slot_s[0]
            for c in weight_copies(e, slot):
                c.wait()
            nxt = lax.while_loop(lambda j: (j < nv) & (te_ref[jnp.minimum(j, last)] == e),
                                 lambda j: j + 1, i + 1)

            @pl.when(nxt < nv)
            def _():
                for c in weight_copies(te_ref[jnp.minimum(nxt, last)], 1 - slot):
                    c.start()

            wg_s[...] = wf_s[slot, 0].astype(BF16)
            wu_s[...] = wf_s[slot, 1].astype(BF16)
            wd_s[...] = wf_s[slot, 2].astype(BF16)
            slot_s[0] = 1 - slot

        xb = _load_token_tiles(x_ref, wg_s.shape[0] // LANES).astype(BF16)
        g = _dot(xb, wg_s[...]) + bg_ref[0]
        u = _dot(xb, wu_s[...]) + bu_ref[0]
        g = jnp.minimum(g, SWIGLU_LIMIT)
        u = jnp.clip(u, -SWIGLU_LIMIT, SWIGLU_LIMIT)
        h = (u + 1.0) * (g * jax.nn.sigmoid(SWIGLU_ALPHA * g))
        _store_token_tiles(y_ref, _dot(h.astype(BF16), wd_s[...]) + bd_ref[0])

    @pl.when(i >= nv)
    def _():
        y_ref[...] = jnp.zeros_like(y_ref)


def _moe(tile_expert, n_valid, xs, w_gate, b_gate, w_up, b_up, w_down, b_down, tm):
    e, d, f = w_gate.shape
    assert d == f
    nsub = d // LANES
    cap = xs.shape[0] // nsub
    n_tiles = cap // tm

    def tile(i, te, nv):
        return jnp.minimum(i, nv[0] - 1)

    exp3 = lambda i, te, nv: (te[tile(i, te, nv)], 0, 0)
    grid_spec = pltpu.PrefetchScalarGridSpec(
        num_scalar_prefetch=2,
        grid=(n_tiles,),
        in_specs=[pl.BlockSpec((tm * nsub, LANES), lambda i, te, nv: (tile(i, te, nv), 0)),
                  pl.BlockSpec((1, 1, f), exp3), pl.BlockSpec((1, 1, f), exp3),
                  pl.BlockSpec((1, 1, d), exp3),
                  pl.BlockSpec(memory_space=pl.ANY), pl.BlockSpec(memory_space=pl.ANY),
                  pl.BlockSpec(memory_space=pl.ANY)],
        out_specs=pl.BlockSpec((tm * nsub, LANES), lambda i, te, nv: (i, 0)),
        scratch_shapes=[pltpu.VMEM((2, 3, d, f), F32),
                        pltpu.VMEM((d, f), BF16), pltpu.VMEM((d, f), BF16), pltpu.VMEM((f, d), BF16),
                        pltpu.SMEM((1,), jnp.int32), pltpu.SemaphoreType.DMA((2,))],
    )
    return pl.pallas_call(
        _moe_kernel,
        out_shape=jax.ShapeDtypeStruct((cap * nsub, LANES), F32),
        grid_spec=grid_spec,
        compiler_params=_params(("arbitrary",), 56),
    )(tile_expert, n_valid, xs, b_gate.reshape(e, 1, f), b_up.reshape(e, 1, f),
      b_down.reshape(e, 1, d), w_gate, w_up, w_down)


def _combine_kernel(pos_ref, posn_ref, w_ref, x1_ref, mod_ref, g_ref, b_ref, y_ref, o_ref,
                    buf, sem, *, tm):
    i = pl.program_id(0)
    n = pl.num_programs(0)
    nsub = x1_ref.shape[1] // LANES

    group = COMBINE_GROUP

    def issue_rows(p_ref, slot, r0):
        for r in range(group):
            for k in range(TOP_K):
                row = r0 + r
                src = pl.ds(pl.multiple_of(p_ref[row * TOP_K + k] * nsub, nsub), nsub)
                dst = pl.ds(pl.multiple_of(row * nsub, nsub), nsub)
                pltpu.make_async_copy(y_ref.at[src], buf.at[slot, k, dst],
                                      sem.at[slot]).start(priority=k % 2)

    def reduce_rows(slot, r0):
        rows = pl.ds(r0, group)
        w = w_ref[rows, :]
        f = jnp.zeros((group, nsub * LANES), F32)
        for k in range(TOP_K):
            yk = jnp.concatenate(
                [buf[slot, k, pl.ds(r0 * nsub + s, group, stride=nsub), :] for s in range(nsub)],
                axis=1)
            f = f + w[:, k:k + 1] * yk
        o_ref[rows, :] = f

    @pl.when(i == 0)
    def _():
        def body(j, c):
            issue_rows(pos_ref, 0, pl.multiple_of(j * group, group))
            return c
        lax.fori_loop(0, tm // group, body, 0)

    slot = i % 2
    for k in range(TOP_K):
        pltpu.make_async_copy(y_ref.at[pl.ds(0, tm * nsub)], buf.at[slot, k], sem.at[slot]).wait()

    @pl.when(i + 1 < n)
    def _():
        def body(j, c):
            r0 = pl.multiple_of(j * group, group)
            issue_rows(posn_ref, 1 - slot, r0)
            reduce_rows(slot, r0)
            return c
        lax.fori_loop(0, tm // group, body, 0)

    @pl.when(i + 1 == n)
    def _():
        def body(j, c):
            reduce_rows(slot, pl.multiple_of(j * group, group))
            return c
        lax.fori_loop(0, tm // group, body, 0)

    o_ref[...] = _layer_norm(DEEPNORM_ALPHA * x1_ref[...] + mod_ref[0, 5:6, :] * o_ref[...],
                             g_ref[...], b_ref[...])


def _combine(pos_flat, w, x1, mod3, g2, b2, y, seq):
    n, d = x1.shape
    tm = min(TM_ROWS, seq)
    nsteps = n // tm
    per_seq = seq // tm
    return pl.pallas_call(
        functools.partial(_combine_kernel, tm=tm),
        out_shape=jax.ShapeDtypeStruct((n, d), F32),
        grid=(nsteps,),
        in_specs=[pl.BlockSpec((tm * TOP_K,), lambda i: (i,), memory_space=pltpu.SMEM),
                  pl.BlockSpec((tm * TOP_K,), lambda i: (jnp.minimum(i + 1, nsteps - 1),),
                               memory_space=pltpu.SMEM),
                  pl.BlockSpec((tm, LANES), lambda i: (i, 0)),
                  pl.BlockSpec((tm, d), lambda i: (i, 0)),
                  pl.BlockSpec((1, 6, d), lambda i: (i // per_seq, 0, 0)),
                  pl.BlockSpec(g2.shape, lambda i: (0, 0)),
                  pl.BlockSpec(b2.shape, lambda i: (0, 0)),
                  pl.BlockSpec(memory_space=pl.ANY)],
        out_specs=pl.BlockSpec((tm, d), lambda i: (i, 0)),
        scratch_shapes=[pltpu.VMEM((2, TOP_K, tm * (d // LANES), LANES), F32),
                        pltpu.SemaphoreType.DMA((2,))],
        compiler_params=_params(("arbitrary",), 48),
    )(pos_flat, pos_flat, w, x1, mod3, g2, b2, y)


def _prep_w_uq(w_uq):
    r = w_uq.shape[0]
    w = w_uq.reshape(r, B_HEADS, B_QK_DIM)
    half = B_ROPE_DIM // 2
    rot = jnp.concatenate([-w[:, :, B_NOPE_DIM + half:], w[:, :, B_NOPE_DIM:B_NOPE_DIM + half]],
                          axis=-1).reshape(r, B_HEADS * B_ROPE_DIM)
    w = jnp.pad(w, ((0, 0), (0, 0), (0, LANES - B_QK_DIM))).reshape(r, B_HEADS * LANES)
    return jnp.concatenate([w, rot], axis=1).astype(BF16)


def _prep_w_ukv(w_ukv):
    r = w_ukv.shape[0]
    w = w_ukv.reshape(r, B_HEADS, B_NOPE_DIM + B_V_DIM)
    wk = jnp.pad(w[:, :, :B_NOPE_DIM], ((0, 0), (0, 0), (0, LANES - B_NOPE_DIM)))
    wv = w[:, :, B_NOPE_DIM:].reshape(r, B_HEADS // 2, 2 * B_V_DIM)
    wv = jnp.pad(wv, ((0, 0), (0, 0), (0, V_PAIR_LANES - 2 * B_V_DIM)))
    return (wk.reshape(r, B_HEADS * LANES).astype(BF16),
            wv.reshape(r, B_HEADS // 2 * V_PAIR_LANES).astype(BF16))


def _rope_tables(positions):
    half = B_ROPE_DIM // 2
    freqs = ROPE_THETA ** (-jnp.arange(half, dtype=F32) / half)
    ang = positions.astype(F32).reshape(1, -1) * freqs[:, None]
    cos, sin = jnp.cos(ang), jnp.sin(ang)
    z = jnp.zeros((B_ROPE_DIM, ang.shape[1]), F32)
    return jnp.concatenate([sin, sin, z, cos, cos, z], 0)


def kernel(x, c, positions, w_ada, b_ada, w_in, rms_q, w_uq, rms_kv, w_ukv, rel_bias,
           w_branch_a, w_branch_b, w_out, ln1_g, ln1_b, w_router, b_router,
           w_gate, b_gate, w_up, b_up, w_down, b_down, ln2_g, ln2_b):
    bsz, seq, d = x.shape
    n_tok = bsz * seq
    assert w_ada.shape[0] == DEPTH == 1
    row = lambda v: v.reshape(1, -1)

    mod3 = _ada(c, w_ada[0], b_ada[0]).reshape(bsz, 6, d)

    qk, va, cq, ckv, kr, ga, gb = _inproj(x, mod3, jnp.swapaxes(w_in, 1, 2))
    ya = _attn_a(qk, va, _bias_table(rel_bias[0], min(TQ_A, seq)))
    wk, wv = _prep_w_ukv(w_ukv[0])
    qm, km, vm = _mla_proj(cq, ckv, kr, _rope_tables(positions), row(rms_q[0]), row(rms_kv[0]),
                           _prep_w_uq(w_uq[0]), wk, wv)
    yb = _mla_attn(qm, km, vm)

    wr = jnp.pad(w_router[0], ((0, 0), (0, LANES - N_EXPERTS)))
    br = jnp.pad(b_router[0], (0, LANES - N_EXPERTS), constant_values=NEG_INF)
    x1, h2, code, wts, cnt = _merge(x, ya, yb, ga, gb, mod3,
                                    w_branch_a[0].astype(BF16), w_branch_b[0].astype(BF16),
                                    w_out[0].astype(BF16), row(ln1_g[0]), row(ln1_b[0]), wr, row(br))
    wts = wts.reshape(n_tok, LANES)

    tm = TM_MOE
    n_tiles = -(-(n_tok * TOP_K) // tm) + N_EXPERTS
    assert n_tiles <= MAX_TILES_PAD
    pos, tmap, meta = _route(code.reshape(n_tok, LANES), cnt, tm)
    pos_flat = pos[:, :TOP_K].reshape(-1)
    fill = meta[:3].reshape(-1)
    nsub = d // LANES
    xs = _dispatch(pos_flat, fill, h2.reshape(n_tok * nsub, LANES), n_tiles * tm, nsub, tm)
    y = _moe(tmap[:n_tiles, 0], meta[0, :1], xs, w_gate[0], b_gate[0], w_up[0], b_up[0],
             w_down[0], b_down[0], tm)
    out = _combine(pos_flat, wts, x1.reshape(n_tok, d), mod3, row(ln2_g[0]), row(ln2_b[0]), y, seq)
    return out.reshape(bsz, seq, d)
```

```python
import functools

import jax
import jax.numpy as jnp
from jax import lax
from jax.experimental import pallas as pl
from jax.experimental.pallas import tpu as pltpu

F32 = jnp.float32
BF16 = jnp.bfloat16

CHUNK = 64
A_HEADS = 8
A_HEAD_DIM = 64
A_LEFT_CHUNKS = 8
A_MAX_REL = 128
A_WIDTH = A_HEADS * A_HEAD_DIM
B_HEADS = 8
B_NOPE_DIM = 64
B_ROPE_DIM = 32
B_V_DIM = 64
B_Q_LORA = 384
B_KV_LORA = 256
B_QK_DIM = B_NOPE_DIM + B_ROPE_DIM
ROPE_THETA = 10000.0
N_EXPERTS = 32
TOP_K = 4
SWIGLU_LIMIT = 7.0
SWIGLU_ALPHA = 1.702
DEPTH = 1
DEEPNORM_ALPHA = (2.0 * DEPTH) ** 0.25
LN_EPS = 1e-5
RMS_EPS = 1e-6
NEG_INF = -1e30
LOG2_E = 1.4426950408889634

LANES = 128
MXU_COLS = 256
V_PAIR_LANES = MXU_COLS

TM_PROJ = 512
TM_MLA_PROJ = 1024
TQ_A = 256
A_QBLOCKS = 8
TQ_B = 512
TM_MERGE = 1024
MERGE_SPLIT = 2
TR_ROUTE = 1024
TM_MOE = 512
TM_DISPATCH = 1024
TM_ROWS = 512
COMBINE_GROUP = 8
MAX_TILES_PAD = 512


def _params(sem, vmem_mb=None):
    return pltpu.CompilerParams(
        dimension_semantics=sem,
        vmem_limit_bytes=None if vmem_mb is None else vmem_mb << 20)


def _dot(a, b):
    return jnp.dot(a, b, preferred_element_type=F32)


def _dot_nt(a, b):
    return lax.dot_general(a, b, (((1,), (1,)), ((), ())), preferred_element_type=F32)


def _store_token_tiles(ref, x):
    rows, d = x.shape
    nsub = d // LANES
    for s in range(nsub):
        ref[pl.ds(s, rows, stride=nsub), :] = x[:, s * LANES:(s + 1) * LANES]


def _load_token_tiles(ref, nsub):
    rows = ref.shape[0] // nsub
    return jnp.concatenate([ref[pl.ds(s, rows, stride=nsub), :] for s in range(nsub)], axis=1)


def _layer_norm(x, g, b):
    mu = jnp.mean(x, axis=-1, keepdims=True)
    xc = x - mu
    var = jnp.mean(xc * xc, axis=-1, keepdims=True)
    return xc * lax.rsqrt(var + LN_EPS) * g + b


def _ada_kernel(c_ref, w_ref, b_ref, o_ref):
    c = c_ref[...]
    sc = c * jax.nn.sigmoid(c)
    w = w_ref[...]
    s_hi, w_hi = sc.astype(BF16), w.astype(BF16)
    s_lo = (sc - s_hi.astype(F32)).astype(BF16)
    w_lo = (w - w_hi.astype(F32)).astype(BF16)
    o_ref[...] = _dot(s_hi, w_hi) + _dot(s_hi, w_lo) + _dot(s_lo, w_hi) + b_ref[...]


def _ada(c, w_ada, b_ada):
    bsz, d = c.shape
    m = w_ada.shape[1]
    tn = 1024
    return pl.pallas_call(
        _ada_kernel,
        out_shape=jax.ShapeDtypeStruct((bsz, m), F32),
        grid=(m // tn,),
        in_specs=[pl.BlockSpec((bsz, d), lambda j: (0, 0)),
                  pl.BlockSpec((d, tn), lambda j: (0, j)),
                  pl.BlockSpec((1, tn), lambda j: (0, j))],
        out_specs=pl.BlockSpec((bsz, tn), lambda j: (0, j)),
        compiler_params=_params(("arbitrary",)),
    )(c, w_ada, b_ada.reshape(1, m))


def _inproj_kernel(x_ref, mod_ref, win_hbm, qk_ref, va_ref, cq_ref, ckv_ref, kr_ref, ga_ref, gb_ref,
                   wf_s, wqk_ref, wv_ref, wcq_ref, wckv_ref, wkr_ref, wga_ref, wgb_ref, sem):
    @pl.when((pl.program_id(0) == 0) & (pl.program_id(1) == 0))
    def _():
        copy = pltpu.make_async_copy(win_hbm.at[0], wf_s, sem)
        copy.start()
        copy.wait()
        d = wf_s.shape[1]
        chunk = 2 * LANES
        off = 0
        for dst, n in ((wqk_ref, 2 * A_WIDTH), (wv_ref, A_WIDTH), (wcq_ref, B_Q_LORA),
                       (wckv_ref, B_KV_LORA), (None, B_ROPE_DIM), (wga_ref, d), (wgb_ref, d)):
            if dst is None:
                blk = wf_s[off:off + LANES, :].T
                lane = lax.broadcasted_iota(jnp.int32, blk.shape, 1)
                blk = pltpu.roll(jnp.where(lane < n, blk, 0.0), B_NOPE_DIM, 1)
                wkr_ref[...] = blk.astype(BF16)
            else:
                for c0 in range(0, n, chunk):
                    cn = min(chunk, n - c0)
                    blk = wf_s[off + c0:off + c0 + cn, :].T
                    if dst is wqk_ref and c0 < A_WIDTH:
                        blk = blk * (A_HEAD_DIM ** -0.5 * LOG2_E)
                    dst[:, c0:c0 + cn] = blk.astype(BF16)
            off += n

    m = mod_ref[0]
    h = (x_ref[0] * (1.0 + m[1:2]) + m[0:1]).astype(BF16)
    for w_ref, ref in ((wqk_ref, qk_ref), (wcq_ref, cq_ref), (wckv_ref, ckv_ref),
                       (wkr_ref, kr_ref), (wga_ref, ga_ref), (wgb_ref, gb_ref)):
        ref[0] = _dot(h, w_ref[...]).astype(ref.dtype)
    v = _dot(h, wv_ref[...]).astype(BF16)
    ones_blk = jnp.ones((v.shape[0], LANES), BF16)
    parts = []
    for hp in range(A_WIDTH // LANES):
        parts += [v[:, hp * LANES:(hp + 1) * LANES], ones_blk]
    va_ref[0] = jnp.concatenate(parts, axis=1)


def _inproj(x, mod3, w_in):
    bsz, seq, d = x.shape
    tm = min(TM_PROJ, seq)
    widths = (2 * A_WIDTH, A_WIDTH // LANES * V_PAIR_LANES, B_Q_LORA, B_KV_LORA, LANES, d, d)
    dtypes = (BF16, BF16, BF16, BF16, F32, BF16, BF16)
    w_widths = (2 * A_WIDTH, A_WIDTH, B_Q_LORA, B_KV_LORA, LANES, d, d)
    assert w_in.shape[0] == 1 and w_in.shape[1] == sum(w_widths) - LANES + B_ROPE_DIM
    return pl.pallas_call(
        _inproj_kernel,
        out_shape=[jax.ShapeDtypeStruct((bsz, seq, n), dt) for n, dt in zip(widths, dtypes)],
        grid=(bsz, seq // tm),
        in_specs=[pl.BlockSpec((1, tm, d), lambda b, i: (b, i, 0)),
                  pl.BlockSpec((1, 6, d), lambda b, i: (b, 0, 0)),
                  pl.BlockSpec(memory_space=pl.ANY)],
        out_specs=[pl.BlockSpec((1, tm, n), lambda b, i: (b, i, 0)) for n in widths],
        scratch_shapes=[pltpu.VMEM(w_in.shape[1:], F32)]
                       + [pltpu.VMEM((d, n), BF16) for n in w_widths]
                       + [pltpu.SemaphoreType.DMA(())],
        compiler_params=_params(("arbitrary", "arbitrary"), 56),
    )(x, mod3, w_in)


def _attn_a_kernel(q_ref, k_ref, v_ref, bias_ref, o_ref, *, tq, nd, nblk, steps):
    band = nd * tq
    lane = lax.broadcasted_iota(jnp.int32, (1, LANES), 1)

    def shifted(base, missing):
        cut = missing * tq
        if cut == 0:
            return base
        return jnp.concatenate([base[:, cut:], jnp.full((tq, cut), NEG_INF, F32)], axis=1)

    def bias_for(hh, i):
        base = bias_ref[hh]
        if isinstance(i, int):
            return shifted(base, max(nd - 1 - i, 0))
        out = base
        for first in range(nd - 1):
            out = jnp.where(i == first, shifted(base, nd - 1 - first), out)
        return out

    for sub in range(nblk):
        if steps == 1:
            i = sub
            rows = slice(max(i - (nd - 1), 0) * tq, max(i - (nd - 1), 0) * tq + band)
        else:
            i = pl.program_id(2) * nblk + sub
            rows = pl.ds(pl.multiple_of(jnp.maximum(i - (nd - 1), 0) * tq, tq), band)
        q = q_ref[0, sub * tq:(sub + 1) * tq, :]
        k = k_ref[0, rows, :]
        v = v_ref[0, rows, :]
        zero = jnp.zeros_like(k)
        kk = jnp.concatenate([jnp.where(lane < A_HEAD_DIM, k, zero),
                              jnp.where(lane >= A_HEAD_DIM, k, zero)], axis=0)
        s = _dot_nt(q, kk)
        ps = []
        for hh in range(2):
            sh = s[:, hh * band:(hh + 1) * band] + bias_for(hh, i)
            m = sh.max(axis=-1, keepdims=True)
            ps.append(jnp.exp2(sh - m).astype(BF16))
        acc = _dot(jnp.concatenate(ps, axis=0), v)
        out = acc[:, :LANES] / acc[:, LANES:]
        o_ref[0, sub * tq:(sub + 1) * tq, :] = jnp.where(
            lane < A_HEAD_DIM, out[:tq], out[tq:]).astype(o_ref.dtype)


def _attn_a(qk, va, bias_tab):
    bsz, seq, _ = qk.shape
    tq = min(TQ_A, seq)
    nd = bias_tab.shape[2] // tq
    assert seq >= nd * tq
    ncb = A_WIDTH // LANES
    nblk = min(A_QBLOCKS, seq // tq)
    steps = seq // (nblk * tq)
    return pl.pallas_call(
        functools.partial(_attn_a_kernel, tq=tq, nd=nd, nblk=nblk, steps=steps),
        out_shape=jax.ShapeDtypeStruct((bsz, seq, A_WIDTH), BF16),
        grid=(ncb, bsz, steps),
        in_specs=[pl.BlockSpec((1, nblk * tq, LANES), lambda hp, b, i: (b, i, hp)),
                  pl.BlockSpec((1, seq, LANES), lambda hp, b, i: (b, 0, ncb + hp)),
                  pl.BlockSpec((1, seq, V_PAIR_LANES), lambda hp, b, i: (b, 0, hp)),
                  pl.BlockSpec((2, tq, nd * tq), lambda hp, b, i: (hp, 0, 0))],
        out_specs=pl.BlockSpec((1, nblk * tq, LANES), lambda hp, b, i: (b, i, hp)),
        compiler_params=_params(("arbitrary",) * 3, 40),
    )(qk, qk, va, bias_tab)


def _bias_table(rel_bias, tq):
    nd = (A_LEFT_CHUNKS * CHUNK + tq - 1) // tq + 1
    width = nd * tq
    period = width + tq
    u = jnp.arange(period)
    dist = jnp.where(u < width, (nd - 1) * tq - u, (nd - 1) * tq + period - u)
    vec = rel_bias[:, jnp.clip(dist, -A_MAX_REL, A_MAX_REL) + A_MAX_REL].astype(F32)
    toep = jnp.tile(vec, (1, tq))[:, :tq * (period - 1)].reshape(-1, tq, period - 1)[:, :, :width]
    qi = jnp.arange(tq)[:, None]
    x = jnp.arange(width)[None, :]
    dchunk = qi // CHUNK - x // CHUNK + (nd - 1) * (tq // CHUNK)
    valid = (dchunk >= 0) & (dchunk <= A_LEFT_CHUNKS)
    return jnp.where(valid[None], toep * LOG2_E, NEG_INF)


def _mla_proj_kernel(cq_ref, ckv_ref, kr_ref, tab_ref, rq_ref, rkv_ref,
                     wq_ref, wk_ref, wv_ref, q_out, k_out, v_out):
    def rms(x, g):
        xf = x.astype(F32)
        ms = jnp.mean(xf * xf, axis=-1, keepdims=True)
        return (xf * lax.rsqrt(ms + RMS_EPS) * g).astype(BF16)

    cqn = rms(cq_ref[0], rq_ref[...])
    ckvn = rms(ckv_ref[0], rkv_ref[...])
    q = _dot(cqn, wq_ref[...])
    kn = _dot(ckvn, wk_ref[...])
    vlane = lax.broadcasted_iota(jnp.int32, (1, v_out.shape[-1]), 1)
    ones_lane = jnp.where(vlane % V_PAIR_LANES >= 2 * B_V_DIM, 1.0, 0.0)
    v_out[0] = (_dot(ckvn, wv_ref[...]) + ones_lane).astype(v_out.dtype)
    tab = tab_ref[...].T
    tlane = lax.broadcasted_iota(jnp.int32, tab.shape, 1)
    rope_end = B_NOPE_DIM + B_ROPE_DIM
    in_rope = (tlane >= B_NOPE_DIM) & (tlane < rope_end)
    cos_t = jnp.where(tlane < B_NOPE_DIM, 1.0, jnp.where(in_rope, tab, 0.0))
    sin_t = jnp.where(in_rope, pltpu.roll(tab, B_NOPE_DIM, 1), 0.0)
    first = tlane < B_NOPE_DIM + B_ROPE_DIM // 2
    sin_a = jnp.where(first, -sin_t, 0.0)
    sin_b = jnp.where(first, 0.0, sin_t)

    def rope(x, tables):
        c, sa, sb = tables
        return (x * c + pltpu.roll(x, LANES - B_ROPE_DIM // 2, 1) * sa
                + pltpu.roll(x, B_ROPE_DIM // 2, 1) * sb)

    kpe = rope(kr_ref[0], (cos_t, sin_a, sin_b))
    scale = B_QK_DIM ** -0.5 * LOG2_E
    cos_q, sin_q = cos_t * scale, sin_t * scale
    nq = B_HEADS * LANES
    per_block = LANES // B_ROPE_DIM
    for h in range(B_HEADS):
        sl = slice(h * LANES, (h + 1) * LANES)
        blk = nq + (h // per_block) * LANES
        partner = q[:, blk:blk + LANES]
        shift = (B_NOPE_DIM - B_ROPE_DIM * (h % per_block)) % LANES
        if shift:
            partner = pltpu.roll(partner, shift, 1)
        q_out[0, :, sl] = (q[:, sl] * cos_q + partner * sin_q).astype(q_out.dtype)
        k_out[0, :, sl] = (kn[:, sl] + kpe).astype(k_out.dtype)


def _mla_proj(cq, ckv, kr, rope_tab, rms_q, rms_kv, wq, wk, wv):
    bsz, seq, _ = cq.shape
    tm = min(TM_MLA_PROJ, seq)
    tok = lambda n: pl.BlockSpec((1, tm, n), lambda b, i: (b, i, 0))
    full = lambda a: pl.BlockSpec(a.shape, lambda b, i: (0,) * a.ndim)
    return pl.pallas_call(
        _mla_proj_kernel,
        out_shape=[jax.ShapeDtypeStruct((bsz, seq, B_HEADS * LANES), BF16),
                   jax.ShapeDtypeStruct((bsz, seq, B_HEADS * LANES), BF16),
                   jax.ShapeDtypeStruct((bsz, seq, wv.shape[1]), BF16)],
        grid=(bsz, seq // tm),
        in_specs=[tok(B_Q_LORA), tok(B_KV_LORA), tok(LANES),
                  pl.BlockSpec((LANES, tm), lambda b, i: (0, b * (seq // tm) + i)),
                  full(rms_q), full(rms_kv), full(wq), full(wk), full(wv)],
        out_specs=[tok(B_HEADS * LANES), tok(B_HEADS * LANES), tok(wv.shape[1])],
        compiler_params=_params(("arbitrary", "arbitrary"), 32),
    )(cq, ckv, kr, rope_tab, rms_q, rms_kv, wq, wk, wv)


def _mla_attn_kernel(q_ref, k_ref, v_ref, o_ref, s_a, s_b, m_s, acc_s, *, t):
    lane = lax.broadcasted_iota(jnp.int32, (1, LANES), 1)
    nq = q_ref.shape[1] // t
    pairs = [(qi, kj) for qi in range(nq) for kj in range(qi + 1)]
    bufs = (s_a, s_b)

    def produce(qi, kj, s_ref):
        for hh in range(2):
            q = q_ref[0, qi * t:(qi + 1) * t, hh * LANES:(hh + 1) * LANES]
            k = k_ref[0, kj * t:(kj + 1) * t, hh * LANES:(hh + 1) * LANES]
            s_ref[hh] = _dot_nt(q, k)

    def consume(qi, kj, s_ref):
        v = v_ref[0, kj * t:(kj + 1) * t, :]
        outs = []
        for hh in range(2):
            s = s_ref[hh]
            if kj == qi:
                row_chunk = lax.broadcasted_iota(jnp.int32, (t, t), 0) // CHUNK
                col_chunk = lax.broadcasted_iota(jnp.int32, (t, t), 1) // CHUNK
                s = jnp.where(col_chunk <= row_chunk, s, NEG_INF)
            row_max = jnp.broadcast_to(s.max(axis=-1, keepdims=True), (t, LANES))
            if kj == 0:
                m_new = row_max
                acc = _dot(jnp.exp2(s - jnp.tile(m_new, (1, t // LANES))).astype(BF16), v)
            else:
                m_old = m_s[hh]
                m_new = jnp.maximum(m_old, row_max)
                p = jnp.exp2(s - jnp.tile(m_new, (1, t // LANES))).astype(BF16)
                alpha = jnp.exp2(m_old - m_new)
                acc = jnp.tile(alpha, (1, V_PAIR_LANES // LANES)) * acc_s[hh] + _dot(p, v)
            if kj == qi:
                outs.append(acc[:, :LANES] / acc[:, LANES:])
            else:
                acc_s[hh] = acc
                m_s[hh] = m_new
        if kj == qi:
            o_ref[0, qi * t:(qi + 1) * t, :] = jnp.where(
                lane < B_V_DIM, outs[0], outs[1]).astype(o_ref.dtype)

    produce(*pairs[0], bufs[0])
    for n, pair in enumerate(pairs):
        if n + 1 < len(pairs):
            produce(*pairs[n + 1], bufs[(n + 1) % 2])
        consume(*pair, bufs[n % 2])


def _mla_attn(qm, km, vm):
    bsz, seq, _ = qm.shape
    t = min(TQ_B, seq)
    return pl.pallas_call(
        functools.partial(_mla_attn_kernel, t=t),
        out_shape=jax.ShapeDtypeStruct((bsz, seq, B_HEADS * B_V_DIM), BF16),
        grid=(bsz, B_HEADS // 2),
        in_specs=[pl.BlockSpec((1, seq, 2 * LANES), lambda b, hp: (b, 0, hp)),
                  pl.BlockSpec((1, seq, 2 * LANES), lambda b, hp: (b, 0, hp)),
                  pl.BlockSpec((1, seq, V_PAIR_LANES), lambda b, hp: (b, 0, hp))],
        out_specs=pl.BlockSpec((1, seq, LANES), lambda b, hp: (b, 0, hp)),
        scratch_shapes=[pltpu.VMEM((2, t, t), F32), pltpu.VMEM((2, t, t), F32),
                        pltpu.VMEM((2, t, LANES), F32),
                        pltpu.VMEM((2, t, V_PAIR_LANES), F32)],
        compiler_params=_params(("arbitrary",) * 2, 40),
    )(qm, km, vm)


def _merge_kernel(x_ref, ya_ref, yb_ref, ga_ref, gb_ref, mod_ref, wa_ref, wb_ref, wo_ref,
                  g1_ref, b1_ref, wrh_ref, wrl_ref, br_ref, x1_ref, h2_ref, code_ref, wts_ref,
                  cnt_ref):
    @pl.when((pl.program_id(0) == 0) & (pl.program_id(1) == 0))
    def _():
        cnt_ref[...] = jnp.zeros_like(cnt_ref)

    tm = x_ref.shape[1]
    half = tm // MERGE_SPLIT
    for part in range(MERGE_SPLIT):
        _merge_rows(slice(part * half, (part + 1) * half), x_ref, ya_ref, yb_ref, ga_ref, gb_ref,
                    mod_ref, wa_ref, wb_ref, wo_ref, g1_ref, b1_ref, wrh_ref, wrl_ref, br_ref,
                    x1_ref, h2_ref, code_ref, wts_ref, cnt_ref)


def _merge_rows(rows, x_ref, ya_ref, yb_ref, ga_ref, gb_ref, mod_ref, wa_ref, wb_ref, wo_ref,
                g1_ref, b1_ref, wrh_ref, wrl_ref, br_ref, x1_ref, h2_ref, code_ref, wts_ref,
                cnt_ref):
    m = mod_ref[0]
    a = _dot(ya_ref[0, rows, :], wa_ref[...])
    b = _dot(yb_ref[0, rows, :], wb_ref[...])
    merged = (jax.nn.sigmoid(ga_ref[0, rows, :].astype(F32)) * a
              + jax.nn.sigmoid(gb_ref[0, rows, :].astype(F32)) * b)
    o = _dot(merged.astype(BF16), wo_ref[...])
    x1 = _layer_norm(DEEPNORM_ALPHA * x_ref[0, rows, :] + m[2:3] * o, g1_ref[...], b1_ref[...])
    h2 = x1 * (1.0 + m[4:5]) + m[3:4]
    x1_ref[0, rows, :] = x1
    nsub = h2.shape[1] // LANES
    _store_token_tiles(h2_ref.at[0, pl.ds(rows.start * nsub, (rows.stop - rows.start) * nsub)], h2)
    h_hi = h2.astype(BF16)
    h_lo = (h2 - h_hi.astype(F32)).astype(BF16)
    work = (_dot(h_hi, wrh_ref[...]) + _dot(h_hi, wrl_ref[...]) + _dot(h_lo, wrh_ref[...])
            + br_ref[...])
    lane = lax.broadcasted_iota(jnp.int32, work.shape, 1)
    code = jnp.zeros(work.shape, F32)
    vals = []
    for k in range(TOP_K):
        top = work.max(axis=-1, keepdims=True)
        idx = jnp.where(work == top, lane, LANES).min(axis=-1, keepdims=True)
        sel = lane == idx
        work = jnp.where(sel, -jnp.inf, work)
        code = jnp.where(sel, k + 1.0, code)
        vals.append(top)
    den = sum(jnp.exp(v - vals[0]) for v in vals)
    wts = jnp.zeros(work.shape, F32)
    for k in range(TOP_K):
        wts = jnp.where(lane == k, jnp.exp(vals[k] - vals[0]) / den, wts)
    code_ref[0, rows, :] = code
    wts_ref[0, rows, :] = wts
    picked = jnp.where(code > 0.0, 1.0, 0.0)
    cnt_ref[...] += jnp.broadcast_to(picked.sum(axis=0, keepdims=True), cnt_ref.shape)


def _merge(x, ya, yb, ga, gb, mod3, wa, wb, wo, g1, b1, wr, br):
    wr_hi = wr.astype(BF16)
    wr_lo = (wr - wr_hi.astype(F32)).astype(BF16)
    bsz, seq, d = x.shape
    tm = min(TM_MERGE, seq)
    tok = lambda n: pl.BlockSpec((1, tm, n), lambda b, i: (b, i, 0))
    full = lambda a: pl.BlockSpec(a.shape, lambda b, i: (0,) * a.ndim)
    return pl.pallas_call(
        _merge_kernel,
        out_shape=[jax.ShapeDtypeStruct((bsz, seq, d), F32),
                   jax.ShapeDtypeStruct((bsz, seq * (d // LANES), LANES), F32),
                   jax.ShapeDtypeStruct((bsz, seq, LANES), F32),
                   jax.ShapeDtypeStruct((bsz, seq, LANES), F32),
                   jax.ShapeDtypeStruct((8, LANES), F32)],
        grid=(bsz, seq // tm),
        in_specs=[tok(d), tok(A_WIDTH), tok(B_HEADS * B_V_DIM), tok(d), tok(d),
                  pl.BlockSpec((1, 6, d), lambda b, i: (b, 0, 0)),
                  full(wa), full(wb), full(wo), full(g1), full(b1), full(wr_hi), full(wr_lo),
                  full(br)],
        out_specs=[tok(d),
                   pl.BlockSpec((1, tm * (d // LANES), LANES), lambda b, i: (b, i, 0)),
                   tok(LANES), tok(LANES), pl.BlockSpec((8, LANES), lambda b, i: (0, 0))],
        compiler_params=_params(("arbitrary", "arbitrary"), 48),
    )(x, ya, yb, ga, gb, mod3, wa, wb, wo, g1, b1, wr_hi, wr_lo, br)


def _route_kernel(code_ref, cnt_ref, pos_ref, tmap_ref, nv_ref, carry_s, pstart_s, *, tr, tm):
    i = pl.program_id(0)
    lane = lax.broadcasted_iota(jnp.int32, (tr, LANES), 1)
    code = code_ref[...]
    picked = jnp.where(code > 0.0, 1.0, 0.0)
    tile_count = jnp.broadcast_to(picked.sum(axis=0, keepdims=True), (8, LANES))

    @pl.when(i == 0)
    def _():
        lane8 = lax.broadcasted_iota(jnp.int32, (8, LANES), 1)
        padded = jnp.floor((cnt_ref[...] + (tm - 1)) * (1.0 / tm)) * tm
        cum = padded
        shift = 1
        while shift < N_EXPERTS:
            cum = cum + jnp.where(lane8 >= shift, pltpu.roll(cum, shift, 1), 0.0)
            shift *= 2
        pstart_s[...] = cum - padded
        carry_s[...] = jnp.zeros_like(carry_s)
        nt = tmap_ref.shape[0]
        tile_start = (lax.broadcasted_iota(jnp.int32, (nt, LANES), 0) * tm).astype(F32)
        lane_t = lax.broadcasted_iota(jnp.int32, (nt, LANES), 1)
        done = jnp.where((lane_t < N_EXPERTS) & (cum[0:1] <= tile_start), 1.0, 0.0)
        te = jnp.minimum(done.sum(axis=-1, keepdims=True), N_EXPERTS - 1.0)
        tmap_ref[...] = jnp.broadcast_to(te, (nt, LANES)).astype(jnp.int32)
        total = jnp.where(lane8 == N_EXPERTS - 1, cum, 0.0).sum(axis=-1, keepdims=True)
        row8 = lax.broadcasted_iota(jnp.int32, (8, LANES), 0)
        meta = jnp.where(row8 == 0, jnp.broadcast_to(total * (1.0 / tm), (8, LANES)),
                         jnp.where(row8 == 1, cum - padded + cnt_ref[...], cum))
        nv_ref[...] = meta.astype(jnp.int32)

    r = lax.broadcasted_iota(jnp.int32, (tr, tr), 0)
    c = lax.broadcasted_iota(jnp.int32, (tr, tr), 1)
    tri = jnp.where(c < r, 1.0, 0.0).astype(BF16)
    base = _dot(tri, picked.astype(BF16)) + (carry_s[0:1] + pstart_s[0:1])
    carry_s[...] += tile_count
    pos = jnp.zeros((tr, LANES), F32)
    for k in range(TOP_K):
        pk = jnp.where(code == k + 1.0, base, 0.0).sum(axis=-1, keepdims=True)
        pos = jnp.where(lane == k, pk, pos)
    pos_ref[...] = pos.astype(jnp.int32)


def _route(code, cnt, tm):
    n = code.shape[0]
    tr = min(TR_ROUTE, n)
    return pl.pallas_call(
        functools.partial(_route_kernel, tr=tr, tm=tm),
        out_shape=[jax.ShapeDtypeStruct((n, LANES), jnp.int32),
                   jax.ShapeDtypeStruct((MAX_TILES_PAD, LANES), jnp.int32),
                   jax.ShapeDtypeStruct((8, LANES), jnp.int32)],
        grid=(n // tr,),
        in_specs=[pl.BlockSpec((tr, LANES), lambda i: (i, 0)),
                  pl.BlockSpec((8, LANES), lambda i: (0, 0))],
        out_specs=[pl.BlockSpec((tr, LANES), lambda i: (i, 0)),
                   pl.BlockSpec((MAX_TILES_PAD, LANES), lambda i: (0, 0)),
                   pl.BlockSpec((8, LANES), lambda i: (0, 0))],
        scratch_shapes=[pltpu.VMEM((8, LANES), F32)] * 2,
        compiler_params=_params(("arbitrary",)),
    )(code, cnt)


def _dispatch_kernel(pos_ref, fill_ref, h_ref, xs_ref, ztile_s, sem, fill_sem, tail_sem,
                     *, tm, nsub, tm_moe):
    def tile(ref, r):
        return ref.at[pl.ds(pl.multiple_of(r * nsub, nsub), nsub)]

    @pl.when(pl.program_id(0) == 0)
    def _():
        ztile_s[...] = jnp.zeros_like(ztile_s)

        def tail_copy(t):
            rows = tm_moe * nsub
            return pltpu.make_async_copy(
                ztile_s, xs_ref.at[pl.ds(pl.multiple_of(t * rows, rows), rows)], tail_sem)

        sizes = [1 << b for b in reversed(range((tm_moe - 1).bit_length()))]

        def fill_pieces(e, act):
            lo, hi = fill_ref[LANES + e], fill_ref[2 * LANES + e]
            at = lo
            for size in sizes:
                has = ((hi - lo) & size) != 0
                copy = pltpu.make_async_copy(
                    ztile_s.at[pl.ds(0, size * nsub)],
                    xs_ref.at[pl.ds(pl.multiple_of(at * nsub, nsub), size * nsub)], fill_sem)
                pl.when(has)(functools.partial(act, copy))
                at = at + jnp.where(has, size, 0)

        def fill_start(e, c):
            fill_pieces(e, lambda copy: copy.start())
            return c

        def fill_wait(e, c):
            fill_pieces(e, lambda copy: copy.wait())
            return c

        n_tiles = xs_ref.shape[0] // (tm_moe * nsub)

        def tail_start(t, c):
            tail_copy(t).start()
            return c

        def tail_drain(_, c):
            tail_copy(0).wait()
            return c

        lax.fori_loop(0, N_EXPERTS, fill_start, 0)
        lax.fori_loop(fill_ref[0], n_tiles, tail_start, 0)
        lax.fori_loop(0, N_EXPERTS, fill_wait, 0)
        lax.fori_loop(fill_ref[0], n_tiles, tail_drain, 0)

    def issue(j, c):
        for r in range(COMBINE_GROUP):
            row = pl.multiple_of(j * COMBINE_GROUP, COMBINE_GROUP) + r
            for k in range(TOP_K):
                pltpu.make_async_copy(tile(h_ref, row), tile(xs_ref, pos_ref[row * TOP_K + k]),
                                      sem).start(priority=k % 2)
        return c

    lax.fori_loop(0, tm // COMBINE_GROUP, issue, 0)
    for _ in range(TOP_K):
        pltpu.make_async_copy(h_ref, xs_ref.at[pl.ds(0, tm * nsub)], sem).wait()


def _dispatch(pos_flat, fill, h2, cap, nsub, tm_moe):
    n = h2.shape[0] // nsub
    tm = min(TM_DISPATCH, n)
    return pl.pallas_call(
        functools.partial(_dispatch_kernel, tm=tm, nsub=nsub, tm_moe=tm_moe),
        out_shape=jax.ShapeDtypeStruct((cap * nsub, LANES), F32),
        grid=(n // tm,),
        in_specs=[pl.BlockSpec((tm * TOP_K,), lambda i: (i,), memory_space=pltpu.SMEM),
                  pl.BlockSpec(fill.shape, lambda i: (0,), memory_space=pltpu.SMEM),
                  pl.BlockSpec((tm * nsub, LANES), lambda i: (i, 0))],
        out_specs=pl.BlockSpec(memory_space=pl.ANY),
        scratch_shapes=[pltpu.VMEM((tm_moe * nsub, LANES), F32),
                        pltpu.SemaphoreType.DMA(()), pltpu.SemaphoreType.DMA(()),
                        pltpu.SemaphoreType.DMA(())],
        compiler_params=_params(("arbitrary",)),
    )(pos_flat, fill, h2)


def _moe_kernel(te_ref, nv_ref, x_ref, bg_ref, bu_ref, bd_ref, wg_hbm, wu_hbm, wd_hbm,
                y_ref, wf_s, wg_s, wu_s, wd_s, slot_s, sem):
    i = pl.program_id(0)
    nv = nv_ref[0]
    last = te_ref.shape[0] - 1

    def weight_copies(e, slot):
        return [pltpu.make_async_copy(w.at[e], wf_s.at[slot, j], sem.at[slot])
                for j, w in enumerate((wg_hbm, wu_hbm, wd_hbm))]

    @pl.when(i < nv)
    def _():
        e = te_ref[i]

        @pl.when(i == 0)
        def _():
            slot_s[0] = 0
            for c in weight_copies(e, 0):
                c.start()

        @pl.when((i == 0) | (e != te_ref[jnp.maximum(i - 1, 0)]))
        def _():
            slot = slot_s[0]
            for c in weight_copies(e, slot):
                c.wait()
            nxt = lax.while_loop(lambda j: (j < nv) & (te_ref[jnp.minimum(j, last)] == e),
                                 lambda j: j + 1, i + 1)

            @pl.when(nxt < nv)
            def _():
                for c in weight_copies(te_ref[jnp.minimum(nxt, last)], 1 - slot):
                    c.start()

            wg_s[...] = wf_s[slot, 0].astype(BF16)
            wu_s[...] = wf_s[slot, 1].astype(BF16)
            wd_s[...] = wf_s[slot, 2].astype(BF16)
            slot_s[0] = 1 - slot

        xb = _load_token_tiles(x_ref, wg_s.shape[0] // LANES).astype(BF16)
        g = _dot(xb, wg_s[...]) + bg_ref[0]
        u = _dot(xb, wu_s[...]) + bu_ref[0]
        g = jnp.minimum(g, SWIGLU_LIMIT)
        u = jnp.clip(u, -SWIGLU_LIMIT, SWIGLU_LIMIT)
        h = (u + 1.0) * (g * jax.nn.sigmoid(SWIGLU_ALPHA * g))
        _store_token_tiles(y_ref, _dot(h.astype(BF16), wd_s[...]) + bd_ref[0])

    @pl.when(i >= nv)
    def _():
        y_ref[...] = jnp.zeros_like(y_ref)


def _moe(tile_expert, n_valid, xs, w_gate, b_gate, w_up, b_up, w_down, b_down, tm):
    e, d, f = w_gate.shape
    assert d == f
    nsub = d // LANES
    cap = xs.shape[0] // nsub
    n_tiles = cap // tm

    def tile(i, te, nv):
        return jnp.minimum(i, nv[0] - 1)

    exp3 = lambda i, te, nv: (te[tile(i, te, nv)], 0, 0)
    grid_spec = pltpu.PrefetchScalarGridSpec(
        num_scalar_prefetch=2,
        grid=(n_tiles,),
        in_specs=[pl.BlockSpec((tm * nsub, LANES), lambda i, te, nv: (tile(i, te, nv), 0)),
                  pl.BlockSpec((1, 1, f), exp3), pl.BlockSpec((1, 1, f), exp3),
                  pl.BlockSpec((1, 1, d), exp3),
                  pl.BlockSpec(memory_space=pl.ANY), pl.BlockSpec(memory_space=pl.ANY),
                  pl.BlockSpec(memory_space=pl.ANY)],
        out_specs=pl.BlockSpec((tm * nsub, LANES), lambda i, te, nv: (i, 0)),
        scratch_shapes=[pltpu.VMEM((2, 3, d, f), F32),
                        pltpu.VMEM((d, f), BF16), pltpu.VMEM((d, f), BF16), pltpu.VMEM((f, d), BF16),
                        pltpu.SMEM((1,), jnp.int32), pltpu.SemaphoreType.DMA((2,))],
    )
    return pl.pallas_call(
        _moe_kernel,
        out_shape=jax.ShapeDtypeStruct((cap * nsub, LANES), F32),
        grid_spec=grid_spec,
        compiler_params=_params(("arbitrary",), 56),
    )(tile_expert, n_valid, xs, b_gate.reshape(e, 1, f), b_up.reshape(e, 1, f),
      b_down.reshape(e, 1, d), w_gate, w_up, w_down)


def _combine_kernel(pos_ref, posn_ref, w_ref, x1_ref, mod_ref, g_ref, b_ref, y_ref, o_ref,
                    buf, sem, *, tm):
    i = pl.program_id(0)
    n = pl.num_programs(0)
    nsub = x1_ref.shape[1] // LANES

    group = COMBINE_GROUP

    def issue_rows(p_ref, slot, r0):
        for r in range(group):
            for k in range(TOP_K):
                row = r0 + r
                src = pl.ds(pl.multiple_of(p_ref[row * TOP_K + k] * nsub, nsub), nsub)
                dst = pl.ds(pl.multiple_of(row * nsub, nsub), nsub)
                pltpu.make_async_copy(y_ref.at[src], buf.at[slot, k, dst],
                                      sem.at[slot]).start(priority=k % 2)

    def reduce_rows(slot, r0):
        rows = pl.ds(r0, group)
        w = w_ref[rows, :]
        f = jnp.zeros((group, nsub * LANES), F32)
        for k in range(TOP_K):
            yk = jnp.concatenate(
                [buf[slot, k, pl.ds(r0 * nsub + s, group, stride=nsub), :] for s in range(nsub)],
                axis=1)
            f = f + w[:, k:k + 1] * yk
        o_ref[rows, :] = f

    @pl.when(i == 0)
    def _():
        def body(j, c):
            issue_rows(pos_ref, 0, pl.multiple_of(j * group, group))
            return c
        lax.fori_loop(0, tm // group, body, 0)

    slot = i % 2
    for k in range(TOP_K):
        pltpu.make_async_copy(y_ref.at[pl.ds(0, tm * nsub)], buf.at[slot, k], sem.at[slot]).wait()

    @pl.when(i + 1 < n)
    def _():
        def body(j, c):
            r0 = pl.multiple_of(j * group, group)
            issue_rows(posn_ref, 1 - slot, r0)
            reduce_rows(slot, r0)
            return c
        lax.fori_loop(0, tm // group, body, 0)

    @pl.when(i + 1 == n)
    def _():
        def body(j, c):
            reduce_rows(slot, pl.multiple_of(j * group, group))
            return c
        lax.fori_loop(0, tm // group, body, 0)

    o_ref[...] = _layer_norm(DEEPNORM_ALPHA * x1_ref[...] + mod_ref[0, 5:6, :] * o_ref[...],
                             g_ref[...], b_ref[...])


def _combine(pos_flat, w, x1, mod3, g2, b2, y, seq):
    n, d = x1.shape
    tm = min(TM_ROWS, seq)
    nsteps = n // tm
    per_seq = seq // tm
    return pl.pallas_call(
        functools.partial(_combine_kernel, tm=tm),
        out_shape=jax.ShapeDtypeStruct((n, d), F32),
        grid=(nsteps,),
        in_specs=[pl.BlockSpec((tm * TOP_K,), lambda i: (i,), memory_space=pltpu.SMEM),
                  pl.BlockSpec((tm * TOP_K,), lambda i: (jnp.minimum(i + 1, nsteps - 1),),
                               memory_space=pltpu.SMEM),
                  pl.BlockSpec((tm, LANES), lambda i: (i, 0)),
                  pl.BlockSpec((tm, d), lambda i: (i, 0)),
                  pl.BlockSpec((1, 6, d), lambda i: (i // per_seq, 0, 0)),
                  pl.BlockSpec(g2.shape, lambda i: (0, 0)),
                  pl.BlockSpec(b2.shape, lambda i: (0, 0)),
                  pl.BlockSpec(memory_space=pl.ANY)],
        out_specs=pl.BlockSpec((tm, d), lambda i: (i, 0)),
        scratch_shapes=[pltpu.VMEM((2, TOP_K, tm * (d // LANES), LANES), F32),
                        pltpu.SemaphoreType.DMA((2,))],
        compiler_params=_params(("arbitrary",), 48),
    )(pos_flat, pos_flat, w, x1, mod3, g2, b2, y)


def _prep_w_uq(w_uq):
    r = w_uq.shape[0]
    w = w_uq.reshape(r, B_HEADS, B_QK_DIM)
    half = B_ROPE_DIM // 2
    rot = jnp.concatenate([-w[:, :, B_NOPE_DIM + half:], w[:, :, B_NOPE_DIM:B_NOPE_DIM + half]],
                          axis=-1).reshape(r, B_HEADS * B_ROPE_DIM)
    w = jnp.pad(w, ((0, 0), (0, 0), (0, LANES - B_QK_DIM))).reshape(r, B_HEADS * LANES)
    return jnp.concatenate([w, rot], axis=1).astype(BF16)


def _prep_w_ukv(w_ukv):
    r = w_ukv.shape[0]
    w = w_ukv.reshape(r, B_HEADS, B_NOPE_DIM + B_V_DIM)
    wk = jnp.pad(w[:, :, :B_NOPE_DIM], ((0, 0), (0, 0), (0, LANES - B_NOPE_DIM)))
    wv = w[:, :, B_NOPE_DIM:].reshape(r, B_HEADS // 2, 2 * B_V_DIM)
    wv = jnp.pad(wv, ((0, 0), (0, 0), (0, V_PAIR_LANES - 2 * B_V_DIM)))
    return (wk.reshape(r, B_HEADS * LANES).astype(BF16),
            wv.reshape(r, B_HEADS // 2 * V_PAIR_LANES).astype(BF16))


def _rope_tables(positions):
    half = B_ROPE_DIM // 2
    freqs = ROPE_THETA ** (-jnp.arange(half, dtype=F32) / half)
    ang = positions.astype(F32).reshape(1, -1) * freqs[:, None]
    cos, sin = jnp.cos(ang), jnp.sin(ang)
    z = jnp.zeros((B_ROPE_DIM, ang.shape[1]), F32)
    return jnp.concatenate([sin, sin, z, cos, cos, z], 0)


def kernel(x, c, positions, w_ada, b_ada, w_in, rms_q, w_uq, rms_kv, w_ukv, rel_bias,
           w_branch_a, w_branch_b, w_out, ln1_g, ln1_b, w_router, b_router,
           w_gate, b_gate, w_up, b_up, w_down, b_down, ln2_g, ln2_b):
    bsz, seq, d = x.shape
    n_tok = bsz * seq
    assert w_ada.shape[0] == DEPTH == 1
    row = lambda v: v.reshape(1, -1)

    mod3 = _ada(c, w_ada[0], b_ada[0]).reshape(bsz, 6, d)

    qk, va, cq, ckv, kr, ga, gb = _inproj(x, mod3, jnp.swapaxes(w_in, 1, 2))
    ya = _attn_a(qk, va, _bias_table(rel_bias[0], min(TQ_A, seq)))
    wk, wv = _prep_w_ukv(w_ukv[0])
    qm, km, vm = _mla_proj(cq, ckv, kr, _rope_tables(positions), row(rms_q[0]), row(rms_kv[0]),
                           _prep_w_uq(w_uq[0]), wk, wv)
    yb = _mla_attn(qm, km, vm)

    wr = jnp.pad(w_router[0], ((0, 0), (0, LANES - N_EXPERTS)))
    br = jnp.pad(b_router[0], (0, LANES - N_EXPERTS), constant_values=NEG_INF)
    x1, h2, code, wts, cnt = _merge(x, ya, yb, ga, gb, mod3,
                                    w_branch_a[0].astype(BF16), w_branch_b[0].astype(BF16),
                                    w_out[0].astype(BF16), row(ln1_g[0]), row(ln1_b[0]), wr, row(br))
    wts = wts.reshape(n_tok, LANES)

    tm = TM_MOE
    n_tiles = -(-(n_tok * TOP_K) // tm) + N_EXPERTS
    assert n_tiles <= MAX_TILES_PAD
    pos, tmap, meta = _route(code.reshape(n_tok, LANES), cnt, tm)
    pos_flat = pos[:, :TOP_K].reshape(-1)
    fill = meta[:3].reshape(-1)
    nsub = d // LANES
    xs = _dispatch(pos_flat, fill, h2.reshape(n_tok * nsub, LANES), n_tiles * tm, nsub, tm)
    y = _moe(tmap[:n_tiles, 0], meta[0, :1], xs, w_gate[0], b_gate[0], w_up[0], b_up[0],
             w_down[0], b_down[0], tm)
    out = _combine(pos_flat, wts, x1.reshape(n_tok, d), mod3, row(ln2_g[0]), row(ln2_b[0]), y, seq)
    return out.reshape(bsz, seq, d)
```
